```python
import jax, jax.numpy as jnp
from jax import lax
import numpy as np

D_MODEL = 1024
BATCH = 4
SEQ = 4096
DEPTH = 1

CHUNK = 64
Q_BLOCK = 128
N_HEADS = 8
Q_LORA = 256
KV_LORA = 128
QK_NOPE = 64
QK_ROPE = 32
V_HEAD = 64
QK_HEAD = QK_NOPE + QK_ROPE
ROPE_THETA = 10000.0
CONV_WIDTH = 512
CONV_K = 3
N_EXPERTS = 32
TOP_K = 4
D_EXPERT = 1024
SWIGLU_LIMIT = 7.0
SWIGLU_ALPHA = 1.702
MOE_BLOCK = 256
RMS_EPS = 1e-6
IN_SPLITS = (Q_LORA, KV_LORA, QK_ROPE, CONV_WIDTH, CONV_WIDTH, CONV_WIDTH, D_MODEL, D_MODEL)
IN_WIDTH = Q_LORA + KV_LORA + QK_ROPE + 3 * CONV_WIDTH + 2 * D_MODEL

kernel_name = 'hybrid_mla_shortconv_moe_adaln_block'


def rms_norm(x, g):
    xf = x.astype(jnp.float32)
    xf = xf * lax.rsqrt(jnp.mean(xf * xf, axis=-1, keepdims=True) + RMS_EPS)
    return xf.astype(x.dtype) * g


def rope_tables(positions, dtype):
    inv_freq = 1.0 / (ROPE_THETA ** (jnp.arange(0, QK_ROPE, 2, dtype=jnp.float32) / QK_ROPE))
    ang = positions.astype(jnp.float32)[..., None] * inv_freq
    return jnp.cos(ang).astype(dtype), jnp.sin(ang).astype(dtype)


def apply_rope(x, cos, sin):
    half = x.shape[-1] // 2
    x1, x2 = x[..., :half], x[..., half:]
    return jnp.concatenate([x1 * cos - x2 * sin, x1 * sin + x2 * cos], axis=-1)


def block_causal_attention(q, k, v):
    B, S, H, Dh = q.shape
    nqb = S // Q_BLOCK
    key_chunk = jnp.arange(S) // CHUNK
    scale = QK_HEAD ** -0.5
    qb = q.reshape(B, nqb, Q_BLOCK, H, Dh).transpose(1, 0, 2, 3, 4)

    def one_block(args):
        q_blk, i = args
        q_chunk = (i * Q_BLOCK + jnp.arange(Q_BLOCK)) // CHUNK
        allowed = key_chunk[None, :] <= q_chunk[:, None]
        s = jnp.einsum('bqhd,bkhd->bhqk', q_blk, k, preferred_element_type=jnp.float32) * scale
        s = jnp.where(allowed[None, None], s, -jnp.inf)
        p = jax.nn.softmax(s, axis=-1).astype(v.dtype)
        return jnp.einsum('bhqk,bkhd->bqhd', p, v)

    out = lax.map(one_block, (qb, jnp.arange(nqb)))
    return out.transpose(1, 0, 2, 3, 4).reshape(B, S, H, v.shape[-1])


def causal_depthwise_conv(z, w):
    C = z.shape[-1]
    return lax.conv_general_dilated(
        z, w[:, None, :].astype(z.dtype), window_strides=(1,), padding=((CONV_K - 1, 0),),
        dimension_numbers=('NWC', 'WIO', 'NWC'), feature_group_count=C)


def hybrid_mixer(h, cos, sin, w_in, q_norm_g, w_uq, kv_norm_g, w_ukv, w_up_attn, conv_w, w_up_conv, w_o):
    B, S, _ = h.shape
    proj = h @ w_in
    split_at = np.cumsum(IN_SPLITS)[:-1].tolist()
    q_lat, kv_lat, k_pe, u, c_gate, b_gate, g_attn, g_conv = jnp.split(proj, split_at, axis=-1)
    q = (rms_norm(q_lat, q_norm_g) @ w_uq).reshape(B, S, N_HEADS, QK_HEAD)
    q = jnp.concatenate([q[..., :QK_NOPE], apply_rope(q[..., QK_NOPE:], cos[:, :, None], sin[:, :, None])], axis=-1)
    kv = (rms_norm(kv_lat, kv_norm_g) @ w_ukv).reshape(B, S, N_HEADS, QK_NOPE + V_HEAD)
    k_nope, v = kv[..., :QK_NOPE], kv[..., QK_NOPE:]
    k_pe = apply_rope(k_pe, cos, sin)[:, :, None, :]
    k = jnp.concatenate([k_nope, jnp.broadcast_to(k_pe, (B, S, N_HEADS, QK_ROPE))], axis=-1)
    attn = block_causal_attention(q, k, v).reshape(B, S, N_HEADS * V_HEAD)
    a_branch = attn @ w_up_attn
    z = causal_depthwise_conv(c_gate * u, conv_w)
    c_branch = (b_gate * z) @ w_up_conv
    merged = jax.nn.sigmoid(g_attn) * a_branch + jax.nn.sigmoid(g_conv) * c_branch
    return merged @ w_o


def moe_ffn(h, router_w, router_b, w_gu, b_gu, w_down, b_down):
    T, D = h.shape
    logits = (h @ router_w + router_b).astype(jnp.float32)
    top_val, top_idx = lax.top_k(logits, TOP_K)
    gate_w = jax.nn.softmax(top_val, axis=-1)
    flat_e = top_idx.reshape(-1)
    flat_tok = jnp.arange(T * TOP_K, dtype=jnp.int32) // TOP_K
    flat_w = gate_w.reshape(-1)
    order = jnp.argsort(flat_e)
    sorted_e = flat_e[order]
    counts = jnp.bincount(flat_e, length=N_EXPERTS)
    padded = (counts + MOE_BLOCK - 1) // MOE_BLOCK * MOE_BLOCK
    pad_end = jnp.cumsum(padded)
    pad_start = pad_end - padded
    grp_start = jnp.cumsum(counts) - counts
    rank = jnp.arange(T * TOP_K) - grp_start[sorted_e]
    dest = pad_start[sorted_e] + rank
    n_rows = T * TOP_K + N_EXPERTS * MOE_BLOCK
    n_blocks = n_rows // MOE_BLOCK
    row_tok = jnp.zeros((n_rows,), jnp.int32).at[dest].set(flat_tok[order])
    row_w = jnp.zeros((n_rows,), jnp.float32).at[dest].set(flat_w[order])
    block_start = jnp.arange(n_blocks) * MOE_BLOCK
    block_expert = jnp.minimum(jnp.searchsorted(pad_end, block_start, side='right'), N_EXPERTS - 1)
    xs = h[row_tok].reshape(n_blocks, MOE_BLOCK, D)

    def expert_block(args):
        xb, e = args
        gu = xb @ w_gu[e] + b_gu[e]
        gate, up = gu[:, :D_EXPERT], gu[:, D_EXPERT:]
        gate = jnp.minimum(gate, SWIGLU_LIMIT)
        up = jnp.clip(up, -SWIGLU_LIMIT, SWIGLU_LIMIT)
        act = (up + 1.0) * (gate * jax.nn.sigmoid(gate * SWIGLU_ALPHA))
        return act @ w_down[e] + b_down[e]

    ys = lax.map(expert_block, (xs, block_expert)).reshape(n_rows, D)
    ys = ys * row_w[:, None].astype(h.dtype)
    return jax.ops.segment_sum(ys, row_tok, num_segments=T)


def setup_inputs(seed: int = 0) -> dict:
    key = jax.random.key(seed)
    ks = jax.random.split(key, 24)
    f32 = jnp.float32
    L, D = DEPTH, D_MODEL

    def nrm(k, shape, scale):
        return jax.random.normal(k, shape, f32) * scale

    x = jax.random.normal(ks[0], (BATCH, SEQ, D), f32)
    c = jax.random.normal(ks[1], (BATCH, D), f32)
    offsets = jax.random.randint(ks[2], (BATCH, 1), 0, 4096, dtype=jnp.int32)
    positions = offsets + jnp.arange(SEQ, dtype=jnp.int32)[None, :]
    return {
        'x': x,
        'c': c,
        'positions': positions,
        'w_ada': nrm(ks[3], (L, D, 6 * D), 0.5 * D ** -0.5),
        'b_ada': nrm(ks[4], (L, 6 * D), 0.01),
        'norm_mix_g': 1.0 + nrm(ks[5], (L, D), 0.05),
        'w_in': nrm(ks[6], (L, D, IN_WIDTH), D ** -0.5),
        'q_norm_g': 1.0 + nrm(ks[7], (L, Q_LORA), 0.05),
        'w_uq': nrm(ks[8], (L, Q_LORA, N_HEADS * QK_HEAD), Q_LORA ** -0.5),
        'kv_norm_g': 1.0 + nrm(ks[9], (L, KV_LORA), 0.05),
        'w_ukv': nrm(ks[10], (L, KV_LORA, N_HEADS * (QK_NOPE + V_HEAD)), KV_LORA ** -0.5),
        'w_up_attn': nrm(ks[11], (L, N_HEADS * V_HEAD, D), (N_HEADS * V_HEAD) ** -0.5),
        'conv_w': nrm(ks[12], (L, CONV_K, CONV_WIDTH), CONV_K ** -0.5),
        'w_up_conv': nrm(ks[13], (L, CONV_WIDTH, D), CONV_WIDTH ** -0.5),
        'w_o': nrm(ks[14], (L, D, D), D ** -0.5),
        'norm_ffn_g': 1.0 + nrm(ks[15], (L, D), 0.05),
        'router_w': nrm(ks[16], (L, D, N_EXPERTS), D ** -0.5),
        'router_b': nrm(ks[17], (L, N_EXPERTS), 0.01),
        'w_gu': nrm(ks[18], (L, N_EXPERTS, D, 2 * D_EXPERT), D ** -0.5),
        'b_gu': nrm(ks[19], (L, N_EXPERTS, 2 * D_EXPERT), 0.01),
        'w_down': nrm(ks[20], (L, N_EXPERTS, D_EXPERT, D), D_EXPERT ** -0.5),
        'b_down': nrm(ks[21], (L, N_EXPERTS, D), 0.01),
        'norm_final_g': 1.0 + nrm(ks[22], (D,), 0.05),
    }


def reference(x, c, positions, w_ada, b_ada, norm_mix_g, w_in, q_norm_g, w_uq, kv_norm_g, w_ukv,
              w_up_attn, conv_w, w_up_conv, w_o, norm_ffn_g, router_w, router_b, w_gu, b_gu,
              w_down, b_down, norm_final_g):
    B, S, D = x.shape
    cos, sin = rope_tables(positions, x.dtype)
    c_act = jax.nn.silu(c)
    for l in range(DEPTH):
        ada = (c_act @ w_ada[l] + b_ada[l])[:, None, :]
        sh_m, sc_m, g_m, sh_f, sc_f, g_f = jnp.split(ada, 6, axis=-1)
        h = rms_norm(x, norm_mix_g[l]) * (1.0 + sc_m) + sh_m
        mix = hybrid_mixer(h, cos, sin, w_in[l], q_norm_g[l], w_uq[l], kv_norm_g[l], w_ukv[l],
                           w_up_attn[l], conv_w[l], w_up_conv[l], w_o[l])
        x = x + g_m * mix
        h = rms_norm(x, norm_ffn_g[l]) * (1.0 + sc_f) + sh_f
        ffn = moe_ffn(h.reshape(B * S, D), router_w[l], router_b[l], w_gu[l], b_gu[l], w_down[l], b_down[l])
        x = x + g_f * ffn.reshape(B, S, D)
    return rms_norm(x, norm_final_g)
```

```python
import functools
import math

import jax
import jax.numpy as jnp
from jax import lax
from jax.experimental import pallas as pl
from jax.experimental.pallas import tpu as pltpu

D_MODEL = 1024
CHUNK = 64
N_HEADS = 8
Q_LORA = 256
KV_LORA = 128
QK_NOPE = 64
QK_ROPE = 32
V_HEAD = 64
QK_HEAD = QK_NOPE + QK_ROPE
ROPE_THETA = 10000.0
CONV_WIDTH = 512
CONV_K = 3
N_EXPERTS = 32
TOP_K = 4
D_EXPERT = 1024
SWIGLU_LIMIT = 7.0
SWIGLU_ALPHA = 1.702
MOE_BLOCK = 256
RMS_EPS = 1e-6

LANES = 128
HEAD_PAD = 128
NEG_BIG = -1e30
VMEM_LIMIT = 56 * 1024 * 1024

F32 = jnp.float32
BF16 = jnp.bfloat16

ROW_TILE = 512
ATT_BLOCK = 256


def _rms(x, g):
    ms = jnp.mean(x * x, axis=-1, keepdims=True)
    return x * lax.rsqrt(ms + RMS_EPS) * g


def _dot(a, b):
    return jnp.dot(a, b, preferred_element_type=F32)


def _ada_kernel(c_ref, w_ref, b_ref, o_ref):
    c = c_ref[...]
    ca = (c * jax.nn.sigmoid(c)).astype(BF16)
    o_ref[...] = _dot(ca, w_ref[...].astype(BF16)) + b_ref[...]


def _ada(c_pad, w_ada, b_ada):
    n = w_ada.shape[1]
    tn = 1024
    return pl.pallas_call(
        _ada_kernel,
        out_shape=jax.ShapeDtypeStruct((c_pad.shape[0], n), F32),
        grid=(n // tn,),
        in_specs=[
            pl.BlockSpec(c_pad.shape, lambda j: (0, 0)),
            pl.BlockSpec((D_MODEL, tn), lambda j: (0, j)),
            pl.BlockSpec((1, tn), lambda j: (0, j)),
        ],
        out_specs=pl.BlockSpec((c_pad.shape[0], tn), lambda j: (0, j)),
        compiler_params=pltpu.CompilerParams(
            dimension_semantics=("arbitrary",), vmem_limit_bytes=VMEM_LIMIT),
        name="ada",
    )(c_pad, w_ada, b_ada)


_C_QLAT = 0
_C_KVLAT = _C_QLAT + Q_LORA
_C_KPE_A = _C_KVLAT + KV_LORA
_C_KPE_B = _C_KPE_A + HEAD_PAD
_C_U = _C_KPE_B + HEAD_PAD
_C_C = _C_U + CONV_WIDTH
_C_B = _C_C + CONV_WIDTH
_C_GA = _C_B + CONV_WIDTH
_C_GC = _C_GA + D_MODEL
_C_END = _C_GC + D_MODEL


def _pre_kernel(tiles_per_seq, x_ref, mod_ref, g_ref, win_ref, qg_ref, wq_ref, wqs_ref,
                kvg_ref, wk_ref, wv_ref, cos_ref, sin_ref, cw_ref, wuc_ref,
                q_ref, k_ref, v_ref, sga_ref, gc_ref, carry_ref):
    i = pl.program_id(0)
    tm = x_ref.shape[0]
    mod = mod_ref[...]
    h = _rms(x_ref[...], g_ref[...]) * (1.0 + mod[1:2]) + mod[0:1]
    hb = h.astype(BF16)

    cosf = cos_ref[...]
    sinf = sin_ref[...]
    cos8 = jnp.concatenate([cosf] * N_HEADS, axis=-1)
    sin8 = jnp.concatenate([sinf] * N_HEADS, axis=-1)

    small = _dot(hb, win_ref[:, _C_QLAT:_C_U])
    q_lat = small[:, _C_QLAT:_C_KVLAT]
    kv_lat = small[:, _C_KVLAT:_C_KPE_A]
    kpe = small[:, _C_KPE_A:_C_KPE_B] * cosf + small[:, _C_KPE_B:_C_U] * sinf
    qn = _rms(q_lat, qg_ref[...]).astype(BF16)
    q = _dot(qn, wq_ref[...]) * cos8 + _dot(qn, wqs_ref[...]) * sin8
    q_ref[...] = q.astype(BF16)
    kvn = _rms(kv_lat, kvg_ref[...]).astype(BF16)
    k = _dot(kvn, wk_ref[...]) + jnp.concatenate([kpe] * N_HEADS, axis=-1)
    k_ref[...] = k.astype(BF16)
    v_ref[...] = _dot(kvn, wv_ref[...]).astype(BF16)

    ucb = _dot(hb, win_ref[:, _C_U:_C_GA])
    cu = ucb[:, 0:CONV_WIDTH] * ucb[:, CONV_WIDTH:2 * CONV_WIDTH]
    b_gate = ucb[:, 2 * CONV_WIDTH:3 * CONV_WIDTH]

    @pl.when(i % tiles_per_seq == 0)
    def _():
        carry_ref[...] = jnp.zeros_like(carry_ref)

    prev = carry_ref[...]
    row = lax.broadcasted_iota(jnp.int32, cu.shape, 0)
    cu1 = jnp.where(row == 0, prev[7:8], pltpu.roll(cu, 1, 0))
    cu2 = jnp.where(row == 0, prev[6:7], jnp.where(row == 1, prev[7:8], pltpu.roll(cu, 2, 0)))
    cw = cw_ref[...]
    z = cw[2:3] * cu + cw[1:2] * cu1 + cw[0:1] * cu2
    carry_ref[...] = cu[tm - 8:tm]
    c_branch = _dot((b_gate * z).astype(BF16), wuc_ref[...])

    gates = _dot(hb, win_ref[:, _C_GA:_C_END])
    sga_ref[...] = jax.nn.sigmoid(gates[:, 0:D_MODEL]).astype(BF16)
    gc_ref[...] = (jax.nn.sigmoid(gates[:, D_MODEL:]) * c_branch).astype(BF16)


def _pre(x2, mod, norm_g, win2, q_norm_g, wq2, wq2s, kv_norm_g, wk2, wv, cosf, sinf, conv_w,
         w_up_conv, seq):
    t = x2.shape[0]
    tm = ROW_TILE
    tiles_per_seq = seq // tm
    full = lambda a: pl.BlockSpec(a.shape, lambda i: (0,) * a.ndim)
    rows = lambda w: pl.BlockSpec((tm, w), lambda i: (i, 0))
    outs = [jax.ShapeDtypeStruct((t, N_HEADS * HEAD_PAD), BF16),
            jax.ShapeDtypeStruct((t, N_HEADS * HEAD_PAD), BF16),
            jax.ShapeDtypeStruct((t, N_HEADS * V_HEAD), BF16),
            jax.ShapeDtypeStruct((t, D_MODEL), BF16),
            jax.ShapeDtypeStruct((t, D_MODEL), BF16)]
    return pl.pallas_call(
        functools.partial(_pre_kernel, tiles_per_seq),
        out_shape=outs,
        grid=(t // tm,),
        in_specs=[
            rows(D_MODEL),
            pl.BlockSpec((None, 8, D_MODEL), lambda i: (i // tiles_per_seq, 0, 0)),
            full(norm_g), full(win2), full(q_norm_g), full(wq2), full(wq2s),
            full(kv_norm_g), full(wk2), full(wv),
            rows(HEAD_PAD), rows(HEAD_PAD), full(conv_w), full(w_up_conv),
        ],
        out_specs=[rows(N_HEADS * HEAD_PAD), rows(N_HEADS * HEAD_PAD), rows(N_HEADS * V_HEAD),
                   rows(D_MODEL), rows(D_MODEL)],
        scratch_shapes=[pltpu.VMEM((8, CONV_WIDTH), F32)],
        compiler_params=pltpu.CompilerParams(
            dimension_semantics=("arbitrary",), vmem_limit_bytes=VMEM_LIMIT),
        name="pre_mixer",
    )(x2, mod, norm_g, win2, q_norm_g, wq2, wq2s, kv_norm_g, wk2, wv, cosf, sinf, conv_w,
      w_up_conv)


def _attn_kernel(q_ref, k_ref, v_ref, o_ref, m_ref, l_ref, acc_ref):
    i = pl.program_id(1)
    tq = q_ref.shape[0]
    tk = tq
    c_exp = (QK_HEAD ** -0.5) * math.log2(math.e)

    m_ref[...] = jnp.full_like(m_ref, NEG_BIG)
    l_ref[...] = jnp.zeros_like(l_ref)
    acc_ref[...] = jnp.zeros_like(acc_ref)

    def step(j, masked):
        k0 = pl.multiple_of(j * tk, tk)
        if masked:
            rq = lax.broadcasted_iota(jnp.int32, (tq, tk), 0) // CHUNK
            ck = lax.broadcasted_iota(jnp.int32, (tq, tk), 1) // CHUNK
            allowed = ck <= rq
        for hd in range(N_HEADS):
            qh = q_ref[:, hd * HEAD_PAD:(hd + 1) * HEAD_PAD]
            kh = k_ref[pl.ds(k0, tk), hd * HEAD_PAD:(hd + 1) * HEAD_PAD]
            s = lax.dot_general(qh, kh, (((1,), (1,)), ((), ())), preferred_element_type=F32)
            if masked:
                s = jnp.where(allowed, s, NEG_BIG)
            m_old = m_ref[hd]
            m_new = jnp.maximum(m_old, jnp.max(s, axis=-1, keepdims=True))
            alpha = jnp.exp2((m_old - m_new) * c_exp)
            p = jnp.exp2((s - m_new) * c_exp)
            l_ref[hd] = alpha * l_ref[hd] + jnp.sum(p, axis=-1, keepdims=True)
            vp = v_ref[pl.ds(k0, tk), (hd // 2) * LANES:(hd // 2 + 1) * LANES]
            acc_ref[hd] = alpha * acc_ref[hd] + _dot(p.astype(BF16), vp)
            m_ref[hd] = m_new

    def body(j, carry):
        step(j, False)
        return carry

    lax.fori_loop(0, i, body, 0)
    step(i, True)

    lane = lax.broadcasted_iota(jnp.int32, (tq, LANES), 1)
    for hp in range(N_HEADS // 2):
        lo = acc_ref[2 * hp] / l_ref[2 * hp]
        hi = acc_ref[2 * hp + 1] / l_ref[2 * hp + 1]
        o_ref[:, hp * LANES:(hp + 1) * LANES] = jnp.where(lane < V_HEAD, lo, hi).astype(BF16)


def _attention(q, k, v, batch, seq):
    tq = ATT_BLOCK
    nq = seq // tq
    return pl.pallas_call(
        _attn_kernel,
        out_shape=jax.ShapeDtypeStruct((batch * seq, N_HEADS * V_HEAD), BF16),
        grid=(batch, nq),
        in_specs=[
            pl.BlockSpec((tq, N_HEADS * HEAD_PAD), lambda b, i: (b * nq + i, 0)),
            pl.BlockSpec((seq, N_HEADS * HEAD_PAD), lambda b, i: (b, 0)),
            pl.BlockSpec((seq, N_HEADS * V_HEAD), lambda b, i: (b, 0)),
        ],
        out_specs=pl.BlockSpec((tq, N_HEADS * V_HEAD), lambda b, i: (b * nq + i, 0)),
        scratch_shapes=[pltpu.VMEM((N_HEADS, tq, 1), F32),
                        pltpu.VMEM((N_HEADS, tq, 1), F32),
                        pltpu.VMEM((N_HEADS, tq, LANES), F32)],
        compiler_params=pltpu.CompilerParams(
            dimension_semantics=("arbitrary", "arbitrary"), vmem_limit_bytes=VMEM_LIMIT),
        name="attention",
    )(q, k, v)


def _post_kernel(attn_ref, sga_ref, gc_ref, x_ref, mod_ref, wua_ref, wo_ref, g_ref, rw_ref,
                 rb_ref, x1_ref, h2_ref, idx_ref, gate_ref):
    mod = mod_ref[...]
    a_branch = _dot(attn_ref[...], wua_ref[...])
    merged = sga_ref[...].astype(F32) * a_branch + gc_ref[...].astype(F32)
    mix = _dot(merged.astype(BF16), wo_ref[...])
    x1 = x_ref[...] + mod[2:3] * mix
    x1_ref[...] = x1
    h2 = _rms(x1, g_ref[...]) * (1.0 + mod[4:5]) + mod[3:4]
    h2_ref[...] = h2.astype(BF16)

    logits = jnp.dot(h2, rw_ref[...], preferred_element_type=F32,
                     precision=lax.Precision.HIGHEST) + rb_ref[...]
    lane = lax.broadcasted_iota(jnp.int32, logits.shape, 1)
    work = logits
    vals, idxs = [], []
    for _ in range(TOP_K):
        mk = jnp.max(work, axis=-1, keepdims=True)
        ik = jnp.min(jnp.where(work == mk, lane, LANES), axis=-1, keepdims=True)
        vals.append(mk)
        idxs.append(ik)
        work = jnp.where(lane == ik, -jnp.inf, work)
    es = [jnp.exp(vk - vals[0]) for vk in vals]
    denom = es[0] + es[1] + es[2] + es[3]
    idx_out = jnp.zeros(logits.shape, jnp.int32)
    gate_out = jnp.zeros(logits.shape, F32)
    for kk in range(TOP_K):
        idx_out = jnp.where(lane == kk, idxs[kk], idx_out)
        gate_out = jnp.where(lane == kk, es[kk] / denom, gate_out)
    idx_ref[...] = idx_out
    gate_ref[...] = gate_out


def _post(attn, sga, gc, x2, mod, wua, wo, norm_g, rw_pad, rb_pad, seq):
    t = x2.shape[0]
    tm = ROW_TILE
    tiles_per_seq = seq // tm
    full = lambda a: pl.BlockSpec(a.shape, lambda i: (0,) * a.ndim)
    rows = lambda w: pl.BlockSpec((tm, w), lambda i: (i, 0))
    outs = [jax.ShapeDtypeStruct((t, D_MODEL), F32),
            jax.ShapeDtypeStruct((t, D_MODEL), BF16),
            jax.ShapeDtypeStruct((t, LANES), jnp.int32),
            jax.ShapeDtypeStruct((t, LANES), F32)]
    return pl.pallas_call(
        _post_kernel,
        out_shape=outs,
        grid=(t // tm,),
        in_specs=[
            rows(N_HEADS * V_HEAD), rows(D_MODEL), rows(D_MODEL), rows(D_MODEL),
            pl.BlockSpec((None, 8, D_MODEL), lambda i: (i // tiles_per_seq, 0, 0)),
            full(wua), full(wo), full(norm_g), full(rw_pad), full(rb_pad),
        ],
        out_specs=[rows(D_MODEL), rows(D_MODEL), rows(LANES), rows(LANES)],
        compiler_params=pltpu.CompilerParams(
            dimension_semantics=("arbitrary",), vmem_limit_bytes=VMEM_LIMIT),
        name="post_mixer",
    )(attn, sga, gc, x2, mod, wua, wo, norm_g, rw_pad, rb_pad)


def _moe_kernel(be_ref, nu_ref, xs_ref, wgu_ref, bgu_ref, wd_ref, bd_ref, o_ref, wgu_bf, wd_bf):
    b = pl.program_id(0)
    used = b < nu_ref[0]
    e = be_ref[b]
    e_prev = be_ref[jnp.maximum(b - 1, 0)]
    fresh = jnp.logical_or(b == 0, e != e_prev)

    @pl.when(jnp.logical_and(used, fresh))
    def _():
        wgu_bf[...] = wgu_ref[...].astype(BF16)
        wd_bf[...] = wd_ref[...].astype(BF16)

    @pl.when(used)
    def _():
        gu = _dot(xs_ref[...], wgu_bf[...]) + bgu_ref[...]
        gate = jnp.minimum(gu[:, :D_EXPERT], SWIGLU_LIMIT)
        up = jnp.clip(gu[:, D_EXPERT:], -SWIGLU_LIMIT, SWIGLU_LIMIT)
        act = (up + 1.0) * (gate * jax.nn.sigmoid(gate * SWIGLU_ALPHA))
        o_ref[...] = _dot(act.astype(BF16), wd_bf[...]) + bd_ref[...]

    @pl.when(jnp.logical_not(used))
    def _():
        o_ref[...] = jnp.zeros_like(o_ref)


def _moe(block_expert, n_used, xs, w_gu, b_gu, w_down, b_down):
    n_rows = xs.shape[0]
    n_blocks = n_rows // MOE_BLOCK
    grid_spec = pltpu.PrefetchScalarGridSpec(
        num_scalar_prefetch=2,
        grid=(n_blocks,),
        in_specs=[
            pl.BlockSpec((MOE_BLOCK, D_MODEL), lambda b, be, nu: (b, 0)),
            pl.BlockSpec((None, D_MODEL, 2 * D_EXPERT), lambda b, be, nu: (be[b], 0, 0)),
            pl.BlockSpec((None, 1, 2 * D_EXPERT), lambda b, be, nu: (be[b], 0, 0)),
            pl.BlockSpec((None, D_EXPERT, D_MODEL), lambda b, be, nu: (be[b], 0, 0)),
            pl.BlockSpec((None, 1, D_MODEL), lambda b, be, nu: (be[b], 0, 0)),
        ],
        out_specs=pl.BlockSpec((MOE_BLOCK, D_MODEL), lambda b, be, nu: (b, 0)),
        scratch_shapes=[pltpu.VMEM((D_MODEL, 2 * D_EXPERT), BF16),
                        pltpu.VMEM((D_EXPERT, D_MODEL), BF16)],
    )
    return pl.pallas_call(
        _moe_kernel,
        out_shape=jax.ShapeDtypeStruct((n_rows, D_MODEL), F32),
        grid_spec=grid_spec,
        compiler_params=pltpu.CompilerParams(
            dimension_semantics=("arbitrary",), vmem_limit_bytes=VMEM_LIMIT),
        name="moe_experts",
    )(block_expert, n_used, xs, w_gu, b_gu, w_down, b_down)


def _final_kernel(last_layer, x1_ref, y_ref, gate_ref, mod_ref, g_ref, o_ref):
    mod = mod_ref[...]
    gate = gate_ref[...]
    ffn = gate[:, 0:1] * y_ref[:, 0:D_MODEL]
    for kk in range(1, TOP_K):
        ffn = ffn + gate[:, kk:kk + 1] * y_ref[:, kk * D_MODEL:(kk + 1) * D_MODEL]
    x = x1_ref[...] + mod[5:6] * ffn
    o_ref[...] = _rms(x, g_ref[...]) if last_layer else x


def _final(x1, y_tk, gate, mod, norm_g, seq, last_layer):
    t = x1.shape[0]
    tm = ROW_TILE
    tiles_per_seq = seq // tm
    rows = lambda w: pl.BlockSpec((tm, w), lambda i: (i, 0))
    return pl.pallas_call(
        functools.partial(_final_kernel, last_layer),
        out_shape=jax.ShapeDtypeStruct((t, D_MODEL), F32),
        grid=(t // tm,),
        in_specs=[
            rows(D_MODEL), rows(TOP_K * D_MODEL), rows(LANES),
            pl.BlockSpec((None, 8, D_MODEL), lambda i: (i // tiles_per_seq, 0, 0)),
            pl.BlockSpec(norm_g.shape, lambda i: (0, 0)),
        ],
        out_specs=rows(D_MODEL),
        compiler_params=pltpu.CompilerParams(
            dimension_semantics=("arbitrary",), vmem_limit_bytes=VMEM_LIMIT),
        name="combine_final",
    )(x1, y_tk, gate, mod, norm_g)


def _swap_halves(w):
    half = w.shape[-1] // 2
    return jnp.concatenate([w[..., half:], w[..., :half]], axis=-1)


def _prep_weights(w_in, w_uq, w_ukv):
    d = w_in.shape[0]
    splits = (Q_LORA, KV_LORA, QK_ROPE, CONV_WIDTH, CONV_WIDTH, CONV_WIDTH, D_MODEL, D_MODEL)
    offs = [0]
    for s in splits:
        offs.append(offs[-1] + s)
    part = lambda n: w_in[:, offs[n]:offs[n + 1]]
    z = lambda n: jnp.zeros((d, n), w_in.dtype)
    w_kpe = part(2)
    kpe_a = jnp.concatenate([z(QK_NOPE), w_kpe, z(HEAD_PAD - QK_HEAD)], axis=1)
    kpe_b = jnp.concatenate([z(QK_NOPE), _swap_halves(w_kpe), z(HEAD_PAD - QK_HEAD)], axis=1)
    win2 = jnp.concatenate([part(0), part(1), kpe_a, kpe_b, part(3), part(4), part(5), part(6),
                            part(7)], axis=1).astype(BF16)

    wq = w_uq.reshape(Q_LORA, N_HEADS, QK_HEAD)
    zq = lambda n: jnp.zeros((Q_LORA, N_HEADS, n), w_uq.dtype)
    wq2 = jnp.concatenate([wq, zq(HEAD_PAD - QK_HEAD)], axis=-1)
    wq2s = jnp.concatenate([zq(QK_NOPE), _swap_halves(wq[..., QK_NOPE:]), zq(HEAD_PAD - QK_HEAD)],
                           axis=-1)
    wq2 = wq2.reshape(Q_LORA, N_HEADS * HEAD_PAD).astype(BF16)
    wq2s = wq2s.reshape(Q_LORA, N_HEADS * HEAD_PAD).astype(BF16)

    wkv = w_ukv.reshape(KV_LORA, N_HEADS, QK_NOPE + V_HEAD)
    wk2 = jnp.concatenate([wkv[..., :QK_NOPE],
                           jnp.zeros((KV_LORA, N_HEADS, HEAD_PAD - QK_NOPE), w_ukv.dtype)], axis=-1)
    wk2 = wk2.reshape(KV_LORA, N_HEADS * HEAD_PAD).astype(BF16)
    wv = wkv[..., QK_NOPE:].reshape(KV_LORA, N_HEADS * V_HEAD).astype(BF16)
    return win2, wq2, wq2s, wk2, wv


def _rope_tables(positions):
    inv_freq = 1.0 / (ROPE_THETA ** (jnp.arange(0, QK_ROPE, 2, dtype=F32) / QK_ROPE))
    ang = positions.astype(F32).reshape(-1, 1) * inv_freq
    cos, sin = jnp.cos(ang), jnp.sin(ang)
    t = ang.shape[0]
    ones = jnp.ones((t, QK_NOPE), F32)
    zpad = jnp.zeros((t, HEAD_PAD - QK_HEAD), F32)
    cosf = jnp.concatenate([ones, cos, cos, zpad], axis=1)
    sinf = jnp.concatenate([jnp.zeros((t, QK_NOPE), F32), -sin, sin, zpad], axis=1)
    return cosf, sinf


def _route(top_idx, n_tokens):
    onehot = (top_idx[:, :, None] == jnp.arange(N_EXPERTS, dtype=jnp.int32)).astype(jnp.int32)
    per_tok = onehot.sum(axis=1)
    csum = jnp.cumsum(per_tok, axis=0)
    counts = csum[-1]
    excl = csum - per_tok
    rank = jnp.take_along_axis(excl, top_idx, axis=1)
    padded = (counts + MOE_BLOCK - 1) // MOE_BLOCK * MOE_BLOCK
    pad_end = jnp.cumsum(padded)
    pad_start = pad_end - padded
    dest = pad_start[top_idx] + rank
    n_rows = n_tokens * TOP_K + N_EXPERTS * MOE_BLOCK
    n_blocks = n_rows // MOE_BLOCK
    block_start = jnp.arange(n_blocks, dtype=jnp.int32) * MOE_BLOCK
    block_expert = jnp.minimum(
        (pad_end[None, :] <= block_start[:, None]).astype(jnp.int32).sum(axis=1), N_EXPERTS - 1)
    n_used = (pad_end[-1] // MOE_BLOCK).astype(jnp.int32).reshape(1)
    return dest.astype(jnp.int32), block_expert, n_used, n_rows


def kernel(x, c, positions, w_ada, b_ada, norm_mix_g, w_in, q_norm_g, w_uq, kv_norm_g, w_ukv,
           w_up_attn, conv_w, w_up_conv, w_o, norm_ffn_g, router_w, router_b, w_gu, b_gu,
           w_down, b_down, norm_final_g):
    batch, seq, d = x.shape
    t = batch * seq
    depth = w_ada.shape[0]
    x2 = x.reshape(t, d)
    cosf, sinf = _rope_tables(positions)
    c_pad = jnp.zeros((8, d), F32).at[:batch].set(c)

    for l in range(depth):
        ada = _ada(c_pad, w_ada[l], b_ada[l].reshape(1, -1))
        mod = ada[:batch].reshape(batch, 6, d)
        mod = jnp.concatenate([mod, jnp.zeros((batch, 2, d), F32)], axis=1)

        win2, wq2, wq2s, wk2, wv = _prep_weights(w_in[l], w_uq[l], w_ukv[l])
        q, k, v, sga, gc = _pre(x2, mod, norm_mix_g[l].reshape(1, d), win2,
                                q_norm_g[l].reshape(1, -1), wq2, wq2s,
                                kv_norm_g[l].reshape(1, -1), wk2, wv, cosf, sinf, conv_w[l],
                                w_up_conv[l].astype(BF16), seq)
        attn = _attention(q, k, v, batch, seq)

        rw_pad = jnp.concatenate([router_w[l], jnp.zeros((d, LANES - N_EXPERTS), F32)], axis=1)
        rb_pad = jnp.concatenate([router_b[l], jnp.full((LANES - N_EXPERTS,), NEG_BIG, F32)])
        x1, h2, idx_pad, gate_pad = _post(attn, sga, gc, x2, mod, w_up_attn[l].astype(BF16),
                                          w_o[l].astype(BF16), norm_ffn_g[l].reshape(1, d),
                                          rw_pad, rb_pad.reshape(1, LANES), seq)

        top_idx = idx_pad[:, :TOP_K]
        dest, block_expert, n_used, n_rows = _route(top_idx, t)
        flat_tok = jnp.arange(t * TOP_K, dtype=jnp.int32) // TOP_K
        row_tok = jnp.zeros((n_rows,), jnp.int32).at[dest.reshape(-1)].set(flat_tok)
        xs = h2[row_tok]
        ys = _moe(block_expert, n_used, xs, w_gu[l], b_gu[l].reshape(N_EXPERTS, 1, -1),
                  w_down[l], b_down[l].reshape(N_EXPERTS, 1, -1))
        y_tk = ys[dest.reshape(-1)].reshape(t, TOP_K * d)
        x2 = _final(x1, y_tk, gate_pad, mod, norm_final_g.reshape(1, d), seq, l == depth - 1)

    return x2.reshape(batch, seq, d)
```

```python
import functools
import math

import jax
import jax.numpy as jnp
from jax import lax
from jax.experimental import pallas as pl
from jax.experimental.pallas import tpu as pltpu

D_MODEL = 1024
CHUNK = 64
N_HEADS = 8
Q_LORA = 256
KV_LORA = 128
QK_NOPE = 64
QK_ROPE = 32
V_HEAD = 64
QK_HEAD = QK_NOPE + QK_ROPE
ROPE_THETA = 10000.0
CONV_WIDTH = 512
CONV_K = 3
N_EXPERTS = 32
TOP_K = 4
D_EXPERT = 1024
SWIGLU_LIMIT = 7.0
SWIGLU_ALPHA = 1.702
MOE_BLOCK = 256
RMS_EPS = 1e-6

LANES = 128
HEAD_PAD = 128
NEG_BIG = -1e30
VMEM_LIMIT = 56 * 1024 * 1024

F32 = jnp.float32
BF16 = jnp.bfloat16

ROW_TILE = 512
ATT_BLOCK = 256


def _rms(x, g):
    ms = jnp.mean(x * x, axis=-1, keepdims=True)
    return x * lax.rsqrt(ms + RMS_EPS) * g


def _dot(a, b):
    return jnp.dot(a, b, preferred_element_type=F32)


def _ada_kernel(c_ref, w_ref, b_ref, o_ref):
    c = c_ref[...]
    ca = (c * jax.nn.sigmoid(c)).astype(BF16)
    o_ref[...] = _dot(ca, w_ref[...].astype(BF16)) + b_ref[...]


def _ada(c_pad, w_ada, b_ada):
    n = w_ada.shape[1]
    tn = 1024
    return pl.pallas_call(
        _ada_kernel,
        out_shape=jax.ShapeDtypeStruct((c_pad.shape[0], n), F32),
        grid=(n // tn,),
        in_specs=[
            pl.BlockSpec(c_pad.shape, lambda j: (0, 0)),
            pl.BlockSpec((D_MODEL, tn), lambda j: (0, j)),
            pl.BlockSpec((1, tn), lambda j: (0, j)),
        ],
        out_specs=pl.BlockSpec((c_pad.shape[0], tn), lambda j: (0, j)),
        compiler_params=pltpu.CompilerParams(
            dimension_semantics=("arbitrary",), vmem_limit_bytes=VMEM_LIMIT),
        name="ada",
    )(c_pad, w_ada, b_ada)


_C_QLAT = 0
_C_KVLAT = _C_QLAT + Q_LORA
_C_KPE_A = _C_KVLAT + KV_LORA
_C_KPE_B = _C_KPE_A + HEAD_PAD
_C_U = _C_KPE_B + HEAD_PAD
_C_C = _C_U + CONV_WIDTH
_C_B = _C_C + CONV_WIDTH
_C_GA = _C_B + CONV_WIDTH
_C_GC = _C_GA + D_MODEL
_C_END = _C_GC + D_MODEL


def _pre_kernel(tiles_per_seq, x_ref, mod_ref, g_ref, win_ref, qg_ref, wq_ref, wqs_ref,
                kvg_ref, wk_ref, wv_ref, cos_ref, sin_ref, cw_ref, wuc_ref,
                q_ref, k_ref, v_ref, sga_ref, gc_ref, carry_ref):
    i = pl.program_id(0)
    tm = x_ref.shape[0]
    mod = mod_ref[...]
    h = _rms(x_ref[...], g_ref[...]) * (1.0 + mod[1:2]) + mod[0:1]
    hb = h.astype(BF16)

    cosf = cos_ref[...]
    sinf = sin_ref[...]
    cos8 = jnp.concatenate([cosf] * N_HEADS, axis=-1)
    sin8 = jnp.concatenate([sinf] * N_HEADS, axis=-1)

    small = _dot(hb, win_ref[:, _C_QLAT:_C_U])
    q_lat = small[:, _C_QLAT:_C_KVLAT]
    kv_lat = small[:, _C_KVLAT:_C_KPE_A]
    kpe = small[:, _C_KPE_A:_C_KPE_B] * cosf + small[:, _C_KPE_B:_C_U] * sinf
    qn = _rms(q_lat, qg_ref[...]).astype(BF16)
    q = _dot(qn, wq_ref[...]) * cos8 + _dot(qn, wqs_ref[...]) * sin8
    q_ref[...] = q.astype(BF16)
    kvn = _rms(kv_lat, kvg_ref[...]).astype(BF16)
    k = _dot(kvn, wk_ref[...]) + jnp.concatenate([kpe] * N_HEADS, axis=-1)
    k_ref[...] = k.astype(BF16)
    lane = lax.broadcasted_iota(jnp.int32, (tm, N_HEADS * HEAD_PAD), 1)
    ones_col = jnp.where(lane % HEAD_PAD == V_HEAD, 1.0, 0.0)
    v_ref[...] = (_dot(kvn, wv_ref[...]) + ones_col).astype(BF16)

    ucb = _dot(hb, win_ref[:, _C_U:_C_GA])
    cu = ucb[:, 0:CONV_WIDTH] * ucb[:, CONV_WIDTH:2 * CONV_WIDTH]
    b_gate = ucb[:, 2 * CONV_WIDTH:3 * CONV_WIDTH]

    @pl.when(i % tiles_per_seq == 0)
    def _():
        carry_ref[...] = jnp.zeros_like(carry_ref)

    prev = carry_ref[...]
    row = lax.broadcasted_iota(jnp.int32, cu.shape, 0)
    cu1 = jnp.where(row == 0, prev[7:8], pltpu.roll(cu, 1, 0))
    cu2 = jnp.where(row == 0, prev[6:7], jnp.where(row == 1, prev[7:8], pltpu.roll(cu, 2, 0)))
    cw = cw_ref[...]
    z = cw[2:3] * cu + cw[1:2] * cu1 + cw[0:1] * cu2
    carry_ref[...] = cu[tm - 8:tm]
    c_branch = _dot((b_gate * z).astype(BF16), wuc_ref[...])

    gates = _dot(hb, win_ref[:, _C_GA:_C_END])
    sga_ref[...] = jax.nn.sigmoid(gates[:, 0:D_MODEL]).astype(BF16)
    gc_ref[...] = (jax.nn.sigmoid(gates[:, D_MODEL:]) * c_branch).astype(BF16)


def _pre(x2, mod, norm_g, win2, q_norm_g, wq2, wq2s, kv_norm_g, wk2, wv, cosf, sinf, conv_w,
         w_up_conv, seq):
    t = x2.shape[0]
    tm = ROW_TILE
    tiles_per_seq = seq // tm
    full = lambda a: pl.BlockSpec(a.shape, lambda i: (0,) * a.ndim)
    rows = lambda w: pl.BlockSpec((tm, w), lambda i: (i, 0))
    outs = [jax.ShapeDtypeStruct((t, N_HEADS * HEAD_PAD), BF16),
            jax.ShapeDtypeStruct((t, N_HEADS * HEAD_PAD), BF16),
            jax.ShapeDtypeStruct((t, N_HEADS * HEAD_PAD), BF16),
            jax.ShapeDtypeStruct((t, D_MODEL), BF16),
            jax.ShapeDtypeStruct((t, D_MODEL), BF16)]
    return pl.pallas_call(
        functools.partial(_pre_kernel, tiles_per_seq),
        out_shape=outs,
        grid=(t // tm,),
        in_specs=[
            rows(D_MODEL),
            pl.BlockSpec((None, 8, D_MODEL), lambda i: (i // tiles_per_seq, 0, 0)),
            full(norm_g), full(win2), full(q_norm_g), full(wq2), full(wq2s),
            full(kv_norm_g), full(wk2), full(wv),
            rows(HEAD_PAD), rows(HEAD_PAD), full(conv_w), full(w_up_conv),
        ],
        out_specs=[rows(N_HEADS * HEAD_PAD), rows(N_HEADS * HEAD_PAD), rows(N_HEADS * HEAD_PAD),
                   rows(D_MODEL), rows(D_MODEL)],
        scratch_shapes=[pltpu.VMEM((8, CONV_WIDTH), F32)],
        compiler_params=pltpu.CompilerParams(
            dimension_semantics=("arbitrary",), vmem_limit_bytes=VMEM_LIMIT),
        name="pre_mixer",
    )(x2, mod, norm_g, win2, q_norm_g, wq2, wq2s, kv_norm_g, wk2, wv, cosf, sinf, conv_w,
      w_up_conv)


def _attn_kernel(q_ref, k_ref, v_ref, o_ref, m_ref, acc_ref):
    i = pl.program_id(1)
    tq = q_ref.shape[0]
    tk = tq
    c_exp = (QK_HEAD ** -0.5) * math.log2(math.e)

    m_ref[...] = jnp.full_like(m_ref, NEG_BIG)
    acc_ref[...] = jnp.zeros_like(acc_ref)

    def step(j, masked):
        k0 = pl.multiple_of(j * tk, tk)
        if masked:
            rq = lax.broadcasted_iota(jnp.int32, (tq, tk), 0) // CHUNK
            ck = lax.broadcasted_iota(jnp.int32, (tq, tk), 1) // CHUNK
            allowed = ck <= rq
        for hd in range(N_HEADS):
            hs = slice(hd * HEAD_PAD, (hd + 1) * HEAD_PAD)
            s = lax.dot_general(q_ref[:, hs], k_ref[pl.ds(k0, tk), hs],
                                (((1,), (1,)), ((), ())), preferred_element_type=F32)
            if masked:
                s = jnp.where(allowed, s, NEG_BIG)
            m_old = m_ref[hd]
            s_max = s[:, 0:LANES]
            for c in range(1, tk // LANES):
                s_max = jnp.maximum(s_max, s[:, c * LANES:(c + 1) * LANES])
            m_new = jnp.maximum(m_old, jnp.max(s_max, axis=-1, keepdims=True))
            alpha = jnp.exp2((m_old - m_new) * c_exp)
            p = jnp.concatenate(
                [jnp.exp2((s[:, c * LANES:(c + 1) * LANES] - m_new) * c_exp).astype(BF16)
                 for c in range(tk // LANES)], axis=-1)
            acc_ref[hd] = alpha * acc_ref[hd] + _dot(p, v_ref[pl.ds(k0, tk), hs])
            m_ref[hd] = m_new

    def body(j, carry):
        step(j, False)
        return carry

    lax.fori_loop(0, i, body, 0)
    step(i, True)

    for hp in range(N_HEADS // 2):
        pair = []
        for hd in (2 * hp, 2 * hp + 1):
            acc = acc_ref[hd]
            pair.append(acc[:, 0:V_HEAD] / acc[:, V_HEAD:V_HEAD + 1])
        o_ref[:, hp * LANES:(hp + 1) * LANES] = jnp.concatenate(pair, axis=-1).astype(BF16)


def _attention(q, k, v, batch, seq):
    tq = ATT_BLOCK
    nq = seq // tq
    return pl.pallas_call(
        _attn_kernel,
        out_shape=jax.ShapeDtypeStruct((batch * seq, N_HEADS * V_HEAD), BF16),
        grid=(batch, nq),
        in_specs=[
            pl.BlockSpec((tq, N_HEADS * HEAD_PAD), lambda b, i: (b * nq + i, 0)),
            pl.BlockSpec((seq, N_HEADS * HEAD_PAD), lambda b, i: (b, 0)),
            pl.BlockSpec((seq, N_HEADS * HEAD_PAD), lambda b, i: (b, 0)),
        ],
        out_specs=pl.BlockSpec((tq, N_HEADS * V_HEAD), lambda b, i: (b * nq + i, 0)),
        scratch_shapes=[pltpu.VMEM((N_HEADS, tq, LANES), F32),
                        pltpu.VMEM((N_HEADS, tq, LANES), F32)],
        compiler_params=pltpu.CompilerParams(
            dimension_semantics=("arbitrary", "arbitrary"), vmem_limit_bytes=VMEM_LIMIT),
        name="attention",
    )(q, k, v)


def _post_kernel(attn_ref, sga_ref, gc_ref, x_ref, mod_ref, wua_ref, wo_ref, g_ref, rw_ref,
                 rb_ref, x1_ref, h2_ref, idx_ref, gate_ref):
    mod = mod_ref[...]
    a_branch = _dot(attn_ref[...], wua_ref[...])
    merged = sga_ref[...].astype(F32) * a_branch + gc_ref[...].astype(F32)
    mix = _dot(merged.astype(BF16), wo_ref[...])
    x1 = x_ref[...] + mod[2:3] * mix
    x1_ref[...] = x1
    h2 = _rms(x1, g_ref[...]) * (1.0 + mod[4:5]) + mod[3:4]
    h2_ref[...] = h2.astype(BF16)

    logits = jnp.dot(h2, rw_ref[...], preferred_element_type=F32,
                     precision=lax.Precision.HIGHEST) + rb_ref[...]
    lane = lax.broadcasted_iota(jnp.int32, logits.shape, 1)
    work = logits
    vals, idxs = [], []
    for _ in range(TOP_K):
        mk = jnp.max(work, axis=-1, keepdims=True)
        ik = jnp.min(jnp.where(work == mk, lane, LANES), axis=-1, keepdims=True)
        vals.append(mk)
        idxs.append(ik)
        work = jnp.where(lane == ik, -jnp.inf, work)
    es = [jnp.exp(vk - vals[0]) for vk in vals]
    denom = es[0] + es[1] + es[2] + es[3]
    idx_out = jnp.zeros(logits.shape, jnp.int32)
    gate_out = jnp.zeros(logits.shape, F32)
    for kk in range(TOP_K):
        idx_out = jnp.where(lane == kk, idxs[kk], idx_out)
        gate_out = jnp.where(lane == kk, es[kk] / denom, gate_out)
    idx_ref[...] = idx_out
    gate_ref[...] = gate_out


def _post(attn, sga, gc, x2, mod, wua, wo, norm_g, rw_pad, rb_pad, seq):
    t = x2.shape[0]
    tm = ROW_TILE
    tiles_per_seq = seq // tm
    full = lambda a: pl.BlockSpec(a.shape, lambda i: (0,) * a.ndim)
    rows = lambda w: pl.BlockSpec((tm, w), lambda i: (i, 0))
    outs = [jax.ShapeDtypeStruct((t, D_MODEL), F32),
            jax.ShapeDtypeStruct((t, D_MODEL), BF16),
            jax.ShapeDtypeStruct((t, LANES), jnp.int32),
            jax.ShapeDtypeStruct((t, LANES), F32)]
    return pl.pallas_call(
        _post_kernel,
        out_shape=outs,
        grid=(t // tm,),
        in_specs=[
            rows(N_HEADS * V_HEAD), rows(D_MODEL), rows(D_MODEL), rows(D_MODEL),
            pl.BlockSpec((None, 8, D_MODEL), lambda i: (i // tiles_per_seq, 0, 0)),
            full(wua), full(wo), full(norm_g), full(rw_pad), full(rb_pad),
        ],
        out_specs=[rows(D_MODEL), rows(D_MODEL), rows(LANES), rows(LANES)],
        compiler_params=pltpu.CompilerParams(
            dimension_semantics=("arbitrary",), vmem_limit_bytes=VMEM_LIMIT),
        name="post_mixer",
    )(attn, sga, gc, x2, mod, wua, wo, norm_g, rw_pad, rb_pad)


def _moe_kernel(be_ref, nu_ref, xs_ref, wgu_ref, bgu_ref, wd_ref, bd_ref, o_ref, wgu_bf, wd_bf):
    b = pl.program_id(0)
    used = b < nu_ref[0]
    e = be_ref[b]
    e_prev = be_ref[jnp.maximum(b - 1, 0)]
    fresh = jnp.logical_or(b == 0, e != e_prev)

    @pl.when(jnp.logical_and(used, fresh))
    def _():
        wgu_bf[...] = wgu_ref[...].astype(BF16)
        wd_bf[...] = wd_ref[...].astype(BF16)

    @pl.when(used)
    def _():
        gu = _dot(xs_ref[...], wgu_bf[...]) + bgu_ref[...]
        gate = jnp.minimum(gu[:, :D_EXPERT], SWIGLU_LIMIT)
        up = jnp.clip(gu[:, D_EXPERT:], -SWIGLU_LIMIT, SWIGLU_LIMIT)
        act = (up + 1.0) * (gate * jax.nn.sigmoid(gate * SWIGLU_ALPHA))
        o_ref[...] = (_dot(act.astype(BF16), wd_bf[...]) + bd_ref[...]).astype(BF16)

    @pl.when(jnp.logical_not(used))
    def _():
        o_ref[...] = jnp.zeros_like(o_ref)


def _moe(block_expert, n_used, xs, w_gu, b_gu, w_down, b_down):
    n_rows = xs.shape[0]
    n_blocks = n_rows // MOE_BLOCK
    grid_spec = pltpu.PrefetchScalarGridSpec(
        num_scalar_prefetch=2,
        grid=(n_blocks,),
        in_specs=[
            pl.BlockSpec((MOE_BLOCK, D_MODEL), lambda b, be, nu: (b, 0)),
            pl.BlockSpec((None, D_MODEL, 2 * D_EXPERT), lambda b, be, nu: (be[b], 0, 0)),
            pl.BlockSpec((None, 1, 2 * D_EXPERT), lambda b, be, nu: (be[b], 0, 0)),
            pl.BlockSpec((None, D_EXPERT, D_MODEL), lambda b, be, nu: (be[b], 0, 0)),
            pl.BlockSpec((None, 1, D_MODEL), lambda b, be, nu: (be[b], 0, 0)),
        ],
        out_specs=pl.BlockSpec((MOE_BLOCK, D_MODEL), lambda b, be, nu: (b, 0)),
        scratch_shapes=[pltpu.VMEM((D_MODEL, 2 * D_EXPERT), BF16),
                        pltpu.VMEM((D_EXPERT, D_MODEL), BF16)],
    )
    return pl.pallas_call(
        _moe_kernel,
        out_shape=jax.ShapeDtypeStruct((n_rows, D_MODEL), BF16),
        grid_spec=grid_spec,
        compiler_params=pltpu.CompilerParams(
            dimension_semantics=("arbitrary",), vmem_limit_bytes=VMEM_LIMIT),
        name="moe_experts",
    )(block_expert, n_used, xs, w_gu, b_gu, w_down, b_down)


def _final_kernel(last_layer, x1_ref, y_ref, gate_ref, mod_ref, g_ref, o_ref):
    mod = mod_ref[...]
    gate = gate_ref[...]
    ffn = gate[:, 0:1] * y_ref[:, 0:D_MODEL].astype(F32)
    for kk in range(1, TOP_K):
        ffn = ffn + gate[:, kk:kk + 1] * y_ref[:, kk * D_MODEL:(kk + 1) * D_MODEL].astype(F32)
    x = x1_ref[...] + mod[5:6] * ffn
    o_ref[...] = _rms(x, g_ref[...]) if last_layer else x


def _final(x1, y_tk, gate, mod, norm_g, seq, last_layer):
    t = x1.shape[0]
    tm = ROW_TILE
    tiles_per_seq = seq // tm
    rows = lambda w: pl.BlockSpec((tm, w), lambda i: (i, 0))
    return pl.pallas_call(
        functools.partial(_final_kernel, last_layer),
        out_shape=jax.ShapeDtypeStruct((t, D_MODEL), F32),
        grid=(t // tm,),
        in_specs=[
            rows(D_MODEL), rows(TOP_K * D_MODEL), rows(LANES),
            pl.BlockSpec((None, 8, D_MODEL), lambda i: (i // tiles_per_seq, 0, 0)),
            pl.BlockSpec(norm_g.shape, lambda i: (0, 0)),
        ],
        out_specs=rows(D_MODEL),
        compiler_params=pltpu.CompilerParams(
            dimension_semantics=("arbitrary",), vmem_limit_bytes=VMEM_LIMIT),
        name="combine_final",
    )(x1, y_tk, gate, mod, norm_g)


def _swap_halves(w):
    half = w.shape[-1] // 2
    return jnp.concatenate([w[..., half:], w[..., :half]], axis=-1)


def _prep_weights(w_in, w_uq, w_ukv):
    d = w_in.shape[0]
    splits = (Q_LORA, KV_LORA, QK_ROPE, CONV_WIDTH, CONV_WIDTH, CONV_WIDTH, D_MODEL, D_MODEL)
    offs = [0]
    for s in splits:
        offs.append(offs[-1] + s)
    part = lambda n: w_in[:, offs[n]:offs[n + 1]]
    z = lambda n: jnp.zeros((d, n), w_in.dtype)
    w_kpe = part(2)
    kpe_a = jnp.concatenate([z(QK_NOPE), w_kpe, z(HEAD_PAD - QK_HEAD)], axis=1)
    kpe_b = jnp.concatenate([z(QK_NOPE), _swap_halves(w_kpe), z(HEAD_PAD - QK_HEAD)], axis=1)
    win2 = jnp.concatenate([part(0), part(1), kpe_a, kpe_b, part(3), part(4), part(5), part(6),
                            part(7)], axis=1).astype(BF16)

    wq = w_uq.reshape(Q_LORA, N_HEADS, QK_HEAD)
    zq = lambda n: jnp.zeros((Q_LORA, N_HEADS, n), w_uq.dtype)
    wq2 = jnp.concatenate([wq, zq(HEAD_PAD - QK_HEAD)], axis=-1)
    wq2s = jnp.concatenate([zq(QK_NOPE), _swap_halves(wq[..., QK_NOPE:]), zq(HEAD_PAD - QK_HEAD)],
                           axis=-1)
    wq2 = wq2.reshape(Q_LORA, N_HEADS * HEAD_PAD).astype(BF16)
    wq2s = wq2s.reshape(Q_LORA, N_HEADS * HEAD_PAD).astype(BF16)

    wkv = w_ukv.reshape(KV_LORA, N_HEADS, QK_NOPE + V_HEAD)
    wk2 = jnp.concatenate([wkv[..., :QK_NOPE],
                           jnp.zeros((KV_LORA, N_HEADS, HEAD_PAD - QK_NOPE), w_ukv.dtype)], axis=-1)
    wk2 = wk2.reshape(KV_LORA, N_HEADS * HEAD_PAD).astype(BF16)
    wv = jnp.concatenate([wkv[..., QK_NOPE:],
                          jnp.zeros((KV_LORA, N_HEADS, HEAD_PAD - V_HEAD), w_ukv.dtype)], axis=-1)
    wv = wv.reshape(KV_LORA, N_HEADS * HEAD_PAD).astype(BF16)
    return win2, wq2, wq2s, wk2, wv


def _rope_tables(positions):
    inv_freq = 1.0 / (ROPE_THETA ** (jnp.arange(0, QK_ROPE, 2, dtype=F32) / QK_ROPE))
    ang = positions.astype(F32).reshape(-1, 1) * inv_freq
    cos, sin = jnp.cos(ang), jnp.sin(ang)
    t = ang.shape[0]
    ones = jnp.ones((t, QK_NOPE), F32)
    zpad = jnp.zeros((t, HEAD_PAD - QK_HEAD), F32)
    cosf = jnp.concatenate([ones, cos, cos, zpad], axis=1)
    sinf = jnp.concatenate([jnp.zeros((t, QK_NOPE), F32), -sin, sin, zpad], axis=1)
    return cosf, sinf


def _route(top_idx, n_tokens):
    onehot = (top_idx[:, :, None] == jnp.arange(N_EXPERTS, dtype=jnp.int32)).astype(jnp.int32)
    per_tok = onehot.sum(axis=1)
    csum = jnp.cumsum(per_tok, axis=0)
    counts = csum[-1]
    excl = csum - per_tok
    rank = jnp.take_along_axis(excl, top_idx, axis=1)
    padded = (counts + MOE_BLOCK - 1) // MOE_BLOCK * MOE_BLOCK
    pad_end = jnp.cumsum(padded)
    pad_start = pad_end - padded
    dest = pad_start[top_idx] + rank
    n_rows = n_tokens * TOP_K + N_EXPERTS * MOE_BLOCK
    n_blocks = n_rows // MOE_BLOCK
    block_start = jnp.arange(n_blocks, dtype=jnp.int32) * MOE_BLOCK
    block_expert = jnp.minimum(
        (pad_end[None, :] <= block_start[:, None]).astype(jnp.int32).sum(axis=1), N_EXPERTS - 1)
    n_used = (pad_end[-1] // MOE_BLOCK).astype(jnp.int32).reshape(1)
    return dest.astype(jnp.int32), block_expert, n_used, n_rows


def kernel(x, c, positions, w_ada, b_ada, norm_mix_g, w_in, q_norm_g, w_uq, kv_norm_g, w_ukv,
           w_up_attn, conv_w, w_up_conv, w_o, norm_ffn_g, router_w, router_b, w_gu, b_gu,
           w_down, b_down, norm_final_g):
    batch, seq, d = x.shape
    t = batch * seq
    depth = w_ada.shape[0]
    x2 = x.reshape(t, d)
    cosf, sinf = _rope_tables(positions)
    c_pad = jnp.zeros((8, d), F32).at[:batch].set(c)

    for l in range(depth):
        ada = _ada(c_pad, w_ada[l], b_ada[l].reshape(1, -1))
        mod = ada[:batch].reshape(batch, 6, d)
        mod = jnp.concatenate([mod, jnp.zeros((batch, 2, d), F32)], axis=1)

        win2, wq2, wq2s, wk2, wv = _prep_weights(w_in[l], w_uq[l], w_ukv[l])
        q, k, v, sga, gc = _pre(x2, mod, norm_mix_g[l].reshape(1, d), win2,
                                q_norm_g[l].reshape(1, -1), wq2, wq2s,
                                kv_norm_g[l].reshape(1, -1), wk2, wv, cosf, sinf, conv_w[l],
                                w_up_conv[l].astype(BF16), seq)
        attn = _attention(q, k, v, batch, seq)

        rw_pad = jnp.concatenate([router_w[l], jnp.zeros((d, LANES - N_EXPERTS), F32)], axis=1)
        rb_pad = jnp.concatenate([router_b[l], jnp.full((LANES - N_EXPERTS,), NEG_BIG, F32)])
        x1, h2, idx_pad, gate_pad = _post(attn, sga, gc, x2, mod, w_up_attn[l].astype(BF16),
                                          w_o[l].astype(BF16), norm_ffn_g[l].reshape(1, d),
                                          rw_pad, rb_pad.reshape(1, LANES), seq)

        top_idx = idx_pad[:, :TOP_K]
        dest, block_expert, n_used, n_rows = _route(top_idx, t)
        flat_tok = jnp.arange(t * TOP_K, dtype=jnp.int32) // TOP_K
        row_tok = jnp.zeros((n_rows,), jnp.int32).at[dest.reshape(-1)].set(flat_tok)
        xs = h2[row_tok]
        ys = _moe(block_expert, n_used, xs, w_gu[l], b_gu[l].reshape(N_EXPERTS, 1, -1),
                  w_down[l], b_down[l].reshape(N_EXPERTS, 1, -1))
        y_tk = ys[dest.reshape(-1)].reshape(t, TOP_K * d)
        x2 = _final(x1, y_tk, gate_pad, mod, norm_final_g.reshape(1, d), seq, l == depth - 1)

    return x2.reshape(batch, seq, d)
```

```python
import functools
import math

import jax
import jax.numpy as jnp
from jax import lax
from jax.experimental import pallas as pl
from jax.experimental.pallas import tpu as pltpu
from jax.experimental.pallas import tpu_sc as plsc

D_MODEL = 1024
CHUNK = 64
N_HEADS = 8
Q_LORA = 256
KV_LORA = 128
QK_NOPE = 64
QK_ROPE = 32
V_HEAD = 64
QK_HEAD = QK_NOPE + QK_ROPE
ROPE_THETA = 10000.0
CONV_WIDTH = 512
CONV_K = 3
N_EXPERTS = 32
TOP_K = 4
D_EXPERT = 1024
SWIGLU_LIMIT = 7.0
SWIGLU_ALPHA = 1.702
MOE_BLOCK = 256
RMS_EPS = 1e-6

LANES = 128
HEAD_PAD = 128
NEG_BIG = -1e30
VMEM_LIMIT = 56 * 1024 * 1024

F32 = jnp.float32
BF16 = jnp.bfloat16

ROW_TILE = 512
ATT_BLOCK = 256


def _rms(x, g):
    ms = jnp.mean(x * x, axis=-1, keepdims=True)
    return x * lax.rsqrt(ms + RMS_EPS) * g


def _dot(a, b):
    return jnp.dot(a, b, preferred_element_type=F32)


def _ada_kernel(c_ref, w_ref, b_ref, o_ref):
    c = c_ref[...]
    ca = (c * jax.nn.sigmoid(c)).astype(BF16)
    o_ref[...] = _dot(ca, w_ref[...].astype(BF16)) + b_ref[...]


def _ada(c_pad, w_ada, b_ada):
    n = w_ada.shape[1]
    tn = 1024
    return pl.pallas_call(
        _ada_kernel,
        out_shape=jax.ShapeDtypeStruct((c_pad.shape[0], n), F32),
        grid=(n // tn,),
        in_specs=[
            pl.BlockSpec(c_pad.shape, lambda j: (0, 0)),
            pl.BlockSpec((D_MODEL, tn), lambda j: (0, j)),
            pl.BlockSpec((1, tn), lambda j: (0, j)),
        ],
        out_specs=pl.BlockSpec((c_pad.shape[0], tn), lambda j: (0, j)),
        compiler_params=pltpu.CompilerParams(
            dimension_semantics=("arbitrary",), vmem_limit_bytes=VMEM_LIMIT),
        name="ada",
    )(c_pad, w_ada, b_ada)


_C_QLAT = 0
_C_KVLAT = _C_QLAT + Q_LORA
_C_KPE_A = _C_KVLAT + KV_LORA
_C_KPE_B = _C_KPE_A + HEAD_PAD
_C_U = _C_KPE_B + HEAD_PAD
_C_C = _C_U + CONV_WIDTH
_C_B = _C_C + CONV_WIDTH
_C_GA = _C_B + CONV_WIDTH
_C_GC = _C_GA + D_MODEL
_C_END = _C_GC + D_MODEL


def _pre_kernel(tiles_per_seq, x_ref, mod_ref, g_ref, win_ref, qg_ref, wq_ref, wqs_ref,
                kvg_ref, wk_ref, wv_ref, cos_ref, sin_ref, cw_ref, wuc_ref,
                q_ref, k_ref, v_ref, sga_ref, gc_ref, carry_ref):
    i = pl.program_id(0)
    tm = x_ref.shape[0]
    mod = mod_ref[...]
    h = _rms(x_ref[...], g_ref[...]) * (1.0 + mod[1:2]) + mod[0:1]
    hb = h.astype(BF16)

    cosf = cos_ref[...]
    sinf = sin_ref[...]
    cos8 = jnp.concatenate([cosf] * N_HEADS, axis=-1)
    sin8 = jnp.concatenate([sinf] * N_HEADS, axis=-1)

    small = _dot(hb, win_ref[:, _C_QLAT:_C_U])
    q_lat = small[:, _C_QLAT:_C_KVLAT]
    kv_lat = small[:, _C_KVLAT:_C_KPE_A]
    kpe = small[:, _C_KPE_A:_C_KPE_B] * cosf + small[:, _C_KPE_B:_C_U] * sinf
    qn = _rms(q_lat, qg_ref[...]).astype(BF16)
    q = _dot(qn, wq_ref[...]) * cos8 + _dot(qn, wqs_ref[...]) * sin8
    q_ref[...] = q.astype(BF16)
    kvn = _rms(kv_lat, kvg_ref[...]).astype(BF16)
    k = _dot(kvn, wk_ref[...]) + jnp.concatenate([kpe] * N_HEADS, axis=-1)
    k_ref[...] = k.astype(BF16)
    lane = lax.broadcasted_iota(jnp.int32, (tm, N_HEADS * HEAD_PAD), 1)
    ones_col = jnp.where(lane % HEAD_PAD == V_HEAD, 1.0, 0.0)
    v_ref[...] = (_dot(kvn, wv_ref[...]) + ones_col).astype(BF16)

    ucb = _dot(hb, win_ref[:, _C_U:_C_GA])
    cu = ucb[:, 0:CONV_WIDTH] * ucb[:, CONV_WIDTH:2 * CONV_WIDTH]
    b_gate = ucb[:, 2 * CONV_WIDTH:3 * CONV_WIDTH]

    @pl.when(i % tiles_per_seq == 0)
    def _():
        carry_ref[...] = jnp.zeros_like(carry_ref)

    prev = carry_ref[...]
    row = lax.broadcasted_iota(jnp.int32, cu.shape, 0)
    cu1 = jnp.where(row == 0, prev[7:8], pltpu.roll(cu, 1, 0))
    cu2 = jnp.where(row == 0, prev[6:7], jnp.where(row == 1, prev[7:8], pltpu.roll(cu, 2, 0)))
    cw = cw_ref[...]
    z = cw[2:3] * cu + cw[1:2] * cu1 + cw[0:1] * cu2
    carry_ref[...] = cu[tm - 8:tm]
    c_branch = _dot((b_gate * z).astype(BF16), wuc_ref[...])

    gates = _dot(hb, win_ref[:, _C_GA:_C_END])
    sga_ref[...] = jax.nn.sigmoid(gates[:, 0:D_MODEL]).astype(BF16)
    gc_ref[...] = (jax.nn.sigmoid(gates[:, D_MODEL:]) * c_branch).astype(BF16)


def _pre(x2, mod, norm_g, win2, q_norm_g, wq2, wq2s, kv_norm_g, wk2, wv, cosf, sinf, conv_w,
         w_up_conv, seq):
    t = x2.shape[0]
    tm = ROW_TILE
    tiles_per_seq = seq // tm
    full = lambda a: pl.BlockSpec(a.shape, lambda i: (0,) * a.ndim)
    rows = lambda w: pl.BlockSpec((tm, w), lambda i: (i, 0))
    outs = [jax.ShapeDtypeStruct((t, N_HEADS * HEAD_PAD), BF16),
            jax.ShapeDtypeStruct((t, N_HEADS * HEAD_PAD), BF16),
            jax.ShapeDtypeStruct((t, N_HEADS * HEAD_PAD), BF16),
            jax.ShapeDtypeStruct((t, D_MODEL), BF16),
            jax.ShapeDtypeStruct((t, D_MODEL), BF16)]
    return pl.pallas_call(
        functools.partial(_pre_kernel, tiles_per_seq),
        out_shape=outs,
        grid=(t // tm,),
        in_specs=[
            rows(D_MODEL),
            pl.BlockSpec((None, 8, D_MODEL), lambda i: (i // tiles_per_seq, 0, 0)),
            full(norm_g), full(win2), full(q_norm_g), full(wq2), full(wq2s),
            full(kv_norm_g), full(wk2), full(wv),
            rows(HEAD_PAD), rows(HEAD_PAD), full(conv_w), full(w_up_conv),
        ],
        out_specs=[rows(N_HEADS * HEAD_PAD), rows(N_HEADS * HEAD_PAD), rows(N_HEADS * HEAD_PAD),
                   rows(D_MODEL), rows(D_MODEL)],
        scratch_shapes=[pltpu.VMEM((8, CONV_WIDTH), F32)],
        compiler_params=pltpu.CompilerParams(
            dimension_semantics=("arbitrary",), vmem_limit_bytes=VMEM_LIMIT),
        name="pre_mixer",
    )(x2, mod, norm_g, win2, q_norm_g, wq2, wq2s, kv_norm_g, wk2, wv, cosf, sinf, conv_w,
      w_up_conv)


def _attn_kernel(q_ref, k_ref, v_ref, o_ref, m_ref, acc_ref):
    i = pl.program_id(1)
    tq = q_ref.shape[0]
    tk = tq
    c_exp = (QK_HEAD ** -0.5) * math.log2(math.e)

    m_ref[...] = jnp.full_like(m_ref, NEG_BIG)
    acc_ref[...] = jnp.zeros_like(acc_ref)

    def step(j, masked):
        k0 = pl.multiple_of(j * tk, tk)
        if masked:
            rq = lax.broadcasted_iota(jnp.int32, (tq, tk), 0) // CHUNK
            ck = lax.broadcasted_iota(jnp.int32, (tq, tk), 1) // CHUNK
            allowed = ck <= rq
        for hd in range(N_HEADS):
            hs = slice(hd * HEAD_PAD, (hd + 1) * HEAD_PAD)
            s = lax.dot_general(q_ref[:, hs], k_ref[pl.ds(k0, tk), hs],
                                (((1,), (1,)), ((), ())), preferred_element_type=F32)
            if masked:
                s = jnp.where(allowed, s, NEG_BIG)
            m_old = m_ref[hd]
            s_max = s[:, 0:LANES]
            for c in range(1, tk // LANES):
                s_max = jnp.maximum(s_max, s[:, c * LANES:(c + 1) * LANES])
            m_new = jnp.maximum(m_old, jnp.max(s_max, axis=-1, keepdims=True))
            alpha = jnp.exp2((m_old - m_new) * c_exp)
            p = jnp.concatenate(
                [jnp.exp2((s[:, c * LANES:(c + 1) * LANES] - m_new) * c_exp).astype(BF16)
                 for c in range(tk // LANES)], axis=-1)
            acc_ref[hd] = alpha * acc_ref[hd] + _dot(p, v_ref[pl.ds(k0, tk), hs])
            m_ref[hd] = m_new

    def body(j, carry):
        step(j, False)
        return carry

    lax.fori_loop(0, i, body, 0)
    step(i, True)

    for hp in range(N_HEADS // 2):
        pair = []
        for hd in (2 * hp, 2 * hp + 1):
            acc = acc_ref[hd]
            pair.append(acc[:, 0:V_HEAD] / acc[:, V_HEAD:V_HEAD + 1])
        o_ref[:, hp * LANES:(hp + 1) * LANES] = jnp.concatenate(pair, axis=-1).astype(BF16)


def _attention(q, k, v, batch, seq):
    tq = ATT_BLOCK
    nq = seq // tq
    return pl.pallas_call(
        _attn_kernel,
        out_shape=jax.ShapeDtypeStruct((batch * seq, N_HEADS * V_HEAD), BF16),
        grid=(batch, nq),
        in_specs=[
            pl.BlockSpec((tq, N_HEADS * HEAD_PAD), lambda b, i: (b * nq + i, 0)),
            pl.BlockSpec((seq, N_HEADS * HEAD_PAD), lambda b, i: (b, 0)),
            pl.BlockSpec((seq, N_HEADS * HEAD_PAD), lambda b, i: (b, 0)),
        ],
        out_specs=pl.BlockSpec((tq, N_HEADS * V_HEAD), lambda b, i: (b * nq + i, 0)),
        scratch_shapes=[pltpu.VMEM((N_HEADS, tq, LANES), F32),
                        pltpu.VMEM((N_HEADS, tq, LANES), F32)],
        compiler_params=pltpu.CompilerParams(
            dimension_semantics=("arbitrary", "arbitrary"), vmem_limit_bytes=VMEM_LIMIT),
        name="attention",
    )(q, k, v)


def _post_kernel(attn_ref, sga_ref, gc_ref, x_ref, mod_ref, wua_ref, wo_ref, g_ref, rw_ref,
                 rb_ref, x1_ref, h2_ref, idx_ref, gate_ref, rank_ref, cnt_out_ref, cnt_ref):
    mod = mod_ref[...]
    a_branch = _dot(attn_ref[...], wua_ref[...])
    merged = sga_ref[...].astype(F32) * a_branch + gc_ref[...].astype(F32)
    mix = _dot(merged.astype(BF16), wo_ref[...])
    x1 = x_ref[...] + mod[2:3] * mix
    x1_ref[...] = x1
    h2 = _rms(x1, g_ref[...]) * (1.0 + mod[4:5]) + mod[3:4]
    h2_ref[...] = h2

    logits = jnp.dot(h2, rw_ref[...], preferred_element_type=F32,
                     precision=lax.Precision.HIGHEST) + rb_ref[...]
    lane = lax.broadcasted_iota(jnp.int32, logits.shape, 1)
    work = logits
    vals, idxs = [], []
    for _ in range(TOP_K):
        mk = jnp.max(work, axis=-1, keepdims=True)
        ik = jnp.min(jnp.where(work == mk, lane, LANES), axis=-1, keepdims=True)
        vals.append(mk)
        idxs.append(ik)
        work = jnp.where(lane == ik, -jnp.inf, work)
    es = [jnp.exp(vk - vals[0]) for vk in vals]
    denom = es[0] + es[1] + es[2] + es[3]
    @pl.when(pl.program_id(0) == 0)
    def _():
        cnt_ref[...] = jnp.zeros_like(cnt_ref)

    tm = logits.shape[0]
    chosen = jnp.zeros(logits.shape, F32)
    for kk in range(TOP_K):
        chosen = chosen + jnp.where(lane == idxs[kk], 1.0, 0.0)
    r_i = lax.broadcasted_iota(jnp.int32, (tm, tm), 0)
    c_i = lax.broadcasted_iota(jnp.int32, (tm, tm), 1)
    earlier = jnp.where(c_i < r_i, 1.0, 0.0).astype(BF16)
    before = _dot(earlier, chosen.astype(BF16)) + cnt_ref[0:1]
    cnt_new = cnt_ref[...] + jnp.sum(chosen, axis=0, keepdims=True)
    cnt_ref[...] = cnt_new
    cnt_out_ref[...] = cnt_new.astype(jnp.int32)

    idx_out = jnp.zeros(logits.shape, jnp.int32)
    gate_out = jnp.zeros(logits.shape, F32)
    rank_out = jnp.zeros(logits.shape, jnp.int32)
    for kk in range(TOP_K):
        rank_k = jnp.sum(jnp.where(lane == idxs[kk], before, 0.0), axis=-1, keepdims=True)
        idx_out = jnp.where(lane == kk, idxs[kk], idx_out)
        gate_out = jnp.where(lane == kk, es[kk] / denom, gate_out)
        rank_out = jnp.where(lane == kk, rank_k.astype(jnp.int32), rank_out)
    idx_ref[...] = idx_out
    gate_ref[...] = gate_out
    rank_ref[...] = rank_out


def _post(attn, sga, gc, x2, mod, wua, wo, norm_g, rw_pad, rb_pad, seq):
    t = x2.shape[0]
    tm = ROW_TILE
    tiles_per_seq = seq // tm
    full = lambda a: pl.BlockSpec(a.shape, lambda i: (0,) * a.ndim)
    rows = lambda w: pl.BlockSpec((tm, w), lambda i: (i, 0))
    outs = [jax.ShapeDtypeStruct((t, D_MODEL), F32),
            jax.ShapeDtypeStruct((t, D_MODEL), F32),
            jax.ShapeDtypeStruct((t, LANES), jnp.int32),
            jax.ShapeDtypeStruct((t, LANES), F32),
            jax.ShapeDtypeStruct((t, LANES), jnp.int32),
            jax.ShapeDtypeStruct((8, LANES), jnp.int32)]
    return pl.pallas_call(
        _post_kernel,
        out_shape=outs,
        grid=(t // tm,),
        in_specs=[
            rows(N_HEADS * V_HEAD), rows(D_MODEL), rows(D_MODEL), rows(D_MODEL),
            pl.BlockSpec((None, 8, D_MODEL), lambda i: (i // tiles_per_seq, 0, 0)),
            full(wua), full(wo), full(norm_g), full(rw_pad), full(rb_pad),
        ],
        out_specs=[rows(D_MODEL), rows(D_MODEL), rows(LANES), rows(LANES), rows(LANES),
                   pl.BlockSpec((8, LANES), lambda i: (0, 0))],
        scratch_shapes=[pltpu.VMEM((8, LANES), F32)],
        compiler_params=pltpu.CompilerParams(
            dimension_semantics=("arbitrary",), vmem_limit_bytes=VMEM_LIMIT),
        name="post_mixer",
    )(attn, sga, gc, x2, mod, wua, wo, norm_g, rw_pad, rb_pad)


def _moe_kernel(be_ref, nv_ref, xs_ref, wgu_ref, bgu_ref, wd_ref, bd_ref, o_ref, wgu_bf, wd_bf):
    b = pl.program_id(0)
    n_valid = nv_ref[b]
    used = n_valid > 0
    e = be_ref[b]
    e_prev = be_ref[jnp.maximum(b - 1, 0)]
    fresh = jnp.logical_or(b == 0, e != e_prev)

    @pl.when(jnp.logical_and(used, fresh))
    def _():
        wgu_bf[...] = wgu_ref[...].astype(BF16)
        wd_bf[...] = wd_ref[...].astype(BF16)

    @pl.when(used)
    def _():
        row = lax.broadcasted_iota(jnp.int32, xs_ref.shape, 0)
        xs = jnp.where(row < n_valid, xs_ref[...], 0.0).astype(BF16)
        gu = _dot(xs, wgu_bf[...]) + bgu_ref[...]
        gate = jnp.minimum(gu[:, :D_EXPERT], SWIGLU_LIMIT)
        up = jnp.clip(gu[:, D_EXPERT:], -SWIGLU_LIMIT, SWIGLU_LIMIT)
        act = (up + 1.0) * (gate * jax.nn.sigmoid(gate * SWIGLU_ALPHA))
        o_ref[...] = _dot(act.astype(BF16), wd_bf[...]) + bd_ref[...]

    @pl.when(jnp.logical_not(used))
    def _():
        o_ref[...] = jnp.zeros_like(o_ref)


def _moe(block_expert, block_valid, xs, w_gu, b_gu, w_down, b_down):
    n_rows = xs.shape[0]
    n_blocks = n_rows // MOE_BLOCK
    grid_spec = pltpu.PrefetchScalarGridSpec(
        num_scalar_prefetch=2,
        grid=(n_blocks,),
        in_specs=[
            pl.BlockSpec((MOE_BLOCK, D_MODEL), lambda b, be, nu: (b, 0)),
            pl.BlockSpec((None, D_MODEL, 2 * D_EXPERT), lambda b, be, nu: (be[b], 0, 0)),
            pl.BlockSpec((None, 1, 2 * D_EXPERT), lambda b, be, nu: (be[b], 0, 0)),
            pl.BlockSpec((None, D_EXPERT, D_MODEL), lambda b, be, nu: (be[b], 0, 0)),
            pl.BlockSpec((None, 1, D_MODEL), lambda b, be, nu: (be[b], 0, 0)),
        ],
        out_specs=pl.BlockSpec((MOE_BLOCK, D_MODEL), lambda b, be, nu: (b, 0)),
        scratch_shapes=[pltpu.VMEM((D_MODEL, 2 * D_EXPERT), BF16),
                        pltpu.VMEM((D_EXPERT, D_MODEL), BF16)],
    )
    return pl.pallas_call(
        _moe_kernel,
        out_shape=jax.ShapeDtypeStruct((n_rows, D_MODEL), F32),
        grid_spec=grid_spec,
        compiler_params=pltpu.CompilerParams(
            dimension_semantics=("arbitrary",), vmem_limit_bytes=VMEM_LIMIT),
        name="moe_experts",
    )(block_expert, block_valid, xs, w_gu, b_gu, w_down, b_down)


def _final_kernel(last_layer, x1_ref, y_ref, gate_ref, mod_ref, g_ref, o_ref):
    mod = mod_ref[...]
    gate = gate_ref[...]
    ffn = gate[:, 0:1] * y_ref[0]
    for kk in range(1, TOP_K):
        ffn = ffn + gate[:, kk:kk + 1] * y_ref[kk]
    x = x1_ref[...] + mod[5:6] * ffn
    o_ref[...] = _rms(x, g_ref[...]) if last_layer else x


def _final(x1, y_kt, gate, mod, norm_g, seq, last_layer):
    t = x1.shape[0]
    tm = ROW_TILE
    tiles_per_seq = seq // tm
    rows = lambda w: pl.BlockSpec((tm, w), lambda i: (i, 0))
    return pl.pallas_call(
        functools.partial(_final_kernel, last_layer),
        out_shape=jax.ShapeDtypeStruct((t, D_MODEL), F32),
        grid=(t // tm,),
        in_specs=[
            rows(D_MODEL), pl.BlockSpec((TOP_K, tm, D_MODEL), lambda i: (0, i, 0)), rows(LANES),
            pl.BlockSpec((None, 8, D_MODEL), lambda i: (i // tiles_per_seq, 0, 0)),
            pl.BlockSpec(norm_g.shape, lambda i: (0, 0)),
        ],
        out_specs=rows(D_MODEL),
        compiler_params=pltpu.CompilerParams(
            dimension_semantics=("arbitrary",), vmem_limit_bytes=VMEM_LIMIT),
        name="combine_final",
    )(x1, y_kt, gate, mod, norm_g)


def _swap_halves(w):
    half = w.shape[-1] // 2
    return jnp.concatenate([w[..., half:], w[..., :half]], axis=-1)


def _prep_weights(w_in, w_uq, w_ukv):
    d = w_in.shape[0]
    splits = (Q_LORA, KV_LORA, QK_ROPE, CONV_WIDTH, CONV_WIDTH, CONV_WIDTH, D_MODEL, D_MODEL)
    offs = [0]
    for s in splits:
        offs.append(offs[-1] + s)
    part = lambda n: w_in[:, offs[n]:offs[n + 1]]
    z = lambda n: jnp.zeros((d, n), w_in.dtype)
    w_kpe = part(2)
    kpe_a = jnp.concatenate([z(QK_NOPE), w_kpe, z(HEAD_PAD - QK_HEAD)], axis=1)
    kpe_b = jnp.concatenate([z(QK_NOPE), _swap_halves(w_kpe), z(HEAD_PAD - QK_HEAD)], axis=1)
    win2 = jnp.concatenate([part(0), part(1), kpe_a, kpe_b, part(3), part(4), part(5), part(6),
                            part(7)], axis=1).astype(BF16)

    wq = w_uq.reshape(Q_LORA, N_HEADS, QK_HEAD)
    zq = lambda n: jnp.zeros((Q_LORA, N_HEADS, n), w_uq.dtype)
    wq2 = jnp.concatenate([wq, zq(HEAD_PAD - QK_HEAD)], axis=-1)
    wq2s = jnp.concatenate([zq(QK_NOPE), _swap_halves(wq[..., QK_NOPE:]), zq(HEAD_PAD - QK_HEAD)],
                           axis=-1)
    wq2 = wq2.reshape(Q_LORA, N_HEADS * HEAD_PAD).astype(BF16)
    wq2s = wq2s.reshape(Q_LORA, N_HEADS * HEAD_PAD).astype(BF16)

    wkv = w_ukv.reshape(KV_LORA, N_HEADS, QK_NOPE + V_HEAD)
    wk2 = jnp.concatenate([wkv[..., :QK_NOPE],
                           jnp.zeros((KV_LORA, N_HEADS, HEAD_PAD - QK_NOPE), w_ukv.dtype)], axis=-1)
    wk2 = wk2.reshape(KV_LORA, N_HEADS * HEAD_PAD).astype(BF16)
    wv = jnp.concatenate([wkv[..., QK_NOPE:],
                          jnp.zeros((KV_LORA, N_HEADS, HEAD_PAD - V_HEAD), w_ukv.dtype)], axis=-1)
    wv = wv.reshape(KV_LORA, N_HEADS * HEAD_PAD).astype(BF16)
    return win2, wq2, wq2s, wk2, wv


def _rope_tables(positions):
    inv_freq = 1.0 / (ROPE_THETA ** (jnp.arange(0, QK_ROPE, 2, dtype=F32) / QK_ROPE))
    ang = positions.astype(F32).reshape(-1, 1) * inv_freq
    cos, sin = jnp.cos(ang), jnp.sin(ang)
    t = ang.shape[0]
    ones = jnp.ones((t, QK_NOPE), F32)
    zpad = jnp.zeros((t, HEAD_PAD - QK_HEAD), F32)
    cosf = jnp.concatenate([ones, cos, cos, zpad], axis=1)
    sinf = jnp.concatenate([jnp.zeros((t, QK_NOPE), F32), -sin, sin, zpad], axis=1)
    return cosf, sinf


def _route(top_idx, rank, counts, n_tokens):
    padded = (counts + MOE_BLOCK - 1) // MOE_BLOCK * MOE_BLOCK
    pad_end = jnp.cumsum(padded)
    pad_start = pad_end - padded
    experts = jnp.arange(N_EXPERTS, dtype=jnp.int32)
    start_of = jnp.sum(jnp.where(top_idx[:, :, None] == experts, pad_start, 0), axis=-1)
    dest = start_of + rank
    n_rows = n_tokens * TOP_K + N_EXPERTS * MOE_BLOCK
    n_blocks = n_rows // MOE_BLOCK
    block_start = jnp.arange(n_blocks, dtype=jnp.int32) * MOE_BLOCK
    block_expert = jnp.minimum(
        (pad_end[None, :] <= block_start[:, None]).astype(jnp.int32).sum(axis=1), N_EXPERTS - 1)
    group_end = (pad_start + counts)[block_expert]
    block_valid = jnp.clip(group_end - block_start, 0, MOE_BLOCK)
    return dest.astype(jnp.int32), block_expert, block_valid.astype(jnp.int32), n_rows


SC_CORES = 2
SC_SUBCORES = 16
SC_WORKERS = SC_CORES * SC_SUBCORES
SC_CHUNK = 64


def _sc_mesh():
    return plsc.VectorSubcoreMesh(core_axis_name="c", subcore_axis_name="s")


def _sc_worker():
    return lax.axis_index("s") * SC_CORES + lax.axis_index("c")


def _dispatch(h2, dest, n_rows):
    t, d = h2.shape
    per_w = t // SC_WORKERS
    n_chunks = per_w // SC_CHUNK
    idx = dest.reshape(SC_WORKERS, n_chunks, SC_CHUNK, TOP_K).transpose(0, 3, 1, 2)
    idx = idx.reshape(SC_WORKERS, TOP_K * n_chunks, SC_CHUNK)

    @functools.partial(
        pl.kernel, mesh=_sc_mesh(),
        out_type=jax.ShapeDtypeStruct((n_rows, d), h2.dtype),
        scratch_types=[pltpu.VMEM((TOP_K * n_chunks, SC_CHUNK), jnp.int32),
                       pltpu.VMEM((SC_CHUNK, d), h2.dtype)],
        name="moe_dispatch")
    def run(h2_hbm, idx_hbm, xs_hbm, idx_v, rows_v):
        w = _sc_worker()
        pltpu.sync_copy(idx_hbm.at[w], idx_v)

        @pl.loop(0, n_chunks)
        def _(g):
            pltpu.sync_copy(h2_hbm.at[pl.ds(w * per_w + g * SC_CHUNK, SC_CHUNK)], rows_v)
            for kk in range(TOP_K):
                pltpu.sync_copy(rows_v, xs_hbm.at[idx_v.at[kk * n_chunks + g]])

    return run(h2, idx)


def _undispatch(ys, dest):
    t = dest.shape[0]
    d = ys.shape[1]
    n_out = t * TOP_K
    per_w = n_out // SC_WORKERS
    n_chunks = per_w // SC_CHUNK
    idx = dest.T.reshape(SC_WORKERS, n_chunks, SC_CHUNK)

    @functools.partial(
        pl.kernel, mesh=_sc_mesh(),
        out_type=jax.ShapeDtypeStruct((n_out, d), ys.dtype),
        scratch_types=[pltpu.VMEM((n_chunks, SC_CHUNK), jnp.int32),
                       pltpu.VMEM((SC_CHUNK, d), ys.dtype),
                       pltpu.SemaphoreType.DMA],
        name="moe_undispatch")
    def run(ys_hbm, idx_hbm, out_hbm, idx_v, rows_v, sem):
        w = _sc_worker()
        pltpu.sync_copy(idx_hbm.at[w], idx_v)

        @pl.loop(0, n_chunks)
        def _(g):
            pltpu.async_copy(ys_hbm.at[idx_v.at[g]], rows_v, sem).wait()
            pltpu.sync_copy(rows_v, out_hbm.at[pl.ds(w * per_w + g * SC_CHUNK, SC_CHUNK)])

    return run(ys, idx).reshape(TOP_K, t, d)


def kernel(x, c, positions, w_ada, b_ada, norm_mix_g, w_in, q_norm_g, w_uq, kv_norm_g, w_ukv,
           w_up_attn, conv_w, w_up_conv, w_o, norm_ffn_g, router_w, router_b, w_gu, b_gu,
           w_down, b_down, norm_final_g):
    batch, seq, d = x.shape
    t = batch * seq
    depth = w_ada.shape[0]
    x2 = x.reshape(t, d)
    cosf, sinf = _rope_tables(positions)
    c_pad = jnp.zeros((8, d), F32).at[:batch].set(c)

    for l in range(depth):
        ada = _ada(c_pad, w_ada[l], b_ada[l].reshape(1, -1))
        mod = ada[:batch].reshape(batch, 6, d)
        mod = jnp.concatenate([mod, jnp.zeros((batch, 2, d), F32)], axis=1)

        win2, wq2, wq2s, wk2, wv = _prep_weights(w_in[l], w_uq[l], w_ukv[l])
        q, k, v, sga, gc = _pre(x2, mod, norm_mix_g[l].reshape(1, d), win2,
                                q_norm_g[l].reshape(1, -1), wq2, wq2s,
                                kv_norm_g[l].reshape(1, -1), wk2, wv, cosf, sinf, conv_w[l],
                                w_up_conv[l].astype(BF16), seq)
        attn = _attention(q, k, v, batch, seq)

        rw_pad = jnp.concatenate([router_w[l], jnp.zeros((d, LANES - N_EXPERTS), F32)], axis=1)
        rb_pad = jnp.concatenate([router_b[l], jnp.full((LANES - N_EXPERTS,), NEG_BIG, F32)])
        x1, h2, idx_pad, gate_pad, rank_pad, counts = _post(attn, sga, gc, x2, mod, w_up_attn[l].astype(BF16),
                                          w_o[l].astype(BF16), norm_ffn_g[l].reshape(1, d),
                                          rw_pad, rb_pad.reshape(1, LANES), seq)

        dest, block_expert, block_valid, n_rows = _route(
            idx_pad[:, :TOP_K], rank_pad[:, :TOP_K], counts[0, :N_EXPERTS], t)
        xs = _dispatch(h2, dest, n_rows)
        ys = _moe(block_expert, block_valid, xs, w_gu[l], b_gu[l].reshape(N_EXPERTS, 1, -1),
                  w_down[l], b_down[l].reshape(N_EXPERTS, 1, -1))
        y_kt = _undispatch(ys, dest)
        x2 = _final(x1, y_kt, gate_pad, mod, norm_final_g.reshape(1, d), seq, l == depth - 1)

    return x2.reshape(batch, seq, d)
```

```python
import functools
import math

import jax
import jax.numpy as jnp
from jax import lax
from jax.experimental import pallas as pl
from jax.experimental.pallas import tpu as pltpu
from jax.experimental.pallas import tpu_sc as plsc

D_MODEL = 1024
CHUNK = 64
N_HEADS = 8
Q_LORA = 256
KV_LORA = 128
QK_NOPE = 64
QK_ROPE = 32
V_HEAD = 64
QK_HEAD = QK_NOPE + QK_ROPE
ROPE_THETA = 10000.0
CONV_WIDTH = 512
CONV_K = 3
N_EXPERTS = 32
TOP_K = 4
D_EXPERT = 1024
SWIGLU_LIMIT = 7.0
SWIGLU_ALPHA = 1.702
MOE_BLOCK = 256
RMS_EPS = 1e-6

LANES = 128
HEAD_PAD = 128
NEG_BIG = -1e30
VMEM_LIMIT = 56 * 1024 * 1024

F32 = jnp.float32
BF16 = jnp.bfloat16

ROW_TILE = 512
ATT_BLOCK = 256


def _rms(x, g):
    ms = jnp.mean(x * x, axis=-1, keepdims=True)
    return x * lax.rsqrt(ms + RMS_EPS) * g


def _dot(a, b):
    return jnp.dot(a, b, preferred_element_type=F32)


def _ada_kernel(c_ref, w_ref, b_ref, o_ref):
    c = c_ref[...]
    ca = (c * jax.nn.sigmoid(c)).astype(BF16)
    o_ref[...] = _dot(ca, w_ref[...].astype(BF16)) + b_ref[...]


def _ada(c_pad, w_ada, b_ada):
    n = w_ada.shape[1]
    tn = 1024
    return pl.pallas_call(
        _ada_kernel,
        out_shape=jax.ShapeDtypeStruct((c_pad.shape[0], n), F32),
        grid=(n // tn,),
        in_specs=[
            pl.BlockSpec(c_pad.shape, lambda j: (0, 0)),
            pl.BlockSpec((D_MODEL, tn), lambda j: (0, j)),
            pl.BlockSpec((1, tn), lambda j: (0, j)),
        ],
        out_specs=pl.BlockSpec((c_pad.shape[0], tn), lambda j: (0, j)),
        compiler_params=pltpu.CompilerParams(
            dimension_semantics=("arbitrary",), vmem_limit_bytes=VMEM_LIMIT),
        name="ada",
    )(c_pad, w_ada, b_ada)


_C_QLAT = 0
_C_KVLAT = _C_QLAT + Q_LORA
_C_KPE_A = _C_KVLAT + KV_LORA
_C_KPE_B = _C_KPE_A + HEAD_PAD
_C_U = _C_KPE_B + HEAD_PAD
_C_C = _C_U + CONV_WIDTH
_C_B = _C_C + CONV_WIDTH
_C_GA = _C_B + CONV_WIDTH
_C_GC = _C_GA + D_MODEL
_C_END = _C_GC + D_MODEL


def _pre_kernel(tiles_per_seq, x_ref, mod_ref, g_ref, win_ref, qg_ref, wq_ref, wqs_ref,
                kvg_ref, wk_ref, wv_ref, cos_ref, sin_ref, cw_ref, wuc_ref,
                q_ref, k_ref, v_ref, sga_ref, gc_ref, carry_ref):
    i = pl.program_id(0)
    tm = x_ref.shape[0]
    mod = mod_ref[...]
    h = _rms(x_ref[...], g_ref[...]) * (1.0 + mod[1:2]) + mod[0:1]
    hb = h.astype(BF16)

    cosf = cos_ref[...]
    sinf = sin_ref[...]
    cos8 = jnp.concatenate([cosf] * N_HEADS, axis=-1)
    sin8 = jnp.concatenate([sinf] * N_HEADS, axis=-1)

    small = _dot(hb, win_ref[:, _C_QLAT:_C_U])
    q_lat = small[:, _C_QLAT:_C_KVLAT]
    kv_lat = small[:, _C_KVLAT:_C_KPE_A]
    kpe = small[:, _C_KPE_A:_C_KPE_B] * cosf + small[:, _C_KPE_B:_C_U] * sinf
    qn = _rms(q_lat, qg_ref[...]).astype(BF16)
    q = _dot(qn, wq_ref[...]) * cos8 + _dot(qn, wqs_ref[...]) * sin8
    q_ref[...] = q.astype(BF16)
    kvn = _rms(kv_lat, kvg_ref[...]).astype(BF16)
    k = _dot(kvn, wk_ref[...]) + jnp.concatenate([kpe] * N_HEADS, axis=-1)
    k_ref[...] = k.astype(BF16)
    lane = lax.broadcasted_iota(jnp.int32, (tm, N_HEADS * HEAD_PAD), 1)
    ones_col = jnp.where(lane % HEAD_PAD == V_HEAD, 1.0, 0.0)
    v_ref[...] = (_dot(kvn, wv_ref[...]) + ones_col).astype(BF16)

    ucb = _dot(hb, win_ref[:, _C_U:_C_GA])
    cu = ucb[:, 0:CONV_WIDTH] * ucb[:, CONV_WIDTH:2 * CONV_WIDTH]
    b_gate = ucb[:, 2 * CONV_WIDTH:3 * CONV_WIDTH]

    @pl.when(i % tiles_per_seq == 0)
    def _():
        carry_ref[...] = jnp.zeros_like(carry_ref)

    prev = carry_ref[...]
    row = lax.broadcasted_iota(jnp.int32, cu.shape, 0)
    cu1 = jnp.where(row == 0, prev[7:8], pltpu.roll(cu, 1, 0))
    cu2 = jnp.where(row == 0, prev[6:7], jnp.where(row == 1, prev[7:8], pltpu.roll(cu, 2, 0)))
    cw = cw_ref[...]
    z = cw[2:3] * cu + cw[1:2] * cu1 + cw[0:1] * cu2
    carry_ref[...] = cu[tm - 8:tm]
    c_branch = _dot((b_gate * z).astype(BF16), wuc_ref[...])

    gates = _dot(hb, win_ref[:, _C_GA:_C_END])
    sga_ref[...] = jax.nn.sigmoid(gates[:, 0:D_MODEL]).astype(BF16)
    gc_ref[...] = (jax.nn.sigmoid(gates[:, D_MODEL:]) * c_branch).astype(BF16)


def _pre(x2, mod, norm_g, win2, q_norm_g, wq2, wq2s, kv_norm_g, wk2, wv, cosf, sinf, conv_w,
         w_up_conv, seq):
    t = x2.shape[0]
    tm = ROW_TILE
    tiles_per_seq = seq // tm
    full = lambda a: pl.BlockSpec(a.shape, lambda i: (0,) * a.ndim)
    rows = lambda w: pl.BlockSpec((tm, w), lambda i: (i, 0))
    outs = [jax.ShapeDtypeStruct((t, N_HEADS * HEAD_PAD), BF16),
            jax.ShapeDtypeStruct((t, N_HEADS * HEAD_PAD), BF16),
            jax.ShapeDtypeStruct((t, N_HEADS * HEAD_PAD), BF16),
            jax.ShapeDtypeStruct((t, D_MODEL), BF16),
            jax.ShapeDtypeStruct((t, D_MODEL), BF16)]
    return pl.pallas_call(
        functools.partial(_pre_kernel, tiles_per_seq),
        out_shape=outs,
        grid=(t // tm,),
        in_specs=[
            rows(D_MODEL),
            pl.BlockSpec((None, 8, D_MODEL), lambda i: (i // tiles_per_seq, 0, 0)),
            full(norm_g), full(win2), full(q_norm_g), full(wq2), full(wq2s),
            full(kv_norm_g), full(wk2), full(wv),
            rows(HEAD_PAD), rows(HEAD_PAD), full(conv_w), full(w_up_conv),
        ],
        out_specs=[rows(N_HEADS * HEAD_PAD), rows(N_HEADS * HEAD_PAD), rows(N_HEADS * HEAD_PAD),
                   rows(D_MODEL), rows(D_MODEL)],
        scratch_shapes=[pltpu.VMEM((8, CONV_WIDTH), F32)],
        compiler_params=pltpu.CompilerParams(
            dimension_semantics=("arbitrary",), vmem_limit_bytes=VMEM_LIMIT),
        name="pre_mixer",
    )(x2, mod, norm_g, win2, q_norm_g, wq2, wq2s, kv_norm_g, wk2, wv, cosf, sinf, conv_w,
      w_up_conv)


def _attn_kernel(q_ref, k_ref, v_ref, o_ref, m_ref, acc_ref):
    i = pl.program_id(1)
    tq = q_ref.shape[0]
    tk = tq
    c_exp = (QK_HEAD ** -0.5) * math.log2(math.e)

    m_ref[...] = jnp.full_like(m_ref, NEG_BIG)
    acc_ref[...] = jnp.zeros_like(acc_ref)

    def step(j, masked):
        k0 = pl.multiple_of(j * tk, tk)
        if masked:
            rq = lax.broadcasted_iota(jnp.int32, (tq, tk), 0) // CHUNK
            ck = lax.broadcasted_iota(jnp.int32, (tq, tk), 1) // CHUNK
            allowed = ck <= rq
        for hd in range(N_HEADS):
            hs = slice(hd * HEAD_PAD, (hd + 1) * HEAD_PAD)
            s = lax.dot_general(q_ref[:, hs], k_ref[pl.ds(k0, tk), hs],
                                (((1,), (1,)), ((), ())), preferred_element_type=F32)
            if masked:
                s = jnp.where(allowed, s, NEG_BIG)
            m_old = m_ref[hd]
            s_max = s[:, 0:LANES]
            for c in range(1, tk // LANES):
                s_max = jnp.maximum(s_max, s[:, c * LANES:(c + 1) * LANES])
            m_new = jnp.maximum(m_old, jnp.max(s_max, axis=-1, keepdims=True))
            alpha = jnp.exp2((m_old - m_new) * c_exp)
            p = jnp.concatenate(
                [jnp.exp2((s[:, c * LANES:(c + 1) * LANES] - m_new) * c_exp).astype(BF16)
                 for c in range(tk // LANES)], axis=-1)
            acc_ref[hd] = alpha * acc_ref[hd] + _dot(p, v_ref[pl.ds(k0, tk), hs])
            m_ref[hd] = m_new

    def body(j, carry):
        step(j, False)
        return carry

    lax.fori_loop(0, i, body, 0)
    step(i, True)

    for hp in range(N_HEADS // 2):
        pair = []
        for hd in (2 * hp, 2 * hp + 1):
            acc = acc_ref[hd]
            pair.append(acc[:, 0:V_HEAD] / acc[:, V_HEAD:V_HEAD + 1])
        o_ref[:, hp * LANES:(hp + 1) * LANES] = jnp.concatenate(pair, axis=-1).astype(BF16)


def _attention(q, k, v, batch, seq):
    tq = ATT_BLOCK
    nq = seq // tq
    return pl.pallas_call(
        _attn_kernel,
        out_shape=jax.ShapeDtypeStruct((batch * seq, N_HEADS * V_HEAD), BF16),
        grid=(batch, nq),
        in_specs=[
            pl.BlockSpec((tq, N_HEADS * HEAD_PAD), lambda b, i: (b * nq + i, 0)),
            pl.BlockSpec((seq, N_HEADS * HEAD_PAD), lambda b, i: (b, 0)),
            pl.BlockSpec((seq, N_HEADS * HEAD_PAD), lambda b, i: (b, 0)),
        ],
        out_specs=pl.BlockSpec((tq, N_HEADS * V_HEAD), lambda b, i: (b * nq + i, 0)),
        scratch_shapes=[pltpu.VMEM((N_HEADS, tq, LANES), F32),
                        pltpu.VMEM((N_HEADS, tq, LANES), F32)],
        compiler_params=pltpu.CompilerParams(
            dimension_semantics=("arbitrary", "arbitrary"), vmem_limit_bytes=VMEM_LIMIT),
        name="attention",
    )(q, k, v)


def _post_kernel(attn_ref, sga_ref, gc_ref, x_ref, mod_ref, wua_ref, wo_ref, g_ref, rwh_ref,
                 rwl_ref, rb_ref, x1_ref, h2_ref, idx_ref, gate_ref, rank_ref, cnt_out_ref, cnt_ref):
    mod = mod_ref[...]
    a_branch = _dot(attn_ref[...], wua_ref[...])
    merged = sga_ref[...].astype(F32) * a_branch + gc_ref[...].astype(F32)
    mix = _dot(merged.astype(BF16), wo_ref[...])
    x1 = x_ref[...] + mod[2:3] * mix
    x1_ref[...] = x1
    h2 = _rms(x1, g_ref[...]) * (1.0 + mod[4:5]) + mod[3:4]
    h2_ref[...] = h2

    h_hi = h2.astype(BF16)
    h_lo = (h2 - h_hi.astype(F32)).astype(BF16)
    logits = (_dot(h_hi, rwh_ref[...]) + _dot(h_lo, rwh_ref[...]) + _dot(h_hi, rwl_ref[...])
              + rb_ref[...])
    lane = lax.broadcasted_iota(jnp.int32, logits.shape, 1)
    work = logits
    vals, idxs = [], []
    for _ in range(TOP_K):
        mk = jnp.max(work, axis=-1, keepdims=True)
        ik = jnp.min(jnp.where(work == mk, lane, LANES), axis=-1, keepdims=True)
        vals.append(mk)
        idxs.append(ik)
        work = jnp.where(lane == ik, -jnp.inf, work)
    es = [jnp.exp(vk - vals[0]) for vk in vals]
    denom = es[0] + es[1] + es[2] + es[3]
    @pl.when(pl.program_id(0) == 0)
    def _():
        cnt_ref[...] = jnp.zeros_like(cnt_ref)

    tm = logits.shape[0]
    chosen = jnp.zeros(logits.shape, F32)
    for kk in range(TOP_K):
        chosen = chosen + jnp.where(lane == idxs[kk], 1.0, 0.0)
    r_i = lax.broadcasted_iota(jnp.int32, (tm, tm), 0)
    c_i = lax.broadcasted_iota(jnp.int32, (tm, tm), 1)
    earlier = jnp.where(c_i < r_i, 1.0, 0.0).astype(BF16)
    before = _dot(earlier, chosen.astype(BF16)) + cnt_ref[0:1]
    cnt_new = cnt_ref[...] + jnp.sum(chosen, axis=0, keepdims=True)
    cnt_ref[...] = cnt_new
    cnt_out_ref[...] = cnt_new.astype(jnp.int32)

    idx_out = jnp.zeros(logits.shape, jnp.int32)
    gate_out = jnp.zeros(logits.shape, F32)
    rank_out = jnp.zeros(logits.shape, jnp.int32)
    for kk in range(TOP_K):
        rank_k = jnp.sum(jnp.where(lane == idxs[kk], before, 0.0), axis=-1, keepdims=True)
        idx_out = jnp.where(lane == kk, idxs[kk], idx_out)
        gate_out = jnp.where(lane == kk, es[kk] / denom, gate_out)
        rank_out = jnp.where(lane == kk, rank_k.astype(jnp.int32), rank_out)
    idx_ref[...] = idx_out
    gate_ref[...] = gate_out
    rank_ref[...] = rank_out


def _post(attn, sga, gc, x2, mod, wua, wo, norm_g, rw_hi, rw_lo, rb_pad, seq):
    t = x2.shape[0]
    tm = ROW_TILE
    tiles_per_seq = seq // tm
    full = lambda a: pl.BlockSpec(a.shape, lambda i: (0,) * a.ndim)
    rows = lambda w: pl.BlockSpec((tm, w), lambda i: (i, 0))
    outs = [jax.ShapeDtypeStruct((t, D_MODEL), F32),
            jax.ShapeDtypeStruct((t, D_MODEL), F32),
            jax.ShapeDtypeStruct((t, LANES), jnp.int32),
            jax.ShapeDtypeStruct((t, LANES), F32),
            jax.ShapeDtypeStruct((t, LANES), jnp.int32),
            jax.ShapeDtypeStruct((8, LANES), jnp.int32)]
    return pl.pallas_call(
        _post_kernel,
        out_shape=outs,
        grid=(t // tm,),
        in_specs=[
            rows(N_HEADS * V_HEAD), rows(D_MODEL), rows(D_MODEL), rows(D_MODEL),
            pl.BlockSpec((None, 8, D_MODEL), lambda i: (i // tiles_per_seq, 0, 0)),
            full(wua), full(wo), full(norm_g), full(rw_hi), full(rw_lo), full(rb_pad),
        ],
        out_specs=[rows(D_MODEL), rows(D_MODEL), rows(LANES), rows(LANES), rows(LANES),
                   pl.BlockSpec((8, LANES), lambda i: (0, 0))],
        scratch_shapes=[pltpu.VMEM((8, LANES), F32)],
        compiler_params=pltpu.CompilerParams(
            dimension_semantics=("arbitrary",), vmem_limit_bytes=VMEM_LIMIT),
        name="post_mixer",
    )(attn, sga, gc, x2, mod, wua, wo, norm_g, rw_hi, rw_lo, rb_pad)


_TB_EXPERT, _TB_VALID, _TB_FIRST, _TB_NEXT, _TB_SLOT = range(5)


def _moe_kernel(tb_ref, xs_ref, wgu_hbm, bgu_ref, wd_hbm, bd_ref, o_ref,
                wgu_f, wd_f, wgu_bf, wd_bf, sem):
    b = pl.program_id(0)
    n_valid = tb_ref[_TB_VALID, b]
    used = n_valid > 0
    slot = tb_ref[_TB_SLOT, b]

    def weight_copies(expert, sl):
        return (pltpu.make_async_copy(wgu_hbm.at[expert], wgu_f.at[sl], sem.at[0, sl]),
                pltpu.make_async_copy(wd_hbm.at[expert], wd_f.at[sl], sem.at[1, sl]))

    @pl.when(b == 0)
    def _():
        for cp in weight_copies(tb_ref[_TB_EXPERT, 0], 0):
            cp.start()

    @pl.when(tb_ref[_TB_FIRST, b] == 1)
    def _():
        nxt = tb_ref[_TB_NEXT, b]

        @pl.when(nxt >= 0)
        def _():
            for cp in weight_copies(nxt, 1 - slot):
                cp.start()

        for cp in weight_copies(tb_ref[_TB_EXPERT, b], slot):
            cp.wait()
        wgu_bf[...] = wgu_f[slot].astype(BF16)
        wd_bf[...] = wd_f[slot].astype(BF16)

    @pl.when(used)
    def _():
        row = lax.broadcasted_iota(jnp.int32, xs_ref.shape, 0)
        xs = jnp.where(row < n_valid, xs_ref[...], 0.0).astype(BF16)
        gu = _dot(xs, wgu_bf[...]) + bgu_ref[...]
        gate = jnp.minimum(gu[:, :D_EXPERT], SWIGLU_LIMIT)
        up = jnp.clip(gu[:, D_EXPERT:], -SWIGLU_LIMIT, SWIGLU_LIMIT)
        act = (up + 1.0) * (gate * jax.nn.sigmoid(gate * SWIGLU_ALPHA))
        o_ref[...] = _dot(act.astype(BF16), wd_bf[...]) + bd_ref[...]

    @pl.when(jnp.logical_not(used))
    def _():
        o_ref[...] = jnp.zeros_like(o_ref)


def _moe(block_table, xs, w_gu, b_gu, w_down, b_down):
    n_rows = xs.shape[0]
    n_blocks = n_rows // MOE_BLOCK
    grid_spec = pltpu.PrefetchScalarGridSpec(
        num_scalar_prefetch=1,
        grid=(n_blocks,),
        in_specs=[
            pl.BlockSpec((MOE_BLOCK, D_MODEL), lambda b, tb: (b, 0)),
            pl.BlockSpec(memory_space=pl.ANY),
            pl.BlockSpec((None, 1, 2 * D_EXPERT), lambda b, tb: (tb[_TB_EXPERT, b], 0, 0)),
            pl.BlockSpec(memory_space=pl.ANY),
            pl.BlockSpec((None, 1, D_MODEL), lambda b, tb: (tb[_TB_EXPERT, b], 0, 0)),
        ],
        out_specs=pl.BlockSpec((MOE_BLOCK, D_MODEL), lambda b, tb: (b, 0)),
        scratch_shapes=[pltpu.VMEM((2, D_MODEL, 2 * D_EXPERT), F32),
                        pltpu.VMEM((2, D_EXPERT, D_MODEL), F32),
                        pltpu.VMEM((D_MODEL, 2 * D_EXPERT), BF16),
                        pltpu.VMEM((D_EXPERT, D_MODEL), BF16),
                        pltpu.SemaphoreType.DMA((2, 2))],
    )
    return pl.pallas_call(
        _moe_kernel,
        out_shape=jax.ShapeDtypeStruct((n_rows, D_MODEL), F32),
        grid_spec=grid_spec,
        compiler_params=pltpu.CompilerParams(
            dimension_semantics=("arbitrary",), vmem_limit_bytes=VMEM_LIMIT),
        name="moe_experts",
    )(block_table, xs, w_gu, b_gu, w_down, b_down)


def _final_kernel(last_layer, x1_ref, y_ref, gate_ref, mod_ref, g_ref, o_ref):
    mod = mod_ref[...]
    gate = gate_ref[...]
    ffn = gate[:, 0:1] * y_ref[0]
    for kk in range(1, TOP_K):
        ffn = ffn + gate[:, kk:kk + 1] * y_ref[kk]
    x = x1_ref[...] + mod[5:6] * ffn
    o_ref[...] = _rms(x, g_ref[...]) if last_layer else x


def _final(x1, y_kt, gate, mod, norm_g, seq, last_layer):
    t = x1.shape[0]
    tm = ROW_TILE
    tiles_per_seq = seq // tm
    rows = lambda w: pl.BlockSpec((tm, w), lambda i: (i, 0))
    return pl.pallas_call(
        functools.partial(_final_kernel, last_layer),
        out_shape=jax.ShapeDtypeStruct((t, D_MODEL), F32),
        grid=(t // tm,),
        in_specs=[
            rows(D_MODEL), pl.BlockSpec((TOP_K, tm, D_MODEL), lambda i: (0, i, 0)), rows(LANES),
            pl.BlockSpec((None, 8, D_MODEL), lambda i: (i // tiles_per_seq, 0, 0)),
            pl.BlockSpec(norm_g.shape, lambda i: (0, 0)),
        ],
        out_specs=rows(D_MODEL),
        compiler_params=pltpu.CompilerParams(
            dimension_semantics=("arbitrary",), vmem_limit_bytes=VMEM_LIMIT),
        name="combine_final",
    )(x1, y_kt, gate, mod, norm_g)


def _swap_halves(w):
    half = w.shape[-1] // 2
    return jnp.concatenate([w[..., half:], w[..., :half]], axis=-1)


def _prep_weights(w_in, w_uq, w_ukv):
    d = w_in.shape[0]
    splits = (Q_LORA, KV_LORA, QK_ROPE, CONV_WIDTH, CONV_WIDTH, CONV_WIDTH, D_MODEL, D_MODEL)
    offs = [0]
    for s in splits:
        offs.append(offs[-1] + s)
    part = lambda n: w_in[:, offs[n]:offs[n + 1]]
    z = lambda n: jnp.zeros((d, n), w_in.dtype)
    w_kpe = part(2)
    kpe_a = jnp.concatenate([z(QK_NOPE), w_kpe, z(HEAD_PAD - QK_HEAD)], axis=1)
    kpe_b = jnp.concatenate([z(QK_NOPE), _swap_halves(w_kpe), z(HEAD_PAD - QK_HEAD)], axis=1)
    win2 = jnp.concatenate([part(0), part(1), kpe_a, kpe_b, part(3), part(4), part(5), part(6),
                            part(7)], axis=1).astype(BF16)

    wq = w_uq.reshape(Q_LORA, N_HEADS, QK_HEAD)
    zq = lambda n: jnp.zeros((Q_LORA, N_HEADS, n), w_uq.dtype)
    wq2 = jnp.concatenate([wq, zq(HEAD_PAD - QK_HEAD)], axis=-1)
    wq2s = jnp.concatenate([zq(QK_NOPE), _swap_halves(wq[..., QK_NOPE:]), zq(HEAD_PAD - QK_HEAD)],
                           axis=-1)
    wq2 = wq2.reshape(Q_LORA, N_HEADS * HEAD_PAD).astype(BF16)
    wq2s = wq2s.reshape(Q_LORA, N_HEADS * HEAD_PAD).astype(BF16)

    wkv = w_ukv.reshape(KV_LORA, N_HEADS, QK_NOPE + V_HEAD)
    wk2 = jnp.concatenate([wkv[..., :QK_NOPE],
                           jnp.zeros((KV_LORA, N_HEADS, HEAD_PAD - QK_NOPE), w_ukv.dtype)], axis=-1)
    wk2 = wk2.reshape(KV_LORA, N_HEADS * HEAD_PAD).astype(BF16)
    wv = jnp.concatenate([wkv[..., QK_NOPE:],
                          jnp.zeros((KV_LORA, N_HEADS, HEAD_PAD - V_HEAD), w_ukv.dtype)], axis=-1)
    wv = wv.reshape(KV_LORA, N_HEADS * HEAD_PAD).astype(BF16)
    return win2, wq2, wq2s, wk2, wv


def _rope_tables(positions):
    inv_freq = 1.0 / (ROPE_THETA ** (jnp.arange(0, QK_ROPE, 2, dtype=F32) / QK_ROPE))
    ang = positions.astype(F32).reshape(-1, 1) * inv_freq
    cos, sin = jnp.cos(ang), jnp.sin(ang)
    t = ang.shape[0]
    ones = jnp.ones((t, QK_NOPE), F32)
    zpad = jnp.zeros((t, HEAD_PAD - QK_HEAD), F32)
    cosf = jnp.concatenate([ones, cos, cos, zpad], axis=1)
    sinf = jnp.concatenate([jnp.zeros((t, QK_NOPE), F32), -sin, sin, zpad], axis=1)
    return cosf, sinf


def _route(top_idx, rank, counts, n_tokens):
    padded = (counts + MOE_BLOCK - 1) // MOE_BLOCK * MOE_BLOCK
    pad_end = jnp.cumsum(padded)
    pad_start = pad_end - padded
    experts = jnp.arange(N_EXPERTS, dtype=jnp.int32)
    start_of = jnp.sum(jnp.where(top_idx[:, :, None] == experts, pad_start, 0), axis=-1)
    dest = start_of + rank
    n_rows = n_tokens * TOP_K + N_EXPERTS * MOE_BLOCK
    n_blocks = n_rows // MOE_BLOCK
    block_start = jnp.arange(n_blocks, dtype=jnp.int32) * MOE_BLOCK
    block_expert = jnp.minimum(
        (pad_end[None, :] <= block_start[:, None]).astype(jnp.int32).sum(axis=1), N_EXPERTS - 1)
    group_end = (pad_start + counts)[block_expert]
    block_valid = jnp.clip(group_end - block_start, 0, MOE_BLOCK)
    blk = jnp.arange(n_blocks, dtype=jnp.int32)
    prev_expert = jnp.concatenate([jnp.full((1,), -1, jnp.int32), block_expert[:-1]])
    first = jnp.logical_and(block_valid > 0, block_expert != prev_expert)
    later_first = jnp.logical_and(first[None, :], blk[None, :] > blk[:, None])
    next_pos = jnp.min(jnp.where(later_first, blk[None, :], n_blocks), axis=1)
    next_expert = jnp.where(next_pos < n_blocks,
                            block_expert[jnp.minimum(next_pos, n_blocks - 1)], -1)
    slot = (jnp.cumsum(first.astype(jnp.int32)) - 1) % 2
    table = jnp.stack([block_expert, block_valid, first.astype(jnp.int32), next_expert, slot])
    return dest.astype(jnp.int32), table.astype(jnp.int32), n_rows


SC_CORES = 2
SC_SUBCORES = 16
SC_WORKERS = SC_CORES * SC_SUBCORES
SC_CHUNK = 64


def _sc_mesh():
    return plsc.VectorSubcoreMesh(core_axis_name="c", subcore_axis_name="s")


def _sc_worker():
    return lax.axis_index("s") * SC_CORES + lax.axis_index("c")


def _dispatch(h2, dest, n_rows):
    t, d = h2.shape
    per_w = t // SC_WORKERS
    n_chunks = per_w // SC_CHUNK
    idx = dest.reshape(SC_WORKERS, n_chunks, SC_CHUNK, TOP_K).transpose(0, 3, 1, 2)
    idx = idx.reshape(SC_WORKERS, TOP_K * n_chunks, SC_CHUNK)

    @functools.partial(
        pl.kernel, mesh=_sc_mesh(),
        out_type=jax.ShapeDtypeStruct((n_rows, d), h2.dtype),
        scratch_types=[pltpu.VMEM((TOP_K * n_chunks, SC_CHUNK), jnp.int32),
                       pltpu.VMEM((SC_CHUNK, d), h2.dtype)],
        name="moe_dispatch")
    def run(h2_hbm, idx_hbm, xs_hbm, idx_v, rows_v):
        w = _sc_worker()
        pltpu.sync_copy(idx_hbm.at[w], idx_v)

        @pl.loop(0, n_chunks)
        def _(g):
            pltpu.sync_copy(h2_hbm.at[pl.ds(w * per_w + g * SC_CHUNK, SC_CHUNK)], rows_v)
            for kk in range(TOP_K):
                pltpu.sync_copy(rows_v, xs_hbm.at[idx_v.at[kk * n_chunks + g]])

    return run(h2, idx)


def _undispatch(ys, dest):
    t = dest.shape[0]
    d = ys.shape[1]
    n_out = t * TOP_K
    per_w = n_out // SC_WORKERS
    n_chunks = per_w // SC_CHUNK
    idx = dest.T.reshape(SC_WORKERS, n_chunks, SC_CHUNK)

    @functools.partial(
        pl.kernel, mesh=_sc_mesh(),
        out_type=jax.ShapeDtypeStruct((n_out, d), ys.dtype),
        scratch_types=[pltpu.VMEM((n_chunks, SC_CHUNK), jnp.int32),
                       pltpu.VMEM((SC_CHUNK, d), ys.dtype),
                       pltpu.SemaphoreType.DMA],
        name="moe_undispatch")
    def run(ys_hbm, idx_hbm, out_hbm, idx_v, rows_v, sem):
        w = _sc_worker()
        pltpu.sync_copy(idx_hbm.at[w], idx_v)

        @pl.loop(0, n_chunks)
        def _(g):
            pltpu.async_copy(ys_hbm.at[idx_v.at[g]], rows_v, sem).wait()
            pltpu.sync_copy(rows_v, out_hbm.at[pl.ds(w * per_w + g * SC_CHUNK, SC_CHUNK)])

    return run(ys, idx).reshape(TOP_K, t, d)


def kernel(x, c, positions, w_ada, b_ada, norm_mix_g, w_in, q_norm_g, w_uq, kv_norm_g, w_ukv,
           w_up_attn, conv_w, w_up_conv, w_o, norm_ffn_g, router_w, router_b, w_gu, b_gu,
           w_down, b_down, norm_final_g):
    batch, seq, d = x.shape
    t = batch * seq
    depth = w_ada.shape[0]
    x2 = x.reshape(t, d)
    cosf, sinf = _rope_tables(positions)
    c_pad = jnp.zeros((8, d), F32).at[:batch].set(c)

    for l in range(depth):
        ada = _ada(c_pad, w_ada[l], b_ada[l].reshape(1, -1))
        mod = ada[:batch].reshape(batch, 6, d)
        mod = jnp.concatenate([mod, jnp.zeros((batch, 2, d), F32)], axis=1)

        win2, wq2, wq2s, wk2, wv = _prep_weights(w_in[l], w_uq[l], w_ukv[l])
        q, k, v, sga, gc = _pre(x2, mod, norm_mix_g[l].reshape(1, d), win2,
                                q_norm_g[l].reshape(1, -1), wq2, wq2s,
                                kv_norm_g[l].reshape(1, -1), wk2, wv, cosf, sinf, conv_w[l],
                                w_up_conv[l].astype(BF16), seq)
        attn = _attention(q, k, v, batch, seq)

        rw_pad = jnp.concatenate([router_w[l], jnp.zeros((d, LANES - N_EXPERTS), F32)], axis=1)
        rb_pad = jnp.concatenate([router_b[l], jnp.full((LANES - N_EXPERTS,), NEG_BIG, F32)])
        rw_hi = rw_pad.astype(BF16)
        rw_lo = (rw_pad - rw_hi.astype(F32)).astype(BF16)
        x1, h2, idx_pad, gate_pad, rank_pad, counts = _post(
            attn, sga, gc, x2, mod, w_up_attn[l].astype(BF16), w_o[l].astype(BF16),
            norm_ffn_g[l].reshape(1, d), rw_hi, rw_lo, rb_pad.reshape(1, LANES), seq)

        dest, block_table, n_rows = _route(
            idx_pad[:, :TOP_K], rank_pad[:, :TOP_K], counts[0, :N_EXPERTS], t)
        xs = _dispatch(h2, dest, n_rows)
        ys = _moe(block_table, xs, w_gu[l], b_gu[l].reshape(N_EXPERTS, 1, -1),
                  w_down[l], b_down[l].reshape(N_EXPERTS, 1, -1))
        y_kt = _undispatch(ys, dest)
        x2 = _final(x1, y_kt, gate_pad, mod, norm_final_g.reshape(1, d), seq, l == depth - 1)

    return x2.reshape(batch, seq, d)
```

```python
import functools
import math

import jax
import jax.numpy as jnp
from jax import lax
from jax.experimental import pallas as pl
from jax.experimental.pallas import tpu as pltpu
from jax.experimental.pallas import tpu_sc as plsc

D_MODEL = 1024
CHUNK = 64
N_HEADS = 8
Q_LORA = 256
KV_LORA = 128
QK_NOPE = 64
QK_ROPE = 32
V_HEAD = 64
QK_HEAD = QK_NOPE + QK_ROPE
ROPE_THETA = 10000.0
CONV_WIDTH = 512
CONV_K = 3
N_EXPERTS = 32
TOP_K = 4
D_EXPERT = 1024
SWIGLU_LIMIT = 7.0
SWIGLU_ALPHA = 1.702
MOE_BLOCK = 256
RMS_EPS = 1e-6

LANES = 128
HEAD_PAD = 128
NEG_BIG = -1e30
VMEM_LIMIT = 56 * 1024 * 1024

F32 = jnp.float32
BF16 = jnp.bfloat16

Q_PRESCALE = (QK_HEAD ** -0.5) * math.log2(math.e)

ROW_TILE = 512
ATT_BLOCK = 512
ATT_WIDE = 2


def _rms(x, g):
    ms = jnp.mean(x * x, axis=-1, keepdims=True)
    return x * lax.rsqrt(ms + RMS_EPS) * g


def _dot(a, b):
    return jnp.dot(a, b, preferred_element_type=F32)


def _ada_kernel(c_ref, w_ref, b_ref, o_ref):
    c = c_ref[...]
    ca = (c * jax.nn.sigmoid(c)).astype(BF16)
    o_ref[...] = _dot(ca, w_ref[...].astype(BF16)) + b_ref[...]


def _ada(c_pad, w_ada, b_ada):
    n = w_ada.shape[1]
    tn = 1024
    return pl.pallas_call(
        _ada_kernel,
        out_shape=jax.ShapeDtypeStruct((c_pad.shape[0], n), F32),
        grid=(n // tn,),
        in_specs=[
            pl.BlockSpec(c_pad.shape, lambda j: (0, 0)),
            pl.BlockSpec((D_MODEL, tn), lambda j: (0, j)),
            pl.BlockSpec((1, tn), lambda j: (0, j)),
        ],
        out_specs=pl.BlockSpec((c_pad.shape[0], tn), lambda j: (0, j)),
        compiler_params=pltpu.CompilerParams(
            dimension_semantics=("arbitrary",), vmem_limit_bytes=VMEM_LIMIT),
        name="ada",
    )(c_pad, w_ada, b_ada)


_C_QLAT = 0
_C_KVLAT = _C_QLAT + Q_LORA
_C_KPE_A = _C_KVLAT + KV_LORA
_C_KPE_B = _C_KPE_A + HEAD_PAD
_C_U = _C_KPE_B + HEAD_PAD
_C_C = _C_U + CONV_WIDTH
_C_B = _C_C + CONV_WIDTH
_C_GA = _C_B + CONV_WIDTH
_C_GC = _C_GA + D_MODEL
_C_END = _C_GC + D_MODEL


def _pre_kernel(tiles_per_seq, x_ref, mod_ref, g_ref, win_ref, qg_ref, wq_ref, wqs_ref,
                kvg_ref, wk_ref, wv_ref, cos_ref, sin_ref, cw_ref, wuc_ref,
                q_ref, k_ref, v_ref, sga_ref, gc_ref, carry_ref):
    i = pl.program_id(0)
    tm = x_ref.shape[0]
    mod = mod_ref[...]
    h = _rms(x_ref[...], g_ref[...]) * (1.0 + mod[1:2]) + mod[0:1]
    hb = h.astype(BF16)

    cosf = cos_ref[...]
    sinf = sin_ref[...]
    cos8 = jnp.concatenate([cosf] * N_HEADS, axis=-1)
    sin8 = jnp.concatenate([sinf] * N_HEADS, axis=-1)

    small = _dot(hb, win_ref[:, _C_QLAT:_C_U])
    q_lat = small[:, _C_QLAT:_C_KVLAT]
    kv_lat = small[:, _C_KVLAT:_C_KPE_A]
    kpe = small[:, _C_KPE_A:_C_KPE_B] * cosf + small[:, _C_KPE_B:_C_U] * sinf
    qn = _rms(q_lat, qg_ref[...]).astype(BF16)
    q = _dot(qn, wq_ref[...]) * cos8 + _dot(qn, wqs_ref[...]) * sin8
    q_ref[...] = (q * Q_PRESCALE).astype(BF16)
    kvn = _rms(kv_lat, kvg_ref[...]).astype(BF16)
    k = _dot(kvn, wk_ref[...]) + jnp.concatenate([kpe] * N_HEADS, axis=-1)
    k_ref[...] = k.astype(BF16)
    lane = lax.broadcasted_iota(jnp.int32, (tm, N_HEADS * HEAD_PAD), 1)
    ones_col = jnp.where(lane % HEAD_PAD == V_HEAD, 1.0, 0.0)
    v_ref[...] = (_dot(kvn, wv_ref[...]) + ones_col).astype(BF16)

    ucb = _dot(hb, win_ref[:, _C_U:_C_GA])
    cu = ucb[:, 0:CONV_WIDTH] * ucb[:, CONV_WIDTH:2 * CONV_WIDTH]
    b_gate = ucb[:, 2 * CONV_WIDTH:3 * CONV_WIDTH]

    @pl.when(i % tiles_per_seq == 0)
    def _():
        carry_ref[...] = jnp.zeros_like(carry_ref)

    prev = carry_ref[...]
    row = lax.broadcasted_iota(jnp.int32, cu.shape, 0)
    cu1 = jnp.where(row == 0, prev[7:8], pltpu.roll(cu, 1, 0))
    cu2 = jnp.where(row == 0, prev[6:7], jnp.where(row == 1, prev[7:8], pltpu.roll(cu, 2, 0)))
    cw = cw_ref[...]
    z = cw[2:3] * cu + cw[1:2] * cu1 + cw[0:1] * cu2
    carry_ref[...] = cu[tm - 8:tm]
    c_branch = _dot((b_gate * z).astype(BF16), wuc_ref[...])

    gates = _dot(hb, win_ref[:, _C_GA:_C_END])
    sga_ref[...] = jax.nn.sigmoid(gates[:, 0:D_MODEL]).astype(BF16)
    gc_ref[...] = (jax.nn.sigmoid(gates[:, D_MODEL:]) * c_branch).astype(BF16)


def _pre(x2, mod, norm_g, win2, q_norm_g, wq2, wq2s, kv_norm_g, wk2, wv, cosf, sinf, conv_w,
         w_up_conv, seq):
    t = x2.shape[0]
    tm = ROW_TILE
    tiles_per_seq = seq // tm
    full = lambda a: pl.BlockSpec(a.shape, lambda i: (0,) * a.ndim)
    rows = lambda w: pl.BlockSpec((tm, w), lambda i: (i, 0))
    outs = [jax.ShapeDtypeStruct((t, N_HEADS * HEAD_PAD), BF16),
            jax.ShapeDtypeStruct((t, N_HEADS * HEAD_PAD), BF16),
            jax.ShapeDtypeStruct((t, N_HEADS * HEAD_PAD), BF16),
            jax.ShapeDtypeStruct((t, D_MODEL), BF16),
            jax.ShapeDtypeStruct((t, D_MODEL), BF16)]
    return pl.pallas_call(
        functools.partial(_pre_kernel, tiles_per_seq),
        out_shape=outs,
        grid=(t // tm,),
        in_specs=[
            rows(D_MODEL),
            pl.BlockSpec((None, 8, D_MODEL), lambda i: (i // tiles_per_seq, 0, 0)),
            full(norm_g), full(win2), full(q_norm_g), full(wq2), full(wq2s),
            full(kv_norm_g), full(wk2), full(wv),
            rows(HEAD_PAD), rows(HEAD_PAD), full(conv_w), full(w_up_conv),
        ],
        out_specs=[rows(N_HEADS * HEAD_PAD), rows(N_HEADS * HEAD_PAD), rows(N_HEADS * HEAD_PAD),
                   rows(D_MODEL), rows(D_MODEL)],
        scratch_shapes=[pltpu.VMEM((8, CONV_WIDTH), F32)],
        compiler_params=pltpu.CompilerParams(
            dimension_semantics=("arbitrary",), vmem_limit_bytes=VMEM_LIMIT),
        name="pre_mixer",
    )(x2, mod, norm_g, win2, q_norm_g, wq2, wq2s, kv_norm_g, wk2, wv, cosf, sinf, conv_w,
      w_up_conv)


def _attn_kernel(q_ref, k_ref, v_ref, o_ref, m_ref, acc_ref):
    i = pl.program_id(1)
    tq = q_ref.shape[0]

    m_ref[...] = jnp.full_like(m_ref, NEG_BIG)
    acc_ref[...] = jnp.zeros_like(acc_ref)

    def step(k0, tk, masked):
        if masked:
            rq = lax.broadcasted_iota(jnp.int32, (tq, tk), 0) // CHUNK
            ck = lax.broadcasted_iota(jnp.int32, (tq, tk), 1) // CHUNK
            allowed = ck <= rq
        for hd in range(N_HEADS):
            hs = slice(hd * HEAD_PAD, (hd + 1) * HEAD_PAD)
            s = lax.dot_general(q_ref[:, hs], k_ref[pl.ds(k0, tk), hs],
                                (((1,), (1,)), ((), ())), preferred_element_type=F32)
            if masked:
                s = jnp.where(allowed, s, NEG_BIG)
            m_old = m_ref[hd]
            s_max = s[:, 0:LANES]
            for c in range(1, tk // LANES):
                s_max = jnp.maximum(s_max, s[:, c * LANES:(c + 1) * LANES])
            m_new = jnp.maximum(m_old, jnp.max(s_max, axis=-1, keepdims=True))
            alpha = jnp.exp2(m_old - m_new)
            p = jnp.concatenate(
                [jnp.exp2(s[:, c * LANES:(c + 1) * LANES] - m_new).astype(BF16)
                 for c in range(tk // LANES)], axis=-1)
            acc_ref[hd] = alpha * acc_ref[hd] + _dot(p, v_ref[pl.ds(k0, tk), hs])
            m_ref[hd] = m_new

    wide = ATT_WIDE * tq

    def body(j, carry):
        step(pl.multiple_of(j * wide, wide), wide, False)
        return carry

    lax.fori_loop(0, i // ATT_WIDE, body, 0)

    for r in range(1, ATT_WIDE):
        @pl.when(i % ATT_WIDE >= r)
        def _():
            step(pl.multiple_of((i - i % ATT_WIDE + r - 1) * tq, tq), tq, False)

    step(pl.multiple_of(i * tq, tq), tq, True)

    for hp in range(N_HEADS // 2):
        pair = []
        for hd in (2 * hp, 2 * hp + 1):
            acc = acc_ref[hd]
            pair.append(acc[:, 0:V_HEAD] / acc[:, V_HEAD:V_HEAD + 1])
        o_ref[:, hp * LANES:(hp + 1) * LANES] = jnp.concatenate(pair, axis=-1).astype(BF16)


def _attention(q, k, v, batch, seq):
    tq = ATT_BLOCK
    nq = seq // tq
    return pl.pallas_call(
        _attn_kernel,
        out_shape=jax.ShapeDtypeStruct((batch * seq, N_HEADS * V_HEAD), BF16),
        grid=(batch, nq),
        in_specs=[
            pl.BlockSpec((tq, N_HEADS * HEAD_PAD), lambda b, i: (b * nq + i, 0)),
            pl.BlockSpec((seq, N_HEADS * HEAD_PAD), lambda b, i: (b, 0)),
            pl.BlockSpec((seq, N_HEADS * HEAD_PAD), lambda b, i: (b, 0)),
        ],
        out_specs=pl.BlockSpec((tq, N_HEADS * V_HEAD), lambda b, i: (b * nq + i, 0)),
        scratch_shapes=[pltpu.VMEM((N_HEADS, tq, LANES), F32),
                        pltpu.VMEM((N_HEADS, tq, LANES), F32)],
        compiler_params=pltpu.CompilerParams(
            dimension_semantics=("arbitrary", "arbitrary"), vmem_limit_bytes=VMEM_LIMIT),
        name="attention",
    )(q, k, v)


def _post_kernel(attn_ref, sga_ref, gc_ref, x_ref, mod_ref, wua_ref, wo_ref, g_ref, rwh_ref,
                 rwl_ref, rb_ref, x1_ref, h2_ref, idx_ref, gate_ref, rank_ref, cnt_out_ref, cnt_ref):
    mod = mod_ref[...]
    a_branch = _dot(attn_ref[...], wua_ref[...])
    merged = sga_ref[...].astype(F32) * a_branch + gc_ref[...].astype(F32)
    mix = _dot(merged.astype(BF16), wo_ref[...])
    x1 = x_ref[...] + mod[2:3] * mix
    x1_ref[...] = x1
    h2 = _rms(x1, g_ref[...]) * (1.0 + mod[4:5]) + mod[3:4]
    h2_ref[...] = h2

    h_hi = h2.astype(BF16)
    h_lo = (h2 - h_hi.astype(F32)).astype(BF16)
    logits = (_dot(h_hi, rwh_ref[...]) + _dot(h_lo, rwh_ref[...]) + _dot(h_hi, rwl_ref[...])
              + rb_ref[...])
    lane = lax.broadcasted_iota(jnp.int32, logits.shape, 1)
    work = logits
    vals, idxs = [], []
    for _ in range(TOP_K):
        mk = jnp.max(work, axis=-1, keepdims=True)
        ik = jnp.min(jnp.where(work == mk, lane, LANES), axis=-1, keepdims=True)
        vals.append(mk)
        idxs.append(ik)
        work = jnp.where(lane == ik, -jnp.inf, work)
    es = [jnp.exp(vk - vals[0]) for vk in vals]
    denom = es[0] + es[1] + es[2] + es[3]
    @pl.when(pl.program_id(0) == 0)
    def _():
        cnt_ref[...] = jnp.zeros_like(cnt_ref)

    tm = logits.shape[0]
    chosen = jnp.zeros(logits.shape, F32)
    for kk in range(TOP_K):
        chosen = chosen + jnp.where(lane == idxs[kk], 1.0, 0.0)
    r_i = lax.broadcasted_iota(jnp.int32, (tm, tm), 0)
    c_i = lax.broadcasted_iota(jnp.int32, (tm, tm), 1)
    earlier = jnp.where(c_i < r_i, 1.0, 0.0).astype(BF16)
    before = _dot(earlier, chosen.astype(BF16)) + cnt_ref[0:1]
    cnt_new = cnt_ref[...] + jnp.sum(chosen, axis=0, keepdims=True)
    cnt_ref[...] = cnt_new
    cnt_out_ref[...] = cnt_new.astype(jnp.int32)

    idx_out = jnp.zeros(logits.shape, jnp.int32)
    gate_out = jnp.zeros(logits.shape, F32)
    rank_out = jnp.zeros(logits.shape, jnp.int32)
    for kk in range(TOP_K):
        rank_k = jnp.sum(jnp.where(lane == idxs[kk], before, 0.0), axis=-1, keepdims=True)
        idx_out = jnp.where(lane == kk, idxs[kk], idx_out)
        gate_out = jnp.where(lane == kk, es[kk] / denom, gate_out)
        rank_out = jnp.where(lane == kk, rank_k.astype(jnp.int32), rank_out)
    idx_ref[...] = idx_out
    gate_ref[...] = gate_out
    rank_ref[...] = rank_out


def _post(attn, sga, gc, x2, mod, wua, wo, norm_g, rw_hi, rw_lo, rb_pad, seq):
    t = x2.shape[0]
    tm = ROW_TILE
    tiles_per_seq = seq // tm
    full = lambda a: pl.BlockSpec(a.shape, lambda i: (0,) * a.ndim)
    rows = lambda w: pl.BlockSpec((tm, w), lambda i: (i, 0))
    outs = [jax.ShapeDtypeStruct((t, D_MODEL), F32),
            jax.ShapeDtypeStruct((t, D_MODEL), F32),
            jax.ShapeDtypeStruct((t, LANES), jnp.int32),
            jax.ShapeDtypeStruct((t, LANES), F32),
            jax.ShapeDtypeStruct((t, LANES), jnp.int32),
            jax.ShapeDtypeStruct((8, LANES), jnp.int32)]
    return pl.pallas_call(
        _post_kernel,
        out_shape=outs,
        grid=(t // tm,),
        in_specs=[
            rows(N_HEADS * V_HEAD), rows(D_MODEL), rows(D_MODEL), rows(D_MODEL),
            pl.BlockSpec((None, 8, D_MODEL), lambda i: (i // tiles_per_seq, 0, 0)),
            full(wua), full(wo), full(norm_g), full(rw_hi), full(rw_lo), full(rb_pad),
        ],
        out_specs=[rows(D_MODEL), rows(D_MODEL), rows(LANES), rows(LANES), rows(LANES),
                   pl.BlockSpec((8, LANES), lambda i: (0, 0))],
        scratch_shapes=[pltpu.VMEM((8, LANES), F32)],
        compiler_params=pltpu.CompilerParams(
            dimension_semantics=("arbitrary",), vmem_limit_bytes=VMEM_LIMIT),
        name="post_mixer",
    )(attn, sga, gc, x2, mod, wua, wo, norm_g, rw_hi, rw_lo, rb_pad)


_TB_EXPERT, _TB_VALID, _TB_FIRST, _TB_NEXT, _TB_SLOT = range(5)


def _moe_kernel(tb_ref, xs_ref, wgu_hbm, bgu_ref, wd_hbm, bd_ref, o_ref,
                wgu_f, wd_f, wgu_bf, wd_bf, sem):
    b = pl.program_id(0)
    n_valid = tb_ref[_TB_VALID, b]
    used = n_valid > 0
    slot = tb_ref[_TB_SLOT, b]

    def weight_copies(expert, sl):
        return (pltpu.make_async_copy(wgu_hbm.at[expert], wgu_f.at[sl], sem.at[0, sl]),
                pltpu.make_async_copy(wd_hbm.at[expert], wd_f.at[sl], sem.at[1, sl]))

    @pl.when(b == 0)
    def _():
        for cp in weight_copies(tb_ref[_TB_EXPERT, 0], 0):
            cp.start()

    @pl.when(tb_ref[_TB_FIRST, b] == 1)
    def _():
        nxt = tb_ref[_TB_NEXT, b]

        @pl.when(nxt >= 0)
        def _():
            for cp in weight_copies(nxt, 1 - slot):
                cp.start(priority=1)

        for cp in weight_copies(tb_ref[_TB_EXPERT, b], slot):
            cp.wait()
        wgu_bf[...] = wgu_f[slot].astype(BF16)
        wd_bf[...] = wd_f[slot].astype(BF16)

    @pl.when(used)
    def _():
        row = lax.broadcasted_iota(jnp.int32, xs_ref.shape, 0)
        xs = jnp.where(row < n_valid, xs_ref[...], 0.0).astype(BF16)
        gu = _dot(xs, wgu_bf[...]) + bgu_ref[...]
        gate = jnp.minimum(gu[:, :D_EXPERT], SWIGLU_LIMIT)
        up = jnp.clip(gu[:, D_EXPERT:], -SWIGLU_LIMIT, SWIGLU_LIMIT)
        act = (up + 1.0) * (gate * jax.nn.sigmoid(gate * SWIGLU_ALPHA))
        o_ref[...] = _dot(act.astype(BF16), wd_bf[...]) + bd_ref[...]

    @pl.when(jnp.logical_not(used))
    def _():
        o_ref[...] = jnp.zeros_like(o_ref)


def _moe(block_table, xs, w_gu, b_gu, w_down, b_down):
    n_rows = xs.shape[0]
    n_blocks = n_rows // MOE_BLOCK
    grid_spec = pltpu.PrefetchScalarGridSpec(
        num_scalar_prefetch=1,
        grid=(n_blocks,),
        in_specs=[
            pl.BlockSpec((MOE_BLOCK, D_MODEL), lambda b, tb: (b, 0)),
            pl.BlockSpec(memory_space=pl.ANY),
            pl.BlockSpec((None, 1, 2 * D_EXPERT), lambda b, tb: (tb[_TB_EXPERT, b], 0, 0)),
            pl.BlockSpec(memory_space=pl.ANY),
            pl.BlockSpec((None, 1, D_MODEL), lambda b, tb: (tb[_TB_EXPERT, b], 0, 0)),
        ],
        out_specs=pl.BlockSpec((MOE_BLOCK, D_MODEL), lambda b, tb: (b, 0)),
        scratch_shapes=[pltpu.VMEM((2, D_MODEL, 2 * D_EXPERT), F32),
                        pltpu.VMEM((2, D_EXPERT, D_MODEL), F32),
                        pltpu.VMEM((D_MODEL, 2 * D_EXPERT), BF16),
                        pltpu.VMEM((D_EXPERT, D_MODEL), BF16),
                        pltpu.SemaphoreType.DMA((2, 2))],
    )
    return pl.pallas_call(
        _moe_kernel,
        out_shape=jax.ShapeDtypeStruct((n_rows, D_MODEL), F32),
        grid_spec=grid_spec,
        compiler_params=pltpu.CompilerParams(
            dimension_semantics=("arbitrary",), vmem_limit_bytes=VMEM_LIMIT),
        name="moe_experts",
    )(block_table, xs, w_gu, b_gu, w_down, b_down)


def _final_kernel(last_layer, x1_ref, y_ref, gate_ref, mod_ref, g_ref, o_ref):
    mod = mod_ref[...]
    gate = gate_ref[...]
    ffn = gate[:, 0:1] * y_ref[0]
    for kk in range(1, TOP_K):
        ffn = ffn + gate[:, kk:kk + 1] * y_ref[kk]
    x = x1_ref[...] + mod[5:6] * ffn
    o_ref[...] = _rms(x, g_ref[...]) if last_layer else x


def _final(x1, y_kt, gate, mod, norm_g, seq, last_layer):
    t = x1.shape[0]
    tm = ROW_TILE
    tiles_per_seq = seq // tm
    rows = lambda w: pl.BlockSpec((tm, w), lambda i: (i, 0))
    return pl.pallas_call(
        functools.partial(_final_kernel, last_layer),
        out_shape=jax.ShapeDtypeStruct((t, D_MODEL), F32),
        grid=(t // tm,),
        in_specs=[
            rows(D_MODEL), pl.BlockSpec((TOP_K, tm, D_MODEL), lambda i: (0, i, 0)), rows(LANES),
            pl.BlockSpec((None, 8, D_MODEL), lambda i: (i // tiles_per_seq, 0, 0)),
            pl.BlockSpec(norm_g.shape, lambda i: (0, 0)),
        ],
        out_specs=rows(D_MODEL),
        compiler_params=pltpu.CompilerParams(
            dimension_semantics=("arbitrary",), vmem_limit_bytes=VMEM_LIMIT),
        name="combine_final",
    )(x1, y_kt, gate, mod, norm_g)


def _swap_halves(w):
    half = w.shape[-1] // 2
    return jnp.concatenate([w[..., half:], w[..., :half]], axis=-1)


def _prep_weights(w_in, w_uq, w_ukv):
    d = w_in.shape[0]
    splits = (Q_LORA, KV_LORA, QK_ROPE, CONV_WIDTH, CONV_WIDTH, CONV_WIDTH, D_MODEL, D_MODEL)
    offs = [0]
    for s in splits:
        offs.append(offs[-1] + s)
    part = lambda n: w_in[:, offs[n]:offs[n + 1]]
    z = lambda n: jnp.zeros((d, n), w_in.dtype)
    w_kpe = part(2)
    kpe_a = jnp.concatenate([z(QK_NOPE), w_kpe, z(HEAD_PAD - QK_HEAD)], axis=1)
    kpe_b = jnp.concatenate([z(QK_NOPE), _swap_halves(w_kpe), z(HEAD_PAD - QK_HEAD)], axis=1)
    win2 = jnp.concatenate([part(0), part(1), kpe_a, kpe_b, part(3), part(4), part(5), part(6),
                            part(7)], axis=1).astype(BF16)

    wq = w_uq.reshape(Q_LORA, N_HEADS, QK_HEAD)
    zq = lambda n: jnp.zeros((Q_LORA, N_HEADS, n), w_uq.dtype)
    wq2 = jnp.concatenate([wq, zq(HEAD_PAD - QK_HEAD)], axis=-1)
    wq2s = jnp.concatenate([zq(QK_NOPE), _swap_halves(wq[..., QK_NOPE:]), zq(HEAD_PAD - QK_HEAD)],
                           axis=-1)
    wq2 = wq2.reshape(Q_LORA, N_HEADS * HEAD_PAD).astype(BF16)
    wq2s = wq2s.reshape(Q_LORA, N_HEADS * HEAD_PAD).astype(BF16)

    wkv = w_ukv.reshape(KV_LORA, N_HEADS, QK_NOPE + V_HEAD)
    wk2 = jnp.concatenate([wkv[..., :QK_NOPE],
                           jnp.zeros((KV_LORA, N_HEADS, HEAD_PAD - QK_NOPE), w_ukv.dtype)], axis=-1)
    wk2 = wk2.reshape(KV_LORA, N_HEADS * HEAD_PAD).astype(BF16)
    wv = jnp.concatenate([wkv[..., QK_NOPE:],
                          jnp.zeros((KV_LORA, N_HEADS, HEAD_PAD - V_HEAD), w_ukv.dtype)], axis=-1)
    wv = wv.reshape(KV_LORA, N_HEADS * HEAD_PAD).astype(BF16)
    return win2, wq2, wq2s, wk2, wv


def _rope_tables(positions):
    inv_freq = 1.0 / (ROPE_THETA ** (jnp.arange(0, QK_ROPE, 2, dtype=F32) / QK_ROPE))
    ang = positions.astype(F32).reshape(-1, 1) * inv_freq
    cos, sin = jnp.cos(ang), jnp.sin(ang)
    t = ang.shape[0]
    ones = jnp.ones((t, QK_NOPE), F32)
    zpad = jnp.zeros((t, HEAD_PAD - QK_HEAD), F32)
    cosf = jnp.concatenate([ones, cos, cos, zpad], axis=1)
    sinf = jnp.concatenate([jnp.zeros((t, QK_NOPE), F32), -sin, sin, zpad], axis=1)
    return cosf, sinf


def _route(top_idx, rank, counts, n_tokens):
    padded = (counts + MOE_BLOCK - 1) // MOE_BLOCK * MOE_BLOCK
    pad_end = jnp.cumsum(padded)
    pad_start = pad_end - padded
    experts = jnp.arange(N_EXPERTS, dtype=jnp.int32)
    start_of = jnp.sum(jnp.where(top_idx[:, :, None] == experts, pad_start, 0), axis=-1)
    dest = start_of + rank
    n_rows = n_tokens * TOP_K + N_EXPERTS * MOE_BLOCK
    n_blocks = n_rows // MOE_BLOCK
    block_start = jnp.arange(n_blocks, dtype=jnp.int32) * MOE_BLOCK
    block_expert = jnp.minimum(
        (pad_end[None, :] <= block_start[:, None]).astype(jnp.int32).sum(axis=1), N_EXPERTS - 1)
    group_end = (pad_start + counts)[block_expert]
    block_valid = jnp.clip(group_end - block_start, 0, MOE_BLOCK)
    blk = jnp.arange(n_blocks, dtype=jnp.int32)
    prev_expert = jnp.concatenate([jnp.full((1,), -1, jnp.int32), block_expert[:-1]])
    first = jnp.logical_and(block_valid > 0, block_expert != prev_expert)
    later_first = jnp.logical_and(first[None, :], blk[None, :] > blk[:, None])
    next_pos = jnp.min(jnp.where(later_first, blk[None, :], n_blocks), axis=1)
    next_expert = jnp.where(next_pos < n_blocks,
                            block_expert[jnp.minimum(next_pos, n_blocks - 1)], -1)
    slot = (jnp.cumsum(first.astype(jnp.int32)) - 1) % 2
    table = jnp.stack([block_expert, block_valid, first.astype(jnp.int32), next_expert, slot])
    return dest.astype(jnp.int32), table.astype(jnp.int32), n_rows


SC_CORES = 2
SC_SUBCORES = 16
SC_WORKERS = SC_CORES * SC_SUBCORES
SC_CHUNK = 64


def _sc_mesh():
    return plsc.VectorSubcoreMesh(core_axis_name="c", subcore_axis_name="s")


def _sc_worker():
    return lax.axis_index("s") * SC_CORES + lax.axis_index("c")


def _dispatch(h2, dest, n_rows):
    t, d = h2.shape
    per_w = t // SC_WORKERS
    n_chunks = per_w // SC_CHUNK
    idx = dest.reshape(SC_WORKERS, n_chunks, SC_CHUNK, TOP_K).transpose(0, 3, 1, 2)
    idx = idx.reshape(SC_WORKERS, TOP_K * n_chunks, SC_CHUNK)

    @functools.partial(
        pl.kernel, mesh=_sc_mesh(),
        out_type=jax.ShapeDtypeStruct((n_rows, d), h2.dtype),
        scratch_types=[pltpu.VMEM((TOP_K * n_chunks, SC_CHUNK), jnp.int32),
                       pltpu.VMEM((SC_CHUNK, d), h2.dtype)],
        name="moe_dispatch")
    def run(h2_hbm, idx_hbm, xs_hbm, idx_v, rows_v):
        w = _sc_worker()
        pltpu.sync_copy(idx_hbm.at[w], idx_v)

        @pl.loop(0, n_chunks)
        def _(g):
            pltpu.sync_copy(h2_hbm.at[pl.ds(w * per_w + g * SC_CHUNK, SC_CHUNK)], rows_v)
            for kk in range(TOP_K):
                pltpu.sync_copy(rows_v, xs_hbm.at[idx_v.at[kk * n_chunks + g]])

    return run(h2, idx)


def _undispatch(ys, dest):
    t = dest.shape[0]
    d = ys.shape[1]
    n_out = t * TOP_K
    per_w = n_out // SC_WORKERS
    n_chunks = per_w // SC_CHUNK
    idx = dest.T.reshape(SC_WORKERS, n_chunks, SC_CHUNK)

    @functools.partial(
        pl.kernel, mesh=_sc_mesh(),
        out_type=jax.ShapeDtypeStruct((n_out, d), ys.dtype),
        scratch_types=[pltpu.VMEM((n_chunks, SC_CHUNK), jnp.int32),
                       pltpu.VMEM((SC_CHUNK, d), ys.dtype),
                       pltpu.SemaphoreType.DMA],
        name="moe_undispatch")
    def run(ys_hbm, idx_hbm, out_hbm, idx_v, rows_v, sem):
        w = _sc_worker()
        pltpu.sync_copy(idx_hbm.at[w], idx_v)

        @pl.loop(0, n_chunks)
        def _(g):
            pltpu.async_copy(ys_hbm.at[idx_v.at[g]], rows_v, sem).wait()
            pltpu.sync_copy(rows_v, out_hbm.at[pl.ds(w * per_w + g * SC_CHUNK, SC_CHUNK)])

    return run(ys, idx).reshape(TOP_K, t, d)


def kernel(x, c, positions, w_ada, b_ada, norm_mix_g, w_in, q_norm_g, w_uq, kv_norm_g, w_ukv,
           w_up_attn, conv_w, w_up_conv, w_o, norm_ffn_g, router_w, router_b, w_gu, b_gu,
           w_down, b_down, norm_final_g):
    batch, seq, d = x.shape
    t = batch * seq
    depth = w_ada.shape[0]
    x2 = x.reshape(t, d)
    cosf, sinf = _rope_tables(positions)
    c_pad = jnp.zeros((8, d), F32).at[:batch].set(c)

    for l in range(depth):
        ada = _ada(c_pad, w_ada[l], b_ada[l].reshape(1, -1))
        mod = ada[:batch].reshape(batch, 6, d)
        mod = jnp.concatenate([mod, jnp.zeros((batch, 2, d), F32)], axis=1)

        win2, wq2, wq2s, wk2, wv = _prep_weights(w_in[l], w_uq[l], w_ukv[l])
        q, k, v, sga, gc = _pre(x2, mod, norm_mix_g[l].reshape(1, d), win2,
                                q_norm_g[l].reshape(1, -1), wq2, wq2s,
                                kv_norm_g[l].reshape(1, -1), wk2, wv, cosf, sinf, conv_w[l],
                                w_up_conv[l].astype(BF16), seq)
        attn = _attention(q, k, v, batch, seq)

        rw_pad = jnp.concatenate([router_w[l], jnp.zeros((d, LANES - N_EXPERTS), F32)], axis=1)
        rb_pad = jnp.concatenate([router_b[l], jnp.full((LANES - N_EXPERTS,), NEG_BIG, F32)])
        rw_hi = rw_pad.astype(BF16)
        rw_lo = (rw_pad - rw_hi.astype(F32)).astype(BF16)
        x1, h2, idx_pad, gate_pad, rank_pad, counts = _post(
            attn, sga, gc, x2, mod, w_up_attn[l].astype(BF16), w_o[l].astype(BF16),
            norm_ffn_g[l].reshape(1, d), rw_hi, rw_lo, rb_pad.reshape(1, LANES), seq)

        dest, block_table, n_rows = _route(
            idx_pad[:, :TOP_K], rank_pad[:, :TOP_K], counts[0, :N_EXPERTS], t)
        xs = _dispatch(h2, dest, n_rows)
        ys = _moe(block_table, xs, w_gu[l], b_gu[l].reshape(N_EXPERTS, 1, -1),
                  w_down[l], b_down[l].reshape(N_EXPERTS, 1, -1))
        y_kt = _undispatch(ys, dest)
        x2 = _final(x1, y_kt, gate_pad, mod, norm_final_g.reshape(1, d), seq, l == depth - 1)

    return x2.reshape(batch, seq, d)
```

```python
import functools
import math

import jax
import jax.numpy as jnp
from jax import lax
from jax.experimental import pallas as pl
from jax.experimental.pallas import tpu as pltpu
from jax.experimental.pallas import tpu_sc as plsc

D_MODEL = 1024
CHUNK = 64
N_HEADS = 8
Q_LORA = 256
KV_LORA = 128
QK_NOPE = 64
QK_ROPE = 32
V_HEAD = 64
QK_HEAD = QK_NOPE + QK_ROPE
ROPE_THETA = 10000.0
CONV_WIDTH = 512
CONV_K = 3
N_EXPERTS = 32
TOP_K = 4
D_EXPERT = 1024
SWIGLU_LIMIT = 7.0
SWIGLU_ALPHA = 1.702
MOE_BLOCK = 256
RMS_EPS = 1e-6

LANES = 128
HEAD_PAD = 128
NEG_BIG = -1e30
VMEM_LIMIT = 56 * 1024 * 1024

F32 = jnp.float32
BF16 = jnp.bfloat16

Q_PRESCALE = (QK_HEAD ** -0.5) * math.log2(math.e)

ROW_TILE = 512
ATT_BLOCK = 512
ATT_WIDE = 2


def _rms(x, g):
    ms = jnp.mean(x * x, axis=-1, keepdims=True)
    return x * lax.rsqrt(ms + RMS_EPS) * g


def _dot(a, b):
    return jnp.dot(a, b, preferred_element_type=F32)


PACKED = D_MODEL // 2


def _pack_row(x):
    return pltpu.pack_elementwise([x[:, :PACKED], x[:, PACKED:]], packed_dtype=BF16)


def _unpack_row(w):
    half = lambda i: pltpu.unpack_elementwise(w, index=i, packed_dtype=BF16, unpacked_dtype=F32)
    return jnp.concatenate([half(0), half(1)], axis=-1)


def _ada_kernel(c_ref, w_ref, b_ref, o_ref):
    c = c_ref[...]
    ca = (c * jax.nn.sigmoid(c)).astype(BF16)
    o_ref[...] = _dot(ca, w_ref[...].astype(BF16)) + b_ref[...]


def _ada(c_pad, w_ada, b_ada):
    n = w_ada.shape[1]
    tn = 1024
    return pl.pallas_call(
        _ada_kernel,
        out_shape=jax.ShapeDtypeStruct((c_pad.shape[0], n), F32),
        grid=(n // tn,),
        in_specs=[
            pl.BlockSpec(c_pad.shape, lambda j: (0, 0)),
            pl.BlockSpec((D_MODEL, tn), lambda j: (0, j)),
            pl.BlockSpec((1, tn), lambda j: (0, j)),
        ],
        out_specs=pl.BlockSpec((c_pad.shape[0], tn), lambda j: (0, j)),
        compiler_params=pltpu.CompilerParams(
            dimension_semantics=("arbitrary",), vmem_limit_bytes=VMEM_LIMIT),
        name="ada",
    )(c_pad, w_ada, b_ada)


_C_QLAT = 0
_C_KVLAT = _C_QLAT + Q_LORA
_C_KPE_A = _C_KVLAT + KV_LORA
_C_KPE_B = _C_KPE_A + HEAD_PAD
_C_U = _C_KPE_B + HEAD_PAD
_C_C = _C_U + CONV_WIDTH
_C_B = _C_C + CONV_WIDTH
_C_GA = _C_B + CONV_WIDTH
_C_GC = _C_GA + D_MODEL
_C_END = _C_GC + D_MODEL


def _pre_kernel(tiles_per_seq, x_ref, mod_ref, g_ref, win_ref, qg_ref, wq_ref, wqs_ref,
                kvg_ref, wk_ref, wv_ref, cos_ref, sin_ref, cw_ref, wuc_ref,
                q_ref, k_ref, v_ref, sga_ref, gc_ref, carry_ref):
    i = pl.program_id(0)
    tm = x_ref.shape[0]
    mod = mod_ref[...]
    h = _rms(x_ref[...], g_ref[...]) * (1.0 + mod[1:2]) + mod[0:1]
    hb = h.astype(BF16)

    cosf = cos_ref[...]
    sinf = sin_ref[...]
    cos8 = jnp.concatenate([cosf] * N_HEADS, axis=-1)
    sin8 = jnp.concatenate([sinf] * N_HEADS, axis=-1)

    small = _dot(hb, win_ref[:, _C_QLAT:_C_U])
    q_lat = small[:, _C_QLAT:_C_KVLAT]
    kv_lat = small[:, _C_KVLAT:_C_KPE_A]
    kpe = small[:, _C_KPE_A:_C_KPE_B] * cosf + small[:, _C_KPE_B:_C_U] * sinf
    qn = _rms(q_lat, qg_ref[...]).astype(BF16)
    q = _dot(qn, wq_ref[...]) * cos8 + _dot(qn, wqs_ref[...]) * sin8
    q_ref[...] = (q * Q_PRESCALE).astype(BF16)
    kvn = _rms(kv_lat, kvg_ref[...]).astype(BF16)
    k = _dot(kvn, wk_ref[...]) + jnp.concatenate([kpe] * N_HEADS, axis=-1)
    k_ref[...] = k.astype(BF16)
    lane = lax.broadcasted_iota(jnp.int32, (tm, N_HEADS * HEAD_PAD), 1)
    ones_col = jnp.where(lane % HEAD_PAD == V_HEAD, 1.0, 0.0)
    v_ref[...] = (_dot(kvn, wv_ref[...]) + ones_col).astype(BF16)

    ucb = _dot(hb, win_ref[:, _C_U:_C_GA])
    cu = ucb[:, 0:CONV_WIDTH] * ucb[:, CONV_WIDTH:2 * CONV_WIDTH]
    b_gate = ucb[:, 2 * CONV_WIDTH:3 * CONV_WIDTH]

    @pl.when(i % tiles_per_seq == 0)
    def _():
        carry_ref[...] = jnp.zeros_like(carry_ref)

    prev = carry_ref[...]
    row = lax.broadcasted_iota(jnp.int32, cu.shape, 0)
    cu1 = jnp.where(row == 0, prev[7:8], pltpu.roll(cu, 1, 0))
    cu2 = jnp.where(row == 0, prev[6:7], jnp.where(row == 1, prev[7:8], pltpu.roll(cu, 2, 0)))
    cw = cw_ref[...]
    z = cw[2:3] * cu + cw[1:2] * cu1 + cw[0:1] * cu2
    carry_ref[...] = cu[tm - 8:tm]
    c_branch = _dot((b_gate * z).astype(BF16), wuc_ref[...])

    gates = _dot(hb, win_ref[:, _C_GA:_C_END])
    sga_ref[...] = jax.nn.sigmoid(gates[:, 0:D_MODEL]).astype(BF16)
    gc_ref[...] = (jax.nn.sigmoid(gates[:, D_MODEL:]) * c_branch).astype(BF16)


def _pre(x2, mod, norm_g, win2, q_norm_g, wq2, wq2s, kv_norm_g, wk2, wv, cosf, sinf, conv_w,
         w_up_conv, seq):
    t = x2.shape[0]
    tm = ROW_TILE
    tiles_per_seq = seq // tm
    full = lambda a: pl.BlockSpec(a.shape, lambda i: (0,) * a.ndim)
    rows = lambda w: pl.BlockSpec((tm, w), lambda i: (i, 0))
    outs = [jax.ShapeDtypeStruct((t, N_HEADS * HEAD_PAD), BF16),
            jax.ShapeDtypeStruct((t, N_HEADS * HEAD_PAD), BF16),
            jax.ShapeDtypeStruct((t, N_HEADS * HEAD_PAD), BF16),
            jax.ShapeDtypeStruct((t, D_MODEL), BF16),
            jax.ShapeDtypeStruct((t, D_MODEL), BF16)]
    return pl.pallas_call(
        functools.partial(_pre_kernel, tiles_per_seq),
        out_shape=outs,
        grid=(t // tm,),
        in_specs=[
            rows(D_MODEL),
            pl.BlockSpec((None, 8, D_MODEL), lambda i: (i // tiles_per_seq, 0, 0)),
            full(norm_g), full(win2), full(q_norm_g), full(wq2), full(wq2s),
            full(kv_norm_g), full(wk2), full(wv),
            rows(HEAD_PAD), rows(HEAD_PAD), full(conv_w), full(w_up_conv),
        ],
        out_specs=[rows(N_HEADS * HEAD_PAD), rows(N_HEADS * HEAD_PAD), rows(N_HEADS * HEAD_PAD),
                   rows(D_MODEL), rows(D_MODEL)],
        scratch_shapes=[pltpu.VMEM((8, CONV_WIDTH), F32)],
        compiler_params=pltpu.CompilerParams(
            dimension_semantics=("arbitrary",), vmem_limit_bytes=VMEM_LIMIT),
        name="pre_mixer",
    )(x2, mod, norm_g, win2, q_norm_g, wq2, wq2s, kv_norm_g, wk2, wv, cosf, sinf, conv_w,
      w_up_conv)


def _attn_kernel(q_ref, k_ref, v_ref, o_ref, m_ref, acc_ref):
    i = pl.program_id(1)
    tq = q_ref.shape[0]

    m_ref[...] = jnp.full_like(m_ref, NEG_BIG)
    acc_ref[...] = jnp.zeros_like(acc_ref)

    def step(k0, tk, masked):
        if masked:
            rq = lax.broadcasted_iota(jnp.int32, (tq, tk), 0) // CHUNK
            ck = lax.broadcasted_iota(jnp.int32, (tq, tk), 1) // CHUNK
            allowed = ck <= rq
        for hd in range(N_HEADS):
            hs = slice(hd * HEAD_PAD, (hd + 1) * HEAD_PAD)
            s = lax.dot_general(q_ref[:, hs], k_ref[pl.ds(k0, tk), hs],
                                (((1,), (1,)), ((), ())), preferred_element_type=F32)
            if masked:
                s = jnp.where(allowed, s, NEG_BIG)
            m_old = m_ref[hd]
            s_max = s[:, 0:LANES]
            for c in range(1, tk // LANES):
                s_max = jnp.maximum(s_max, s[:, c * LANES:(c + 1) * LANES])
            m_new = jnp.maximum(m_old, jnp.max(s_max, axis=-1, keepdims=True))
            alpha = jnp.exp2(m_old - m_new)
            p = jnp.concatenate(
                [jnp.exp2(s[:, c * LANES:(c + 1) * LANES] - m_new).astype(BF16)
                 for c in range(tk // LANES)], axis=-1)
            acc_ref[hd] = alpha * acc_ref[hd] + _dot(p, v_ref[pl.ds(k0, tk), hs])
            m_ref[hd] = m_new

    wide = ATT_WIDE * tq

    def body(j, carry):
        step(pl.multiple_of(j * wide, wide), wide, False)
        return carry

    lax.fori_loop(0, i // ATT_WIDE, body, 0)

    for r in range(1, ATT_WIDE):
        @pl.when(i % ATT_WIDE >= r)
        def _():
            step(pl.multiple_of((i - i % ATT_WIDE + r - 1) * tq, tq), tq, False)

    step(pl.multiple_of(i * tq, tq), tq, True)

    for hp in range(N_HEADS // 2):
        pair = []
        for hd in (2 * hp, 2 * hp + 1):
            acc = acc_ref[hd]
            pair.append(acc[:, 0:V_HEAD] / acc[:, V_HEAD:V_HEAD + 1])
        o_ref[:, hp * LANES:(hp + 1) * LANES] = jnp.concatenate(pair, axis=-1).astype(BF16)


def _attention(q, k, v, batch, seq):
    tq = ATT_BLOCK
    nq = seq // tq
    return pl.pallas_call(
        _attn_kernel,
        out_shape=jax.ShapeDtypeStruct((batch * seq, N_HEADS * V_HEAD), BF16),
        grid=(batch, nq),
        in_specs=[
            pl.BlockSpec((tq, N_HEADS * HEAD_PAD), lambda b, i: (b * nq + i, 0)),
            pl.BlockSpec((seq, N_HEADS * HEAD_PAD), lambda b, i: (b, 0)),
            pl.BlockSpec((seq, N_HEADS * HEAD_PAD), lambda b, i: (b, 0)),
        ],
        out_specs=pl.BlockSpec((tq, N_HEADS * V_HEAD), lambda b, i: (b * nq + i, 0)),
        scratch_shapes=[pltpu.VMEM((N_HEADS, tq, LANES), F32),
                        pltpu.VMEM((N_HEADS, tq, LANES), F32)],
        compiler_params=pltpu.CompilerParams(
            dimension_semantics=("arbitrary", "arbitrary"), vmem_limit_bytes=VMEM_LIMIT),
        name="attention",
    )(q, k, v)


def _post_kernel(attn_ref, sga_ref, gc_ref, x_ref, mod_ref, wua_ref, wo_ref, g_ref, rwh_ref,
                 rwl_ref, rb_ref, x1_ref, h2_ref, idx_ref, gate_ref, rank_ref, cnt_out_ref, cnt_ref):
    mod = mod_ref[...]
    a_branch = _dot(attn_ref[...], wua_ref[...])
    merged = sga_ref[...].astype(F32) * a_branch + gc_ref[...].astype(F32)
    mix = _dot(merged.astype(BF16), wo_ref[...])
    x1 = x_ref[...] + mod[2:3] * mix
    x1_ref[...] = x1
    h2 = _rms(x1, g_ref[...]) * (1.0 + mod[4:5]) + mod[3:4]
    h2_ref[...] = _pack_row(h2)

    h_hi = h2.astype(BF16)
    h_lo = (h2 - h_hi.astype(F32)).astype(BF16)
    logits = (_dot(h_hi, rwh_ref[...]) + _dot(h_lo, rwh_ref[...]) + _dot(h_hi, rwl_ref[...])
              + rb_ref[...])
    lane = lax.broadcasted_iota(jnp.int32, logits.shape, 1)
    work = logits
    vals, idxs = [], []
    for _ in range(TOP_K):
        mk = jnp.max(work, axis=-1, keepdims=True)
        ik = jnp.min(jnp.where(work == mk, lane, LANES), axis=-1, keepdims=True)
        vals.append(mk)
        idxs.append(ik)
        work = jnp.where(lane == ik, -jnp.inf, work)
    es = [jnp.exp(vk - vals[0]) for vk in vals]
    denom = es[0] + es[1] + es[2] + es[3]
    @pl.when(pl.program_id(0) == 0)
    def _():
        cnt_ref[...] = jnp.zeros_like(cnt_ref)

    tm = logits.shape[0]
    chosen = jnp.zeros(logits.shape, F32)
    for kk in range(TOP_K):
        chosen = chosen + jnp.where(lane == idxs[kk], 1.0, 0.0)
    r_i = lax.broadcasted_iota(jnp.int32, (tm, tm), 0)
    c_i = lax.broadcasted_iota(jnp.int32, (tm, tm), 1)
    earlier = jnp.where(c_i < r_i, 1.0, 0.0).astype(BF16)
    before = _dot(earlier, chosen.astype(BF16)) + cnt_ref[0:1]
    cnt_new = cnt_ref[...] + jnp.sum(chosen, axis=0, keepdims=True)
    cnt_ref[...] = cnt_new
    cnt_out_ref[...] = cnt_new.astype(jnp.int32)

    idx_out = jnp.zeros(logits.shape, jnp.int32)
    gate_out = jnp.zeros(logits.shape, F32)
    rank_out = jnp.zeros(logits.shape, jnp.int32)
    for kk in range(TOP_K):
        rank_k = jnp.sum(jnp.where(lane == idxs[kk], before, 0.0), axis=-1, keepdims=True)
        idx_out = jnp.where(lane == kk, idxs[kk], idx_out)
        gate_out = jnp.where(lane == kk, es[kk] / denom, gate_out)
        rank_out = jnp.where(lane == kk, rank_k.astype(jnp.int32), rank_out)
    idx_ref[...] = idx_out
    gate_ref[...] = gate_out
    rank_ref[...] = rank_out


def _post(attn, sga, gc, x2, mod, wua, wo, norm_g, rw_hi, rw_lo, rb_pad, seq):
    t = x2.shape[0]
    tm = ROW_TILE
    tiles_per_seq = seq // tm
    full = lambda a: pl.BlockSpec(a.shape, lambda i: (0,) * a.ndim)
    rows = lambda w: pl.BlockSpec((tm, w), lambda i: (i, 0))
    outs = [jax.ShapeDtypeStruct((t, D_MODEL), F32),
            jax.ShapeDtypeStruct((t, PACKED), jnp.uint32),
            jax.ShapeDtypeStruct((t, LANES), jnp.int32),
            jax.ShapeDtypeStruct((t, LANES), F32),
            jax.ShapeDtypeStruct((t, LANES), jnp.int32),
            jax.ShapeDtypeStruct((8, LANES), jnp.int32)]
    return pl.pallas_call(
        _post_kernel,
        out_shape=outs,
        grid=(t // tm,),
        in_specs=[
            rows(N_HEADS * V_HEAD), rows(D_MODEL), rows(D_MODEL), rows(D_MODEL),
            pl.BlockSpec((None, 8, D_MODEL), lambda i: (i // tiles_per_seq, 0, 0)),
            full(wua), full(wo), full(norm_g), full(rw_hi), full(rw_lo), full(rb_pad),
        ],
        out_specs=[rows(D_MODEL), rows(PACKED), rows(LANES), rows(LANES), rows(LANES),
                   pl.BlockSpec((8, LANES), lambda i: (0, 0))],
        scratch_shapes=[pltpu.VMEM((8, LANES), F32)],
        compiler_params=pltpu.CompilerParams(
            dimension_semantics=("arbitrary",), vmem_limit_bytes=VMEM_LIMIT),
        name="post_mixer",
    )(attn, sga, gc, x2, mod, wua, wo, norm_g, rw_hi, rw_lo, rb_pad)


_TB_EXPERT, _TB_VALID, _TB_FIRST, _TB_NEXT, _TB_SLOT = range(5)


def _moe_kernel(tb_ref, xs_ref, wgu_hbm, bgu_ref, wd_hbm, bd_ref, o_ref,
                wgu_f, wd_f, wgu_bf, wd_bf, sem):
    b = pl.program_id(0)
    n_valid = tb_ref[_TB_VALID, b]
    used = n_valid > 0
    slot = tb_ref[_TB_SLOT, b]

    def weight_copies(expert, sl):
        return (pltpu.make_async_copy(wgu_hbm.at[expert], wgu_f.at[sl], sem.at[0, sl]),
                pltpu.make_async_copy(wd_hbm.at[expert], wd_f.at[sl], sem.at[1, sl]))

    @pl.when(b == 0)
    def _():
        for cp in weight_copies(tb_ref[_TB_EXPERT, 0], 0):
            cp.start()

    @pl.when(tb_ref[_TB_FIRST, b] == 1)
    def _():
        nxt = tb_ref[_TB_NEXT, b]

        @pl.when(nxt >= 0)
        def _():
            for cp in weight_copies(nxt, 1 - slot):
                cp.start(priority=1)

        for cp in weight_copies(tb_ref[_TB_EXPERT, b], slot):
            cp.wait()
        wgu_bf[...] = wgu_f[slot].astype(BF16)
        wd_bf[...] = wd_f[slot].astype(BF16)

    @pl.when(used)
    def _():
        row = lax.broadcasted_iota(jnp.int32, xs_ref.shape, 0)
        xs = _unpack_row(jnp.where(row < n_valid, xs_ref[...], 0)).astype(BF16)
        gu = _dot(xs, wgu_bf[...]) + bgu_ref[...]
        gate = jnp.minimum(gu[:, :D_EXPERT], SWIGLU_LIMIT)
        up = jnp.clip(gu[:, D_EXPERT:], -SWIGLU_LIMIT, SWIGLU_LIMIT)
        act = (up + 1.0) * (gate * jax.nn.sigmoid(gate * SWIGLU_ALPHA))
        o_ref[...] = _pack_row(_dot(act.astype(BF16), wd_bf[...]) + bd_ref[...])

    @pl.when(jnp.logical_not(used))
    def _():
        o_ref[...] = jnp.zeros_like(o_ref)


def _moe(block_table, xs, w_gu, b_gu, w_down, b_down):
    n_rows = xs.shape[0]
    n_blocks = n_rows // MOE_BLOCK
    grid_spec = pltpu.PrefetchScalarGridSpec(
        num_scalar_prefetch=1,
        grid=(n_blocks,),
        in_specs=[
            pl.BlockSpec((MOE_BLOCK, PACKED), lambda b, tb: (b, 0)),
            pl.BlockSpec(memory_space=pl.ANY),
            pl.BlockSpec((None, 1, 2 * D_EXPERT), lambda b, tb: (tb[_TB_EXPERT, b], 0, 0)),
            pl.BlockSpec(memory_space=pl.ANY),
            pl.BlockSpec((None, 1, D_MODEL), lambda b, tb: (tb[_TB_EXPERT, b], 0, 0)),
        ],
        out_specs=pl.BlockSpec((MOE_BLOCK, PACKED), lambda b, tb: (b, 0)),
        scratch_shapes=[pltpu.VMEM((2, D_MODEL, 2 * D_EXPERT), F32),
                        pltpu.VMEM((2, D_EXPERT, D_MODEL), F32),
                        pltpu.VMEM((D_MODEL, 2 * D_EXPERT), BF16),
                        pltpu.VMEM((D_EXPERT, D_MODEL), BF16),
                        pltpu.SemaphoreType.DMA((2, 2))],
    )
    return pl.pallas_call(
        _moe_kernel,
        out_shape=jax.ShapeDtypeStruct((n_rows, PACKED), jnp.uint32),
        grid_spec=grid_spec,
        compiler_params=pltpu.CompilerParams(
            dimension_semantics=("arbitrary",), vmem_limit_bytes=VMEM_LIMIT),
        name="moe_experts",
    )(block_table, xs, w_gu, b_gu, w_down, b_down)


def _final_kernel(last_layer, x1_ref, y_ref, gate_ref, mod_ref, g_ref, o_ref):
    mod = mod_ref[...]
    gate = gate_ref[...]
    ffn = gate[:, 0:1] * _unpack_row(y_ref[0])
    for kk in range(1, TOP_K):
        ffn = ffn + gate[:, kk:kk + 1] * _unpack_row(y_ref[kk])
    x = x1_ref[...] + mod[5:6] * ffn
    o_ref[...] = _rms(x, g_ref[...]) if last_layer else x


def _final(x1, y_kt, gate, mod, norm_g, seq, last_layer):
    t = x1.shape[0]
    tm = ROW_TILE
    tiles_per_seq = seq // tm
    rows = lambda w: pl.BlockSpec((tm, w), lambda i: (i, 0))
    return pl.pallas_call(
        functools.partial(_final_kernel, last_layer),
        out_shape=jax.ShapeDtypeStruct((t, D_MODEL), F32),
        grid=(t // tm,),
        in_specs=[
            rows(D_MODEL), pl.BlockSpec((TOP_K, tm, PACKED), lambda i: (0, i, 0)), rows(LANES),
            pl.BlockSpec((None, 8, D_MODEL), lambda i: (i // tiles_per_seq, 0, 0)),
            pl.BlockSpec(norm_g.shape, lambda i: (0, 0)),
        ],
        out_specs=rows(D_MODEL),
        compiler_params=pltpu.CompilerParams(
            dimension_semantics=("arbitrary",), vmem_limit_bytes=VMEM_LIMIT),
        name="combine_final",
    )(x1, y_kt, gate, mod, norm_g)


def _swap_halves(w):
    half = w.shape[-1] // 2
    return jnp.concatenate([w[..., half:], w[..., :half]], axis=-1)


def _prep_weights(w_in, w_uq, w_ukv):
    d = w_in.shape[0]
    splits = (Q_LORA, KV_LORA, QK_ROPE, CONV_WIDTH, CONV_WIDTH, CONV_WIDTH, D_MODEL, D_MODEL)
    offs = [0]
    for s in splits:
        offs.append(offs[-1] + s)
    part = lambda n: w_in[:, offs[n]:offs[n + 1]]
    z = lambda n: jnp.zeros((d, n), w_in.dtype)
    w_kpe = part(2)
    kpe_a = jnp.concatenate([z(QK_NOPE), w_kpe, z(HEAD_PAD - QK_HEAD)], axis=1)
    kpe_b = jnp.concatenate([z(QK_NOPE), _swap_halves(w_kpe), z(HEAD_PAD - QK_HEAD)], axis=1)
    win2 = jnp.concatenate([part(0), part(1), kpe_a, kpe_b, part(3), part(4), part(5), part(6),
                            part(7)], axis=1).astype(BF16)

    wq = w_uq.reshape(Q_LORA, N_HEADS, QK_HEAD)
    zq = lambda n: jnp.zeros((Q_LORA, N_HEADS, n), w_uq.dtype)
    wq2 = jnp.concatenate([wq, zq(HEAD_PAD - QK_HEAD)], axis=-1)
    wq2s = jnp.concatenate([zq(QK_NOPE), _swap_halves(wq[..., QK_NOPE:]), zq(HEAD_PAD - QK_HEAD)],
                           axis=-1)
    wq2 = wq2.reshape(Q_LORA, N_HEADS * HEAD_PAD).astype(BF16)
    wq2s = wq2s.reshape(Q_LORA, N_HEADS * HEAD_PAD).astype(BF16)

    wkv = w_ukv.reshape(KV_LORA, N_HEADS, QK_NOPE + V_HEAD)
    wk2 = jnp.concatenate([wkv[..., :QK_NOPE],
                           jnp.zeros((KV_LORA, N_HEADS, HEAD_PAD - QK_NOPE), w_ukv.dtype)], axis=-1)
    wk2 = wk2.reshape(KV_LORA, N_HEADS * HEAD_PAD).astype(BF16)
    wv = jnp.concatenate([wkv[..., QK_NOPE:],
                          jnp.zeros((KV_LORA, N_HEADS, HEAD_PAD - V_HEAD), w_ukv.dtype)], axis=-1)
    wv = wv.reshape(KV_LORA, N_HEADS * HEAD_PAD).astype(BF16)
    return win2, wq2, wq2s, wk2, wv


def _rope_tables(positions):
    inv_freq = 1.0 / (ROPE_THETA ** (jnp.arange(0, QK_ROPE, 2, dtype=F32) / QK_ROPE))
    ang = positions.astype(F32).reshape(-1, 1) * inv_freq
    cos, sin = jnp.cos(ang), jnp.sin(ang)
    t = ang.shape[0]
    ones = jnp.ones((t, QK_NOPE), F32)
    zpad = jnp.zeros((t, HEAD_PAD - QK_HEAD), F32)
    cosf = jnp.concatenate([ones, cos, cos, zpad], axis=1)
    sinf = jnp.concatenate([jnp.zeros((t, QK_NOPE), F32), -sin, sin, zpad], axis=1)
    return cosf, sinf


def _route(top_idx, rank, counts, n_tokens):
    padded = (counts + MOE_BLOCK - 1) // MOE_BLOCK * MOE_BLOCK
    pad_end = jnp.cumsum(padded)
    pad_start = pad_end - padded
    experts = jnp.arange(N_EXPERTS, dtype=jnp.int32)
    start_of = jnp.sum(jnp.where(top_idx[:, :, None] == experts, pad_start, 0), axis=-1)
    dest = start_of + rank
    n_rows = n_tokens * TOP_K + N_EXPERTS * MOE_BLOCK
    n_blocks = n_rows // MOE_BLOCK
    block_start = jnp.arange(n_blocks, dtype=jnp.int32) * MOE_BLOCK
    block_expert = jnp.minimum(
        (pad_end[None, :] <= block_start[:, None]).astype(jnp.int32).sum(axis=1), N_EXPERTS - 1)
    group_end = (pad_start + counts)[block_expert]
    block_valid = jnp.clip(group_end - block_start, 0, MOE_BLOCK)
    blk = jnp.arange(n_blocks, dtype=jnp.int32)
    prev_expert = jnp.concatenate([jnp.full((1,), -1, jnp.int32), block_expert[:-1]])
    first = jnp.logical_and(block_valid > 0, block_expert != prev_expert)
    later_first = jnp.logical_and(first[None, :], blk[None, :] > blk[:, None])
    next_pos = jnp.min(jnp.where(later_first, blk[None, :], n_blocks), axis=1)
    next_expert = jnp.where(next_pos < n_blocks,
                            block_expert[jnp.minimum(next_pos, n_blocks - 1)], -1)
    slot = (jnp.cumsum(first.astype(jnp.int32)) - 1) % 2
    table = jnp.stack([block_expert, block_valid, first.astype(jnp.int32), next_expert, slot])
    return dest.astype(jnp.int32), table.astype(jnp.int32), n_rows


SC_CORES = 2
SC_SUBCORES = 16
SC_WORKERS = SC_CORES * SC_SUBCORES
SC_CHUNK = 64


def _sc_mesh():
    return plsc.VectorSubcoreMesh(core_axis_name="c", subcore_axis_name="s")


def _sc_worker():
    return lax.axis_index("s") * SC_CORES + lax.axis_index("c")


def _dispatch(h2, dest, n_rows):
    t, d = h2.shape
    per_w = t // SC_WORKERS
    n_chunks = per_w // SC_CHUNK
    assert per_w % (2 * SC_CHUNK) == 0
    idx = dest.reshape(SC_WORKERS, n_chunks, SC_CHUNK, TOP_K).transpose(0, 3, 1, 2)
    idx = idx.reshape(SC_WORKERS, TOP_K * n_chunks, SC_CHUNK)

    @functools.partial(
        pl.kernel, mesh=_sc_mesh(),
        out_type=jax.ShapeDtypeStruct((n_rows, d), h2.dtype),
        scratch_types=[pltpu.VMEM((TOP_K * n_chunks, SC_CHUNK), jnp.int32),
                       pltpu.VMEM((2, SC_CHUNK, d), h2.dtype),
                       pltpu.SemaphoreType.DMA((2,)),
                       pltpu.SemaphoreType.DMA((2,))],
        name="moe_dispatch")
    def run(h2_hbm, idx_hbm, xs_hbm, idx_v, rows_v, rsem, ssem):
        w = _sc_worker()
        pltpu.sync_copy(idx_hbm.at[w], idx_v)

        def read(g, b):
            src = h2_hbm.at[pl.ds(w * per_w + g * SC_CHUNK, SC_CHUNK)]
            return pltpu.make_async_copy(src, rows_v.at[b], rsem.at[b])

        def scatter(g, kk, b):
            dst = xs_hbm.at[idx_v.at[kk * n_chunks + g]]
            return pltpu.make_async_copy(rows_v.at[b], dst, ssem.at[b])

        read(0, 0).start()

        @pl.loop(0, n_chunks, step=2)
        def _(g0):
            for b in range(2):
                g = g0 + b
                read(g, b).wait()

                @pl.when(g + 1 < n_chunks)
                def _():
                    read(g + 1, 1 - b).start()

                for kk in range(TOP_K):
                    scatter(g, kk, b).start()
                for kk in range(TOP_K):
                    scatter(g, kk, b).wait()

    return run(h2, idx)


def _undispatch(ys, dest):
    t = dest.shape[0]
    d = ys.shape[1]
    n_out = t * TOP_K
    per_w = n_out // SC_WORKERS
    n_chunks = per_w // SC_CHUNK
    assert per_w % (2 * SC_CHUNK) == 0
    idx = dest.T.reshape(SC_WORKERS, n_chunks, SC_CHUNK)

    @functools.partial(
        pl.kernel, mesh=_sc_mesh(),
        out_type=jax.ShapeDtypeStruct((n_out, d), ys.dtype),
        scratch_types=[pltpu.VMEM((n_chunks, SC_CHUNK), jnp.int32),
                       pltpu.VMEM((2, SC_CHUNK, d), ys.dtype),
                       pltpu.SemaphoreType.DMA((2,)),
                       pltpu.SemaphoreType.DMA((2,))],
        name="moe_undispatch")
    def run(ys_hbm, idx_hbm, out_hbm, idx_v, rows_v, gsem, wsem):
        w = _sc_worker()
        pltpu.sync_copy(idx_hbm.at[w], idx_v)

        def gather(g, b):
            return pltpu.make_async_copy(ys_hbm.at[idx_v.at[g]], rows_v.at[b], gsem.at[b])

        def write(g, b):
            dst = out_hbm.at[pl.ds(w * per_w + g * SC_CHUNK, SC_CHUNK)]
            return pltpu.make_async_copy(rows_v.at[b], dst, wsem.at[b])

        gather(0, 0).start()

        @pl.loop(0, n_chunks, step=2)
        def _(g0):
            for b in range(2):
                g = g0 + b
                gather(g, b).wait()

                @pl.when(g >= 1)
                def _():
                    write(g - 1, 1 - b).wait()

                @pl.when(g + 1 < n_chunks)
                def _():
                    gather(g + 1, 1 - b).start()

                write(g, b).start()

        write(n_chunks - 1, (n_chunks - 1) % 2).wait()

    return run(ys, idx).reshape(TOP_K, t, d)


def kernel(x, c, positions, w_ada, b_ada, norm_mix_g, w_in, q_norm_g, w_uq, kv_norm_g, w_ukv,
           w_up_attn, conv_w, w_up_conv, w_o, norm_ffn_g, router_w, router_b, w_gu, b_gu,
           w_down, b_down, norm_final_g):
    batch, seq, d = x.shape
    t = batch * seq
    depth = w_ada.shape[0]
    x2 = x.reshape(t, d)
    cosf, sinf = _rope_tables(positions)
    c_pad = jnp.zeros((8, d), F32).at[:batch].set(c)

    for l in range(depth):
        ada = _ada(c_pad, w_ada[l], b_ada[l].reshape(1, -1))
        mod = ada[:batch].reshape(batch, 6, d)
        mod = jnp.concatenate([mod, jnp.zeros((batch, 2, d), F32)], axis=1)

        win2, wq2, wq2s, wk2, wv = _prep_weights(w_in[l], w_uq[l], w_ukv[l])
        q, k, v, sga, gc = _pre(x2, mod, norm_mix_g[l].reshape(1, d), win2,
                                q_norm_g[l].reshape(1, -1), wq2, wq2s,
                                kv_norm_g[l].reshape(1, -1), wk2, wv, cosf, sinf, conv_w[l],
                                w_up_conv[l].astype(BF16), seq)
        attn = _attention(q, k, v, batch, seq)

        rw_pad = jnp.concatenate([router_w[l], jnp.zeros((d, LANES - N_EXPERTS), F32)], axis=1)
        rb_pad = jnp.concatenate([router_b[l], jnp.full((LANES - N_EXPERTS,), NEG_BIG, F32)])
        rw_hi = rw_pad.astype(BF16)
        rw_lo = (rw_pad - rw_hi.astype(F32)).astype(BF16)
        x1, h2, idx_pad, gate_pad, rank_pad, counts = _post(
            attn, sga, gc, x2, mod, w_up_attn[l].astype(BF16), w_o[l].astype(BF16),
            norm_ffn_g[l].reshape(1, d), rw_hi, rw_lo, rb_pad.reshape(1, LANES), seq)

        dest, block_table, n_rows = _route(
            idx_pad[:, :TOP_K], rank_pad[:, :TOP_K], counts[0, :N_EXPERTS], t)
        xs = _dispatch(h2, dest, n_rows)
        ys = _moe(block_table, xs, w_gu[l], b_gu[l].reshape(N_EXPERTS, 1, -1),
                  w_down[l], b_down[l].reshape(N_EXPERTS, 1, -1))
        y_kt = _undispatch(ys, dest)
        x2 = _final(x1, y_kt, gate_pad, mod, norm_final_g.reshape(1, d), seq, l == depth - 1)

    return x2.reshape(batch, seq, d)
```

```python
import functools
import math

import jax
import jax.numpy as jnp
from jax import lax
from jax.experimental import pallas as pl
from jax.experimental.pallas import tpu as pltpu
from jax.experimental.pallas import tpu_sc as plsc

D_MODEL = 1024
CHUNK = 64
N_HEADS = 8
Q_LORA = 256
KV_LORA = 128
QK_NOPE = 64
QK_ROPE = 32
V_HEAD = 64
QK_HEAD = QK_NOPE + QK_ROPE
ROPE_THETA = 10000.0
CONV_WIDTH = 512
CONV_K = 3
N_EXPERTS = 32
TOP_K = 4
D_EXPERT = 1024
SWIGLU_LIMIT = 7.0
SWIGLU_ALPHA = 1.702
MOE_BLOCK = 256
RMS_EPS = 1e-6

LANES = 128
HEAD_PAD = 128
NEG_BIG = -1e30
VMEM_LIMIT = 56 * 1024 * 1024

F32 = jnp.float32
BF16 = jnp.bfloat16

Q_PRESCALE = (QK_HEAD ** -0.5) * math.log2(math.e)

ROW_TILE = 512
POST_TILE = 1024
POST_SUB = 1024
ATT_BLOCK = 512
ATT_WIDE = 2


def _rms(x, g):
    ms = jnp.mean(x * x, axis=-1, keepdims=True)
    return x * lax.rsqrt(ms + RMS_EPS) * g


def _dot(a, b):
    return jnp.dot(a, b, preferred_element_type=F32)


PACKED = D_MODEL // 2


def _pack_row(x):
    return pltpu.pack_elementwise([x[:, :PACKED], x[:, PACKED:]], packed_dtype=BF16)


def _unpack_row(w):
    half = lambda i: pltpu.unpack_elementwise(w, index=i, packed_dtype=BF16, unpacked_dtype=F32)
    return jnp.concatenate([half(0), half(1)], axis=-1)


def _ada_kernel(c_ref, w_ref, b_ref, o_ref):
    c = c_ref[...]
    ca = (c * jax.nn.sigmoid(c)).astype(BF16)
    o_ref[...] = _dot(ca, w_ref[...].astype(BF16)) + b_ref[...]


def _ada(c_pad, w_ada, b_ada):
    n = w_ada.shape[1]
    tn = 1024
    return pl.pallas_call(
        _ada_kernel,
        out_shape=jax.ShapeDtypeStruct((c_pad.shape[0], n), F32),
        grid=(n // tn,),
        in_specs=[
            pl.BlockSpec(c_pad.shape, lambda j: (0, 0)),
            pl.BlockSpec((D_MODEL, tn), lambda j: (0, j)),
            pl.BlockSpec((1, tn), lambda j: (0, j)),
        ],
        out_specs=pl.BlockSpec((c_pad.shape[0], tn), lambda j: (0, j)),
        compiler_params=pltpu.CompilerParams(
            dimension_semantics=("arbitrary",), vmem_limit_bytes=VMEM_LIMIT),
        name="ada",
    )(c_pad, w_ada, b_ada)


_C_QLAT = 0
_C_KVLAT = _C_QLAT + Q_LORA
_C_KPE_A = _C_KVLAT + KV_LORA
_C_KPE_B = _C_KPE_A + HEAD_PAD
_C_U = _C_KPE_B + HEAD_PAD
_C_C = _C_U + CONV_WIDTH
_C_B = _C_C + CONV_WIDTH
_C_GA = _C_B + CONV_WIDTH
_C_GC = _C_GA + D_MODEL
_C_END = _C_GC + D_MODEL


def _pre_kernel(tiles_per_seq, x_ref, mod_ref, g_ref, win_ref, qg_ref, wq_ref, wqs_ref,
                kvg_ref, wk_ref, wv_ref, cos_ref, sin_ref, cw_ref, wuc_ref,
                q_ref, k_ref, v_ref, sga_ref, gc_ref, carry_ref):
    i = pl.program_id(0)
    tm = x_ref.shape[0]
    mod = mod_ref[...]
    h = _rms(x_ref[...], g_ref[...]) * (1.0 + mod[1:2]) + mod[0:1]
    hb = h.astype(BF16)

    cosf = cos_ref[...]
    sinf = sin_ref[...]
    cos8 = jnp.concatenate([cosf] * N_HEADS, axis=-1)
    sin8 = jnp.concatenate([sinf] * N_HEADS, axis=-1)

    small = _dot(hb, win_ref[:, _C_QLAT:_C_U])
    q_lat = small[:, _C_QLAT:_C_KVLAT]
    kv_lat = small[:, _C_KVLAT:_C_KPE_A]
    kpe = small[:, _C_KPE_A:_C_KPE_B] * cosf + small[:, _C_KPE_B:_C_U] * sinf
    qn = _rms(q_lat, qg_ref[...]).astype(BF16)
    q = _dot(qn, wq_ref[...]) * cos8 + _dot(qn, wqs_ref[...]) * sin8
    q_ref[...] = (q * Q_PRESCALE).astype(BF16)
    kvn = _rms(kv_lat, kvg_ref[...]).astype(BF16)
    k = _dot(kvn, wk_ref[...]) + jnp.concatenate([kpe] * N_HEADS, axis=-1)
    k_ref[...] = k.astype(BF16)
    lane = lax.broadcasted_iota(jnp.int32, (tm, N_HEADS * HEAD_PAD), 1)
    ones_col = jnp.where(lane % HEAD_PAD == V_HEAD, 1.0, 0.0)
    v_ref[...] = (_dot(kvn, wv_ref[...]) + ones_col).astype(BF16)

    ucb = _dot(hb, win_ref[:, _C_U:_C_GA])
    cu = ucb[:, 0:CONV_WIDTH] * ucb[:, CONV_WIDTH:2 * CONV_WIDTH]
    b_gate = ucb[:, 2 * CONV_WIDTH:3 * CONV_WIDTH]

    @pl.when(i % tiles_per_seq == 0)
    def _():
        carry_ref[...] = jnp.zeros_like(carry_ref)

    prev = carry_ref[...]
    row = lax.broadcasted_iota(jnp.int32, cu.shape, 0)
    cu1 = jnp.where(row == 0, prev[7:8], pltpu.roll(cu, 1, 0))
    cu2 = jnp.where(row == 0, prev[6:7], jnp.where(row == 1, prev[7:8], pltpu.roll(cu, 2, 0)))
    cw = cw_ref[...]
    z = cw[2:3] * cu + cw[1:2] * cu1 + cw[0:1] * cu2
    carry_ref[...] = cu[tm - 8:tm]
    c_branch = _dot((b_gate * z).astype(BF16), wuc_ref[...])

    gates = _dot(hb, win_ref[:, _C_GA:_C_END])
    sga_ref[...] = jax.nn.sigmoid(gates[:, 0:D_MODEL]).astype(BF16)
    gc_ref[...] = (jax.nn.sigmoid(gates[:, D_MODEL:]) * c_branch).astype(BF16)


def _pre(x2, mod, norm_g, win2, q_norm_g, wq2, wq2s, kv_norm_g, wk2, wv, cosf, sinf, conv_w,
         w_up_conv, seq):
    t = x2.shape[0]
    tm = ROW_TILE
    tiles_per_seq = seq // tm
    full = lambda a: pl.BlockSpec(a.shape, lambda i: (0,) * a.ndim)
    rows = lambda w: pl.BlockSpec((tm, w), lambda i: (i, 0))
    outs = [jax.ShapeDtypeStruct((t, N_HEADS * HEAD_PAD), BF16),
            jax.ShapeDtypeStruct((t, N_HEADS * HEAD_PAD), BF16),
            jax.ShapeDtypeStruct((t, N_HEADS * HEAD_PAD), BF16),
            jax.ShapeDtypeStruct((t, D_MODEL), BF16),
            jax.ShapeDtypeStruct((t, D_MODEL), BF16)]
    return pl.pallas_call(
        functools.partial(_pre_kernel, tiles_per_seq),
        out_shape=outs,
        grid=(t // tm,),
        in_specs=[
            rows(D_MODEL),
            pl.BlockSpec((None, 8, D_MODEL), lambda i: (i // tiles_per_seq, 0, 0)),
            full(norm_g), full(win2), full(q_norm_g), full(wq2), full(wq2s),
            full(kv_norm_g), full(wk2), full(wv),
            rows(HEAD_PAD), rows(HEAD_PAD), full(conv_w), full(w_up_conv),
        ],
        out_specs=[rows(N_HEADS * HEAD_PAD), rows(N_HEADS * HEAD_PAD), rows(N_HEADS * HEAD_PAD),
                   rows(D_MODEL), rows(D_MODEL)],
        scratch_shapes=[pltpu.VMEM((8, CONV_WIDTH), F32)],
        compiler_params=pltpu.CompilerParams(
            dimension_semantics=("arbitrary",), vmem_limit_bytes=VMEM_LIMIT),
        name="pre_mixer",
    )(x2, mod, norm_g, win2, q_norm_g, wq2, wq2s, kv_norm_g, wk2, wv, cosf, sinf, conv_w,
      w_up_conv)


def _attn_kernel(q_ref, k_ref, v_ref, o_ref, m_ref, acc_ref):
    i = pl.program_id(1)
    tq = q_ref.shape[0]

    m_ref[...] = jnp.full_like(m_ref, NEG_BIG)
    acc_ref[...] = jnp.zeros_like(acc_ref)

    def step(k0, tk, masked):
        if masked:
            rq = lax.broadcasted_iota(jnp.int32, (tq, tk), 0) // CHUNK
            ck = lax.broadcasted_iota(jnp.int32, (tq, tk), 1) // CHUNK
            allowed = ck <= rq
        for hd in range(N_HEADS):
            hs = slice(hd * HEAD_PAD, (hd + 1) * HEAD_PAD)
            s = lax.dot_general(q_ref[:, hs], k_ref[pl.ds(k0, tk), hs],
                                (((1,), (1,)), ((), ())), preferred_element_type=F32)
            if masked:
                s = jnp.where(allowed, s, NEG_BIG)
            m_old = m_ref[hd]
            s_max = s[:, 0:LANES]
            for c in range(1, tk // LANES):
                s_max = jnp.maximum(s_max, s[:, c * LANES:(c + 1) * LANES])
            m_new = jnp.maximum(m_old, jnp.max(s_max, axis=-1, keepdims=True))
            alpha = jnp.exp2(m_old - m_new)
            p = jnp.concatenate(
                [jnp.exp2(s[:, c * LANES:(c + 1) * LANES] - m_new).astype(BF16)
                 for c in range(tk // LANES)], axis=-1)
            acc_ref[hd] = alpha * acc_ref[hd] + _dot(p, v_ref[pl.ds(k0, tk), hs])
            m_ref[hd] = m_new

    wide = ATT_WIDE * tq

    def body(j, carry):
        step(pl.multiple_of(j * wide, wide), wide, False)
        return carry

    lax.fori_loop(0, i // ATT_WIDE, body, 0)

    for r in range(1, ATT_WIDE):
        @pl.when(i % ATT_WIDE >= r)
        def _():
            step(pl.multiple_of((i - i % ATT_WIDE + r - 1) * tq, tq), tq, False)

    step(pl.multiple_of(i * tq, tq), tq, True)

    for hp in range(N_HEADS // 2):
        pair = []
        for hd in (2 * hp, 2 * hp + 1):
            acc = acc_ref[hd]
            pair.append(acc[:, 0:V_HEAD] / acc[:, V_HEAD:V_HEAD + 1])
        o_ref[:, hp * LANES:(hp + 1) * LANES] = jnp.concatenate(pair, axis=-1).astype(BF16)


def _attention(q, k, v, batch, seq):
    tq = ATT_BLOCK
    nq = seq // tq
    return pl.pallas_call(
        _attn_kernel,
        out_shape=jax.ShapeDtypeStruct((batch * seq, N_HEADS * V_HEAD), BF16),
        grid=(batch, nq),
        in_specs=[
            pl.BlockSpec((tq, N_HEADS * HEAD_PAD), lambda b, i: (b * nq + i, 0)),
            pl.BlockSpec((seq, N_HEADS * HEAD_PAD), lambda b, i: (b, 0)),
            pl.BlockSpec((seq, N_HEADS * HEAD_PAD), lambda b, i: (b, 0)),
        ],
        out_specs=pl.BlockSpec((tq, N_HEADS * V_HEAD), lambda b, i: (b * nq + i, 0)),
        scratch_shapes=[pltpu.VMEM((N_HEADS, tq, LANES), F32),
                        pltpu.VMEM((N_HEADS, tq, LANES), F32)],
        compiler_params=pltpu.CompilerParams(
            dimension_semantics=("arbitrary", "arbitrary"), vmem_limit_bytes=VMEM_LIMIT),
        name="attention",
    )(q, k, v)


def _post_kernel(attn_ref, sga_ref, gc_ref, x_ref, mod_ref, wua_ref, wo_ref, g_ref, rwh_ref,
                 rwl_ref, rb_ref, x1_ref, h2_ref, idx_ref, gate_ref, rank_ref, cnt_out_ref, cnt_ref):
    @pl.when(pl.program_id(0) == 0)
    def _():
        cnt_ref[...] = jnp.zeros_like(cnt_ref)

    counts = cnt_ref[...]
    for r0 in range(0, x_ref.shape[0], POST_SUB):
        counts = _post_rows(slice(r0, r0 + POST_SUB), counts, attn_ref, sga_ref, gc_ref, x_ref,
                            mod_ref, wua_ref, wo_ref, g_ref, rwh_ref, rwl_ref, rb_ref, x1_ref,
                            h2_ref, idx_ref, gate_ref, rank_ref)
    cnt_ref[...] = counts
    cnt_out_ref[...] = counts.astype(jnp.int32)


def _post_rows(rs, counts, attn_ref, sga_ref, gc_ref, x_ref, mod_ref, wua_ref, wo_ref, g_ref,
               rwh_ref, rwl_ref, rb_ref, x1_ref, h2_ref, idx_ref, gate_ref, rank_ref):
    mod = mod_ref[...]
    a_branch = _dot(attn_ref[rs, :], wua_ref[...])
    merged = sga_ref[rs, :].astype(F32) * a_branch + gc_ref[rs, :].astype(F32)
    mix = _dot(merged.astype(BF16), wo_ref[...])
    x1 = x_ref[rs, :] + mod[2:3] * mix
    x1_ref[rs, :] = x1
    h2 = _rms(x1, g_ref[...]) * (1.0 + mod[4:5]) + mod[3:4]
    h2_ref[rs, :] = _pack_row(h2)

    h_hi = h2.astype(BF16)
    h_lo = (h2 - h_hi.astype(F32)).astype(BF16)
    logits = (_dot(h_hi, rwh_ref[...]) + _dot(h_lo, rwh_ref[...]) + _dot(h_hi, rwl_ref[...])
              + rb_ref[...])
    lane = lax.broadcasted_iota(jnp.int32, logits.shape, 1)
    work = logits
    vals, idxs = [], []
    for _ in range(TOP_K):
        mk = jnp.max(work, axis=-1, keepdims=True)
        ik = jnp.min(jnp.where(work == mk, lane, LANES), axis=-1, keepdims=True)
        vals.append(mk)
        idxs.append(ik)
        work = jnp.where(lane == ik, -jnp.inf, work)
    es = [jnp.exp(vk - vals[0]) for vk in vals]
    denom = es[0] + es[1] + es[2] + es[3]
    tm = logits.shape[0]
    chosen = jnp.zeros(logits.shape, F32)
    for kk in range(TOP_K):
        chosen = chosen + jnp.where(lane == idxs[kk], 1.0, 0.0)
    r_i = lax.broadcasted_iota(jnp.int32, (tm, tm), 0)
    c_i = lax.broadcasted_iota(jnp.int32, (tm, tm), 1)
    earlier = jnp.where(c_i < r_i, 1.0, 0.0).astype(BF16)
    before = _dot(earlier, chosen.astype(BF16)) + counts[0:1]

    idx_out = jnp.zeros(logits.shape, jnp.int32)
    gate_out = jnp.zeros(logits.shape, F32)
    rank_out = jnp.zeros(logits.shape, jnp.int32)
    for kk in range(TOP_K):
        rank_k = jnp.sum(jnp.where(lane == idxs[kk], before, 0.0), axis=-1, keepdims=True)
        idx_out = jnp.where(lane == kk, idxs[kk], idx_out)
        gate_out = jnp.where(lane == kk, es[kk] / denom, gate_out)
        rank_out = jnp.where(lane == kk, rank_k.astype(jnp.int32), rank_out)
    idx_ref[rs, :] = idx_out
    gate_ref[rs, :] = gate_out
    rank_ref[rs, :] = rank_out
    return counts + jnp.sum(chosen, axis=0, keepdims=True)


def _post(attn, sga, gc, x2, mod, wua, wo, norm_g, rw_hi, rw_lo, rb_pad, seq):
    t = x2.shape[0]
    tm = POST_TILE
    tiles_per_seq = seq // tm
    full = lambda a: pl.BlockSpec(a.shape, lambda i: (0,) * a.ndim)
    rows = lambda w: pl.BlockSpec((tm, w), lambda i: (i, 0))
    outs = [jax.ShapeDtypeStruct((t, D_MODEL), F32),
            jax.ShapeDtypeStruct((t, PACKED), jnp.uint32),
            jax.ShapeDtypeStruct((t, LANES), jnp.int32),
            jax.ShapeDtypeStruct((t, LANES), F32),
            jax.ShapeDtypeStruct((t, LANES), jnp.int32),
            jax.ShapeDtypeStruct((8, LANES), jnp.int32)]
    return pl.pallas_call(
        _post_kernel,
        out_shape=outs,
        grid=(t // tm,),
        in_specs=[
            rows(N_HEADS * V_HEAD), rows(D_MODEL), rows(D_MODEL), rows(D_MODEL),
            pl.BlockSpec((None, 8, D_MODEL), lambda i: (i // tiles_per_seq, 0, 0)),
            full(wua), full(wo), full(norm_g), full(rw_hi), full(rw_lo), full(rb_pad),
        ],
        out_specs=[rows(D_MODEL), rows(PACKED), rows(LANES), rows(LANES), rows(LANES),
                   pl.BlockSpec((8, LANES), lambda i: (0, 0))],
        scratch_shapes=[pltpu.VMEM((8, LANES), F32)],
        compiler_params=pltpu.CompilerParams(
            dimension_semantics=("arbitrary",), vmem_limit_bytes=VMEM_LIMIT),
        name="post_mixer",
    )(attn, sga, gc, x2, mod, wua, wo, norm_g, rw_hi, rw_lo, rb_pad)


_TB_EXPERT, _TB_VALID, _TB_FIRST, _TB_NEXT, _TB_SLOT = range(5)


def _moe_kernel(tb_ref, xs_ref, wgu_hbm, bgu_ref, wd_hbm, bd_ref, o_ref,
                wgu_f, wd_f, wgu_bf, wd_bf, sem):
    b = pl.program_id(0)
    n_valid = tb_ref[_TB_VALID, b]
    used = n_valid > 0
    slot = tb_ref[_TB_SLOT, b]

    def weight_copies(expert, sl):
        return (pltpu.make_async_copy(wgu_hbm.at[expert], wgu_f.at[sl], sem.at[0, sl]),
                pltpu.make_async_copy(wd_hbm.at[expert], wd_f.at[sl], sem.at[1, sl]))

    @pl.when(b == 0)
    def _():
        for cp in weight_copies(tb_ref[_TB_EXPERT, 0], 0):
            cp.start()

    @pl.when(tb_ref[_TB_FIRST, b] == 1)
    def _():
        nxt = tb_ref[_TB_NEXT, b]

        @pl.when(nxt >= 0)
        def _():
            for cp in weight_copies(nxt, 1 - slot):
                cp.start(priority=1)

        for cp in weight_copies(tb_ref[_TB_EXPERT, b], slot):
            cp.wait()
        wgu_bf[...] = wgu_f[slot].astype(BF16)
        wd_bf[...] = wd_f[slot].astype(BF16)

    @pl.when(used)
    def _():
        row = lax.broadcasted_iota(jnp.int32, xs_ref.shape, 0)
        xs = _unpack_row(jnp.where(row < n_valid, xs_ref[...], 0)).astype(BF16)
        gu = _dot(xs, wgu_bf[...]) + bgu_ref[...]
        gate = jnp.minimum(gu[:, :D_EXPERT], SWIGLU_LIMIT)
        up = jnp.clip(gu[:, D_EXPERT:], -SWIGLU_LIMIT, SWIGLU_LIMIT)
        act = (up + 1.0) * (gate * jax.nn.sigmoid(gate * SWIGLU_ALPHA))
        o_ref[...] = _pack_row(_dot(act.astype(BF16), wd_bf[...]) + bd_ref[...])

    @pl.when(jnp.logical_not(used))
    def _():
        o_ref[...] = jnp.zeros_like(o_ref)


def _moe(block_table, xs, w_gu, b_gu, w_down, b_down):
    n_rows = xs.shape[0]
    n_blocks = n_rows // MOE_BLOCK
    grid_spec = pltpu.PrefetchScalarGridSpec(
        num_scalar_prefetch=1,
        grid=(n_blocks,),
        in_specs=[
            pl.BlockSpec((MOE_BLOCK, PACKED), lambda b, tb: (b, 0)),
            pl.BlockSpec(memory_space=pl.ANY),
            pl.BlockSpec((None, 1, 2 * D_EXPERT), lambda b, tb: (tb[_TB_EXPERT, b], 0, 0)),
            pl.BlockSpec(memory_space=pl.ANY),
            pl.BlockSpec((None, 1, D_MODEL), lambda b, tb: (tb[_TB_EXPERT, b], 0, 0)),
        ],
        out_specs=pl.BlockSpec((MOE_BLOCK, PACKED), lambda b, tb: (b, 0)),
        scratch_shapes=[pltpu.VMEM((2, D_MODEL, 2 * D_EXPERT), F32),
                        pltpu.VMEM((2, D_EXPERT, D_MODEL), F32),
                        pltpu.VMEM((D_MODEL, 2 * D_EXPERT), BF16),
                        pltpu.VMEM((D_EXPERT, D_MODEL), BF16),
                        pltpu.SemaphoreType.DMA((2, 2))],
    )
    return pl.pallas_call(
        _moe_kernel,
        out_shape=jax.ShapeDtypeStruct((n_rows, PACKED), jnp.uint32),
        grid_spec=grid_spec,
        compiler_params=pltpu.CompilerParams(
            dimension_semantics=("arbitrary",), vmem_limit_bytes=VMEM_LIMIT),
        name="moe_experts",
    )(block_table, xs, w_gu, b_gu, w_down, b_down)


def _final_kernel(last_layer, x1_ref, y_ref, gate_ref, mod_ref, g_ref, *rest):
    o_ref = rest[-1]
    mod = mod_ref[...]
    gate = gate_ref[...]
    ffn = gate[:, 0:1] * _unpack_row(y_ref[0])
    for kk in range(1, TOP_K):
        ffn = ffn + gate[:, kk:kk + 1] * _unpack_row(y_ref[kk])
    x = x1_ref[...] + mod[5:6] * ffn
    o_ref[...] = _rms(x, g_ref[...]) if last_layer else x


def _final(x1, y_kt, gate, mod, norm_g, seq, last_layer, part, prev_out):
    t = x1.shape[0]
    tm = ROW_TILE
    tiles_per_seq = seq // tm
    tiles = y_kt.shape[1] // tm
    first = part * tiles
    rows = lambda w: pl.BlockSpec((tm, w), lambda i: (first + i, 0))
    in_specs = [
        rows(D_MODEL), pl.BlockSpec((TOP_K, tm, PACKED), lambda i: (0, i, 0)), rows(LANES),
        pl.BlockSpec((None, 8, D_MODEL), lambda i: ((first + i) // tiles_per_seq, 0, 0)),
        pl.BlockSpec(norm_g.shape, lambda i: (0, 0)),
    ]
    args = [x1, y_kt, gate, mod, norm_g]
    aliases = {}
    if prev_out is not None:
        in_specs.append(pl.BlockSpec(memory_space=pl.ANY))
        args.append(prev_out)
        aliases = {len(args) - 1: 0}
    return pl.pallas_call(
        functools.partial(_final_kernel, last_layer),
        out_shape=jax.ShapeDtypeStruct((t, D_MODEL), F32),
        grid=(tiles,),
        in_specs=in_specs,
        out_specs=rows(D_MODEL),
        input_output_aliases=aliases,
        compiler_params=pltpu.CompilerParams(
            dimension_semantics=("arbitrary",), vmem_limit_bytes=VMEM_LIMIT),
        name="combine_final",
    )(*args)


def _swap_halves(w):
    half = w.shape[-1] // 2
    return jnp.concatenate([w[..., half:], w[..., :half]], axis=-1)


def _prep_weights(w_in, w_uq, w_ukv):
    d = w_in.shape[0]
    splits = (Q_LORA, KV_LORA, QK_ROPE, CONV_WIDTH, CONV_WIDTH, CONV_WIDTH, D_MODEL, D_MODEL)
    offs = [0]
    for s in splits:
        offs.append(offs[-1] + s)
    part = lambda n: w_in[:, offs[n]:offs[n + 1]]
    z = lambda n: jnp.zeros((d, n), w_in.dtype)
    w_kpe = part(2)
    kpe_a = jnp.concatenate([z(QK_NOPE), w_kpe, z(HEAD_PAD - QK_HEAD)], axis=1)
    kpe_b = jnp.concatenate([z(QK_NOPE), _swap_halves(w_kpe), z(HEAD_PAD - QK_HEAD)], axis=1)
    win2 = jnp.concatenate([part(0), part(1), kpe_a, kpe_b, part(3), part(4), part(5), part(6),
                            part(7)], axis=1).astype(BF16)

    wq = w_uq.reshape(Q_LORA, N_HEADS, QK_HEAD)
    zq = lambda n: jnp.zeros((Q_LORA, N_HEADS, n), w_uq.dtype)
    wq2 = jnp.concatenate([wq, zq(HEAD_PAD - QK_HEAD)], axis=-1)
    wq2s = jnp.concatenate([zq(QK_NOPE), _swap_halves(wq[..., QK_NOPE:]), zq(HEAD_PAD - QK_HEAD)],
                           axis=-1)
    wq2 = wq2.reshape(Q_LORA, N_HEADS * HEAD_PAD).astype(BF16)
    wq2s = wq2s.reshape(Q_LORA, N_HEADS * HEAD_PAD).astype(BF16)

    wkv = w_ukv.reshape(KV_LORA, N_HEADS, QK_NOPE + V_HEAD)
    wk2 = jnp.concatenate([wkv[..., :QK_NOPE],
                           jnp.zeros((KV_LORA, N_HEADS, HEAD_PAD - QK_NOPE), w_ukv.dtype)], axis=-1)
    wk2 = wk2.reshape(KV_LORA, N_HEADS * HEAD_PAD).astype(BF16)
    wv = jnp.concatenate([wkv[..., QK_NOPE:],
                          jnp.zeros((KV_LORA, N_HEADS, HEAD_PAD - V_HEAD), w_ukv.dtype)], axis=-1)
    wv = wv.reshape(KV_LORA, N_HEADS * HEAD_PAD).astype(BF16)
    return win2, wq2, wq2s, wk2, wv


def _rope_tables(positions):
    inv_freq = 1.0 / (ROPE_THETA ** (jnp.arange(0, QK_ROPE, 2, dtype=F32) / QK_ROPE))
    ang = positions.astype(F32).reshape(-1, 1) * inv_freq
    cos, sin = jnp.cos(ang), jnp.sin(ang)
    t = ang.shape[0]
    ones = jnp.ones((t, QK_NOPE), F32)
    zpad = jnp.zeros((t, HEAD_PAD - QK_HEAD), F32)
    cosf = jnp.concatenate([ones, cos, cos, zpad], axis=1)
    sinf = jnp.concatenate([jnp.zeros((t, QK_NOPE), F32), -sin, sin, zpad], axis=1)
    return cosf, sinf


def _route(top_idx, rank, counts, n_tokens):
    padded = (counts + MOE_BLOCK - 1) // MOE_BLOCK * MOE_BLOCK
    pad_end = jnp.cumsum(padded)
    pad_start = pad_end - padded
    experts = jnp.arange(N_EXPERTS, dtype=jnp.int32)
    start_of = jnp.sum(jnp.where(top_idx[:, :, None] == experts, pad_start, 0), axis=-1)
    dest = start_of + rank
    n_rows = n_tokens * TOP_K + N_EXPERTS * MOE_BLOCK
    n_blocks = n_rows // MOE_BLOCK
    block_start = jnp.arange(n_blocks, dtype=jnp.int32) * MOE_BLOCK
    block_expert = jnp.minimum(
        (pad_end[None, :] <= block_start[:, None]).astype(jnp.int32).sum(axis=1), N_EXPERTS - 1)
    group_end = (pad_start + counts)[block_expert]
    block_valid = jnp.clip(group_end - block_start, 0, MOE_BLOCK)
    blk = jnp.arange(n_blocks, dtype=jnp.int32)
    prev_expert = jnp.concatenate([jnp.full((1,), -1, jnp.int32), block_expert[:-1]])
    first = jnp.logical_and(block_valid > 0, block_expert != prev_expert)
    later_first = jnp.logical_and(first[None, :], blk[None, :] > blk[:, None])
    next_pos = jnp.min(jnp.where(later_first, blk[None, :], n_blocks), axis=1)
    next_expert = jnp.where(next_pos < n_blocks,
                            block_expert[jnp.minimum(next_pos, n_blocks - 1)], -1)
    slot = (jnp.cumsum(first.astype(jnp.int32)) - 1) % 2
    table = jnp.stack([block_expert, block_valid, first.astype(jnp.int32), next_expert, slot])
    return dest.astype(jnp.int32), table.astype(jnp.int32), n_rows


SC_CORES = 2
SC_SUBCORES = 16
SC_WORKERS = SC_CORES * SC_SUBCORES
SC_CHUNK = 64
COMBINE_PARTS = 2


def _sc_mesh():
    return plsc.VectorSubcoreMesh(core_axis_name="c", subcore_axis_name="s")


def _sc_worker():
    return lax.axis_index("s") * SC_CORES + lax.axis_index("c")


def _dispatch(h2, dest, n_rows):
    t, d = h2.shape
    per_w = t // SC_WORKERS
    n_chunks = per_w // SC_CHUNK
    assert per_w % (2 * SC_CHUNK) == 0
    idx = dest.reshape(SC_WORKERS, n_chunks, SC_CHUNK, TOP_K).transpose(0, 3, 1, 2)
    idx = idx.reshape(SC_WORKERS, TOP_K * n_chunks, SC_CHUNK)

    @functools.partial(
        pl.kernel, mesh=_sc_mesh(),
        out_type=jax.ShapeDtypeStruct((n_rows, d), h2.dtype),
        scratch_types=[pltpu.VMEM((TOP_K * n_chunks, SC_CHUNK), jnp.int32),
                       pltpu.VMEM((2, SC_CHUNK, d), h2.dtype),
                       pltpu.SemaphoreType.DMA((2,)),
                       pltpu.SemaphoreType.DMA((2,))],
        name="moe_dispatch")
    def run(h2_hbm, idx_hbm, xs_hbm, idx_v, rows_v, rsem, ssem):
        w = _sc_worker()
        pltpu.sync_copy(idx_hbm.at[w], idx_v)

        def read(g, b):
            src = h2_hbm.at[pl.ds(w * per_w + g * SC_CHUNK, SC_CHUNK)]
            return pltpu.make_async_copy(src, rows_v.at[b], rsem.at[b])

        def scatter(g, kk, b):
            dst = xs_hbm.at[idx_v.at[kk * n_chunks + g]]
            return pltpu.make_async_copy(rows_v.at[b], dst, ssem.at[b])

        read(0, 0).start()

        @pl.loop(0, n_chunks, step=2)
        def _(g0):
            for b in range(2):
                g = g0 + b
                read(g, b).wait()

                @pl.when(g + 1 < n_chunks)
                def _():
                    read(g + 1, 1 - b).start()

                for kk in range(TOP_K):
                    scatter(g, kk, b).start()
                for kk in range(TOP_K):
                    scatter(g, kk, b).wait()

    return run(h2, idx)


def _undispatch(ys, dest):
    t = dest.shape[0]
    d = ys.shape[1]
    n_out = t * TOP_K
    per_w = n_out // SC_WORKERS
    n_chunks = per_w // SC_CHUNK
    assert per_w % (2 * SC_CHUNK) == 0
    idx = dest.T.reshape(SC_WORKERS, n_chunks, SC_CHUNK)

    @functools.partial(
        pl.kernel, mesh=_sc_mesh(),
        out_type=jax.ShapeDtypeStruct((n_out, d), ys.dtype),
        scratch_types=[pltpu.VMEM((n_chunks, SC_CHUNK), jnp.int32),
                       pltpu.VMEM((2, SC_CHUNK, d), ys.dtype),
                       pltpu.SemaphoreType.DMA((2,)),
                       pltpu.SemaphoreType.DMA((2,))],
        name="moe_undispatch")
    def run(ys_hbm, idx_hbm, out_hbm, idx_v, rows_v, gsem, wsem):
        w = _sc_worker()
        pltpu.sync_copy(idx_hbm.at[w], idx_v)

        def gather(g, b):
            return pltpu.make_async_copy(ys_hbm.at[idx_v.at[g]], rows_v.at[b], gsem.at[b])

        def write(g, b):
            dst = out_hbm.at[pl.ds(w * per_w + g * SC_CHUNK, SC_CHUNK)]
            return pltpu.make_async_copy(rows_v.at[b], dst, wsem.at[b])

        gather(0, 0).start()

        @pl.loop(0, n_chunks, step=2)
        def _(g0):
            for b in range(2):
                g = g0 + b
                gather(g, b).wait()

                @pl.when(g >= 1)
                def _():
                    write(g - 1, 1 - b).wait()

                @pl.when(g + 1 < n_chunks)
                def _():
                    gather(g + 1, 1 - b).start()

                write(g, b).start()

        write(n_chunks - 1, (n_chunks - 1) % 2).wait()

    return run(ys, idx).reshape(TOP_K, t, d)


def kernel(x, c, positions, w_ada, b_ada, norm_mix_g, w_in, q_norm_g, w_uq, kv_norm_g, w_ukv,
           w_up_attn, conv_w, w_up_conv, w_o, norm_ffn_g, router_w, router_b, w_gu, b_gu,
           w_down, b_down, norm_final_g):
    batch, seq, d = x.shape
    t = batch * seq
    depth = w_ada.shape[0]
    x2 = x.reshape(t, d)
    cosf, sinf = _rope_tables(positions)
    c_pad = jnp.zeros((8, d), F32).at[:batch].set(c)

    for l in range(depth):
        ada = _ada(c_pad, w_ada[l], b_ada[l].reshape(1, -1))
        mod = ada[:batch].reshape(batch, 6, d)
        mod = jnp.concatenate([mod, jnp.zeros((batch, 2, d), F32)], axis=1)

        win2, wq2, wq2s, wk2, wv = _prep_weights(w_in[l], w_uq[l], w_ukv[l])
        q, k, v, sga, gc = _pre(x2, mod, norm_mix_g[l].reshape(1, d), win2,
                                q_norm_g[l].reshape(1, -1), wq2, wq2s,
                                kv_norm_g[l].reshape(1, -1), wk2, wv, cosf, sinf, conv_w[l],
                                w_up_conv[l].astype(BF16), seq)
        attn = _attention(q, k, v, batch, seq)

        rw_pad = jnp.concatenate([router_w[l], jnp.zeros((d, LANES - N_EXPERTS), F32)], axis=1)
        rb_pad = jnp.concatenate([router_b[l], jnp.full((LANES - N_EXPERTS,), NEG_BIG, F32)])
        rw_hi = rw_pad.astype(BF16)
        rw_lo = (rw_pad - rw_hi.astype(F32)).astype(BF16)
        x1, h2, idx_pad, gate_pad, rank_pad, counts = _post(
            attn, sga, gc, x2, mod, w_up_attn[l].astype(BF16), w_o[l].astype(BF16),
            norm_ffn_g[l].reshape(1, d), rw_hi, rw_lo, rb_pad.reshape(1, LANES), seq)

        dest, block_table, n_rows = _route(
            idx_pad[:, :TOP_K], rank_pad[:, :TOP_K], counts[0, :N_EXPERTS], t)
        xs = _dispatch(h2, dest, n_rows)
        ys = _moe(block_table, xs, w_gu[l], b_gu[l].reshape(N_EXPERTS, 1, -1),
                  w_down[l], b_down[l].reshape(N_EXPERTS, 1, -1))
        x2 = None
        per_part = t // COMBINE_PARTS
        for part in range(COMBINE_PARTS):
            y_kt = _undispatch(ys, dest[part * per_part:(part + 1) * per_part])
            x2 = _final(x1, y_kt, gate_pad, mod, norm_final_g.reshape(1, d), seq,
                        l == depth - 1, part, x2)

    return x2.reshape(batch, seq, d)
```

```python
import functools
import math

import jax
import jax.numpy as jnp
from jax import lax
from jax.experimental import pallas as pl
from jax.experimental.pallas import tpu as pltpu
from jax.experimental.pallas import tpu_sc as plsc

D_MODEL = 1024
CHUNK = 64
N_HEADS = 8
Q_LORA = 256
KV_LORA = 128
QK_NOPE = 64
QK_ROPE = 32
V_HEAD = 64
QK_HEAD = QK_NOPE + QK_ROPE
ROPE_THETA = 10000.0
CONV_WIDTH = 512
CONV_K = 3
N_EXPERTS = 32
TOP_K = 4
D_EXPERT = 1024
SWIGLU_LIMIT = 7.0
SWIGLU_ALPHA = 1.702
MOE_BLOCK = 256
RMS_EPS = 1e-6

LANES = 128
HEAD_PAD = 128
NEG_BIG = -1e30
VMEM_LIMIT = 56 * 1024 * 1024

F32 = jnp.float32
BF16 = jnp.bfloat16

Q_PRESCALE = (QK_HEAD ** -0.5) * math.log2(math.e)

ROW_TILE = 512
POST_TILE = 1024
POST_SUB = 1024
ATT_BLOCK = 512
ATT_WIDE = 2


def _rms(x, g):
    ms = jnp.mean(x * x, axis=-1, keepdims=True)
    return x * lax.rsqrt(ms + RMS_EPS) * g


def _dot(a, b):
    return jnp.dot(a, b, preferred_element_type=F32)


PACKED = D_MODEL // 2


def _pack_row(x):
    return pltpu.pack_elementwise([x[:, :PACKED], x[:, PACKED:]], packed_dtype=BF16)


def _unpack_row(w):
    half = lambda i: pltpu.unpack_elementwise(w, index=i, packed_dtype=BF16, unpacked_dtype=F32)
    return jnp.concatenate([half(0), half(1)], axis=-1)


def _ada_kernel(c_ref, w_ref, b_ref, o_ref):
    c = c_ref[...]
    ca = (c * jax.nn.sigmoid(c)).astype(BF16)
    o_ref[...] = _dot(ca, w_ref[...].astype(BF16)) + b_ref[...]


def _ada(c_pad, w_ada, b_ada):
    n = w_ada.shape[1]
    tn = 1024
    return pl.pallas_call(
        _ada_kernel,
        out_shape=jax.ShapeDtypeStruct((c_pad.shape[0], n), F32),
        grid=(n // tn,),
        in_specs=[
            pl.BlockSpec(c_pad.shape, lambda j: (0, 0)),
            pl.BlockSpec((D_MODEL, tn), lambda j: (0, j)),
            pl.BlockSpec((1, tn), lambda j: (0, j)),
        ],
        out_specs=pl.BlockSpec((c_pad.shape[0], tn), lambda j: (0, j)),
        compiler_params=pltpu.CompilerParams(
            dimension_semantics=("arbitrary",), vmem_limit_bytes=VMEM_LIMIT),
        name="ada",
    )(c_pad, w_ada, b_ada)


_C_QLAT = 0
_C_KVLAT = _C_QLAT + Q_LORA
_C_KPE_A = _C_KVLAT + KV_LORA
_C_KPE_B = _C_KPE_A + HEAD_PAD
_C_U = _C_KPE_B + HEAD_PAD


def _pre_kernel(tiles_per_seq, x_ref, mod_ref, g_ref, wlat_ref, wconv_ref, wgate_ref, qg_ref,
                wq_ref, wqs_ref, kvg_ref, wk_ref, wv_ref, pos_ref, freq_ref, cw_ref, wuc_ref,
                q_ref, k_ref, v_ref, sga_ref, gc_ref, carry_ref):
    i = pl.program_id(0)
    tm = x_ref.shape[0]
    mod = mod_ref[...]
    h = _rms(x_ref[...], g_ref[...]) * (1.0 + mod[1:2]) + mod[0:1]
    hb = h.astype(BF16)

    lane_h = lax.broadcasted_iota(jnp.int32, (tm, HEAD_PAD), 1)
    ang = pos_ref[...] * freq_ref[...]
    cosf = jnp.where(lane_h < QK_HEAD, jnp.cos(ang), 0.0)
    sin_a = jnp.sin(ang)
    sinf = jnp.where(lane_h < QK_NOPE + QK_ROPE // 2, -sin_a, sin_a)
    cos8 = jnp.concatenate([cosf] * N_HEADS, axis=-1)
    sin8 = jnp.concatenate([sinf] * N_HEADS, axis=-1)

    small = _dot(hb, wlat_ref[...])
    q_lat = small[:, _C_QLAT:_C_KVLAT]
    kv_lat = small[:, _C_KVLAT:_C_KPE_A]
    kpe = small[:, _C_KPE_A:_C_KPE_B] * cosf + small[:, _C_KPE_B:_C_U] * sinf
    qn = _rms(q_lat, qg_ref[...]).astype(BF16)
    q = _dot(qn, wq_ref[...]) * cos8 + _dot(qn, wqs_ref[...]) * sin8
    q_ref[...] = (q * Q_PRESCALE).astype(BF16)
    kvn = _rms(kv_lat, kvg_ref[...]).astype(BF16)
    k = _dot(kvn, wk_ref[...]) + jnp.concatenate([kpe] * N_HEADS, axis=-1)
    k_ref[...] = k.astype(BF16)
    lane = lax.broadcasted_iota(jnp.int32, (tm, N_HEADS * HEAD_PAD), 1)
    ones_col = jnp.where(lane % HEAD_PAD == V_HEAD, 1.0, 0.0)
    v_ref[...] = (_dot(kvn, wv_ref[...]) + ones_col).astype(BF16)

    ucb = _dot(hb, wconv_ref[...])
    cu = ucb[:, 0:CONV_WIDTH] * ucb[:, CONV_WIDTH:2 * CONV_WIDTH]
    b_gate = ucb[:, 2 * CONV_WIDTH:3 * CONV_WIDTH]

    @pl.when(i % tiles_per_seq == 0)
    def _():
        carry_ref[...] = jnp.zeros_like(carry_ref)

    prev = carry_ref[...]
    row = lax.broadcasted_iota(jnp.int32, cu.shape, 0)
    cu1 = jnp.where(row == 0, prev[7:8], pltpu.roll(cu, 1, 0))
    cu2 = jnp.where(row == 0, prev[6:7], jnp.where(row == 1, prev[7:8], pltpu.roll(cu, 2, 0)))
    cw = cw_ref[...]
    z = cw[2:3] * cu + cw[1:2] * cu1 + cw[0:1] * cu2
    carry_ref[...] = cu[tm - 8:tm]
    c_branch = _dot((b_gate * z).astype(BF16), wuc_ref[...])

    gates = _dot(hb, wgate_ref[...])
    sga_ref[...] = jax.nn.sigmoid(gates[:, 0:D_MODEL]).astype(BF16)
    gc_ref[...] = (jax.nn.sigmoid(gates[:, D_MODEL:]) * c_branch).astype(BF16)


def _pre(x2, mod, norm_g, w_lat, w_conv, w_gate, q_norm_g, wq2, wq2s, kv_norm_g, wk2, wv, pos,
         freqs, conv_w, w_up_conv, seq):
    t = x2.shape[0]
    tm = ROW_TILE
    tiles_per_seq = seq // tm
    full = lambda a: pl.BlockSpec(a.shape, lambda i: (0,) * a.ndim)
    rows = lambda w: pl.BlockSpec((tm, w), lambda i: (i, 0))
    outs = [jax.ShapeDtypeStruct((t, N_HEADS * HEAD_PAD), BF16),
            jax.ShapeDtypeStruct((t, N_HEADS * HEAD_PAD), BF16),
            jax.ShapeDtypeStruct((t, N_HEADS * HEAD_PAD), BF16),
            jax.ShapeDtypeStruct((t, D_MODEL), BF16),
            jax.ShapeDtypeStruct((t, D_MODEL), BF16)]
    return pl.pallas_call(
        functools.partial(_pre_kernel, tiles_per_seq),
        out_shape=outs,
        grid=(t // tm,),
        in_specs=[
            rows(D_MODEL),
            pl.BlockSpec((None, 8, D_MODEL), lambda i: (i // tiles_per_seq, 0, 0)),
            full(norm_g), full(w_lat), full(w_conv), full(w_gate), full(q_norm_g), full(wq2),
            full(wq2s), full(kv_norm_g), full(wk2), full(wv),
            rows(1), full(freqs), full(conv_w), full(w_up_conv),
        ],
        out_specs=[rows(N_HEADS * HEAD_PAD), rows(N_HEADS * HEAD_PAD), rows(N_HEADS * HEAD_PAD),
                   rows(D_MODEL), rows(D_MODEL)],
        scratch_shapes=[pltpu.VMEM((8, CONV_WIDTH), F32)],
        compiler_params=pltpu.CompilerParams(
            dimension_semantics=("arbitrary",), vmem_limit_bytes=VMEM_LIMIT),
        name="pre_mixer",
    )(x2, mod, norm_g, w_lat, w_conv, w_gate, q_norm_g, wq2, wq2s, kv_norm_g, wk2, wv, pos,
      freqs, conv_w, w_up_conv)


def _attn_kernel(q_ref, k_ref, v_ref, o_ref, m_ref, acc_ref):
    i = pl.program_id(1)
    tq = q_ref.shape[0]

    m_ref[...] = jnp.full_like(m_ref, NEG_BIG)
    acc_ref[...] = jnp.zeros_like(acc_ref)

    def step(k0, tk, masked):
        if masked:
            rq = lax.broadcasted_iota(jnp.int32, (tq, tk), 0) // CHUNK
            ck = lax.broadcasted_iota(jnp.int32, (tq, tk), 1) // CHUNK
            allowed = ck <= rq
        for hd in range(N_HEADS):
            hs = slice(hd * HEAD_PAD, (hd + 1) * HEAD_PAD)
            s = lax.dot_general(q_ref[:, hs], k_ref[pl.ds(k0, tk), hs],
                                (((1,), (1,)), ((), ())), preferred_element_type=F32)
            if masked:
                s = jnp.where(allowed, s, NEG_BIG)
            m_old = m_ref[hd]
            s_max = s[:, 0:LANES]
            for c in range(1, tk // LANES):
                s_max = jnp.maximum(s_max, s[:, c * LANES:(c + 1) * LANES])
            m_new = jnp.maximum(m_old, jnp.max(s_max, axis=-1, keepdims=True))
            alpha = jnp.exp2(m_old - m_new)
            p = jnp.concatenate(
                [jnp.exp2(s[:, c * LANES:(c + 1) * LANES] - m_new).astype(BF16)
                 for c in range(tk // LANES)], axis=-1)
            acc_ref[hd] = alpha * acc_ref[hd] + _dot(p, v_ref[pl.ds(k0, tk), hs])
            m_ref[hd] = m_new

    wide = ATT_WIDE * tq

    def body(j, carry):
        step(pl.multiple_of(j * wide, wide), wide, False)
        return carry

    lax.fori_loop(0, i // ATT_WIDE, body, 0)

    for r in range(1, ATT_WIDE):
        @pl.when(i % ATT_WIDE >= r)
        def _():
            step(pl.multiple_of((i - i % ATT_WIDE + r - 1) * tq, tq), tq, False)

    step(pl.multiple_of(i * tq, tq), tq, True)

    for hp in range(N_HEADS // 2):
        pair = []
        for hd in (2 * hp, 2 * hp + 1):
            acc = acc_ref[hd]
            pair.append(acc[:, 0:V_HEAD] / acc[:, V_HEAD:V_HEAD + 1])
        o_ref[:, hp * LANES:(hp + 1) * LANES] = jnp.concatenate(pair, axis=-1).astype(BF16)


def _attention(q, k, v, batch, seq):
    tq = ATT_BLOCK
    nq = seq // tq
    return pl.pallas_call(
        _attn_kernel,
        out_shape=jax.ShapeDtypeStruct((batch * seq, N_HEADS * V_HEAD), BF16),
        grid=(batch, nq),
        in_specs=[
            pl.BlockSpec((tq, N_HEADS * HEAD_PAD), lambda b, i: (b * nq + i, 0)),
            pl.BlockSpec((seq, N_HEADS * HEAD_PAD), lambda b, i: (b, 0)),
            pl.BlockSpec((seq, N_HEADS * HEAD_PAD), lambda b, i: (b, 0)),
        ],
        out_specs=pl.BlockSpec((tq, N_HEADS * V_HEAD), lambda b, i: (b * nq + i, 0)),
        scratch_shapes=[pltpu.VMEM((N_HEADS, tq, LANES), F32),
                        pltpu.VMEM((N_HEADS, tq, LANES), F32)],
        compiler_params=pltpu.CompilerParams(
            dimension_semantics=("arbitrary", "arbitrary"), vmem_limit_bytes=VMEM_LIMIT),
        name="attention",
    )(q, k, v)


def _post_kernel(attn_ref, sga_ref, gc_ref, x_ref, mod_ref, wua_ref, wo_ref, g_ref, rwh_ref,
                 rwl_ref, rb_ref, x1_ref, h2_ref, idx_ref, gate_ref, rank_ref, cnt_out_ref, cnt_ref):
    @pl.when(pl.program_id(0) == 0)
    def _():
        cnt_ref[...] = jnp.zeros_like(cnt_ref)

    counts = cnt_ref[...]
    for r0 in range(0, x_ref.shape[0], POST_SUB):
        counts = _post_rows(slice(r0, r0 + POST_SUB), counts, attn_ref, sga_ref, gc_ref, x_ref,
                            mod_ref, wua_ref, wo_ref, g_ref, rwh_ref, rwl_ref, rb_ref, x1_ref,
                            h2_ref, idx_ref, gate_ref, rank_ref)
    cnt_ref[...] = counts
    cnt_out_ref[...] = counts.astype(jnp.int32)


def _post_rows(rs, counts, attn_ref, sga_ref, gc_ref, x_ref, mod_ref, wua_ref, wo_ref, g_ref,
               rwh_ref, rwl_ref, rb_ref, x1_ref, h2_ref, idx_ref, gate_ref, rank_ref):
    mod = mod_ref[...]
    a_branch = _dot(attn_ref[rs, :], wua_ref[...])
    merged = sga_ref[rs, :].astype(F32) * a_branch + gc_ref[rs, :].astype(F32)
    mix = _dot(merged.astype(BF16), wo_ref[...])
    x1 = x_ref[rs, :] + mod[2:3] * mix
    x1_ref[rs, :] = x1
    h2 = _rms(x1, g_ref[...]) * (1.0 + mod[4:5]) + mod[3:4]
    h2_ref[rs, :] = _pack_row(h2)

    h_hi = h2.astype(BF16)
    h_lo = (h2 - h_hi.astype(F32)).astype(BF16)
    logits = (_dot(h_hi, rwh_ref[...]) + _dot(h_lo, rwh_ref[...]) + _dot(h_hi, rwl_ref[...])
              + rb_ref[...])
    lane = lax.broadcasted_iota(jnp.int32, logits.shape, 1)
    work = logits
    vals, idxs = [], []
    for _ in range(TOP_K):
        mk = jnp.max(work, axis=-1, keepdims=True)
        ik = jnp.min(jnp.where(work == mk, lane, LANES), axis=-1, keepdims=True)
        vals.append(mk)
        idxs.append(ik)
        work = jnp.where(lane == ik, -jnp.inf, work)
    es = [jnp.exp(vk - vals[0]) for vk in vals]
    denom = es[0] + es[1] + es[2] + es[3]
    tm = logits.shape[0]
    chosen = jnp.zeros(logits.shape, F32)
    for kk in range(TOP_K):
        chosen = chosen + jnp.where(lane == idxs[kk], 1.0, 0.0)
    r_i = lax.broadcasted_iota(jnp.int32, (tm, tm), 0)
    c_i = lax.broadcasted_iota(jnp.int32, (tm, tm), 1)
    earlier = jnp.where(c_i < r_i, 1.0, 0.0).astype(BF16)
    before = _dot(earlier, chosen.astype(BF16)) + counts[0:1]

    idx_out = jnp.zeros(logits.shape, jnp.int32)
    gate_out = jnp.zeros(logits.shape, F32)
    rank_out = jnp.zeros(logits.shape, jnp.int32)
    for kk in range(TOP_K):
        rank_k = jnp.sum(jnp.where(lane == idxs[kk], before, 0.0), axis=-1, keepdims=True)
        idx_out = jnp.where(lane == kk, idxs[kk], idx_out)
        gate_out = jnp.where(lane == kk, es[kk] / denom, gate_out)
        rank_out = jnp.where(lane == kk, rank_k.astype(jnp.int32), rank_out)
    idx_ref[rs, :] = idx_out
    gate_ref[rs, :] = gate_out
    rank_ref[rs, :] = rank_out
    return counts + jnp.sum(chosen, axis=0, keepdims=True)


def _post(attn, sga, gc, x2, mod, wua, wo, norm_g, rw_hi, rw_lo, rb_pad, seq):
    t = x2.shape[0]
    tm = POST_TILE
    tiles_per_seq = seq // tm
    full = lambda a: pl.BlockSpec(a.shape, lambda i: (0,) * a.ndim)
    rows = lambda w: pl.BlockSpec((tm, w), lambda i: (i, 0))
    outs = [jax.ShapeDtypeStruct((t, D_MODEL), F32),
            jax.ShapeDtypeStruct((t, PACKED), jnp.uint32),
            jax.ShapeDtypeStruct((t, LANES), jnp.int32),
            jax.ShapeDtypeStruct((t, LANES), F32),
            jax.ShapeDtypeStruct((t, LANES), jnp.int32),
            jax.ShapeDtypeStruct((8, LANES), jnp.int32)]
    return pl.pallas_call(
        _post_kernel,
        out_shape=outs,
        grid=(t // tm,),
        in_specs=[
            rows(N_HEADS * V_HEAD), rows(D_MODEL), rows(D_MODEL), rows(D_MODEL),
            pl.BlockSpec((None, 8, D_MODEL), lambda i: (i // tiles_per_seq, 0, 0)),
            full(wua), full(wo), full(norm_g), full(rw_hi), full(rw_lo), full(rb_pad),
        ],
        out_specs=[rows(D_MODEL), rows(PACKED), rows(LANES), rows(LANES), rows(LANES),
                   pl.BlockSpec((8, LANES), lambda i: (0, 0))],
        scratch_shapes=[pltpu.VMEM((8, LANES), F32)],
        compiler_params=pltpu.CompilerParams(
            dimension_semantics=("arbitrary",), vmem_limit_bytes=VMEM_LIMIT),
        name="post_mixer",
    )(attn, sga, gc, x2, mod, wua, wo, norm_g, rw_hi, rw_lo, rb_pad)


_TB_EXPERT, _TB_VALID, _TB_FIRST, _TB_NEXT, _TB_SLOT = range(5)


def _moe_kernel(tb_ref, xs_ref, wgu_hbm, bgu_ref, wd_hbm, bd_ref, o_ref,
                wgu_f, wd_f, wgu_bf, wd_bf, sem):
    b = pl.program_id(0)
    n_valid = tb_ref[_TB_VALID, b]
    used = n_valid > 0
    slot = tb_ref[_TB_SLOT, b]

    def weight_copies(expert, sl):
        return (pltpu.make_async_copy(wgu_hbm.at[expert], wgu_f.at[sl], sem.at[0, sl]),
                pltpu.make_async_copy(wd_hbm.at[expert], wd_f.at[sl], sem.at[1, sl]))

    @pl.when(b == 0)
    def _():
        for cp in weight_copies(tb_ref[_TB_EXPERT, 0], 0):
            cp.start()

    @pl.when(tb_ref[_TB_FIRST, b] == 1)
    def _():
        nxt = tb_ref[_TB_NEXT, b]

        @pl.when(nxt >= 0)
        def _():
            for cp in weight_copies(nxt, 1 - slot):
                cp.start(priority=1)

        for cp in weight_copies(tb_ref[_TB_EXPERT, b], slot):
            cp.wait()
        wgu_bf[...] = wgu_f[slot].astype(BF16)
        wd_bf[...] = wd_f[slot].astype(BF16)

    @pl.when(used)
    def _():
        row = lax.broadcasted_iota(jnp.int32, xs_ref.shape, 0)
        xs = _unpack_row(jnp.where(row < n_valid, xs_ref[...], 0)).astype(BF16)
        gu = _dot(xs, wgu_bf[...]) + bgu_ref[...]
        gate = jnp.minimum(gu[:, :D_EXPERT], SWIGLU_LIMIT)
        up = jnp.clip(gu[:, D_EXPERT:], -SWIGLU_LIMIT, SWIGLU_LIMIT)
        act = (up + 1.0) * (gate * jax.nn.sigmoid(gate * SWIGLU_ALPHA))
        o_ref[...] = _pack_row(_dot(act.astype(BF16), wd_bf[...]) + bd_ref[...])

    @pl.when(jnp.logical_not(used))
    def _():
        o_ref[...] = jnp.zeros_like(o_ref)


def _moe(block_table, xs, w_gu, b_gu, w_down, b_down):
    n_rows = xs.shape[0]
    n_blocks = n_rows // MOE_BLOCK
    grid_spec = pltpu.PrefetchScalarGridSpec(
        num_scalar_prefetch=1,
        grid=(n_blocks,),
        in_specs=[
            pl.BlockSpec((MOE_BLOCK, PACKED), lambda b, tb: (b, 0)),
            pl.BlockSpec(memory_space=pl.ANY),
            pl.BlockSpec((None, 1, 2 * D_EXPERT), lambda b, tb: (tb[_TB_EXPERT, b], 0, 0)),
            pl.BlockSpec(memory_space=pl.ANY),
            pl.BlockSpec((None, 1, D_MODEL), lambda b, tb: (tb[_TB_EXPERT, b], 0, 0)),
        ],
        out_specs=pl.BlockSpec((MOE_BLOCK, PACKED), lambda b, tb: (b, 0)),
        scratch_shapes=[pltpu.VMEM((2, D_MODEL, 2 * D_EXPERT), F32),
                        pltpu.VMEM((2, D_EXPERT, D_MODEL), F32),
                        pltpu.VMEM((D_MODEL, 2 * D_EXPERT), BF16),
                        pltpu.VMEM((D_EXPERT, D_MODEL), BF16),
                        pltpu.SemaphoreType.DMA((2, 2))],
    )
    return pl.pallas_call(
        _moe_kernel,
        out_shape=jax.ShapeDtypeStruct((n_rows, PACKED), jnp.uint32),
        grid_spec=grid_spec,
        compiler_params=pltpu.CompilerParams(
            dimension_semantics=("arbitrary",), vmem_limit_bytes=VMEM_LIMIT),
        name="moe_experts",
    )(block_table, xs, w_gu, b_gu, w_down, b_down)


def _final_kernel(last_layer, x1_ref, y_ref, gate_ref, mod_ref, g_ref, o_ref):
    mod = mod_ref[...]
    gate = gate_ref[...]
    ffn = gate[:, 0:1] * _unpack_row(y_ref[0])
    for kk in range(1, TOP_K):
        ffn = ffn + gate[:, kk:kk + 1] * _unpack_row(y_ref[kk])
    x = x1_ref[...] + mod[5:6] * ffn
    o_ref[...] = _rms(x, g_ref[...]) if last_layer else x


def _final(x1, y_kt, gate, mod, norm_g, seq, last_layer):
    t = x1.shape[0]
    tm = ROW_TILE
    tiles_per_seq = seq // tm
    rows = lambda w: pl.BlockSpec((tm, w), lambda i: (i, 0))
    return pl.pallas_call(
        functools.partial(_final_kernel, last_layer),
        out_shape=jax.ShapeDtypeStruct((t, D_MODEL), F32),
        grid=(t // tm,),
        in_specs=[
            rows(D_MODEL), pl.BlockSpec((TOP_K, tm, PACKED), lambda i: (0, i, 0)), rows(LANES),
            pl.BlockSpec((None, 8, D_MODEL), lambda i: (i // tiles_per_seq, 0, 0)),
            pl.BlockSpec(norm_g.shape, lambda i: (0, 0)),
        ],
        out_specs=rows(D_MODEL),
        compiler_params=pltpu.CompilerParams(
            dimension_semantics=("arbitrary",), vmem_limit_bytes=VMEM_LIMIT),
        name="combine_final",
    )(x1, y_kt, gate, mod, norm_g)


def _swap_halves(w):
    half = w.shape[-1] // 2
    return jnp.concatenate([w[..., half:], w[..., :half]], axis=-1)


def _prep_weights(w_in, w_uq, w_ukv):
    d = w_in.shape[0]
    splits = (Q_LORA, KV_LORA, QK_ROPE, CONV_WIDTH, CONV_WIDTH, CONV_WIDTH, D_MODEL, D_MODEL)
    offs = [0]
    for s in splits:
        offs.append(offs[-1] + s)
    part = lambda n: w_in[:, offs[n]:offs[n + 1]]
    z = lambda n: jnp.zeros((d, n), w_in.dtype)
    w_kpe = part(2)
    kpe_a = jnp.concatenate([z(QK_NOPE), w_kpe, z(HEAD_PAD - QK_HEAD)], axis=1)
    kpe_b = jnp.concatenate([z(QK_NOPE), _swap_halves(w_kpe), z(HEAD_PAD - QK_HEAD)], axis=1)
    w_lat = jnp.concatenate([part(0), part(1), kpe_a, kpe_b], axis=1).astype(BF16)
    w_conv = w_in[:, offs[3]:offs[6]].astype(BF16)
    w_gate = w_in[:, offs[6]:offs[8]].astype(BF16)

    wq = w_uq.reshape(Q_LORA, N_HEADS, QK_HEAD)
    zq = lambda n: jnp.zeros((Q_LORA, N_HEADS, n), w_uq.dtype)
    wq2 = jnp.concatenate([wq, zq(HEAD_PAD - QK_HEAD)], axis=-1)
    wq2s = jnp.concatenate([zq(QK_NOPE), _swap_halves(wq[..., QK_NOPE:]), zq(HEAD_PAD - QK_HEAD)],
                           axis=-1)
    wq2 = wq2.reshape(Q_LORA, N_HEADS * HEAD_PAD).astype(BF16)
    wq2s = wq2s.reshape(Q_LORA, N_HEADS * HEAD_PAD).astype(BF16)

    wkv = w_ukv.reshape(KV_LORA, N_HEADS, QK_NOPE + V_HEAD)
    wk2 = jnp.concatenate([wkv[..., :QK_NOPE],
                           jnp.zeros((KV_LORA, N_HEADS, HEAD_PAD - QK_NOPE), w_ukv.dtype)], axis=-1)
    wk2 = wk2.reshape(KV_LORA, N_HEADS * HEAD_PAD).astype(BF16)
    wv = jnp.concatenate([wkv[..., QK_NOPE:],
                          jnp.zeros((KV_LORA, N_HEADS, HEAD_PAD - V_HEAD), w_ukv.dtype)], axis=-1)
    wv = wv.reshape(KV_LORA, N_HEADS * HEAD_PAD).astype(BF16)
    return w_lat, w_conv, w_gate, wq2, wq2s, wk2, wv


def _rope_freqs():
    inv_freq = 1.0 / (ROPE_THETA ** (jnp.arange(0, QK_ROPE, 2, dtype=F32) / QK_ROPE))
    return jnp.concatenate([jnp.zeros((QK_NOPE,), F32), inv_freq, inv_freq,
                            jnp.zeros((HEAD_PAD - QK_HEAD,), F32)]).reshape(1, HEAD_PAD)


def _route(top_idx, rank, counts, n_tokens):
    padded = (counts + MOE_BLOCK - 1) // MOE_BLOCK * MOE_BLOCK
    pad_end = jnp.cumsum(padded)
    pad_start = pad_end - padded
    experts = jnp.arange(N_EXPERTS, dtype=jnp.int32)
    start_of = jnp.sum(jnp.where(top_idx[:, :, None] == experts, pad_start, 0), axis=-1)
    dest = start_of + rank
    n_rows = n_tokens * TOP_K + N_EXPERTS * MOE_BLOCK
    n_blocks = n_rows // MOE_BLOCK
    block_start = jnp.arange(n_blocks, dtype=jnp.int32) * MOE_BLOCK
    block_expert = jnp.minimum(
        (pad_end[None, :] <= block_start[:, None]).astype(jnp.int32).sum(axis=1), N_EXPERTS - 1)
    group_end = (pad_start + counts)[block_expert]
    block_valid = jnp.clip(group_end - block_start, 0, MOE_BLOCK)
    blk = jnp.arange(n_blocks, dtype=jnp.int32)
    prev_expert = jnp.concatenate([jnp.full((1,), -1, jnp.int32), block_expert[:-1]])
    first = jnp.logical_and(block_valid > 0, block_expert != prev_expert)
    first_pos = jnp.where(first, blk, n_blocks)
    at_or_after = jnp.flip(lax.cummin(jnp.flip(first_pos)))
    next_pos = jnp.concatenate([at_or_after[1:], jnp.full((1,), n_blocks, jnp.int32)])
    next_expert = jnp.where(next_pos < n_blocks,
                            block_expert[jnp.minimum(next_pos, n_blocks - 1)], -1)
    slot = (jnp.cumsum(first.astype(jnp.int32)) - 1) % 2
    table = jnp.stack([block_expert, block_valid, first.astype(jnp.int32), next_expert, slot])
    return dest.astype(jnp.int32), table.astype(jnp.int32), n_rows


SC_CORES = 2
SC_SUBCORES = 16
SC_WORKERS = SC_CORES * SC_SUBCORES
SC_CHUNK = 64

def _sc_mesh():
    return plsc.VectorSubcoreMesh(core_axis_name="c", subcore_axis_name="s")


def _sc_worker():
    return lax.axis_index("s") * SC_CORES + lax.axis_index("c")


def _dispatch(h2, dest, n_rows):
    t, d = h2.shape
    per_w = t // SC_WORKERS
    n_chunks = per_w // SC_CHUNK
    assert per_w % (2 * SC_CHUNK) == 0
    idx = dest.reshape(SC_WORKERS, n_chunks, SC_CHUNK, TOP_K).transpose(0, 3, 1, 2)
    idx = idx.reshape(SC_WORKERS, TOP_K * n_chunks, SC_CHUNK)

    @functools.partial(
        pl.kernel, mesh=_sc_mesh(),
        out_type=jax.ShapeDtypeStruct((n_rows, d), h2.dtype),
        scratch_types=[pltpu.VMEM((TOP_K * n_chunks, SC_CHUNK), jnp.int32),
                       pltpu.VMEM((2, SC_CHUNK, d), h2.dtype),
                       pltpu.SemaphoreType.DMA((2,)),
                       pltpu.SemaphoreType.DMA((2,))],
        name="moe_dispatch")
    def run(h2_hbm, idx_hbm, xs_hbm, idx_v, rows_v, rsem, ssem):
        w = _sc_worker()
        pltpu.sync_copy(idx_hbm.at[w], idx_v)

        def read(g, b):
            src = h2_hbm.at[pl.ds(w * per_w + g * SC_CHUNK, SC_CHUNK)]
            return pltpu.make_async_copy(src, rows_v.at[b], rsem.at[b])

        def scatter(g, kk, b):
            dst = xs_hbm.at[idx_v.at[kk * n_chunks + g]]
            return pltpu.make_async_copy(rows_v.at[b], dst, ssem.at[b])

        read(0, 0).start()

        @pl.loop(0, n_chunks, step=2)
        def _(g0):
            for b in range(2):
                g = g0 + b
                read(g, b).wait()

                @pl.when(g + 1 < n_chunks)
                def _():
                    read(g + 1, 1 - b).start()

                for kk in range(TOP_K):
                    scatter(g, kk, b).start()
                for kk in range(TOP_K):
                    scatter(g, kk, b).wait()

    return run(h2, idx)


def _undispatch(ys, dest):
    t = dest.shape[0]
    d = ys.shape[1]
    n_out = t * TOP_K
    per_w = n_out // SC_WORKERS
    n_chunks = per_w // SC_CHUNK
    assert per_w % (2 * SC_CHUNK) == 0
    idx = dest.T.reshape(SC_WORKERS, n_chunks, SC_CHUNK)

    @functools.partial(
        pl.kernel, mesh=_sc_mesh(),
        out_type=jax.ShapeDtypeStruct((n_out, d), ys.dtype),
        scratch_types=[pltpu.VMEM((n_chunks, SC_CHUNK), jnp.int32),
                       pltpu.VMEM((2, SC_CHUNK, d), ys.dtype),
                       pltpu.SemaphoreType.DMA((2,)),
                       pltpu.SemaphoreType.DMA((2,))],
        name="moe_undispatch")
    def run(ys_hbm, idx_hbm, out_hbm, idx_v, rows_v, gsem, wsem):
        w = _sc_worker()
        pltpu.sync_copy(idx_hbm.at[w], idx_v)

        def gather(g, b):
            return pltpu.make_async_copy(ys_hbm.at[idx_v.at[g]], rows_v.at[b], gsem.at[b])

        def write(g, b):
            dst = out_hbm.at[pl.ds(w * per_w + g * SC_CHUNK, SC_CHUNK)]
            return pltpu.make_async_copy(rows_v.at[b], dst, wsem.at[b])

        gather(0, 0).start()

        @pl.loop(0, n_chunks, step=2)
        def _(g0):
            for b in range(2):
                g = g0 + b
                gather(g, b).wait()

                @pl.when(g >= 1)
                def _():
                    write(g - 1, 1 - b).wait()

                @pl.when(g + 1 < n_chunks)
                def _():
                    gather(g + 1, 1 - b).start()

                write(g, b).start()

        write(n_chunks - 1, (n_chunks - 1) % 2).wait()

    return run(ys, idx).reshape(TOP_K, t, d)


def kernel(x, c, positions, w_ada, b_ada, norm_mix_g, w_in, q_norm_g, w_uq, kv_norm_g, w_ukv,
           w_up_attn, conv_w, w_up_conv, w_o, norm_ffn_g, router_w, router_b, w_gu, b_gu,
           w_down, b_down, norm_final_g):
    batch, seq, d = x.shape
    t = batch * seq
    depth = w_ada.shape[0]
    x2 = x.reshape(t, d)
    pos = positions.astype(F32).reshape(t, 1)
    freqs = _rope_freqs()
    c_pad = jnp.zeros((8, d), F32).at[:batch].set(c)

    for l in range(depth):
        ada = _ada(c_pad, w_ada[l], b_ada[l].reshape(1, -1))
        mod = ada[:batch].reshape(batch, 6, d)
        mod = jnp.concatenate([mod, jnp.zeros((batch, 2, d), F32)], axis=1)

        w_lat, w_conv, w_gate, wq2, wq2s, wk2, wv = _prep_weights(w_in[l], w_uq[l], w_ukv[l])
        q, k, v, sga, gc = _pre(x2, mod, norm_mix_g[l].reshape(1, d), w_lat, w_conv, w_gate,
                                q_norm_g[l].reshape(1, -1), wq2, wq2s,
                                kv_norm_g[l].reshape(1, -1), wk2, wv, pos, freqs, conv_w[l],
                                w_up_conv[l].astype(BF16), seq)
        attn = _attention(q, k, v, batch, seq)

        rw_pad = jnp.concatenate([router_w[l], jnp.zeros((d, LANES - N_EXPERTS), F32)], axis=1)
        rb_pad = jnp.concatenate([router_b[l], jnp.full((LANES - N_EXPERTS,), NEG_BIG, F32)])
        rw_hi = rw_pad.astype(BF16)
        rw_lo = (rw_pad - rw_hi.astype(F32)).astype(BF16)
        x1, h2, idx_pad, gate_pad, rank_pad, counts = _post(
            attn, sga, gc, x2, mod, w_up_attn[l].astype(BF16), w_o[l].astype(BF16),
            norm_ffn_g[l].reshape(1, d), rw_hi, rw_lo, rb_pad.reshape(1, LANES), seq)

        dest, block_table, n_rows = _route(
            idx_pad[:, :TOP_K], rank_pad[:, :TOP_K], counts[0, :N_EXPERTS], t)
        xs = _dispatch(h2, dest, n_rows)
        ys = _moe(block_table, xs, w_gu[l], b_gu[l].reshape(N_EXPERTS, 1, -1),
                  w_down[l], b_down[l].reshape(N_EXPERTS, 1, -1))
        y_kt = _undispatch(ys, dest)
        x2 = _final(x1, y_kt, gate_pad, mod, norm_final_g.reshape(1, d), seq, l == depth - 1)

    return x2.reshape(batch, seq, d)
```

```python
import functools
import math

import jax
import jax.numpy as jnp
from jax import lax
from jax.experimental import pallas as pl
from jax.experimental.pallas import tpu as pltpu
from jax.experimental.pallas import tpu_sc as plsc

D_MODEL = 1024
CHUNK = 64
N_HEADS = 8
Q_LORA = 256
KV_LORA = 128
QK_NOPE = 64
QK_ROPE = 32
V_HEAD = 64
QK_HEAD = QK_NOPE + QK_ROPE
ROPE_THETA = 10000.0
CONV_WIDTH = 512
CONV_K = 3
N_EXPERTS = 32
TOP_K = 4
D_EXPERT = 1024
SWIGLU_LIMIT = 7.0
SWIGLU_ALPHA = 1.702
MOE_BLOCK = 256
RMS_EPS = 1e-6

LANES = 128
HEAD_PAD = 128
NEG_BIG = -1e30
VMEM_LIMIT = 56 * 1024 * 1024

F32 = jnp.float32
BF16 = jnp.bfloat16

Q_PRESCALE = (QK_HEAD ** -0.5) * math.log2(math.e)

ROW_TILE = 512
POST_TILE = 1024
POST_SUB = 1024
ATT_BLOCK = 512
ATT_WIDE = 2


def _rms(x, g):
    ms = jnp.mean(x * x, axis=-1, keepdims=True)
    return x * lax.rsqrt(ms + RMS_EPS) * g


def _dot(a, b):
    return jnp.dot(a, b, preferred_element_type=F32)


PACKED = D_MODEL // 2


def _pack_row(x):
    return pltpu.pack_elementwise([x[:, :PACKED], x[:, PACKED:]], packed_dtype=BF16)


def _unpack_row(w):
    half = lambda i: pltpu.unpack_elementwise(w, index=i, packed_dtype=BF16, unpacked_dtype=F32)
    return jnp.concatenate([half(0), half(1)], axis=-1)


def _ada_kernel(c_ref, w_ref, b_ref, o_ref):
    c = c_ref[...]
    ca = (c * jax.nn.sigmoid(c)).astype(BF16)
    o_ref[...] = _dot(ca, w_ref[...].astype(BF16)) + b_ref[...]


def _ada(c_pad, w_ada, b_ada):
    n = w_ada.shape[1]
    tn = 1024
    return pl.pallas_call(
        _ada_kernel,
        out_shape=jax.ShapeDtypeStruct((c_pad.shape[0], n), F32),
        grid=(n // tn,),
        in_specs=[
            pl.BlockSpec(c_pad.shape, lambda j: (0, 0)),
            pl.BlockSpec((D_MODEL, tn), lambda j: (0, j)),
            pl.BlockSpec((1, tn), lambda j: (0, j)),
        ],
        out_specs=pl.BlockSpec((c_pad.shape[0], tn), lambda j: (0, j)),
        compiler_params=pltpu.CompilerParams(
            dimension_semantics=("arbitrary",), vmem_limit_bytes=VMEM_LIMIT),
        name="ada",
    )(c_pad, w_ada, b_ada)


_C_QLAT = 0
_C_KVLAT = _C_QLAT + Q_LORA
_C_KPE_A = _C_KVLAT + KV_LORA
_C_KPE_B = _C_KPE_A + HEAD_PAD
_C_U = _C_KPE_B + HEAD_PAD


def _pre_kernel(tiles_per_seq, x_ref, mod_ref, g_ref, wlat_ref, wconv_ref, wgate_ref, qg_ref,
                wq_ref, wqs_ref, kvg_ref, wk_ref, wv_ref, pos_ref, freq_ref, cw_ref, wuc_ref,
                q_ref, k_ref, v_ref, sga_ref, gc_ref, carry_ref):
    i = pl.program_id(0)
    tm = x_ref.shape[0]
    mod = mod_ref[...]
    h = _rms(x_ref[...], g_ref[...]) * (1.0 + mod[1:2]) + mod[0:1]
    hb = h.astype(BF16)

    lane_h = lax.broadcasted_iota(jnp.int32, (tm, HEAD_PAD), 1)
    ang = pos_ref[...] * freq_ref[...]
    cosf = jnp.where(lane_h < QK_HEAD, jnp.cos(ang), 0.0)
    sin_a = jnp.sin(ang)
    sinf = jnp.where(lane_h < QK_NOPE + QK_ROPE // 2, -sin_a, sin_a)
    cos8 = jnp.concatenate([cosf] * N_HEADS, axis=-1)
    sin8 = jnp.concatenate([sinf] * N_HEADS, axis=-1)

    small = _dot(hb, wlat_ref[...])
    q_lat = small[:, _C_QLAT:_C_KVLAT]
    kv_lat = small[:, _C_KVLAT:_C_KPE_A]
    kpe = small[:, _C_KPE_A:_C_KPE_B] * cosf + small[:, _C_KPE_B:_C_U] * sinf
    qn = _rms(q_lat, qg_ref[...]).astype(BF16)
    q = _dot(qn, wq_ref[...]) * cos8 + _dot(qn, wqs_ref[...]) * sin8
    q_ref[...] = (q * Q_PRESCALE).astype(BF16)
    kvn = _rms(kv_lat, kvg_ref[...]).astype(BF16)
    k = _dot(kvn, wk_ref[...]) + jnp.concatenate([kpe] * N_HEADS, axis=-1)
    k_ref[...] = k.astype(BF16)
    lane = lax.broadcasted_iota(jnp.int32, (tm, N_HEADS * HEAD_PAD), 1)
    ones_col = jnp.where(lane % HEAD_PAD == V_HEAD, 1.0, 0.0)
    v_ref[...] = (_dot(kvn, wv_ref[...]) + ones_col).astype(BF16)

    ucb = _dot(hb, wconv_ref[...])
    cu = ucb[:, 0:CONV_WIDTH] * ucb[:, CONV_WIDTH:2 * CONV_WIDTH]
    b_gate = ucb[:, 2 * CONV_WIDTH:3 * CONV_WIDTH]

    @pl.when(i % tiles_per_seq == 0)
    def _():
        carry_ref[...] = jnp.zeros_like(carry_ref)

    prev = carry_ref[...]
    row = lax.broadcasted_iota(jnp.int32, cu.shape, 0)
    cu1 = jnp.where(row == 0, prev[7:8], pltpu.roll(cu, 1, 0))
    cu2 = jnp.where(row == 0, prev[6:7], jnp.where(row == 1, prev[7:8], pltpu.roll(cu, 2, 0)))
    cw = cw_ref[...]
    z = cw[2:3] * cu + cw[1:2] * cu1 + cw[0:1] * cu2
    carry_ref[...] = cu[tm - 8:tm]
    c_branch = _dot((b_gate * z).astype(BF16), wuc_ref[...])

    gates = _dot(hb, wgate_ref[...])
    sga_ref[...] = jax.nn.sigmoid(gates[:, 0:D_MODEL]).astype(BF16)
    gc_ref[...] = (jax.nn.sigmoid(gates[:, D_MODEL:]) * c_branch).astype(BF16)


def _pre(x2, mod, norm_g, w_lat, w_conv, w_gate, q_norm_g, wq2, wq2s, kv_norm_g, wk2, wv, pos,
         freqs, conv_w, w_up_conv, seq):
    t = x2.shape[0]
    tm = ROW_TILE
    tiles_per_seq = seq // tm
    full = lambda a: pl.BlockSpec(a.shape, lambda i: (0,) * a.ndim)
    rows = lambda w: pl.BlockSpec((tm, w), lambda i: (i, 0))
    outs = [jax.ShapeDtypeStruct((t, N_HEADS * HEAD_PAD), BF16),
            jax.ShapeDtypeStruct((t, N_HEADS * HEAD_PAD), BF16),
            jax.ShapeDtypeStruct((t, N_HEADS * HEAD_PAD), BF16),
            jax.ShapeDtypeStruct((t, D_MODEL), BF16),
            jax.ShapeDtypeStruct((t, D_MODEL), BF16)]
    return pl.pallas_call(
        functools.partial(_pre_kernel, tiles_per_seq),
        out_shape=outs,
        grid=(t // tm,),
        in_specs=[
            rows(D_MODEL),
            pl.BlockSpec((None, 8, D_MODEL), lambda i: (i // tiles_per_seq, 0, 0)),
            full(norm_g), full(w_lat), full(w_conv), full(w_gate), full(q_norm_g), full(wq2),
            full(wq2s), full(kv_norm_g), full(wk2), full(wv),
            rows(1), full(freqs), full(conv_w), full(w_up_conv),
        ],
        out_specs=[rows(N_HEADS * HEAD_PAD), rows(N_HEADS * HEAD_PAD), rows(N_HEADS * HEAD_PAD),
                   rows(D_MODEL), rows(D_MODEL)],
        scratch_shapes=[pltpu.VMEM((8, CONV_WIDTH), F32)],
        compiler_params=pltpu.CompilerParams(
            dimension_semantics=("arbitrary",), vmem_limit_bytes=VMEM_LIMIT),
        name="pre_mixer",
    )(x2, mod, norm_g, w_lat, w_conv, w_gate, q_norm_g, wq2, wq2s, kv_norm_g, wk2, wv, pos,
      freqs, conv_w, w_up_conv)


def _attn_kernel(q_ref, k_ref, v_ref, o_ref, m_ref, acc_ref):
    i = pl.program_id(1)
    tq = q_ref.shape[0]

    m_ref[...] = jnp.full_like(m_ref, NEG_BIG)
    acc_ref[...] = jnp.zeros_like(acc_ref)

    def step(k0, tk, masked):
        if masked:
            rq = lax.broadcasted_iota(jnp.int32, (tq, tk), 0) // CHUNK
            ck = lax.broadcasted_iota(jnp.int32, (tq, tk), 1) // CHUNK
            allowed = ck <= rq
        for hd in range(N_HEADS):
            hs = slice(hd * HEAD_PAD, (hd + 1) * HEAD_PAD)
            s = lax.dot_general(q_ref[:, hs], k_ref[pl.ds(k0, tk), hs],
                                (((1,), (1,)), ((), ())), preferred_element_type=F32)
            if masked:
                s = jnp.where(allowed, s, NEG_BIG)
            m_old = m_ref[hd]
            s_max = s[:, 0:LANES]
            for c in range(1, tk // LANES):
                s_max = jnp.maximum(s_max, s[:, c * LANES:(c + 1) * LANES])
            m_new = jnp.maximum(m_old, jnp.max(s_max, axis=-1, keepdims=True))
            alpha = jnp.exp2(m_old - m_new)
            p = jnp.concatenate(
                [jnp.exp2(s[:, c * LANES:(c + 1) * LANES] - m_new).astype(BF16)
                 for c in range(tk // LANES)], axis=-1)
            acc_ref[hd] = alpha * acc_ref[hd] + _dot(p, v_ref[pl.ds(k0, tk), hs])
            m_ref[hd] = m_new

    wide = ATT_WIDE * tq

    def body(j, carry):
        step(pl.multiple_of(j * wide, wide), wide, False)
        return carry

    lax.fori_loop(0, i // ATT_WIDE, body, 0)

    for r in range(1, ATT_WIDE):
        @pl.when(i % ATT_WIDE >= r)
        def _():
            step(pl.multiple_of((i - i % ATT_WIDE + r - 1) * tq, tq), tq, False)

    step(pl.multiple_of(i * tq, tq), tq, True)

    for hp in range(N_HEADS // 2):
        pair = []
        for hd in (2 * hp, 2 * hp + 1):
            acc = acc_ref[hd]
            pair.append(acc[:, 0:V_HEAD] / acc[:, V_HEAD:V_HEAD + 1])
        o_ref[:, hp * LANES:(hp + 1) * LANES] = jnp.concatenate(pair, axis=-1).astype(BF16)


def _attention(q, k, v, batch, seq):
    tq = ATT_BLOCK
    nq = seq // tq
    return pl.pallas_call(
        _attn_kernel,
        out_shape=jax.ShapeDtypeStruct((batch * seq, N_HEADS * V_HEAD), BF16),
        grid=(batch, nq),
        in_specs=[
            pl.BlockSpec((tq, N_HEADS * HEAD_PAD), lambda b, i: (b * nq + i, 0)),
            pl.BlockSpec((seq, N_HEADS * HEAD_PAD), lambda b, i: (b, 0)),
            pl.BlockSpec((seq, N_HEADS * HEAD_PAD), lambda b, i: (b, 0)),
        ],
        out_specs=pl.BlockSpec((tq, N_HEADS * V_HEAD), lambda b, i: (b * nq + i, 0)),
        scratch_shapes=[pltpu.VMEM((N_HEADS, tq, LANES), F32),
                        pltpu.VMEM((N_HEADS, tq, LANES), F32)],
        compiler_params=pltpu.CompilerParams(
            dimension_semantics=("arbitrary", "arbitrary"), vmem_limit_bytes=VMEM_LIMIT),
        name="attention",
    )(q, k, v)


def _post_kernel(attn_ref, sga_ref, gc_ref, x_ref, mod_ref, wua_ref, wo_ref, g_ref, rwh_ref,
                 rwl_ref, rb_ref, x1_ref, h2_ref, idx_ref, gate_ref, rank_ref, cnt_out_ref, cnt_ref):
    @pl.when(pl.program_id(0) == 0)
    def _():
        cnt_ref[...] = jnp.zeros_like(cnt_ref)

    counts = cnt_ref[...]
    for r0 in range(0, x_ref.shape[0], POST_SUB):
        counts = _post_rows(slice(r0, r0 + POST_SUB), counts, attn_ref, sga_ref, gc_ref, x_ref,
                            mod_ref, wua_ref, wo_ref, g_ref, rwh_ref, rwl_ref, rb_ref, x1_ref,
                            h2_ref, idx_ref, gate_ref, rank_ref)
    cnt_ref[...] = counts
    cnt_out_ref[...] = counts.astype(jnp.int32)


def _post_rows(rs, counts, attn_ref, sga_ref, gc_ref, x_ref, mod_ref, wua_ref, wo_ref, g_ref,
               rwh_ref, rwl_ref, rb_ref, x1_ref, h2_ref, idx_ref, gate_ref, rank_ref):
    mod = mod_ref[...]
    a_branch = _dot(attn_ref[rs, :], wua_ref[...])
    merged = sga_ref[rs, :].astype(F32) * a_branch + gc_ref[rs, :].astype(F32)
    mix = _dot(merged.astype(BF16), wo_ref[...])
    x1 = x_ref[rs, :] + mod[2:3] * mix
    x1_ref[rs, :] = x1
    h2 = _rms(x1, g_ref[...]) * (1.0 + mod[4:5]) + mod[3:4]
    h2_ref[rs, :] = _pack_row(h2)

    h_hi = h2.astype(BF16)
    h_lo = (h2 - h_hi.astype(F32)).astype(BF16)
    logits = (_dot(h_hi, rwh_ref[...]) + _dot(h_lo, rwh_ref[...]) + _dot(h_hi, rwl_ref[...])
              + rb_ref[...])
    lane = lax.broadcasted_iota(jnp.int32, logits.shape, 1)
    work = logits
    vals, idxs = [], []
    for _ in range(TOP_K):
        mk = jnp.max(work, axis=-1, keepdims=True)
        ik = jnp.min(jnp.where(work == mk, lane, LANES), axis=-1, keepdims=True)
        vals.append(mk)
        idxs.append(ik)
        work = jnp.where(lane == ik, -jnp.inf, work)
    es = [jnp.exp(vk - vals[0]) for vk in vals]
    denom = es[0] + es[1] + es[2] + es[3]
    tm = logits.shape[0]
    chosen = jnp.zeros(logits.shape, F32)
    for kk in range(TOP_K):
        chosen = chosen + jnp.where(lane == idxs[kk], 1.0, 0.0)
    r_i = lax.broadcasted_iota(jnp.int32, (tm, tm), 0)
    c_i = lax.broadcasted_iota(jnp.int32, (tm, tm), 1)
    earlier = jnp.where(c_i < r_i, 1.0, 0.0).astype(BF16)
    before = _dot(earlier, chosen.astype(BF16)) + counts[0:1]

    idx_out = jnp.zeros(logits.shape, jnp.int32)
    gate_out = jnp.zeros(logits.shape, F32)
    rank_out = jnp.zeros(logits.shape, jnp.int32)
    for kk in range(TOP_K):
        rank_k = jnp.sum(jnp.where(lane == idxs[kk], before, 0.0), axis=-1, keepdims=True)
        idx_out = jnp.where(lane == kk, idxs[kk], idx_out)
        gate_out = jnp.where(lane == kk, es[kk] / denom, gate_out)
        rank_out = jnp.where(lane == kk, rank_k.astype(jnp.int32), rank_out)
    idx_ref[rs, :] = idx_out
    gate_ref[rs, :] = gate_out
    rank_ref[rs, :] = rank_out
    return counts + jnp.sum(chosen, axis=0, keepdims=True)


def _post(attn, sga, gc, x2, mod, wua, wo, norm_g, rw_hi, rw_lo, rb_pad, seq):
    t = x2.shape[0]
    tm = POST_TILE
    tiles_per_seq = seq // tm
    full = lambda a: pl.BlockSpec(a.shape, lambda i: (0,) * a.ndim)
    rows = lambda w: pl.BlockSpec((tm, w), lambda i: (i, 0))
    outs = [jax.ShapeDtypeStruct((t, D_MODEL), F32),
            jax.ShapeDtypeStruct((t, PACKED), jnp.uint32),
            jax.ShapeDtypeStruct((t, LANES), jnp.int32),
            jax.ShapeDtypeStruct((t, LANES), F32),
            jax.ShapeDtypeStruct((t, LANES), jnp.int32),
            jax.ShapeDtypeStruct((8, LANES), jnp.int32)]
    return pl.pallas_call(
        _post_kernel,
        out_shape=outs,
        grid=(t // tm,),
        in_specs=[
            rows(N_HEADS * V_HEAD), rows(D_MODEL), rows(D_MODEL), rows(D_MODEL),
            pl.BlockSpec((None, 8, D_MODEL), lambda i: (i // tiles_per_seq, 0, 0)),
            full(wua), full(wo), full(norm_g), full(rw_hi), full(rw_lo), full(rb_pad),
        ],
        out_specs=[rows(D_MODEL), rows(PACKED), rows(LANES), rows(LANES), rows(LANES),
                   pl.BlockSpec((8, LANES), lambda i: (0, 0))],
        scratch_shapes=[pltpu.VMEM((8, LANES), F32)],
        compiler_params=pltpu.CompilerParams(
            dimension_semantics=("arbitrary",), vmem_limit_bytes=VMEM_LIMIT),
        name="post_mixer",
    )(attn, sga, gc, x2, mod, wua, wo, norm_g, rw_hi, rw_lo, rb_pad)


_GM_COUNT, _GM_FIRST, _GM_BLOCKS = range(3)
_ST_EXPERT, _ST_SLOT = range(2)


def _moe_kernel(gm_ref, xs_ref, wgu_hbm, bgu_ref, wd_hbm, bd_ref, o_ref,
                wgu_f, wd_f, wgu_bf, wd_bf, sem, st_ref):
    b = pl.program_id(0)

    def weight_copies(expert, sl):
        return (pltpu.make_async_copy(wgu_hbm.at[expert], wgu_f.at[sl], sem.at[0, sl]),
                pltpu.make_async_copy(wd_hbm.at[expert], wd_f.at[sl], sem.at[1, sl]))

    def next_group(e):
        return lax.while_loop(
            lambda k: jnp.logical_and(k < N_EXPERTS,
                                      gm_ref[_GM_BLOCKS, jnp.minimum(k, N_EXPERTS - 1)] == 0),
            lambda k: k + 1, e)

    @pl.when(b == 0)
    def _():
        e0 = next_group(0)
        st_ref[_ST_EXPERT] = e0
        st_ref[_ST_SLOT] = 1
        for cp in weight_copies(e0, 0):
            cp.start()

    e_prev = st_ref[_ST_EXPERT]
    past = b >= gm_ref[_GM_FIRST, e_prev] + gm_ref[_GM_BLOCKS, e_prev]
    e = jnp.minimum(jnp.where(past, next_group(e_prev + 1), e_prev), N_EXPERTS - 1)
    st_ref[_ST_EXPERT] = e
    in_group = b - gm_ref[_GM_FIRST, e]
    used = jnp.logical_and(in_group >= 0, in_group < gm_ref[_GM_BLOCKS, e])
    n_valid = jnp.clip(gm_ref[_GM_COUNT, e] - in_group * MOE_BLOCK, 0, MOE_BLOCK)

    @pl.when(jnp.logical_and(used, in_group == 0))
    def _():
        slot = 1 - st_ref[_ST_SLOT]
        st_ref[_ST_SLOT] = slot
        nxt = next_group(e + 1)

        @pl.when(nxt < N_EXPERTS)
        def _():
            for cp in weight_copies(nxt, 1 - slot):
                cp.start(priority=1)

        for cp in weight_copies(e, slot):
            cp.wait()
        wgu_bf[...] = wgu_f[slot].astype(BF16)
        wd_bf[...] = wd_f[slot].astype(BF16)

    @pl.when(used)
    def _():
        row = lax.broadcasted_iota(jnp.int32, xs_ref.shape, 0)
        xs = _unpack_row(jnp.where(row < n_valid, xs_ref[...], 0)).astype(BF16)
        gu = _dot(xs, wgu_bf[...]) + bgu_ref[e]
        gate = jnp.minimum(gu[:, :D_EXPERT], SWIGLU_LIMIT)
        up = jnp.clip(gu[:, D_EXPERT:], -SWIGLU_LIMIT, SWIGLU_LIMIT)
        act = (up + 1.0) * (gate * jax.nn.sigmoid(gate * SWIGLU_ALPHA))
        o_ref[...] = _pack_row(_dot(act.astype(BF16), wd_bf[...]) + bd_ref[e])

    @pl.when(jnp.logical_not(used))
    def _():
        o_ref[...] = jnp.zeros_like(o_ref)


def _moe(group_table, xs, w_gu, b_gu, w_down, b_down):
    n_rows = xs.shape[0]
    n_blocks = n_rows // MOE_BLOCK
    grid_spec = pltpu.PrefetchScalarGridSpec(
        num_scalar_prefetch=1,
        grid=(n_blocks,),
        in_specs=[
            pl.BlockSpec((MOE_BLOCK, PACKED), lambda b, gm: (b, 0)),
            pl.BlockSpec(memory_space=pl.ANY),
            pl.BlockSpec(b_gu.shape, lambda b, gm: (0, 0, 0)),
            pl.BlockSpec(memory_space=pl.ANY),
            pl.BlockSpec(b_down.shape, lambda b, gm: (0, 0, 0)),
        ],
        out_specs=pl.BlockSpec((MOE_BLOCK, PACKED), lambda b, gm: (b, 0)),
        scratch_shapes=[pltpu.VMEM((2, D_MODEL, 2 * D_EXPERT), F32),
                        pltpu.VMEM((2, D_EXPERT, D_MODEL), F32),
                        pltpu.VMEM((D_MODEL, 2 * D_EXPERT), BF16),
                        pltpu.VMEM((D_EXPERT, D_MODEL), BF16),
                        pltpu.SemaphoreType.DMA((2, 2)),
                        pltpu.SMEM((2,), jnp.int32)],
    )
    return pl.pallas_call(
        _moe_kernel,
        out_shape=jax.ShapeDtypeStruct((n_rows, PACKED), jnp.uint32),
        grid_spec=grid_spec,
        compiler_params=pltpu.CompilerParams(
            dimension_semantics=("arbitrary",), vmem_limit_bytes=VMEM_LIMIT),
        name="moe_experts",
    )(group_table, xs, w_gu, b_gu, w_down, b_down)


def _final_kernel(last_layer, x1_ref, y_ref, gate_ref, mod_ref, g_ref, o_ref):
    mod = mod_ref[...]
    gate = gate_ref[...]
    ffn = gate[:, 0:1] * _unpack_row(y_ref[0])
    for kk in range(1, TOP_K):
        ffn = ffn + gate[:, kk:kk + 1] * _unpack_row(y_ref[kk])
    x = x1_ref[...] + mod[5:6] * ffn
    o_ref[...] = _rms(x, g_ref[...]) if last_layer else x


def _final(x1, y_kt, gate, mod, norm_g, seq, last_layer):
    t = x1.shape[0]
    tm = ROW_TILE
    tiles_per_seq = seq // tm
    rows = lambda w: pl.BlockSpec((tm, w), lambda i: (i, 0))
    return pl.pallas_call(
        functools.partial(_final_kernel, last_layer),
        out_shape=jax.ShapeDtypeStruct((t, D_MODEL), F32),
        grid=(t // tm,),
        in_specs=[
            rows(D_MODEL), pl.BlockSpec((TOP_K, tm, PACKED), lambda i: (0, i, 0)), rows(LANES),
            pl.BlockSpec((None, 8, D_MODEL), lambda i: (i // tiles_per_seq, 0, 0)),
            pl.BlockSpec(norm_g.shape, lambda i: (0, 0)),
        ],
        out_specs=rows(D_MODEL),
        compiler_params=pltpu.CompilerParams(
            dimension_semantics=("arbitrary",), vmem_limit_bytes=VMEM_LIMIT),
        name="combine_final",
    )(x1, y_kt, gate, mod, norm_g)


def _swap_halves(w):
    half = w.shape[-1] // 2
    return jnp.concatenate([w[..., half:], w[..., :half]], axis=-1)


def _prep_weights(w_in, w_uq, w_ukv):
    d = w_in.shape[0]
    splits = (Q_LORA, KV_LORA, QK_ROPE, CONV_WIDTH, CONV_WIDTH, CONV_WIDTH, D_MODEL, D_MODEL)
    offs = [0]
    for s in splits:
        offs.append(offs[-1] + s)
    part = lambda n: w_in[:, offs[n]:offs[n + 1]]
    z = lambda n: jnp.zeros((d, n), w_in.dtype)
    w_kpe = part(2)
    kpe_a = jnp.concatenate([z(QK_NOPE), w_kpe, z(HEAD_PAD - QK_HEAD)], axis=1)
    kpe_b = jnp.concatenate([z(QK_NOPE), _swap_halves(w_kpe), z(HEAD_PAD - QK_HEAD)], axis=1)
    w_lat = jnp.concatenate([part(0), part(1), kpe_a, kpe_b], axis=1).astype(BF16)
    w_conv = w_in[:, offs[3]:offs[6]].astype(BF16)
    w_gate = w_in[:, offs[6]:offs[8]].astype(BF16)

    wq = w_uq.reshape(Q_LORA, N_HEADS, QK_HEAD)
    zq = lambda n: jnp.zeros((Q_LORA, N_HEADS, n), w_uq.dtype)
    wq2 = jnp.concatenate([wq, zq(HEAD_PAD - QK_HEAD)], axis=-1)
    wq2s = jnp.concatenate([zq(QK_NOPE), _swap_halves(wq[..., QK_NOPE:]), zq(HEAD_PAD - QK_HEAD)],
                           axis=-1)
    wq2 = wq2.reshape(Q_LORA, N_HEADS * HEAD_PAD).astype(BF16)
    wq2s = wq2s.reshape(Q_LORA, N_HEADS * HEAD_PAD).astype(BF16)

    wkv = w_ukv.reshape(KV_LORA, N_HEADS, QK_NOPE + V_HEAD)
    wk2 = jnp.concatenate([wkv[..., :QK_NOPE],
                           jnp.zeros((KV_LORA, N_HEADS, HEAD_PAD - QK_NOPE), w_ukv.dtype)], axis=-1)
    wk2 = wk2.reshape(KV_LORA, N_HEADS * HEAD_PAD).astype(BF16)
    wv = jnp.concatenate([wkv[..., QK_NOPE:],
                          jnp.zeros((KV_LORA, N_HEADS, HEAD_PAD - V_HEAD), w_ukv.dtype)], axis=-1)
    wv = wv.reshape(KV_LORA, N_HEADS * HEAD_PAD).astype(BF16)
    return w_lat, w_conv, w_gate, wq2, wq2s, wk2, wv


def _rope_freqs():
    inv_freq = 1.0 / (ROPE_THETA ** (jnp.arange(0, QK_ROPE, 2, dtype=F32) / QK_ROPE))
    return jnp.concatenate([jnp.zeros((QK_NOPE,), F32), inv_freq, inv_freq,
                            jnp.zeros((HEAD_PAD - QK_HEAD,), F32)]).reshape(1, HEAD_PAD)


def _route(top_idx, rank, counts, n_tokens):
    blocks = (counts + MOE_BLOCK - 1) // MOE_BLOCK
    first_block = jnp.cumsum(blocks) - blocks
    experts = jnp.arange(N_EXPERTS, dtype=jnp.int32)
    start_of = jnp.sum(jnp.where(top_idx[:, :, None] == experts, first_block * MOE_BLOCK, 0),
                       axis=-1)
    dest = start_of + rank
    n_rows = n_tokens * TOP_K + N_EXPERTS * MOE_BLOCK
    table = jnp.stack([counts, first_block, blocks])
    return dest.astype(jnp.int32), table.astype(jnp.int32), n_rows


SC_CORES = 2
SC_SUBCORES = 16
SC_WORKERS = SC_CORES * SC_SUBCORES
SC_CHUNK = 64

def _sc_mesh():
    return plsc.VectorSubcoreMesh(core_axis_name="c", subcore_axis_name="s")


def _sc_worker():
    return lax.axis_index("s") * SC_CORES + lax.axis_index("c")


def _dispatch(h2, dest, n_rows):
    t, d = h2.shape
    per_w = t // SC_WORKERS
    n_chunks = per_w // SC_CHUNK
    assert per_w % (2 * SC_CHUNK) == 0
    idx = dest.reshape(SC_WORKERS, n_chunks, SC_CHUNK, TOP_K).transpose(0, 3, 1, 2)
    idx = idx.reshape(SC_WORKERS, TOP_K * n_chunks, SC_CHUNK)

    @functools.partial(
        pl.kernel, mesh=_sc_mesh(),
        out_type=jax.ShapeDtypeStruct((n_rows, d), h2.dtype),
        scratch_types=[pltpu.VMEM((TOP_K * n_chunks, SC_CHUNK), jnp.int32),
                       pltpu.VMEM((2, SC_CHUNK, d), h2.dtype),
                       pltpu.SemaphoreType.DMA((2,)),
                       pltpu.SemaphoreType.DMA((2,))],
        name="moe_dispatch")
    def run(h2_hbm, idx_hbm, xs_hbm, idx_v, rows_v, rsem, ssem):
        w = _sc_worker()
        pltpu.sync_copy(idx_hbm.at[w], idx_v)

        def read(g, b):
            src = h2_hbm.at[pl.ds(w * per_w + g * SC_CHUNK, SC_CHUNK)]
            return pltpu.make_async_copy(src, rows_v.at[b], rsem.at[b])

        def scatter(g, kk, b):
            dst = xs_hbm.at[idx_v.at[kk * n_chunks + g]]
            return pltpu.make_async_copy(rows_v.at[b], dst, ssem.at[b])

        read(0, 0).start()

        @pl.loop(0, n_chunks, step=2)
        def _(g0):
            for b in range(2):
                g = g0 + b
                read(g, b).wait()

                @pl.when(g + 1 < n_chunks)
                def _():
                    read(g + 1, 1 - b).start()

                for kk in range(TOP_K):
                    scatter(g, kk, b).start()
                for kk in range(TOP_K):
                    scatter(g, kk, b).wait()

    return run(h2, idx)


def _undispatch(ys, dest):
    t = dest.shape[0]
    d = ys.shape[1]
    n_out = t * TOP_K
    per_w = n_out // SC_WORKERS
    n_chunks = per_w // SC_CHUNK
    assert per_w % (2 * SC_CHUNK) == 0
    idx = dest.T.reshape(SC_WORKERS, n_chunks, SC_CHUNK)

    @functools.partial(
        pl.kernel, mesh=_sc_mesh(),
        out_type=jax.ShapeDtypeStruct((n_out, d), ys.dtype),
        scratch_types=[pltpu.VMEM((n_chunks, SC_CHUNK), jnp.int32),
                       pltpu.VMEM((2, SC_CHUNK, d), ys.dtype),
                       pltpu.SemaphoreType.DMA((2,)),
                       pltpu.SemaphoreType.DMA((2,))],
        name="moe_undispatch")
    def run(ys_hbm, idx_hbm, out_hbm, idx_v, rows_v, gsem, wsem):
        w = _sc_worker()
        pltpu.sync_copy(idx_hbm.at[w], idx_v)

        def gather(g, b):
            return pltpu.make_async_copy(ys_hbm.at[idx_v.at[g]], rows_v.at[b], gsem.at[b])

        def write(g, b):
            dst = out_hbm.at[pl.ds(w * per_w + g * SC_CHUNK, SC_CHUNK)]
            return pltpu.make_async_copy(rows_v.at[b], dst, wsem.at[b])

        gather(0, 0).start()

        @pl.loop(0, n_chunks, step=2)
        def _(g0):
            for b in range(2):
                g = g0 + b
                gather(g, b).wait()

                @pl.when(g >= 1)
                def _():
                    write(g - 1, 1 - b).wait()

                @pl.when(g + 1 < n_chunks)
                def _():
                    gather(g + 1, 1 - b).start()

                write(g, b).start()

        write(n_chunks - 1, (n_chunks - 1) % 2).wait()

    return run(ys, idx).reshape(TOP_K, t, d)


def kernel(x, c, positions, w_ada, b_ada, norm_mix_g, w_in, q_norm_g, w_uq, kv_norm_g, w_ukv,
           w_up_attn, conv_w, w_up_conv, w_o, norm_ffn_g, router_w, router_b, w_gu, b_gu,
           w_down, b_down, norm_final_g):
    batch, seq, d = x.shape
    t = batch * seq
    depth = w_ada.shape[0]
    x2 = x.reshape(t, d)
    pos = positions.astype(F32).reshape(t, 1)
    freqs = _rope_freqs()
    c_pad = jnp.zeros((8, d), F32).at[:batch].set(c)

    for l in range(depth):
        ada = _ada(c_pad, w_ada[l], b_ada[l].reshape(1, -1))
        mod = ada[:batch].reshape(batch, 6, d)
        mod = jnp.concatenate([mod, jnp.zeros((batch, 2, d), F32)], axis=1)

        w_lat, w_conv, w_gate, wq2, wq2s, wk2, wv = _prep_weights(w_in[l], w_uq[l], w_ukv[l])
        q, k, v, sga, gc = _pre(x2, mod, norm_mix_g[l].reshape(1, d), w_lat, w_conv, w_gate,
                                q_norm_g[l].reshape(1, -1), wq2, wq2s,
                                kv_norm_g[l].reshape(1, -1), wk2, wv, pos, freqs, conv_w[l],
                                w_up_conv[l].astype(BF16), seq)
        attn = _attention(q, k, v, batch, seq)

        rw_pad = jnp.concatenate([router_w[l], jnp.zeros((d, LANES - N_EXPERTS), F32)], axis=1)
        rb_pad = jnp.concatenate([router_b[l], jnp.full((LANES - N_EXPERTS,), NEG_BIG, F32)])
        rw_hi = rw_pad.astype(BF16)
        rw_lo = (rw_pad - rw_hi.astype(F32)).astype(BF16)
        x1, h2, idx_pad, gate_pad, rank_pad, counts = _post(
            attn, sga, gc, x2, mod, w_up_attn[l].astype(BF16), w_o[l].astype(BF16),
            norm_ffn_g[l].reshape(1, d), rw_hi, rw_lo, rb_pad.reshape(1, LANES), seq)

        dest, group_table, n_rows = _route(
            idx_pad[:, :TOP_K], rank_pad[:, :TOP_K], counts[0, :N_EXPERTS], t)
        xs = _dispatch(h2, dest, n_rows)
        ys = _moe(group_table, xs, w_gu[l], b_gu[l].reshape(N_EXPERTS, 1, -1),
                  w_down[l], b_down[l].reshape(N_EXPERTS, 1, -1))
        y_kt = _undispatch(ys, dest)
        x2 = _final(x1, y_kt, gate_pad, mod, norm_final_g.reshape(1, d), seq, l == depth - 1)

    return x2.reshape(batch, seq, d)
```

```python
import functools
import math

import jax
import jax.numpy as jnp
from jax import lax
from jax.experimental import pallas as pl
from jax.experimental.pallas import tpu as pltpu
from jax.experimental.pallas import tpu_sc as plsc

D_MODEL = 1024
CHUNK = 64
N_HEADS = 8
Q_LORA = 256
KV_LORA = 128
QK_NOPE = 64
QK_ROPE = 32
V_HEAD = 64
QK_HEAD = QK_NOPE + QK_ROPE
ROPE_THETA = 10000.0
CONV_WIDTH = 512
CONV_K = 3
N_EXPERTS = 32
TOP_K = 4
D_EXPERT = 1024
SWIGLU_LIMIT = 7.0
SWIGLU_ALPHA = 1.702
MOE_BLOCK = 256
RMS_EPS = 1e-6

LANES = 128
HEAD_PAD = 128
NEG_BIG = -1e30
VMEM_LIMIT = 56 * 1024 * 1024

F32 = jnp.float32
BF16 = jnp.bfloat16

Q_PRESCALE = (QK_HEAD ** -0.5) * math.log2(math.e)

ROW_TILE = 512
MOE_STEP_BLOCKS = 4
POST_TILE = 1024
POST_SUB = 1024
ATT_BLOCK = 512
ATT_WIDE = 2


def _rms(x, g):
    ms = jnp.mean(x * x, axis=-1, keepdims=True)
    return x * lax.rsqrt(ms + RMS_EPS) * g


def _dot(a, b):
    return jnp.dot(a, b, preferred_element_type=F32)


PACKED = D_MODEL // 2


def _pack_row(x):
    return pltpu.pack_elementwise([x[:, :PACKED], x[:, PACKED:]], packed_dtype=BF16)


def _unpack_row(w):
    half = lambda i: pltpu.unpack_elementwise(w, index=i, packed_dtype=BF16, unpacked_dtype=F32)
    return jnp.concatenate([half(0), half(1)], axis=-1)


def _ada_kernel(c_ref, w_ref, b_ref, o_ref):
    c = c_ref[...]
    ca = (c * jax.nn.sigmoid(c)).astype(BF16)
    o_ref[...] = _dot(ca, w_ref[...].astype(BF16)) + b_ref[...]


def _ada(c_pad, w_ada, b_ada):
    n = w_ada.shape[1]
    tn = 1024
    return pl.pallas_call(
        _ada_kernel,
        out_shape=jax.ShapeDtypeStruct((c_pad.shape[0], n), F32),
        grid=(n // tn,),
        in_specs=[
            pl.BlockSpec(c_pad.shape, lambda j: (0, 0)),
            pl.BlockSpec((D_MODEL, tn), lambda j: (0, j)),
            pl.BlockSpec((1, tn), lambda j: (0, j)),
        ],
        out_specs=pl.BlockSpec((c_pad.shape[0], tn), lambda j: (0, j)),
        compiler_params=pltpu.CompilerParams(
            dimension_semantics=("arbitrary",), vmem_limit_bytes=VMEM_LIMIT),
        name="ada",
    )(c_pad, w_ada, b_ada)


_C_QLAT = 0
_C_KVLAT = _C_QLAT + Q_LORA
_C_KPE_A = _C_KVLAT + KV_LORA
_C_KPE_B = _C_KPE_A + HEAD_PAD
_C_U = _C_KPE_B + HEAD_PAD


def _pre_kernel(tiles_per_seq, x_ref, mod_ref, g_ref, wlat_ref, wconv_ref, wgate_ref, qg_ref,
                wq_ref, wqs_ref, kvg_ref, wk_ref, wv_ref, pos_ref, freq_ref, cw_ref, wuc_ref,
                q_ref, k_ref, v_ref, sga_ref, gc_ref, carry_ref):
    i = pl.program_id(0)
    tm = x_ref.shape[0]
    mod = mod_ref[...]
    h = _rms(x_ref[...], g_ref[...]) * (1.0 + mod[1:2]) + mod[0:1]
    hb = h.astype(BF16)

    lane_h = lax.broadcasted_iota(jnp.int32, (tm, HEAD_PAD), 1)
    ang = pos_ref[...] * freq_ref[...]
    cosf = jnp.where(lane_h < QK_HEAD, jnp.cos(ang), 0.0)
    sin_a = jnp.sin(ang)
    sinf = jnp.where(lane_h < QK_NOPE + QK_ROPE // 2, -sin_a, sin_a)
    cos8 = jnp.concatenate([cosf] * N_HEADS, axis=-1)
    sin8 = jnp.concatenate([sinf] * N_HEADS, axis=-1)

    small = _dot(hb, wlat_ref[...])
    q_lat = small[:, _C_QLAT:_C_KVLAT]
    kv_lat = small[:, _C_KVLAT:_C_KPE_A]
    kpe = small[:, _C_KPE_A:_C_KPE_B] * cosf + small[:, _C_KPE_B:_C_U] * sinf
    qn = _rms(q_lat, qg_ref[...]).astype(BF16)
    q = _dot(qn, wq_ref[...]) * cos8 + _dot(qn, wqs_ref[...]) * sin8
    q_ref[...] = (q * Q_PRESCALE).astype(BF16)
    kvn = _rms(kv_lat, kvg_ref[...]).astype(BF16)
    k = _dot(kvn, wk_ref[...]) + jnp.concatenate([kpe] * N_HEADS, axis=-1)
    k_ref[...] = k.astype(BF16)
    lane = lax.broadcasted_iota(jnp.int32, (tm, N_HEADS * HEAD_PAD), 1)
    ones_col = jnp.where(lane % HEAD_PAD == V_HEAD, 1.0, 0.0)
    v_ref[...] = (_dot(kvn, wv_ref[...]) + ones_col).astype(BF16)

    ucb = _dot(hb, wconv_ref[...])
    cu = ucb[:, 0:CONV_WIDTH] * ucb[:, CONV_WIDTH:2 * CONV_WIDTH]
    b_gate = ucb[:, 2 * CONV_WIDTH:3 * CONV_WIDTH]

    @pl.when(i % tiles_per_seq == 0)
    def _():
        carry_ref[...] = jnp.zeros_like(carry_ref)

    prev = carry_ref[...]
    row = lax.broadcasted_iota(jnp.int32, cu.shape, 0)
    cu1 = jnp.where(row == 0, prev[7:8], pltpu.roll(cu, 1, 0))
    cu2 = jnp.where(row == 0, prev[6:7], jnp.where(row == 1, prev[7:8], pltpu.roll(cu, 2, 0)))
    cw = cw_ref[...]
    z = cw[2:3] * cu + cw[1:2] * cu1 + cw[0:1] * cu2
    carry_ref[...] = cu[tm - 8:tm]
    c_branch = _dot((b_gate * z).astype(BF16), wuc_ref[...])

    gates = _dot(hb, wgate_ref[...])
    sga_ref[...] = jax.nn.sigmoid(gates[:, 0:D_MODEL]).astype(BF16)
    gc_ref[...] = (jax.nn.sigmoid(gates[:, D_MODEL:]) * c_branch).astype(BF16)


def _pre(x2, mod, norm_g, w_lat, w_conv, w_gate, q_norm_g, wq2, wq2s, kv_norm_g, wk2, wv, pos,
         freqs, conv_w, w_up_conv, seq):
    t = x2.shape[0]
    tm = ROW_TILE
    tiles_per_seq = seq // tm
    full = lambda a: pl.BlockSpec(a.shape, lambda i: (0,) * a.ndim)
    rows = lambda w: pl.BlockSpec((tm, w), lambda i: (i, 0))
    outs = [jax.ShapeDtypeStruct((t, N_HEADS * HEAD_PAD), BF16),
            jax.ShapeDtypeStruct((t, N_HEADS * HEAD_PAD), BF16),
            jax.ShapeDtypeStruct((t, N_HEADS * HEAD_PAD), BF16),
            jax.ShapeDtypeStruct((t, D_MODEL), BF16),
            jax.ShapeDtypeStruct((t, D_MODEL), BF16)]
    return pl.pallas_call(
        functools.partial(_pre_kernel, tiles_per_seq),
        out_shape=outs,
        grid=(t // tm,),
        in_specs=[
            rows(D_MODEL),
            pl.BlockSpec((None, 8, D_MODEL), lambda i: (i // tiles_per_seq, 0, 0)),
            full(norm_g), full(w_lat), full(w_conv), full(w_gate), full(q_norm_g), full(wq2),
            full(wq2s), full(kv_norm_g), full(wk2), full(wv),
            rows(1), full(freqs), full(conv_w), full(w_up_conv),
        ],
        out_specs=[rows(N_HEADS * HEAD_PAD), rows(N_HEADS * HEAD_PAD), rows(N_HEADS * HEAD_PAD),
                   rows(D_MODEL), rows(D_MODEL)],
        scratch_shapes=[pltpu.VMEM((8, CONV_WIDTH), F32)],
        compiler_params=pltpu.CompilerParams(
            dimension_semantics=("arbitrary",), vmem_limit_bytes=VMEM_LIMIT),
        name="pre_mixer",
    )(x2, mod, norm_g, w_lat, w_conv, w_gate, q_norm_g, wq2, wq2s, kv_norm_g, wk2, wv, pos,
      freqs, conv_w, w_up_conv)


def _attn_kernel(q_ref, k_ref, v_ref, o_ref, m_ref, acc_ref):
    i = pl.program_id(1)
    tq = q_ref.shape[0]

    m_ref[...] = jnp.full_like(m_ref, NEG_BIG)
    acc_ref[...] = jnp.zeros_like(acc_ref)

    def step(k0, tk, masked):
        if masked:
            rq = lax.broadcasted_iota(jnp.int32, (tq, tk), 0) // CHUNK
            ck = lax.broadcasted_iota(jnp.int32, (tq, tk), 1) // CHUNK
            allowed = ck <= rq
        for hd in range(N_HEADS):
            hs = slice(hd * HEAD_PAD, (hd + 1) * HEAD_PAD)
            s = lax.dot_general(q_ref[:, hs], k_ref[pl.ds(k0, tk), hs],
                                (((1,), (1,)), ((), ())), preferred_element_type=F32)
            if masked:
                s = jnp.where(allowed, s, NEG_BIG)
            m_old = m_ref[hd]
            s_max = s[:, 0:LANES]
            for c in range(1, tk // LANES):
                s_max = jnp.maximum(s_max, s[:, c * LANES:(c + 1) * LANES])
            m_new = jnp.maximum(m_old, jnp.max(s_max, axis=-1, keepdims=True))
            alpha = jnp.exp2(m_old - m_new)
            p = jnp.concatenate(
                [jnp.exp2(s[:, c * LANES:(c + 1) * LANES] - m_new).astype(BF16)
                 for c in range(tk // LANES)], axis=-1)
            acc_ref[hd] = alpha * acc_ref[hd] + _dot(p, v_ref[pl.ds(k0, tk), hs])
            m_ref[hd] = m_new

    wide = ATT_WIDE * tq

    def body(j, carry):
        step(pl.multiple_of(j * wide, wide), wide, False)
        return carry

    lax.fori_loop(0, i // ATT_WIDE, body, 0)

    for r in range(1, ATT_WIDE):
        @pl.when(i % ATT_WIDE >= r)
        def _():
            step(pl.multiple_of((i - i % ATT_WIDE + r - 1) * tq, tq), tq, False)

    step(pl.multiple_of(i * tq, tq), tq, True)

    for hp in range(N_HEADS // 2):
        pair = []
        for hd in (2 * hp, 2 * hp + 1):
            acc = acc_ref[hd]
            pair.append(acc[:, 0:V_HEAD] / acc[:, V_HEAD:V_HEAD + 1])
        o_ref[:, hp * LANES:(hp + 1) * LANES] = jnp.concatenate(pair, axis=-1).astype(BF16)


def _attention(q, k, v, batch, seq):
    tq = ATT_BLOCK
    nq = seq // tq
    return pl.pallas_call(
        _attn_kernel,
        out_shape=jax.ShapeDtypeStruct((batch * seq, N_HEADS * V_HEAD), BF16),
        grid=(batch, nq),
        in_specs=[
            pl.BlockSpec((tq, N_HEADS * HEAD_PAD), lambda b, i: (b * nq + i, 0)),
            pl.BlockSpec((seq, N_HEADS * HEAD_PAD), lambda b, i: (b, 0)),
            pl.BlockSpec((seq, N_HEADS * HEAD_PAD), lambda b, i: (b, 0)),
        ],
        out_specs=pl.BlockSpec((tq, N_HEADS * V_HEAD), lambda b, i: (b * nq + i, 0)),
        scratch_shapes=[pltpu.VMEM((N_HEADS, tq, LANES), F32),
                        pltpu.VMEM((N_HEADS, tq, LANES), F32)],
        compiler_params=pltpu.CompilerParams(
            dimension_semantics=("arbitrary", "arbitrary"), vmem_limit_bytes=VMEM_LIMIT),
        name="attention",
    )(q, k, v)


def _post_kernel(attn_ref, sga_ref, gc_ref, x_ref, mod_ref, wua_ref, wo_ref, g_ref, rwh_ref,
                 rwl_ref, rb_ref, x1_ref, h2_ref, idx_ref, gate_ref, rank_ref, cnt_out_ref, cnt_ref):
    @pl.when(pl.program_id(0) == 0)
    def _():
        cnt_ref[...] = jnp.zeros_like(cnt_ref)

    counts = cnt_ref[...]
    for r0 in range(0, x_ref.shape[0], POST_SUB):
        counts = _post_rows(slice(r0, r0 + POST_SUB), counts, attn_ref, sga_ref, gc_ref, x_ref,
                            mod_ref, wua_ref, wo_ref, g_ref, rwh_ref, rwl_ref, rb_ref, x1_ref,
                            h2_ref, idx_ref, gate_ref, rank_ref)
    cnt_ref[...] = counts
    cnt_out_ref[...] = counts.astype(jnp.int32)


def _post_rows(rs, counts, attn_ref, sga_ref, gc_ref, x_ref, mod_ref, wua_ref, wo_ref, g_ref,
               rwh_ref, rwl_ref, rb_ref, x1_ref, h2_ref, idx_ref, gate_ref, rank_ref):
    mod = mod_ref[...]
    a_branch = _dot(attn_ref[rs, :], wua_ref[...])
    merged = sga_ref[rs, :].astype(F32) * a_branch + gc_ref[rs, :].astype(F32)
    mix = _dot(merged.astype(BF16), wo_ref[...])
    x1 = x_ref[rs, :] + mod[2:3] * mix
    x1_ref[rs, :] = x1
    h2 = _rms(x1, g_ref[...]) * (1.0 + mod[4:5]) + mod[3:4]
    h2_ref[rs, :] = _pack_row(h2)

    h_hi = h2.astype(BF16)
    h_lo = (h2 - h_hi.astype(F32)).astype(BF16)
    logits = (_dot(h_hi, rwh_ref[...]) + _dot(h_lo, rwh_ref[...]) + _dot(h_hi, rwl_ref[...])
              + rb_ref[...])
    lane = lax.broadcasted_iota(jnp.int32, logits.shape, 1)
    work = logits
    vals, idxs = [], []
    for _ in range(TOP_K):
        mk = jnp.max(work, axis=-1, keepdims=True)
        ik = jnp.min(jnp.where(work == mk, lane, LANES), axis=-1, keepdims=True)
        vals.append(mk)
        idxs.append(ik)
        work = jnp.where(lane == ik, -jnp.inf, work)
    es = [jnp.exp(vk - vals[0]) for vk in vals]
    denom = es[0] + es[1] + es[2] + es[3]
    tm = logits.shape[0]
    chosen = jnp.zeros(logits.shape, F32)
    for kk in range(TOP_K):
        chosen = chosen + jnp.where(lane == idxs[kk], 1.0, 0.0)
    r_i = lax.broadcasted_iota(jnp.int32, (tm, tm), 0)
    c_i = lax.broadcasted_iota(jnp.int32, (tm, tm), 1)
    earlier = jnp.where(c_i < r_i, 1.0, 0.0).astype(BF16)
    before = _dot(earlier, chosen.astype(BF16)) + counts[0:1]

    idx_out = jnp.zeros(logits.shape, jnp.int32)
    gate_out = jnp.zeros(logits.shape, F32)
    rank_out = jnp.zeros(logits.shape, jnp.int32)
    for kk in range(TOP_K):
        rank_k = jnp.sum(jnp.where(lane == idxs[kk], before, 0.0), axis=-1, keepdims=True)
        idx_out = jnp.where(lane == kk, idxs[kk], idx_out)
        gate_out = jnp.where(lane == kk, es[kk] / denom, gate_out)
        rank_out = jnp.where(lane == kk, rank_k.astype(jnp.int32), rank_out)
    idx_ref[rs, :] = idx_out
    gate_ref[rs, :] = gate_out
    rank_ref[rs, :] = rank_out
    return counts + jnp.sum(chosen, axis=0, keepdims=True)


def _post(attn, sga, gc, x2, mod, wua, wo, norm_g, rw_hi, rw_lo, rb_pad, seq):
    t = x2.shape[0]
    tm = POST_TILE
    tiles_per_seq = seq // tm
    full = lambda a: pl.BlockSpec(a.shape, lambda i: (0,) * a.ndim)
    rows = lambda w: pl.BlockSpec((tm, w), lambda i: (i, 0))
    outs = [jax.ShapeDtypeStruct((t, D_MODEL), F32),
            jax.ShapeDtypeStruct((t, PACKED), jnp.uint32),
            jax.ShapeDtypeStruct((t, LANES), jnp.int32),
            jax.ShapeDtypeStruct((t, LANES), F32),
            jax.ShapeDtypeStruct((t, LANES), jnp.int32),
            jax.ShapeDtypeStruct((8, LANES), jnp.int32)]
    return pl.pallas_call(
        _post_kernel,
        out_shape=outs,
        grid=(t // tm,),
        in_specs=[
            rows(N_HEADS * V_HEAD), rows(D_MODEL), rows(D_MODEL), rows(D_MODEL),
            pl.BlockSpec((None, 8, D_MODEL), lambda i: (i // tiles_per_seq, 0, 0)),
            full(wua), full(wo), full(norm_g), full(rw_hi), full(rw_lo), full(rb_pad),
        ],
        out_specs=[rows(D_MODEL), rows(PACKED), rows(LANES), rows(LANES), rows(LANES),
                   pl.BlockSpec((8, LANES), lambda i: (0, 0))],
        scratch_shapes=[pltpu.VMEM((8, LANES), F32)],
        compiler_params=pltpu.CompilerParams(
            dimension_semantics=("arbitrary",), vmem_limit_bytes=VMEM_LIMIT),
        name="post_mixer",
    )(attn, sga, gc, x2, mod, wua, wo, norm_g, rw_hi, rw_lo, rb_pad)


_GM_COUNT, _GM_FIRST, _GM_BLOCKS = range(3)
_ST_EXPERT, _ST_SLOT = range(2)


def _moe_kernel(gm_ref, xs_ref, wgu_hbm, bgu_ref, wd_hbm, bd_ref, o_ref,
                wgu_f, wd_f, wgu_bf, wd_bf, sem, st_ref):
    for r in range(MOE_STEP_BLOCKS):
        _moe_block(pl.program_id(0) * MOE_STEP_BLOCKS + r, slice(r * MOE_BLOCK, (r + 1) * MOE_BLOCK),
                   gm_ref, xs_ref, wgu_hbm, bgu_ref, wd_hbm, bd_ref, o_ref,
                   wgu_f, wd_f, wgu_bf, wd_bf, sem, st_ref)


def _moe_block(b, rs, gm_ref, xs_ref, wgu_hbm, bgu_ref, wd_hbm, bd_ref, o_ref,
               wgu_f, wd_f, wgu_bf, wd_bf, sem, st_ref):
    def weight_copies(expert, sl):
        return (pltpu.make_async_copy(wgu_hbm.at[expert], wgu_f.at[sl], sem.at[0, sl]),
                pltpu.make_async_copy(wd_hbm.at[expert], wd_f.at[sl], sem.at[1, sl]))

    def next_group(e):
        return lax.while_loop(
            lambda k: jnp.logical_and(k < N_EXPERTS,
                                      gm_ref[_GM_BLOCKS, jnp.minimum(k, N_EXPERTS - 1)] == 0),
            lambda k: k + 1, e)

    @pl.when(b == 0)
    def _():
        e0 = next_group(0)
        st_ref[_ST_EXPERT] = e0
        st_ref[_ST_SLOT] = 1
        for cp in weight_copies(e0, 0):
            cp.start()

    e_prev = st_ref[_ST_EXPERT]
    past = b >= gm_ref[_GM_FIRST, e_prev] + gm_ref[_GM_BLOCKS, e_prev]
    e = jnp.minimum(jnp.where(past, next_group(e_prev + 1), e_prev), N_EXPERTS - 1)
    st_ref[_ST_EXPERT] = e
    in_group = b - gm_ref[_GM_FIRST, e]
    used = jnp.logical_and(in_group >= 0, in_group < gm_ref[_GM_BLOCKS, e])
    n_valid = jnp.clip(gm_ref[_GM_COUNT, e] - in_group * MOE_BLOCK, 0, MOE_BLOCK)

    @pl.when(jnp.logical_and(used, in_group == 0))
    def _():
        slot = 1 - st_ref[_ST_SLOT]
        st_ref[_ST_SLOT] = slot
        nxt = next_group(e + 1)

        @pl.when(nxt < N_EXPERTS)
        def _():
            for cp in weight_copies(nxt, 1 - slot):
                cp.start(priority=1)

        for cp in weight_copies(e, slot):
            cp.wait()
        wgu_bf[...] = wgu_f[slot].astype(BF16)
        wd_bf[...] = wd_f[slot].astype(BF16)

    @pl.when(used)
    def _():
        row = lax.broadcasted_iota(jnp.int32, (MOE_BLOCK, PACKED), 0)
        xs = _unpack_row(jnp.where(row < n_valid, xs_ref[rs, :], 0)).astype(BF16)
        gu = _dot(xs, wgu_bf[...]) + bgu_ref[e]
        gate = jnp.minimum(gu[:, :D_EXPERT], SWIGLU_LIMIT)
        up = jnp.clip(gu[:, D_EXPERT:], -SWIGLU_LIMIT, SWIGLU_LIMIT)
        act = (up + 1.0) * (gate * jax.nn.sigmoid(gate * SWIGLU_ALPHA))
        o_ref[rs, :] = _pack_row(_dot(act.astype(BF16), wd_bf[...]) + bd_ref[e])

    @pl.when(jnp.logical_not(used))
    def _():
        o_ref[rs, :] = jnp.zeros((MOE_BLOCK, PACKED), o_ref.dtype)


def _moe(group_table, xs, w_gu, b_gu, w_down, b_down):
    n_rows = xs.shape[0]
    step_rows = MOE_STEP_BLOCKS * MOE_BLOCK
    assert n_rows % step_rows == 0
    grid_spec = pltpu.PrefetchScalarGridSpec(
        num_scalar_prefetch=1,
        grid=(n_rows // step_rows,),
        in_specs=[
            pl.BlockSpec((step_rows, PACKED), lambda b, gm: (b, 0)),
            pl.BlockSpec(memory_space=pl.ANY),
            pl.BlockSpec(b_gu.shape, lambda b, gm: (0, 0, 0)),
            pl.BlockSpec(memory_space=pl.ANY),
            pl.BlockSpec(b_down.shape, lambda b, gm: (0, 0, 0)),
        ],
        out_specs=pl.BlockSpec((step_rows, PACKED), lambda b, gm: (b, 0)),
        scratch_shapes=[pltpu.VMEM((2, D_MODEL, 2 * D_EXPERT), F32),
                        pltpu.VMEM((2, D_EXPERT, D_MODEL), F32),
                        pltpu.VMEM((D_MODEL, 2 * D_EXPERT), BF16),
                        pltpu.VMEM((D_EXPERT, D_MODEL), BF16),
                        pltpu.SemaphoreType.DMA((2, 2)),
                        pltpu.SMEM((2,), jnp.int32)],
    )
    return pl.pallas_call(
        _moe_kernel,
        out_shape=jax.ShapeDtypeStruct((n_rows, PACKED), jnp.uint32),
        grid_spec=grid_spec,
        compiler_params=pltpu.CompilerParams(
            dimension_semantics=("arbitrary",), vmem_limit_bytes=VMEM_LIMIT),
        name="moe_experts",
    )(group_table, xs, w_gu, b_gu, w_down, b_down)


def _final_kernel(last_layer, x1_ref, y_ref, gate_ref, mod_ref, g_ref, o_ref):
    mod = mod_ref[...]
    gate = gate_ref[...]
    ffn = gate[:, 0:1] * _unpack_row(y_ref[0])
    for kk in range(1, TOP_K):
        ffn = ffn + gate[:, kk:kk + 1] * _unpack_row(y_ref[kk])
    x = x1_ref[...] + mod[5:6] * ffn
    o_ref[...] = _rms(x, g_ref[...]) if last_layer else x


def _final(x1, y_kt, gate, mod, norm_g, seq, last_layer):
    t = x1.shape[0]
    tm = ROW_TILE
    tiles_per_seq = seq // tm
    rows = lambda w: pl.BlockSpec((tm, w), lambda i: (i, 0))
    return pl.pallas_call(
        functools.partial(_final_kernel, last_layer),
        out_shape=jax.ShapeDtypeStruct((t, D_MODEL), F32),
        grid=(t // tm,),
        in_specs=[
            rows(D_MODEL), pl.BlockSpec((TOP_K, tm, PACKED), lambda i: (0, i, 0)), rows(LANES),
            pl.BlockSpec((None, 8, D_MODEL), lambda i: (i // tiles_per_seq, 0, 0)),
            pl.BlockSpec(norm_g.shape, lambda i: (0, 0)),
        ],
        out_specs=rows(D_MODEL),
        compiler_params=pltpu.CompilerParams(
            dimension_semantics=("arbitrary",), vmem_limit_bytes=VMEM_LIMIT),
        name="combine_final",
    )(x1, y_kt, gate, mod, norm_g)


def _swap_halves(w):
    half = w.shape[-1] // 2
    return jnp.concatenate([w[..., half:], w[..., :half]], axis=-1)


def _prep_weights(w_in, w_uq, w_ukv):
    d = w_in.shape[0]
    splits = (Q_LORA, KV_LORA, QK_ROPE, CONV_WIDTH, CONV_WIDTH, CONV_WIDTH, D_MODEL, D_MODEL)
    offs = [0]
    for s in splits:
        offs.append(offs[-1] + s)
    part = lambda n: w_in[:, offs[n]:offs[n + 1]]
    z = lambda n: jnp.zeros((d, n), w_in.dtype)
    w_kpe = part(2)
    kpe_a = jnp.concatenate([z(QK_NOPE), w_kpe, z(HEAD_PAD - QK_HEAD)], axis=1)
    kpe_b = jnp.concatenate([z(QK_NOPE), _swap_halves(w_kpe), z(HEAD_PAD - QK_HEAD)], axis=1)
    w_lat = jnp.concatenate([part(0), part(1), kpe_a, kpe_b], axis=1).astype(BF16)
    w_conv = w_in[:, offs[3]:offs[6]].astype(BF16)
    w_gate = w_in[:, offs[6]:offs[8]].astype(BF16)

    wq = w_uq.reshape(Q_LORA, N_HEADS, QK_HEAD)
    zq = lambda n: jnp.zeros((Q_LORA, N_HEADS, n), w_uq.dtype)
    wq2 = jnp.concatenate([wq, zq(HEAD_PAD - QK_HEAD)], axis=-1)
    wq2s = jnp.concatenate([zq(QK_NOPE), _swap_halves(wq[..., QK_NOPE:]), zq(HEAD_PAD - QK_HEAD)],
                           axis=-1)
    wq2 = wq2.reshape(Q_LORA, N_HEADS * HEAD_PAD).astype(BF16)
    wq2s = wq2s.reshape(Q_LORA, N_HEADS * HEAD_PAD).astype(BF16)

    wkv = w_ukv.reshape(KV_LORA, N_HEADS, QK_NOPE + V_HEAD)
    wk2 = jnp.concatenate([wkv[..., :QK_NOPE],
                           jnp.zeros((KV_LORA, N_HEADS, HEAD_PAD - QK_NOPE), w_ukv.dtype)], axis=-1)
    wk2 = wk2.reshape(KV_LORA, N_HEADS * HEAD_PAD).astype(BF16)
    wv = jnp.concatenate([wkv[..., QK_NOPE:],
                          jnp.zeros((KV_LORA, N_HEADS, HEAD_PAD - V_HEAD), w_ukv.dtype)], axis=-1)
    wv = wv.reshape(KV_LORA, N_HEADS * HEAD_PAD).astype(BF16)
    return w_lat, w_conv, w_gate, wq2, wq2s, wk2, wv


def _rope_freqs():
    inv_freq = 1.0 / (ROPE_THETA ** (jnp.arange(0, QK_ROPE, 2, dtype=F32) / QK_ROPE))
    return jnp.concatenate([jnp.zeros((QK_NOPE,), F32), inv_freq, inv_freq,
                            jnp.zeros((HEAD_PAD - QK_HEAD,), F32)]).reshape(1, HEAD_PAD)


def _route(top_idx, rank, counts, n_tokens):
    blocks = (counts + MOE_BLOCK - 1) // MOE_BLOCK
    first_block = jnp.cumsum(blocks) - blocks
    experts = jnp.arange(N_EXPERTS, dtype=jnp.int32)
    start_of = jnp.sum(jnp.where(top_idx[:, :, None] == experts, first_block * MOE_BLOCK, 0),
                       axis=-1)
    dest = start_of + rank
    n_rows = n_tokens * TOP_K + N_EXPERTS * MOE_BLOCK
    table = jnp.stack([counts, first_block, blocks])
    return dest.astype(jnp.int32), table.astype(jnp.int32), n_rows


SC_CORES = 2
SC_SUBCORES = 16
SC_WORKERS = SC_CORES * SC_SUBCORES
SC_CHUNK = 64

def _sc_mesh():
    return plsc.VectorSubcoreMesh(core_axis_name="c", subcore_axis_name="s")


def _sc_worker():
    return lax.axis_index("s") * SC_CORES + lax.axis_index("c")


def _dispatch(h2, dest, n_rows):
    t, d = h2.shape
    per_w = t // SC_WORKERS
    n_chunks = per_w // SC_CHUNK
    assert per_w % (2 * SC_CHUNK) == 0
    idx = dest.reshape(SC_WORKERS, n_chunks, SC_CHUNK, TOP_K).transpose(0, 3, 1, 2)
    idx = idx.reshape(SC_WORKERS, TOP_K * n_chunks, SC_CHUNK)

    @functools.partial(
        pl.kernel, mesh=_sc_mesh(),
        out_type=jax.ShapeDtypeStruct((n_rows, d), h2.dtype),
        scratch_types=[pltpu.VMEM((TOP_K * n_chunks, SC_CHUNK), jnp.int32),
                       pltpu.VMEM((2, SC_CHUNK, d), h2.dtype),
                       pltpu.SemaphoreType.DMA((2,)),
                       pltpu.SemaphoreType.DMA((2,))],
        name="moe_dispatch")
    def run(h2_hbm, idx_hbm, xs_hbm, idx_v, rows_v, rsem, ssem):
        w = _sc_worker()
        pltpu.sync_copy(idx_hbm.at[w], idx_v)

        def read(g, b):
            src = h2_hbm.at[pl.ds(w * per_w + g * SC_CHUNK, SC_CHUNK)]
            return pltpu.make_async_copy(src, rows_v.at[b], rsem.at[b])

        def scatter(g, kk, b):
            dst = xs_hbm.at[idx_v.at[kk * n_chunks + g]]
            return pltpu.make_async_copy(rows_v.at[b], dst, ssem.at[b])

        read(0, 0).start()

        @pl.loop(0, n_chunks, step=2)
        def _(g0):
            for b in range(2):
                g = g0 + b
                read(g, b).wait()

                @pl.when(g + 1 < n_chunks)
                def _():
                    read(g + 1, 1 - b).start()

                for kk in range(TOP_K):
                    scatter(g, kk, b).start()
                for kk in range(TOP_K):
                    scatter(g, kk, b).wait()

    return run(h2, idx)


def _undispatch(ys, dest):
    t = dest.shape[0]
    d = ys.shape[1]
    n_out = t * TOP_K
    per_w = n_out // SC_WORKERS
    n_chunks = per_w // SC_CHUNK
    assert per_w % (2 * SC_CHUNK) == 0
    idx = dest.T.reshape(SC_WORKERS, n_chunks, SC_CHUNK)

    @functools.partial(
        pl.kernel, mesh=_sc_mesh(),
        out_type=jax.ShapeDtypeStruct((n_out, d), ys.dtype),
        scratch_types=[pltpu.VMEM((n_chunks, SC_CHUNK), jnp.int32),
                       pltpu.VMEM((2, SC_CHUNK, d), ys.dtype),
                       pltpu.SemaphoreType.DMA((2,)),
                       pltpu.SemaphoreType.DMA((2,))],
        name="moe_undispatch")
    def run(ys_hbm, idx_hbm, out_hbm, idx_v, rows_v, gsem, wsem):
        w = _sc_worker()
        pltpu.sync_copy(idx_hbm.at[w], idx_v)

        def gather(g, b):
            return pltpu.make_async_copy(ys_hbm.at[idx_v.at[g]], rows_v.at[b], gsem.at[b])

        def write(g, b):
            dst = out_hbm.at[pl.ds(w * per_w + g * SC_CHUNK, SC_CHUNK)]
            return pltpu.make_async_copy(rows_v.at[b], dst, wsem.at[b])

        gather(0, 0).start()

        @pl.loop(0, n_chunks, step=2)
        def _(g0):
            for b in range(2):
                g = g0 + b
                gather(g, b).wait()

                @pl.when(g >= 1)
                def _():
                    write(g - 1, 1 - b).wait()

                @pl.when(g + 1 < n_chunks)
                def _():
                    gather(g + 1, 1 - b).start()

                write(g, b).start()

        write(n_chunks - 1, (n_chunks - 1) % 2).wait()

    return run(ys, idx).reshape(TOP_K, t, d)


def kernel(x, c, positions, w_ada, b_ada, norm_mix_g, w_in, q_norm_g, w_uq, kv_norm_g, w_ukv,
           w_up_attn, conv_w, w_up_conv, w_o, norm_ffn_g, router_w, router_b, w_gu, b_gu,
           w_down, b_down, norm_final_g):
    batch, seq, d = x.shape
    t = batch * seq
    depth = w_ada.shape[0]
    x2 = x.reshape(t, d)
    pos = positions.astype(F32).reshape(t, 1)
    freqs = _rope_freqs()
    c_pad = jnp.zeros((8, d), F32).at[:batch].set(c)

    for l in range(depth):
        ada = _ada(c_pad, w_ada[l], b_ada[l].reshape(1, -1))
        mod = ada[:batch].reshape(batch, 6, d)
        mod = jnp.concatenate([mod, jnp.zeros((batch, 2, d), F32)], axis=1)

        w_lat, w_conv, w_gate, wq2, wq2s, wk2, wv = _prep_weights(w_in[l], w_uq[l], w_ukv[l])
        q, k, v, sga, gc = _pre(x2, mod, norm_mix_g[l].reshape(1, d), w_lat, w_conv, w_gate,
                                q_norm_g[l].reshape(1, -1), wq2, wq2s,
                                kv_norm_g[l].reshape(1, -1), wk2, wv, pos, freqs, conv_w[l],
                                w_up_conv[l].astype(BF16), seq)
        attn = _attention(q, k, v, batch, seq)

        rw_pad = jnp.concatenate([router_w[l], jnp.zeros((d, LANES - N_EXPERTS), F32)], axis=1)
        rb_pad = jnp.concatenate([router_b[l], jnp.full((LANES - N_EXPERTS,), NEG_BIG, F32)])
        rw_hi = rw_pad.astype(BF16)
        rw_lo = (rw_pad - rw_hi.astype(F32)).astype(BF16)
        x1, h2, idx_pad, gate_pad, rank_pad, counts = _post(
            attn, sga, gc, x2, mod, w_up_attn[l].astype(BF16), w_o[l].astype(BF16),
            norm_ffn_g[l].reshape(1, d), rw_hi, rw_lo, rb_pad.reshape(1, LANES), seq)

        dest, group_table, n_rows = _route(
            idx_pad[:, :TOP_K], rank_pad[:, :TOP_K], counts[0, :N_EXPERTS], t)
        xs = _dispatch(h2, dest, n_rows)
        ys = _moe(group_table, xs, w_gu[l], b_gu[l].reshape(N_EXPERTS, 1, -1),
                  w_down[l], b_down[l].reshape(N_EXPERTS, 1, -1))
        y_kt = _undispatch(ys, dest)
        x2 = _final(x1, y_kt, gate_pad, mod, norm_final_g.reshape(1, d), seq, l == depth - 1)

    return x2.reshape(batch, seq, d)
```

```python
import functools
import math

import jax
import jax.numpy as jnp
from jax import lax
from jax.experimental import pallas as pl
from jax.experimental.pallas import tpu as pltpu
from jax.experimental.pallas import tpu_sc as plsc

D_MODEL = 1024
CHUNK = 64
N_HEADS = 8
Q_LORA = 256
KV_LORA = 128
QK_NOPE = 64
QK_ROPE = 32
V_HEAD = 64
QK_HEAD = QK_NOPE + QK_ROPE
ROPE_THETA = 10000.0
CONV_WIDTH = 512
CONV_K = 3
N_EXPERTS = 32
TOP_K = 4
D_EXPERT = 1024
SWIGLU_LIMIT = 7.0
SWIGLU_ALPHA = 1.702
MOE_BLOCK = 256
RMS_EPS = 1e-6

LANES = 128
HEAD_PAD = 128
NEG_BIG = -1e30
VMEM_LIMIT = 56 * 1024 * 1024

F32 = jnp.float32
BF16 = jnp.bfloat16

Q_PRESCALE = (QK_HEAD ** -0.5) * math.log2(math.e)

ROW_TILE = 1024
MOE_STEP_BLOCKS = 4
POST_TILE = 1024
POST_SUB = 1024
ATT_BLOCK = 512
ATT_WIDE = 2


def _rms(x, g):
    ms = jnp.mean(x * x, axis=-1, keepdims=True)
    return x * lax.rsqrt(ms + RMS_EPS) * g


def _dot(a, b):
    return jnp.dot(a, b, preferred_element_type=F32)


PACKED = D_MODEL // 2


def _pack_row(x):
    return pltpu.pack_elementwise([x[:, :PACKED], x[:, PACKED:]], packed_dtype=BF16)


def _unpack_row(w):
    half = lambda i: pltpu.unpack_elementwise(w, index=i, packed_dtype=BF16, unpacked_dtype=F32)
    return jnp.concatenate([half(0), half(1)], axis=-1)


def _ada_kernel(c_ref, w_ref, b_ref, o_ref):
    c = c_ref[...]
    ca = (c * jax.nn.sigmoid(c)).astype(BF16)
    o_ref[...] = _dot(ca, w_ref[...].astype(BF16)) + b_ref[...]


def _ada(c_pad, w_ada, b_ada):
    n = w_ada.shape[1]
    tn = 1024
    return pl.pallas_call(
        _ada_kernel,
        out_shape=jax.ShapeDtypeStruct((c_pad.shape[0], n), F32),
        grid=(n // tn,),
        in_specs=[
            pl.BlockSpec(c_pad.shape, lambda j: (0, 0)),
            pl.BlockSpec((D_MODEL, tn), lambda j: (0, j)),
            pl.BlockSpec((1, tn), lambda j: (0, j)),
        ],
        out_specs=pl.BlockSpec((c_pad.shape[0], tn), lambda j: (0, j)),
        compiler_params=pltpu.CompilerParams(
            dimension_semantics=("arbitrary",), vmem_limit_bytes=VMEM_LIMIT),
        name="ada",
    )(c_pad, w_ada, b_ada)


_C_QLAT = 0
_C_KVLAT = _C_QLAT + Q_LORA
_C_KPE_A = _C_KVLAT + KV_LORA
_C_KPE_B = _C_KPE_A + HEAD_PAD
_C_U = _C_KPE_B + HEAD_PAD


def _pre_kernel(tiles_per_seq, x_ref, mod_ref, g_ref, wlat_ref, wconv_ref, wgate_ref, qg_ref,
                wq_ref, wqs_ref, kvg_ref, wk_ref, wv_ref, pos_ref, freq_ref, cw_ref, wuc_ref,
                q_ref, k_ref, v_ref, sga_ref, gc_ref, carry_ref):
    i = pl.program_id(0)
    tm = x_ref.shape[0]
    mod = mod_ref[...]
    h = _rms(x_ref[...], g_ref[...]) * (1.0 + mod[1:2]) + mod[0:1]
    hb = h.astype(BF16)

    ang = freq_ref[...] * pos_ref[...]
    cos_t, sin_t = jnp.cos(ang), jnp.sin(ang)
    ones_t = jnp.ones((QK_NOPE, tm), F32)
    zeros_t = jnp.zeros((QK_NOPE, tm), F32)
    pad_t = jnp.zeros((HEAD_PAD - QK_HEAD, tm), F32)
    cosf = jnp.concatenate([ones_t, cos_t, cos_t, pad_t], axis=0).T
    sinf = jnp.concatenate([zeros_t, -sin_t, sin_t, pad_t], axis=0).T
    cos8 = jnp.concatenate([cosf] * N_HEADS, axis=-1)
    sin8 = jnp.concatenate([sinf] * N_HEADS, axis=-1)

    small = _dot(hb, wlat_ref[...])
    q_lat = small[:, _C_QLAT:_C_KVLAT]
    kv_lat = small[:, _C_KVLAT:_C_KPE_A]
    kpe = small[:, _C_KPE_A:_C_KPE_B] * cosf + small[:, _C_KPE_B:_C_U] * sinf
    qn = _rms(q_lat, qg_ref[...]).astype(BF16)
    q = _dot(qn, wq_ref[...]) * cos8 + _dot(qn, wqs_ref[...]) * sin8
    q_ref[...] = (q * Q_PRESCALE).astype(BF16)
    kvn = _rms(kv_lat, kvg_ref[...]).astype(BF16)
    k = _dot(kvn, wk_ref[...]) + jnp.concatenate([kpe] * N_HEADS, axis=-1)
    k_ref[...] = k.astype(BF16)
    lane = lax.broadcasted_iota(jnp.int32, (tm, N_HEADS * HEAD_PAD), 1)
    ones_col = jnp.where(lane % HEAD_PAD == V_HEAD, 1.0, 0.0)
    v_ref[...] = (_dot(kvn, wv_ref[...]) + ones_col).astype(BF16)

    ucb = _dot(hb, wconv_ref[...])
    cu = ucb[:, 0:CONV_WIDTH] * ucb[:, CONV_WIDTH:2 * CONV_WIDTH]
    b_gate = ucb[:, 2 * CONV_WIDTH:3 * CONV_WIDTH]

    @pl.when(i % tiles_per_seq == 0)
    def _():
        carry_ref[...] = jnp.zeros_like(carry_ref)

    prev = carry_ref[...]
    row = lax.broadcasted_iota(jnp.int32, cu.shape, 0)
    cu1 = jnp.where(row == 0, prev[7:8], pltpu.roll(cu, 1, 0))
    cu2 = jnp.where(row == 0, prev[6:7], jnp.where(row == 1, prev[7:8], pltpu.roll(cu, 2, 0)))
    cw = cw_ref[...]
    z = cw[2:3] * cu + cw[1:2] * cu1 + cw[0:1] * cu2
    carry_ref[...] = cu[tm - 8:tm]
    c_branch = _dot((b_gate * z).astype(BF16), wuc_ref[...])

    gates = _dot(hb, wgate_ref[...])
    sga_ref[...] = jax.nn.sigmoid(gates[:, 0:D_MODEL]).astype(BF16)
    gc_ref[...] = (jax.nn.sigmoid(gates[:, D_MODEL:]) * c_branch).astype(BF16)


def _pre(x2, mod, norm_g, w_lat, w_conv, w_gate, q_norm_g, wq2, wq2s, kv_norm_g, wk2, wv, pos,
         freqs, conv_w, w_up_conv, seq):
    t = x2.shape[0]
    tm = ROW_TILE
    tiles_per_seq = seq // tm
    full = lambda a: pl.BlockSpec(a.shape, lambda i: (0,) * a.ndim)
    rows = lambda w: pl.BlockSpec((tm, w), lambda i: (i, 0))
    outs = [jax.ShapeDtypeStruct((t, N_HEADS * HEAD_PAD), BF16),
            jax.ShapeDtypeStruct((t, N_HEADS * HEAD_PAD), BF16),
            jax.ShapeDtypeStruct((t, N_HEADS * HEAD_PAD), BF16),
            jax.ShapeDtypeStruct((t, D_MODEL), BF16),
            jax.ShapeDtypeStruct((t, D_MODEL), BF16)]
    return pl.pallas_call(
        functools.partial(_pre_kernel, tiles_per_seq),
        out_shape=outs,
        grid=(t // tm,),
        in_specs=[
            rows(D_MODEL),
            pl.BlockSpec((None, 8, D_MODEL), lambda i: (i // tiles_per_seq, 0, 0)),
            full(norm_g), full(w_lat), full(w_conv), full(w_gate), full(q_norm_g), full(wq2),
            full(wq2s), full(kv_norm_g), full(wk2), full(wv),
            pl.BlockSpec((None, 1, tm), lambda i: (i, 0, 0)), full(freqs), full(conv_w),
            full(w_up_conv),
        ],
        out_specs=[rows(N_HEADS * HEAD_PAD), rows(N_HEADS * HEAD_PAD), rows(N_HEADS * HEAD_PAD),
                   rows(D_MODEL), rows(D_MODEL)],
        scratch_shapes=[pltpu.VMEM((8, CONV_WIDTH), F32)],
        compiler_params=pltpu.CompilerParams(
            dimension_semantics=("arbitrary",), vmem_limit_bytes=VMEM_LIMIT),
        name="pre_mixer",
    )(x2, mod, norm_g, w_lat, w_conv, w_gate, q_norm_g, wq2, wq2s, kv_norm_g, wk2, wv, pos,
      freqs, conv_w, w_up_conv)


def _attn_kernel(q_ref, k_ref, v_ref, o_ref, m_ref, acc_ref):
    i = pl.program_id(1)
    tq = q_ref.shape[0]

    m_ref[...] = jnp.full_like(m_ref, NEG_BIG)
    acc_ref[...] = jnp.zeros_like(acc_ref)

    def step(k0, tk, masked):
        if masked:
            rq = (lax.broadcasted_iota(jnp.int32, (tq, tk), 0) + (tk - tq)) // CHUNK
            ck = lax.broadcasted_iota(jnp.int32, (tq, tk), 1) // CHUNK
            allowed = ck <= rq
        for hd in range(N_HEADS):
            hs = slice(hd * HEAD_PAD, (hd + 1) * HEAD_PAD)
            s = lax.dot_general(q_ref[:, hs], k_ref[pl.ds(k0, tk), hs],
                                (((1,), (1,)), ((), ())), preferred_element_type=F32)
            if masked:
                s = jnp.where(allowed, s, NEG_BIG)
            m_old = m_ref[hd]
            s_max = s[:, 0:LANES]
            for c in range(1, tk // LANES):
                s_max = jnp.maximum(s_max, s[:, c * LANES:(c + 1) * LANES])
            m_new = jnp.maximum(m_old, jnp.max(s_max, axis=-1, keepdims=True))
            alpha = jnp.exp2(m_old - m_new)
            p = jnp.concatenate(
                [jnp.exp2(s[:, c * LANES:(c + 1) * LANES] - m_new).astype(BF16)
                 for c in range(tk // LANES)], axis=-1)
            acc_ref[hd] = alpha * acc_ref[hd] + _dot(p, v_ref[pl.ds(k0, tk), hs])
            m_ref[hd] = m_new

    wide = ATT_WIDE * tq

    def body(j, carry):
        step(pl.multiple_of(j * wide, wide), wide, False)
        return carry

    lax.fori_loop(0, i // ATT_WIDE, body, 0)

    for r in range(ATT_WIDE):
        @pl.when(i % ATT_WIDE == r)
        def _():
            step(pl.multiple_of((i - r) * tq, tq), (r + 1) * tq, True)

    for hp in range(N_HEADS // 2):
        pair = []
        for hd in (2 * hp, 2 * hp + 1):
            acc = acc_ref[hd]
            pair.append(acc[:, 0:V_HEAD] / acc[:, V_HEAD:V_HEAD + 1])
        o_ref[:, hp * LANES:(hp + 1) * LANES] = jnp.concatenate(pair, axis=-1).astype(BF16)


def _attention(q, k, v, batch, seq):
    tq = ATT_BLOCK
    nq = seq // tq
    return pl.pallas_call(
        _attn_kernel,
        out_shape=jax.ShapeDtypeStruct((batch * seq, N_HEADS * V_HEAD), BF16),
        grid=(batch, nq),
        in_specs=[
            pl.BlockSpec((tq, N_HEADS * HEAD_PAD), lambda b, i: (b * nq + i, 0)),
            pl.BlockSpec((seq, N_HEADS * HEAD_PAD), lambda b, i: (b, 0)),
            pl.BlockSpec((seq, N_HEADS * HEAD_PAD), lambda b, i: (b, 0)),
        ],
        out_specs=pl.BlockSpec((tq, N_HEADS * V_HEAD), lambda b, i: (b * nq + i, 0)),
        scratch_shapes=[pltpu.VMEM((N_HEADS, tq, LANES), F32),
                        pltpu.VMEM((N_HEADS, tq, LANES), F32)],
        compiler_params=pltpu.CompilerParams(
            dimension_semantics=("arbitrary", "arbitrary"), vmem_limit_bytes=VMEM_LIMIT),
        name="attention",
    )(q, k, v)


def _post_kernel(attn_ref, sga_ref, gc_ref, x_ref, mod_ref, wua_ref, wo_ref, g_ref, rwh_ref,
                 rwl_ref, rb_ref, x1_ref, h2_ref, idx_ref, gate_ref, rank_ref, cnt_out_ref, cnt_ref):
    @pl.when(pl.program_id(0) == 0)
    def _():
        cnt_ref[...] = jnp.zeros_like(cnt_ref)

    counts = cnt_ref[...]
    for r0 in range(0, x_ref.shape[0], POST_SUB):
        counts = _post_rows(slice(r0, r0 + POST_SUB), counts, attn_ref, sga_ref, gc_ref, x_ref,
                            mod_ref, wua_ref, wo_ref, g_ref, rwh_ref, rwl_ref, rb_ref, x1_ref,
                            h2_ref, idx_ref, gate_ref, rank_ref)
    cnt_ref[...] = counts
    cnt_out_ref[...] = counts.astype(jnp.int32)


def _post_rows(rs, counts, attn_ref, sga_ref, gc_ref, x_ref, mod_ref, wua_ref, wo_ref, g_ref,
               rwh_ref, rwl_ref, rb_ref, x1_ref, h2_ref, idx_ref, gate_ref, rank_ref):
    mod = mod_ref[...]
    a_branch = _dot(attn_ref[rs, :], wua_ref[...])
    merged = sga_ref[rs, :].astype(F32) * a_branch + gc_ref[rs, :].astype(F32)
    mix = _dot(merged.astype(BF16), wo_ref[...])
    x1 = x_ref[rs, :] + mod[2:3] * mix
    x1_ref[rs, :] = x1
    h2 = _rms(x1, g_ref[...]) * (1.0 + mod[4:5]) + mod[3:4]
    h2_ref[rs, :] = _pack_row(h2)

    h_hi = h2.astype(BF16)
    h_lo = (h2 - h_hi.astype(F32)).astype(BF16)
    logits = (_dot(h_hi, rwh_ref[...]) + _dot(h_lo, rwh_ref[...]) + _dot(h_hi, rwl_ref[...])
              + rb_ref[...])
    lane = lax.broadcasted_iota(jnp.int32, logits.shape, 1)
    work = logits
    vals, idxs = [], []
    for _ in range(TOP_K):
        mk = jnp.max(work, axis=-1, keepdims=True)
        ik = jnp.min(jnp.where(work == mk, lane, LANES), axis=-1, keepdims=True)
        vals.append(mk)
        idxs.append(ik)
        work = jnp.where(lane == ik, -jnp.inf, work)
    es = [jnp.exp(vk - vals[0]) for vk in vals]
    denom = es[0] + es[1] + es[2] + es[3]
    tm = logits.shape[0]
    chosen = jnp.zeros(logits.shape, F32)
    for kk in range(TOP_K):
        chosen = chosen + jnp.where(lane == idxs[kk], 1.0, 0.0)
    r_i = lax.broadcasted_iota(jnp.int32, (tm, tm), 0)
    c_i = lax.broadcasted_iota(jnp.int32, (tm, tm), 1)
    earlier = jnp.where(c_i < r_i, 1.0, 0.0).astype(BF16)
    before = _dot(earlier, chosen.astype(BF16)) + counts[0:1]

    idx_out = jnp.zeros(logits.shape, jnp.int32)
    gate_out = jnp.zeros(logits.shape, F32)
    rank_out = jnp.zeros(logits.shape, jnp.int32)
    for kk in range(TOP_K):
        rank_k = jnp.sum(jnp.where(lane == idxs[kk], before, 0.0), axis=-1, keepdims=True)
        idx_out = jnp.where(lane == kk, idxs[kk], idx_out)
        gate_out = jnp.where(lane == kk, es[kk] / denom, gate_out)
        rank_out = jnp.where(lane == kk, rank_k.astype(jnp.int32), rank_out)
    idx_ref[rs, :] = idx_out
    gate_ref[rs, :] = gate_out
    rank_ref[rs, :] = rank_out
    return counts + jnp.sum(chosen, axis=0, keepdims=True)


def _post(attn, sga, gc, x2, mod, wua, wo, norm_g, rw_hi, rw_lo, rb_pad, seq):
    t = x2.shape[0]
    tm = POST_TILE
    tiles_per_seq = seq // tm
    full = lambda a: pl.BlockSpec(a.shape, lambda i: (0,) * a.ndim)
    rows = lambda w: pl.BlockSpec((tm, w), lambda i: (i, 0))
    outs = [jax.ShapeDtypeStruct((t, D_MODEL), F32),
            jax.ShapeDtypeStruct((t, PACKED), jnp.uint32),
            jax.ShapeDtypeStruct((t, LANES), jnp.int32),
            jax.ShapeDtypeStruct((t, LANES), F32),
            jax.ShapeDtypeStruct((t, LANES), jnp.int32),
            jax.ShapeDtypeStruct((8, LANES), jnp.int32)]
    return pl.pallas_call(
        _post_kernel,
        out_shape=outs,
        grid=(t // tm,),
        in_specs=[
            rows(N_HEADS * V_HEAD), rows(D_MODEL), rows(D_MODEL), rows(D_MODEL),
            pl.BlockSpec((None, 8, D_MODEL), lambda i: (i // tiles_per_seq, 0, 0)),
            full(wua), full(wo), full(norm_g), full(rw_hi), full(rw_lo), full(rb_pad),
        ],
        out_specs=[rows(D_MODEL), rows(PACKED), rows(LANES), rows(LANES), rows(LANES),
                   pl.BlockSpec((8, LANES), lambda i: (0, 0))],
        scratch_shapes=[pltpu.VMEM((8, LANES), F32)],
        compiler_params=pltpu.CompilerParams(
            dimension_semantics=("arbitrary",), vmem_limit_bytes=VMEM_LIMIT),
        name="post_mixer",
    )(attn, sga, gc, x2, mod, wua, wo, norm_g, rw_hi, rw_lo, rb_pad)


_GM_COUNT, _GM_FIRST, _GM_BLOCKS = range(3)
_ST_EXPERT, _ST_SLOT = range(2)


def _moe_kernel(gm_ref, xs_ref, wgu_hbm, bgu_ref, wd_hbm, bd_ref, o_ref,
                wgu_f, wd_f, wgu_bf, wd_bf, sem, st_ref):
    for r in range(MOE_STEP_BLOCKS):
        _moe_block(pl.program_id(0) * MOE_STEP_BLOCKS + r, slice(r * MOE_BLOCK, (r + 1) * MOE_BLOCK),
                   gm_ref, xs_ref, wgu_hbm, bgu_ref, wd_hbm, bd_ref, o_ref,
                   wgu_f, wd_f, wgu_bf, wd_bf, sem, st_ref)


def _moe_block(b, rs, gm_ref, xs_ref, wgu_hbm, bgu_ref, wd_hbm, bd_ref, o_ref,
               wgu_f, wd_f, wgu_bf, wd_bf, sem, st_ref):
    def weight_copies(expert, sl):
        return (pltpu.make_async_copy(wgu_hbm.at[expert], wgu_f.at[sl], sem.at[0, sl]),
                pltpu.make_async_copy(wd_hbm.at[expert], wd_f.at[sl], sem.at[1, sl]))

    def next_group(e):
        return lax.while_loop(
            lambda k: jnp.logical_and(k < N_EXPERTS,
                                      gm_ref[_GM_BLOCKS, jnp.minimum(k, N_EXPERTS - 1)] == 0),
            lambda k: k + 1, e)

    @pl.when(b == 0)
    def _():
        e0 = next_group(0)
        st_ref[_ST_EXPERT] = e0
        st_ref[_ST_SLOT] = 1
        for cp in weight_copies(e0, 0):
            cp.start()

    e_prev = st_ref[_ST_EXPERT]
    past = b >= gm_ref[_GM_FIRST, e_prev] + gm_ref[_GM_BLOCKS, e_prev]
    e = jnp.minimum(jnp.where(past, next_group(e_prev + 1), e_prev), N_EXPERTS - 1)
    st_ref[_ST_EXPERT] = e
    in_group = b - gm_ref[_GM_FIRST, e]
    used = jnp.logical_and(in_group >= 0, in_group < gm_ref[_GM_BLOCKS, e])
    n_valid = jnp.clip(gm_ref[_GM_COUNT, e] - in_group * MOE_BLOCK, 0, MOE_BLOCK)

    @pl.when(jnp.logical_and(used, in_group == 0))
    def _():
        slot = 1 - st_ref[_ST_SLOT]
        st_ref[_ST_SLOT] = slot
        nxt = next_group(e + 1)

        @pl.when(nxt < N_EXPERTS)
        def _():
            for cp in weight_copies(nxt, 1 - slot):
                cp.start(priority=1)

        for cp in weight_copies(e, slot):
            cp.wait()
        wgu_bf[...] = wgu_f[slot].astype(BF16)
        wd_bf[...] = wd_f[slot].astype(BF16)

    @pl.when(used)
    def _():
        row = lax.broadcasted_iota(jnp.int32, (MOE_BLOCK, PACKED), 0)
        xs = _unpack_row(jnp.where(row < n_valid, xs_ref[rs, :], 0)).astype(BF16)
        gu = _dot(xs, wgu_bf[...]) + bgu_ref[e]
        gate = jnp.minimum(gu[:, :D_EXPERT], SWIGLU_LIMIT)
        up = jnp.clip(gu[:, D_EXPERT:], -SWIGLU_LIMIT, SWIGLU_LIMIT)
        act = (up + 1.0) * (gate * jax.nn.sigmoid(gate * SWIGLU_ALPHA))
        o_ref[rs, :] = _pack_row(_dot(act.astype(BF16), wd_bf[...]) + bd_ref[e])

    @pl.when(jnp.logical_not(used))
    def _():
        o_ref[rs, :] = jnp.zeros((MOE_BLOCK, PACKED), o_ref.dtype)


def _moe(group_table, xs, w_gu, b_gu, w_down, b_down):
    n_rows = xs.shape[0]
    step_rows = MOE_STEP_BLOCKS * MOE_BLOCK
    assert n_rows % step_rows == 0
    grid_spec = pltpu.PrefetchScalarGridSpec(
        num_scalar_prefetch=1,
        grid=(n_rows // step_rows,),
        in_specs=[
            pl.BlockSpec((step_rows, PACKED), lambda b, gm: (b, 0)),
            pl.BlockSpec(memory_space=pl.ANY),
            pl.BlockSpec(b_gu.shape, lambda b, gm: (0, 0, 0)),
            pl.BlockSpec(memory_space=pl.ANY),
            pl.BlockSpec(b_down.shape, lambda b, gm: (0, 0, 0)),
        ],
        out_specs=pl.BlockSpec((step_rows, PACKED), lambda b, gm: (b, 0)),
        scratch_shapes=[pltpu.VMEM((2, D_MODEL, 2 * D_EXPERT), F32),
                        pltpu.VMEM((2, D_EXPERT, D_MODEL), F32),
                        pltpu.VMEM((D_MODEL, 2 * D_EXPERT), BF16),
                        pltpu.VMEM((D_EXPERT, D_MODEL), BF16),
                        pltpu.SemaphoreType.DMA((2, 2)),
                        pltpu.SMEM((2,), jnp.int32)],
    )
    return pl.pallas_call(
        _moe_kernel,
        out_shape=jax.ShapeDtypeStruct((n_rows, PACKED), jnp.uint32),
        grid_spec=grid_spec,
        compiler_params=pltpu.CompilerParams(
            dimension_semantics=("arbitrary",), vmem_limit_bytes=VMEM_LIMIT),
        name="moe_experts",
    )(group_table, xs, w_gu, b_gu, w_down, b_down)


def _final_kernel(last_layer, x1_ref, y_ref, gate_ref, mod_ref, g_ref, o_ref):
    mod = mod_ref[...]
    gate = gate_ref[...]
    ffn = gate[:, 0:1] * _unpack_row(y_ref[0])
    for kk in range(1, TOP_K):
        ffn = ffn + gate[:, kk:kk + 1] * _unpack_row(y_ref[kk])
    x = x1_ref[...] + mod[5:6] * ffn
    o_ref[...] = _rms(x, g_ref[...]) if last_layer else x


def _final(x1, y_kt, gate, mod, norm_g, seq, last_layer):
    t = x1.shape[0]
    tm = ROW_TILE
    tiles_per_seq = seq // tm
    rows = lambda w: pl.BlockSpec((tm, w), lambda i: (i, 0))
    return pl.pallas_call(
        functools.partial(_final_kernel, last_layer),
        out_shape=jax.ShapeDtypeStruct((t, D_MODEL), F32),
        grid=(t // tm,),
        in_specs=[
            rows(D_MODEL), pl.BlockSpec((TOP_K, tm, PACKED), lambda i: (0, i, 0)), rows(LANES),
            pl.BlockSpec((None, 8, D_MODEL), lambda i: (i // tiles_per_seq, 0, 0)),
            pl.BlockSpec(norm_g.shape, lambda i: (0, 0)),
        ],
        out_specs=rows(D_MODEL),
        compiler_params=pltpu.CompilerParams(
            dimension_semantics=("arbitrary",), vmem_limit_bytes=VMEM_LIMIT),
        name="combine_final",
    )(x1, y_kt, gate, mod, norm_g)


def _swap_halves(w):
    half = w.shape[-1] // 2
    return jnp.concatenate([w[..., half:], w[..., :half]], axis=-1)


def _prep_weights(w_in, w_uq, w_ukv):
    d = w_in.shape[0]
    splits = (Q_LORA, KV_LORA, QK_ROPE, CONV_WIDTH, CONV_WIDTH, CONV_WIDTH, D_MODEL, D_MODEL)
    offs = [0]
    for s in splits:
        offs.append(offs[-1] + s)
    part = lambda n: w_in[:, offs[n]:offs[n + 1]]
    z = lambda n: jnp.zeros((d, n), w_in.dtype)
    w_kpe = part(2)
    kpe_a = jnp.concatenate([z(QK_NOPE), w_kpe, z(HEAD_PAD - QK_HEAD)], axis=1)
    kpe_b = jnp.concatenate([z(QK_NOPE), _swap_halves(w_kpe), z(HEAD_PAD - QK_HEAD)], axis=1)
    w_lat = jnp.concatenate([part(0), part(1), kpe_a, kpe_b], axis=1).astype(BF16)
    w_conv = w_in[:, offs[3]:offs[6]].astype(BF16)
    w_gate = w_in[:, offs[6]:offs[8]].astype(BF16)

    wq = w_uq.reshape(Q_LORA, N_HEADS, QK_HEAD)
    zq = lambda n: jnp.zeros((Q_LORA, N_HEADS, n), w_uq.dtype)
    wq2 = jnp.concatenate([wq, zq(HEAD_PAD - QK_HEAD)], axis=-1)
    wq2s = jnp.concatenate([zq(QK_NOPE), _swap_halves(wq[..., QK_NOPE:]), zq(HEAD_PAD - QK_HEAD)],
                           axis=-1)
    wq2 = wq2.reshape(Q_LORA, N_HEADS * HEAD_PAD).astype(BF16)
    wq2s = wq2s.reshape(Q_LORA, N_HEADS * HEAD_PAD).astype(BF16)

    wkv = w_ukv.reshape(KV_LORA, N_HEADS, QK_NOPE + V_HEAD)
    wk2 = jnp.concatenate([wkv[..., :QK_NOPE],
                           jnp.zeros((KV_LORA, N_HEADS, HEAD_PAD - QK_NOPE), w_ukv.dtype)], axis=-1)
    wk2 = wk2.reshape(KV_LORA, N_HEADS * HEAD_PAD).astype(BF16)
    wv = jnp.concatenate([wkv[..., QK_NOPE:],
                          jnp.zeros((KV_LORA, N_HEADS, HEAD_PAD - V_HEAD), w_ukv.dtype)], axis=-1)
    wv = wv.reshape(KV_LORA, N_HEADS * HEAD_PAD).astype(BF16)
    return w_lat, w_conv, w_gate, wq2, wq2s, wk2, wv


def _rope_freqs():
    inv_freq = 1.0 / (ROPE_THETA ** (jnp.arange(0, QK_ROPE, 2, dtype=F32) / QK_ROPE))
    return inv_freq.reshape(QK_ROPE // 2, 1)


def _route(top_idx, rank, counts, n_tokens):
    blocks = (counts + MOE_BLOCK - 1) // MOE_BLOCK
    first_block = jnp.cumsum(blocks) - blocks
    experts = jnp.arange(N_EXPERTS, dtype=jnp.int32)
    start_of = jnp.sum(jnp.where(top_idx[:, :, None] == experts, first_block * MOE_BLOCK, 0),
                       axis=-1)
    dest = start_of + rank
    n_rows = n_tokens * TOP_K + N_EXPERTS * MOE_BLOCK
    table = jnp.stack([counts, first_block, blocks])
    return dest.astype(jnp.int32), table.astype(jnp.int32), n_rows


SC_CORES = 2
SC_SUBCORES = 16
SC_WORKERS = SC_CORES * SC_SUBCORES
SC_CHUNK = 64

def _sc_mesh():
    return plsc.VectorSubcoreMesh(core_axis_name="c", subcore_axis_name="s")


def _sc_worker():
    return lax.axis_index("s") * SC_CORES + lax.axis_index("c")


def _dispatch(h2, dest, n_rows):
    t, d = h2.shape
    per_w = t // SC_WORKERS
    n_chunks = per_w // SC_CHUNK
    assert per_w % (2 * SC_CHUNK) == 0
    idx = dest.reshape(SC_WORKERS, n_chunks, SC_CHUNK, TOP_K).transpose(0, 3, 1, 2)
    idx = idx.reshape(SC_WORKERS, TOP_K * n_chunks, SC_CHUNK)

    @functools.partial(
        pl.kernel, mesh=_sc_mesh(),
        out_type=jax.ShapeDtypeStruct((n_rows, d), h2.dtype),
        scratch_types=[pltpu.VMEM((TOP_K * n_chunks, SC_CHUNK), jnp.int32),
                       pltpu.VMEM((2, SC_CHUNK, d), h2.dtype),
                       pltpu.SemaphoreType.DMA((2,)),
                       pltpu.SemaphoreType.DMA((2,))],
        name="moe_dispatch")
    def run(h2_hbm, idx_hbm, xs_hbm, idx_v, rows_v, rsem, ssem):
        w = _sc_worker()
        pltpu.sync_copy(idx_hbm.at[w], idx_v)

        def read(g, b):
            src = h2_hbm.at[pl.ds(w * per_w + g * SC_CHUNK, SC_CHUNK)]
            return pltpu.make_async_copy(src, rows_v.at[b], rsem.at[b])

        def scatter(g, kk, b):
            dst = xs_hbm.at[idx_v.at[kk * n_chunks + g]]
            return pltpu.make_async_copy(rows_v.at[b], dst, ssem.at[b])

        read(0, 0).start()

        @pl.loop(0, n_chunks, step=2)
        def _(g0):
            for b in range(2):
                g = g0 + b
                read(g, b).wait()

                @pl.when(g + 1 < n_chunks)
                def _():
                    read(g + 1, 1 - b).start()

                for kk in range(TOP_K):
                    scatter(g, kk, b).start()
                for kk in range(TOP_K):
                    scatter(g, kk, b).wait()

    return run(h2, idx)


def _undispatch(ys, dest):
    t = dest.shape[0]
    d = ys.shape[1]
    n_out = t * TOP_K
    per_w = n_out // SC_WORKERS
    n_chunks = per_w // SC_CHUNK
    assert per_w % (2 * SC_CHUNK) == 0
    idx = dest.T.reshape(SC_WORKERS, n_chunks, SC_CHUNK)

    @functools.partial(
        pl.kernel, mesh=_sc_mesh(),
        out_type=jax.ShapeDtypeStruct((n_out, d), ys.dtype),
        scratch_types=[pltpu.VMEM((n_chunks, SC_CHUNK), jnp.int32),
                       pltpu.VMEM((2, SC_CHUNK, d), ys.dtype),
                       pltpu.SemaphoreType.DMA((2,)),
                       pltpu.SemaphoreType.DMA((2,))],
        name="moe_undispatch")
    def run(ys_hbm, idx_hbm, out_hbm, idx_v, rows_v, gsem, wsem):
        w = _sc_worker()
        pltpu.sync_copy(idx_hbm.at[w], idx_v)

        def gather(g, b):
            return pltpu.make_async_copy(ys_hbm.at[idx_v.at[g]], rows_v.at[b], gsem.at[b])

        def write(g, b):
            dst = out_hbm.at[pl.ds(w * per_w + g * SC_CHUNK, SC_CHUNK)]
            return pltpu.make_async_copy(rows_v.at[b], dst, wsem.at[b])

        gather(0, 0).start()

        @pl.loop(0, n_chunks, step=2)
        def _(g0):
            for b in range(2):
                g = g0 + b
                gather(g, b).wait()

                @pl.when(g >= 1)
                def _():
                    write(g - 1, 1 - b).wait()

                @pl.when(g + 1 < n_chunks)
                def _():
                    gather(g + 1, 1 - b).start()

                write(g, b).start()

        write(n_chunks - 1, (n_chunks - 1) % 2).wait()

    return run(ys, idx).reshape(TOP_K, t, d)


def kernel(x, c, positions, w_ada, b_ada, norm_mix_g, w_in, q_norm_g, w_uq, kv_norm_g, w_ukv,
           w_up_attn, conv_w, w_up_conv, w_o, norm_ffn_g, router_w, router_b, w_gu, b_gu,
           w_down, b_down, norm_final_g):
    batch, seq, d = x.shape
    t = batch * seq
    depth = w_ada.shape[0]
    x2 = x.reshape(t, d)
    pos = positions.astype(F32).reshape(t // ROW_TILE, 1, ROW_TILE)
    freqs = _rope_freqs()
    c_pad = jnp.zeros((8, d), F32).at[:batch].set(c)

    for l in range(depth):
        ada = _ada(c_pad, w_ada[l], b_ada[l].reshape(1, -1))
        mod = ada[:batch].reshape(batch, 6, d)
        mod = jnp.concatenate([mod, jnp.zeros((batch, 2, d), F32)], axis=1)

        w_lat, w_conv, w_gate, wq2, wq2s, wk2, wv = _prep_weights(w_in[l], w_uq[l], w_ukv[l])
        q, k, v, sga, gc = _pre(x2, mod, norm_mix_g[l].reshape(1, d), w_lat, w_conv, w_gate,
                                q_norm_g[l].reshape(1, -1), wq2, wq2s,
                                kv_norm_g[l].reshape(1, -1), wk2, wv, pos, freqs, conv_w[l],
                                w_up_conv[l].astype(BF16), seq)
        attn = _attention(q, k, v, batch, seq)

        rw_pad = jnp.concatenate([router_w[l], jnp.zeros((d, LANES - N_EXPERTS), F32)], axis=1)
        rb_pad = jnp.concatenate([router_b[l], jnp.full((LANES - N_EXPERTS,), NEG_BIG, F32)])
        rw_hi = rw_pad.astype(BF16)
        rw_lo = (rw_pad - rw_hi.astype(F32)).astype(BF16)
        x1, h2, idx_pad, gate_pad, rank_pad, counts = _post(
            attn, sga, gc, x2, mod, w_up_attn[l].astype(BF16), w_o[l].astype(BF16),
            norm_ffn_g[l].reshape(1, d), rw_hi, rw_lo, rb_pad.reshape(1, LANES), seq)

        dest, group_table, n_rows = _route(
            idx_pad[:, :TOP_K], rank_pad[:, :TOP_K], counts[0, :N_EXPERTS], t)
        xs = _dispatch(h2, dest, n_rows)
        ys = _moe(group_table, xs, w_gu[l], b_gu[l].reshape(N_EXPERTS, 1, -1),
                  w_down[l], b_down[l].reshape(N_EXPERTS, 1, -1))
        y_kt = _undispatch(ys, dest)
        x2 = _final(x1, y_kt, gate_pad, mod, norm_final_g.reshape(1, d), seq, l == depth - 1)

    return x2.reshape(batch, seq, d)
```

```python
import functools
import math

import jax
import jax.numpy as jnp
from jax import lax
from jax.experimental import pallas as pl
from jax.experimental.pallas import tpu as pltpu
from jax.experimental.pallas import tpu_sc as plsc

D_MODEL = 1024
CHUNK = 64
N_HEADS = 8
Q_LORA = 256
KV_LORA = 128
QK_NOPE = 64
QK_ROPE = 32
V_HEAD = 64
QK_HEAD = QK_NOPE + QK_ROPE
ROPE_THETA = 10000.0
CONV_WIDTH = 512
CONV_K = 3
N_EXPERTS = 32
TOP_K = 4
D_EXPERT = 1024
SWIGLU_LIMIT = 7.0
SWIGLU_ALPHA = 1.702
MOE_BLOCK = 256
RMS_EPS = 1e-6

LANES = 128
HEAD_PAD = 128
NEG_BIG = -1e30
VMEM_LIMIT = 56 * 1024 * 1024

F32 = jnp.float32
BF16 = jnp.bfloat16

Q_PRESCALE = (QK_HEAD ** -0.5) * math.log2(math.e)

ROW_TILE = 1024
MOE_STEP_BLOCKS = 4
POST_TILE = 1024
POST_SUB = 1024
ATT_BLOCK = 512
ATT_WIDE = 2


def _rms(x, g):
    ms = jnp.mean(x * x, axis=-1, keepdims=True)
    return x * lax.rsqrt(ms + RMS_EPS) * g


def _dot(a, b):
    return jnp.dot(a, b, preferred_element_type=F32)


PACKED = D_MODEL // 2


def _pack_row(x):
    return pltpu.pack_elementwise([x[:, :PACKED], x[:, PACKED:]], packed_dtype=BF16)


def _unpack_row(w):
    half = lambda i: pltpu.unpack_elementwise(w, index=i, packed_dtype=BF16, unpacked_dtype=F32)
    return jnp.concatenate([half(0), half(1)], axis=-1)


def _ada_kernel(c_ref, w_ref, b_ref, o_ref):
    c = c_ref[...]
    ca = (c * jax.nn.sigmoid(c)).astype(BF16)
    o_ref[...] = _dot(ca, w_ref[...].astype(BF16)) + b_ref[...]


def _ada(c_pad, w_ada, b_ada):
    n = w_ada.shape[1]
    tn = 1024
    return pl.pallas_call(
        _ada_kernel,
        out_shape=jax.ShapeDtypeStruct((c_pad.shape[0], n), F32),
        grid=(n // tn,),
        in_specs=[
            pl.BlockSpec(c_pad.shape, lambda j: (0, 0)),
            pl.BlockSpec((D_MODEL, tn), lambda j: (0, j)),
            pl.BlockSpec((1, tn), lambda j: (0, j)),
        ],
        out_specs=pl.BlockSpec((c_pad.shape[0], tn), lambda j: (0, j)),
        compiler_params=pltpu.CompilerParams(
            dimension_semantics=("arbitrary",), vmem_limit_bytes=VMEM_LIMIT),
        name="ada",
    )(c_pad, w_ada, b_ada)


_C_QLAT = 0
_C_KVLAT = _C_QLAT + Q_LORA
_C_KPE_A = _C_KVLAT + KV_LORA
_C_KPE_B = _C_KPE_A + HEAD_PAD
_C_U = _C_KPE_B + HEAD_PAD


def _pre_kernel(tiles_per_seq, x_ref, mod_ref, g_ref, wlat_ref, wconv_ref, wgate_ref, qg_ref,
                wq_ref, wqs_ref, kvg_ref, wk_ref, wv_ref, pos_ref, freq_ref, cw_ref, wuc_ref,
                q_ref, k_ref, v_ref, sga_ref, gc_ref, carry_ref):
    i = pl.program_id(0)
    tm = x_ref.shape[0]
    mod = mod_ref[...]
    h = _rms(x_ref[...], g_ref[...]) * (1.0 + mod[1:2]) + mod[0:1]
    hb = h.astype(BF16)

    ang = freq_ref[...] * pos_ref[...]
    cos_t, sin_t = jnp.cos(ang), jnp.sin(ang)
    ones_t = jnp.ones((QK_NOPE, tm), F32)
    zeros_t = jnp.zeros((QK_NOPE, tm), F32)
    pad_t = jnp.zeros((HEAD_PAD - QK_HEAD, tm), F32)
    cosf = jnp.concatenate([ones_t, cos_t, cos_t, pad_t], axis=0).T
    sinf = jnp.concatenate([zeros_t, -sin_t, sin_t, pad_t], axis=0).T
    cos8 = jnp.concatenate([cosf] * N_HEADS, axis=-1)
    sin8 = jnp.concatenate([sinf] * N_HEADS, axis=-1)

    small = _dot(hb, wlat_ref[...])
    q_lat = small[:, _C_QLAT:_C_KVLAT]
    kv_lat = small[:, _C_KVLAT:_C_KPE_A]
    kpe = small[:, _C_KPE_A:_C_KPE_B] * cosf + small[:, _C_KPE_B:_C_U] * sinf
    qn = _rms(q_lat, qg_ref[...]).astype(BF16)
    q = _dot(qn, wq_ref[...]) * cos8 + _dot(qn, wqs_ref[...]) * sin8
    q_ref[...] = (q * Q_PRESCALE).astype(BF16)
    kvn = _rms(kv_lat, kvg_ref[...]).astype(BF16)
    k = _dot(kvn, wk_ref[...]) + jnp.concatenate([kpe] * N_HEADS, axis=-1)
    k_ref[...] = k.astype(BF16)
    lane = lax.broadcasted_iota(jnp.int32, (tm, N_HEADS * HEAD_PAD), 1)
    ones_col = jnp.where(lane % HEAD_PAD == V_HEAD, 1.0, 0.0)
    v_ref[...] = (_dot(kvn, wv_ref[...]) + ones_col).astype(BF16)

    ucb = _dot(hb, wconv_ref[...])
    cu = ucb[:, 0:CONV_WIDTH] * ucb[:, CONV_WIDTH:2 * CONV_WIDTH]
    b_gate = ucb[:, 2 * CONV_WIDTH:3 * CONV_WIDTH]

    @pl.when(i % tiles_per_seq == 0)
    def _():
        carry_ref[...] = jnp.zeros_like(carry_ref)

    prev = carry_ref[...]
    row = lax.broadcasted_iota(jnp.int32, cu.shape, 0)
    cu1 = jnp.where(row == 0, prev[7:8], pltpu.roll(cu, 1, 0))
    cu2 = jnp.where(row == 0, prev[6:7], jnp.where(row == 1, prev[7:8], pltpu.roll(cu, 2, 0)))
    cw = cw_ref[...]
    z = cw[2:3] * cu + cw[1:2] * cu1 + cw[0:1] * cu2
    carry_ref[...] = cu[tm - 8:tm]
    c_branch = _dot((b_gate * z).astype(BF16), wuc_ref[...])

    gates = _dot(hb, wgate_ref[...])
    sga_ref[...] = jax.nn.sigmoid(gates[:, 0:D_MODEL]).astype(BF16)
    gc_ref[...] = (jax.nn.sigmoid(gates[:, D_MODEL:]) * c_branch).astype(BF16)


def _pre(x2, mod, norm_g, w_lat, w_conv, w_gate, q_norm_g, wq2, wq2s, kv_norm_g, wk2, wv, pos,
         freqs, conv_w, w_up_conv, seq):
    t = x2.shape[0]
    tm = ROW_TILE
    tiles_per_seq = seq // tm
    full = lambda a: pl.BlockSpec(a.shape, lambda i: (0,) * a.ndim)
    rows = lambda w: pl.BlockSpec((tm, w), lambda i: (i, 0))
    outs = [jax.ShapeDtypeStruct((t, N_HEADS * HEAD_PAD), BF16),
            jax.ShapeDtypeStruct((t, N_HEADS * HEAD_PAD), BF16),
            jax.ShapeDtypeStruct((t, N_HEADS * HEAD_PAD), BF16),
            jax.ShapeDtypeStruct((t, D_MODEL), BF16),
            jax.ShapeDtypeStruct((t, D_MODEL), BF16)]
    return pl.pallas_call(
        functools.partial(_pre_kernel, tiles_per_seq),
        out_shape=outs,
        grid=(t // tm,),
        in_specs=[
            rows(D_MODEL),
            pl.BlockSpec((None, 8, D_MODEL), lambda i: (i // tiles_per_seq, 0, 0)),
            full(norm_g), full(w_lat), full(w_conv), full(w_gate), full(q_norm_g), full(wq2),
            full(wq2s), full(kv_norm_g), full(wk2), full(wv),
            pl.BlockSpec((None, 1, tm), lambda i: (i, 0, 0)), full(freqs), full(conv_w),
            full(w_up_conv),
        ],
        out_specs=[rows(N_HEADS * HEAD_PAD), rows(N_HEADS * HEAD_PAD), rows(N_HEADS * HEAD_PAD),
                   rows(D_MODEL), rows(D_MODEL)],
        scratch_shapes=[pltpu.VMEM((8, CONV_WIDTH), F32)],
        compiler_params=pltpu.CompilerParams(
            dimension_semantics=("arbitrary",), vmem_limit_bytes=VMEM_LIMIT),
        name="pre_mixer",
    )(x2, mod, norm_g, w_lat, w_conv, w_gate, q_norm_g, wq2, wq2s, kv_norm_g, wk2, wv, pos,
      freqs, conv_w, w_up_conv)


def _attn_kernel(q_ref, k_ref, v_ref, o_ref, m_ref, acc_ref):
    i = pl.program_id(1)
    tq = q_ref.shape[0]

    m_ref[...] = jnp.full_like(m_ref, NEG_BIG)
    acc_ref[...] = jnp.zeros_like(acc_ref)

    def step(k0, tk, masked):
        if masked:
            rq = (lax.broadcasted_iota(jnp.int32, (tq, tk), 0) + (tk - tq)) // CHUNK
            ck = lax.broadcasted_iota(jnp.int32, (tq, tk), 1) // CHUNK
            allowed = ck <= rq
        for hd in range(N_HEADS):
            hs = slice(hd * HEAD_PAD, (hd + 1) * HEAD_PAD)
            s = lax.dot_general(q_ref[:, hs], k_ref[pl.ds(k0, tk), hs],
                                (((1,), (1,)), ((), ())), preferred_element_type=F32)
            if masked:
                s = jnp.where(allowed, s, NEG_BIG)
            m_old = m_ref[hd]
            s_max = s[:, 0:LANES]
            for c in range(1, tk // LANES):
                s_max = jnp.maximum(s_max, s[:, c * LANES:(c + 1) * LANES])
            m_new = jnp.maximum(m_old, jnp.max(s_max, axis=-1, keepdims=True))
            alpha = jnp.exp2(m_old - m_new)
            p = jnp.concatenate(
                [jnp.exp2(s[:, c * LANES:(c + 1) * LANES] - m_new).astype(BF16)
                 for c in range(tk // LANES)], axis=-1)
            acc_ref[hd] = alpha * acc_ref[hd] + _dot(p, v_ref[pl.ds(k0, tk), hs])
            m_ref[hd] = m_new

    wide = ATT_WIDE * tq

    def body(j, carry):
        step(pl.multiple_of(j * wide, wide), wide, False)
        return carry

    lax.fori_loop(0, i // ATT_WIDE, body, 0)

    for r in range(ATT_WIDE):
        @pl.when(i % ATT_WIDE == r)
        def _():
            step(pl.multiple_of((i - r) * tq, tq), (r + 1) * tq, True)

    for hp in range(N_HEADS // 2):
        pair = []
        for hd in (2 * hp, 2 * hp + 1):
            acc = acc_ref[hd]
            pair.append(acc[:, 0:V_HEAD] / acc[:, V_HEAD:V_HEAD + 1])
        o_ref[:, hp * LANES:(hp + 1) * LANES] = jnp.concatenate(pair, axis=-1).astype(BF16)


def _attention(q, k, v, batch, seq):
    tq = ATT_BLOCK
    nq = seq // tq
    return pl.pallas_call(
        _attn_kernel,
        out_shape=jax.ShapeDtypeStruct((batch * seq, N_HEADS * V_HEAD), BF16),
        grid=(batch, nq),
        in_specs=[
            pl.BlockSpec((tq, N_HEADS * HEAD_PAD), lambda b, i: (b * nq + i, 0)),
            pl.BlockSpec((seq, N_HEADS * HEAD_PAD), lambda b, i: (b, 0)),
            pl.BlockSpec((seq, N_HEADS * HEAD_PAD), lambda b, i: (b, 0)),
        ],
        out_specs=pl.BlockSpec((tq, N_HEADS * V_HEAD), lambda b, i: (b * nq + i, 0)),
        scratch_shapes=[pltpu.VMEM((N_HEADS, tq, LANES), F32),
                        pltpu.VMEM((N_HEADS, tq, LANES), F32)],
        compiler_params=pltpu.CompilerParams(
            dimension_semantics=("arbitrary", "arbitrary"), vmem_limit_bytes=VMEM_LIMIT),
        name="attention",
    )(q, k, v)


def _post_kernel(attn_ref, sga_ref, gc_ref, x_ref, mod_ref, wua_ref, wo_ref, g_ref, rwh_ref,
                 rwl_ref, rb_ref, x1_ref, h2_ref, idx_ref, gate_ref, rank_ref, cnt_out_ref, cnt_ref):
    @pl.when(pl.program_id(0) == 0)
    def _():
        cnt_ref[...] = jnp.zeros_like(cnt_ref)

    counts = cnt_ref[...]
    for r0 in range(0, x_ref.shape[0], POST_SUB):
        counts = _post_rows(slice(r0, r0 + POST_SUB), counts, attn_ref, sga_ref, gc_ref, x_ref,
                            mod_ref, wua_ref, wo_ref, g_ref, rwh_ref, rwl_ref, rb_ref, x1_ref,
                            h2_ref, idx_ref, gate_ref, rank_ref)
    cnt_ref[...] = counts
    cnt_out_ref[...] = counts.astype(jnp.int32)


def _post_rows(rs, counts, attn_ref, sga_ref, gc_ref, x_ref, mod_ref, wua_ref, wo_ref, g_ref,
               rwh_ref, rwl_ref, rb_ref, x1_ref, h2_ref, idx_ref, gate_ref, rank_ref):
    mod = mod_ref[...]
    a_branch = _dot(attn_ref[rs, :], wua_ref[...])
    merged = sga_ref[rs, :].astype(F32) * a_branch + gc_ref[rs, :].astype(F32)
    mix = _dot(merged.astype(BF16), wo_ref[...])
    x1 = x_ref[rs, :] + mod[2:3] * mix
    x1_ref[rs, :] = x1
    h2 = _rms(x1, g_ref[...]) * (1.0 + mod[4:5]) + mod[3:4]
    h2_ref[rs, :] = _pack_row(h2)

    h_hi = h2.astype(BF16)
    h_lo = (h2 - h_hi.astype(F32)).astype(BF16)
    logits = (_dot(h_hi, rwh_ref[...]) + _dot(h_lo, rwh_ref[...]) + _dot(h_hi, rwl_ref[...])
              + rb_ref[...])
    lane = lax.broadcasted_iota(jnp.int32, logits.shape, 1)
    work = logits
    vals, idxs = [], []
    for _ in range(TOP_K):
        mk = jnp.max(work, axis=-1, keepdims=True)
        ik = jnp.min(jnp.where(work == mk, lane, LANES), axis=-1, keepdims=True)
        vals.append(mk)
        idxs.append(ik)
        work = jnp.where(lane == ik, -jnp.inf, work)
    es = [jnp.exp(vk - vals[0]) for vk in vals]
    denom = es[0] + es[1] + es[2] + es[3]
    tm = logits.shape[0]
    chosen = jnp.zeros(logits.shape, F32)
    for kk in range(TOP_K):
        chosen = chosen + jnp.where(lane == idxs[kk], 1.0, 0.0)
    r_i = lax.broadcasted_iota(jnp.int32, (tm, tm), 0)
    c_i = lax.broadcasted_iota(jnp.int32, (tm, tm), 1)
    earlier = jnp.where(c_i < r_i, 1.0, 0.0).astype(BF16)
    before = _dot(earlier, chosen.astype(BF16)) + counts[0:1]

    idx_out = jnp.zeros(logits.shape, jnp.int32)
    gate_out = jnp.zeros(logits.shape, F32)
    rank_out = jnp.zeros(logits.shape, jnp.int32)
    for kk in range(TOP_K):
        rank_k = jnp.sum(jnp.where(lane == idxs[kk], before, 0.0), axis=-1, keepdims=True)
        idx_out = jnp.where(lane == kk, idxs[kk], idx_out)
        gate_out = jnp.where(lane == kk, es[kk] / denom, gate_out)
        rank_out = jnp.where(lane == kk, rank_k.astype(jnp.int32), rank_out)
    idx_ref[rs, :] = idx_out
    gate_ref[rs, :] = gate_out
    rank_ref[rs, :] = rank_out
    return counts + jnp.sum(chosen, axis=0, keepdims=True)


def _post(attn, sga, gc, x2, mod, wua, wo, norm_g, rw_hi, rw_lo, rb_pad, seq):
    t = x2.shape[0]
    tm = POST_TILE
    tiles_per_seq = seq // tm
    full = lambda a: pl.BlockSpec(a.shape, lambda i: (0,) * a.ndim)
    rows = lambda w: pl.BlockSpec((tm, w), lambda i: (i, 0))
    outs = [jax.ShapeDtypeStruct((t, D_MODEL), F32),
            jax.ShapeDtypeStruct((t, PACKED), jnp.uint32),
            jax.ShapeDtypeStruct((t, LANES), jnp.int32),
            jax.ShapeDtypeStruct((t, LANES), F32),
            jax.ShapeDtypeStruct((t, LANES), jnp.int32),
            jax.ShapeDtypeStruct((8, LANES), jnp.int32)]
    return pl.pallas_call(
        _post_kernel,
        out_shape=outs,
        grid=(t // tm,),
        in_specs=[
            rows(N_HEADS * V_HEAD), rows(D_MODEL), rows(D_MODEL), rows(D_MODEL),
            pl.BlockSpec((None, 8, D_MODEL), lambda i: (i // tiles_per_seq, 0, 0)),
            full(wua), full(wo), full(norm_g), full(rw_hi), full(rw_lo), full(rb_pad),
        ],
        out_specs=[rows(D_MODEL), rows(PACKED), rows(LANES), rows(LANES), rows(LANES),
                   pl.BlockSpec((8, LANES), lambda i: (0, 0))],
        scratch_shapes=[pltpu.VMEM((8, LANES), F32)],
        compiler_params=pltpu.CompilerParams(
            dimension_semantics=("arbitrary",), vmem_limit_bytes=VMEM_LIMIT),
        name="post_mixer",
    )(attn, sga, gc, x2, mod, wua, wo, norm_g, rw_hi, rw_lo, rb_pad)


_GM_COUNT, _GM_FIRST, _GM_BLOCKS = range(3)
_ST_EXPERT, _ST_SLOT = range(2)


def _moe_kernel(gm_ref, xs_ref, wgu_hbm, bgu_ref, wd_hbm, bd_ref, o_ref,
                wgu_f, wd_f, wgu_bf, wd_bf, sem, st_ref):
    for r in range(MOE_STEP_BLOCKS):
        _moe_block(pl.program_id(0) * MOE_STEP_BLOCKS + r, slice(r * MOE_BLOCK, (r + 1) * MOE_BLOCK),
                   gm_ref, xs_ref, wgu_hbm, bgu_ref, wd_hbm, bd_ref, o_ref,
                   wgu_f, wd_f, wgu_bf, wd_bf, sem, st_ref)


def _moe_block(b, rs, gm_ref, xs_ref, wgu_hbm, bgu_ref, wd_hbm, bd_ref, o_ref,
               wgu_f, wd_f, wgu_bf, wd_bf, sem, st_ref):
    def weight_copies(expert, sl):
        return (pltpu.make_async_copy(wgu_hbm.at[expert], wgu_f.at[sl], sem.at[0, sl]),
                pltpu.make_async_copy(wd_hbm.at[expert], wd_f.at[sl], sem.at[1, sl]))

    def next_group(e):
        return lax.while_loop(
            lambda k: jnp.logical_and(k < N_EXPERTS,
                                      gm_ref[_GM_BLOCKS, jnp.minimum(k, N_EXPERTS - 1)] == 0),
            lambda k: k + 1, e)

    @pl.when(b == 0)
    def _():
        e0 = next_group(0)
        st_ref[_ST_EXPERT] = e0
        st_ref[_ST_SLOT] = 1
        for cp in weight_copies(e0, 0):
            cp.start()

    e_prev = st_ref[_ST_EXPERT]
    past = b >= gm_ref[_GM_FIRST, e_prev] + gm_ref[_GM_BLOCKS, e_prev]
    e = jnp.minimum(jnp.where(past, next_group(e_prev + 1), e_prev), N_EXPERTS - 1)
    st_ref[_ST_EXPERT] = e
    in_group = b - gm_ref[_GM_FIRST, e]
    used = jnp.logical_and(in_group >= 0, in_group < gm_ref[_GM_BLOCKS, e])
    n_valid = jnp.clip(gm_ref[_GM_COUNT, e] - in_group * MOE_BLOCK, 0, MOE_BLOCK)

    @pl.when(jnp.logical_and(used, in_group == 0))
    def _():
        slot = 1 - st_ref[_ST_SLOT]
        st_ref[_ST_SLOT] = slot
        nxt = next_group(e + 1)

        @pl.when(nxt < N_EXPERTS)
        def _():
            for cp in weight_copies(nxt, 1 - slot):
                cp.start(priority=1)

        for cp in weight_copies(e, slot):
            cp.wait()
        wgu_bf[...] = wgu_f[slot].astype(BF16)
        wd_bf[...] = wd_f[slot].astype(BF16)

    @pl.when(used)
    def _():
        row = lax.broadcasted_iota(jnp.int32, (MOE_BLOCK, PACKED), 0)
        xs = _unpack_row(jnp.where(row < n_valid, xs_ref[rs, :], 0)).astype(BF16)
        gu = _dot(xs, wgu_bf[...]) + bgu_ref[e]
        gate = jnp.minimum(gu[:, :D_EXPERT], SWIGLU_LIMIT)
        up = jnp.clip(gu[:, D_EXPERT:], -SWIGLU_LIMIT, SWIGLU_LIMIT)
        act = (up + 1.0) * (gate * jax.nn.sigmoid(gate * SWIGLU_ALPHA))
        o_ref[rs, :] = _pack_row(_dot(act.astype(BF16), wd_bf[...]) + bd_ref[e])

    @pl.when(jnp.logical_not(used))
    def _():
        o_ref[rs, :] = jnp.zeros((MOE_BLOCK, PACKED), o_ref.dtype)


def _moe(group_table, xs, w_gu, b_gu, w_down, b_down):
    n_rows = xs.shape[0]
    step_rows = MOE_STEP_BLOCKS * MOE_BLOCK
    assert n_rows % step_rows == 0
    grid_spec = pltpu.PrefetchScalarGridSpec(
        num_scalar_prefetch=1,
        grid=(n_rows // step_rows,),
        in_specs=[
            pl.BlockSpec((step_rows, PACKED), lambda b, gm: (b, 0)),
            pl.BlockSpec(memory_space=pl.ANY),
            pl.BlockSpec(b_gu.shape, lambda b, gm: (0, 0, 0)),
            pl.BlockSpec(memory_space=pl.ANY),
            pl.BlockSpec(b_down.shape, lambda b, gm: (0, 0, 0)),
        ],
        out_specs=pl.BlockSpec((step_rows, PACKED), lambda b, gm: (b, 0)),
        scratch_shapes=[pltpu.VMEM((2, D_MODEL, 2 * D_EXPERT), F32),
                        pltpu.VMEM((2, D_EXPERT, D_MODEL), F32),
                        pltpu.VMEM((D_MODEL, 2 * D_EXPERT), BF16),
                        pltpu.VMEM((D_EXPERT, D_MODEL), BF16),
                        pltpu.SemaphoreType.DMA((2, 2)),
                        pltpu.SMEM((2,), jnp.int32)],
    )
    return pl.pallas_call(
        _moe_kernel,
        out_shape=jax.ShapeDtypeStruct((n_rows, PACKED), jnp.uint32),
        grid_spec=grid_spec,
        compiler_params=pltpu.CompilerParams(
            dimension_semantics=("arbitrary",), vmem_limit_bytes=VMEM_LIMIT),
        name="moe_experts",
    )(group_table, xs, w_gu, b_gu, w_down, b_down)


def _final_kernel(last_layer, x1_ref, y_ref, gate_ref, mod_ref, g_ref, o_ref):
    mod = mod_ref[...]
    gate = gate_ref[...]
    ffn = gate[:, 0:1] * _unpack_row(y_ref[0])
    for kk in range(1, TOP_K):
        ffn = ffn + gate[:, kk:kk + 1] * _unpack_row(y_ref[kk])
    x = x1_ref[...] + mod[5:6] * ffn
    o_ref[...] = _rms(x, g_ref[...]) if last_layer else x


def _final(x1, y_kt, gate, mod, norm_g, seq, last_layer):
    t = x1.shape[0]
    tm = ROW_TILE
    tiles_per_seq = seq // tm
    rows = lambda w: pl.BlockSpec((tm, w), lambda i: (i, 0))
    return pl.pallas_call(
        functools.partial(_final_kernel, last_layer),
        out_shape=jax.ShapeDtypeStruct((t, D_MODEL), F32),
        grid=(t // tm,),
        in_specs=[
            rows(D_MODEL), pl.BlockSpec((TOP_K, tm, PACKED), lambda i: (0, i, 0)), rows(LANES),
            pl.BlockSpec((None, 8, D_MODEL), lambda i: (i // tiles_per_seq, 0, 0)),
            pl.BlockSpec(norm_g.shape, lambda i: (0, 0)),
        ],
        out_specs=rows(D_MODEL),
        compiler_params=pltpu.CompilerParams(
            dimension_semantics=("arbitrary",), vmem_limit_bytes=VMEM_LIMIT),
        name="combine_final",
    )(x1, y_kt, gate, mod, norm_g)


def _swap_halves(w):
    half = w.shape[-1] // 2
    return jnp.concatenate([w[..., half:], w[..., :half]], axis=-1)


def _prep_weights(w_in, w_uq, w_ukv):
    d = w_in.shape[0]
    splits = (Q_LORA, KV_LORA, QK_ROPE, CONV_WIDTH, CONV_WIDTH, CONV_WIDTH, D_MODEL, D_MODEL)
    offs = [0]
    for s in splits:
        offs.append(offs[-1] + s)
    part = lambda n: w_in[:, offs[n]:offs[n + 1]]
    z = lambda n: jnp.zeros((d, n), w_in.dtype)
    w_kpe = part(2)
    kpe_a = jnp.concatenate([z(QK_NOPE), w_kpe, z(HEAD_PAD - QK_HEAD)], axis=1)
    kpe_b = jnp.concatenate([z(QK_NOPE), _swap_halves(w_kpe), z(HEAD_PAD - QK_HEAD)], axis=1)
    w_lat = jnp.concatenate([part(0), part(1), kpe_a, kpe_b], axis=1).astype(BF16)
    w_conv = w_in[:, offs[3]:offs[6]].astype(BF16)
    w_gate = w_in[:, offs[6]:offs[8]].astype(BF16)

    wq = w_uq.reshape(Q_LORA, N_HEADS, QK_HEAD)
    zq = lambda n: jnp.zeros((Q_LORA, N_HEADS, n), w_uq.dtype)
    wq2 = jnp.concatenate([wq, zq(HEAD_PAD - QK_HEAD)], axis=-1)
    wq2s = jnp.concatenate([zq(QK_NOPE), _swap_halves(wq[..., QK_NOPE:]), zq(HEAD_PAD - QK_HEAD)],
                           axis=-1)
    wq2 = wq2.reshape(Q_LORA, N_HEADS * HEAD_PAD).astype(BF16)
    wq2s = wq2s.reshape(Q_LORA, N_HEADS * HEAD_PAD).astype(BF16)

    wkv = w_ukv.reshape(KV_LORA, N_HEADS, QK_NOPE + V_HEAD)
    wk2 = jnp.concatenate([wkv[..., :QK_NOPE],
                           jnp.zeros((KV_LORA, N_HEADS, HEAD_PAD - QK_NOPE), w_ukv.dtype)], axis=-1)
    wk2 = wk2.reshape(KV_LORA, N_HEADS * HEAD_PAD).astype(BF16)
    wv = jnp.concatenate([wkv[..., QK_NOPE:],
                          jnp.zeros((KV_LORA, N_HEADS, HEAD_PAD - V_HEAD), w_ukv.dtype)], axis=-1)
    wv = wv.reshape(KV_LORA, N_HEADS * HEAD_PAD).astype(BF16)
    return w_lat, w_conv, w_gate, wq2, wq2s, wk2, wv


def _rope_freqs():
    inv_freq = 1.0 / (ROPE_THETA ** (jnp.arange(0, QK_ROPE, 2, dtype=F32) / QK_ROPE))
    return inv_freq.reshape(QK_ROPE // 2, 1)


def _route(top_idx, rank, counts, n_tokens):
    blocks = (counts + MOE_BLOCK - 1) // MOE_BLOCK
    first_block = jnp.cumsum(blocks) - blocks
    experts = jnp.arange(N_EXPERTS, dtype=jnp.int32)
    start_of = jnp.sum(jnp.where(top_idx[:, :, None] == experts, first_block * MOE_BLOCK, 0),
                       axis=-1)
    dest = start_of + rank
    n_rows = n_tokens * TOP_K + N_EXPERTS * MOE_BLOCK
    table = jnp.stack([counts, first_block, blocks])
    return dest.astype(jnp.int32), table.astype(jnp.int32), n_rows


SC_CORES = 2
SC_SUBCORES = 16
SC_WORKERS = SC_CORES * SC_SUBCORES
SC_CHUNK = 64
SC_GATHER_RING = 3

def _sc_mesh():
    return plsc.VectorSubcoreMesh(core_axis_name="c", subcore_axis_name="s")


def _sc_worker():
    return lax.axis_index("s") * SC_CORES + lax.axis_index("c")


def _dispatch(h2, dest, n_rows):
    t, d = h2.shape
    per_w = t // SC_WORKERS
    n_chunks = per_w // SC_CHUNK
    assert per_w % (2 * SC_CHUNK) == 0
    idx = dest.reshape(SC_WORKERS, n_chunks, SC_CHUNK, TOP_K).transpose(0, 3, 1, 2)
    idx = idx.reshape(SC_WORKERS, TOP_K * n_chunks, SC_CHUNK)

    @functools.partial(
        pl.kernel, mesh=_sc_mesh(),
        out_type=jax.ShapeDtypeStruct((n_rows, d), h2.dtype),
        scratch_types=[pltpu.VMEM((TOP_K * n_chunks, SC_CHUNK), jnp.int32),
                       pltpu.VMEM((2, SC_CHUNK, d), h2.dtype),
                       pltpu.SemaphoreType.DMA((2,)),
                       pltpu.SemaphoreType.DMA((2,))],
        name="moe_dispatch")
    def run(h2_hbm, idx_hbm, xs_hbm, idx_v, rows_v, rsem, ssem):
        w = _sc_worker()
        pltpu.sync_copy(idx_hbm.at[w], idx_v)

        def read(g, b):
            src = h2_hbm.at[pl.ds(w * per_w + g * SC_CHUNK, SC_CHUNK)]
            return pltpu.make_async_copy(src, rows_v.at[b], rsem.at[b])

        def scatter(g, kk, b):
            dst = xs_hbm.at[idx_v.at[kk * n_chunks + g]]
            return pltpu.make_async_copy(rows_v.at[b], dst, ssem.at[b])

        read(0, 0).start()

        @pl.loop(0, n_chunks, step=2)
        def _(g0):
            for b in range(2):
                g = g0 + b
                read(g, b).wait()

                @pl.when(g + 1 < n_chunks)
                def _():
                    read(g + 1, 1 - b).start()

                for kk in range(TOP_K):
                    scatter(g, kk, b).start()
                for kk in range(TOP_K):
                    scatter(g, kk, b).wait()

    return run(h2, idx)


def _undispatch(ys, dest):
    t = dest.shape[0]
    d = ys.shape[1]
    n_out = t * TOP_K
    per_w = n_out // SC_WORKERS
    n_chunks = per_w // SC_CHUNK
    idx = dest.T.reshape(SC_WORKERS, n_chunks, SC_CHUNK)
    ring = SC_GATHER_RING

    @functools.partial(
        pl.kernel, mesh=_sc_mesh(),
        out_type=jax.ShapeDtypeStruct((n_out, d), ys.dtype),
        scratch_types=[pltpu.VMEM((n_chunks, SC_CHUNK), jnp.int32),
                       pltpu.VMEM((ring, SC_CHUNK, d), ys.dtype),
                       pltpu.SemaphoreType.DMA((ring,)),
                       pltpu.SemaphoreType.DMA((ring,))],
        name="moe_undispatch")
    def run(ys_hbm, idx_hbm, out_hbm, idx_v, rows_v, gsem, wsem):
        w = _sc_worker()
        pltpu.sync_copy(idx_hbm.at[w], idx_v)

        def gather(g):
            b = g % ring
            return pltpu.make_async_copy(ys_hbm.at[idx_v.at[g]], rows_v.at[b], gsem.at[b])

        def write(g):
            b = g % ring
            dst = out_hbm.at[pl.ds(w * per_w + g * SC_CHUNK, SC_CHUNK)]
            return pltpu.make_async_copy(rows_v.at[b], dst, wsem.at[b])

        for g in range(min(ring - 1, n_chunks)):
            gather(g).start()
        for g in range(n_chunks):
            gather(g).wait()
            ahead = g + ring - 1
            if ahead < n_chunks:
                if g >= 1:
                    write(g - 1).wait()
                gather(ahead).start()
            write(g).start()
        for g in range(max(n_chunks - ring, 0), n_chunks):
            write(g).wait()

    return run(ys, idx).reshape(TOP_K, t, d)


def kernel(x, c, positions, w_ada, b_ada, norm_mix_g, w_in, q_norm_g, w_uq, kv_norm_g, w_ukv,
           w_up_attn, conv_w, w_up_conv, w_o, norm_ffn_g, router_w, router_b, w_gu, b_gu,
           w_down, b_down, norm_final_g):
    batch, seq, d = x.shape
    t = batch * seq
    depth = w_ada.shape[0]
    x2 = x.reshape(t, d)
    pos = positions.astype(F32).reshape(t // ROW_TILE, 1, ROW_TILE)
    freqs = _rope_freqs()
    c_pad = jnp.zeros((8, d), F32).at[:batch].set(c)

    for l in range(depth):
        ada = _ada(c_pad, w_ada[l], b_ada[l].reshape(1, -1))
        mod = ada[:batch].reshape(batch, 6, d)
        mod = jnp.concatenate([mod, jnp.zeros((batch, 2, d), F32)], axis=1)

        w_lat, w_conv, w_gate, wq2, wq2s, wk2, wv = _prep_weights(w_in[l], w_uq[l], w_ukv[l])
        q, k, v, sga, gc = _pre(x2, mod, norm_mix_g[l].reshape(1, d), w_lat, w_conv, w_gate,
                                q_norm_g[l].reshape(1, -1), wq2, wq2s,
                                kv_norm_g[l].reshape(1, -1), wk2, wv, pos, freqs, conv_w[l],
                                w_up_conv[l].astype(BF16), seq)
        attn = _attention(q, k, v, batch, seq)

        rw_pad = jnp.concatenate([router_w[l], jnp.zeros((d, LANES - N_EXPERTS), F32)], axis=1)
        rb_pad = jnp.concatenate([router_b[l], jnp.full((LANES - N_EXPERTS,), NEG_BIG, F32)])
        rw_hi = rw_pad.astype(BF16)
        rw_lo = (rw_pad - rw_hi.astype(F32)).astype(BF16)
        x1, h2, idx_pad, gate_pad, rank_pad, counts = _post(
            attn, sga, gc, x2, mod, w_up_attn[l].astype(BF16), w_o[l].astype(BF16),
            norm_ffn_g[l].reshape(1, d), rw_hi, rw_lo, rb_pad.reshape(1, LANES), seq)

        dest, group_table, n_rows = _route(
            idx_pad[:, :TOP_K], rank_pad[:, :TOP_K], counts[0, :N_EXPERTS], t)
        xs = _dispatch(h2, dest, n_rows)
        ys = _moe(group_table, xs, w_gu[l], b_gu[l].reshape(N_EXPERTS, 1, -1),
                  w_down[l], b_down[l].reshape(N_EXPERTS, 1, -1))
        y_kt = _undispatch(ys, dest)
        x2 = _final(x1, y_kt, gate_pad, mod, norm_final_g.reshape(1, d), seq, l == depth - 1)

    return x2.reshape(batch, seq, d)
```

```python
import functools
import math

import jax
import jax.numpy as jnp
from jax import lax
from jax.experimental import pallas as pl
from jax.experimental.pallas import tpu as pltpu
from jax.experimental.pallas import tpu_sc as plsc

D_MODEL = 1024
CHUNK = 64
N_HEADS = 8
Q_LORA = 256
KV_LORA = 128
QK_NOPE = 64
QK_ROPE = 32
V_HEAD = 64
QK_HEAD = QK_NOPE + QK_ROPE
ROPE_THETA = 10000.0
CONV_WIDTH = 512
CONV_K = 3
N_EXPERTS = 32
TOP_K = 4
D_EXPERT = 1024
SWIGLU_LIMIT = 7.0
SWIGLU_ALPHA = 1.702
MOE_BLOCK = 256
RMS_EPS = 1e-6

LANES = 128
HEAD_PAD = 128
NEG_BIG = -1e30
VMEM_LIMIT = 56 * 1024 * 1024

F32 = jnp.float32
BF16 = jnp.bfloat16

Q_PRESCALE = (QK_HEAD ** -0.5) * math.log2(math.e)

ROW_TILE = 1024
MOE_STEP_BLOCKS = 4
POST_TILE = 1024
POST_SUB = 1024
ATT_BLOCK = 512
ATT_WIDE = 2


def _rms(x, g):
    ms = jnp.mean(x * x, axis=-1, keepdims=True)
    return x * lax.rsqrt(ms + RMS_EPS) * g


def _dot(a, b):
    return jnp.dot(a, b, preferred_element_type=F32)


PACKED = D_MODEL // 2


def _pack_row(x):
    return pltpu.pack_elementwise([x[:, :PACKED], x[:, PACKED:]], packed_dtype=BF16)


def _unpack_row(w):
    half = lambda i: pltpu.unpack_elementwise(w, index=i, packed_dtype=BF16, unpacked_dtype=F32)
    return jnp.concatenate([half(0), half(1)], axis=-1)


def _ada_kernel(c_ref, w_ref, b_ref, o_ref):
    c = c_ref[...]
    ca = (c * jax.nn.sigmoid(c)).astype(BF16)
    o_ref[...] = _dot(ca, w_ref[...].astype(BF16)) + b_ref[...]


def _ada(c_pad, w_ada, b_ada):
    n = w_ada.shape[1]
    tn = 1024
    return pl.pallas_call(
        _ada_kernel,
        out_shape=jax.ShapeDtypeStruct((c_pad.shape[0], n), F32),
        grid=(n // tn,),
        in_specs=[
            pl.BlockSpec(c_pad.shape, lambda j: (0, 0)),
            pl.BlockSpec((D_MODEL, tn), lambda j: (0, j)),
            pl.BlockSpec((1, tn), lambda j: (0, j)),
        ],
        out_specs=pl.BlockSpec((c_pad.shape[0], tn), lambda j: (0, j)),
        compiler_params=pltpu.CompilerParams(
            dimension_semantics=("arbitrary",), vmem_limit_bytes=VMEM_LIMIT),
        name="ada",
    )(c_pad, w_ada, b_ada)


_C_QLAT = 0
_C_KVLAT = _C_QLAT + Q_LORA
_C_KPE_A = _C_KVLAT + KV_LORA
_C_KPE_B = _C_KPE_A + HEAD_PAD
_C_U = _C_KPE_B + HEAD_PAD


def _pre_kernel(tiles_per_seq, x_ref, mod_ref, g_ref, wlat_ref, wconv_ref, wgate_ref, qg_ref,
                wq_ref, wqs_ref, kvg_ref, wk_ref, wv_ref, pos_ref, freq_ref, cw_ref, wuc_ref,
                q_ref, k_ref, v_ref, sga_ref, gc_ref, carry_ref):
    i = pl.program_id(0)
    tm = x_ref.shape[0]
    mod = mod_ref[...]
    h = _rms(x_ref[...], g_ref[...]) * (1.0 + mod[1:2]) + mod[0:1]
    hb = h.astype(BF16)

    ang = freq_ref[...] * pos_ref[...]
    cos_t, sin_t = jnp.cos(ang), jnp.sin(ang)
    ones_t = jnp.ones((QK_NOPE, tm), F32)
    zeros_t = jnp.zeros((QK_NOPE, tm), F32)
    pad_t = jnp.zeros((HEAD_PAD - QK_HEAD, tm), F32)
    cosf = jnp.concatenate([ones_t, cos_t, cos_t, pad_t], axis=0).T
    sinf = jnp.concatenate([zeros_t, -sin_t, sin_t, pad_t], axis=0).T
    cos8 = jnp.concatenate([cosf] * N_HEADS, axis=-1)
    sin8 = jnp.concatenate([sinf] * N_HEADS, axis=-1)

    small = _dot(hb, wlat_ref[...])
    q_lat = small[:, _C_QLAT:_C_KVLAT]
    kv_lat = small[:, _C_KVLAT:_C_KPE_A]
    kpe = small[:, _C_KPE_A:_C_KPE_B] * cosf + small[:, _C_KPE_B:_C_U] * sinf
    qn = _rms(q_lat, qg_ref[...]).astype(BF16)
    q = _dot(qn, wq_ref[...]) * cos8 + _dot(qn, wqs_ref[...]) * sin8
    q_ref[...] = (q * Q_PRESCALE).astype(BF16)
    kvn = _rms(kv_lat, kvg_ref[...]).astype(BF16)
    k = _dot(kvn, wk_ref[...]) + jnp.concatenate([kpe] * N_HEADS, axis=-1)
    k_ref[...] = k.astype(BF16)
    lane = lax.broadcasted_iota(jnp.int32, (tm, N_HEADS * HEAD_PAD), 1)
    ones_col = jnp.where(lane % HEAD_PAD == V_HEAD, 1.0, 0.0)
    v_ref[...] = (_dot(kvn, wv_ref[...]) + ones_col).astype(BF16)

    ucb = _dot(hb, wconv_ref[...])
    cu = ucb[:, 0:CONV_WIDTH] * ucb[:, CONV_WIDTH:2 * CONV_WIDTH]
    b_gate = ucb[:, 2 * CONV_WIDTH:3 * CONV_WIDTH]

    @pl.when(i % tiles_per_seq == 0)
    def _():
        carry_ref[...] = jnp.zeros_like(carry_ref)

    prev = carry_ref[...]
    row = lax.broadcasted_iota(jnp.int32, cu.shape, 0)
    cu1 = jnp.where(row == 0, prev[7:8], pltpu.roll(cu, 1, 0))
    cu2 = jnp.where(row == 0, prev[6:7], jnp.where(row == 1, prev[7:8], pltpu.roll(cu, 2, 0)))
    cw = cw_ref[...]
    z = cw[2:3] * cu + cw[1:2] * cu1 + cw[0:1] * cu2
    carry_ref[...] = cu[tm - 8:tm]
    c_branch = _dot((b_gate * z).astype(BF16), wuc_ref[...])

    gates = _dot(hb, wgate_ref[...])
    sga_ref[...] = jax.nn.sigmoid(gates[:, 0:D_MODEL]).astype(BF16)
    gc_ref[...] = (jax.nn.sigmoid(gates[:, D_MODEL:]) * c_branch).astype(BF16)


def _pre(x2, mod, norm_g, w_lat, w_conv, w_gate, q_norm_g, wq2, wq2s, kv_norm_g, wk2, wv, pos,
         freqs, conv_w, w_up_conv, seq):
    t = x2.shape[0]
    tm = ROW_TILE
    tiles_per_seq = seq // tm
    full = lambda a: pl.BlockSpec(a.shape, lambda i: (0,) * a.ndim)
    rows = lambda w: pl.BlockSpec((tm, w), lambda i: (i, 0))
    outs = [jax.ShapeDtypeStruct((t, N_HEADS * HEAD_PAD), BF16),
            jax.ShapeDtypeStruct((t, N_HEADS * HEAD_PAD), BF16),
            jax.ShapeDtypeStruct((t, N_HEADS * HEAD_PAD), BF16),
            jax.ShapeDtypeStruct((t, D_MODEL), BF16),
            jax.ShapeDtypeStruct((t, D_MODEL), BF16)]
    return pl.pallas_call(
        functools.partial(_pre_kernel, tiles_per_seq),
        out_shape=outs,
        grid=(t // tm,),
        in_specs=[
            rows(D_MODEL),
            pl.BlockSpec((None, 8, D_MODEL), lambda i: (i // tiles_per_seq, 0, 0)),
            full(norm_g), full(w_lat), full(w_conv), full(w_gate), full(q_norm_g), full(wq2),
            full(wq2s), full(kv_norm_g), full(wk2), full(wv),
            pl.BlockSpec((None, 1, tm), lambda i: (i, 0, 0)), full(freqs), full(conv_w),
            full(w_up_conv),
        ],
        out_specs=[rows(N_HEADS * HEAD_PAD), rows(N_HEADS * HEAD_PAD), rows(N_HEADS * HEAD_PAD),
                   rows(D_MODEL), rows(D_MODEL)],
        scratch_shapes=[pltpu.VMEM((8, CONV_WIDTH), F32)],
        compiler_params=pltpu.CompilerParams(
            dimension_semantics=("arbitrary",), vmem_limit_bytes=VMEM_LIMIT),
        name="pre_mixer",
    )(x2, mod, norm_g, w_lat, w_conv, w_gate, q_norm_g, wq2, wq2s, kv_norm_g, wk2, wv, pos,
      freqs, conv_w, w_up_conv)


def _attn_kernel(q_ref, k_ref, v_ref, o_ref, m_ref, acc_ref):
    i = pl.program_id(1)
    tq = q_ref.shape[0]

    m_ref[...] = jnp.full_like(m_ref, NEG_BIG)
    acc_ref[...] = jnp.zeros_like(acc_ref)

    def step(k0, tk, masked):
        if masked:
            rq = (lax.broadcasted_iota(jnp.int32, (tq, tk), 0) + (tk - tq)) // CHUNK
            ck = lax.broadcasted_iota(jnp.int32, (tq, tk), 1) // CHUNK
            allowed = ck <= rq
        for hd in range(N_HEADS):
            hs = slice(hd * HEAD_PAD, (hd + 1) * HEAD_PAD)
            s = lax.dot_general(q_ref[:, hs], k_ref[pl.ds(k0, tk), hs],
                                (((1,), (1,)), ((), ())), preferred_element_type=F32)
            if masked:
                s = jnp.where(allowed, s, NEG_BIG)
            m_old = m_ref[hd]
            s_max = s[:, 0:LANES]
            for c in range(1, tk // LANES):
                s_max = jnp.maximum(s_max, s[:, c * LANES:(c + 1) * LANES])
            m_new = jnp.maximum(m_old, jnp.max(s_max, axis=-1, keepdims=True))
            alpha = jnp.exp2(m_old - m_new)
            p = jnp.concatenate(
                [jnp.exp2(s[:, c * LANES:(c + 1) * LANES] - m_new).astype(BF16)
                 for c in range(tk // LANES)], axis=-1)
            acc_ref[hd] = alpha * acc_ref[hd] + _dot(p, v_ref[pl.ds(k0, tk), hs])
            m_ref[hd] = m_new

    wide = ATT_WIDE * tq

    def body(j, carry):
        step(pl.multiple_of(j * wide, wide), wide, False)
        return carry

    lax.fori_loop(0, i // ATT_WIDE, body, 0)

    for r in range(ATT_WIDE):
        @pl.when(i % ATT_WIDE == r)
        def _():
            step(pl.multiple_of((i - r) * tq, tq), (r + 1) * tq, True)

    for hp in range(N_HEADS // 2):
        pair = []
        for hd in (2 * hp, 2 * hp + 1):
            acc = acc_ref[hd]
            pair.append(acc[:, 0:V_HEAD] / acc[:, V_HEAD:V_HEAD + 1])
        o_ref[:, hp * LANES:(hp + 1) * LANES] = jnp.concatenate(pair, axis=-1).astype(BF16)


def _attention(q, k, v, batch, seq):
    tq = ATT_BLOCK
    nq = seq // tq
    return pl.pallas_call(
        _attn_kernel,
        out_shape=jax.ShapeDtypeStruct((batch * seq, N_HEADS * V_HEAD), BF16),
        grid=(batch, nq),
        in_specs=[
            pl.BlockSpec((tq, N_HEADS * HEAD_PAD), lambda b, i: (b * nq + i, 0)),
            pl.BlockSpec((seq, N_HEADS * HEAD_PAD), lambda b, i: (b, 0)),
            pl.BlockSpec((seq, N_HEADS * HEAD_PAD), lambda b, i: (b, 0)),
        ],
        out_specs=pl.BlockSpec((tq, N_HEADS * V_HEAD), lambda b, i: (b * nq + i, 0)),
        scratch_shapes=[pltpu.VMEM((N_HEADS, tq, LANES), F32),
                        pltpu.VMEM((N_HEADS, tq, LANES), F32)],
        compiler_params=pltpu.CompilerParams(
            dimension_semantics=("arbitrary", "arbitrary"), vmem_limit_bytes=VMEM_LIMIT),
        name="attention",
    )(q, k, v)


def _post_kernel(attn_ref, sga_ref, gc_ref, x_ref, mod_ref, wua_ref, wo_ref, g_ref, rwh_ref,
                 rwl_ref, rb_ref, x1_ref, h2_ref, idx_ref, gate_ref, rank_ref, cnt_out_ref, cnt_ref):
    @pl.when(pl.program_id(0) == 0)
    def _():
        cnt_ref[...] = jnp.zeros_like(cnt_ref)

    counts = cnt_ref[...]
    for r0 in range(0, x_ref.shape[0], POST_SUB):
        counts = _post_rows(slice(r0, r0 + POST_SUB), counts, attn_ref, sga_ref, gc_ref, x_ref,
                            mod_ref, wua_ref, wo_ref, g_ref, rwh_ref, rwl_ref, rb_ref, x1_ref,
                            h2_ref, idx_ref, gate_ref, rank_ref)
    cnt_ref[...] = counts
    cnt_out_ref[...] = counts.astype(jnp.int32)


def _post_rows(rs, counts, attn_ref, sga_ref, gc_ref, x_ref, mod_ref, wua_ref, wo_ref, g_ref,
               rwh_ref, rwl_ref, rb_ref, x1_ref, h2_ref, idx_ref, gate_ref, rank_ref):
    mod = mod_ref[...]
    a_branch = _dot(attn_ref[rs, :], wua_ref[...])
    merged = sga_ref[rs, :].astype(F32) * a_branch + gc_ref[rs, :].astype(F32)
    mix = _dot(merged.astype(BF16), wo_ref[...])
    x1 = x_ref[rs, :] + mod[2:3] * mix
    x1_ref[rs, :] = x1
    h2 = _rms(x1, g_ref[...]) * (1.0 + mod[4:5]) + mod[3:4]
    h2_ref[rs, :] = _pack_row(h2)

    h_hi = h2.astype(BF16)
    h_lo = (h2 - h_hi.astype(F32)).astype(BF16)
    logits = (_dot(h_hi, rwh_ref[...]) + _dot(h_lo, rwh_ref[...]) + _dot(h_hi, rwl_ref[...])
              + rb_ref[...])
    lane = lax.broadcasted_iota(jnp.int32, logits.shape, 1)
    work = logits
    vals, idxs = [], []
    for _ in range(TOP_K):
        mk = jnp.max(work, axis=-1, keepdims=True)
        ik = jnp.min(jnp.where(work == mk, lane, LANES), axis=-1, keepdims=True)
        vals.append(mk)
        idxs.append(ik)
        work = jnp.where(lane == ik, -jnp.inf, work)
    es = [jnp.exp(vk - vals[0]) for vk in vals]
    denom = es[0] + es[1] + es[2] + es[3]
    tm = logits.shape[0]
    chosen = jnp.zeros(logits.shape, F32)
    for kk in range(TOP_K):
        chosen = chosen + jnp.where(lane == idxs[kk], 1.0, 0.0)
    r_i = lax.broadcasted_iota(jnp.int32, (tm, tm), 0)
    c_i = lax.broadcasted_iota(jnp.int32, (tm, tm), 1)
    earlier = jnp.where(c_i < r_i, 1.0, 0.0).astype(BF16)
    before = _dot(earlier, chosen.astype(BF16)) + counts[0:1]

    idx_out = jnp.zeros(logits.shape, F32)
    gate_out = jnp.zeros(logits.shape, F32)
    rank_out = jnp.zeros(logits.shape, F32)
    for kk in range(TOP_K):
        rank_k = jnp.sum(jnp.where(lane == idxs[kk], before, 0.0), axis=-1, keepdims=True)
        idx_out = jnp.where(lane == kk, idxs[kk].astype(F32), idx_out)
        gate_out = jnp.where(lane == kk, es[kk] / denom, gate_out)
        rank_out = jnp.where(lane == kk, rank_k, rank_out)
    gate_ref[rs, :] = gate_out
    idx_ref[:, rs] = idx_out.T[0:8].astype(jnp.int32)
    rank_ref[:, rs] = rank_out.T[0:8].astype(jnp.int32)
    return counts + jnp.sum(chosen, axis=0, keepdims=True)


def _post(attn, sga, gc, x2, mod, wua, wo, norm_g, rw_hi, rw_lo, rb_pad, seq):
    t = x2.shape[0]
    tm = POST_TILE
    tiles_per_seq = seq // tm
    full = lambda a: pl.BlockSpec(a.shape, lambda i: (0,) * a.ndim)
    rows = lambda w: pl.BlockSpec((tm, w), lambda i: (i, 0))
    outs = [jax.ShapeDtypeStruct((t, D_MODEL), F32),
            jax.ShapeDtypeStruct((t, PACKED), jnp.uint32),
            jax.ShapeDtypeStruct((8, t), jnp.int32),
            jax.ShapeDtypeStruct((t, LANES), F32),
            jax.ShapeDtypeStruct((8, t), jnp.int32),
            jax.ShapeDtypeStruct((8, LANES), jnp.int32)]
    slots = pl.BlockSpec((8, tm), lambda i: (0, i))
    return pl.pallas_call(
        _post_kernel,
        out_shape=outs,
        grid=(t // tm,),
        in_specs=[
            rows(N_HEADS * V_HEAD), rows(D_MODEL), rows(D_MODEL), rows(D_MODEL),
            pl.BlockSpec((None, 8, D_MODEL), lambda i: (i // tiles_per_seq, 0, 0)),
            full(wua), full(wo), full(norm_g), full(rw_hi), full(rw_lo), full(rb_pad),
        ],
        out_specs=[rows(D_MODEL), rows(PACKED), slots, rows(LANES), slots,
                   pl.BlockSpec((8, LANES), lambda i: (0, 0))],
        scratch_shapes=[pltpu.VMEM((8, LANES), F32)],
        compiler_params=pltpu.CompilerParams(
            dimension_semantics=("arbitrary",), vmem_limit_bytes=VMEM_LIMIT),
        name="post_mixer",
    )(attn, sga, gc, x2, mod, wua, wo, norm_g, rw_hi, rw_lo, rb_pad)


_GM_COUNT, _GM_FIRST, _GM_BLOCKS = range(3)
_ST_EXPERT, _ST_SLOT = range(2)


def _moe_kernel(gm_ref, xs_ref, wgu_hbm, bgu_ref, wd_hbm, bd_ref, o_ref,
                wgu_f, wd_f, wgu_bf, wd_bf, sem, st_ref):
    for r in range(MOE_STEP_BLOCKS):
        _moe_block(pl.program_id(0) * MOE_STEP_BLOCKS + r, slice(r * MOE_BLOCK, (r + 1) * MOE_BLOCK),
                   gm_ref, xs_ref, wgu_hbm, bgu_ref, wd_hbm, bd_ref, o_ref,
                   wgu_f, wd_f, wgu_bf, wd_bf, sem, st_ref)


def _moe_block(b, rs, gm_ref, xs_ref, wgu_hbm, bgu_ref, wd_hbm, bd_ref, o_ref,
               wgu_f, wd_f, wgu_bf, wd_bf, sem, st_ref):
    def weight_copies(expert, sl):
        return (pltpu.make_async_copy(wgu_hbm.at[expert], wgu_f.at[sl], sem.at[0, sl]),
                pltpu.make_async_copy(wd_hbm.at[expert], wd_f.at[sl], sem.at[1, sl]))

    def next_group(e):
        return lax.while_loop(
            lambda k: jnp.logical_and(k < N_EXPERTS,
                                      gm_ref[_GM_BLOCKS, jnp.minimum(k, N_EXPERTS - 1)] == 0),
            lambda k: k + 1, e)

    @pl.when(b == 0)
    def _():
        e0 = next_group(0)
        st_ref[_ST_EXPERT] = e0
        st_ref[_ST_SLOT] = 1
        for cp in weight_copies(e0, 0):
            cp.start()

    e_prev = st_ref[_ST_EXPERT]
    past = b >= gm_ref[_GM_FIRST, e_prev] + gm_ref[_GM_BLOCKS, e_prev]
    e = jnp.minimum(jnp.where(past, next_group(e_prev + 1), e_prev), N_EXPERTS - 1)
    st_ref[_ST_EXPERT] = e
    in_group = b - gm_ref[_GM_FIRST, e]
    used = jnp.logical_and(in_group >= 0, in_group < gm_ref[_GM_BLOCKS, e])
    n_valid = jnp.clip(gm_ref[_GM_COUNT, e] - in_group * MOE_BLOCK, 0, MOE_BLOCK)

    @pl.when(jnp.logical_and(used, in_group == 0))
    def _():
        slot = 1 - st_ref[_ST_SLOT]
        st_ref[_ST_SLOT] = slot
        nxt = next_group(e + 1)

        @pl.when(nxt < N_EXPERTS)
        def _():
            for cp in weight_copies(nxt, 1 - slot):
                cp.start(priority=1)

        for cp in weight_copies(e, slot):
            cp.wait()
        wgu_bf[...] = wgu_f[slot].astype(BF16)
        wd_bf[...] = wd_f[slot].astype(BF16)

    @pl.when(used)
    def _():
        row = lax.broadcasted_iota(jnp.int32, (MOE_BLOCK, PACKED), 0)
        xs = _unpack_row(jnp.where(row < n_valid, xs_ref[rs, :], 0)).astype(BF16)
        gu = _dot(xs, wgu_bf[...]) + bgu_ref[e]
        gate = jnp.minimum(gu[:, :D_EXPERT], SWIGLU_LIMIT)
        up = jnp.clip(gu[:, D_EXPERT:], -SWIGLU_LIMIT, SWIGLU_LIMIT)
        act = (up + 1.0) * (gate * jax.nn.sigmoid(gate * SWIGLU_ALPHA))
        o_ref[rs, :] = _pack_row(_dot(act.astype(BF16), wd_bf[...]) + bd_ref[e])

    @pl.when(jnp.logical_not(used))
    def _():
        o_ref[rs, :] = jnp.zeros((MOE_BLOCK, PACKED), o_ref.dtype)


def _moe(group_table, xs, w_gu, b_gu, w_down, b_down):
    n_rows = xs.shape[0]
    step_rows = MOE_STEP_BLOCKS * MOE_BLOCK
    assert n_rows % step_rows == 0
    grid_spec = pltpu.PrefetchScalarGridSpec(
        num_scalar_prefetch=1,
        grid=(n_rows // step_rows,),
        in_specs=[
            pl.BlockSpec((step_rows, PACKED), lambda b, gm: (b, 0)),
            pl.BlockSpec(memory_space=pl.ANY),
            pl.BlockSpec(b_gu.shape, lambda b, gm: (0, 0, 0)),
            pl.BlockSpec(memory_space=pl.ANY),
            pl.BlockSpec(b_down.shape, lambda b, gm: (0, 0, 0)),
        ],
        out_specs=pl.BlockSpec((step_rows, PACKED), lambda b, gm: (b, 0)),
        scratch_shapes=[pltpu.VMEM((2, D_MODEL, 2 * D_EXPERT), F32),
                        pltpu.VMEM((2, D_EXPERT, D_MODEL), F32),
                        pltpu.VMEM((D_MODEL, 2 * D_EXPERT), BF16),
                        pltpu.VMEM((D_EXPERT, D_MODEL), BF16),
                        pltpu.SemaphoreType.DMA((2, 2)),
                        pltpu.SMEM((2,), jnp.int32)],
    )
    return pl.pallas_call(
        _moe_kernel,
        out_shape=jax.ShapeDtypeStruct((n_rows, PACKED), jnp.uint32),
        grid_spec=grid_spec,
        compiler_params=pltpu.CompilerParams(
            dimension_semantics=("arbitrary",), vmem_limit_bytes=VMEM_LIMIT),
        name="moe_experts",
    )(group_table, xs, w_gu, b_gu, w_down, b_down)


def _final_kernel(last_layer, x1_ref, y_ref, gate_ref, mod_ref, g_ref, o_ref):
    mod = mod_ref[...]
    gate = gate_ref[...]
    ffn = gate[:, 0:1] * _unpack_row(y_ref[0])
    for kk in range(1, TOP_K):
        ffn = ffn + gate[:, kk:kk + 1] * _unpack_row(y_ref[kk])
    x = x1_ref[...] + mod[5:6] * ffn
    o_ref[...] = _rms(x, g_ref[...]) if last_layer else x


def _final(x1, y_kt, gate, mod, norm_g, seq, last_layer):
    t = x1.shape[0]
    tm = ROW_TILE
    tiles_per_seq = seq // tm
    rows = lambda w: pl.BlockSpec((tm, w), lambda i: (i, 0))
    return pl.pallas_call(
        functools.partial(_final_kernel, last_layer),
        out_shape=jax.ShapeDtypeStruct((t, D_MODEL), F32),
        grid=(t // tm,),
        in_specs=[
            rows(D_MODEL), pl.BlockSpec((TOP_K, tm, PACKED), lambda i: (0, i, 0)), rows(LANES),
            pl.BlockSpec((None, 8, D_MODEL), lambda i: (i // tiles_per_seq, 0, 0)),
            pl.BlockSpec(norm_g.shape, lambda i: (0, 0)),
        ],
        out_specs=rows(D_MODEL),
        compiler_params=pltpu.CompilerParams(
            dimension_semantics=("arbitrary",), vmem_limit_bytes=VMEM_LIMIT),
        name="combine_final",
    )(x1, y_kt, gate, mod, norm_g)


def _swap_halves(w):
    half = w.shape[-1] // 2
    return jnp.concatenate([w[..., half:], w[..., :half]], axis=-1)


def _prep_weights(w_in, w_uq, w_ukv):
    d = w_in.shape[0]
    splits = (Q_LORA, KV_LORA, QK_ROPE, CONV_WIDTH, CONV_WIDTH, CONV_WIDTH, D_MODEL, D_MODEL)
    offs = [0]
    for s in splits:
        offs.append(offs[-1] + s)
    part = lambda n: w_in[:, offs[n]:offs[n + 1]]
    z = lambda n: jnp.zeros((d, n), w_in.dtype)
    w_kpe = part(2)
    kpe_a = jnp.concatenate([z(QK_NOPE), w_kpe, z(HEAD_PAD - QK_HEAD)], axis=1)
    kpe_b = jnp.concatenate([z(QK_NOPE), _swap_halves(w_kpe), z(HEAD_PAD - QK_HEAD)], axis=1)
    w_lat = jnp.concatenate([part(0), part(1), kpe_a, kpe_b], axis=1).astype(BF16)
    w_conv = w_in[:, offs[3]:offs[6]].astype(BF16)
    w_gate = w_in[:, offs[6]:offs[8]].astype(BF16)

    wq = w_uq.reshape(Q_LORA, N_HEADS, QK_HEAD)
    zq = lambda n: jnp.zeros((Q_LORA, N_HEADS, n), w_uq.dtype)
    wq2 = jnp.concatenate([wq, zq(HEAD_PAD - QK_HEAD)], axis=-1)
    wq2s = jnp.concatenate([zq(QK_NOPE), _swap_halves(wq[..., QK_NOPE:]), zq(HEAD_PAD - QK_HEAD)],
                           axis=-1)
    wq2 = wq2.reshape(Q_LORA, N_HEADS * HEAD_PAD).astype(BF16)
    wq2s = wq2s.reshape(Q_LORA, N_HEADS * HEAD_PAD).astype(BF16)

    wkv = w_ukv.reshape(KV_LORA, N_HEADS, QK_NOPE + V_HEAD)
    wk2 = jnp.concatenate([wkv[..., :QK_NOPE],
                           jnp.zeros((KV_LORA, N_HEADS, HEAD_PAD - QK_NOPE), w_ukv.dtype)], axis=-1)
    wk2 = wk2.reshape(KV_LORA, N_HEADS * HEAD_PAD).astype(BF16)
    wv = jnp.concatenate([wkv[..., QK_NOPE:],
                          jnp.zeros((KV_LORA, N_HEADS, HEAD_PAD - V_HEAD), w_ukv.dtype)], axis=-1)
    wv = wv.reshape(KV_LORA, N_HEADS * HEAD_PAD).astype(BF16)
    return w_lat, w_conv, w_gate, wq2, wq2s, wk2, wv


def _rope_freqs():
    inv_freq = 1.0 / (ROPE_THETA ** (jnp.arange(0, QK_ROPE, 2, dtype=F32) / QK_ROPE))
    return inv_freq.reshape(QK_ROPE // 2, 1)


def _route(top_idx, rank, counts, n_tokens):
    blocks = (counts + MOE_BLOCK - 1) // MOE_BLOCK
    first_block = jnp.cumsum(blocks) - blocks
    first_row = first_block * MOE_BLOCK
    start_of = jnp.zeros_like(rank)
    for e in range(N_EXPERTS):
        start_of = jnp.where(top_idx == e, first_row[e], start_of)
    dest = start_of + rank
    n_rows = n_tokens * TOP_K + N_EXPERTS * MOE_BLOCK
    table = jnp.stack([counts, first_block, blocks])
    return dest.astype(jnp.int32), table.astype(jnp.int32), n_rows


SC_CORES = 2
SC_SUBCORES = 16
SC_WORKERS = SC_CORES * SC_SUBCORES
SC_CHUNK = 64
SC_GATHER_RING = 3

def _sc_mesh():
    return plsc.VectorSubcoreMesh(core_axis_name="c", subcore_axis_name="s")


def _sc_worker():
    return lax.axis_index("s") * SC_CORES + lax.axis_index("c")


def _dispatch(h2, dest, n_rows):
    t, d = h2.shape
    per_w = t // SC_WORKERS
    n_chunks = per_w // SC_CHUNK
    assert per_w % (2 * SC_CHUNK) == 0
    idx = dest.reshape(TOP_K, SC_WORKERS, n_chunks, SC_CHUNK).transpose(1, 0, 2, 3)
    idx = idx.reshape(SC_WORKERS, TOP_K * n_chunks, SC_CHUNK)

    @functools.partial(
        pl.kernel, mesh=_sc_mesh(),
        out_type=jax.ShapeDtypeStruct((n_rows, d), h2.dtype),
        scratch_types=[pltpu.VMEM((TOP_K * n_chunks, SC_CHUNK), jnp.int32),
                       pltpu.VMEM((2, SC_CHUNK, d), h2.dtype),
                       pltpu.SemaphoreType.DMA((2,)),
                       pltpu.SemaphoreType.DMA((2,))],
        name="moe_dispatch")
    def run(h2_hbm, idx_hbm, xs_hbm, idx_v, rows_v, rsem, ssem):
        w = _sc_worker()
        pltpu.sync_copy(idx_hbm.at[w], idx_v)

        def read(g, b):
            src = h2_hbm.at[pl.ds(w * per_w + g * SC_CHUNK, SC_CHUNK)]
            return pltpu.make_async_copy(src, rows_v.at[b], rsem.at[b])

        def scatter(g, kk, b):
            dst = xs_hbm.at[idx_v.at[kk * n_chunks + g]]
            return pltpu.make_async_copy(rows_v.at[b], dst, ssem.at[b])

        read(0, 0).start()

        @pl.loop(0, n_chunks, step=2)
        def _(g0):
            for b in range(2):
                g = g0 + b
                read(g, b).wait()

                @pl.when(g + 1 < n_chunks)
                def _():
                    read(g + 1, 1 - b).start()

                for kk in range(TOP_K):
                    scatter(g, kk, b).start()
                for kk in range(TOP_K):
                    scatter(g, kk, b).wait()

    return run(h2, idx)


def _undispatch(ys, dest):
    t = dest.shape[1]
    d = ys.shape[1]
    n_out = t * TOP_K
    per_w = n_out // SC_WORKERS
    n_chunks = per_w // SC_CHUNK
    idx = dest.reshape(SC_WORKERS, n_chunks, SC_CHUNK)
    ring = SC_GATHER_RING

    @functools.partial(
        pl.kernel, mesh=_sc_mesh(),
        out_type=jax.ShapeDtypeStruct((n_out, d), ys.dtype),
        scratch_types=[pltpu.VMEM((n_chunks, SC_CHUNK), jnp.int32),
                       pltpu.VMEM((ring, SC_CHUNK, d), ys.dtype),
                       pltpu.SemaphoreType.DMA((ring,)),
                       pltpu.SemaphoreType.DMA((ring,))],
        name="moe_undispatch")
    def run(ys_hbm, idx_hbm, out_hbm, idx_v, rows_v, gsem, wsem):
        w = _sc_worker()
        pltpu.sync_copy(idx_hbm.at[w], idx_v)

        def gather(g):
            b = g % ring
            return pltpu.make_async_copy(ys_hbm.at[idx_v.at[g]], rows_v.at[b], gsem.at[b])

        def write(g):
            b = g % ring
            dst = out_hbm.at[pl.ds(w * per_w + g * SC_CHUNK, SC_CHUNK)]
            return pltpu.make_async_copy(rows_v.at[b], dst, wsem.at[b])

        for g in range(min(ring - 1, n_chunks)):
            gather(g).start()
        for g in range(n_chunks):
            gather(g).wait()
            ahead = g + ring - 1
            if ahead < n_chunks:
                if g >= 1:
                    write(g - 1).wait()
                gather(ahead).start()
            write(g).start()
        for g in range(max(n_chunks - ring, 0), n_chunks):
            write(g).wait()

    return run(ys, idx).reshape(TOP_K, t, d)


def kernel(x, c, positions, w_ada, b_ada, norm_mix_g, w_in, q_norm_g, w_uq, kv_norm_g, w_ukv,
           w_up_attn, conv_w, w_up_conv, w_o, norm_ffn_g, router_w, router_b, w_gu, b_gu,
           w_down, b_down, norm_final_g):
    batch, seq, d = x.shape
    t = batch * seq
    depth = w_ada.shape[0]
    x2 = x.reshape(t, d)
    pos = positions.astype(F32).reshape(t // ROW_TILE, 1, ROW_TILE)
    freqs = _rope_freqs()
    c_pad = jnp.zeros((8, d), F32).at[:batch].set(c)

    for l in range(depth):
        ada = _ada(c_pad, w_ada[l], b_ada[l].reshape(1, -1))
        mod = ada[:batch].reshape(batch, 6, d)
        mod = jnp.concatenate([mod, jnp.zeros((batch, 2, d), F32)], axis=1)

        w_lat, w_conv, w_gate, wq2, wq2s, wk2, wv = _prep_weights(w_in[l], w_uq[l], w_ukv[l])
        q, k, v, sga, gc = _pre(x2, mod, norm_mix_g[l].reshape(1, d), w_lat, w_conv, w_gate,
                                q_norm_g[l].reshape(1, -1), wq2, wq2s,
                                kv_norm_g[l].reshape(1, -1), wk2, wv, pos, freqs, conv_w[l],
                                w_up_conv[l].astype(BF16), seq)
        attn = _attention(q, k, v, batch, seq)

        rw_pad = jnp.concatenate([router_w[l], jnp.zeros((d, LANES - N_EXPERTS), F32)], axis=1)
        rb_pad = jnp.concatenate([router_b[l], jnp.full((LANES - N_EXPERTS,), NEG_BIG, F32)])
        rw_hi = rw_pad.astype(BF16)
        rw_lo = (rw_pad - rw_hi.astype(F32)).astype(BF16)
        x1, h2, idx_pad, gate_pad, rank_pad, counts = _post(
            attn, sga, gc, x2, mod, w_up_attn[l].astype(BF16), w_o[l].astype(BF16),
            norm_ffn_g[l].reshape(1, d), rw_hi, rw_lo, rb_pad.reshape(1, LANES), seq)

        dest, group_table, n_rows = _route(
            idx_pad[:TOP_K], rank_pad[:TOP_K], counts[0, :N_EXPERTS], t)
        xs = _dispatch(h2, dest, n_rows)
        ys = _moe(group_table, xs, w_gu[l], b_gu[l].reshape(N_EXPERTS, 1, -1),
                  w_down[l], b_down[l].reshape(N_EXPERTS, 1, -1))
        y_kt = _undispatch(ys, dest)
        x2 = _final(x1, y_kt, gate_pad, mod, norm_final_g.reshape(1, d), seq, l == depth - 1)

    return x2.reshape(batch, seq, d)
```

```python
import functools
import math

import jax
import jax.numpy as jnp
from jax import lax
from jax.experimental import pallas as pl
from jax.experimental.pallas import tpu as pltpu
from jax.experimental.pallas import tpu_sc as plsc

D_MODEL = 1024
CHUNK = 64
N_HEADS = 8
Q_LORA = 256
KV_LORA = 128
QK_NOPE = 64
QK_ROPE = 32
V_HEAD = 64
QK_HEAD = QK_NOPE + QK_ROPE
ROPE_THETA = 10000.0
CONV_WIDTH = 512
CONV_K = 3
N_EXPERTS = 32
TOP_K = 4
D_EXPERT = 1024
SWIGLU_LIMIT = 7.0
SWIGLU_ALPHA = 1.702
MOE_BLOCK = 256
RMS_EPS = 1e-6

LANES = 128
HEAD_PAD = 128
NEG_BIG = -1e30
VMEM_LIMIT = 56 * 1024 * 1024

F32 = jnp.float32
BF16 = jnp.bfloat16

Q_PRESCALE = (QK_HEAD ** -0.5) * math.log2(math.e)

ROW_TILE = 1024
MOE_STEP_BLOCKS = 4
POST_TILE = 1024
POST_SUB = 1024
ATT_BLOCK = 512
ATT_WIDE = 2


def _rms(x, g):
    ms = jnp.mean(x * x, axis=-1, keepdims=True)
    return x * lax.rsqrt(ms + RMS_EPS) * g


def _dot(a, b):
    return jnp.dot(a, b, preferred_element_type=F32)


PACKED = D_MODEL // 2


def _pack_row(x):
    return pltpu.pack_elementwise([x[:, :PACKED], x[:, PACKED:]], packed_dtype=BF16)


def _unpack_row(w):
    half = lambda i: pltpu.unpack_elementwise(w, index=i, packed_dtype=BF16, unpacked_dtype=F32)
    return jnp.concatenate([half(0), half(1)], axis=-1)


def _ada_kernel(c_ref, w_ref, b_ref, o_ref):
    c = c_ref[...]
    ca = (c * jax.nn.sigmoid(c)).astype(BF16)
    o_ref[...] = _dot(ca, w_ref[...].astype(BF16)) + b_ref[...]


def _ada(c_pad, w_ada, b_ada):
    n = w_ada.shape[1]
    tn = 1024
    return pl.pallas_call(
        _ada_kernel,
        out_shape=jax.ShapeDtypeStruct((c_pad.shape[0], n), F32),
        grid=(n // tn,),
        in_specs=[
            pl.BlockSpec(c_pad.shape, lambda j: (0, 0)),
            pl.BlockSpec((D_MODEL, tn), lambda j: (0, j)),
            pl.BlockSpec((1, tn), lambda j: (0, j)),
        ],
        out_specs=pl.BlockSpec((c_pad.shape[0], tn), lambda j: (0, j)),
        compiler_params=pltpu.CompilerParams(
            dimension_semantics=("arbitrary",), vmem_limit_bytes=VMEM_LIMIT),
        name="ada",
    )(c_pad, w_ada, b_ada)


_C_QLAT = 0
_C_KVLAT = _C_QLAT + Q_LORA
_C_KPE_A = _C_KVLAT + KV_LORA
_C_KPE_B = _C_KPE_A + HEAD_PAD
_C_U = _C_KPE_B + HEAD_PAD


def _pre_kernel(tiles_per_seq, x_ref, mod_ref, g_ref, wlat_ref, wconv_ref, wgate_ref, qg_ref,
                wq_ref, wqs_ref, kvg_ref, wk_ref, wv_ref, pos_ref, freq_ref, cw_ref, wuc_ref,
                q_ref, k_ref, v_ref, sga_ref, gc_ref, carry_ref):
    i = pl.program_id(0)
    tm = x_ref.shape[0]
    mod = mod_ref[...]
    h = _rms(x_ref[...], g_ref[...]) * (1.0 + mod[1:2]) + mod[0:1]
    hb = h.astype(BF16)

    ang = freq_ref[...] * pos_ref[...]
    cos_t, sin_t = jnp.cos(ang), jnp.sin(ang)
    ones_t = jnp.ones((QK_NOPE, tm), F32)
    zeros_t = jnp.zeros((QK_NOPE, tm), F32)
    pad_t = jnp.zeros((HEAD_PAD - QK_HEAD, tm), F32)
    cosf = jnp.concatenate([ones_t, cos_t, cos_t, pad_t], axis=0).T
    sinf = jnp.concatenate([zeros_t, -sin_t, sin_t, pad_t], axis=0).T
    cos8 = jnp.concatenate([cosf] * N_HEADS, axis=-1)
    sin8 = jnp.concatenate([sinf] * N_HEADS, axis=-1)

    small = _dot(hb, wlat_ref[...])
    q_lat = small[:, _C_QLAT:_C_KVLAT]
    kv_lat = small[:, _C_KVLAT:_C_KPE_A]
    kpe = small[:, _C_KPE_A:_C_KPE_B] * cosf + small[:, _C_KPE_B:_C_U] * sinf
    qn = _rms(q_lat, qg_ref[...]).astype(BF16)
    q = _dot(qn, wq_ref[...]) * cos8 + _dot(qn, wqs_ref[...]) * sin8
    q_ref[...] = (q * Q_PRESCALE).astype(BF16)
    kvn = _rms(kv_lat, kvg_ref[...]).astype(BF16)
    k = _dot(kvn, wk_ref[...]) + jnp.concatenate([kpe] * N_HEADS, axis=-1)
    k_ref[...] = k.astype(BF16)
    lane = lax.broadcasted_iota(jnp.int32, (tm, N_HEADS * HEAD_PAD), 1)
    ones_col = jnp.where(lane % HEAD_PAD == V_HEAD, 1.0, 0.0)
    v_ref[...] = (_dot(kvn, wv_ref[...]) + ones_col).astype(BF16)

    ucb = _dot(hb, wconv_ref[...])
    cu = ucb[:, 0:CONV_WIDTH] * ucb[:, CONV_WIDTH:2 * CONV_WIDTH]
    b_gate = ucb[:, 2 * CONV_WIDTH:3 * CONV_WIDTH]

    @pl.when(i % tiles_per_seq == 0)
    def _():
        carry_ref[...] = jnp.zeros_like(carry_ref)

    prev = carry_ref[...]
    row = lax.broadcasted_iota(jnp.int32, cu.shape, 0)
    cu1 = jnp.where(row == 0, prev[7:8], pltpu.roll(cu, 1, 0))
    cu2 = jnp.where(row == 0, prev[6:7], jnp.where(row == 1, prev[7:8], pltpu.roll(cu, 2, 0)))
    cw = cw_ref[...]
    z = cw[2:3] * cu + cw[1:2] * cu1 + cw[0:1] * cu2
    carry_ref[...] = cu[tm - 8:tm]
    c_branch = _dot((b_gate * z).astype(BF16), wuc_ref[...])

    gates = _dot(hb, wgate_ref[...])
    sga_ref[...] = jax.nn.sigmoid(gates[:, 0:D_MODEL]).astype(BF16)
    gc_ref[...] = (jax.nn.sigmoid(gates[:, D_MODEL:]) * c_branch).astype(BF16)


def _pre(x2, mod, norm_g, w_lat, w_conv, w_gate, q_norm_g, wq2, wq2s, kv_norm_g, wk2, wv, pos,
         freqs, conv_w, w_up_conv, seq):
    t = x2.shape[0]
    tm = ROW_TILE
    tiles_per_seq = seq // tm
    full = lambda a: pl.BlockSpec(a.shape, lambda i: (0,) * a.ndim)
    rows = lambda w: pl.BlockSpec((tm, w), lambda i: (i, 0))
    outs = [jax.ShapeDtypeStruct((t, N_HEADS * HEAD_PAD), BF16),
            jax.ShapeDtypeStruct((t, N_HEADS * HEAD_PAD), BF16),
            jax.ShapeDtypeStruct((t, N_HEADS * HEAD_PAD), BF16),
            jax.ShapeDtypeStruct((t, D_MODEL), BF16),
            jax.ShapeDtypeStruct((t, D_MODEL), BF16)]
    return pl.pallas_call(
        functools.partial(_pre_kernel, tiles_per_seq),
        out_shape=outs,
        grid=(t // tm,),
        in_specs=[
            rows(D_MODEL),
            pl.BlockSpec((None, 8, D_MODEL), lambda i: (i // tiles_per_seq, 0, 0)),
            full(norm_g), full(w_lat), full(w_conv), full(w_gate), full(q_norm_g), full(wq2),
            full(wq2s), full(kv_norm_g), full(wk2), full(wv),
            pl.BlockSpec((None, 1, tm), lambda i: (i, 0, 0)), full(freqs), full(conv_w),
            full(w_up_conv),
        ],
        out_specs=[rows(N_HEADS * HEAD_PAD), rows(N_HEADS * HEAD_PAD), rows(N_HEADS * HEAD_PAD),
                   rows(D_MODEL), rows(D_MODEL)],
        scratch_shapes=[pltpu.VMEM((8, CONV_WIDTH), F32)],
        compiler_params=pltpu.CompilerParams(
            dimension_semantics=("arbitrary",), vmem_limit_bytes=VMEM_LIMIT),
        name="pre_mixer",
    )(x2, mod, norm_g, w_lat, w_conv, w_gate, q_norm_g, wq2, wq2s, kv_norm_g, wk2, wv, pos,
      freqs, conv_w, w_up_conv)


def _attn_kernel(q_ref, k_ref, v_ref, o_ref, m_ref, acc_ref):
    i = pl.program_id(1)
    tq = q_ref.shape[0]

    m_ref[...] = jnp.full_like(m_ref, NEG_BIG)
    acc_ref[...] = jnp.zeros_like(acc_ref)

    def step(k0, tk, masked):
        if masked:
            rq = (lax.broadcasted_iota(jnp.int32, (tq, tk), 0) + (tk - tq)) // CHUNK
            ck = lax.broadcasted_iota(jnp.int32, (tq, tk), 1) // CHUNK
            allowed = ck <= rq
        for hd in range(N_HEADS):
            hs = slice(hd * HEAD_PAD, (hd + 1) * HEAD_PAD)
            s = lax.dot_general(q_ref[:, hs], k_ref[pl.ds(k0, tk), hs],
                                (((1,), (1,)), ((), ())), preferred_element_type=F32)
            if masked:
                s = jnp.where(allowed, s, NEG_BIG)
            m_old = m_ref[hd]
            s_max = s[:, 0:LANES]
            for c in range(1, tk // LANES):
                s_max = jnp.maximum(s_max, s[:, c * LANES:(c + 1) * LANES])
            m_new = jnp.maximum(m_old, jnp.max(s_max, axis=-1, keepdims=True))
            alpha = jnp.exp2(m_old - m_new)
            p = jnp.concatenate(
                [jnp.exp2(s[:, c * LANES:(c + 1) * LANES] - m_new).astype(BF16)
                 for c in range(tk // LANES)], axis=-1)
            acc_ref[hd] = alpha * acc_ref[hd] + _dot(p, v_ref[pl.ds(k0, tk), hs])
            m_ref[hd] = m_new

    wide = ATT_WIDE * tq

    def body(j, carry):
        step(pl.multiple_of(j * wide, wide), wide, False)
        return carry

    lax.fori_loop(0, i // ATT_WIDE, body, 0)

    for r in range(ATT_WIDE):
        @pl.when(i % ATT_WIDE == r)
        def _():
            step(pl.multiple_of((i - r) * tq, tq), (r + 1) * tq, True)

    for hp in range(N_HEADS // 2):
        pair = []
        for hd in (2 * hp, 2 * hp + 1):
            acc = acc_ref[hd]
            pair.append(acc[:, 0:V_HEAD] / acc[:, V_HEAD:V_HEAD + 1])
        o_ref[:, hp * LANES:(hp + 1) * LANES] = jnp.concatenate(pair, axis=-1).astype(BF16)


def _attention(q, k, v, batch, seq):
    tq = ATT_BLOCK
    nq = seq // tq
    return pl.pallas_call(
        _attn_kernel,
        out_shape=jax.ShapeDtypeStruct((batch * seq, N_HEADS * V_HEAD), BF16),
        grid=(batch, nq),
        in_specs=[
            pl.BlockSpec((tq, N_HEADS * HEAD_PAD), lambda b, i: (b * nq + i, 0)),
            pl.BlockSpec((seq, N_HEADS * HEAD_PAD), lambda b, i: (b, 0)),
            pl.BlockSpec((seq, N_HEADS * HEAD_PAD), lambda b, i: (b, 0)),
        ],
        out_specs=pl.BlockSpec((tq, N_HEADS * V_HEAD), lambda b, i: (b * nq + i, 0)),
        scratch_shapes=[pltpu.VMEM((N_HEADS, tq, LANES), F32),
                        pltpu.VMEM((N_HEADS, tq, LANES), F32)],
        compiler_params=pltpu.CompilerParams(
            dimension_semantics=("arbitrary", "arbitrary"), vmem_limit_bytes=VMEM_LIMIT),
        name="attention",
    )(q, k, v)


def _post_kernel(attn_ref, sga_ref, gc_ref, x_ref, mod_ref, wua_ref, wo_ref, g_ref, rwh_ref,
                 rwl_ref, rb_ref, x1_ref, h2_ref, idx_ref, gate_ref, rank_ref, cnt_out_ref, cnt_ref):
    @pl.when(pl.program_id(0) == 0)
    def _():
        cnt_ref[...] = jnp.zeros_like(cnt_ref)

    counts = cnt_ref[...]
    for r0 in range(0, x_ref.shape[0], POST_SUB):
        counts = _post_rows(slice(r0, r0 + POST_SUB), counts, attn_ref, sga_ref, gc_ref, x_ref,
                            mod_ref, wua_ref, wo_ref, g_ref, rwh_ref, rwl_ref, rb_ref, x1_ref,
                            h2_ref, idx_ref, gate_ref, rank_ref)
    cnt_ref[...] = counts
    cnt_out_ref[...] = counts.astype(jnp.int32)


def _post_rows(rs, counts, attn_ref, sga_ref, gc_ref, x_ref, mod_ref, wua_ref, wo_ref, g_ref,
               rwh_ref, rwl_ref, rb_ref, x1_ref, h2_ref, idx_ref, gate_ref, rank_ref):
    mod = mod_ref[...]
    a_branch = _dot(attn_ref[rs, :], wua_ref[...])
    merged = sga_ref[rs, :].astype(F32) * a_branch + gc_ref[rs, :].astype(F32)
    mix = _dot(merged.astype(BF16), wo_ref[...])
    x1 = x_ref[rs, :] + mod[2:3] * mix
    x1_ref[rs, :] = x1
    h2 = _rms(x1, g_ref[...]) * (1.0 + mod[4:5]) + mod[3:4]
    h2_ref[rs, :] = _pack_row(h2)

    h_hi = h2.astype(BF16)
    h_lo = (h2 - h_hi.astype(F32)).astype(BF16)
    logits = (_dot(h_hi, rwh_ref[...]) + _dot(h_lo, rwh_ref[...]) + _dot(h_hi, rwl_ref[...])
              + rb_ref[...])
    lane = lax.broadcasted_iota(jnp.int32, logits.shape, 1)
    work = logits
    vals, idxs = [], []
    for _ in range(TOP_K):
        mk = jnp.max(work, axis=-1, keepdims=True)
        ik = jnp.min(jnp.where(work == mk, lane, LANES), axis=-1, keepdims=True)
        vals.append(mk)
        idxs.append(ik)
        work = jnp.where(lane == ik, -jnp.inf, work)
    es = [jnp.exp(vk - vals[0]) for vk in vals]
    denom = es[0] + es[1] + es[2] + es[3]
    tm = logits.shape[0]
    chosen = jnp.zeros(logits.shape, F32)
    for kk in range(TOP_K):
        chosen = chosen + jnp.where(lane == idxs[kk], 1.0, 0.0)
    r_i = lax.broadcasted_iota(jnp.int32, (tm, tm), 0)
    c_i = lax.broadcasted_iota(jnp.int32, (tm, tm), 1)
    earlier = jnp.where(c_i < r_i, 1.0, 0.0).astype(BF16)
    before = _dot(earlier, chosen.astype(BF16)) + counts[0:1]

    idx_out = jnp.zeros(logits.shape, F32)
    gate_out = jnp.zeros(logits.shape, F32)
    rank_out = jnp.zeros(logits.shape, F32)
    for kk in range(TOP_K):
        rank_k = jnp.sum(jnp.where(lane == idxs[kk], before, 0.0), axis=-1, keepdims=True)
        idx_out = jnp.where(lane == kk, idxs[kk].astype(F32), idx_out)
        gate_out = jnp.where(lane == kk, es[kk] / denom, gate_out)
        rank_out = jnp.where(lane == kk, rank_k, rank_out)
    gate_ref[rs, :] = gate_out
    idx_ref[:, rs] = idx_out.T[0:8].astype(jnp.int32)
    rank_ref[:, rs] = rank_out.T[0:8].astype(jnp.int32)
    return counts + jnp.sum(chosen, axis=0, keepdims=True)


def _post(attn, sga, gc, x2, mod, wua, wo, norm_g, rw_hi, rw_lo, rb_pad, seq):
    t = x2.shape[0]
    tm = POST_TILE
    tiles_per_seq = seq // tm
    full = lambda a: pl.BlockSpec(a.shape, lambda i: (0,) * a.ndim)
    rows = lambda w: pl.BlockSpec((tm, w), lambda i: (i, 0))
    outs = [jax.ShapeDtypeStruct((t, D_MODEL), F32),
            jax.ShapeDtypeStruct((t, PACKED), jnp.uint32),
            jax.ShapeDtypeStruct((8, t), jnp.int32),
            jax.ShapeDtypeStruct((t, LANES), F32),
            jax.ShapeDtypeStruct((8, t), jnp.int32),
            jax.ShapeDtypeStruct((8, LANES), jnp.int32)]
    slots = pl.BlockSpec((8, tm), lambda i: (0, i))
    return pl.pallas_call(
        _post_kernel,
        out_shape=outs,
        grid=(t // tm,),
        in_specs=[
            rows(N_HEADS * V_HEAD), rows(D_MODEL), rows(D_MODEL), rows(D_MODEL),
            pl.BlockSpec((None, 8, D_MODEL), lambda i: (i // tiles_per_seq, 0, 0)),
            full(wua), full(wo), full(norm_g), full(rw_hi), full(rw_lo), full(rb_pad),
        ],
        out_specs=[rows(D_MODEL), rows(PACKED), slots, rows(LANES), slots,
                   pl.BlockSpec((8, LANES), lambda i: (0, 0))],
        scratch_shapes=[pltpu.VMEM((8, LANES), F32)],
        compiler_params=pltpu.CompilerParams(
            dimension_semantics=("arbitrary",), vmem_limit_bytes=VMEM_LIMIT),
        name="post_mixer",
    )(attn, sga, gc, x2, mod, wua, wo, norm_g, rw_hi, rw_lo, rb_pad)


_GM_COUNT, _GM_FIRST, _GM_BLOCKS = range(3)
_ST_EXPERT, _ST_SLOT = range(2)


def _moe_kernel(gm_ref, xs_ref, wgu_hbm, bgu_ref, wd_hbm, bd_ref, o_ref,
                wgu_f, wd_f, wgu_bf, wd_bf, sem, st_ref):
    weights = (wgu_hbm, wd_hbm, wgu_f, wd_f, wgu_bf, wd_bf, sem)
    rows = (xs_ref, bgu_ref, bd_ref, o_ref, wgu_bf, wd_bf)
    b0 = pl.program_id(0) * MOE_STEP_BLOCKS
    e, in_group, used = _moe_enter(b0, gm_ref, st_ref, *weights)
    together = jnp.logical_and(used, in_group + MOE_STEP_BLOCKS <= gm_ref[_GM_BLOCKS, e])

    @pl.when(together)
    def _():
        _moe_rows(slice(0, MOE_STEP_BLOCKS * MOE_BLOCK), e, in_group, True, gm_ref, *rows)

    @pl.when(jnp.logical_not(together))
    def _():
        _moe_rows(slice(0, MOE_BLOCK), e, in_group, used, gm_ref, *rows)
        for r in range(1, MOE_STEP_BLOCKS):
            e_r, in_group_r, used_r = _moe_enter(b0 + r, gm_ref, st_ref, *weights)
            _moe_rows(slice(r * MOE_BLOCK, (r + 1) * MOE_BLOCK), e_r, in_group_r, used_r, gm_ref,
                      *rows)


def _moe_enter(b, gm_ref, st_ref, wgu_hbm, wd_hbm, wgu_f, wd_f, wgu_bf, wd_bf, sem):
    def weight_copies(expert, sl):
        return (pltpu.make_async_copy(wgu_hbm.at[expert], wgu_f.at[sl], sem.at[0, sl]),
                pltpu.make_async_copy(wd_hbm.at[expert], wd_f.at[sl], sem.at[1, sl]))

    def next_group(e):
        return lax.while_loop(
            lambda k: jnp.logical_and(k < N_EXPERTS,
                                      gm_ref[_GM_BLOCKS, jnp.minimum(k, N_EXPERTS - 1)] == 0),
            lambda k: k + 1, e)

    @pl.when(b == 0)
    def _():
        e0 = next_group(0)
        st_ref[_ST_EXPERT] = e0
        st_ref[_ST_SLOT] = 1
        for cp in weight_copies(e0, 0):
            cp.start()

    e_prev = st_ref[_ST_EXPERT]
    past = b >= gm_ref[_GM_FIRST, e_prev] + gm_ref[_GM_BLOCKS, e_prev]
    e = jnp.minimum(jnp.where(past, next_group(e_prev + 1), e_prev), N_EXPERTS - 1)
    st_ref[_ST_EXPERT] = e
    in_group = b - gm_ref[_GM_FIRST, e]
    used = jnp.logical_and(in_group >= 0, in_group < gm_ref[_GM_BLOCKS, e])

    @pl.when(jnp.logical_and(used, in_group == 0))
    def _():
        slot = 1 - st_ref[_ST_SLOT]
        st_ref[_ST_SLOT] = slot
        nxt = next_group(e + 1)

        @pl.when(nxt < N_EXPERTS)
        def _():
            for cp in weight_copies(nxt, 1 - slot):
                cp.start(priority=1)

        for cp in weight_copies(e, slot):
            cp.wait()
        wgu_bf[...] = wgu_f[slot].astype(BF16)
        wd_bf[...] = wd_f[slot].astype(BF16)

    return e, in_group, used


def _moe_rows(rs, e, in_group, used, gm_ref, xs_ref, bgu_ref, bd_ref, o_ref, wgu_bf, wd_bf):
    n = rs.stop - rs.start

    def ffn():
        n_valid = gm_ref[_GM_COUNT, e] - in_group * MOE_BLOCK
        row = lax.broadcasted_iota(jnp.int32, (n, PACKED), 0)
        xs = _unpack_row(jnp.where(row < n_valid, xs_ref[rs, :], 0)).astype(BF16)
        gu = _dot(xs, wgu_bf[...]) + bgu_ref[e]
        gate = jnp.minimum(gu[:, :D_EXPERT], SWIGLU_LIMIT)
        up = jnp.clip(gu[:, D_EXPERT:], -SWIGLU_LIMIT, SWIGLU_LIMIT)
        act = (up + 1.0) * (gate * jax.nn.sigmoid(gate * SWIGLU_ALPHA))
        o_ref[rs, :] = _pack_row(_dot(act.astype(BF16), wd_bf[...]) + bd_ref[e])

    if used is True:
        ffn()
        return
    pl.when(used)(ffn)

    @pl.when(jnp.logical_not(used))
    def _():
        o_ref[rs, :] = jnp.zeros((n, PACKED), o_ref.dtype)


def _moe(group_table, xs, w_gu, b_gu, w_down, b_down):
    n_rows = xs.shape[0]
    step_rows = MOE_STEP_BLOCKS * MOE_BLOCK
    assert n_rows % step_rows == 0
    grid_spec = pltpu.PrefetchScalarGridSpec(
        num_scalar_prefetch=1,
        grid=(n_rows // step_rows,),
        in_specs=[
            pl.BlockSpec((step_rows, PACKED), lambda b, gm: (b, 0)),
            pl.BlockSpec(memory_space=pl.ANY),
            pl.BlockSpec(b_gu.shape, lambda b, gm: (0, 0, 0)),
            pl.BlockSpec(memory_space=pl.ANY),
            pl.BlockSpec(b_down.shape, lambda b, gm: (0, 0, 0)),
        ],
        out_specs=pl.BlockSpec((step_rows, PACKED), lambda b, gm: (b, 0)),
        scratch_shapes=[pltpu.VMEM((2, D_MODEL, 2 * D_EXPERT), F32),
                        pltpu.VMEM((2, D_EXPERT, D_MODEL), F32),
                        pltpu.VMEM((D_MODEL, 2 * D_EXPERT), BF16),
                        pltpu.VMEM((D_EXPERT, D_MODEL), BF16),
                        pltpu.SemaphoreType.DMA((2, 2)),
                        pltpu.SMEM((2,), jnp.int32)],
    )
    return pl.pallas_call(
        _moe_kernel,
        out_shape=jax.ShapeDtypeStruct((n_rows, PACKED), jnp.uint32),
        grid_spec=grid_spec,
        compiler_params=pltpu.CompilerParams(
            dimension_semantics=("arbitrary",), vmem_limit_bytes=VMEM_LIMIT),
        name="moe_experts",
    )(group_table, xs, w_gu, b_gu, w_down, b_down)


def _final_kernel(last_layer, x1_ref, y_ref, gate_ref, mod_ref, g_ref, o_ref):
    mod = mod_ref[...]
    gate = gate_ref[...]
    ffn = gate[:, 0:1] * _unpack_row(y_ref[0])
    for kk in range(1, TOP_K):
        ffn = ffn + gate[:, kk:kk + 1] * _unpack_row(y_ref[kk])
    x = x1_ref[...] + mod[5:6] * ffn
    o_ref[...] = _rms(x, g_ref[...]) if last_layer else x


def _final(x1, y_kt, gate, mod, norm_g, seq, last_layer):
    t = x1.shape[0]
    tm = ROW_TILE
    tiles_per_seq = seq // tm
    rows = lambda w: pl.BlockSpec((tm, w), lambda i: (i, 0))
    return pl.pallas_call(
        functools.partial(_final_kernel, last_layer),
        out_shape=jax.ShapeDtypeStruct((t, D_MODEL), F32),
        grid=(t // tm,),
        in_specs=[
            rows(D_MODEL), pl.BlockSpec((TOP_K, tm, PACKED), lambda i: (0, i, 0)), rows(LANES),
            pl.BlockSpec((None, 8, D_MODEL), lambda i: (i // tiles_per_seq, 0, 0)),
            pl.BlockSpec(norm_g.shape, lambda i: (0, 0)),
        ],
        out_specs=rows(D_MODEL),
        compiler_params=pltpu.CompilerParams(
            dimension_semantics=("arbitrary",), vmem_limit_bytes=VMEM_LIMIT),
        name="combine_final",
    )(x1, y_kt, gate, mod, norm_g)


def _swap_halves(w):
    half = w.shape[-1] // 2
    return jnp.concatenate([w[..., half:], w[..., :half]], axis=-1)


def _prep_weights(w_in, w_uq, w_ukv):
    d = w_in.shape[0]
    splits = (Q_LORA, KV_LORA, QK_ROPE, CONV_WIDTH, CONV_WIDTH, CONV_WIDTH, D_MODEL, D_MODEL)
    offs = [0]
    for s in splits:
        offs.append(offs[-1] + s)
    part = lambda n: w_in[:, offs[n]:offs[n + 1]]
    z = lambda n: jnp.zeros((d, n), w_in.dtype)
    w_kpe = part(2)
    kpe_a = jnp.concatenate([z(QK_NOPE), w_kpe, z(HEAD_PAD - QK_HEAD)], axis=1)
    kpe_b = jnp.concatenate([z(QK_NOPE), _swap_halves(w_kpe), z(HEAD_PAD - QK_HEAD)], axis=1)
    w_lat = jnp.concatenate([part(0), part(1), kpe_a, kpe_b], axis=1).astype(BF16)
    w_conv = w_in[:, offs[3]:offs[6]].astype(BF16)
    w_gate = w_in[:, offs[6]:offs[8]].astype(BF16)

    wq = w_uq.reshape(Q_LORA, N_HEADS, QK_HEAD)
    zq = lambda n: jnp.zeros((Q_LORA, N_HEADS, n), w_uq.dtype)
    wq2 = jnp.concatenate([wq, zq(HEAD_PAD - QK_HEAD)], axis=-1)
    wq2s = jnp.concatenate([zq(QK_NOPE), _swap_halves(wq[..., QK_NOPE:]), zq(HEAD_PAD - QK_HEAD)],
                           axis=-1)
    wq2 = wq2.reshape(Q_LORA, N_HEADS * HEAD_PAD).astype(BF16)
    wq2s = wq2s.reshape(Q_LORA, N_HEADS * HEAD_PAD).astype(BF16)

    wkv = w_ukv.reshape(KV_LORA, N_HEADS, QK_NOPE + V_HEAD)
    wk2 = jnp.concatenate([wkv[..., :QK_NOPE],
                           jnp.zeros((KV_LORA, N_HEADS, HEAD_PAD - QK_NOPE), w_ukv.dtype)], axis=-1)
    wk2 = wk2.reshape(KV_LORA, N_HEADS * HEAD_PAD).astype(BF16)
    wv = jnp.concatenate([wkv[..., QK_NOPE:],
                          jnp.zeros((KV_LORA, N_HEADS, HEAD_PAD - V_HEAD), w_ukv.dtype)], axis=-1)
    wv = wv.reshape(KV_LORA, N_HEADS * HEAD_PAD).astype(BF16)
    return w_lat, w_conv, w_gate, wq2, wq2s, wk2, wv


def _rope_freqs():
    inv_freq = 1.0 / (ROPE_THETA ** (jnp.arange(0, QK_ROPE, 2, dtype=F32) / QK_ROPE))
    return inv_freq.reshape(QK_ROPE // 2, 1)


def _route(top_idx, rank, counts, n_tokens):
    blocks = (counts + MOE_BLOCK - 1) // MOE_BLOCK
    first_block = jnp.cumsum(blocks) - blocks
    first_row = first_block * MOE_BLOCK
    start_of = jnp.zeros_like(rank)
    for e in range(N_EXPERTS):
        start_of = jnp.where(top_idx == e, first_row[e], start_of)
    dest = (start_of + rank)[:TOP_K]
    n_rows = n_tokens * TOP_K + N_EXPERTS * MOE_BLOCK
    table = jnp.stack([counts, first_block, blocks])
    return dest.astype(jnp.int32), table.astype(jnp.int32), n_rows


SC_CORES = 2
SC_SUBCORES = 16
SC_WORKERS = SC_CORES * SC_SUBCORES
SC_CHUNK = 64
SC_GATHER_RING = 3

def _sc_mesh():
    return plsc.VectorSubcoreMesh(core_axis_name="c", subcore_axis_name="s")


def _sc_worker():
    return lax.axis_index("s") * SC_CORES + lax.axis_index("c")


def _dispatch(h2, dest, n_rows):
    t, d = h2.shape
    per_w = t // SC_WORKERS
    n_chunks = per_w // SC_CHUNK
    assert per_w % (2 * SC_CHUNK) == 0
    idx = dest.reshape(TOP_K, SC_WORKERS, n_chunks, SC_CHUNK).transpose(1, 0, 2, 3)
    idx = idx.reshape(SC_WORKERS, TOP_K * n_chunks, SC_CHUNK)

    @functools.partial(
        pl.kernel, mesh=_sc_mesh(),
        out_type=jax.ShapeDtypeStruct((n_rows, d), h2.dtype),
        scratch_types=[pltpu.VMEM((TOP_K * n_chunks, SC_CHUNK), jnp.int32),
                       pltpu.VMEM((2, SC_CHUNK, d), h2.dtype),
                       pltpu.SemaphoreType.DMA((2,)),
                       pltpu.SemaphoreType.DMA((2,))],
        name="moe_dispatch")
    def run(h2_hbm, idx_hbm, xs_hbm, idx_v, rows_v, rsem, ssem):
        w = _sc_worker()
        pltpu.sync_copy(idx_hbm.at[w], idx_v)

        def read(g, b):
            src = h2_hbm.at[pl.ds(w * per_w + g * SC_CHUNK, SC_CHUNK)]
            return pltpu.make_async_copy(src, rows_v.at[b], rsem.at[b])

        def scatter(g, kk, b):
            dst = xs_hbm.at[idx_v.at[kk * n_chunks + g]]
            return pltpu.make_async_copy(rows_v.at[b], dst, ssem.at[b])

        read(0, 0).start()

        @pl.loop(0, n_chunks, step=2)
        def _(g0):
            for b in range(2):
                g = g0 + b
                read(g, b).wait()

                @pl.when(g + 1 < n_chunks)
                def _():
                    read(g + 1, 1 - b).start()

                for kk in range(TOP_K):
                    scatter(g, kk, b).start()
                for kk in range(TOP_K):
                    scatter(g, kk, b).wait()

    return run(h2, idx)


def _undispatch(ys, dest):
    t = dest.shape[1]
    d = ys.shape[1]
    n_out = t * TOP_K
    per_w = n_out // SC_WORKERS
    n_chunks = per_w // SC_CHUNK
    idx = dest.reshape(SC_WORKERS, n_chunks, SC_CHUNK)
    ring = SC_GATHER_RING

    @functools.partial(
        pl.kernel, mesh=_sc_mesh(),
        out_type=jax.ShapeDtypeStruct((n_out, d), ys.dtype),
        scratch_types=[pltpu.VMEM((n_chunks, SC_CHUNK), jnp.int32),
                       pltpu.VMEM((ring, SC_CHUNK, d), ys.dtype),
                       pltpu.SemaphoreType.DMA((ring,)),
                       pltpu.SemaphoreType.DMA((ring,))],
        name="moe_undispatch")
    def run(ys_hbm, idx_hbm, out_hbm, idx_v, rows_v, gsem, wsem):
        w = _sc_worker()
        pltpu.sync_copy(idx_hbm.at[w], idx_v)

        def gather(g):
            b = g % ring
            return pltpu.make_async_copy(ys_hbm.at[idx_v.at[g]], rows_v.at[b], gsem.at[b])

        def write(g):
            b = g % ring
            dst = out_hbm.at[pl.ds(w * per_w + g * SC_CHUNK, SC_CHUNK)]
            return pltpu.make_async_copy(rows_v.at[b], dst, wsem.at[b])

        for g in range(min(ring - 1, n_chunks)):
            gather(g).start()
        for g in range(n_chunks):
            gather(g).wait()
            ahead = g + ring - 1
            if ahead < n_chunks:
                if g >= 1:
                    write(g - 1).wait()
                gather(ahead).start()
            write(g).start()
        for g in range(max(n_chunks - ring, 0), n_chunks):
            write(g).wait()

    return run(ys, idx).reshape(TOP_K, t, d)


def kernel(x, c, positions, w_ada, b_ada, norm_mix_g, w_in, q_norm_g, w_uq, kv_norm_g, w_ukv,
           w_up_attn, conv_w, w_up_conv, w_o, norm_ffn_g, router_w, router_b, w_gu, b_gu,
           w_down, b_down, norm_final_g):
    batch, seq, d = x.shape
    t = batch * seq
    depth = w_ada.shape[0]
    x2 = x.reshape(t, d)
    pos = positions.astype(F32).reshape(t // ROW_TILE, 1, ROW_TILE)
    freqs = _rope_freqs()
    c_pad = jnp.zeros((8, d), F32).at[:batch].set(c)

    for l in range(depth):
        ada = _ada(c_pad, w_ada[l], b_ada[l].reshape(1, -1))
        mod = ada[:batch].reshape(batch, 6, d)
        mod = jnp.concatenate([mod, jnp.zeros((batch, 2, d), F32)], axis=1)

        w_lat, w_conv, w_gate, wq2, wq2s, wk2, wv = _prep_weights(w_in[l], w_uq[l], w_ukv[l])
        q, k, v, sga, gc = _pre(x2, mod, norm_mix_g[l].reshape(1, d), w_lat, w_conv, w_gate,
                                q_norm_g[l].reshape(1, -1), wq2, wq2s,
                                kv_norm_g[l].reshape(1, -1), wk2, wv, pos, freqs, conv_w[l],
                                w_up_conv[l].astype(BF16), seq)
        attn = _attention(q, k, v, batch, seq)

        rw_pad = jnp.concatenate([router_w[l], jnp.zeros((d, LANES - N_EXPERTS), F32)], axis=1)
        rb_pad = jnp.concatenate([router_b[l], jnp.full((LANES - N_EXPERTS,), NEG_BIG, F32)])
        rw_hi = rw_pad.astype(BF16)
        rw_lo = (rw_pad - rw_hi.astype(F32)).astype(BF16)
        x1, h2, idx_pad, gate_pad, rank_pad, counts = _post(
            attn, sga, gc, x2, mod, w_up_attn[l].astype(BF16), w_o[l].astype(BF16),
            norm_ffn_g[l].reshape(1, d), rw_hi, rw_lo, rb_pad.reshape(1, LANES), seq)

        dest, group_table, n_rows = _route(
            idx_pad, rank_pad, counts[0, :N_EXPERTS], t)
        xs = _dispatch(h2, dest, n_rows)
        ys = _moe(group_table, xs, w_gu[l], b_gu[l].reshape(N_EXPERTS, 1, -1),
                  w_down[l], b_down[l].reshape(N_EXPERTS, 1, -1))
        y_kt = _undispatch(ys, dest)
        x2 = _final(x1, y_kt, gate_pad, mod, norm_final_g.reshape(1, d), seq, l == depth - 1)

    return x2.reshape(batch, seq, d)
```

```python
import functools
import math

import jax
import jax.numpy as jnp
from jax import lax
from jax.experimental import pallas as pl
from jax.experimental.pallas import tpu as pltpu
from jax.experimental.pallas import tpu_sc as plsc

D_MODEL = 1024
CHUNK = 64
N_HEADS = 8
Q_LORA = 256
KV_LORA = 128
QK_NOPE = 64
QK_ROPE = 32
V_HEAD = 64
QK_HEAD = QK_NOPE + QK_ROPE
ROPE_THETA = 10000.0
CONV_WIDTH = 512
CONV_K = 3
N_EXPERTS = 32
TOP_K = 4
D_EXPERT = 1024
SWIGLU_LIMIT = 7.0
SWIGLU_ALPHA = 1.702
MOE_BLOCK = 256
RMS_EPS = 1e-6

LANES = 128
HEAD_PAD = 128
NEG_BIG = -1e30
VMEM_LIMIT = 56 * 1024 * 1024

F32 = jnp.float32
BF16 = jnp.bfloat16

Q_PRESCALE = (QK_HEAD ** -0.5) * math.log2(math.e)

ROW_TILE = 1024
MOE_STEP_BLOCKS = 4
POST_TILE = 1024
POST_SUB = 1024
ATT_BLOCK = 512
ATT_WIDE = 2


def _rms(x, g):
    ms = jnp.mean(x * x, axis=-1, keepdims=True)
    return x * lax.rsqrt(ms + RMS_EPS) * g


def _dot(a, b):
    return jnp.dot(a, b, preferred_element_type=F32)


PACKED = D_MODEL // 2


def _pack_row(x):
    return pltpu.pack_elementwise([x[:, :PACKED], x[:, PACKED:]], packed_dtype=BF16)


def _unpack_row(w):
    half = lambda i: pltpu.unpack_elementwise(w, index=i, packed_dtype=BF16, unpacked_dtype=F32)
    return jnp.concatenate([half(0), half(1)], axis=-1)


def _ada_kernel(c_ref, w_ref, b_ref, o_ref):
    c = c_ref[...]
    ca = (c * jax.nn.sigmoid(c)).astype(BF16)
    o_ref[...] = _dot(ca, w_ref[...].astype(BF16)) + b_ref[...]


def _ada(c_pad, w_ada, b_ada):
    n = w_ada.shape[1]
    tn = 1024
    return pl.pallas_call(
        _ada_kernel,
        out_shape=jax.ShapeDtypeStruct((c_pad.shape[0], n), F32),
        grid=(n // tn,),
        in_specs=[
            pl.BlockSpec(c_pad.shape, lambda j: (0, 0)),
            pl.BlockSpec((D_MODEL, tn), lambda j: (0, j)),
            pl.BlockSpec((1, tn), lambda j: (0, j)),
        ],
        out_specs=pl.BlockSpec((c_pad.shape[0], tn), lambda j: (0, j)),
        compiler_params=pltpu.CompilerParams(
            dimension_semantics=("arbitrary",), vmem_limit_bytes=VMEM_LIMIT),
        name="ada",
    )(c_pad, w_ada, b_ada)


_C_QLAT = 0
_C_KVLAT = _C_QLAT + Q_LORA
_C_KPE_A = _C_KVLAT + KV_LORA
_C_KPE_B = _C_KPE_A + HEAD_PAD
_C_U = _C_KPE_B + HEAD_PAD


def _pre_kernel(tiles_per_seq, x_ref, mod_ref, g_ref, wlat_ref, wconv_ref, wgate_ref, qg_ref,
                wq_ref, wqs_ref, kvg_ref, wk_ref, wv_ref, pos_ref, freq_ref, cw_ref, wuc_ref,
                q_ref, k_ref, v_ref, sga_ref, gc_ref, carry_ref):
    i = pl.program_id(0)
    tm = x_ref.shape[0]
    mod = mod_ref[...]
    h = _rms(x_ref[...], g_ref[...]) * (1.0 + mod[1:2]) + mod[0:1]
    hb = h.astype(BF16)

    ang = freq_ref[...] * pos_ref[...]
    cos_t, sin_t = jnp.cos(ang), jnp.sin(ang)
    ones_t = jnp.ones((QK_NOPE, tm), F32)
    zeros_t = jnp.zeros((QK_NOPE, tm), F32)
    pad_t = jnp.zeros((HEAD_PAD - QK_HEAD, tm), F32)
    cosf = jnp.concatenate([ones_t, cos_t, cos_t, pad_t], axis=0).T
    sinf = jnp.concatenate([zeros_t, -sin_t, sin_t, pad_t], axis=0).T
    cos8 = jnp.concatenate([cosf] * N_HEADS, axis=-1)
    sin8 = jnp.concatenate([sinf] * N_HEADS, axis=-1)

    small = _dot(hb, wlat_ref[...])
    q_lat = small[:, _C_QLAT:_C_KVLAT]
    kv_lat = small[:, _C_KVLAT:_C_KPE_A]
    kpe = small[:, _C_KPE_A:_C_KPE_B] * cosf + small[:, _C_KPE_B:_C_U] * sinf
    qn = _rms(q_lat, qg_ref[...]).astype(BF16)
    q = _dot(qn, wq_ref[...]) * cos8 + _dot(qn, wqs_ref[...]) * sin8
    q_ref[...] = (q * Q_PRESCALE).astype(BF16)
    kvn = _rms(kv_lat, kvg_ref[...]).astype(BF16)
    k = _dot(kvn, wk_ref[...]) + jnp.concatenate([kpe] * N_HEADS, axis=-1)
    k_ref[...] = k.astype(BF16)
    lane = lax.broadcasted_iota(jnp.int32, (tm, N_HEADS * HEAD_PAD), 1)
    ones_col = jnp.where(lane % HEAD_PAD == V_HEAD, 1.0, 0.0)
    v_ref[...] = (_dot(kvn, wv_ref[...]) + ones_col).astype(BF16)

    ucb = _dot(hb, wconv_ref[...])
    cu = ucb[:, 0:CONV_WIDTH] * ucb[:, CONV_WIDTH:2 * CONV_WIDTH]
    b_gate = ucb[:, 2 * CONV_WIDTH:3 * CONV_WIDTH]

    @pl.when(i % tiles_per_seq == 0)
    def _():
        carry_ref[...] = jnp.zeros_like(carry_ref)

    prev = carry_ref[...]
    row = lax.broadcasted_iota(jnp.int32, cu.shape, 0)
    cu1 = jnp.where(row == 0, prev[7:8], pltpu.roll(cu, 1, 0))
    cu2 = jnp.where(row == 0, prev[6:7], jnp.where(row == 1, prev[7:8], pltpu.roll(cu, 2, 0)))
    cw = cw_ref[...]
    z = cw[2:3] * cu + cw[1:2] * cu1 + cw[0:1] * cu2
    carry_ref[...] = cu[tm - 8:tm]
    c_branch = _dot((b_gate * z).astype(BF16), wuc_ref[...])

    gates = _dot(hb, wgate_ref[...])
    sga_ref[...] = jax.nn.sigmoid(gates[:, 0:D_MODEL]).astype(BF16)
    gc_ref[...] = (jax.nn.sigmoid(gates[:, D_MODEL:]) * c_branch).astype(BF16)


def _pre(x2, mod, norm_g, w_lat, w_conv, w_gate, q_norm_g, wq2, wq2s, kv_norm_g, wk2, wv, pos,
         freqs, conv_w, w_up_conv, seq):
    t = x2.shape[0]
    tm = ROW_TILE
    tiles_per_seq = seq // tm
    full = lambda a: pl.BlockSpec(a.shape, lambda i: (0,) * a.ndim)
    rows = lambda w: pl.BlockSpec((tm, w), lambda i: (i, 0))
    outs = [jax.ShapeDtypeStruct((t, N_HEADS * HEAD_PAD), BF16),
            jax.ShapeDtypeStruct((t, N_HEADS * HEAD_PAD), BF16),
            jax.ShapeDtypeStruct((t, N_HEADS * HEAD_PAD), BF16),
            jax.ShapeDtypeStruct((t, D_MODEL), BF16),
            jax.ShapeDtypeStruct((t, D_MODEL), BF16)]
    return pl.pallas_call(
        functools.partial(_pre_kernel, tiles_per_seq),
        out_shape=outs,
        grid=(t // tm,),
        in_specs=[
            rows(D_MODEL),
            pl.BlockSpec((None, 8, D_MODEL), lambda i: (i // tiles_per_seq, 0, 0)),
            full(norm_g), full(w_lat), full(w_conv), full(w_gate), full(q_norm_g), full(wq2),
            full(wq2s), full(kv_norm_g), full(wk2), full(wv),
            pl.BlockSpec((None, 1, tm), lambda i: (i, 0, 0)), full(freqs), full(conv_w),
            full(w_up_conv),
        ],
        out_specs=[rows(N_HEADS * HEAD_PAD), rows(N_HEADS * HEAD_PAD), rows(N_HEADS * HEAD_PAD),
                   rows(D_MODEL), rows(D_MODEL)],
        scratch_shapes=[pltpu.VMEM((8, CONV_WIDTH), F32)],
        compiler_params=pltpu.CompilerParams(
            dimension_semantics=("arbitrary",), vmem_limit_bytes=VMEM_LIMIT),
        name="pre_mixer",
    )(x2, mod, norm_g, w_lat, w_conv, w_gate, q_norm_g, wq2, wq2s, kv_norm_g, wk2, wv, pos,
      freqs, conv_w, w_up_conv)


def _attn_kernel(q_ref, k_ref, v_ref, o_ref, m_ref, acc_ref):
    i = pl.program_id(1)
    tq = q_ref.shape[0]

    m_ref[...] = jnp.full_like(m_ref, NEG_BIG)
    acc_ref[...] = jnp.zeros_like(acc_ref)

    def step(k0, tk, masked):
        if masked:
            rq = (lax.broadcasted_iota(jnp.int32, (tq, tk), 0) + (tk - tq)) // CHUNK
            ck = lax.broadcasted_iota(jnp.int32, (tq, tk), 1) // CHUNK
            allowed = ck <= rq
        for hd in range(N_HEADS):
            hs = slice(hd * HEAD_PAD, (hd + 1) * HEAD_PAD)
            s = lax.dot_general(q_ref[:, hs], k_ref[pl.ds(k0, tk), hs],
                                (((1,), (1,)), ((), ())), preferred_element_type=F32)
            if masked:
                s = jnp.where(allowed, s, NEG_BIG)
            m_old = m_ref[hd]
            s_max = s[:, 0:LANES]
            for c in range(1, tk // LANES):
                s_max = jnp.maximum(s_max, s[:, c * LANES:(c + 1) * LANES])
            m_new = jnp.maximum(m_old, jnp.max(s_max, axis=-1, keepdims=True))
            alpha = jnp.exp2(m_old - m_new)
            p = jnp.concatenate(
                [jnp.exp2(s[:, c * LANES:(c + 1) * LANES] - m_new).astype(BF16)
                 for c in range(tk // LANES)], axis=-1)
            acc_ref[hd] = alpha * acc_ref[hd] + _dot(p, v_ref[pl.ds(k0, tk), hs])
            m_ref[hd] = m_new

    wide = ATT_WIDE * tq

    def body(j, carry):
        step(pl.multiple_of(j * wide, wide), wide, False)
        return carry

    lax.fori_loop(0, i // ATT_WIDE, body, 0)

    for r in range(ATT_WIDE):
        @pl.when(i % ATT_WIDE == r)
        def _():
            step(pl.multiple_of((i - r) * tq, tq), (r + 1) * tq, True)

    for hp in range(N_HEADS // 2):
        pair = []
        for hd in (2 * hp, 2 * hp + 1):
            acc = acc_ref[hd]
            pair.append(acc[:, 0:V_HEAD] / acc[:, V_HEAD:V_HEAD + 1])
        o_ref[:, hp * LANES:(hp + 1) * LANES] = jnp.concatenate(pair, axis=-1).astype(BF16)


def _attention(q, k, v, batch, seq):
    tq = ATT_BLOCK
    nq = seq // tq
    return pl.pallas_call(
        _attn_kernel,
        out_shape=jax.ShapeDtypeStruct((batch * seq, N_HEADS * V_HEAD), BF16),
        grid=(batch, nq),
        in_specs=[
            pl.BlockSpec((tq, N_HEADS * HEAD_PAD), lambda b, i: (b * nq + i, 0)),
            pl.BlockSpec((seq, N_HEADS * HEAD_PAD), lambda b, i: (b, 0)),
            pl.BlockSpec((seq, N_HEADS * HEAD_PAD), lambda b, i: (b, 0)),
        ],
        out_specs=pl.BlockSpec((tq, N_HEADS * V_HEAD), lambda b, i: (b * nq + i, 0)),
        scratch_shapes=[pltpu.VMEM((N_HEADS, tq, LANES), F32),
                        pltpu.VMEM((N_HEADS, tq, LANES), F32)],
        compiler_params=pltpu.CompilerParams(
            dimension_semantics=("arbitrary", "arbitrary"), vmem_limit_bytes=VMEM_LIMIT),
        name="attention",
    )(q, k, v)


def _post_kernel(attn_ref, sga_ref, gc_ref, x_ref, mod_ref, wua_ref, wo_ref, g_ref, rwh_ref,
                 rwl_ref, rb_ref, x1_ref, h2_ref, idx_ref, gate_ref, rank_ref, cnt_out_ref, cnt_ref):
    @pl.when(pl.program_id(0) == 0)
    def _():
        cnt_ref[...] = jnp.zeros_like(cnt_ref)

    counts = cnt_ref[...]
    for r0 in range(0, x_ref.shape[0], POST_SUB):
        counts = _post_rows(slice(r0, r0 + POST_SUB), counts, attn_ref, sga_ref, gc_ref, x_ref,
                            mod_ref, wua_ref, wo_ref, g_ref, rwh_ref, rwl_ref, rb_ref, x1_ref,
                            h2_ref, idx_ref, gate_ref, rank_ref)
    cnt_ref[...] = counts
    cnt_out_ref[...] = counts.astype(jnp.int32)


def _post_rows(rs, counts, attn_ref, sga_ref, gc_ref, x_ref, mod_ref, wua_ref, wo_ref, g_ref,
               rwh_ref, rwl_ref, rb_ref, x1_ref, h2_ref, idx_ref, gate_ref, rank_ref):
    mod = mod_ref[...]
    a_branch = _dot(attn_ref[rs, :], wua_ref[...])
    merged = sga_ref[rs, :].astype(F32) * a_branch + gc_ref[rs, :].astype(F32)
    mix = _dot(merged.astype(BF16), wo_ref[...])
    x1 = x_ref[rs, :] + mod[2:3] * mix
    x1_ref[rs, :] = x1
    h2 = _rms(x1, g_ref[...]) * (1.0 + mod[4:5]) + mod[3:4]
    h2_ref[rs, :] = _pack_row(h2)

    h_hi = h2.astype(BF16)
    h_lo = (h2 - h_hi.astype(F32)).astype(BF16)
    logits = (_dot(h_hi, rwh_ref[...]) + _dot(h_lo, rwh_ref[...]) + _dot(h_hi, rwl_ref[...])
              + rb_ref[...])
    lane = lax.broadcasted_iota(jnp.int32, logits.shape, 1)
    work = logits
    vals, idxs = [], []
    for _ in range(TOP_K):
        mk = jnp.max(work, axis=-1, keepdims=True)
        ik = jnp.min(jnp.where(work == mk, lane, LANES), axis=-1, keepdims=True)
        vals.append(mk)
        idxs.append(ik)
        work = jnp.where(lane == ik, -jnp.inf, work)
    es = [jnp.exp(vk - vals[0]) for vk in vals]
    denom = es[0] + es[1] + es[2] + es[3]
    tm = logits.shape[0]
    chosen = jnp.zeros(logits.shape, F32)
    for kk in range(TOP_K):
        chosen = chosen + jnp.where(lane == idxs[kk], 1.0, 0.0)
    r_i = lax.broadcasted_iota(jnp.int32, (tm, tm), 0)
    c_i = lax.broadcasted_iota(jnp.int32, (tm, tm), 1)
    earlier = jnp.where(c_i < r_i, 1.0, 0.0).astype(BF16)
    before = _dot(earlier, chosen.astype(BF16)) + counts[0:1]

    idx_out = jnp.zeros(logits.shape, F32)
    gate_out = jnp.zeros(logits.shape, F32)
    rank_out = jnp.zeros(logits.shape, F32)
    for kk in range(TOP_K):
        rank_k = jnp.sum(jnp.where(lane == idxs[kk], before, 0.0), axis=-1, keepdims=True)
        idx_out = jnp.where(lane == kk, idxs[kk].astype(F32), idx_out)
        gate_out = jnp.where(lane == kk, es[kk] / denom, gate_out)
        rank_out = jnp.where(lane == kk, rank_k, rank_out)
    gate_ref[rs, :] = gate_out
    idx_ref[:, rs] = idx_out.T[0:8].astype(jnp.int32)
    rank_ref[:, rs] = rank_out.T[0:8].astype(jnp.int32)
    return counts + jnp.sum(chosen, axis=0, keepdims=True)


def _post(attn, sga, gc, x2, mod, wua, wo, norm_g, rw_hi, rw_lo, rb_pad, seq):
    t = x2.shape[0]
    tm = POST_TILE
    tiles_per_seq = seq // tm
    full = lambda a: pl.BlockSpec(a.shape, lambda i: (0,) * a.ndim)
    rows = lambda w: pl.BlockSpec((tm, w), lambda i: (i, 0))
    outs = [jax.ShapeDtypeStruct((t, D_MODEL), F32),
            jax.ShapeDtypeStruct((t, PACKED), jnp.uint32),
            jax.ShapeDtypeStruct((8, t), jnp.int32),
            jax.ShapeDtypeStruct((t, LANES), F32),
            jax.ShapeDtypeStruct((8, t), jnp.int32),
            jax.ShapeDtypeStruct((8, LANES), jnp.int32)]
    slots = pl.BlockSpec((8, tm), lambda i: (0, i))
    return pl.pallas_call(
        _post_kernel,
        out_shape=outs,
        grid=(t // tm,),
        in_specs=[
            rows(N_HEADS * V_HEAD), rows(D_MODEL), rows(D_MODEL), rows(D_MODEL),
            pl.BlockSpec((None, 8, D_MODEL), lambda i: (i // tiles_per_seq, 0, 0)),
            full(wua), full(wo), full(norm_g), full(rw_hi), full(rw_lo), full(rb_pad),
        ],
        out_specs=[rows(D_MODEL), rows(PACKED), slots, rows(LANES), slots,
                   pl.BlockSpec((8, LANES), lambda i: (0, 0))],
        scratch_shapes=[pltpu.VMEM((8, LANES), F32)],
        compiler_params=pltpu.CompilerParams(
            dimension_semantics=("arbitrary",), vmem_limit_bytes=VMEM_LIMIT),
        name="post_mixer",
    )(attn, sga, gc, x2, mod, wua, wo, norm_g, rw_hi, rw_lo, rb_pad)


_GM_COUNT, _GM_FIRST, _GM_BLOCKS = range(3)
_ST_EXPERT, _ST_SLOT = range(2)


def _moe_kernel(gm_ref, xs_ref, wgu_hbm, bgu_ref, wd_hbm, bd_ref, o_ref,
                wgu_f, wd_f, wgu_bf, wd_bf, sem, st_ref):
    weights = (wgu_hbm, wd_hbm, wgu_f, wd_f, wgu_bf, wd_bf, sem)
    rows = (xs_ref, bgu_ref, bd_ref, o_ref, wgu_bf, wd_bf)
    b0 = pl.program_id(0) * MOE_STEP_BLOCKS
    e, in_group, used = _moe_enter(b0, gm_ref, st_ref, *weights)
    together = jnp.logical_and(used, in_group + MOE_STEP_BLOCKS <= gm_ref[_GM_BLOCKS, e])

    @pl.when(together)
    def _():
        _moe_rows(slice(0, MOE_STEP_BLOCKS * MOE_BLOCK), e, in_group, True, gm_ref, *rows)

    @pl.when(jnp.logical_not(together))
    def _():
        _moe_rows(slice(0, MOE_BLOCK), e, in_group, used, gm_ref, *rows)
        for r in range(1, MOE_STEP_BLOCKS):
            e_r, in_group_r, used_r = _moe_enter(b0 + r, gm_ref, st_ref, *weights)
            _moe_rows(slice(r * MOE_BLOCK, (r + 1) * MOE_BLOCK), e_r, in_group_r, used_r, gm_ref,
                      *rows)


def _moe_enter(b, gm_ref, st_ref, wgu_hbm, wd_hbm, wgu_f, wd_f, wgu_bf, wd_bf, sem):
    def weight_copies(expert, sl):
        return (pltpu.make_async_copy(wgu_hbm.at[expert], wgu_f.at[sl], sem.at[0, sl]),
                pltpu.make_async_copy(wd_hbm.at[expert], wd_f.at[sl], sem.at[1, sl]))

    def next_group(e):
        return lax.while_loop(
            lambda k: jnp.logical_and(k < N_EXPERTS,
                                      gm_ref[_GM_BLOCKS, jnp.minimum(k, N_EXPERTS - 1)] == 0),
            lambda k: k + 1, e)

    @pl.when(b == 0)
    def _():
        e0 = next_group(0)
        st_ref[_ST_EXPERT] = e0
        st_ref[_ST_SLOT] = 1
        for cp in weight_copies(e0, 0):
            cp.start()

    e_prev = st_ref[_ST_EXPERT]
    past = b >= gm_ref[_GM_FIRST, e_prev] + gm_ref[_GM_BLOCKS, e_prev]
    e = jnp.minimum(jnp.where(past, next_group(e_prev + 1), e_prev), N_EXPERTS - 1)
    st_ref[_ST_EXPERT] = e
    in_group = b - gm_ref[_GM_FIRST, e]
    used = jnp.logical_and(in_group >= 0, in_group < gm_ref[_GM_BLOCKS, e])

    @pl.when(jnp.logical_and(used, in_group == 0))
    def _():
        slot = 1 - st_ref[_ST_SLOT]
        st_ref[_ST_SLOT] = slot
        nxt = next_group(e + 1)

        @pl.when(nxt < N_EXPERTS)
        def _():
            for cp in weight_copies(nxt, 1 - slot):
                cp.start(priority=1)

        for cp in weight_copies(e, slot):
            cp.wait()
        wgu_bf[...] = wgu_f[slot].astype(BF16)
        wd_bf[...] = wd_f[slot].astype(BF16)

    return e, in_group, used


def _moe_rows(rs, e, in_group, used, gm_ref, xs_ref, bgu_ref, bd_ref, o_ref, wgu_bf, wd_bf):
    n = rs.stop - rs.start

    def ffn():
        n_valid = gm_ref[_GM_COUNT, e] - in_group * MOE_BLOCK
        row = lax.broadcasted_iota(jnp.int32, (n, PACKED), 0)
        xs = _unpack_row(jnp.where(row < n_valid, xs_ref[rs, :], 0)).astype(BF16)
        gu = _dot(xs, wgu_bf[...]) + bgu_ref[e]
        gate = jnp.minimum(gu[:, :D_EXPERT], SWIGLU_LIMIT)
        up = jnp.clip(gu[:, D_EXPERT:], -SWIGLU_LIMIT, SWIGLU_LIMIT)
        act = (up + 1.0) * (gate * jax.nn.sigmoid(gate * SWIGLU_ALPHA))
        o_ref[rs, :] = _pack_row(_dot(act.astype(BF16), wd_bf[...]) + bd_ref[e])

    if used is True:
        ffn()
        return
    pl.when(used)(ffn)

    @pl.when(jnp.logical_not(used))
    def _():
        o_ref[rs, :] = jnp.zeros((n, PACKED), o_ref.dtype)


def _moe(group_table, xs, w_gu, b_gu, w_down, b_down):
    n_rows = xs.shape[0]
    step_rows = MOE_STEP_BLOCKS * MOE_BLOCK
    assert n_rows % step_rows == 0
    grid_spec = pltpu.PrefetchScalarGridSpec(
        num_scalar_prefetch=1,
        grid=(n_rows // step_rows,),
        in_specs=[
            pl.BlockSpec((step_rows, PACKED), lambda b, gm: (b, 0)),
            pl.BlockSpec(memory_space=pl.ANY),
            pl.BlockSpec(b_gu.shape, lambda b, gm: (0, 0, 0)),
            pl.BlockSpec(memory_space=pl.ANY),
            pl.BlockSpec(b_down.shape, lambda b, gm: (0, 0, 0)),
        ],
        out_specs=pl.BlockSpec((step_rows, PACKED), lambda b, gm: (b, 0)),
        scratch_shapes=[pltpu.VMEM((2, D_MODEL, 2 * D_EXPERT), F32),
                        pltpu.VMEM((2, D_EXPERT, D_MODEL), F32),
                        pltpu.VMEM((D_MODEL, 2 * D_EXPERT), BF16),
                        pltpu.VMEM((D_EXPERT, D_MODEL), BF16),
                        pltpu.SemaphoreType.DMA((2, 2)),
                        pltpu.SMEM((2,), jnp.int32)],
    )
    return pl.pallas_call(
        _moe_kernel,
        out_shape=jax.ShapeDtypeStruct((n_rows, PACKED), jnp.uint32),
        grid_spec=grid_spec,
        compiler_params=pltpu.CompilerParams(
            dimension_semantics=("arbitrary",), vmem_limit_bytes=VMEM_LIMIT),
        name="moe_experts",
    )(group_table, xs, w_gu, b_gu, w_down, b_down)


def _final_kernel(last_layer, x1_ref, y_ref, gate_ref, mod_ref, g_ref, o_ref):
    mod = mod_ref[...]
    gate = gate_ref[...]
    ffn = gate[:, 0:1] * _unpack_row(y_ref[0])
    for kk in range(1, TOP_K):
        ffn = ffn + gate[:, kk:kk + 1] * _unpack_row(y_ref[kk])
    x = x1_ref[...] + mod[5:6] * ffn
    o_ref[...] = _rms(x, g_ref[...]) if last_layer else x


def _final(x1, y_kt, gate, mod, norm_g, seq, last_layer):
    t = x1.shape[0]
    tm = ROW_TILE
    tiles_per_seq = seq // tm
    rows = lambda w: pl.BlockSpec((tm, w), lambda i: (i, 0))
    return pl.pallas_call(
        functools.partial(_final_kernel, last_layer),
        out_shape=jax.ShapeDtypeStruct((t, D_MODEL), F32),
        grid=(t // tm,),
        in_specs=[
            rows(D_MODEL), pl.BlockSpec((TOP_K, tm, PACKED), lambda i: (0, i, 0)), rows(LANES),
            pl.BlockSpec((None, 8, D_MODEL), lambda i: (i // tiles_per_seq, 0, 0)),
            pl.BlockSpec(norm_g.shape, lambda i: (0, 0)),
        ],
        out_specs=rows(D_MODEL),
        compiler_params=pltpu.CompilerParams(
            dimension_semantics=("arbitrary",), vmem_limit_bytes=VMEM_LIMIT),
        name="combine_final",
    )(x1, y_kt, gate, mod, norm_g)


def _swap_halves(w):
    half = w.shape[-1] // 2
    return jnp.concatenate([w[..., half:], w[..., :half]], axis=-1)


def _prep_weights(w_in, w_uq, w_ukv):
    d = w_in.shape[0]
    splits = (Q_LORA, KV_LORA, QK_ROPE, CONV_WIDTH, CONV_WIDTH, CONV_WIDTH, D_MODEL, D_MODEL)
    offs = [0]
    for s in splits:
        offs.append(offs[-1] + s)
    part = lambda n: w_in[:, offs[n]:offs[n + 1]]
    z = lambda n: jnp.zeros((d, n), w_in.dtype)
    w_kpe = part(2)
    kpe_a = jnp.concatenate([z(QK_NOPE), w_kpe, z(HEAD_PAD - QK_HEAD)], axis=1)
    kpe_b = jnp.concatenate([z(QK_NOPE), _swap_halves(w_kpe), z(HEAD_PAD - QK_HEAD)], axis=1)
    w_lat = jnp.concatenate([part(0), part(1), kpe_a, kpe_b], axis=1).astype(BF16)
    w_conv = w_in[:, offs[3]:offs[6]].astype(BF16)
    w_gate = w_in[:, offs[6]:offs[8]].astype(BF16)

    wq = w_uq.reshape(Q_LORA, N_HEADS, QK_HEAD)
    zq = lambda n: jnp.zeros((Q_LORA, N_HEADS, n), w_uq.dtype)
    wq2 = jnp.concatenate([wq, zq(HEAD_PAD - QK_HEAD)], axis=-1)
    wq2s = jnp.concatenate([zq(QK_NOPE), _swap_halves(wq[..., QK_NOPE:]), zq(HEAD_PAD - QK_HEAD)],
                           axis=-1)
    wq2 = wq2.reshape(Q_LORA, N_HEADS * HEAD_PAD).astype(BF16)
    wq2s = wq2s.reshape(Q_LORA, N_HEADS * HEAD_PAD).astype(BF16)

    wkv = w_ukv.reshape(KV_LORA, N_HEADS, QK_NOPE + V_HEAD)
    wk2 = jnp.concatenate([wkv[..., :QK_NOPE],
                           jnp.zeros((KV_LORA, N_HEADS, HEAD_PAD - QK_NOPE), w_ukv.dtype)], axis=-1)
    wk2 = wk2.reshape(KV_LORA, N_HEADS * HEAD_PAD).astype(BF16)
    wv = jnp.concatenate([wkv[..., QK_NOPE:],
                          jnp.zeros((KV_LORA, N_HEADS, HEAD_PAD - V_HEAD), w_ukv.dtype)], axis=-1)
    wv = wv.reshape(KV_LORA, N_HEADS * HEAD_PAD).astype(BF16)
    return w_lat, w_conv, w_gate, wq2, wq2s, wk2, wv


def _rope_freqs():
    inv_freq = 1.0 / (ROPE_THETA ** (jnp.arange(0, QK_ROPE, 2, dtype=F32) / QK_ROPE))
    return inv_freq.reshape(QK_ROPE // 2, 1)


def _dest_kernel(gm_ref, idx_ref, rank_ref, o_ref):
    idx = idx_ref[...]
    dest = rank_ref[...]
    for e in range(N_EXPERTS):
        dest = dest + jnp.where(idx == e, gm_ref[_GM_FIRST, e] * MOE_BLOCK, 0)
    o_ref[...] = dest


def _route(top_idx, rank, counts, n_tokens):
    blocks = (counts + MOE_BLOCK - 1) // MOE_BLOCK
    first_block = jnp.cumsum(blocks) - blocks
    table = jnp.stack([counts, first_block, blocks]).astype(jnp.int32)
    whole = pl.BlockSpec(top_idx.shape, lambda i, gm: (0, 0))
    dest = pl.pallas_call(
        _dest_kernel,
        out_shape=jax.ShapeDtypeStruct(rank.shape, jnp.int32),
        grid_spec=pltpu.PrefetchScalarGridSpec(
            num_scalar_prefetch=1, grid=(1,), in_specs=[whole, whole], out_specs=whole),
        name="row_destinations",
    )(table, top_idx, rank)[:TOP_K]
    n_rows = n_tokens * TOP_K + N_EXPERTS * MOE_BLOCK
    return dest, table, n_rows


SC_CORES = 2
SC_SUBCORES = 16
SC_WORKERS = SC_CORES * SC_SUBCORES
SC_CHUNK = 64
SC_GATHER_RING = 3

def _sc_mesh():
    return plsc.VectorSubcoreMesh(core_axis_name="c", subcore_axis_name="s")


def _sc_worker():
    return lax.axis_index("s") * SC_CORES + lax.axis_index("c")


def _dispatch(h2, dest, n_rows):
    t, d = h2.shape
    per_w = t // SC_WORKERS
    n_chunks = per_w // SC_CHUNK
    assert per_w % (2 * SC_CHUNK) == 0
    idx = dest.reshape(TOP_K, SC_WORKERS, n_chunks, SC_CHUNK).transpose(1, 0, 2, 3)
    idx = idx.reshape(SC_WORKERS, TOP_K * n_chunks, SC_CHUNK)

    @functools.partial(
        pl.kernel, mesh=_sc_mesh(),
        out_type=jax.ShapeDtypeStruct((n_rows, d), h2.dtype),
        scratch_types=[pltpu.VMEM((TOP_K * n_chunks, SC_CHUNK), jnp.int32),
                       pltpu.VMEM((2, SC_CHUNK, d), h2.dtype),
                       pltpu.SemaphoreType.DMA((2,)),
                       pltpu.SemaphoreType.DMA((2,))],
        name="moe_dispatch")
    def run(h2_hbm, idx_hbm, xs_hbm, idx_v, rows_v, rsem, ssem):
        w = _sc_worker()
        pltpu.sync_copy(idx_hbm.at[w], idx_v)

        def read(g, b):
            src = h2_hbm.at[pl.ds(w * per_w + g * SC_CHUNK, SC_CHUNK)]
            return pltpu.make_async_copy(src, rows_v.at[b], rsem.at[b])

        def scatter(g, kk, b):
            dst = xs_hbm.at[idx_v.at[kk * n_chunks + g]]
            return pltpu.make_async_copy(rows_v.at[b], dst, ssem.at[b])

        read(0, 0).start()

        @pl.loop(0, n_chunks, step=2)
        def _(g0):
            for b in range(2):
                g = g0 + b
                read(g, b).wait()

                @pl.when(g + 1 < n_chunks)
                def _():
                    read(g + 1, 1 - b).start()

                for kk in range(TOP_K):
                    scatter(g, kk, b).start()
                for kk in range(TOP_K):
                    scatter(g, kk, b).wait()

    return run(h2, idx)


def _undispatch(ys, dest):
    t = dest.shape[1]
    d = ys.shape[1]
    n_out = t * TOP_K
    per_w = n_out // SC_WORKERS
    n_chunks = per_w // SC_CHUNK
    idx = dest.reshape(SC_WORKERS, n_chunks, SC_CHUNK)
    ring = SC_GATHER_RING

    @functools.partial(
        pl.kernel, mesh=_sc_mesh(),
        out_type=jax.ShapeDtypeStruct((n_out, d), ys.dtype),
        scratch_types=[pltpu.VMEM((n_chunks, SC_CHUNK), jnp.int32),
                       pltpu.VMEM((ring, SC_CHUNK, d), ys.dtype),
                       pltpu.SemaphoreType.DMA((ring,)),
                       pltpu.SemaphoreType.DMA((ring,))],
        name="moe_undispatch")
    def run(ys_hbm, idx_hbm, out_hbm, idx_v, rows_v, gsem, wsem):
        w = _sc_worker()
        pltpu.sync_copy(idx_hbm.at[w], idx_v)

        def gather(g):
            b = g % ring
            return pltpu.make_async_copy(ys_hbm.at[idx_v.at[g]], rows_v.at[b], gsem.at[b])

        def write(g):
            b = g % ring
            dst = out_hbm.at[pl.ds(w * per_w + g * SC_CHUNK, SC_CHUNK)]
            return pltpu.make_async_copy(rows_v.at[b], dst, wsem.at[b])

        for g in range(min(ring - 1, n_chunks)):
            gather(g).start()
        for g in range(n_chunks):
            gather(g).wait()
            ahead = g + ring - 1
            if ahead < n_chunks:
                if g >= 1:
                    write(g - 1).wait()
                gather(ahead).start()
            write(g).start()
        for g in range(max(n_chunks - ring, 0), n_chunks):
            write(g).wait()

    return run(ys, idx).reshape(TOP_K, t, d)


def kernel(x, c, positions, w_ada, b_ada, norm_mix_g, w_in, q_norm_g, w_uq, kv_norm_g, w_ukv,
           w_up_attn, conv_w, w_up_conv, w_o, norm_ffn_g, router_w, router_b, w_gu, b_gu,
           w_down, b_down, norm_final_g):
    batch, seq, d = x.shape
    t = batch * seq
    depth = w_ada.shape[0]
    x2 = x.reshape(t, d)
    pos = positions.astype(F32).reshape(t // ROW_TILE, 1, ROW_TILE)
    freqs = _rope_freqs()
    c_pad = jnp.zeros((8, d), F32).at[:batch].set(c)

    for l in range(depth):
        ada = _ada(c_pad, w_ada[l], b_ada[l].reshape(1, -1))
        mod = ada[:batch].reshape(batch, 6, d)
        mod = jnp.concatenate([mod, jnp.zeros((batch, 2, d), F32)], axis=1)

        w_lat, w_conv, w_gate, wq2, wq2s, wk2, wv = _prep_weights(w_in[l], w_uq[l], w_ukv[l])
        q, k, v, sga, gc = _pre(x2, mod, norm_mix_g[l].reshape(1, d), w_lat, w_conv, w_gate,
                                q_norm_g[l].reshape(1, -1), wq2, wq2s,
                                kv_norm_g[l].reshape(1, -1), wk2, wv, pos, freqs, conv_w[l],
                                w_up_conv[l].astype(BF16), seq)
        attn = _attention(q, k, v, batch, seq)

        rw_pad = jnp.concatenate([router_w[l], jnp.zeros((d, LANES - N_EXPERTS), F32)], axis=1)
        rb_pad = jnp.concatenate([router_b[l], jnp.full((LANES - N_EXPERTS,), NEG_BIG, F32)])
        rw_hi = rw_pad.astype(BF16)
        rw_lo = (rw_pad - rw_hi.astype(F32)).astype(BF16)
        x1, h2, idx_pad, gate_pad, rank_pad, counts = _post(
            attn, sga, gc, x2, mod, w_up_attn[l].astype(BF16), w_o[l].astype(BF16),
            norm_ffn_g[l].reshape(1, d), rw_hi, rw_lo, rb_pad.reshape(1, LANES), seq)

        dest, group_table, n_rows = _route(
            idx_pad, rank_pad, counts[0, :N_EXPERTS], t)
        xs = _dispatch(h2, dest, n_rows)
        ys = _moe(group_table, xs, w_gu[l], b_gu[l].reshape(N_EXPERTS, 1, -1),
                  w_down[l], b_down[l].reshape(N_EXPERTS, 1, -1))
        y_kt = _undispatch(ys, dest)
        x2 = _final(x1, y_kt, gate_pad, mod, norm_final_g.reshape(1, d), seq, l == depth - 1)

    return x2.reshape(batch, seq, d)
```

```python
import functools
import math

import jax
import jax.numpy as jnp
from jax import lax
from jax.experimental import pallas as pl
from jax.experimental.pallas import tpu as pltpu
from jax.experimental.pallas import tpu_sc as plsc

D_MODEL = 1024
CHUNK = 64
N_HEADS = 8
Q_LORA = 256
KV_LORA = 128
QK_NOPE = 64
QK_ROPE = 32
V_HEAD = 64
QK_HEAD = QK_NOPE + QK_ROPE
ROPE_THETA = 10000.0
CONV_WIDTH = 512
CONV_K = 3
N_EXPERTS = 32
TOP_K = 4
D_EXPERT = 1024
SWIGLU_LIMIT = 7.0
SWIGLU_ALPHA = 1.702
MOE_BLOCK = 256
RMS_EPS = 1e-6

LANES = 128
HEAD_PAD = 128
NEG_BIG = -1e30
VMEM_LIMIT = 56 * 1024 * 1024

F32 = jnp.float32
BF16 = jnp.bfloat16

Q_PRESCALE = (QK_HEAD ** -0.5) * math.log2(math.e)

ROW_TILE = 1024
MOE_STEP_BLOCKS = 4
POST_TILE = 1024
POST_SUB = 512
ATT_BLOCK = 512
ATT_WIDE = 2


def _rms(x, g):
    ms = jnp.mean(x * x, axis=-1, keepdims=True)
    return x * lax.rsqrt(ms + RMS_EPS) * g


def _dot(a, b):
    return jnp.dot(a, b, preferred_element_type=F32)


PACKED = D_MODEL // 2


def _pack_row(x):
    return pltpu.pack_elementwise([x[:, :PACKED], x[:, PACKED:]], packed_dtype=BF16)


def _unpack_row(w):
    half = lambda i: pltpu.unpack_elementwise(w, index=i, packed_dtype=BF16, unpacked_dtype=F32)
    return jnp.concatenate([half(0), half(1)], axis=-1)


def _ada_kernel(c_ref, w_ref, b_ref, o_ref):
    c = c_ref[...]
    ca = (c * jax.nn.sigmoid(c)).astype(BF16)
    o_ref[...] = _dot(ca, w_ref[...].astype(BF16)) + b_ref[...]


def _ada(c_pad, w_ada, b_ada):
    n = w_ada.shape[1]
    tn = 1024
    return pl.pallas_call(
        _ada_kernel,
        out_shape=jax.ShapeDtypeStruct((c_pad.shape[0], n), F32),
        grid=(n // tn,),
        in_specs=[
            pl.BlockSpec(c_pad.shape, lambda j: (0, 0)),
            pl.BlockSpec((D_MODEL, tn), lambda j: (0, j)),
            pl.BlockSpec((1, tn), lambda j: (0, j)),
        ],
        out_specs=pl.BlockSpec((c_pad.shape[0], tn), lambda j: (0, j)),
        compiler_params=pltpu.CompilerParams(
            dimension_semantics=("arbitrary",), vmem_limit_bytes=VMEM_LIMIT),
        name="ada",
    )(c_pad, w_ada, b_ada)


_C_QLAT = 0
_C_KVLAT = _C_QLAT + Q_LORA
_C_KPE = _C_KVLAT + KV_LORA
_C_END = _C_KPE + HEAD_PAD


def _pre_kernel(tiles_per_seq, x_ref, mod_ref, g_ref, wlat_ref, wconv_ref, wgate_ref, qg_ref,
                wq_ref, kvg_ref, wk_ref, wv_ref, pos_ref, freq_ref, cw_ref, wuc_ref,
                q_ref, k_ref, v_ref, sga_ref, gc_ref, carry_ref):
    i = pl.program_id(0)
    tm = x_ref.shape[0]
    mod = mod_ref[...]
    h = _rms(x_ref[...], g_ref[...]) * (1.0 + mod[1:2]) + mod[0:1]
    hb = h.astype(BF16)

    ang = freq_ref[...] * pos_ref[...]
    cos_t, sin_t = jnp.cos(ang), jnp.sin(ang)
    ones_t = jnp.ones((QK_NOPE, tm), F32)
    zeros_t = jnp.zeros((QK_NOPE, tm), F32)
    pad_t = jnp.zeros((HEAD_PAD - QK_HEAD, tm), F32)
    cosf = jnp.concatenate([ones_t, cos_t, cos_t, pad_t], axis=0).T
    sinf = jnp.concatenate([zeros_t, -sin_t, sin_t, pad_t], axis=0).T

    first_half = lax.broadcasted_iota(jnp.int32, (tm, HEAD_PAD), 1) < QK_NOPE + QK_ROPE // 2

    def rope(slab):
        swapped = jnp.where(first_half, pltpu.roll(slab, HEAD_PAD - QK_ROPE // 2, 1),
                            pltpu.roll(slab, QK_ROPE // 2, 1))
        return slab * cosf + swapped * sinf

    small = _dot(hb, wlat_ref[...])
    q_lat = small[:, _C_QLAT:_C_KVLAT]
    kv_lat = small[:, _C_KVLAT:_C_KPE]
    kpe = rope(small[:, _C_KPE:_C_END])
    qn = _rms(q_lat, qg_ref[...]).astype(BF16)
    q = _dot(qn, wq_ref[...])
    q = jnp.concatenate([rope(q[:, hd * HEAD_PAD:(hd + 1) * HEAD_PAD]) for hd in range(N_HEADS)],
                        axis=-1)
    q_ref[...] = (q * Q_PRESCALE).astype(BF16)
    kvn = _rms(kv_lat, kvg_ref[...]).astype(BF16)
    k = _dot(kvn, wk_ref[...]) + jnp.concatenate([kpe] * N_HEADS, axis=-1)
    k_ref[...] = k.astype(BF16)
    lane = lax.broadcasted_iota(jnp.int32, (tm, N_HEADS * HEAD_PAD), 1)
    ones_col = jnp.where(lane % HEAD_PAD == V_HEAD, 1.0, 0.0)
    v_ref[...] = (_dot(kvn, wv_ref[...]) + ones_col).astype(BF16)

    ucb = _dot(hb, wconv_ref[...])
    cu = ucb[:, 0:CONV_WIDTH] * ucb[:, CONV_WIDTH:2 * CONV_WIDTH]
    b_gate = ucb[:, 2 * CONV_WIDTH:3 * CONV_WIDTH]

    @pl.when(i % tiles_per_seq == 0)
    def _():
        carry_ref[...] = jnp.zeros_like(carry_ref)

    prev = carry_ref[...]
    row = lax.broadcasted_iota(jnp.int32, cu.shape, 0)
    cu1 = jnp.where(row == 0, prev[7:8], pltpu.roll(cu, 1, 0))
    cu2 = jnp.where(row == 0, prev[6:7], jnp.where(row == 1, prev[7:8], pltpu.roll(cu, 2, 0)))
    cw = cw_ref[...]
    z = cw[2:3] * cu + cw[1:2] * cu1 + cw[0:1] * cu2
    carry_ref[...] = cu[tm - 8:tm]
    c_branch = _dot((b_gate * z).astype(BF16), wuc_ref[...])

    gates = _dot(hb, wgate_ref[...])
    sga_ref[...] = jax.nn.sigmoid(gates[:, 0:D_MODEL]).astype(BF16)
    gc_ref[...] = (jax.nn.sigmoid(gates[:, D_MODEL:]) * c_branch).astype(BF16)


def _pre(x2, mod, norm_g, w_lat, w_conv, w_gate, q_norm_g, wq2, kv_norm_g, wk2, wv, pos,
         freqs, conv_w, w_up_conv, seq):
    t = x2.shape[0]
    tm = ROW_TILE
    tiles_per_seq = seq // tm
    full = lambda a: pl.BlockSpec(a.shape, lambda i: (0,) * a.ndim)
    rows = lambda w: pl.BlockSpec((tm, w), lambda i: (i, 0))
    outs = [jax.ShapeDtypeStruct((t, N_HEADS * HEAD_PAD), BF16),
            jax.ShapeDtypeStruct((t, N_HEADS * HEAD_PAD), BF16),
            jax.ShapeDtypeStruct((t, N_HEADS * HEAD_PAD), BF16),
            jax.ShapeDtypeStruct((t, D_MODEL), BF16),
            jax.ShapeDtypeStruct((t, D_MODEL), BF16)]
    return pl.pallas_call(
        functools.partial(_pre_kernel, tiles_per_seq),
        out_shape=outs,
        grid=(t // tm,),
        in_specs=[
            rows(D_MODEL),
            pl.BlockSpec((None, 8, D_MODEL), lambda i: (i // tiles_per_seq, 0, 0)),
            full(norm_g), full(w_lat), full(w_conv), full(w_gate), full(q_norm_g), full(wq2),
            full(kv_norm_g), full(wk2), full(wv),
            pl.BlockSpec((None, 1, tm), lambda i: (i, 0, 0)), full(freqs), full(conv_w),
            full(w_up_conv),
        ],
        out_specs=[rows(N_HEADS * HEAD_PAD), rows(N_HEADS * HEAD_PAD), rows(N_HEADS * HEAD_PAD),
                   rows(D_MODEL), rows(D_MODEL)],
        scratch_shapes=[pltpu.VMEM((8, CONV_WIDTH), F32)],
        compiler_params=pltpu.CompilerParams(
            dimension_semantics=("arbitrary",), vmem_limit_bytes=VMEM_LIMIT),
        name="pre_mixer",
    )(x2, mod, norm_g, w_lat, w_conv, w_gate, q_norm_g, wq2, kv_norm_g, wk2, wv, pos,
      freqs, conv_w, w_up_conv)


def _attn_kernel(q_ref, k_ref, v_ref, o_ref, m_ref, acc_ref):
    i = pl.program_id(1)
    tq = q_ref.shape[0]

    m_ref[...] = jnp.full_like(m_ref, NEG_BIG)
    acc_ref[...] = jnp.zeros_like(acc_ref)

    def step(k0, tk, masked):
        if masked:
            rq = (lax.broadcasted_iota(jnp.int32, (tq, tk), 0) + (tk - tq)) // CHUNK
            ck = lax.broadcasted_iota(jnp.int32, (tq, tk), 1) // CHUNK
            allowed = ck <= rq
        for hd in range(N_HEADS):
            hs = slice(hd * HEAD_PAD, (hd + 1) * HEAD_PAD)
            s = lax.dot_general(q_ref[:, hs], k_ref[pl.ds(k0, tk), hs],
                                (((1,), (1,)), ((), ())), preferred_element_type=F32)
            if masked:
                s = jnp.where(allowed, s, NEG_BIG)
            m_old = m_ref[hd]
            s_max = s[:, 0:LANES]
            for c in range(1, tk // LANES):
                s_max = jnp.maximum(s_max, s[:, c * LANES:(c + 1) * LANES])
            m_new = jnp.maximum(m_old, jnp.max(s_max, axis=-1, keepdims=True))
            alpha = jnp.exp2(m_old - m_new)
            p = jnp.concatenate(
                [jnp.exp2(s[:, c * LANES:(c + 1) * LANES] - m_new).astype(BF16)
                 for c in range(tk // LANES)], axis=-1)
            acc_ref[hd] = alpha * acc_ref[hd] + _dot(p, v_ref[pl.ds(k0, tk), hs])
            m_ref[hd] = m_new

    wide = ATT_WIDE * tq

    def body(j, carry):
        step(pl.multiple_of(j * wide, wide), wide, False)
        return carry

    lax.fori_loop(0, i // ATT_WIDE, body, 0)

    for r in range(ATT_WIDE):
        @pl.when(i % ATT_WIDE == r)
        def _():
            step(pl.multiple_of((i - r) * tq, tq), (r + 1) * tq, True)

    for hp in range(N_HEADS // 2):
        pair = []
        for hd in (2 * hp, 2 * hp + 1):
            acc = acc_ref[hd]
            pair.append(acc[:, 0:V_HEAD] / acc[:, V_HEAD:V_HEAD + 1])
        o_ref[:, hp * LANES:(hp + 1) * LANES] = jnp.concatenate(pair, axis=-1).astype(BF16)


def _attention(q, k, v, batch, seq):
    tq = ATT_BLOCK
    nq = seq // tq
    return pl.pallas_call(
        _attn_kernel,
        out_shape=jax.ShapeDtypeStruct((batch * seq, N_HEADS * V_HEAD), BF16),
        grid=(batch, nq),
        in_specs=[
            pl.BlockSpec((tq, N_HEADS * HEAD_PAD), lambda b, i: (b * nq + i, 0)),
            pl.BlockSpec((seq, N_HEADS * HEAD_PAD), lambda b, i: (b, 0)),
            pl.BlockSpec((seq, N_HEADS * HEAD_PAD), lambda b, i: (b, 0)),
        ],
        out_specs=pl.BlockSpec((tq, N_HEADS * V_HEAD), lambda b, i: (b * nq + i, 0)),
        scratch_shapes=[pltpu.VMEM((N_HEADS, tq, LANES), F32),
                        pltpu.VMEM((N_HEADS, tq, LANES), F32)],
        compiler_params=pltpu.CompilerParams(
            dimension_semantics=("arbitrary", "arbitrary"), vmem_limit_bytes=VMEM_LIMIT),
        name="attention",
    )(q, k, v)


def _post_kernel(attn_ref, sga_ref, gc_ref, x_ref, mod_ref, wua_ref, wo_ref, g_ref, rwh_ref,
                 rwl_ref, rb_ref, x1_ref, h2_ref, idx_ref, gate_ref, rank_ref, cnt_out_ref, cnt_ref):
    @pl.when(pl.program_id(0) == 0)
    def _():
        cnt_ref[...] = jnp.zeros_like(cnt_ref)

    counts = cnt_ref[...]
    for r0 in range(0, x_ref.shape[0], POST_SUB):
        counts = _post_rows(slice(r0, r0 + POST_SUB), counts, attn_ref, sga_ref, gc_ref, x_ref,
                            mod_ref, wua_ref, wo_ref, g_ref, rwh_ref, rwl_ref, rb_ref, x1_ref,
                            h2_ref, idx_ref, gate_ref, rank_ref)
    cnt_ref[...] = counts
    cnt_out_ref[...] = counts.astype(jnp.int32)


def _post_rows(rs, counts, attn_ref, sga_ref, gc_ref, x_ref, mod_ref, wua_ref, wo_ref, g_ref,
               rwh_ref, rwl_ref, rb_ref, x1_ref, h2_ref, idx_ref, gate_ref, rank_ref):
    mod = mod_ref[...]
    a_branch = _dot(attn_ref[rs, :], wua_ref[...])
    merged = sga_ref[rs, :].astype(F32) * a_branch + gc_ref[rs, :].astype(F32)
    mix = _dot(merged.astype(BF16), wo_ref[...])
    x1 = x_ref[rs, :] + mod[2:3] * mix
    x1_ref[rs, :] = x1
    h2 = _rms(x1, g_ref[...]) * (1.0 + mod[4:5]) + mod[3:4]
    h2_ref[rs, :] = _pack_row(h2)

    h_hi = h2.astype(BF16)
    h_lo = (h2 - h_hi.astype(F32)).astype(BF16)
    logits = (_dot(h_hi, rwh_ref[...]) + _dot(h_lo, rwh_ref[...]) + _dot(h_hi, rwl_ref[...])
              + rb_ref[...])
    lane = lax.broadcasted_iota(jnp.int32, logits.shape, 1)
    work = logits
    vals, idxs = [], []
    for _ in range(TOP_K):
        mk = jnp.max(work, axis=-1, keepdims=True)
        ik = jnp.min(jnp.where(work == mk, lane, LANES), axis=-1, keepdims=True)
        vals.append(mk)
        idxs.append(ik)
        work = jnp.where(lane == ik, -jnp.inf, work)
    es = [jnp.exp(vk - vals[0]) for vk in vals]
    denom = es[0] + es[1] + es[2] + es[3]
    tm = logits.shape[0]
    chosen = jnp.zeros(logits.shape, F32)
    for kk in range(TOP_K):
        chosen = chosen + jnp.where(lane == idxs[kk], 1.0, 0.0)
    r_i = lax.broadcasted_iota(jnp.int32, (tm, tm), 0)
    c_i = lax.broadcasted_iota(jnp.int32, (tm, tm), 1)
    earlier = jnp.where(c_i < r_i, 1.0, 0.0).astype(BF16)
    before = _dot(earlier, chosen.astype(BF16)) + counts[0:1]

    idx_out = jnp.zeros(logits.shape, F32)
    gate_out = jnp.zeros(logits.shape, F32)
    rank_out = jnp.zeros(logits.shape, F32)
    for kk in range(TOP_K):
        rank_k = jnp.sum(jnp.where(lane == idxs[kk], before, 0.0), axis=-1, keepdims=True)
        idx_out = jnp.where(lane == kk, idxs[kk].astype(F32), idx_out)
        gate_out = jnp.where(lane == kk, es[kk] / denom, gate_out)
        rank_out = jnp.where(lane == kk, rank_k, rank_out)
    gate_ref[rs, :] = gate_out
    idx_ref[:, rs] = idx_out.T[0:8].astype(jnp.int32)
    rank_ref[:, rs] = rank_out.T[0:8].astype(jnp.int32)
    return counts + jnp.sum(chosen, axis=0, keepdims=True)


def _post(attn, sga, gc, x2, mod, wua, wo, norm_g, rw_hi, rw_lo, rb_pad, seq):
    t = x2.shape[0]
    tm = POST_TILE
    tiles_per_seq = seq // tm
    full = lambda a: pl.BlockSpec(a.shape, lambda i: (0,) * a.ndim)
    rows = lambda w: pl.BlockSpec((tm, w), lambda i: (i, 0))
    outs = [jax.ShapeDtypeStruct((t, D_MODEL), F32),
            jax.ShapeDtypeStruct((t, PACKED), jnp.uint32),
            jax.ShapeDtypeStruct((8, t), jnp.int32),
            jax.ShapeDtypeStruct((t, LANES), F32),
            jax.ShapeDtypeStruct((8, t), jnp.int32),
            jax.ShapeDtypeStruct((8, LANES), jnp.int32)]
    slots = pl.BlockSpec((8, tm), lambda i: (0, i))
    return pl.pallas_call(
        _post_kernel,
        out_shape=outs,
        grid=(t // tm,),
        in_specs=[
            rows(N_HEADS * V_HEAD), rows(D_MODEL), rows(D_MODEL), rows(D_MODEL),
            pl.BlockSpec((None, 8, D_MODEL), lambda i: (i // tiles_per_seq, 0, 0)),
            full(wua), full(wo), full(norm_g), full(rw_hi), full(rw_lo), full(rb_pad),
        ],
        out_specs=[rows(D_MODEL), rows(PACKED), slots, rows(LANES), slots,
                   pl.BlockSpec((8, LANES), lambda i: (0, 0))],
        scratch_shapes=[pltpu.VMEM((8, LANES), F32)],
        compiler_params=pltpu.CompilerParams(
            dimension_semantics=("arbitrary",), vmem_limit_bytes=VMEM_LIMIT),
        name="post_mixer",
    )(attn, sga, gc, x2, mod, wua, wo, norm_g, rw_hi, rw_lo, rb_pad)


_GM_COUNT, _GM_FIRST, _GM_BLOCKS = range(3)
_ST_EXPERT, _ST_SLOT = range(2)


def _moe_kernel(gm_ref, xs_ref, wgu_hbm, bgu_ref, wd_hbm, bd_ref, o_ref,
                wgu_f, wd_f, wgu_bf, wd_bf, sem, st_ref):
    weights = (wgu_hbm, wd_hbm, wgu_f, wd_f, wgu_bf, wd_bf, sem)
    rows = (xs_ref, bgu_ref, bd_ref, o_ref, wgu_bf, wd_bf)
    b0 = pl.program_id(0) * MOE_STEP_BLOCKS
    e, in_group, used = _moe_enter(b0, gm_ref, st_ref, *weights)
    together = jnp.logical_and(used, in_group + MOE_STEP_BLOCKS <= gm_ref[_GM_BLOCKS, e])

    @pl.when(together)
    def _():
        _moe_rows(slice(0, MOE_STEP_BLOCKS * MOE_BLOCK), e, in_group, True, gm_ref, *rows)

    @pl.when(jnp.logical_not(together))
    def _():
        _moe_rows(slice(0, MOE_BLOCK), e, in_group, used, gm_ref, *rows)
        for r in range(1, MOE_STEP_BLOCKS):
            e_r, in_group_r, used_r = _moe_enter(b0 + r, gm_ref, st_ref, *weights)
            _moe_rows(slice(r * MOE_BLOCK, (r + 1) * MOE_BLOCK), e_r, in_group_r, used_r, gm_ref,
                      *rows)


def _moe_enter(b, gm_ref, st_ref, wgu_hbm, wd_hbm, wgu_f, wd_f, wgu_bf, wd_bf, sem):
    def weight_copies(expert, sl):
        return (pltpu.make_async_copy(wgu_hbm.at[expert], wgu_f.at[sl], sem.at[0, sl]),
                pltpu.make_async_copy(wd_hbm.at[expert], wd_f.at[sl], sem.at[1, sl]))

    def next_group(e):
        return lax.while_loop(
            lambda k: jnp.logical_and(k < N_EXPERTS,
                                      gm_ref[_GM_BLOCKS, jnp.minimum(k, N_EXPERTS - 1)] == 0),
            lambda k: k + 1, e)

    @pl.when(b == 0)
    def _():
        e0 = next_group(0)
        st_ref[_ST_EXPERT] = e0
        st_ref[_ST_SLOT] = 1
        for cp in weight_copies(e0, 0):
            cp.start()

    e_prev = st_ref[_ST_EXPERT]
    past = b >= gm_ref[_GM_FIRST, e_prev] + gm_ref[_GM_BLOCKS, e_prev]
    e = jnp.minimum(jnp.where(past, next_group(e_prev + 1), e_prev), N_EXPERTS - 1)
    st_ref[_ST_EXPERT] = e
    in_group = b - gm_ref[_GM_FIRST, e]
    used = jnp.logical_and(in_group >= 0, in_group < gm_ref[_GM_BLOCKS, e])

    @pl.when(jnp.logical_and(used, in_group == 0))
    def _():
        slot = 1 - st_ref[_ST_SLOT]
        st_ref[_ST_SLOT] = slot
        nxt = next_group(e + 1)

        @pl.when(nxt < N_EXPERTS)
        def _():
            for cp in weight_copies(nxt, 1 - slot):
                cp.start(priority=1)

        for cp in weight_copies(e, slot):
            cp.wait()
        wgu_bf[...] = wgu_f[slot].astype(BF16)
        wd_bf[...] = wd_f[slot].astype(BF16)

    return e, in_group, used


def _moe_rows(rs, e, in_group, used, gm_ref, xs_ref, bgu_ref, bd_ref, o_ref, wgu_bf, wd_bf):
    n = rs.stop - rs.start

    def ffn():
        n_valid = gm_ref[_GM_COUNT, e] - in_group * MOE_BLOCK
        row = lax.broadcasted_iota(jnp.int32, (n, PACKED), 0)
        xs = _unpack_row(jnp.where(row < n_valid, xs_ref[rs, :], 0)).astype(BF16)
        gu = _dot(xs, wgu_bf[...]) + bgu_ref[e]
        gate = jnp.minimum(gu[:, :D_EXPERT], SWIGLU_LIMIT)
        up = jnp.clip(gu[:, D_EXPERT:], -SWIGLU_LIMIT, SWIGLU_LIMIT)
        act = (up + 1.0) * (gate * jax.nn.sigmoid(gate * SWIGLU_ALPHA))
        o_ref[rs, :] = _pack_row(_dot(act.astype(BF16), wd_bf[...]) + bd_ref[e])

    if used is True:
        ffn()
        return
    pl.when(used)(ffn)

    @pl.when(jnp.logical_not(used))
    def _():
        o_ref[rs, :] = jnp.zeros((n, PACKED), o_ref.dtype)


def _moe(group_table, xs, w_gu, b_gu, w_down, b_down):
    n_rows = xs.shape[0]
    step_rows = MOE_STEP_BLOCKS * MOE_BLOCK
    assert n_rows % step_rows == 0
    grid_spec = pltpu.PrefetchScalarGridSpec(
        num_scalar_prefetch=1,
        grid=(n_rows // step_rows,),
        in_specs=[
            pl.BlockSpec((step_rows, PACKED), lambda b, gm: (b, 0)),
            pl.BlockSpec(memory_space=pl.ANY),
            pl.BlockSpec(b_gu.shape, lambda b, gm: (0, 0, 0)),
            pl.BlockSpec(memory_space=pl.ANY),
            pl.BlockSpec(b_down.shape, lambda b, gm: (0, 0, 0)),
        ],
        out_specs=pl.BlockSpec((step_rows, PACKED), lambda b, gm: (b, 0)),
        scratch_shapes=[pltpu.VMEM((2, D_MODEL, 2 * D_EXPERT), F32),
                        pltpu.VMEM((2, D_EXPERT, D_MODEL), F32),
                        pltpu.VMEM((D_MODEL, 2 * D_EXPERT), BF16),
                        pltpu.VMEM((D_EXPERT, D_MODEL), BF16),
                        pltpu.SemaphoreType.DMA((2, 2)),
                        pltpu.SMEM((2,), jnp.int32)],
    )
    return pl.pallas_call(
        _moe_kernel,
        out_shape=jax.ShapeDtypeStruct((n_rows, PACKED), jnp.uint32),
        grid_spec=grid_spec,
        compiler_params=pltpu.CompilerParams(
            dimension_semantics=("arbitrary",), vmem_limit_bytes=VMEM_LIMIT),
        name="moe_experts",
    )(group_table, xs, w_gu, b_gu, w_down, b_down)


def _final_kernel(last_layer, x1_ref, y_ref, gate_ref, mod_ref, g_ref, o_ref):
    mod = mod_ref[...]
    gate = gate_ref[...]
    ffn = gate[:, 0:1] * _unpack_row(y_ref[0])
    for kk in range(1, TOP_K):
        ffn = ffn + gate[:, kk:kk + 1] * _unpack_row(y_ref[kk])
    x = x1_ref[...] + mod[5:6] * ffn
    o_ref[...] = _rms(x, g_ref[...]) if last_layer else x


def _final(x1, y_kt, gate, mod, norm_g, seq, last_layer):
    t = x1.shape[0]
    tm = ROW_TILE
    tiles_per_seq = seq // tm
    rows = lambda w: pl.BlockSpec((tm, w), lambda i: (i, 0))
    return pl.pallas_call(
        functools.partial(_final_kernel, last_layer),
        out_shape=jax.ShapeDtypeStruct((t, D_MODEL), F32),
        grid=(t // tm,),
        in_specs=[
            rows(D_MODEL), pl.BlockSpec((TOP_K, tm, PACKED), lambda i: (0, i, 0)), rows(LANES),
            pl.BlockSpec((None, 8, D_MODEL), lambda i: (i // tiles_per_seq, 0, 0)),
            pl.BlockSpec(norm_g.shape, lambda i: (0, 0)),
        ],
        out_specs=rows(D_MODEL),
        compiler_params=pltpu.CompilerParams(
            dimension_semantics=("arbitrary",), vmem_limit_bytes=VMEM_LIMIT),
        name="combine_final",
    )(x1, y_kt, gate, mod, norm_g)


def _prep_weights(w_in, w_uq, w_ukv):
    d = w_in.shape[0]
    splits = (Q_LORA, KV_LORA, QK_ROPE, CONV_WIDTH, CONV_WIDTH, CONV_WIDTH, D_MODEL, D_MODEL)
    offs = [0]
    for s in splits:
        offs.append(offs[-1] + s)
    part = lambda n: w_in[:, offs[n]:offs[n + 1]]
    z = lambda n: jnp.zeros((d, n), w_in.dtype)
    w_kpe = part(2)
    kpe_slab = jnp.concatenate([z(QK_NOPE), w_kpe, z(HEAD_PAD - QK_HEAD)], axis=1)
    w_lat = jnp.concatenate([part(0), part(1), kpe_slab], axis=1).astype(BF16)
    w_conv = w_in[:, offs[3]:offs[6]].astype(BF16)
    w_gate = w_in[:, offs[6]:offs[8]].astype(BF16)

    wq = w_uq.reshape(Q_LORA, N_HEADS, QK_HEAD)
    zq = lambda n: jnp.zeros((Q_LORA, N_HEADS, n), w_uq.dtype)
    wq2 = jnp.concatenate([wq, zq(HEAD_PAD - QK_HEAD)], axis=-1)
    wq2 = wq2.reshape(Q_LORA, N_HEADS * HEAD_PAD).astype(BF16)

    wkv = w_ukv.reshape(KV_LORA, N_HEADS, QK_NOPE + V_HEAD)
    wk2 = jnp.concatenate([wkv[..., :QK_NOPE],
                           jnp.zeros((KV_LORA, N_HEADS, HEAD_PAD - QK_NOPE), w_ukv.dtype)], axis=-1)
    wk2 = wk2.reshape(KV_LORA, N_HEADS * HEAD_PAD).astype(BF16)
    wv = jnp.concatenate([wkv[..., QK_NOPE:],
                          jnp.zeros((KV_LORA, N_HEADS, HEAD_PAD - V_HEAD), w_ukv.dtype)], axis=-1)
    wv = wv.reshape(KV_LORA, N_HEADS * HEAD_PAD).astype(BF16)
    return w_lat, w_conv, w_gate, wq2, wk2, wv


def _rope_freqs():
    inv_freq = 1.0 / (ROPE_THETA ** (jnp.arange(0, QK_ROPE, 2, dtype=F32) / QK_ROPE))
    return inv_freq.reshape(QK_ROPE // 2, 1)


def _dest_kernel(gm_ref, idx_ref, rank_ref, o_ref):
    idx = idx_ref[...]
    dest = rank_ref[...]
    for e in range(N_EXPERTS):
        dest = dest + jnp.where(idx == e, gm_ref[_GM_FIRST, e] * MOE_BLOCK, 0)
    o_ref[...] = dest


def _route(top_idx, rank, counts, n_tokens):
    blocks = (counts + MOE_BLOCK - 1) // MOE_BLOCK
    first_block = jnp.cumsum(blocks) - blocks
    table = jnp.stack([counts, first_block, blocks]).astype(jnp.int32)
    whole = pl.BlockSpec(top_idx.shape, lambda i, gm: (0, 0))
    dest = pl.pallas_call(
        _dest_kernel,
        out_shape=jax.ShapeDtypeStruct(rank.shape, jnp.int32),
        grid_spec=pltpu.PrefetchScalarGridSpec(
            num_scalar_prefetch=1, grid=(1,), in_specs=[whole, whole], out_specs=whole),
        name="row_destinations",
    )(table, top_idx, rank)[:TOP_K]
    n_rows = n_tokens * TOP_K + N_EXPERTS * MOE_BLOCK
    return dest, table, n_rows


SC_CORES = 2
SC_SUBCORES = 16
SC_WORKERS = SC_CORES * SC_SUBCORES
SC_CHUNK = 64
SC_GATHER_RING = 3

def _sc_mesh():
    return plsc.VectorSubcoreMesh(core_axis_name="c", subcore_axis_name="s")


def _sc_worker():
    return lax.axis_index("s") * SC_CORES + lax.axis_index("c")


def _dispatch(h2, dest, n_rows):
    t, d = h2.shape
    per_w = t // SC_WORKERS
    n_chunks = per_w // SC_CHUNK
    assert per_w % (2 * SC_CHUNK) == 0
    idx = dest.reshape(TOP_K, SC_WORKERS, n_chunks, SC_CHUNK).transpose(1, 0, 2, 3)
    idx = idx.reshape(SC_WORKERS, TOP_K * n_chunks, SC_CHUNK)

    @functools.partial(
        pl.kernel, mesh=_sc_mesh(),
        out_type=jax.ShapeDtypeStruct((n_rows, d), h2.dtype),
        scratch_types=[pltpu.VMEM((TOP_K * n_chunks, SC_CHUNK), jnp.int32),
                       pltpu.VMEM((2, SC_CHUNK, d), h2.dtype),
                       pltpu.SemaphoreType.DMA((2,)),
                       pltpu.SemaphoreType.DMA((2,))],
        name="moe_dispatch")
    def run(h2_hbm, idx_hbm, xs_hbm, idx_v, rows_v, rsem, ssem):
        w = _sc_worker()
        pltpu.sync_copy(idx_hbm.at[w], idx_v)

        def read(g, b):
            src = h2_hbm.at[pl.ds(w * per_w + g * SC_CHUNK, SC_CHUNK)]
            return pltpu.make_async_copy(src, rows_v.at[b], rsem.at[b])

        def scatter(g, kk, b):
            dst = xs_hbm.at[idx_v.at[kk * n_chunks + g]]
            return pltpu.make_async_copy(rows_v.at[b], dst, ssem.at[b])

        read(0, 0).start()

        @pl.loop(0, n_chunks, step=2)
        def _(g0):
            for b in range(2):
                g = g0 + b
                read(g, b).wait()

                @pl.when(g + 1 < n_chunks)
                def _():
                    read(g + 1, 1 - b).start()

                for kk in range(TOP_K):
                    scatter(g, kk, b).start()
                for kk in range(TOP_K):
                    scatter(g, kk, b).wait()

    return run(h2, idx)


def _undispatch(ys, dest):
    t = dest.shape[1]
    d = ys.shape[1]
    n_out = t * TOP_K
    per_w = n_out // SC_WORKERS
    n_chunks = per_w // SC_CHUNK
    idx = dest.reshape(SC_WORKERS, n_chunks, SC_CHUNK)
    ring = SC_GATHER_RING

    @functools.partial(
        pl.kernel, mesh=_sc_mesh(),
        out_type=jax.ShapeDtypeStruct((n_out, d), ys.dtype),
        scratch_types=[pltpu.VMEM((n_chunks, SC_CHUNK), jnp.int32),
                       pltpu.VMEM((ring, SC_CHUNK, d), ys.dtype),
                       pltpu.SemaphoreType.DMA((ring,)),
                       pltpu.SemaphoreType.DMA((ring,))],
        name="moe_undispatch")
    def run(ys_hbm, idx_hbm, out_hbm, idx_v, rows_v, gsem, wsem):
        w = _sc_worker()
        pltpu.sync_copy(idx_hbm.at[w], idx_v)

        def gather(g):
            b = g % ring
            return pltpu.make_async_copy(ys_hbm.at[idx_v.at[g]], rows_v.at[b], gsem.at[b])

        def write(g):
            b = g % ring
            dst = out_hbm.at[pl.ds(w * per_w + g * SC_CHUNK, SC_CHUNK)]
            return pltpu.make_async_copy(rows_v.at[b], dst, wsem.at[b])

        for g in range(min(ring - 1, n_chunks)):
            gather(g).start()
        for g in range(n_chunks):
            gather(g).wait()
            ahead = g + ring - 1
            if ahead < n_chunks:
                if g >= 1:
                    write(g - 1).wait()
                gather(ahead).start()
            write(g).start()
        for g in range(max(n_chunks - ring, 0), n_chunks):
            write(g).wait()

    return run(ys, idx).reshape(TOP_K, t, d)


def kernel(x, c, positions, w_ada, b_ada, norm_mix_g, w_in, q_norm_g, w_uq, kv_norm_g, w_ukv,
           w_up_attn, conv_w, w_up_conv, w_o, norm_ffn_g, router_w, router_b, w_gu, b_gu,
           w_down, b_down, norm_final_g):
    batch, seq, d = x.shape
    t = batch * seq
    depth = w_ada.shape[0]
    x2 = x.reshape(t, d)
    pos = positions.astype(F32).reshape(t // ROW_TILE, 1, ROW_TILE)
    freqs = _rope_freqs()
    c_pad = jnp.zeros((8, d), F32).at[:batch].set(c)

    for l in range(depth):
        ada = _ada(c_pad, w_ada[l], b_ada[l].reshape(1, -1))
        mod = ada[:batch].reshape(batch, 6, d)
        mod = jnp.concatenate([mod, jnp.zeros((batch, 2, d), F32)], axis=1)

        w_lat, w_conv, w_gate, wq2, wk2, wv = _prep_weights(w_in[l], w_uq[l], w_ukv[l])
        q, k, v, sga, gc = _pre(x2, mod, norm_mix_g[l].reshape(1, d), w_lat, w_conv, w_gate,
                                q_norm_g[l].reshape(1, -1), wq2,
                                kv_norm_g[l].reshape(1, -1), wk2, wv, pos, freqs, conv_w[l],
                                w_up_conv[l].astype(BF16), seq)
        attn = _attention(q, k, v, batch, seq)

        rw_pad = jnp.concatenate([router_w[l], jnp.zeros((d, LANES - N_EXPERTS), F32)], axis=1)
        rb_pad = jnp.concatenate([router_b[l], jnp.full((LANES - N_EXPERTS,), NEG_BIG, F32)])
        rw_hi = rw_pad.astype(BF16)
        rw_lo = (rw_pad - rw_hi.astype(F32)).astype(BF16)
        x1, h2, idx_pad, gate_pad, rank_pad, counts = _post(
            attn, sga, gc, x2, mod, w_up_attn[l].astype(BF16), w_o[l].astype(BF16),
            norm_ffn_g[l].reshape(1, d), rw_hi, rw_lo, rb_pad.reshape(1, LANES), seq)

        dest, group_table, n_rows = _route(
            idx_pad, rank_pad, counts[0, :N_EXPERTS], t)
        xs = _dispatch(h2, dest, n_rows)
        ys = _moe(group_table, xs, w_gu[l], b_gu[l].reshape(N_EXPERTS, 1, -1),
                  w_down[l], b_down[l].reshape(N_EXPERTS, 1, -1))
        y_kt = _undispatch(ys, dest)
        x2 = _final(x1, y_kt, gate_pad, mod, norm_final_g.reshape(1, d), seq, l == depth - 1)

    return x2.reshape(batch, seq, d)
```

```python
import functools
import math

import jax
import jax.numpy as jnp
from jax import lax
from jax.experimental import pallas as pl
from jax.experimental.pallas import tpu as pltpu
from jax.experimental.pallas import tpu_sc as plsc

D_MODEL = 1024
CHUNK = 64
N_HEADS = 8
Q_LORA = 256
KV_LORA = 128
QK_NOPE = 64
QK_ROPE = 32
V_HEAD = 64
QK_HEAD = QK_NOPE + QK_ROPE
ROPE_THETA = 10000.0
CONV_WIDTH = 512
CONV_K = 3
N_EXPERTS = 32
TOP_K = 4
D_EXPERT = 1024
SWIGLU_LIMIT = 7.0
SWIGLU_ALPHA = 1.702
MOE_BLOCK = 256
RMS_EPS = 1e-6

LANES = 128
HEAD_PAD = 128
NEG_BIG = -1e30
VMEM_LIMIT = 56 * 1024 * 1024

F32 = jnp.float32
BF16 = jnp.bfloat16

Q_PRESCALE = (QK_HEAD ** -0.5) * math.log2(math.e)

ROW_TILE = 1024
MOE_STEP_BLOCKS = 4
POST_TILE = 1024
POST_SUB = 512
ATT_BLOCK = 512
ATT_WIDE = 4


def _rms(x, g):
    ms = jnp.mean(x * x, axis=-1, keepdims=True)
    return x * lax.rsqrt(ms + RMS_EPS) * g


def _dot(a, b):
    return jnp.dot(a, b, preferred_element_type=F32)


PACKED = D_MODEL // 2


def _pack_row(x):
    return pltpu.pack_elementwise([x[:, :PACKED], x[:, PACKED:]], packed_dtype=BF16)


def _unpack_row(w):
    half = lambda i: pltpu.unpack_elementwise(w, index=i, packed_dtype=BF16, unpacked_dtype=F32)
    return jnp.concatenate([half(0), half(1)], axis=-1)


def _ada_kernel(c_ref, w_ref, b_ref, o_ref):
    c = c_ref[...]
    ca = (c * jax.nn.sigmoid(c)).astype(BF16)
    o_ref[...] = _dot(ca, w_ref[...].astype(BF16)) + b_ref[...]


def _ada(c_pad, w_ada, b_ada):
    n = w_ada.shape[1]
    tn = 1024
    return pl.pallas_call(
        _ada_kernel,
        out_shape=jax.ShapeDtypeStruct((c_pad.shape[0], n), F32),
        grid=(n // tn,),
        in_specs=[
            pl.BlockSpec(c_pad.shape, lambda j: (0, 0)),
            pl.BlockSpec((D_MODEL, tn), lambda j: (0, j)),
            pl.BlockSpec((1, tn), lambda j: (0, j)),
        ],
        out_specs=pl.BlockSpec((c_pad.shape[0], tn), lambda j: (0, j)),
        compiler_params=pltpu.CompilerParams(
            dimension_semantics=("arbitrary",), vmem_limit_bytes=VMEM_LIMIT),
        name="ada",
    )(c_pad, w_ada, b_ada)


_C_QLAT = 0
_C_KVLAT = _C_QLAT + Q_LORA
_C_KPE = _C_KVLAT + KV_LORA
_C_END = _C_KPE + HEAD_PAD


def _pre_kernel(tiles_per_seq, x_ref, mod_ref, g_ref, wlat_ref, wconv_ref, wgate_ref, qg_ref,
                wq_ref, kvg_ref, wk_ref, wv_ref, pos_ref, freq_ref, cw_ref, wuc_ref,
                q_ref, k_ref, v_ref, sga_ref, gc_ref, carry_ref):
    i = pl.program_id(0)
    tm = x_ref.shape[0]
    mod = mod_ref[...]
    h = _rms(x_ref[...], g_ref[...]) * (1.0 + mod[1:2]) + mod[0:1]
    hb = h.astype(BF16)

    ang = freq_ref[...] * pos_ref[...]
    cos_t, sin_t = jnp.cos(ang), jnp.sin(ang)
    ones_t = jnp.ones((QK_NOPE, tm), F32)
    zeros_t = jnp.zeros((QK_NOPE, tm), F32)
    pad_t = jnp.zeros((HEAD_PAD - QK_HEAD, tm), F32)
    cosf = jnp.concatenate([ones_t, cos_t, cos_t, pad_t], axis=0).T
    sinf = jnp.concatenate([zeros_t, -sin_t, sin_t, pad_t], axis=0).T

    first_half = lax.broadcasted_iota(jnp.int32, (tm, HEAD_PAD), 1) < QK_NOPE + QK_ROPE // 2

    def rope(slab):
        swapped = jnp.where(first_half, pltpu.roll(slab, HEAD_PAD - QK_ROPE // 2, 1),
                            pltpu.roll(slab, QK_ROPE // 2, 1))
        return slab * cosf + swapped * sinf

    small = _dot(hb, wlat_ref[...])
    q_lat = small[:, _C_QLAT:_C_KVLAT]
    kv_lat = small[:, _C_KVLAT:_C_KPE]
    kpe = rope(small[:, _C_KPE:_C_END])
    qn = _rms(q_lat, qg_ref[...]).astype(BF16)
    q = _dot(qn, wq_ref[...])
    q = jnp.concatenate([rope(q[:, hd * HEAD_PAD:(hd + 1) * HEAD_PAD]) for hd in range(N_HEADS)],
                        axis=-1)
    q_ref[...] = (q * Q_PRESCALE).astype(BF16)
    kvn = _rms(kv_lat, kvg_ref[...]).astype(BF16)
    k = _dot(kvn, wk_ref[...]) + jnp.concatenate([kpe] * N_HEADS, axis=-1)
    k_ref[...] = k.astype(BF16)
    lane = lax.broadcasted_iota(jnp.int32, (tm, N_HEADS * HEAD_PAD), 1)
    ones_col = jnp.where(lane % HEAD_PAD == V_HEAD, 1.0, 0.0)
    v_ref[...] = (_dot(kvn, wv_ref[...]) + ones_col).astype(BF16)

    ucb = _dot(hb, wconv_ref[...])
    cu = ucb[:, 0:CONV_WIDTH] * ucb[:, CONV_WIDTH:2 * CONV_WIDTH]
    b_gate = ucb[:, 2 * CONV_WIDTH:3 * CONV_WIDTH]

    @pl.when(i % tiles_per_seq == 0)
    def _():
        carry_ref[...] = jnp.zeros_like(carry_ref)

    prev = carry_ref[...]
    row = lax.broadcasted_iota(jnp.int32, cu.shape, 0)
    cu1 = jnp.where(row == 0, prev[7:8], pltpu.roll(cu, 1, 0))
    cu2 = jnp.where(row == 0, prev[6:7], jnp.where(row == 1, prev[7:8], pltpu.roll(cu, 2, 0)))
    cw = cw_ref[...]
    z = cw[2:3] * cu + cw[1:2] * cu1 + cw[0:1] * cu2
    carry_ref[...] = cu[tm - 8:tm]
    c_branch = _dot((b_gate * z).astype(BF16), wuc_ref[...])

    gates = _dot(hb, wgate_ref[...])
    sga_ref[...] = jax.nn.sigmoid(gates[:, 0:D_MODEL]).astype(BF16)
    gc_ref[...] = (jax.nn.sigmoid(gates[:, D_MODEL:]) * c_branch).astype(BF16)


def _pre(x2, mod, norm_g, w_lat, w_conv, w_gate, q_norm_g, wq2, kv_norm_g, wk2, wv, pos,
         freqs, conv_w, w_up_conv, seq):
    t = x2.shape[0]
    tm = ROW_TILE
    tiles_per_seq = seq // tm
    full = lambda a: pl.BlockSpec(a.shape, lambda i: (0,) * a.ndim)
    rows = lambda w: pl.BlockSpec((tm, w), lambda i: (i, 0))
    outs = [jax.ShapeDtypeStruct((t, N_HEADS * HEAD_PAD), BF16),
            jax.ShapeDtypeStruct((t, N_HEADS * HEAD_PAD), BF16),
            jax.ShapeDtypeStruct((t, N_HEADS * HEAD_PAD), BF16),
            jax.ShapeDtypeStruct((t, D_MODEL), BF16),
            jax.ShapeDtypeStruct((t, D_MODEL), BF16)]
    return pl.pallas_call(
        functools.partial(_pre_kernel, tiles_per_seq),
        out_shape=outs,
        grid=(t // tm,),
        in_specs=[
            rows(D_MODEL),
            pl.BlockSpec((None, 8, D_MODEL), lambda i: (i // tiles_per_seq, 0, 0)),
            full(norm_g), full(w_lat), full(w_conv), full(w_gate), full(q_norm_g), full(wq2),
            full(kv_norm_g), full(wk2), full(wv),
            pl.BlockSpec((None, 1, tm), lambda i: (i, 0, 0)), full(freqs), full(conv_w),
            full(w_up_conv),
        ],
        out_specs=[rows(N_HEADS * HEAD_PAD), rows(N_HEADS * HEAD_PAD), rows(N_HEADS * HEAD_PAD),
                   rows(D_MODEL), rows(D_MODEL)],
        scratch_shapes=[pltpu.VMEM((8, CONV_WIDTH), F32)],
        compiler_params=pltpu.CompilerParams(
            dimension_semantics=("arbitrary",), vmem_limit_bytes=VMEM_LIMIT),
        name="pre_mixer",
    )(x2, mod, norm_g, w_lat, w_conv, w_gate, q_norm_g, wq2, kv_norm_g, wk2, wv, pos,
      freqs, conv_w, w_up_conv)


def _attn_kernel(q_ref, k_ref, v_ref, o_ref, m_ref, acc_ref):
    i = pl.program_id(1)
    tq = q_ref.shape[0]

    m_ref[...] = jnp.full_like(m_ref, NEG_BIG)
    acc_ref[...] = jnp.zeros_like(acc_ref)

    def step(k0, tk, masked):
        if masked:
            rq = (lax.broadcasted_iota(jnp.int32, (tq, tk), 0) + (tk - tq)) // CHUNK
            ck = lax.broadcasted_iota(jnp.int32, (tq, tk), 1) // CHUNK
            allowed = ck <= rq
        for hd in range(N_HEADS):
            hs = slice(hd * HEAD_PAD, (hd + 1) * HEAD_PAD)
            s = lax.dot_general(q_ref[:, hs], k_ref[pl.ds(k0, tk), hs],
                                (((1,), (1,)), ((), ())), preferred_element_type=F32)
            if masked:
                s = jnp.where(allowed, s, NEG_BIG)
            m_old = m_ref[hd]
            s_max = s[:, 0:LANES]
            for c in range(1, tk // LANES):
                s_max = jnp.maximum(s_max, s[:, c * LANES:(c + 1) * LANES])
            m_new = jnp.maximum(m_old, jnp.max(s_max, axis=-1, keepdims=True))
            alpha = jnp.exp2(m_old - m_new)
            p = jnp.concatenate(
                [jnp.exp2(s[:, c * LANES:(c + 1) * LANES] - m_new).astype(BF16)
                 for c in range(tk // LANES)], axis=-1)
            acc_ref[hd] = alpha * acc_ref[hd] + _dot(p, v_ref[pl.ds(k0, tk), hs])
            m_ref[hd] = m_new

    wide = ATT_WIDE * tq

    def body(j, carry):
        step(pl.multiple_of(j * wide, wide), wide, False)
        return carry

    lax.fori_loop(0, i // ATT_WIDE, body, 0)

    for r in range(ATT_WIDE):
        @pl.when(i % ATT_WIDE == r)
        def _():
            step(pl.multiple_of((i - r) * tq, tq), (r + 1) * tq, True)

    for hp in range(N_HEADS // 2):
        pair = []
        for hd in (2 * hp, 2 * hp + 1):
            acc = acc_ref[hd]
            pair.append(acc[:, 0:V_HEAD] / acc[:, V_HEAD:V_HEAD + 1])
        o_ref[:, hp * LANES:(hp + 1) * LANES] = jnp.concatenate(pair, axis=-1).astype(BF16)


def _attention(q, k, v, batch, seq):
    tq = ATT_BLOCK
    nq = seq // tq
    return pl.pallas_call(
        _attn_kernel,
        out_shape=jax.ShapeDtypeStruct((batch * seq, N_HEADS * V_HEAD), BF16),
        grid=(batch, nq),
        in_specs=[
            pl.BlockSpec((tq, N_HEADS * HEAD_PAD), lambda b, i: (b * nq + i, 0)),
            pl.BlockSpec((seq, N_HEADS * HEAD_PAD), lambda b, i: (b, 0)),
            pl.BlockSpec((seq, N_HEADS * HEAD_PAD), lambda b, i: (b, 0)),
        ],
        out_specs=pl.BlockSpec((tq, N_HEADS * V_HEAD), lambda b, i: (b * nq + i, 0)),
        scratch_shapes=[pltpu.VMEM((N_HEADS, tq, LANES), F32),
                        pltpu.VMEM((N_HEADS, tq, LANES), F32)],
        compiler_params=pltpu.CompilerParams(
            dimension_semantics=("arbitrary", "arbitrary"), vmem_limit_bytes=VMEM_LIMIT),
        name="attention",
    )(q, k, v)


def _post_kernel(attn_ref, sga_ref, gc_ref, x_ref, mod_ref, wua_ref, wo_ref, g_ref, rwh_ref,
                 rwl_ref, rb_ref, x1_ref, h2_ref, idx_ref, gate_ref, rank_ref, cnt_out_ref, cnt_ref):
    @pl.when(pl.program_id(0) == 0)
    def _():
        cnt_ref[...] = jnp.zeros_like(cnt_ref)

    counts = cnt_ref[...]
    for r0 in range(0, x_ref.shape[0], POST_SUB):
        counts = _post_rows(slice(r0, r0 + POST_SUB), counts, attn_ref, sga_ref, gc_ref, x_ref,
                            mod_ref, wua_ref, wo_ref, g_ref, rwh_ref, rwl_ref, rb_ref, x1_ref,
                            h2_ref, idx_ref, gate_ref, rank_ref)
    cnt_ref[...] = counts
    cnt_out_ref[...] = counts.astype(jnp.int32)


def _post_rows(rs, counts, attn_ref, sga_ref, gc_ref, x_ref, mod_ref, wua_ref, wo_ref, g_ref,
               rwh_ref, rwl_ref, rb_ref, x1_ref, h2_ref, idx_ref, gate_ref, rank_ref):
    mod = mod_ref[...]
    a_branch = _dot(attn_ref[rs, :], wua_ref[...])
    merged = sga_ref[rs, :].astype(F32) * a_branch + gc_ref[rs, :].astype(F32)
    mix = _dot(merged.astype(BF16), wo_ref[...])
    x1 = x_ref[rs, :] + mod[2:3] * mix
    x1_ref[rs, :] = x1
    h2 = _rms(x1, g_ref[...]) * (1.0 + mod[4:5]) + mod[3:4]
    h2_ref[rs, :] = _pack_row(h2)

    h_hi = h2.astype(BF16)
    h_lo = (h2 - h_hi.astype(F32)).astype(BF16)
    logits = (_dot(h_hi, rwh_ref[...]) + _dot(h_lo, rwh_ref[...]) + _dot(h_hi, rwl_ref[...])
              + rb_ref[...])
    lane = lax.broadcasted_iota(jnp.int32, logits.shape, 1)
    work = logits
    vals, idxs = [], []
    for _ in range(TOP_K):
        mk = jnp.max(work, axis=-1, keepdims=True)
        ik = jnp.min(jnp.where(work == mk, lane, LANES), axis=-1, keepdims=True)
        vals.append(mk)
        idxs.append(ik)
        work = jnp.where(lane == ik, -jnp.inf, work)
    es = [jnp.exp(vk - vals[0]) for vk in vals]
    denom = es[0] + es[1] + es[2] + es[3]
    tm = logits.shape[0]
    chosen = jnp.zeros(logits.shape, F32)
    for kk in range(TOP_K):
        chosen = chosen + jnp.where(lane == idxs[kk], 1.0, 0.0)
    r_i = lax.broadcasted_iota(jnp.int32, (tm, tm), 0)
    c_i = lax.broadcasted_iota(jnp.int32, (tm, tm), 1)
    earlier = jnp.where(c_i < r_i, 1.0, 0.0).astype(BF16)
    before = _dot(earlier, chosen.astype(BF16)) + counts[0:1]

    idx_out = jnp.zeros(logits.shape, F32)
    gate_out = jnp.zeros(logits.shape, F32)
    rank_out = jnp.zeros(logits.shape, F32)
    for kk in range(TOP_K):
        rank_k = jnp.sum(jnp.where(lane == idxs[kk], before, 0.0), axis=-1, keepdims=True)
        idx_out = jnp.where(lane == kk, idxs[kk].astype(F32), idx_out)
        gate_out = jnp.where(lane == kk, es[kk] / denom, gate_out)
        rank_out = jnp.where(lane == kk, rank_k, rank_out)
    gate_ref[rs, :] = gate_out
    idx_ref[:, rs] = idx_out.T[0:8].astype(jnp.int32)
    rank_ref[:, rs] = rank_out.T[0:8].astype(jnp.int32)
    return counts + jnp.sum(chosen, axis=0, keepdims=True)


def _post(attn, sga, gc, x2, mod, wua, wo, norm_g, rw_hi, rw_lo, rb_pad, seq):
    t = x2.shape[0]
    tm = POST_TILE
    tiles_per_seq = seq // tm
    full = lambda a: pl.BlockSpec(a.shape, lambda i: (0,) * a.ndim)
    rows = lambda w: pl.BlockSpec((tm, w), lambda i: (i, 0))
    outs = [jax.ShapeDtypeStruct((t, D_MODEL), F32),
            jax.ShapeDtypeStruct((t, PACKED), jnp.uint32),
            jax.ShapeDtypeStruct((8, t), jnp.int32),
            jax.ShapeDtypeStruct((t, LANES), F32),
            jax.ShapeDtypeStruct((8, t), jnp.int32),
            jax.ShapeDtypeStruct((8, LANES), jnp.int32)]
    slots = pl.BlockSpec((8, tm), lambda i: (0, i))
    return pl.pallas_call(
        _post_kernel,
        out_shape=outs,
        grid=(t // tm,),
        in_specs=[
            rows(N_HEADS * V_HEAD), rows(D_MODEL), rows(D_MODEL), rows(D_MODEL),
            pl.BlockSpec((None, 8, D_MODEL), lambda i: (i // tiles_per_seq, 0, 0)),
            full(wua), full(wo), full(norm_g), full(rw_hi), full(rw_lo), full(rb_pad),
        ],
        out_specs=[rows(D_MODEL), rows(PACKED), slots, rows(LANES), slots,
                   pl.BlockSpec((8, LANES), lambda i: (0, 0))],
        scratch_shapes=[pltpu.VMEM((8, LANES), F32)],
        compiler_params=pltpu.CompilerParams(
            dimension_semantics=("arbitrary",), vmem_limit_bytes=VMEM_LIMIT),
        name="post_mixer",
    )(attn, sga, gc, x2, mod, wua, wo, norm_g, rw_hi, rw_lo, rb_pad)


_GM_COUNT, _GM_FIRST, _GM_BLOCKS = range(3)
_ST_EXPERT, _ST_SLOT = range(2)


def _moe_kernel(gm_ref, xs_ref, wgu_hbm, bgu_ref, wd_hbm, bd_ref, o_ref,
                wgu_f, wd_f, wgu_bf, wd_bf, sem, st_ref):
    weights = (wgu_hbm, wd_hbm, wgu_f, wd_f, wgu_bf, wd_bf, sem)
    rows = (xs_ref, bgu_ref, bd_ref, o_ref, wgu_bf, wd_bf)
    b0 = pl.program_id(0) * MOE_STEP_BLOCKS
    e, in_group, used = _moe_enter(b0, gm_ref, st_ref, *weights)
    together = jnp.logical_and(used, in_group + MOE_STEP_BLOCKS <= gm_ref[_GM_BLOCKS, e])

    @pl.when(together)
    def _():
        _moe_rows(slice(0, MOE_STEP_BLOCKS * MOE_BLOCK), e, in_group, True, gm_ref, *rows)

    @pl.when(jnp.logical_not(together))
    def _():
        _moe_rows(slice(0, MOE_BLOCK), e, in_group, used, gm_ref, *rows)
        for r in range(1, MOE_STEP_BLOCKS):
            e_r, in_group_r, used_r = _moe_enter(b0 + r, gm_ref, st_ref, *weights)
            _moe_rows(slice(r * MOE_BLOCK, (r + 1) * MOE_BLOCK), e_r, in_group_r, used_r, gm_ref,
                      *rows)


def _moe_enter(b, gm_ref, st_ref, wgu_hbm, wd_hbm, wgu_f, wd_f, wgu_bf, wd_bf, sem):
    def weight_copies(expert, sl):
        return (pltpu.make_async_copy(wgu_hbm.at[expert], wgu_f.at[sl], sem.at[0, sl]),
                pltpu.make_async_copy(wd_hbm.at[expert], wd_f.at[sl], sem.at[1, sl]))

    def next_group(e):
        return lax.while_loop(
            lambda k: jnp.logical_and(k < N_EXPERTS,
                                      gm_ref[_GM_BLOCKS, jnp.minimum(k, N_EXPERTS - 1)] == 0),
            lambda k: k + 1, e)

    @pl.when(b == 0)
    def _():
        e0 = next_group(0)
        st_ref[_ST_EXPERT] = e0
        st_ref[_ST_SLOT] = 1
        for cp in weight_copies(e0, 0):
            cp.start()

    e_prev = st_ref[_ST_EXPERT]
    past = b >= gm_ref[_GM_FIRST, e_prev] + gm_ref[_GM_BLOCKS, e_prev]
    e = jnp.minimum(jnp.where(past, next_group(e_prev + 1), e_prev), N_EXPERTS - 1)
    st_ref[_ST_EXPERT] = e
    in_group = b - gm_ref[_GM_FIRST, e]
    used = jnp.logical_and(in_group >= 0, in_group < gm_ref[_GM_BLOCKS, e])

    @pl.when(jnp.logical_and(used, in_group == 0))
    def _():
        slot = 1 - st_ref[_ST_SLOT]
        st_ref[_ST_SLOT] = slot
        nxt = next_group(e + 1)

        @pl.when(nxt < N_EXPERTS)
        def _():
            for cp in weight_copies(nxt, 1 - slot):
                cp.start(priority=1)

        for cp in weight_copies(e, slot):
            cp.wait()
        wgu_bf[...] = wgu_f[slot].astype(BF16)
        wd_bf[...] = wd_f[slot].astype(BF16)

    return e, in_group, used


def _moe_rows(rs, e, in_group, used, gm_ref, xs_ref, bgu_ref, bd_ref, o_ref, wgu_bf, wd_bf):
    n = rs.stop - rs.start

    def ffn():
        n_valid = gm_ref[_GM_COUNT, e] - in_group * MOE_BLOCK
        row = lax.broadcasted_iota(jnp.int32, (n, PACKED), 0)
        xs = _unpack_row(jnp.where(row < n_valid, xs_ref[rs, :], 0)).astype(BF16)
        gu = _dot(xs, wgu_bf[...]) + bgu_ref[e]
        gate = jnp.minimum(gu[:, :D_EXPERT], SWIGLU_LIMIT)
        up = jnp.clip(gu[:, D_EXPERT:], -SWIGLU_LIMIT, SWIGLU_LIMIT)
        act = (up + 1.0) * (gate * jax.nn.sigmoid(gate * SWIGLU_ALPHA))
        o_ref[rs, :] = _pack_row(_dot(act.astype(BF16), wd_bf[...]) + bd_ref[e])

    if used is True:
        ffn()
        return
    pl.when(used)(ffn)

    @pl.when(jnp.logical_not(used))
    def _():
        o_ref[rs, :] = jnp.zeros((n, PACKED), o_ref.dtype)


def _moe(group_table, xs, w_gu, b_gu, w_down, b_down):
    n_rows = xs.shape[0]
    step_rows = MOE_STEP_BLOCKS * MOE_BLOCK
    assert n_rows % step_rows == 0
    grid_spec = pltpu.PrefetchScalarGridSpec(
        num_scalar_prefetch=1,
        grid=(n_rows // step_rows,),
        in_specs=[
            pl.BlockSpec((step_rows, PACKED), lambda b, gm: (b, 0)),
            pl.BlockSpec(memory_space=pl.ANY),
            pl.BlockSpec(b_gu.shape, lambda b, gm: (0, 0, 0)),
            pl.BlockSpec(memory_space=pl.ANY),
            pl.BlockSpec(b_down.shape, lambda b, gm: (0, 0, 0)),
        ],
        out_specs=pl.BlockSpec((step_rows, PACKED), lambda b, gm: (b, 0)),
        scratch_shapes=[pltpu.VMEM((2, D_MODEL, 2 * D_EXPERT), F32),
                        pltpu.VMEM((2, D_EXPERT, D_MODEL), F32),
                        pltpu.VMEM((D_MODEL, 2 * D_EXPERT), BF16),
                        pltpu.VMEM((D_EXPERT, D_MODEL), BF16),
                        pltpu.SemaphoreType.DMA((2, 2)),
                        pltpu.SMEM((2,), jnp.int32)],
    )
    return pl.pallas_call(
        _moe_kernel,
        out_shape=jax.ShapeDtypeStruct((n_rows, PACKED), jnp.uint32),
        grid_spec=grid_spec,
        compiler_params=pltpu.CompilerParams(
            dimension_semantics=("arbitrary",), vmem_limit_bytes=VMEM_LIMIT),
        name="moe_experts",
    )(group_table, xs, w_gu, b_gu, w_down, b_down)


def _final_kernel(last_layer, x1_ref, y_ref, gate_ref, mod_ref, g_ref, o_ref):
    mod = mod_ref[...]
    gate = gate_ref[...]
    ffn = gate[:, 0:1] * _unpack_row(y_ref[0])
    for kk in range(1, TOP_K):
        ffn = ffn + gate[:, kk:kk + 1] * _unpack_row(y_ref[kk])
    x = x1_ref[...] + mod[5:6] * ffn
    o_ref[...] = _rms(x, g_ref[...]) if last_layer else x


def _final(x1, y_kt, gate, mod, norm_g, seq, last_layer):
    t = x1.shape[0]
    tm = ROW_TILE
    tiles_per_seq = seq // tm
    rows = lambda w: pl.BlockSpec((tm, w), lambda i: (i, 0))
    return pl.pallas_call(
        functools.partial(_final_kernel, last_layer),
        out_shape=jax.ShapeDtypeStruct((t, D_MODEL), F32),
        grid=(t // tm,),
        in_specs=[
            rows(D_MODEL), pl.BlockSpec((TOP_K, tm, PACKED), lambda i: (0, i, 0)), rows(LANES),
            pl.BlockSpec((None, 8, D_MODEL), lambda i: (i // tiles_per_seq, 0, 0)),
            pl.BlockSpec(norm_g.shape, lambda i: (0, 0)),
        ],
        out_specs=rows(D_MODEL),
        compiler_params=pltpu.CompilerParams(
            dimension_semantics=("arbitrary",), vmem_limit_bytes=VMEM_LIMIT),
        name="combine_final",
    )(x1, y_kt, gate, mod, norm_g)


def _prep_weights(w_in, w_uq, w_ukv):
    d = w_in.shape[0]
    splits = (Q_LORA, KV_LORA, QK_ROPE, CONV_WIDTH, CONV_WIDTH, CONV_WIDTH, D_MODEL, D_MODEL)
    offs = [0]
    for s in splits:
        offs.append(offs[-1] + s)
    part = lambda n: w_in[:, offs[n]:offs[n + 1]]
    z = lambda n: jnp.zeros((d, n), w_in.dtype)
    w_kpe = part(2)
    kpe_slab = jnp.concatenate([z(QK_NOPE), w_kpe, z(HEAD_PAD - QK_HEAD)], axis=1)
    w_lat = jnp.concatenate([part(0), part(1), kpe_slab], axis=1).astype(BF16)
    w_conv = w_in[:, offs[3]:offs[6]].astype(BF16)
    w_gate = w_in[:, offs[6]:offs[8]].astype(BF16)

    wq = w_uq.reshape(Q_LORA, N_HEADS, QK_HEAD)
    zq = lambda n: jnp.zeros((Q_LORA, N_HEADS, n), w_uq.dtype)
    wq2 = jnp.concatenate([wq, zq(HEAD_PAD - QK_HEAD)], axis=-1)
    wq2 = wq2.reshape(Q_LORA, N_HEADS * HEAD_PAD).astype(BF16)

    wkv = w_ukv.reshape(KV_LORA, N_HEADS, QK_NOPE + V_HEAD)
    wk2 = jnp.concatenate([wkv[..., :QK_NOPE],
                           jnp.zeros((KV_LORA, N_HEADS, HEAD_PAD - QK_NOPE), w_ukv.dtype)], axis=-1)
    wk2 = wk2.reshape(KV_LORA, N_HEADS * HEAD_PAD).astype(BF16)
    wv = jnp.concatenate([wkv[..., QK_NOPE:],
                          jnp.zeros((KV_LORA, N_HEADS, HEAD_PAD - V_HEAD), w_ukv.dtype)], axis=-1)
    wv = wv.reshape(KV_LORA, N_HEADS * HEAD_PAD).astype(BF16)
    return w_lat, w_conv, w_gate, wq2, wk2, wv


def _rope_freqs():
    inv_freq = 1.0 / (ROPE_THETA ** (jnp.arange(0, QK_ROPE, 2, dtype=F32) / QK_ROPE))
    return inv_freq.reshape(QK_ROPE // 2, 1)


def _dest_kernel(gm_ref, idx_ref, rank_ref, o_ref):
    idx = idx_ref[...]
    dest = rank_ref[...]
    for e in range(N_EXPERTS):
        dest = dest + jnp.where(idx == e, gm_ref[_GM_FIRST, e] * MOE_BLOCK, 0)
    o_ref[...] = dest


def _route(top_idx, rank, counts, n_tokens):
    blocks = (counts + MOE_BLOCK - 1) // MOE_BLOCK
    first_block = jnp.cumsum(blocks) - blocks
    table = jnp.stack([counts, first_block, blocks]).astype(jnp.int32)
    whole = pl.BlockSpec(top_idx.shape, lambda i, gm: (0, 0))
    dest = pl.pallas_call(
        _dest_kernel,
        out_shape=jax.ShapeDtypeStruct(rank.shape, jnp.int32),
        grid_spec=pltpu.PrefetchScalarGridSpec(
            num_scalar_prefetch=1, grid=(1,), in_specs=[whole, whole], out_specs=whole),
        name="row_destinations",
    )(table, top_idx, rank)[:TOP_K]
    n_rows = n_tokens * TOP_K + N_EXPERTS * MOE_BLOCK
    return dest, table, n_rows


SC_CORES = 2
SC_SUBCORES = 16
SC_WORKERS = SC_CORES * SC_SUBCORES
SC_CHUNK = 64
SC_GATHER_RING = 3

def _sc_mesh():
    return plsc.VectorSubcoreMesh(core_axis_name="c", subcore_axis_name="s")


def _sc_worker():
    return lax.axis_index("s") * SC_CORES + lax.axis_index("c")


def _dispatch(h2, dest, n_rows):
    t, d = h2.shape
    per_w = t // SC_WORKERS
    n_chunks = per_w // SC_CHUNK
    assert per_w % (2 * SC_CHUNK) == 0
    idx = dest.reshape(TOP_K, SC_WORKERS, n_chunks, SC_CHUNK).transpose(1, 0, 2, 3)
    idx = idx.reshape(SC_WORKERS, TOP_K * n_chunks, SC_CHUNK)

    @functools.partial(
        pl.kernel, mesh=_sc_mesh(),
        out_type=jax.ShapeDtypeStruct((n_rows, d), h2.dtype),
        scratch_types=[pltpu.VMEM((TOP_K * n_chunks, SC_CHUNK), jnp.int32),
                       pltpu.VMEM((2, SC_CHUNK, d), h2.dtype),
                       pltpu.SemaphoreType.DMA((2,)),
                       pltpu.SemaphoreType.DMA((2,))],
        name="moe_dispatch")
    def run(h2_hbm, idx_hbm, xs_hbm, idx_v, rows_v, rsem, ssem):
        w = _sc_worker()
        pltpu.sync_copy(idx_hbm.at[w], idx_v)

        def read(g, b):
            src = h2_hbm.at[pl.ds(w * per_w + g * SC_CHUNK, SC_CHUNK)]
            return pltpu.make_async_copy(src, rows_v.at[b], rsem.at[b])

        def scatter(g, kk, b):
            dst = xs_hbm.at[idx_v.at[kk * n_chunks + g]]
            return pltpu.make_async_copy(rows_v.at[b], dst, ssem.at[b])

        read(0, 0).start()

        @pl.loop(0, n_chunks, step=2)
        def _(g0):
            for b in range(2):
                g = g0 + b
                read(g, b).wait()

                @pl.when(g + 1 < n_chunks)
                def _():
                    read(g + 1, 1 - b).start()

                for kk in range(TOP_K):
                    scatter(g, kk, b).start()
                for kk in range(TOP_K):
                    scatter(g, kk, b).wait()

    return run(h2, idx)


def _undispatch(ys, dest):
    t = dest.shape[1]
    d = ys.shape[1]
    n_out = t * TOP_K
    per_w = n_out // SC_WORKERS
    n_chunks = per_w // SC_CHUNK
    idx = dest.reshape(SC_WORKERS, n_chunks, SC_CHUNK)
    ring = SC_GATHER_RING

    @functools.partial(
        pl.kernel, mesh=_sc_mesh(),
        out_type=jax.ShapeDtypeStruct((n_out, d), ys.dtype),
        scratch_types=[pltpu.VMEM((n_chunks, SC_CHUNK), jnp.int32),
                       pltpu.VMEM((ring, SC_CHUNK, d), ys.dtype),
                       pltpu.SemaphoreType.DMA((ring,)),
                       pltpu.SemaphoreType.DMA((ring,))],
        name="moe_undispatch")
    def run(ys_hbm, idx_hbm, out_hbm, idx_v, rows_v, gsem, wsem):
        w = _sc_worker()
        pltpu.sync_copy(idx_hbm.at[w], idx_v)

        def gather(g):
            b = g % ring
            return pltpu.make_async_copy(ys_hbm.at[idx_v.at[g]], rows_v.at[b], gsem.at[b])

        def write(g):
            b = g % ring
            dst = out_hbm.at[pl.ds(w * per_w + g * SC_CHUNK, SC_CHUNK)]
            return pltpu.make_async_copy(rows_v.at[b], dst, wsem.at[b])

        for g in range(min(ring - 1, n_chunks)):
            gather(g).start()
        for g in range(n_chunks):
            gather(g).wait()
            ahead = g + ring - 1
            if ahead < n_chunks:
                if g >= 1:
                    write(g - 1).wait()
                gather(ahead).start()
            write(g).start()
        for g in range(max(n_chunks - ring, 0), n_chunks):
            write(g).wait()

    return run(ys, idx).reshape(TOP_K, t, d)


def kernel(x, c, positions, w_ada, b_ada, norm_mix_g, w_in, q_norm_g, w_uq, kv_norm_g, w_ukv,
           w_up_attn, conv_w, w_up_conv, w_o, norm_ffn_g, router_w, router_b, w_gu, b_gu,
           w_down, b_down, norm_final_g):
    batch, seq, d = x.shape
    t = batch * seq
    depth = w_ada.shape[0]
    x2 = x.reshape(t, d)
    pos = positions.astype(F32).reshape(t // ROW_TILE, 1, ROW_TILE)
    freqs = _rope_freqs()
    c_pad = jnp.zeros((8, d), F32).at[:batch].set(c)

    for l in range(depth):
        ada = _ada(c_pad, w_ada[l], b_ada[l].reshape(1, -1))
        mod = ada[:batch].reshape(batch, 6, d)
        mod = jnp.concatenate([mod, jnp.zeros((batch, 2, d), F32)], axis=1)

        w_lat, w_conv, w_gate, wq2, wk2, wv = _prep_weights(w_in[l], w_uq[l], w_ukv[l])
        q, k, v, sga, gc = _pre(x2, mod, norm_mix_g[l].reshape(1, d), w_lat, w_conv, w_gate,
                                q_norm_g[l].reshape(1, -1), wq2,
                                kv_norm_g[l].reshape(1, -1), wk2, wv, pos, freqs, conv_w[l],
                                w_up_conv[l].astype(BF16), seq)
        attn = _attention(q, k, v, batch, seq)

        rw_pad = jnp.concatenate([router_w[l], jnp.zeros((d, LANES - N_EXPERTS), F32)], axis=1)
        rb_pad = jnp.concatenate([router_b[l], jnp.full((LANES - N_EXPERTS,), NEG_BIG, F32)])
        rw_hi = rw_pad.astype(BF16)
        rw_lo = (rw_pad - rw_hi.astype(F32)).astype(BF16)
        x1, h2, idx_pad, gate_pad, rank_pad, counts = _post(
            attn, sga, gc, x2, mod, w_up_attn[l].astype(BF16), w_o[l].astype(BF16),
            norm_ffn_g[l].reshape(1, d), rw_hi, rw_lo, rb_pad.reshape(1, LANES), seq)

        dest, group_table, n_rows = _route(
            idx_pad, rank_pad, counts[0, :N_EXPERTS], t)
        xs = _dispatch(h2, dest, n_rows)
        ys = _moe(group_table, xs, w_gu[l], b_gu[l].reshape(N_EXPERTS, 1, -1),
                  w_down[l], b_down[l].reshape(N_EXPERTS, 1, -1))
        y_kt = _undispatch(ys, dest)
        x2 = _final(x1, y_kt, gate_pad, mod, norm_final_g.reshape(1, d), seq, l == depth - 1)

    return x2.reshape(batch, seq, d)
```

```python
import functools
import math

import jax
import jax.numpy as jnp
from jax import lax
from jax.experimental import pallas as pl
from jax.experimental.pallas import tpu as pltpu
from jax.experimental.pallas import tpu_sc as plsc

D_MODEL = 1024
CHUNK = 64
N_HEADS = 8
Q_LORA = 256
KV_LORA = 128
QK_NOPE = 64
QK_ROPE = 32
V_HEAD = 64
QK_HEAD = QK_NOPE + QK_ROPE
ROPE_THETA = 10000.0
CONV_WIDTH = 512
CONV_K = 3
N_EXPERTS = 32
TOP_K = 4
D_EXPERT = 1024
SWIGLU_LIMIT = 7.0
SWIGLU_ALPHA = 1.702
MOE_BLOCK = 256
RMS_EPS = 1e-6

LANES = 128
HEAD_PAD = 128
NEG_BIG = -1e30
VMEM_LIMIT = 56 * 1024 * 1024

F32 = jnp.float32
BF16 = jnp.bfloat16

Q_PRESCALE = (QK_HEAD ** -0.5) * math.log2(math.e)

ROW_TILE = 1024
MOE_STEP_BLOCKS = 4
POST_TILE = 1024
POST_SUB = 512
ATT_BLOCK = 512
ATT_WIDE = 2


def _rms(x, g):
    ms = jnp.mean(x * x, axis=-1, keepdims=True)
    return x * lax.rsqrt(ms + RMS_EPS) * g


def _dot(a, b):
    return jnp.dot(a, b, preferred_element_type=F32)


PACKED = D_MODEL // 2


def _pack_row(x):
    return pltpu.pack_elementwise([x[:, :PACKED], x[:, PACKED:]], packed_dtype=BF16)


def _unpack_row(w):
    half = lambda i: pltpu.unpack_elementwise(w, index=i, packed_dtype=BF16, unpacked_dtype=F32)
    return jnp.concatenate([half(0), half(1)], axis=-1)


def _ada_kernel(c_ref, w_ref, b_ref, o_ref):
    c = c_ref[...]
    ca = (c * jax.nn.sigmoid(c)).astype(BF16)
    o_ref[...] = _dot(ca, w_ref[...].astype(BF16)) + b_ref[...]


def _ada(c_pad, w_ada, b_ada):
    n = w_ada.shape[1]
    tn = 1024
    return pl.pallas_call(
        _ada_kernel,
        out_shape=jax.ShapeDtypeStruct((c_pad.shape[0], n), F32),
        grid=(n // tn,),
        in_specs=[
            pl.BlockSpec(c_pad.shape, lambda j: (0, 0)),
            pl.BlockSpec((D_MODEL, tn), lambda j: (0, j)),
            pl.BlockSpec((1, tn), lambda j: (0, j)),
        ],
        out_specs=pl.BlockSpec((c_pad.shape[0], tn), lambda j: (0, j)),
        compiler_params=pltpu.CompilerParams(
            dimension_semantics=("arbitrary",), vmem_limit_bytes=VMEM_LIMIT),
        name="ada",
    )(c_pad, w_ada, b_ada)


_C_QLAT = 0
_C_KVLAT = _C_QLAT + Q_LORA
_C_KPE = _C_KVLAT + KV_LORA
_C_END = _C_KPE + HEAD_PAD


def _pre_kernel(tiles_per_seq, x_ref, mod_ref, g_ref, wlat_ref, wconv_ref, wgate_ref, qg_ref,
                wq_ref, kvg_ref, wk_ref, wv_ref, pos_ref, freq_ref, cw_ref, wuc_ref,
                q_ref, k_ref, v_ref, sga_ref, gc_ref, carry_ref):
    i = pl.program_id(0)
    tm = x_ref.shape[0]
    mod = mod_ref[...]
    h = _rms(x_ref[...], g_ref[...]) * (1.0 + mod[1:2]) + mod[0:1]
    hb = h.astype(BF16)

    ang = freq_ref[...] * pos_ref[...]
    cos_t, sin_t = jnp.cos(ang), jnp.sin(ang)
    ones_t = jnp.ones((QK_NOPE, tm), F32)
    zeros_t = jnp.zeros((QK_NOPE, tm), F32)
    pad_t = jnp.zeros((HEAD_PAD - QK_HEAD, tm), F32)
    cosf = jnp.concatenate([ones_t, cos_t, cos_t, pad_t], axis=0).T
    sinf = jnp.concatenate([zeros_t, -sin_t, sin_t, pad_t], axis=0).T

    first_half = lax.broadcasted_iota(jnp.int32, (tm, HEAD_PAD), 1) < QK_NOPE + QK_ROPE // 2

    def rope(slab):
        swapped = jnp.where(first_half, pltpu.roll(slab, HEAD_PAD - QK_ROPE // 2, 1),
                            pltpu.roll(slab, QK_ROPE // 2, 1))
        return slab * cosf + swapped * sinf

    small = _dot(hb, wlat_ref[...])
    q_lat = small[:, _C_QLAT:_C_KVLAT]
    kv_lat = small[:, _C_KVLAT:_C_KPE]
    kpe = rope(small[:, _C_KPE:_C_END])
    qn = _rms(q_lat, qg_ref[...]).astype(BF16)
    q = _dot(qn, wq_ref[...])
    q = jnp.concatenate([rope(q[:, hd * HEAD_PAD:(hd + 1) * HEAD_PAD]) for hd in range(N_HEADS)],
                        axis=-1)
    q_ref[...] = (q * Q_PRESCALE).astype(BF16)
    kvn = _rms(kv_lat, kvg_ref[...]).astype(BF16)
    k = _dot(kvn, wk_ref[...]) + jnp.concatenate([kpe] * N_HEADS, axis=-1)
    k_ref[...] = k.astype(BF16)
    lane = lax.broadcasted_iota(jnp.int32, (tm, N_HEADS * HEAD_PAD), 1)
    ones_col = jnp.where(lane % HEAD_PAD == V_HEAD, 1.0, 0.0)
    v_ref[...] = (_dot(kvn, wv_ref[...]) + ones_col).astype(BF16)

    ucb = _dot(hb, wconv_ref[...])
    cu = ucb[:, 0:CONV_WIDTH] * ucb[:, CONV_WIDTH:2 * CONV_WIDTH]
    b_gate = ucb[:, 2 * CONV_WIDTH:3 * CONV_WIDTH]

    @pl.when(i % tiles_per_seq == 0)
    def _():
        carry_ref[...] = jnp.zeros_like(carry_ref)

    prev = carry_ref[...]
    row = lax.broadcasted_iota(jnp.int32, cu.shape, 0)
    cu1 = jnp.where(row == 0, prev[7:8], pltpu.roll(cu, 1, 0))
    cu2 = jnp.where(row == 0, prev[6:7], jnp.where(row == 1, prev[7:8], pltpu.roll(cu, 2, 0)))
    cw = cw_ref[...]
    z = cw[2:3] * cu + cw[1:2] * cu1 + cw[0:1] * cu2
    carry_ref[...] = cu[tm - 8:tm]
    c_branch = _dot((b_gate * z).astype(BF16), wuc_ref[...])

    gates = _dot(hb, wgate_ref[...])
    sga_ref[...] = jax.nn.sigmoid(gates[:, 0:D_MODEL]).astype(BF16)
    gc_ref[...] = (jax.nn.sigmoid(gates[:, D_MODEL:]) * c_branch).astype(BF16)


def _pre(x2, mod, norm_g, w_lat, w_conv, w_gate, q_norm_g, wq2, kv_norm_g, wk2, wv, pos,
         freqs, conv_w, w_up_conv, seq):
    t = x2.shape[0]
    tm = ROW_TILE
    tiles_per_seq = seq // tm
    full = lambda a: pl.BlockSpec(a.shape, lambda i: (0,) * a.ndim)
    rows = lambda w: pl.BlockSpec((tm, w), lambda i: (i, 0))
    outs = [jax.ShapeDtypeStruct((t, N_HEADS * HEAD_PAD), BF16),
            jax.ShapeDtypeStruct((t, N_HEADS * HEAD_PAD), BF16),
            jax.ShapeDtypeStruct((t, N_HEADS * HEAD_PAD), BF16),
            jax.ShapeDtypeStruct((t, D_MODEL), BF16),
            jax.ShapeDtypeStruct((t, D_MODEL), BF16)]
    return pl.pallas_call(
        functools.partial(_pre_kernel, tiles_per_seq),
        out_shape=outs,
        grid=(t // tm,),
        in_specs=[
            rows(D_MODEL),
            pl.BlockSpec((None, 8, D_MODEL), lambda i: (i // tiles_per_seq, 0, 0)),
            full(norm_g), full(w_lat), full(w_conv), full(w_gate), full(q_norm_g), full(wq2),
            full(kv_norm_g), full(wk2), full(wv),
            pl.BlockSpec((None, 1, tm), lambda i: (i, 0, 0)), full(freqs), full(conv_w),
            full(w_up_conv),
        ],
        out_specs=[rows(N_HEADS * HEAD_PAD), rows(N_HEADS * HEAD_PAD), rows(N_HEADS * HEAD_PAD),
                   rows(D_MODEL), rows(D_MODEL)],
        scratch_shapes=[pltpu.VMEM((8, CONV_WIDTH), F32)],
        compiler_params=pltpu.CompilerParams(
            dimension_semantics=("arbitrary",), vmem_limit_bytes=VMEM_LIMIT),
        name="pre_mixer",
    )(x2, mod, norm_g, w_lat, w_conv, w_gate, q_norm_g, wq2, kv_norm_g, wk2, wv, pos,
      freqs, conv_w, w_up_conv)


def _attn_kernel(q_ref, k_ref, v_ref, o_ref, m_ref, acc_ref):
    i = pl.program_id(1)
    tq = q_ref.shape[0]

    m_ref[...] = jnp.full_like(m_ref, NEG_BIG)
    acc_ref[...] = jnp.zeros_like(acc_ref)

    def step(k0, tk, masked):
        if masked:
            rq = (lax.broadcasted_iota(jnp.int32, (tq, tk), 0) + (tk - tq)) // CHUNK
            ck = lax.broadcasted_iota(jnp.int32, (tq, tk), 1) // CHUNK
            allowed = ck <= rq
        for hd in range(N_HEADS):
            hs = slice(hd * HEAD_PAD, (hd + 1) * HEAD_PAD)
            s = lax.dot_general(q_ref[:, hs], k_ref[pl.ds(k0, tk), hs],
                                (((1,), (1,)), ((), ())), preferred_element_type=F32)
            if masked:
                s = jnp.where(allowed, s, NEG_BIG)
            m_old = m_ref[hd]
            s_max = s[:, 0:LANES]
            for c in range(1, tk // LANES):
                s_max = jnp.maximum(s_max, s[:, c * LANES:(c + 1) * LANES])
            m_new = jnp.maximum(m_old, jnp.max(s_max, axis=-1, keepdims=True))
            alpha = jnp.exp2(m_old - m_new)
            p = jnp.concatenate(
                [jnp.exp2(s[:, c * LANES:(c + 1) * LANES] - m_new).astype(BF16)
                 for c in range(tk // LANES)], axis=-1)
            acc_ref[hd] = alpha * acc_ref[hd] + _dot(p, v_ref[pl.ds(k0, tk), hs])
            m_ref[hd] = m_new

    wide = ATT_WIDE * tq

    def body(j, carry):
        step(pl.multiple_of(j * wide, wide), wide, False)
        return carry

    lax.fori_loop(0, i // ATT_WIDE, body, 0)

    for r in range(ATT_WIDE):
        @pl.when(i % ATT_WIDE == r)
        def _():
            step(pl.multiple_of((i - r) * tq, tq), (r + 1) * tq, True)

    for hp in range(N_HEADS // 2):
        pair = []
        for hd in (2 * hp, 2 * hp + 1):
            acc = acc_ref[hd]
            pair.append(acc[:, 0:V_HEAD] / acc[:, V_HEAD:V_HEAD + 1])
        o_ref[:, hp * LANES:(hp + 1) * LANES] = jnp.concatenate(pair, axis=-1).astype(BF16)


def _attention(q, k, v, batch, seq):
    tq = ATT_BLOCK
    nq = seq // tq
    return pl.pallas_call(
        _attn_kernel,
        out_shape=jax.ShapeDtypeStruct((batch * seq, N_HEADS * V_HEAD), BF16),
        grid=(batch, nq),
        in_specs=[
            pl.BlockSpec((tq, N_HEADS * HEAD_PAD), lambda b, i: (b * nq + i, 0)),
            pl.BlockSpec((seq, N_HEADS * HEAD_PAD), lambda b, i: (b, 0)),
            pl.BlockSpec((seq, N_HEADS * HEAD_PAD), lambda b, i: (b, 0)),
        ],
        out_specs=pl.BlockSpec((tq, N_HEADS * V_HEAD), lambda b, i: (b * nq + i, 0)),
        scratch_shapes=[pltpu.VMEM((N_HEADS, tq, LANES), F32),
                        pltpu.VMEM((N_HEADS, tq, LANES), F32)],
        compiler_params=pltpu.CompilerParams(
            dimension_semantics=("arbitrary", "arbitrary"), vmem_limit_bytes=VMEM_LIMIT),
        name="attention",
    )(q, k, v)


def _post_kernel(attn_ref, sga_ref, gc_ref, x_ref, mod_ref, wua_ref, wo_ref, g_ref, rwh_ref,
                 rwl_ref, rb_ref, x1_ref, h2_ref, idx_ref, gate_ref, rank_ref, cnt_out_ref, cnt_ref,
                 lg_ref):
    i = pl.program_id(0)

    @pl.when(i == 0)
    def _():
        cnt_ref[...] = jnp.zeros_like(cnt_ref)
        lg_ref[...] = jnp.zeros_like(lg_ref)

    prev_logits = lg_ref[(i + 1) % 2]
    counts = cnt_ref[...]
    routed = counts
    for r0 in range(0, x_ref.shape[0], POST_SUB):
        rs = slice(r0, r0 + POST_SUB)
        lg_ref[i % 2, rs, :] = _post_mix(rs, attn_ref, sga_ref, gc_ref, x_ref, mod_ref, wua_ref,
                                         wo_ref, g_ref, rwh_ref, rwl_ref, rb_ref, x1_ref, h2_ref)
        routed = _post_route(rs, prev_logits[rs, :], routed, idx_ref, gate_ref, rank_ref)
    counts = jnp.where(i > 0, routed, counts)
    cnt_ref[...] = counts
    cnt_out_ref[...] = counts.astype(jnp.int32)


def _post_mix(rs, attn_ref, sga_ref, gc_ref, x_ref, mod_ref, wua_ref, wo_ref, g_ref,
              rwh_ref, rwl_ref, rb_ref, x1_ref, h2_ref):
    mod = mod_ref[...]
    a_branch = _dot(attn_ref[rs, :], wua_ref[...])
    merged = sga_ref[rs, :].astype(F32) * a_branch + gc_ref[rs, :].astype(F32)
    mix = _dot(merged.astype(BF16), wo_ref[...])
    x1 = x_ref[rs, :] + mod[2:3] * mix
    x1_ref[rs, :] = x1
    h2 = _rms(x1, g_ref[...]) * (1.0 + mod[4:5]) + mod[3:4]
    h2_ref[rs, :] = _pack_row(h2)

    h_hi = h2.astype(BF16)
    h_lo = (h2 - h_hi.astype(F32)).astype(BF16)
    return (_dot(h_hi, rwh_ref[...]) + _dot(h_lo, rwh_ref[...]) + _dot(h_hi, rwl_ref[...])
            + rb_ref[...])


def _post_route(rs, logits, counts, idx_ref, gate_ref, rank_ref):
    lane = lax.broadcasted_iota(jnp.int32, logits.shape, 1)
    work = logits
    vals, idxs = [], []
    for _ in range(TOP_K):
        mk = jnp.max(work, axis=-1, keepdims=True)
        ik = jnp.min(jnp.where(work == mk, lane, LANES), axis=-1, keepdims=True)
        vals.append(mk)
        idxs.append(ik)
        work = jnp.where(lane == ik, -jnp.inf, work)
    es = [jnp.exp(vk - vals[0]) for vk in vals]
    denom = es[0] + es[1] + es[2] + es[3]
    tm = logits.shape[0]
    chosen = jnp.zeros(logits.shape, F32)
    for kk in range(TOP_K):
        chosen = chosen + jnp.where(lane == idxs[kk], 1.0, 0.0)
    r_i = lax.broadcasted_iota(jnp.int32, (tm, tm), 0)
    c_i = lax.broadcasted_iota(jnp.int32, (tm, tm), 1)
    earlier = jnp.where(c_i < r_i, 1.0, 0.0).astype(BF16)
    before = _dot(earlier, chosen.astype(BF16)) + counts[0:1]

    idx_out = jnp.zeros(logits.shape, F32)
    gate_out = jnp.zeros(logits.shape, F32)
    rank_out = jnp.zeros(logits.shape, F32)
    for kk in range(TOP_K):
        rank_k = jnp.sum(jnp.where(lane == idxs[kk], before, 0.0), axis=-1, keepdims=True)
        idx_out = jnp.where(lane == kk, idxs[kk].astype(F32), idx_out)
        gate_out = jnp.where(lane == kk, es[kk] / denom, gate_out)
        rank_out = jnp.where(lane == kk, rank_k, rank_out)
    gate_ref[rs, :] = gate_out
    idx_ref[:, rs] = idx_out.T[0:8].astype(jnp.int32)
    rank_ref[:, rs] = rank_out.T[0:8].astype(jnp.int32)
    return counts + jnp.sum(chosen, axis=0, keepdims=True)


def _post(attn, sga, gc, x2, mod, wua, wo, norm_g, rw_hi, rw_lo, rb_pad, seq):
    t = x2.shape[0]
    tm = POST_TILE
    tiles_per_seq = seq // tm
    n_tiles = t // tm
    full = lambda a: pl.BlockSpec(a.shape, lambda i: (0,) * a.ndim)
    mix_tile = lambda i: jnp.minimum(i, n_tiles - 1)
    route_tile = lambda i: jnp.maximum(i - 1, 0)
    rows = lambda w: pl.BlockSpec((tm, w), lambda i: (mix_tile(i), 0))
    outs = [jax.ShapeDtypeStruct((t, D_MODEL), F32),
            jax.ShapeDtypeStruct((t, PACKED), jnp.uint32),
            jax.ShapeDtypeStruct((8, t), jnp.int32),
            jax.ShapeDtypeStruct((t, LANES), F32),
            jax.ShapeDtypeStruct((8, t), jnp.int32),
            jax.ShapeDtypeStruct((8, LANES), jnp.int32)]
    slots = pl.BlockSpec((8, tm), lambda i: (0, route_tile(i)))
    return pl.pallas_call(
        _post_kernel,
        out_shape=outs,
        grid=(n_tiles + 1,),
        in_specs=[
            rows(N_HEADS * V_HEAD), rows(D_MODEL), rows(D_MODEL), rows(D_MODEL),
            pl.BlockSpec((None, 8, D_MODEL), lambda i: (mix_tile(i) // tiles_per_seq, 0, 0)),
            full(wua), full(wo), full(norm_g), full(rw_hi), full(rw_lo), full(rb_pad),
        ],
        out_specs=[rows(D_MODEL), rows(PACKED), slots,
                   pl.BlockSpec((tm, LANES), lambda i: (route_tile(i), 0)), slots,
                   pl.BlockSpec((8, LANES), lambda i: (0, 0))],
        scratch_shapes=[pltpu.VMEM((8, LANES), F32), pltpu.VMEM((2, tm, LANES), F32)],
        compiler_params=pltpu.CompilerParams(
            dimension_semantics=("arbitrary",), vmem_limit_bytes=VMEM_LIMIT),
        name="post_mixer",
    )(attn, sga, gc, x2, mod, wua, wo, norm_g, rw_hi, rw_lo, rb_pad)


_GM_COUNT, _GM_FIRST, _GM_BLOCKS = range(3)
_ST_EXPERT, _ST_SLOT = range(2)


def _moe_kernel(gm_ref, xs_ref, wgu_hbm, bgu_ref, wd_hbm, bd_ref, o_ref,
                wgu_f, wd_f, wgu_bf, wd_bf, sem, st_ref):
    weights = (wgu_hbm, wd_hbm, wgu_f, wd_f, wgu_bf, wd_bf, sem)
    rows = (xs_ref, bgu_ref, bd_ref, o_ref, wgu_bf, wd_bf)
    b0 = pl.program_id(0) * MOE_STEP_BLOCKS
    e, in_group, used = _moe_enter(b0, gm_ref, st_ref, *weights)
    together = jnp.logical_and(used, in_group + MOE_STEP_BLOCKS <= gm_ref[_GM_BLOCKS, e])

    @pl.when(together)
    def _():
        _moe_rows(slice(0, MOE_STEP_BLOCKS * MOE_BLOCK), e, in_group, True, gm_ref, *rows)

    @pl.when(jnp.logical_not(together))
    def _():
        _moe_rows(slice(0, MOE_BLOCK), e, in_group, used, gm_ref, *rows)
        for r in range(1, MOE_STEP_BLOCKS):
            e_r, in_group_r, used_r = _moe_enter(b0 + r, gm_ref, st_ref, *weights)
            _moe_rows(slice(r * MOE_BLOCK, (r + 1) * MOE_BLOCK), e_r, in_group_r, used_r, gm_ref,
                      *rows)


def _moe_enter(b, gm_ref, st_ref, wgu_hbm, wd_hbm, wgu_f, wd_f, wgu_bf, wd_bf, sem):
    def weight_copies(expert, sl):
        return (pltpu.make_async_copy(wgu_hbm.at[expert], wgu_f.at[sl], sem.at[0, sl]),
                pltpu.make_async_copy(wd_hbm.at[expert], wd_f.at[sl], sem.at[1, sl]))

    def next_group(e):
        return lax.while_loop(
            lambda k: jnp.logical_and(k < N_EXPERTS,
                                      gm_ref[_GM_BLOCKS, jnp.minimum(k, N_EXPERTS - 1)] == 0),
            lambda k: k + 1, e)

    @pl.when(b == 0)
    def _():
        e0 = next_group(0)
        st_ref[_ST_EXPERT] = e0
        st_ref[_ST_SLOT] = 1
        for cp in weight_copies(e0, 0):
            cp.start()

    e_prev = st_ref[_ST_EXPERT]
    past = b >= gm_ref[_GM_FIRST, e_prev] + gm_ref[_GM_BLOCKS, e_prev]
    e = jnp.minimum(jnp.where(past, next_group(e_prev + 1), e_prev), N_EXPERTS - 1)
    st_ref[_ST_EXPERT] = e
    in_group = b - gm_ref[_GM_FIRST, e]
    used = jnp.logical_and(in_group >= 0, in_group < gm_ref[_GM_BLOCKS, e])

    @pl.when(jnp.logical_and(used, in_group == 0))
    def _():
        slot = 1 - st_ref[_ST_SLOT]
        st_ref[_ST_SLOT] = slot
        nxt = next_group(e + 1)

        @pl.when(nxt < N_EXPERTS)
        def _():
            for cp in weight_copies(nxt, 1 - slot):
                cp.start(priority=1)

        for cp in weight_copies(e, slot):
            cp.wait()
        wgu_bf[...] = wgu_f[slot].astype(BF16)
        wd_bf[...] = wd_f[slot].astype(BF16)

    return e, in_group, used


def _moe_rows(rs, e, in_group, used, gm_ref, xs_ref, bgu_ref, bd_ref, o_ref, wgu_bf, wd_bf):
    n = rs.stop - rs.start

    def ffn():
        n_valid = gm_ref[_GM_COUNT, e] - in_group * MOE_BLOCK
        row = lax.broadcasted_iota(jnp.int32, (n, PACKED), 0)
        xs = _unpack_row(jnp.where(row < n_valid, xs_ref[rs, :], 0)).astype(BF16)
        gu = _dot(xs, wgu_bf[...]) + bgu_ref[e]
        gate = jnp.minimum(gu[:, :D_EXPERT], SWIGLU_LIMIT)
        up = jnp.clip(gu[:, D_EXPERT:], -SWIGLU_LIMIT, SWIGLU_LIMIT)
        act = (up + 1.0) * (gate * jax.nn.sigmoid(gate * SWIGLU_ALPHA))
        o_ref[rs, :] = _pack_row(_dot(act.astype(BF16), wd_bf[...]) + bd_ref[e])

    if used is True:
        ffn()
        return
    pl.when(used)(ffn)

    @pl.when(jnp.logical_not(used))
    def _():
        o_ref[rs, :] = jnp.zeros((n, PACKED), o_ref.dtype)


def _moe(group_table, xs, w_gu, b_gu, w_down, b_down):
    n_rows = xs.shape[0]
    step_rows = MOE_STEP_BLOCKS * MOE_BLOCK
    assert n_rows % step_rows == 0
    grid_spec = pltpu.PrefetchScalarGridSpec(
        num_scalar_prefetch=1,
        grid=(n_rows // step_rows,),
        in_specs=[
            pl.BlockSpec((step_rows, PACKED), lambda b, gm: (b, 0)),
            pl.BlockSpec(memory_space=pl.ANY),
            pl.BlockSpec(b_gu.shape, lambda b, gm: (0, 0, 0)),
            pl.BlockSpec(memory_space=pl.ANY),
            pl.BlockSpec(b_down.shape, lambda b, gm: (0, 0, 0)),
        ],
        out_specs=pl.BlockSpec((step_rows, PACKED), lambda b, gm: (b, 0)),
        scratch_shapes=[pltpu.VMEM((2, D_MODEL, 2 * D_EXPERT), F32),
                        pltpu.VMEM((2, D_EXPERT, D_MODEL), F32),
                        pltpu.VMEM((D_MODEL, 2 * D_EXPERT), BF16),
                        pltpu.VMEM((D_EXPERT, D_MODEL), BF16),
                        pltpu.SemaphoreType.DMA((2, 2)),
                        pltpu.SMEM((2,), jnp.int32)],
    )
    return pl.pallas_call(
        _moe_kernel,
        out_shape=jax.ShapeDtypeStruct((n_rows, PACKED), jnp.uint32),
        grid_spec=grid_spec,
        compiler_params=pltpu.CompilerParams(
            dimension_semantics=("arbitrary",), vmem_limit_bytes=VMEM_LIMIT),
        name="moe_experts",
    )(group_table, xs, w_gu, b_gu, w_down, b_down)


def _final_kernel(last_layer, x1_ref, y_ref, gate_ref, mod_ref, g_ref, o_ref):
    mod = mod_ref[...]
    gate = gate_ref[...]
    ffn = gate[:, 0:1] * _unpack_row(y_ref[0])
    for kk in range(1, TOP_K):
        ffn = ffn + gate[:, kk:kk + 1] * _unpack_row(y_ref[kk])
    x = x1_ref[...] + mod[5:6] * ffn
    o_ref[...] = _rms(x, g_ref[...]) if last_layer else x


def _final(x1, y_kt, gate, mod, norm_g, seq, last_layer):
    t = x1.shape[0]
    tm = ROW_TILE
    tiles_per_seq = seq // tm
    rows = lambda w: pl.BlockSpec((tm, w), lambda i: (i, 0))
    return pl.pallas_call(
        functools.partial(_final_kernel, last_layer),
        out_shape=jax.ShapeDtypeStruct((t, D_MODEL), F32),
        grid=(t // tm,),
        in_specs=[
            rows(D_MODEL), pl.BlockSpec((TOP_K, tm, PACKED), lambda i: (0, i, 0)), rows(LANES),
            pl.BlockSpec((None, 8, D_MODEL), lambda i: (i // tiles_per_seq, 0, 0)),
            pl.BlockSpec(norm_g.shape, lambda i: (0, 0)),
        ],
        out_specs=rows(D_MODEL),
        compiler_params=pltpu.CompilerParams(
            dimension_semantics=("arbitrary",), vmem_limit_bytes=VMEM_LIMIT),
        name="combine_final",
    )(x1, y_kt, gate, mod, norm_g)


def _prep_weights(w_in, w_uq, w_ukv):
    d = w_in.shape[0]
    splits = (Q_LORA, KV_LORA, QK_ROPE, CONV_WIDTH, CONV_WIDTH, CONV_WIDTH, D_MODEL, D_MODEL)
    offs = [0]
    for s in splits:
        offs.append(offs[-1] + s)
    part = lambda n: w_in[:, offs[n]:offs[n + 1]]
    z = lambda n: jnp.zeros((d, n), w_in.dtype)
    w_kpe = part(2)
    kpe_slab = jnp.concatenate([z(QK_NOPE), w_kpe, z(HEAD_PAD - QK_HEAD)], axis=1)
    w_lat = jnp.concatenate([part(0), part(1), kpe_slab], axis=1).astype(BF16)
    w_conv = w_in[:, offs[3]:offs[6]].astype(BF16)
    w_gate = w_in[:, offs[6]:offs[8]].astype(BF16)

    wq = w_uq.reshape(Q_LORA, N_HEADS, QK_HEAD)
    zq = lambda n: jnp.zeros((Q_LORA, N_HEADS, n), w_uq.dtype)
    wq2 = jnp.concatenate([wq, zq(HEAD_PAD - QK_HEAD)], axis=-1)
    wq2 = wq2.reshape(Q_LORA, N_HEADS * HEAD_PAD).astype(BF16)

    wkv = w_ukv.reshape(KV_LORA, N_HEADS, QK_NOPE + V_HEAD)
    wk2 = jnp.concatenate([wkv[..., :QK_NOPE],
                           jnp.zeros((KV_LORA, N_HEADS, HEAD_PAD - QK_NOPE), w_ukv.dtype)], axis=-1)
    wk2 = wk2.reshape(KV_LORA, N_HEADS * HEAD_PAD).astype(BF16)
    wv = jnp.concatenate([wkv[..., QK_NOPE:],
                          jnp.zeros((KV_LORA, N_HEADS, HEAD_PAD - V_HEAD), w_ukv.dtype)], axis=-1)
    wv = wv.reshape(KV_LORA, N_HEADS * HEAD_PAD).astype(BF16)
    return w_lat, w_conv, w_gate, wq2, wk2, wv


def _rope_freqs():
    inv_freq = 1.0 / (ROPE_THETA ** (jnp.arange(0, QK_ROPE, 2, dtype=F32) / QK_ROPE))
    return inv_freq.reshape(QK_ROPE // 2, 1)


def _dest_kernel(gm_ref, idx_ref, rank_ref, o_ref):
    idx = idx_ref[...]
    dest = rank_ref[...]
    for e in range(N_EXPERTS):
        dest = dest + jnp.where(idx == e, gm_ref[_GM_FIRST, e] * MOE_BLOCK, 0)
    o_ref[...] = dest


def _route(top_idx, rank, counts, n_tokens):
    blocks = (counts + MOE_BLOCK - 1) // MOE_BLOCK
    first_block = jnp.cumsum(blocks) - blocks
    table = jnp.stack([counts, first_block, blocks]).astype(jnp.int32)
    whole = pl.BlockSpec(top_idx.shape, lambda i, gm: (0, 0))
    dest = pl.pallas_call(
        _dest_kernel,
        out_shape=jax.ShapeDtypeStruct(rank.shape, jnp.int32),
        grid_spec=pltpu.PrefetchScalarGridSpec(
            num_scalar_prefetch=1, grid=(1,), in_specs=[whole, whole], out_specs=whole),
        name="row_destinations",
    )(table, top_idx, rank)[:TOP_K]
    n_rows = n_tokens * TOP_K + N_EXPERTS * MOE_BLOCK
    return dest, table, n_rows


SC_CORES = 2
SC_SUBCORES = 16
SC_WORKERS = SC_CORES * SC_SUBCORES
SC_CHUNK = 64
SC_GATHER_RING = 3

def _sc_mesh():
    return plsc.VectorSubcoreMesh(core_axis_name="c", subcore_axis_name="s")


def _sc_worker():
    return lax.axis_index("s") * SC_CORES + lax.axis_index("c")


def _dispatch(h2, dest, n_rows):
    t, d = h2.shape
    per_w = t // SC_WORKERS
    n_chunks = per_w // SC_CHUNK
    assert per_w % (2 * SC_CHUNK) == 0
    idx = dest.reshape(TOP_K, SC_WORKERS, n_chunks, SC_CHUNK).transpose(1, 0, 2, 3)
    idx = idx.reshape(SC_WORKERS, TOP_K * n_chunks, SC_CHUNK)

    @functools.partial(
        pl.kernel, mesh=_sc_mesh(),
        out_type=jax.ShapeDtypeStruct((n_rows, d), h2.dtype),
        scratch_types=[pltpu.VMEM((TOP_K * n_chunks, SC_CHUNK), jnp.int32),
                       pltpu.VMEM((2, SC_CHUNK, d), h2.dtype),
                       pltpu.SemaphoreType.DMA((2,)),
                       pltpu.SemaphoreType.DMA((2,))],
        name="moe_dispatch")
    def run(h2_hbm, idx_hbm, xs_hbm, idx_v, rows_v, rsem, ssem):
        w = _sc_worker()
        pltpu.sync_copy(idx_hbm.at[w], idx_v)

        def read(g, b):
            src = h2_hbm.at[pl.ds(w * per_w + g * SC_CHUNK, SC_CHUNK)]
            return pltpu.make_async_copy(src, rows_v.at[b], rsem.at[b])

        def scatter(g, kk, b):
            dst = xs_hbm.at[idx_v.at[kk * n_chunks + g]]
            return pltpu.make_async_copy(rows_v.at[b], dst, ssem.at[b])

        read(0, 0).start()

        @pl.loop(0, n_chunks, step=2)
        def _(g0):
            for b in range(2):
                g = g0 + b
                read(g, b).wait()

                @pl.when(g + 1 < n_chunks)
                def _():
                    read(g + 1, 1 - b).start()

                for kk in range(TOP_K):
                    scatter(g, kk, b).start()
                for kk in range(TOP_K):
                    scatter(g, kk, b).wait()

    return run(h2, idx)


def _undispatch(ys, dest):
    t = dest.shape[1]
    d = ys.shape[1]
    n_out = t * TOP_K
    per_w = n_out // SC_WORKERS
    n_chunks = per_w // SC_CHUNK
    idx = dest.reshape(SC_WORKERS, n_chunks, SC_CHUNK)
    ring = SC_GATHER_RING

    @functools.partial(
        pl.kernel, mesh=_sc_mesh(),
        out_type=jax.ShapeDtypeStruct((n_out, d), ys.dtype),
        scratch_types=[pltpu.VMEM((n_chunks, SC_CHUNK), jnp.int32),
                       pltpu.VMEM((ring, SC_CHUNK, d), ys.dtype),
                       pltpu.SemaphoreType.DMA((ring,)),
                       pltpu.SemaphoreType.DMA((ring,))],
        name="moe_undispatch")
    def run(ys_hbm, idx_hbm, out_hbm, idx_v, rows_v, gsem, wsem):
        w = _sc_worker()
        pltpu.sync_copy(idx_hbm.at[w], idx_v)

        def gather(g):
            b = g % ring
            return pltpu.make_async_copy(ys_hbm.at[idx_v.at[g]], rows_v.at[b], gsem.at[b])

        def write(g):
            b = g % ring
            dst = out_hbm.at[pl.ds(w * per_w + g * SC_CHUNK, SC_CHUNK)]
            return pltpu.make_async_copy(rows_v.at[b], dst, wsem.at[b])

        for g in range(min(ring - 1, n_chunks)):
            gather(g).start()
        for g in range(n_chunks):
            gather(g).wait()
            ahead = g + ring - 1
            if ahead < n_chunks:
                if g >= 1:
                    write(g - 1).wait()
                gather(ahead).start()
            write(g).start()
        for g in range(max(n_chunks - ring, 0), n_chunks):
            write(g).wait()

    return run(ys, idx).reshape(TOP_K, t, d)


def kernel(x, c, positions, w_ada, b_ada, norm_mix_g, w_in, q_norm_g, w_uq, kv_norm_g, w_ukv,
           w_up_attn, conv_w, w_up_conv, w_o, norm_ffn_g, router_w, router_b, w_gu, b_gu,
           w_down, b_down, norm_final_g):
    batch, seq, d = x.shape
    t = batch * seq
    depth = w_ada.shape[0]
    x2 = x.reshape(t, d)
    pos = positions.astype(F32).reshape(t // ROW_TILE, 1, ROW_TILE)
    freqs = _rope_freqs()
    c_pad = jnp.zeros((8, d), F32).at[:batch].set(c)

    for l in range(depth):
        ada = _ada(c_pad, w_ada[l], b_ada[l].reshape(1, -1))
        mod = ada[:batch].reshape(batch, 6, d)
        mod = jnp.concatenate([mod, jnp.zeros((batch, 2, d), F32)], axis=1)

        w_lat, w_conv, w_gate, wq2, wk2, wv = _prep_weights(w_in[l], w_uq[l], w_ukv[l])
        q, k, v, sga, gc = _pre(x2, mod, norm_mix_g[l].reshape(1, d), w_lat, w_conv, w_gate,
                                q_norm_g[l].reshape(1, -1), wq2,
                                kv_norm_g[l].reshape(1, -1), wk2, wv, pos, freqs, conv_w[l],
                                w_up_conv[l].astype(BF16), seq)
        attn = _attention(q, k, v, batch, seq)

        rw_pad = jnp.concatenate([router_w[l], jnp.zeros((d, LANES - N_EXPERTS), F32)], axis=1)
        rb_pad = jnp.concatenate([router_b[l], jnp.full((LANES - N_EXPERTS,), NEG_BIG, F32)])
        rw_hi = rw_pad.astype(BF16)
        rw_lo = (rw_pad - rw_hi.astype(F32)).astype(BF16)
        x1, h2, idx_pad, gate_pad, rank_pad, counts = _post(
            attn, sga, gc, x2, mod, w_up_attn[l].astype(BF16), w_o[l].astype(BF16),
            norm_ffn_g[l].reshape(1, d), rw_hi, rw_lo, rb_pad.reshape(1, LANES), seq)

        dest, group_table, n_rows = _route(
            idx_pad, rank_pad, counts[0, :N_EXPERTS], t)
        xs = _dispatch(h2, dest, n_rows)
        ys = _moe(group_table, xs, w_gu[l], b_gu[l].reshape(N_EXPERTS, 1, -1),
                  w_down[l], b_down[l].reshape(N_EXPERTS, 1, -1))
        y_kt = _undispatch(ys, dest)
        x2 = _final(x1, y_kt, gate_pad, mod, norm_final_g.reshape(1, d), seq, l == depth - 1)

    return x2.reshape(batch, seq, d)
```

```python
import functools
import math

import jax
import jax.numpy as jnp
from jax import lax
from jax.experimental import pallas as pl
from jax.experimental.pallas import tpu as pltpu
from jax.experimental.pallas import tpu_sc as plsc

D_MODEL = 1024
CHUNK = 64
N_HEADS = 8
Q_LORA = 256
KV_LORA = 128
QK_NOPE = 64
QK_ROPE = 32
V_HEAD = 64
QK_HEAD = QK_NOPE + QK_ROPE
ROPE_THETA = 10000.0
CONV_WIDTH = 512
CONV_K = 3
N_EXPERTS = 32
TOP_K = 4
D_EXPERT = 1024
SWIGLU_LIMIT = 7.0
SWIGLU_ALPHA = 1.702
MOE_BLOCK = 256
RMS_EPS = 1e-6

LANES = 128
HEAD_PAD = 128
NEG_BIG = -1e30
VMEM_LIMIT = 56 * 1024 * 1024

F32 = jnp.float32
BF16 = jnp.bfloat16

Q_PRESCALE = (QK_HEAD ** -0.5) * math.log2(math.e)

ROW_TILE = 1024
MOE_STEP_BLOCKS = 4
POST_TILE = 1024
POST_SUB = 512
ATT_BLOCK = 512
ATT_WIDE = 2


def _rms(x, g):
    ms = jnp.mean(x * x, axis=-1, keepdims=True)
    return x * lax.rsqrt(ms + RMS_EPS) * g


def _dot(a, b):
    return jnp.dot(a, b, preferred_element_type=F32)


PACKED = D_MODEL // 2


def _pack_row(x):
    return pltpu.pack_elementwise([x[:, :PACKED], x[:, PACKED:]], packed_dtype=BF16)


def _unpack_row(w):
    half = lambda i: pltpu.unpack_elementwise(w, index=i, packed_dtype=BF16, unpacked_dtype=F32)
    return jnp.concatenate([half(0), half(1)], axis=-1)


def _ada_kernel(c_ref, w_ref, b_ref, o_ref):
    c = c_ref[...]
    ca = (c * jax.nn.sigmoid(c)).astype(BF16)
    o_ref[...] = _dot(ca, w_ref[...].astype(BF16)) + b_ref[...]


def _ada(c_pad, w_ada, b_ada):
    n = w_ada.shape[1]
    tn = 1024
    return pl.pallas_call(
        _ada_kernel,
        out_shape=jax.ShapeDtypeStruct((c_pad.shape[0], n), F32),
        grid=(n // tn,),
        in_specs=[
            pl.BlockSpec(c_pad.shape, lambda j: (0, 0)),
            pl.BlockSpec((D_MODEL, tn), lambda j: (0, j)),
            pl.BlockSpec((1, tn), lambda j: (0, j)),
        ],
        out_specs=pl.BlockSpec((c_pad.shape[0], tn), lambda j: (0, j)),
        compiler_params=pltpu.CompilerParams(
            dimension_semantics=("arbitrary",), vmem_limit_bytes=VMEM_LIMIT),
        name="ada",
    )(c_pad, w_ada, b_ada)


_C_QLAT = 0
_C_KVLAT = _C_QLAT + Q_LORA
_C_KPE = _C_KVLAT + KV_LORA
_C_END = _C_KPE + HEAD_PAD


def _pre_kernel(tiles_per_seq, x_ref, mod_ref, g_ref, wlat_ref, wconv_ref, wgate_ref, qg_ref,
                wq_ref, kvg_ref, wk_ref, wv_ref, pos_ref, freq_ref, cw_ref, wuc_ref,
                q_ref, k_ref, v_ref, sga_ref, gc_ref, carry_ref):
    i = pl.program_id(0)
    tm = x_ref.shape[0]
    mod = mod_ref[...]
    h = _rms(x_ref[...], g_ref[...]) * (1.0 + mod[1:2]) + mod[0:1]
    hb = h.astype(BF16)

    ang = freq_ref[...] * pos_ref[...]
    cos_t, sin_t = jnp.cos(ang), jnp.sin(ang)
    ones_t = jnp.ones((QK_NOPE, tm), F32)
    zeros_t = jnp.zeros((QK_NOPE, tm), F32)
    pad_t = jnp.zeros((HEAD_PAD - QK_HEAD, tm), F32)
    cosf = jnp.concatenate([ones_t, cos_t, cos_t, pad_t], axis=0).T
    sinf = jnp.concatenate([zeros_t, -sin_t, sin_t, pad_t], axis=0).T

    first_half = lax.broadcasted_iota(jnp.int32, (tm, HEAD_PAD), 1) < QK_NOPE + QK_ROPE // 2

    def rope(slab):
        swapped = jnp.where(first_half, pltpu.roll(slab, HEAD_PAD - QK_ROPE // 2, 1),
                            pltpu.roll(slab, QK_ROPE // 2, 1))
        return slab * cosf + swapped * sinf

    small = _dot(hb, wlat_ref[...])
    q_lat = small[:, _C_QLAT:_C_KVLAT]
    kv_lat = small[:, _C_KVLAT:_C_KPE]
    kpe = rope(small[:, _C_KPE:_C_END])
    qn = _rms(q_lat, qg_ref[...]).astype(BF16)
    q = _dot(qn, wq_ref[...])
    q = jnp.concatenate([rope(q[:, hd * HEAD_PAD:(hd + 1) * HEAD_PAD]) for hd in range(N_HEADS)],
                        axis=-1)
    q_ref[...] = (q * Q_PRESCALE).astype(BF16)
    kvn = _rms(kv_lat, kvg_ref[...]).astype(BF16)
    k = _dot(kvn, wk_ref[...]) + jnp.concatenate([kpe] * N_HEADS, axis=-1)
    k_ref[...] = k.astype(BF16)
    lane = lax.broadcasted_iota(jnp.int32, (tm, N_HEADS * HEAD_PAD), 1)
    ones_col = jnp.where(lane % HEAD_PAD == V_HEAD, 1.0, 0.0)
    v_ref[...] = (_dot(kvn, wv_ref[...]) + ones_col).astype(BF16)

    ucb = _dot(hb, wconv_ref[...])
    cu = ucb[:, 0:CONV_WIDTH] * ucb[:, CONV_WIDTH:2 * CONV_WIDTH]
    b_gate = ucb[:, 2 * CONV_WIDTH:3 * CONV_WIDTH]

    @pl.when(i % tiles_per_seq == 0)
    def _():
        carry_ref[...] = jnp.zeros_like(carry_ref)

    prev = carry_ref[...]
    row = lax.broadcasted_iota(jnp.int32, cu.shape, 0)
    cu1 = jnp.where(row == 0, prev[7:8], pltpu.roll(cu, 1, 0))
    cu2 = jnp.where(row == 0, prev[6:7], jnp.where(row == 1, prev[7:8], pltpu.roll(cu, 2, 0)))
    cw = cw_ref[...]
    z = cw[2:3] * cu + cw[1:2] * cu1 + cw[0:1] * cu2
    carry_ref[...] = cu[tm - 8:tm]
    c_branch = _dot((b_gate * z).astype(BF16), wuc_ref[...])

    gates = _dot(hb, wgate_ref[...])
    sga_ref[...] = jax.nn.sigmoid(gates[:, 0:D_MODEL]).astype(BF16)
    gc_ref[...] = (jax.nn.sigmoid(gates[:, D_MODEL:]) * c_branch).astype(BF16)


def _pre(x2, mod, norm_g, w_lat, w_conv, w_gate, q_norm_g, wq2, kv_norm_g, wk2, wv, pos,
         freqs, conv_w, w_up_conv, seq):
    t = x2.shape[0]
    tm = ROW_TILE
    tiles_per_seq = seq // tm
    full = lambda a: pl.BlockSpec(a.shape, lambda i: (0,) * a.ndim)
    rows = lambda w: pl.BlockSpec((tm, w), lambda i: (i, 0))
    outs = [jax.ShapeDtypeStruct((t, N_HEADS * HEAD_PAD), BF16),
            jax.ShapeDtypeStruct((t, N_HEADS * HEAD_PAD), BF16),
            jax.ShapeDtypeStruct((t, N_HEADS * HEAD_PAD), BF16),
            jax.ShapeDtypeStruct((t, D_MODEL), BF16),
            jax.ShapeDtypeStruct((t, D_MODEL), BF16)]
    return pl.pallas_call(
        functools.partial(_pre_kernel, tiles_per_seq),
        out_shape=outs,
        grid=(t // tm,),
        in_specs=[
            rows(D_MODEL),
            pl.BlockSpec((None, 8, D_MODEL), lambda i: (i // tiles_per_seq, 0, 0)),
            full(norm_g), full(w_lat), full(w_conv), full(w_gate), full(q_norm_g), full(wq2),
            full(kv_norm_g), full(wk2), full(wv),
            pl.BlockSpec((None, 1, tm), lambda i: (i, 0, 0)), full(freqs), full(conv_w),
            full(w_up_conv),
        ],
        out_specs=[rows(N_HEADS * HEAD_PAD), rows(N_HEADS * HEAD_PAD), rows(N_HEADS * HEAD_PAD),
                   rows(D_MODEL), rows(D_MODEL)],
        scratch_shapes=[pltpu.VMEM((8, CONV_WIDTH), F32)],
        compiler_params=pltpu.CompilerParams(
            dimension_semantics=("arbitrary",), vmem_limit_bytes=VMEM_LIMIT),
        name="pre_mixer",
    )(x2, mod, norm_g, w_lat, w_conv, w_gate, q_norm_g, wq2, kv_norm_g, wk2, wv, pos,
      freqs, conv_w, w_up_conv)


def _attn_kernel(q_ref, k_ref, v_ref, o_ref, m_ref, acc_ref):
    i = pl.program_id(1)
    tq = q_ref.shape[0]

    def step(k0, tk, masked, first=False):
        if masked:
            rq = (lax.broadcasted_iota(jnp.int32, (tq, tk), 0) + (tk - tq)) // CHUNK
            ck = lax.broadcasted_iota(jnp.int32, (tq, tk), 1) // CHUNK
            allowed = ck <= rq
        for hd in range(N_HEADS):
            hs = slice(hd * HEAD_PAD, (hd + 1) * HEAD_PAD)
            s = lax.dot_general(q_ref[:, hs], k_ref[pl.ds(k0, tk), hs],
                                (((1,), (1,)), ((), ())), preferred_element_type=F32)
            if masked:
                s = jnp.where(allowed, s, NEG_BIG)
            s_max = s[:, 0:LANES]
            for c in range(1, tk // LANES):
                s_max = jnp.maximum(s_max, s[:, c * LANES:(c + 1) * LANES])
            m_new = jnp.broadcast_to(jnp.max(s_max, axis=-1, keepdims=True), (tq, LANES))
            if not first:
                m_old = m_ref[hd]
                m_new = jnp.maximum(m_old, m_new)
            p = jnp.concatenate(
                [jnp.exp2(s[:, c * LANES:(c + 1) * LANES] - m_new).astype(BF16)
                 for c in range(tk // LANES)], axis=-1)
            pv = _dot(p, v_ref[pl.ds(k0, tk), hs])
            acc_ref[hd] = pv if first else jnp.exp2(m_old - m_new) * acc_ref[hd] + pv
            m_ref[hd] = m_new

    wide = ATT_WIDE * tq
    n_wide = i // ATT_WIDE

    @pl.when(n_wide > 0)
    def _():
        step(0, wide, False, first=True)

    @pl.when(n_wide == 0)
    def _():
        m_ref[...] = jnp.full_like(m_ref, NEG_BIG)
        acc_ref[...] = jnp.zeros_like(acc_ref)

    def body(j, carry):
        step(pl.multiple_of(j * wide, wide), wide, False)
        return carry

    lax.fori_loop(1, n_wide, body, 0)

    for r in range(ATT_WIDE):
        @pl.when(i % ATT_WIDE == r)
        def _():
            step(pl.multiple_of((i - r) * tq, tq), (r + 1) * tq, True)

    for hp in range(N_HEADS // 2):
        pair = []
        for hd in (2 * hp, 2 * hp + 1):
            acc = acc_ref[hd]
            pair.append(acc[:, 0:V_HEAD] / acc[:, V_HEAD:V_HEAD + 1])
        o_ref[:, hp * LANES:(hp + 1) * LANES] = jnp.concatenate(pair, axis=-1).astype(BF16)


def _attention(q, k, v, batch, seq):
    tq = ATT_BLOCK
    nq = seq // tq
    return pl.pallas_call(
        _attn_kernel,
        out_shape=jax.ShapeDtypeStruct((batch * seq, N_HEADS * V_HEAD), BF16),
        grid=(batch, nq),
        in_specs=[
            pl.BlockSpec((tq, N_HEADS * HEAD_PAD), lambda b, i: (b * nq + i, 0)),
            pl.BlockSpec((seq, N_HEADS * HEAD_PAD), lambda b, i: (b, 0)),
            pl.BlockSpec((seq, N_HEADS * HEAD_PAD), lambda b, i: (b, 0)),
        ],
        out_specs=pl.BlockSpec((tq, N_HEADS * V_HEAD), lambda b, i: (b * nq + i, 0)),
        scratch_shapes=[pltpu.VMEM((N_HEADS, tq, LANES), F32),
                        pltpu.VMEM((N_HEADS, tq, LANES), F32)],
        compiler_params=pltpu.CompilerParams(
            dimension_semantics=("arbitrary", "arbitrary"), vmem_limit_bytes=VMEM_LIMIT),
        name="attention",
    )(q, k, v)


def _post_kernel(attn_ref, sga_ref, gc_ref, x_ref, mod_ref, wua_ref, wo_ref, g_ref, rwh_ref,
                 rwl_ref, rb_ref, x1_ref, h2_ref, idx_ref, gate_ref, rank_ref, cnt_out_ref, cnt_ref,
                 lg_ref):
    i = pl.program_id(0)

    @pl.when(i == 0)
    def _():
        cnt_ref[...] = jnp.zeros_like(cnt_ref)
        lg_ref[...] = jnp.zeros_like(lg_ref)

    prev_logits = lg_ref[(i + 1) % 2]
    counts = cnt_ref[...]
    routed = counts
    for r0 in range(0, x_ref.shape[0], POST_SUB):
        rs = slice(r0, r0 + POST_SUB)
        lg_ref[i % 2, rs, :] = _post_mix(rs, attn_ref, sga_ref, gc_ref, x_ref, mod_ref, wua_ref,
                                         wo_ref, g_ref, rwh_ref, rwl_ref, rb_ref, x1_ref, h2_ref)
        routed = _post_route(rs, prev_logits[rs, :], routed, idx_ref, gate_ref, rank_ref)
    counts = jnp.where(i > 0, routed, counts)
    cnt_ref[...] = counts
    cnt_out_ref[...] = counts.astype(jnp.int32)


def _post_mix(rs, attn_ref, sga_ref, gc_ref, x_ref, mod_ref, wua_ref, wo_ref, g_ref,
              rwh_ref, rwl_ref, rb_ref, x1_ref, h2_ref):
    mod = mod_ref[...]
    a_branch = _dot(attn_ref[rs, :], wua_ref[...])
    merged = sga_ref[rs, :].astype(F32) * a_branch + gc_ref[rs, :].astype(F32)
    mix = _dot(merged.astype(BF16), wo_ref[...])
    x1 = x_ref[rs, :] + mod[2:3] * mix
    x1_ref[rs, :] = x1
    h2 = _rms(x1, g_ref[...]) * (1.0 + mod[4:5]) + mod[3:4]
    h2_ref[rs, :] = _pack_row(h2)

    h_hi = h2.astype(BF16)
    h_lo = (h2 - h_hi.astype(F32)).astype(BF16)
    return (_dot(h_hi, rwh_ref[...]) + _dot(h_lo, rwh_ref[...]) + _dot(h_hi, rwl_ref[...])
            + rb_ref[...])


def _post_route(rs, logits, counts, idx_ref, gate_ref, rank_ref):
    lane = lax.broadcasted_iota(jnp.int32, logits.shape, 1)
    work = logits
    vals, idxs = [], []
    for _ in range(TOP_K):
        mk = jnp.max(work, axis=-1, keepdims=True)
        ik = jnp.min(jnp.where(work == mk, lane, LANES), axis=-1, keepdims=True)
        vals.append(mk)
        idxs.append(ik)
        work = jnp.where(lane == ik, -jnp.inf, work)
    es = [jnp.exp(vk - vals[0]) for vk in vals]
    denom = es[0] + es[1] + es[2] + es[3]
    tm = logits.shape[0]
    chosen = jnp.zeros(logits.shape, F32)
    for kk in range(TOP_K):
        chosen = chosen + jnp.where(lane == idxs[kk], 1.0, 0.0)
    r_i = lax.broadcasted_iota(jnp.int32, (tm, tm), 0)
    c_i = lax.broadcasted_iota(jnp.int32, (tm, tm), 1)
    earlier = jnp.where(c_i < r_i, 1.0, 0.0).astype(BF16)
    before = _dot(earlier, chosen.astype(BF16)) + counts[0:1]

    idx_out = jnp.zeros(logits.shape, F32)
    gate_out = jnp.zeros(logits.shape, F32)
    rank_out = jnp.zeros(logits.shape, F32)
    for kk in range(TOP_K):
        rank_k = jnp.sum(jnp.where(lane == idxs[kk], before, 0.0), axis=-1, keepdims=True)
        idx_out = jnp.where(lane == kk, idxs[kk].astype(F32), idx_out)
        gate_out = jnp.where(lane == kk, es[kk] / denom, gate_out)
        rank_out = jnp.where(lane == kk, rank_k, rank_out)
    gate_ref[rs, :] = gate_out
    idx_ref[:, rs] = idx_out.T[0:8].astype(jnp.int32)
    rank_ref[:, rs] = rank_out.T[0:8].astype(jnp.int32)
    return counts + jnp.sum(chosen, axis=0, keepdims=True)


def _post(attn, sga, gc, x2, mod, wua, wo, norm_g, rw_hi, rw_lo, rb_pad, seq):
    t = x2.shape[0]
    tm = POST_TILE
    tiles_per_seq = seq // tm
    n_tiles = t // tm
    full = lambda a: pl.BlockSpec(a.shape, lambda i: (0,) * a.ndim)
    mix_tile = lambda i: jnp.minimum(i, n_tiles - 1)
    route_tile = lambda i: jnp.maximum(i - 1, 0)
    rows = lambda w: pl.BlockSpec((tm, w), lambda i: (mix_tile(i), 0))
    outs = [jax.ShapeDtypeStruct((t, D_MODEL), F32),
            jax.ShapeDtypeStruct((t, PACKED), jnp.uint32),
            jax.ShapeDtypeStruct((8, t), jnp.int32),
            jax.ShapeDtypeStruct((t, LANES), F32),
            jax.ShapeDtypeStruct((8, t), jnp.int32),
            jax.ShapeDtypeStruct((8, LANES), jnp.int32)]
    slots = pl.BlockSpec((8, tm), lambda i: (0, route_tile(i)))
    return pl.pallas_call(
        _post_kernel,
        out_shape=outs,
        grid=(n_tiles + 1,),
        in_specs=[
            rows(N_HEADS * V_HEAD), rows(D_MODEL), rows(D_MODEL), rows(D_MODEL),
            pl.BlockSpec((None, 8, D_MODEL), lambda i: (mix_tile(i) // tiles_per_seq, 0, 0)),
            full(wua), full(wo), full(norm_g), full(rw_hi), full(rw_lo), full(rb_pad),
        ],
        out_specs=[rows(D_MODEL), rows(PACKED), slots,
                   pl.BlockSpec((tm, LANES), lambda i: (route_tile(i), 0)), slots,
                   pl.BlockSpec((8, LANES), lambda i: (0, 0))],
        scratch_shapes=[pltpu.VMEM((8, LANES), F32), pltpu.VMEM((2, tm, LANES), F32)],
        compiler_params=pltpu.CompilerParams(
            dimension_semantics=("arbitrary",), vmem_limit_bytes=VMEM_LIMIT),
        name="post_mixer",
    )(attn, sga, gc, x2, mod, wua, wo, norm_g, rw_hi, rw_lo, rb_pad)


_GM_COUNT, _GM_FIRST, _GM_BLOCKS = range(3)
_ST_EXPERT, _ST_SLOT = range(2)


def _moe_kernel(gm_ref, xs_ref, wgu_hbm, bgu_ref, wd_hbm, bd_ref, o_ref,
                wgu_f, wd_f, wgu_bf, wd_bf, sem, st_ref):
    weights = (wgu_hbm, wd_hbm, wgu_f, wd_f, wgu_bf, wd_bf, sem)
    rows = (xs_ref, bgu_ref, bd_ref, o_ref, wgu_bf, wd_bf)
    b0 = pl.program_id(0) * MOE_STEP_BLOCKS
    e, in_group, used = _moe_enter(b0, gm_ref, st_ref, *weights)
    together = jnp.logical_and(used, in_group + MOE_STEP_BLOCKS <= gm_ref[_GM_BLOCKS, e])

    @pl.when(together)
    def _():
        _moe_rows(slice(0, MOE_STEP_BLOCKS * MOE_BLOCK), e, in_group, True, gm_ref, *rows)

    @pl.when(jnp.logical_not(together))
    def _():
        _moe_rows(slice(0, MOE_BLOCK), e, in_group, used, gm_ref, *rows)
        for r in range(1, MOE_STEP_BLOCKS):
            e_r, in_group_r, used_r = _moe_enter(b0 + r, gm_ref, st_ref, *weights)
            _moe_rows(slice(r * MOE_BLOCK, (r + 1) * MOE_BLOCK), e_r, in_group_r, used_r, gm_ref,
                      *rows)


def _moe_enter(b, gm_ref, st_ref, wgu_hbm, wd_hbm, wgu_f, wd_f, wgu_bf, wd_bf, sem):
    def weight_copies(expert, sl):
        return (pltpu.make_async_copy(wgu_hbm.at[expert], wgu_f.at[sl], sem.at[0, sl]),
                pltpu.make_async_copy(wd_hbm.at[expert], wd_f.at[sl], sem.at[1, sl]))

    def next_group(e):
        return lax.while_loop(
            lambda k: jnp.logical_and(k < N_EXPERTS,
                                      gm_ref[_GM_BLOCKS, jnp.minimum(k, N_EXPERTS - 1)] == 0),
            lambda k: k + 1, e)

    @pl.when(b == 0)
    def _():
        e0 = next_group(0)
        st_ref[_ST_EXPERT] = e0
        st_ref[_ST_SLOT] = 1
        for cp in weight_copies(e0, 0):
            cp.start()

    e_prev = st_ref[_ST_EXPERT]
    past = b >= gm_ref[_GM_FIRST, e_prev] + gm_ref[_GM_BLOCKS, e_prev]
    e = jnp.minimum(jnp.where(past, next_group(e_prev + 1), e_prev), N_EXPERTS - 1)
    st_ref[_ST_EXPERT] = e
    in_group = b - gm_ref[_GM_FIRST, e]
    used = jnp.logical_and(in_group >= 0, in_group < gm_ref[_GM_BLOCKS, e])

    @pl.when(jnp.logical_and(used, in_group == 0))
    def _():
        slot = 1 - st_ref[_ST_SLOT]
        st_ref[_ST_SLOT] = slot
        nxt = next_group(e + 1)

        @pl.when(nxt < N_EXPERTS)
        def _():
            for cp in weight_copies(nxt, 1 - slot):
                cp.start(priority=1)

        for cp in weight_copies(e, slot):
            cp.wait()
        wgu_bf[...] = wgu_f[slot].astype(BF16)
        wd_bf[...] = wd_f[slot].astype(BF16)

    return e, in_group, used


def _moe_rows(rs, e, in_group, used, gm_ref, xs_ref, bgu_ref, bd_ref, o_ref, wgu_bf, wd_bf):
    n_valid = gm_ref[_GM_COUNT, e] - in_group * MOE_BLOCK

    def ffn(r):
        n = r.stop - r.start
        row = lax.broadcasted_iota(jnp.int32, (n, PACKED), 0)
        xs = _unpack_row(jnp.where(row < n_valid, xs_ref[r, :], 0)).astype(BF16)
        gu = _dot(xs, wgu_bf[...]) + bgu_ref[e]
        gate = jnp.minimum(gu[:, :D_EXPERT], SWIGLU_LIMIT)
        up = jnp.clip(gu[:, D_EXPERT:], -SWIGLU_LIMIT, SWIGLU_LIMIT)
        act = (up + 1.0) * (gate * jax.nn.sigmoid(gate * SWIGLU_ALPHA))
        o_ref[r, :] = _pack_row(_dot(act.astype(BF16), wd_bf[...]) + bd_ref[e])

    def zeros(r):
        o_ref[r, :] = jnp.zeros((r.stop - r.start, PACKED), o_ref.dtype)

    if used is True:
        ffn(rs)
        return

    mid = rs.start + (rs.stop - rs.start) // 2
    short = n_valid <= mid - rs.start

    @pl.when(jnp.logical_and(used, jnp.logical_not(short)))
    def _():
        ffn(rs)

    @pl.when(jnp.logical_and(used, short))
    def _():
        ffn(slice(rs.start, mid))
        zeros(slice(mid, rs.stop))

    @pl.when(jnp.logical_not(used))
    def _():
        zeros(rs)


def _moe(group_table, xs, w_gu, b_gu, w_down, b_down):
    n_rows = xs.shape[0]
    step_rows = MOE_STEP_BLOCKS * MOE_BLOCK
    assert n_rows % step_rows == 0
    grid_spec = pltpu.PrefetchScalarGridSpec(
        num_scalar_prefetch=1,
        grid=(n_rows // step_rows,),
        in_specs=[
            pl.BlockSpec((step_rows, PACKED), lambda b, gm: (b, 0)),
            pl.BlockSpec(memory_space=pl.ANY),
            pl.BlockSpec(b_gu.shape, lambda b, gm: (0, 0, 0)),
            pl.BlockSpec(memory_space=pl.ANY),
            pl.BlockSpec(b_down.shape, lambda b, gm: (0, 0, 0)),
        ],
        out_specs=pl.BlockSpec((step_rows, PACKED), lambda b, gm: (b, 0)),
        scratch_shapes=[pltpu.VMEM((2, D_MODEL, 2 * D_EXPERT), F32),
                        pltpu.VMEM((2, D_EXPERT, D_MODEL), F32),
                        pltpu.VMEM((D_MODEL, 2 * D_EXPERT), BF16),
                        pltpu.VMEM((D_EXPERT, D_MODEL), BF16),
                        pltpu.SemaphoreType.DMA((2, 2)),
                        pltpu.SMEM((2,), jnp.int32)],
    )
    return pl.pallas_call(
        _moe_kernel,
        out_shape=jax.ShapeDtypeStruct((n_rows, PACKED), jnp.uint32),
        grid_spec=grid_spec,
        compiler_params=pltpu.CompilerParams(
            dimension_semantics=("arbitrary",), vmem_limit_bytes=VMEM_LIMIT),
        name="moe_experts",
    )(group_table, xs, w_gu, b_gu, w_down, b_down)


def _final_kernel(last_layer, x1_ref, y_ref, gate_ref, mod_ref, g_ref, o_ref):
    mod = mod_ref[...]
    gate = gate_ref[...]
    ffn = gate[:, 0:1] * _unpack_row(y_ref[0])
    for kk in range(1, TOP_K):
        ffn = ffn + gate[:, kk:kk + 1] * _unpack_row(y_ref[kk])
    x = x1_ref[...] + mod[5:6] * ffn
    o_ref[...] = _rms(x, g_ref[...]) if last_layer else x


def _final(x1, y_kt, gate, mod, norm_g, seq, last_layer):
    t = x1.shape[0]
    tm = ROW_TILE
    tiles_per_seq = seq // tm
    rows = lambda w: pl.BlockSpec((tm, w), lambda i: (i, 0))
    return pl.pallas_call(
        functools.partial(_final_kernel, last_layer),
        out_shape=jax.ShapeDtypeStruct((t, D_MODEL), F32),
        grid=(t // tm,),
        in_specs=[
            rows(D_MODEL), pl.BlockSpec((TOP_K, tm, PACKED), lambda i: (0, i, 0)), rows(LANES),
            pl.BlockSpec((None, 8, D_MODEL), lambda i: (i // tiles_per_seq, 0, 0)),
            pl.BlockSpec(norm_g.shape, lambda i: (0, 0)),
        ],
        out_specs=rows(D_MODEL),
        compiler_params=pltpu.CompilerParams(
            dimension_semantics=("arbitrary",), vmem_limit_bytes=VMEM_LIMIT),
        name="combine_final",
    )(x1, y_kt, gate, mod, norm_g)


def _prep_weights(w_in, w_uq, w_ukv):
    d = w_in.shape[0]
    splits = (Q_LORA, KV_LORA, QK_ROPE, CONV_WIDTH, CONV_WIDTH, CONV_WIDTH, D_MODEL, D_MODEL)
    offs = [0]
    for s in splits:
        offs.append(offs[-1] + s)
    part = lambda n: w_in[:, offs[n]:offs[n + 1]]
    z = lambda n: jnp.zeros((d, n), w_in.dtype)
    w_kpe = part(2)
    kpe_slab = jnp.concatenate([z(QK_NOPE), w_kpe, z(HEAD_PAD - QK_HEAD)], axis=1)
    w_lat = jnp.concatenate([part(0), part(1), kpe_slab], axis=1).astype(BF16)
    w_conv = w_in[:, offs[3]:offs[6]].astype(BF16)
    w_gate = w_in[:, offs[6]:offs[8]].astype(BF16)

    wq = w_uq.reshape(Q_LORA, N_HEADS, QK_HEAD)
    zq = lambda n: jnp.zeros((Q_LORA, N_HEADS, n), w_uq.dtype)
    wq2 = jnp.concatenate([wq, zq(HEAD_PAD - QK_HEAD)], axis=-1)
    wq2 = wq2.reshape(Q_LORA, N_HEADS * HEAD_PAD).astype(BF16)

    wkv = w_ukv.reshape(KV_LORA, N_HEADS, QK_NOPE + V_HEAD)
    wk2 = jnp.concatenate([wkv[..., :QK_NOPE],
                           jnp.zeros((KV_LORA, N_HEADS, HEAD_PAD - QK_NOPE), w_ukv.dtype)], axis=-1)
    wk2 = wk2.reshape(KV_LORA, N_HEADS * HEAD_PAD).astype(BF16)
    wv = jnp.concatenate([wkv[..., QK_NOPE:],
                          jnp.zeros((KV_LORA, N_HEADS, HEAD_PAD - V_HEAD), w_ukv.dtype)], axis=-1)
    wv = wv.reshape(KV_LORA, N_HEADS * HEAD_PAD).astype(BF16)
    return w_lat, w_conv, w_gate, wq2, wk2, wv


def _rope_freqs():
    inv_freq = 1.0 / (ROPE_THETA ** (jnp.arange(0, QK_ROPE, 2, dtype=F32) / QK_ROPE))
    return inv_freq.reshape(QK_ROPE // 2, 1)


def _dest_kernel(gm_ref, idx_ref, rank_ref, o_ref):
    idx = idx_ref[...]
    dest = rank_ref[...]
    for e in range(N_EXPERTS):
        dest = dest + jnp.where(idx == e, gm_ref[_GM_FIRST, e] * MOE_BLOCK, 0)
    o_ref[...] = dest


def _route(top_idx, rank, counts, n_tokens):
    blocks = (counts + MOE_BLOCK - 1) // MOE_BLOCK
    first_block = jnp.cumsum(blocks) - blocks
    table = jnp.stack([counts, first_block, blocks]).astype(jnp.int32)
    whole = pl.BlockSpec(top_idx.shape, lambda i, gm: (0, 0))
    dest = pl.pallas_call(
        _dest_kernel,
        out_shape=jax.ShapeDtypeStruct(rank.shape, jnp.int32),
        grid_spec=pltpu.PrefetchScalarGridSpec(
            num_scalar_prefetch=1, grid=(1,), in_specs=[whole, whole], out_specs=whole),
        name="row_destinations",
    )(table, top_idx, rank)[:TOP_K]
    n_rows = n_tokens * TOP_K + N_EXPERTS * MOE_BLOCK
    return dest, table, n_rows


SC_CORES = 2
SC_SUBCORES = 16
SC_WORKERS = SC_CORES * SC_SUBCORES
SC_CHUNK = 64
SC_GATHER_RING = 3

def _sc_mesh():
    return plsc.VectorSubcoreMesh(core_axis_name="c", subcore_axis_name="s")


def _sc_worker():
    return lax.axis_index("s") * SC_CORES + lax.axis_index("c")


def _dispatch(h2, dest, n_rows):
    t, d = h2.shape
    per_w = t // SC_WORKERS
    n_chunks = per_w // SC_CHUNK
    assert per_w % (2 * SC_CHUNK) == 0
    idx = dest.reshape(TOP_K, SC_WORKERS, n_chunks, SC_CHUNK).transpose(1, 0, 2, 3)
    idx = idx.reshape(SC_WORKERS, TOP_K * n_chunks, SC_CHUNK)

    @functools.partial(
        pl.kernel, mesh=_sc_mesh(),
        out_type=jax.ShapeDtypeStruct((n_rows, d), h2.dtype),
        scratch_types=[pltpu.VMEM((TOP_K * n_chunks, SC_CHUNK), jnp.int32),
                       pltpu.VMEM((2, SC_CHUNK, d), h2.dtype),
                       pltpu.SemaphoreType.DMA((2,)),
                       pltpu.SemaphoreType.DMA((2,))],
        name="moe_dispatch")
    def run(h2_hbm, idx_hbm, xs_hbm, idx_v, rows_v, rsem, ssem):
        w = _sc_worker()
        pltpu.sync_copy(idx_hbm.at[w], idx_v)

        def read(g, b):
            src = h2_hbm.at[pl.ds(w * per_w + g * SC_CHUNK, SC_CHUNK)]
            return pltpu.make_async_copy(src, rows_v.at[b], rsem.at[b])

        def scatter(g, kk, b):
            dst = xs_hbm.at[idx_v.at[kk * n_chunks + g]]
            return pltpu.make_async_copy(rows_v.at[b], dst, ssem.at[b])

        read(0, 0).start()

        @pl.loop(0, n_chunks, step=2)
        def _(g0):
            for b in range(2):
                g = g0 + b
                read(g, b).wait()

                @pl.when(g + 1 < n_chunks)
                def _():
                    read(g + 1, 1 - b).start()

                for kk in range(TOP_K):
                    scatter(g, kk, b).start()
                for kk in range(TOP_K):
                    scatter(g, kk, b).wait()

    return run(h2, idx)


def _undispatch(ys, dest):
    t = dest.shape[1]
    d = ys.shape[1]
    n_out = t * TOP_K
    per_w = n_out // SC_WORKERS
    n_chunks = per_w // SC_CHUNK
    idx = dest.reshape(SC_WORKERS, n_chunks, SC_CHUNK)
    ring = SC_GATHER_RING

    @functools.partial(
        pl.kernel, mesh=_sc_mesh(),
        out_type=jax.ShapeDtypeStruct((n_out, d), ys.dtype),
        scratch_types=[pltpu.VMEM((n_chunks, SC_CHUNK), jnp.int32),
                       pltpu.VMEM((ring, SC_CHUNK, d), ys.dtype),
                       pltpu.SemaphoreType.DMA((ring,)),
                       pltpu.SemaphoreType.DMA((ring,))],
        name="moe_undispatch")
    def run(ys_hbm, idx_hbm, out_hbm, idx_v, rows_v, gsem, wsem):
        w = _sc_worker()
        pltpu.sync_copy(idx_hbm.at[w], idx_v)

        def gather(g):
            b = g % ring
            return pltpu.make_async_copy(ys_hbm.at[idx_v.at[g]], rows_v.at[b], gsem.at[b])

        def write(g):
            b = g % ring
            dst = out_hbm.at[pl.ds(w * per_w + g * SC_CHUNK, SC_CHUNK)]
            return pltpu.make_async_copy(rows_v.at[b], dst, wsem.at[b])

        for g in range(min(ring - 1, n_chunks)):
            gather(g).start()
        for g in range(n_chunks):
            gather(g).wait()
            ahead = g + ring - 1
            if ahead < n_chunks:
                if g >= 1:
                    write(g - 1).wait()
                gather(ahead).start()
            write(g).start()
        for g in range(max(n_chunks - ring, 0), n_chunks):
            write(g).wait()

    return run(ys, idx).reshape(TOP_K, t, d)


def kernel(x, c, positions, w_ada, b_ada, norm_mix_g, w_in, q_norm_g, w_uq, kv_norm_g, w_ukv,
           w_up_attn, conv_w, w_up_conv, w_o, norm_ffn_g, router_w, router_b, w_gu, b_gu,
           w_down, b_down, norm_final_g):
    batch, seq, d = x.shape
    t = batch * seq
    depth = w_ada.shape[0]
    x2 = x.reshape(t, d)
    pos = positions.astype(F32).reshape(t // ROW_TILE, 1, ROW_TILE)
    freqs = _rope_freqs()
    c_pad = jnp.zeros((8, d), F32).at[:batch].set(c)

    for l in range(depth):
        ada = _ada(c_pad, w_ada[l], b_ada[l].reshape(1, -1))
        mod = ada[:batch].reshape(batch, 6, d)
        mod = jnp.concatenate([mod, jnp.zeros((batch, 2, d), F32)], axis=1)

        w_lat, w_conv, w_gate, wq2, wk2, wv = _prep_weights(w_in[l], w_uq[l], w_ukv[l])
        q, k, v, sga, gc = _pre(x2, mod, norm_mix_g[l].reshape(1, d), w_lat, w_conv, w_gate,
                                q_norm_g[l].reshape(1, -1), wq2,
                                kv_norm_g[l].reshape(1, -1), wk2, wv, pos, freqs, conv_w[l],
                                w_up_conv[l].astype(BF16), seq)
        attn = _attention(q, k, v, batch, seq)

        rw_pad = jnp.concatenate([router_w[l], jnp.zeros((d, LANES - N_EXPERTS), F32)], axis=1)
        rb_pad = jnp.concatenate([router_b[l], jnp.full((LANES - N_EXPERTS,), NEG_BIG, F32)])
        rw_hi = rw_pad.astype(BF16)
        rw_lo = (rw_pad - rw_hi.astype(F32)).astype(BF16)
        x1, h2, idx_pad, gate_pad, rank_pad, counts = _post(
            attn, sga, gc, x2, mod, w_up_attn[l].astype(BF16), w_o[l].astype(BF16),
            norm_ffn_g[l].reshape(1, d), rw_hi, rw_lo, rb_pad.reshape(1, LANES), seq)

        dest, group_table, n_rows = _route(
            idx_pad, rank_pad, counts[0, :N_EXPERTS], t)
        xs = _dispatch(h2, dest, n_rows)
        ys = _moe(group_table, xs, w_gu[l], b_gu[l].reshape(N_EXPERTS, 1, -1),
                  w_down[l], b_down[l].reshape(N_EXPERTS, 1, -1))
        y_kt = _undispatch(ys, dest)
        x2 = _final(x1, y_kt, gate_pad, mod, norm_final_g.reshape(1, d), seq, l == depth - 1)

    return x2.reshape(batch, seq, d)
```

```python
import functools
import math

import jax
import jax.numpy as jnp
from jax import lax
from jax.experimental import pallas as pl
from jax.experimental.pallas import tpu as pltpu
from jax.experimental.pallas import tpu_sc as plsc

D_MODEL = 1024
CHUNK = 64
N_HEADS = 8
Q_LORA = 256
KV_LORA = 128
QK_NOPE = 64
QK_ROPE = 32
V_HEAD = 64
QK_HEAD = QK_NOPE + QK_ROPE
ROPE_THETA = 10000.0
CONV_WIDTH = 512
CONV_K = 3
N_EXPERTS = 32
TOP_K = 4
D_EXPERT = 1024
SWIGLU_LIMIT = 7.0
SWIGLU_ALPHA = 1.702
MOE_BLOCK = 256
RMS_EPS = 1e-6

LANES = 128
HEAD_PAD = 128
NEG_BIG = -1e30
VMEM_LIMIT = 56 * 1024 * 1024

F32 = jnp.float32
BF16 = jnp.bfloat16

Q_PRESCALE = (QK_HEAD ** -0.5) * math.log2(math.e)

ROW_TILE = 1024
MOE_STEP_BLOCKS = 4
POST_TILE = 1024
POST_SUB = 512
ATT_BLOCK = 512
ATT_WIDE = 2


def _rms(x, g):
    ms = jnp.mean(x * x, axis=-1, keepdims=True)
    return x * lax.rsqrt(ms + RMS_EPS) * g


def _dot(a, b):
    return jnp.dot(a, b, preferred_element_type=F32)


PACKED = D_MODEL // 2


def _pack_row(x):
    return pltpu.pack_elementwise([x[:, :PACKED], x[:, PACKED:]], packed_dtype=BF16)


def _unpack_row(w):
    half = lambda i: pltpu.unpack_elementwise(w, index=i, packed_dtype=BF16, unpacked_dtype=F32)
    return jnp.concatenate([half(0), half(1)], axis=-1)


def _ada_kernel(c_ref, w_ref, b_ref, o_ref):
    c = c_ref[...]
    ca = (c * jax.nn.sigmoid(c)).astype(BF16)
    o_ref[...] = _dot(ca, w_ref[...].astype(BF16)) + b_ref[...]


def _ada(c_pad, w_ada, b_ada):
    n = w_ada.shape[1]
    tn = 1024
    return pl.pallas_call(
        _ada_kernel,
        out_shape=jax.ShapeDtypeStruct((c_pad.shape[0], n), F32),
        grid=(n // tn,),
        in_specs=[
            pl.BlockSpec(c_pad.shape, lambda j: (0, 0)),
            pl.BlockSpec((D_MODEL, tn), lambda j: (0, j)),
            pl.BlockSpec((1, tn), lambda j: (0, j)),
        ],
        out_specs=pl.BlockSpec((c_pad.shape[0], tn), lambda j: (0, j)),
        compiler_params=pltpu.CompilerParams(
            dimension_semantics=("arbitrary",), vmem_limit_bytes=VMEM_LIMIT),
        name="ada",
    )(c_pad, w_ada, b_ada)


_C_QLAT = 0
_C_KVLAT = _C_QLAT + Q_LORA
_C_KPE = _C_KVLAT + KV_LORA
_C_END = _C_KPE + HEAD_PAD


def _pre_kernel(tiles_per_seq, x_ref, mod_ref, g_ref, wlat_ref, wconv_ref, wgate_ref, qg_ref,
                wq_ref, kvg_ref, wk_ref, wv_ref, pos_ref, freq_ref, cw_ref, wuc_ref,
                q_ref, k_ref, v_ref, sga_ref, gc_ref, carry_ref):
    i = pl.program_id(0)
    tm = x_ref.shape[0]
    mod = mod_ref[...]
    h = _rms(x_ref[...], g_ref[...]) * (1.0 + mod[1:2]) + mod[0:1]
    hb = h.astype(BF16)

    ang = freq_ref[...] * pos_ref[...]
    cos_t, sin_t = jnp.cos(ang), jnp.sin(ang)
    ones_t = jnp.ones((QK_NOPE, tm), F32)
    zeros_t = jnp.zeros((QK_NOPE, tm), F32)
    pad_t = jnp.zeros((HEAD_PAD - QK_HEAD, tm), F32)
    cosf = jnp.concatenate([ones_t, cos_t, cos_t, pad_t], axis=0).T
    sinf = jnp.concatenate([zeros_t, -sin_t, sin_t, pad_t], axis=0).T

    first_half = lax.broadcasted_iota(jnp.int32, (tm, HEAD_PAD), 1) < QK_NOPE + QK_ROPE // 2

    def rope(slab):
        swapped = jnp.where(first_half, pltpu.roll(slab, HEAD_PAD - QK_ROPE // 2, 1),
                            pltpu.roll(slab, QK_ROPE // 2, 1))
        return slab * cosf + swapped * sinf

    small = _dot(hb, wlat_ref[...])
    q_lat = small[:, _C_QLAT:_C_KVLAT]
    kv_lat = small[:, _C_KVLAT:_C_KPE]
    kpe = rope(small[:, _C_KPE:_C_END])
    qn = _rms(q_lat, qg_ref[...]).astype(BF16)
    q = _dot(qn, wq_ref[...])
    q = jnp.concatenate([rope(q[:, hd * HEAD_PAD:(hd + 1) * HEAD_PAD]) for hd in range(N_HEADS)],
                        axis=-1)
    q_ref[...] = (q * Q_PRESCALE).astype(BF16)
    kvn = _rms(kv_lat, kvg_ref[...]).astype(BF16)
    k = _dot(kvn, wk_ref[...]) + jnp.concatenate([kpe] * N_HEADS, axis=-1)
    k_ref[...] = k.astype(BF16)
    lane = lax.broadcasted_iota(jnp.int32, (tm, N_HEADS * HEAD_PAD), 1)
    ones_col = jnp.where(lane % HEAD_PAD == V_HEAD, 1.0, 0.0)
    v_ref[...] = (_dot(kvn, wv_ref[...]) + ones_col).astype(BF16)

    ucb = _dot(hb, wconv_ref[...])
    cu = ucb[:, 0:CONV_WIDTH] * ucb[:, CONV_WIDTH:2 * CONV_WIDTH]
    b_gate = ucb[:, 2 * CONV_WIDTH:3 * CONV_WIDTH]

    @pl.when(i % tiles_per_seq == 0)
    def _():
        carry_ref[...] = jnp.zeros_like(carry_ref)

    prev = carry_ref[...]
    row = lax.broadcasted_iota(jnp.int32, cu.shape, 0)
    cu1 = jnp.where(row == 0, prev[7:8], pltpu.roll(cu, 1, 0))
    cu2 = jnp.where(row == 0, prev[6:7], jnp.where(row == 1, prev[7:8], pltpu.roll(cu, 2, 0)))
    cw = cw_ref[...]
    z = cw[2:3] * cu + cw[1:2] * cu1 + cw[0:1] * cu2
    carry_ref[...] = cu[tm - 8:tm]
    c_branch = _dot((b_gate * z).astype(BF16), wuc_ref[...])

    gates = _dot(hb, wgate_ref[...])
    sga_ref[...] = jax.nn.sigmoid(gates[:, 0:D_MODEL]).astype(BF16)
    gc_ref[...] = (jax.nn.sigmoid(gates[:, D_MODEL:]) * c_branch).astype(BF16)


def _pre(x2, mod, norm_g, w_lat, w_conv, w_gate, q_norm_g, wq2, kv_norm_g, wk2, wv, pos,
         freqs, conv_w, w_up_conv, seq):
    t = x2.shape[0]
    tm = ROW_TILE
    tiles_per_seq = seq // tm
    full = lambda a: pl.BlockSpec(a.shape, lambda i: (0,) * a.ndim)
    rows = lambda w: pl.BlockSpec((tm, w), lambda i: (i, 0))
    outs = [jax.ShapeDtypeStruct((t, N_HEADS * HEAD_PAD), BF16),
            jax.ShapeDtypeStruct((t, N_HEADS * HEAD_PAD), BF16),
            jax.ShapeDtypeStruct((t, N_HEADS * HEAD_PAD), BF16),
            jax.ShapeDtypeStruct((t, D_MODEL), BF16),
            jax.ShapeDtypeStruct((t, D_MODEL), BF16)]
    return pl.pallas_call(
        functools.partial(_pre_kernel, tiles_per_seq),
        out_shape=outs,
        grid=(t // tm,),
        in_specs=[
            rows(D_MODEL),
            pl.BlockSpec((None, 8, D_MODEL), lambda i: (i // tiles_per_seq, 0, 0)),
            full(norm_g), full(w_lat), full(w_conv), full(w_gate), full(q_norm_g), full(wq2),
            full(kv_norm_g), full(wk2), full(wv),
            pl.BlockSpec((None, 1, tm), lambda i: (i, 0, 0)), full(freqs), full(conv_w),
            full(w_up_conv),
        ],
        out_specs=[rows(N_HEADS * HEAD_PAD), rows(N_HEADS * HEAD_PAD), rows(N_HEADS * HEAD_PAD),
                   rows(D_MODEL), rows(D_MODEL)],
        scratch_shapes=[pltpu.VMEM((8, CONV_WIDTH), F32)],
        compiler_params=pltpu.CompilerParams(
            dimension_semantics=("arbitrary",), vmem_limit_bytes=VMEM_LIMIT),
        name="pre_mixer",
    )(x2, mod, norm_g, w_lat, w_conv, w_gate, q_norm_g, wq2, kv_norm_g, wk2, wv, pos,
      freqs, conv_w, w_up_conv)


def _attn_kernel(q_ref, k_ref, v_ref, sga_ref, gc_ref, wua_ref, o_ref, m_ref, acc_ref):
    i = pl.program_id(1)
    tq = q_ref.shape[0]

    def step(k0, tk, masked, first=False):
        if masked:
            rq = (lax.broadcasted_iota(jnp.int32, (tq, tk), 0) + (tk - tq)) // CHUNK
            ck = lax.broadcasted_iota(jnp.int32, (tq, tk), 1) // CHUNK
            allowed = ck <= rq
        for hd in range(N_HEADS):
            hs = slice(hd * HEAD_PAD, (hd + 1) * HEAD_PAD)
            s = lax.dot_general(q_ref[:, hs], k_ref[pl.ds(k0, tk), hs],
                                (((1,), (1,)), ((), ())), preferred_element_type=F32)
            if masked:
                s = jnp.where(allowed, s, NEG_BIG)
            s_max = s[:, 0:LANES]
            for c in range(1, tk // LANES):
                s_max = jnp.maximum(s_max, s[:, c * LANES:(c + 1) * LANES])
            m_new = jnp.broadcast_to(jnp.max(s_max, axis=-1, keepdims=True), (tq, LANES))
            if not first:
                m_old = m_ref[hd]
                m_new = jnp.maximum(m_old, m_new)
            p = jnp.concatenate(
                [jnp.exp2(s[:, c * LANES:(c + 1) * LANES] - m_new).astype(BF16)
                 for c in range(tk // LANES)], axis=-1)
            pv = _dot(p, v_ref[pl.ds(k0, tk), hs])
            acc_ref[hd] = pv if first else jnp.exp2(m_old - m_new) * acc_ref[hd] + pv
            m_ref[hd] = m_new

    wide = ATT_WIDE * tq
    n_wide = i // ATT_WIDE

    @pl.when(n_wide > 0)
    def _():
        step(0, wide, False, first=True)

    @pl.when(n_wide == 0)
    def _():
        m_ref[...] = jnp.full_like(m_ref, NEG_BIG)
        acc_ref[...] = jnp.zeros_like(acc_ref)

    def body(j, carry):
        step(pl.multiple_of(j * wide, wide), wide, False)
        return carry

    lax.fori_loop(1, n_wide, body, 0)

    for r in range(ATT_WIDE):
        @pl.when(i % ATT_WIDE == r)
        def _():
            step(pl.multiple_of((i - r) * tq, tq), (r + 1) * tq, True)

    heads = []
    for hd in range(N_HEADS):
        acc = acc_ref[hd]
        heads.append((acc[:, 0:V_HEAD] / acc[:, V_HEAD:V_HEAD + 1]).astype(BF16))
    a_branch = _dot(jnp.concatenate(heads, axis=-1), wua_ref[...])
    o_ref[...] = (sga_ref[...].astype(F32) * a_branch + gc_ref[...].astype(F32)).astype(BF16)


def _attention(q, k, v, sga, gc, wua, batch, seq):
    tq = ATT_BLOCK
    nq = seq // tq
    q_rows = lambda w: pl.BlockSpec((tq, w), lambda b, i: (b * nq + i, 0))
    whole_seq = pl.BlockSpec((seq, N_HEADS * HEAD_PAD), lambda b, i: (b, 0))
    return pl.pallas_call(
        _attn_kernel,
        out_shape=jax.ShapeDtypeStruct((batch * seq, D_MODEL), BF16),
        grid=(batch, nq),
        in_specs=[q_rows(N_HEADS * HEAD_PAD), whole_seq, whole_seq, q_rows(D_MODEL),
                  q_rows(D_MODEL), pl.BlockSpec(wua.shape, lambda b, i: (0, 0))],
        out_specs=q_rows(D_MODEL),
        scratch_shapes=[pltpu.VMEM((N_HEADS, tq, LANES), F32),
                        pltpu.VMEM((N_HEADS, tq, LANES), F32)],
        compiler_params=pltpu.CompilerParams(
            dimension_semantics=("arbitrary", "arbitrary"), vmem_limit_bytes=VMEM_LIMIT),
        name="attention",
    )(q, k, v, sga, gc, wua)


def _post_kernel(merged_ref, x_ref, mod_ref, wo_ref, g_ref, rwh_ref, rwl_ref, rb_ref,
                 x1_ref, h2_ref, idx_ref, gate_ref, rank_ref, cnt_out_ref, cnt_ref, lg_ref):
    i = pl.program_id(0)

    @pl.when(i == 0)
    def _():
        cnt_ref[...] = jnp.zeros_like(cnt_ref)
        lg_ref[...] = jnp.zeros_like(lg_ref)

    prev_logits = lg_ref[(i + 1) % 2]
    counts = cnt_ref[...]
    routed = counts
    for r0 in range(0, x_ref.shape[0], POST_SUB):
        rs = slice(r0, r0 + POST_SUB)
        lg_ref[i % 2, rs, :] = _post_mix(rs, merged_ref, x_ref, mod_ref, wo_ref, g_ref, rwh_ref,
                                         rwl_ref, rb_ref, x1_ref, h2_ref)
        routed = _post_route(rs, prev_logits[rs, :], routed, idx_ref, gate_ref, rank_ref)
    counts = jnp.where(i > 0, routed, counts)
    cnt_ref[...] = counts
    cnt_out_ref[...] = counts.astype(jnp.int32)


def _post_mix(rs, merged_ref, x_ref, mod_ref, wo_ref, g_ref, rwh_ref, rwl_ref, rb_ref, x1_ref,
              h2_ref):
    mod = mod_ref[...]
    mix = _dot(merged_ref[rs, :], wo_ref[...])
    x1 = x_ref[rs, :] + mod[2:3] * mix
    x1_ref[rs, :] = x1
    h2 = _rms(x1, g_ref[...]) * (1.0 + mod[4:5]) + mod[3:4]
    h2_ref[rs, :] = _pack_row(h2)

    h_hi = h2.astype(BF16)
    h_lo = (h2 - h_hi.astype(F32)).astype(BF16)
    return (_dot(h_hi, rwh_ref[...]) + _dot(h_lo, rwh_ref[...]) + _dot(h_hi, rwl_ref[...])
            + rb_ref[...])


def _post_route(rs, logits, counts, idx_ref, gate_ref, rank_ref):
    lane = lax.broadcasted_iota(jnp.int32, logits.shape, 1)
    work = logits
    vals, idxs = [], []
    for _ in range(TOP_K):
        mk = jnp.max(work, axis=-1, keepdims=True)
        ik = jnp.min(jnp.where(work == mk, lane, LANES), axis=-1, keepdims=True)
        vals.append(mk)
        idxs.append(ik)
        work = jnp.where(lane == ik, -jnp.inf, work)
    es = [jnp.exp(vk - vals[0]) for vk in vals]
    denom = es[0] + es[1] + es[2] + es[3]
    tm = logits.shape[0]
    chosen = jnp.zeros(logits.shape, F32)
    for kk in range(TOP_K):
        chosen = chosen + jnp.where(lane == idxs[kk], 1.0, 0.0)
    r_i = lax.broadcasted_iota(jnp.int32, (tm, tm), 0)
    c_i = lax.broadcasted_iota(jnp.int32, (tm, tm), 1)
    earlier = jnp.where(c_i < r_i, 1.0, 0.0).astype(BF16)
    before = _dot(earlier, chosen.astype(BF16)) + counts[0:1]

    idx_out = jnp.zeros(logits.shape, F32)
    gate_out = jnp.zeros(logits.shape, F32)
    rank_out = jnp.zeros(logits.shape, F32)
    for kk in range(TOP_K):
        rank_k = jnp.sum(jnp.where(lane == idxs[kk], before, 0.0), axis=-1, keepdims=True)
        idx_out = jnp.where(lane == kk, idxs[kk].astype(F32), idx_out)
        gate_out = jnp.where(lane == kk, es[kk] / denom, gate_out)
        rank_out = jnp.where(lane == kk, rank_k, rank_out)
    gate_ref[rs, :] = gate_out
    idx_ref[:, rs] = idx_out.T[0:8].astype(jnp.int32)
    rank_ref[:, rs] = rank_out.T[0:8].astype(jnp.int32)
    return counts + jnp.sum(chosen, axis=0, keepdims=True)


def _post(merged, x2, mod, wo, norm_g, rw_hi, rw_lo, rb_pad, seq):
    t = x2.shape[0]
    tm = POST_TILE
    tiles_per_seq = seq // tm
    n_tiles = t // tm
    full = lambda a: pl.BlockSpec(a.shape, lambda i: (0,) * a.ndim)
    mix_tile = lambda i: jnp.minimum(i, n_tiles - 1)
    route_tile = lambda i: jnp.maximum(i - 1, 0)
    rows = lambda w: pl.BlockSpec((tm, w), lambda i: (mix_tile(i), 0))
    outs = [jax.ShapeDtypeStruct((t, D_MODEL), F32),
            jax.ShapeDtypeStruct((t, PACKED), jnp.uint32),
            jax.ShapeDtypeStruct((8, t), jnp.int32),
            jax.ShapeDtypeStruct((t, LANES), F32),
            jax.ShapeDtypeStruct((8, t), jnp.int32),
            jax.ShapeDtypeStruct((8, LANES), jnp.int32)]
    slots = pl.BlockSpec((8, tm), lambda i: (0, route_tile(i)))
    return pl.pallas_call(
        _post_kernel,
        out_shape=outs,
        grid=(n_tiles + 1,),
        in_specs=[
            rows(D_MODEL), rows(D_MODEL),
            pl.BlockSpec((None, 8, D_MODEL), lambda i: (mix_tile(i) // tiles_per_seq, 0, 0)),
            full(wo), full(norm_g), full(rw_hi), full(rw_lo), full(rb_pad),
        ],
        out_specs=[rows(D_MODEL), rows(PACKED), slots,
                   pl.BlockSpec((tm, LANES), lambda i: (route_tile(i), 0)), slots,
                   pl.BlockSpec((8, LANES), lambda i: (0, 0))],
        scratch_shapes=[pltpu.VMEM((8, LANES), F32), pltpu.VMEM((2, tm, LANES), F32)],
        compiler_params=pltpu.CompilerParams(
            dimension_semantics=("arbitrary",), vmem_limit_bytes=VMEM_LIMIT),
        name="post_mixer",
    )(merged, x2, mod, wo, norm_g, rw_hi, rw_lo, rb_pad)


_GM_COUNT, _GM_FIRST, _GM_BLOCKS = range(3)
_ST_EXPERT, _ST_SLOT = range(2)


def _moe_kernel(gm_ref, xs_ref, wgu_hbm, bgu_ref, wd_hbm, bd_ref, o_ref,
                wgu_f, wd_f, wgu_bf, wd_bf, sem, st_ref):
    weights = (wgu_hbm, wd_hbm, wgu_f, wd_f, wgu_bf, wd_bf, sem)
    rows = (xs_ref, bgu_ref, bd_ref, o_ref, wgu_bf, wd_bf)
    b0 = pl.program_id(0) * MOE_STEP_BLOCKS
    e, in_group, used = _moe_enter(b0, gm_ref, st_ref, *weights)
    together = jnp.logical_and(used, in_group + MOE_STEP_BLOCKS <= gm_ref[_GM_BLOCKS, e])

    @pl.when(together)
    def _():
        _moe_rows(slice(0, MOE_STEP_BLOCKS * MOE_BLOCK), e, in_group, True, gm_ref, *rows)

    @pl.when(jnp.logical_not(together))
    def _():
        _moe_rows(slice(0, MOE_BLOCK), e, in_group, used, gm_ref, *rows)
        for r in range(1, MOE_STEP_BLOCKS):
            e_r, in_group_r, used_r = _moe_enter(b0 + r, gm_ref, st_ref, *weights)
            _moe_rows(slice(r * MOE_BLOCK, (r + 1) * MOE_BLOCK), e_r, in_group_r, used_r, gm_ref,
                      *rows)


def _moe_enter(b, gm_ref, st_ref, wgu_hbm, wd_hbm, wgu_f, wd_f, wgu_bf, wd_bf, sem):
    def weight_copies(expert, sl):
        return (pltpu.make_async_copy(wgu_hbm.at[expert], wgu_f.at[sl], sem.at[0, sl]),
                pltpu.make_async_copy(wd_hbm.at[expert], wd_f.at[sl], sem.at[1, sl]))

    def next_group(e):
        return lax.while_loop(
            lambda k: jnp.logical_and(k < N_EXPERTS,
                                      gm_ref[_GM_BLOCKS, jnp.minimum(k, N_EXPERTS - 1)] == 0),
            lambda k: k + 1, e)

    @pl.when(b == 0)
    def _():
        e0 = next_group(0)
        st_ref[_ST_EXPERT] = e0
        st_ref[_ST_SLOT] = 1
        for cp in weight_copies(e0, 0):
            cp.start()

    e_prev = st_ref[_ST_EXPERT]
    past = b >= gm_ref[_GM_FIRST, e_prev] + gm_ref[_GM_BLOCKS, e_prev]
    e = jnp.minimum(jnp.where(past, next_group(e_prev + 1), e_prev), N_EXPERTS - 1)
    st_ref[_ST_EXPERT] = e
    in_group = b - gm_ref[_GM_FIRST, e]
    used = jnp.logical_and(in_group >= 0, in_group < gm_ref[_GM_BLOCKS, e])

    @pl.when(jnp.logical_and(used, in_group == 0))
    def _():
        slot = 1 - st_ref[_ST_SLOT]
        st_ref[_ST_SLOT] = slot
        nxt = next_group(e + 1)

        @pl.when(nxt < N_EXPERTS)
        def _():
            for cp in weight_copies(nxt, 1 - slot):
                cp.start(priority=1)

        for cp in weight_copies(e, slot):
            cp.wait()
        wgu_bf[...] = wgu_f[slot].astype(BF16)
        wd_bf[...] = wd_f[slot].astype(BF16)

    return e, in_group, used


def _moe_rows(rs, e, in_group, used, gm_ref, xs_ref, bgu_ref, bd_ref, o_ref, wgu_bf, wd_bf):
    n_valid = gm_ref[_GM_COUNT, e] - in_group * MOE_BLOCK

    def ffn(r):
        n = r.stop - r.start
        row = lax.broadcasted_iota(jnp.int32, (n, PACKED), 0)
        xs = _unpack_row(jnp.where(row < n_valid, xs_ref[r, :], 0)).astype(BF16)
        gu = _dot(xs, wgu_bf[...]) + bgu_ref[e]
        gate = jnp.minimum(gu[:, :D_EXPERT], SWIGLU_LIMIT)
        up = jnp.clip(gu[:, D_EXPERT:], -SWIGLU_LIMIT, SWIGLU_LIMIT)
        act = (up + 1.0) * (gate * jax.nn.sigmoid(gate * SWIGLU_ALPHA))
        o_ref[r, :] = _pack_row(_dot(act.astype(BF16), wd_bf[...]) + bd_ref[e])

    def zeros(r):
        o_ref[r, :] = jnp.zeros((r.stop - r.start, PACKED), o_ref.dtype)

    if used is True:
        ffn(rs)
        return

    mid = rs.start + (rs.stop - rs.start) // 2
    short = n_valid <= mid - rs.start

    @pl.when(jnp.logical_and(used, jnp.logical_not(short)))
    def _():
        ffn(rs)

    @pl.when(jnp.logical_and(used, short))
    def _():
        ffn(slice(rs.start, mid))
        zeros(slice(mid, rs.stop))

    @pl.when(jnp.logical_not(used))
    def _():
        zeros(rs)


def _moe(group_table, xs, w_gu, b_gu, w_down, b_down):
    n_rows = xs.shape[0]
    step_rows = MOE_STEP_BLOCKS * MOE_BLOCK
    assert n_rows % step_rows == 0
    grid_spec = pltpu.PrefetchScalarGridSpec(
        num_scalar_prefetch=1,
        grid=(n_rows // step_rows,),
        in_specs=[
            pl.BlockSpec((step_rows, PACKED), lambda b, gm: (b, 0)),
            pl.BlockSpec(memory_space=pl.ANY),
            pl.BlockSpec(b_gu.shape, lambda b, gm: (0, 0, 0)),
            pl.BlockSpec(memory_space=pl.ANY),
            pl.BlockSpec(b_down.shape, lambda b, gm: (0, 0, 0)),
        ],
        out_specs=pl.BlockSpec((step_rows, PACKED), lambda b, gm: (b, 0)),
        scratch_shapes=[pltpu.VMEM((2, D_MODEL, 2 * D_EXPERT), F32),
                        pltpu.VMEM((2, D_EXPERT, D_MODEL), F32),
                        pltpu.VMEM((D_MODEL, 2 * D_EXPERT), BF16),
                        pltpu.VMEM((D_EXPERT, D_MODEL), BF16),
                        pltpu.SemaphoreType.DMA((2, 2)),
                        pltpu.SMEM((2,), jnp.int32)],
    )
    return pl.pallas_call(
        _moe_kernel,
        out_shape=jax.ShapeDtypeStruct((n_rows, PACKED), jnp.uint32),
        grid_spec=grid_spec,
        compiler_params=pltpu.CompilerParams(
            dimension_semantics=("arbitrary",), vmem_limit_bytes=VMEM_LIMIT),
        name="moe_experts",
    )(group_table, xs, w_gu, b_gu, w_down, b_down)


def _final_kernel(last_layer, x1_ref, y_ref, gate_ref, mod_ref, g_ref, o_ref):
    mod = mod_ref[...]
    gate = gate_ref[...]
    ffn = gate[:, 0:1] * _unpack_row(y_ref[0])
    for kk in range(1, TOP_K):
        ffn = ffn + gate[:, kk:kk + 1] * _unpack_row(y_ref[kk])
    x = x1_ref[...] + mod[5:6] * ffn
    o_ref[...] = _rms(x, g_ref[...]) if last_layer else x


def _final(x1, y_kt, gate, mod, norm_g, seq, last_layer):
    t = x1.shape[0]
    tm = ROW_TILE
    tiles_per_seq = seq // tm
    rows = lambda w: pl.BlockSpec((tm, w), lambda i: (i, 0))
    return pl.pallas_call(
        functools.partial(_final_kernel, last_layer),
        out_shape=jax.ShapeDtypeStruct((t, D_MODEL), F32),
        grid=(t // tm,),
        in_specs=[
            rows(D_MODEL), pl.BlockSpec((TOP_K, tm, PACKED), lambda i: (0, i, 0)), rows(LANES),
            pl.BlockSpec((None, 8, D_MODEL), lambda i: (i // tiles_per_seq, 0, 0)),
            pl.BlockSpec(norm_g.shape, lambda i: (0, 0)),
        ],
        out_specs=rows(D_MODEL),
        compiler_params=pltpu.CompilerParams(
            dimension_semantics=("arbitrary",), vmem_limit_bytes=VMEM_LIMIT),
        name="combine_final",
    )(x1, y_kt, gate, mod, norm_g)


def _prep_weights(w_in, w_uq, w_ukv):
    d = w_in.shape[0]
    splits = (Q_LORA, KV_LORA, QK_ROPE, CONV_WIDTH, CONV_WIDTH, CONV_WIDTH, D_MODEL, D_MODEL)
    offs = [0]
    for s in splits:
        offs.append(offs[-1] + s)
    part = lambda n: w_in[:, offs[n]:offs[n + 1]]
    z = lambda n: jnp.zeros((d, n), w_in.dtype)
    w_kpe = part(2)
    kpe_slab = jnp.concatenate([z(QK_NOPE), w_kpe, z(HEAD_PAD - QK_HEAD)], axis=1)
    w_lat = jnp.concatenate([part(0), part(1), kpe_slab], axis=1).astype(BF16)
    w_conv = w_in[:, offs[3]:offs[6]].astype(BF16)
    w_gate = w_in[:, offs[6]:offs[8]].astype(BF16)

    wq = w_uq.reshape(Q_LORA, N_HEADS, QK_HEAD)
    zq = lambda n: jnp.zeros((Q_LORA, N_HEADS, n), w_uq.dtype)
    wq2 = jnp.concatenate([wq, zq(HEAD_PAD - QK_HEAD)], axis=-1)
    wq2 = wq2.reshape(Q_LORA, N_HEADS * HEAD_PAD).astype(BF16)

    wkv = w_ukv.reshape(KV_LORA, N_HEADS, QK_NOPE + V_HEAD)
    wk2 = jnp.concatenate([wkv[..., :QK_NOPE],
                           jnp.zeros((KV_LORA, N_HEADS, HEAD_PAD - QK_NOPE), w_ukv.dtype)], axis=-1)
    wk2 = wk2.reshape(KV_LORA, N_HEADS * HEAD_PAD).astype(BF16)
    wv = jnp.concatenate([wkv[..., QK_NOPE:],
                          jnp.zeros((KV_LORA, N_HEADS, HEAD_PAD - V_HEAD), w_ukv.dtype)], axis=-1)
    wv = wv.reshape(KV_LORA, N_HEADS * HEAD_PAD).astype(BF16)
    return w_lat, w_conv, w_gate, wq2, wk2, wv


def _rope_freqs():
    inv_freq = 1.0 / (ROPE_THETA ** (jnp.arange(0, QK_ROPE, 2, dtype=F32) / QK_ROPE))
    return inv_freq.reshape(QK_ROPE // 2, 1)


def _dest_kernel(gm_ref, idx_ref, rank_ref, o_ref):
    idx = idx_ref[...]
    dest = rank_ref[...]
    for e in range(N_EXPERTS):
        dest = dest + jnp.where(idx == e, gm_ref[_GM_FIRST, e] * MOE_BLOCK, 0)
    o_ref[...] = dest


def _route(top_idx, rank, counts, n_tokens):
    blocks = (counts + MOE_BLOCK - 1) // MOE_BLOCK
    first_block = jnp.cumsum(blocks) - blocks
    table = jnp.stack([counts, first_block, blocks]).astype(jnp.int32)
    whole = pl.BlockSpec(top_idx.shape, lambda i, gm: (0, 0))
    dest = pl.pallas_call(
        _dest_kernel,
        out_shape=jax.ShapeDtypeStruct(rank.shape, jnp.int32),
        grid_spec=pltpu.PrefetchScalarGridSpec(
            num_scalar_prefetch=1, grid=(1,), in_specs=[whole, whole], out_specs=whole),
        name="row_destinations",
    )(table, top_idx, rank)[:TOP_K]
    n_rows = n_tokens * TOP_K + N_EXPERTS * MOE_BLOCK
    return dest, table, n_rows


SC_CORES = 2
SC_SUBCORES = 16
SC_WORKERS = SC_CORES * SC_SUBCORES
SC_CHUNK = 64
SC_GATHER_RING = 3

def _sc_mesh():
    return plsc.VectorSubcoreMesh(core_axis_name="c", subcore_axis_name="s")


def _sc_worker():
    return lax.axis_index("s") * SC_CORES + lax.axis_index("c")


def _dispatch(h2, dest, n_rows):
    t, d = h2.shape
    per_w = t // SC_WORKERS
    n_chunks = per_w // SC_CHUNK
    assert per_w % (2 * SC_CHUNK) == 0
    idx = dest.reshape(TOP_K, SC_WORKERS, n_chunks, SC_CHUNK).transpose(1, 0, 2, 3)
    idx = idx.reshape(SC_WORKERS, TOP_K * n_chunks, SC_CHUNK)

    @functools.partial(
        pl.kernel, mesh=_sc_mesh(),
        out_type=jax.ShapeDtypeStruct((n_rows, d), h2.dtype),
        scratch_types=[pltpu.VMEM((TOP_K * n_chunks, SC_CHUNK), jnp.int32),
                       pltpu.VMEM((2, SC_CHUNK, d), h2.dtype),
                       pltpu.SemaphoreType.DMA((2,)),
                       pltpu.SemaphoreType.DMA((2,))],
        name="moe_dispatch")
    def run(h2_hbm, idx_hbm, xs_hbm, idx_v, rows_v, rsem, ssem):
        w = _sc_worker()
        pltpu.sync_copy(idx_hbm.at[w], idx_v)

        def read(g, b):
            src = h2_hbm.at[pl.ds(w * per_w + g * SC_CHUNK, SC_CHUNK)]
            return pltpu.make_async_copy(src, rows_v.at[b], rsem.at[b])

        def scatter(g, kk, b):
            dst = xs_hbm.at[idx_v.at[kk * n_chunks + g]]
            return pltpu.make_async_copy(rows_v.at[b], dst, ssem.at[b])

        read(0, 0).start()

        @pl.loop(0, n_chunks, step=2)
        def _(g0):
            for b in range(2):
                g = g0 + b
                read(g, b).wait()

                @pl.when(g + 1 < n_chunks)
                def _():
                    read(g + 1, 1 - b).start()

                for kk in range(TOP_K):
                    scatter(g, kk, b).start()
                for kk in range(TOP_K):
                    scatter(g, kk, b).wait()

    return run(h2, idx)


def _undispatch(ys, dest):
    t = dest.shape[1]
    d = ys.shape[1]
    n_out = t * TOP_K
    per_w = n_out // SC_WORKERS
    n_chunks = per_w // SC_CHUNK
    idx = dest.reshape(SC_WORKERS, n_chunks, SC_CHUNK)
    ring = SC_GATHER_RING

    @functools.partial(
        pl.kernel, mesh=_sc_mesh(),
        out_type=jax.ShapeDtypeStruct((n_out, d), ys.dtype),
        scratch_types=[pltpu.VMEM((n_chunks, SC_CHUNK), jnp.int32),
                       pltpu.VMEM((ring, SC_CHUNK, d), ys.dtype),
                       pltpu.SemaphoreType.DMA((ring,)),
                       pltpu.SemaphoreType.DMA((ring,))],
        name="moe_undispatch")
    def run(ys_hbm, idx_hbm, out_hbm, idx_v, rows_v, gsem, wsem):
        w = _sc_worker()
        pltpu.sync_copy(idx_hbm.at[w], idx_v)

        def gather(g):
            b = g % ring
            return pltpu.make_async_copy(ys_hbm.at[idx_v.at[g]], rows_v.at[b], gsem.at[b])

        def write(g):
            b = g % ring
            dst = out_hbm.at[pl.ds(w * per_w + g * SC_CHUNK, SC_CHUNK)]
            return pltpu.make_async_copy(rows_v.at[b], dst, wsem.at[b])

        for g in range(min(ring - 1, n_chunks)):
            gather(g).start()
        for g in range(n_chunks):
            gather(g).wait()
            ahead = g + ring - 1
            if ahead < n_chunks:
                if g >= 1:
                    write(g - 1).wait()
                gather(ahead).start()
            write(g).start()
        for g in range(max(n_chunks - ring, 0), n_chunks):
            write(g).wait()

    return run(ys, idx).reshape(TOP_K, t, d)


def kernel(x, c, positions, w_ada, b_ada, norm_mix_g, w_in, q_norm_g, w_uq, kv_norm_g, w_ukv,
           w_up_attn, conv_w, w_up_conv, w_o, norm_ffn_g, router_w, router_b, w_gu, b_gu,
           w_down, b_down, norm_final_g):
    batch, seq, d = x.shape
    t = batch * seq
    depth = w_ada.shape[0]
    assert d == D_MODEL and batch <= 8 and conv_w.shape[1:] == (CONV_K, CONV_WIDTH)
    assert seq % ROW_TILE == 0 and seq % POST_TILE == 0 and seq % ATT_BLOCK == 0
    assert t % (2 * SC_CHUNK * SC_WORKERS) == 0
    x2 = x.reshape(t, d)
    pos = positions.astype(F32).reshape(t // ROW_TILE, 1, ROW_TILE)
    freqs = _rope_freqs()
    c_pad = jnp.zeros((8, d), F32).at[:batch].set(c)

    for l in range(depth):
        ada = _ada(c_pad, w_ada[l], b_ada[l].reshape(1, -1))
        mod = ada[:batch].reshape(batch, 6, d)
        mod = jnp.concatenate([mod, jnp.zeros((batch, 2, d), F32)], axis=1)

        w_lat, w_conv, w_gate, wq2, wk2, wv = _prep_weights(w_in[l], w_uq[l], w_ukv[l])
        q, k, v, sga, gc = _pre(x2, mod, norm_mix_g[l].reshape(1, d), w_lat, w_conv, w_gate,
                                q_norm_g[l].reshape(1, -1), wq2,
                                kv_norm_g[l].reshape(1, -1), wk2, wv, pos, freqs, conv_w[l],
                                w_up_conv[l].astype(BF16), seq)
        merged = _attention(q, k, v, sga, gc, w_up_attn[l].astype(BF16), batch, seq)

        rw_pad = jnp.concatenate([router_w[l], jnp.zeros((d, LANES - N_EXPERTS), F32)], axis=1)
        rb_pad = jnp.concatenate([router_b[l], jnp.full((LANES - N_EXPERTS,), NEG_BIG, F32)])
        rw_hi = rw_pad.astype(BF16)
        rw_lo = (rw_pad - rw_hi.astype(F32)).astype(BF16)
        x1, h2, idx_pad, gate_pad, rank_pad, counts = _post(
            merged, x2, mod, w_o[l].astype(BF16), norm_ffn_g[l].reshape(1, d), rw_hi, rw_lo,
            rb_pad.reshape(1, LANES), seq)

        dest, group_table, n_rows = _route(
            idx_pad, rank_pad, counts[0, :N_EXPERTS], t)
        xs = _dispatch(h2, dest, n_rows)
        ys = _moe(group_table, xs, w_gu[l], b_gu[l].reshape(N_EXPERTS, 1, -1),
                  w_down[l], b_down[l].reshape(N_EXPERTS, 1, -1))
        y_kt = _undispatch(ys, dest)
        x2 = _final(x1, y_kt, gate_pad, mod, norm_final_g.reshape(1, d), seq, l == depth - 1)

    return x2.reshape(batch, seq, d)
```

```python
import functools
import math

import jax
import jax.numpy as jnp
from jax import lax
from jax.experimental import pallas as pl
from jax.experimental.pallas import tpu as pltpu
from jax.experimental.pallas import tpu_sc as plsc

D_MODEL = 1024
CHUNK = 64
N_HEADS = 8
Q_LORA = 256
KV_LORA = 128
QK_NOPE = 64
QK_ROPE = 32
V_HEAD = 64
QK_HEAD = QK_NOPE + QK_ROPE
ROPE_THETA = 10000.0
CONV_WIDTH = 512
CONV_K = 3
N_EXPERTS = 32
TOP_K = 4
D_EXPERT = 1024
SWIGLU_LIMIT = 7.0
SWIGLU_ALPHA = 1.702
MOE_BLOCK = 256
RMS_EPS = 1e-6

LANES = 128
HEAD_PAD = 128
NEG_BIG = -1e30
VMEM_LIMIT = 56 * 1024 * 1024

F32 = jnp.float32
BF16 = jnp.bfloat16

Q_PRESCALE = (QK_HEAD ** -0.5) * math.log2(math.e)

ROW_TILE = 1024
MOE_STEP_BLOCKS = 4
POST_TILE = 1024
POST_SUB = 512
ATT_BLOCK = 512
ATT_WIDE = 2


def _rms(x, g):
    ms = jnp.mean(x * x, axis=-1, keepdims=True)
    return x * lax.rsqrt(ms + RMS_EPS) * g


def _dot(a, b):
    return jnp.dot(a, b, preferred_element_type=F32)


PACKED = D_MODEL // 2


def _pack_row(x):
    return pltpu.pack_elementwise([x[:, :PACKED], x[:, PACKED:]], packed_dtype=BF16)


def _unpack_row(w):
    half = lambda i: pltpu.unpack_elementwise(w, index=i, packed_dtype=BF16, unpacked_dtype=F32)
    return jnp.concatenate([half(0), half(1)], axis=-1)


def _ada_kernel(c_ref, w_ref, b_ref, o_ref):
    c = c_ref[...]
    ca = (c * jax.nn.sigmoid(c)).astype(BF16)
    o_ref[...] = _dot(ca, w_ref[...].astype(BF16)) + b_ref[...]


def _ada(c_pad, w_ada, b_ada):
    n = w_ada.shape[1]
    tn = 1024
    return pl.pallas_call(
        _ada_kernel,
        out_shape=jax.ShapeDtypeStruct((c_pad.shape[0], n), F32),
        grid=(n // tn,),
        in_specs=[
            pl.BlockSpec(c_pad.shape, lambda j: (0, 0)),
            pl.BlockSpec((D_MODEL, tn), lambda j: (0, j)),
            pl.BlockSpec((1, tn), lambda j: (0, j)),
        ],
        out_specs=pl.BlockSpec((c_pad.shape[0], tn), lambda j: (0, j)),
        compiler_params=pltpu.CompilerParams(
            dimension_semantics=("arbitrary",), vmem_limit_bytes=VMEM_LIMIT),
        name="ada",
    )(c_pad, w_ada, b_ada)


_C_QLAT = 0
_C_KVLAT = _C_QLAT + Q_LORA
_C_KPE = _C_KVLAT + KV_LORA
_C_END = _C_KPE + HEAD_PAD


def _pre_kernel(tiles_per_seq, x_ref, mod_ref, g_ref, wlat_ref, wconv_ref, wgate_ref, qg_ref,
                wq_ref, kvg_ref, wk_ref, wv_ref, pos_ref, freq_ref, cw_ref, wuc_ref,
                q_ref, k_ref, v_ref, sga_ref, gc_ref, carry_ref):
    i = pl.program_id(0)
    tm = x_ref.shape[0]
    mod = mod_ref[...]
    h = _rms(x_ref[...], g_ref[...]) * (1.0 + mod[1:2]) + mod[0:1]
    hb = h.astype(BF16)

    ang = freq_ref[...] * pos_ref[...]
    cos_t, sin_t = jnp.cos(ang), jnp.sin(ang)
    ones_t = jnp.ones((QK_NOPE, tm), F32)
    zeros_t = jnp.zeros((QK_NOPE, tm), F32)
    pad_t = jnp.zeros((HEAD_PAD - QK_HEAD, tm), F32)
    cosf = jnp.concatenate([ones_t, cos_t, cos_t, pad_t], axis=0).T
    sinf = jnp.concatenate([zeros_t, -sin_t, sin_t, pad_t], axis=0).T

    first_half = lax.broadcasted_iota(jnp.int32, (tm, HEAD_PAD), 1) < QK_NOPE + QK_ROPE // 2

    def rope(slab):
        swapped = jnp.where(first_half, pltpu.roll(slab, HEAD_PAD - QK_ROPE // 2, 1),
                            pltpu.roll(slab, QK_ROPE // 2, 1))
        return slab * cosf + swapped * sinf

    small = _dot(hb, wlat_ref[...])
    q_lat = small[:, _C_QLAT:_C_KVLAT]
    kv_lat = small[:, _C_KVLAT:_C_KPE]
    kpe = rope(small[:, _C_KPE:_C_END])
    qn = _rms(q_lat, qg_ref[...]).astype(BF16)
    q = _dot(qn, wq_ref[...])
    q = jnp.concatenate([rope(q[:, hd * HEAD_PAD:(hd + 1) * HEAD_PAD]) for hd in range(N_HEADS)],
                        axis=-1)
    q_ref[...] = (q * Q_PRESCALE).astype(BF16)
    kvn = _rms(kv_lat, kvg_ref[...]).astype(BF16)
    k = _dot(kvn, wk_ref[...]) + jnp.concatenate([kpe] * N_HEADS, axis=-1)
    k_ref[...] = k.astype(BF16)
    lane = lax.broadcasted_iota(jnp.int32, (tm, N_HEADS * HEAD_PAD), 1)
    ones_col = jnp.where(lane % HEAD_PAD == V_HEAD, 1.0, 0.0)
    v_ref[...] = (_dot(kvn, wv_ref[...]) + ones_col).astype(BF16)

    ucb = _dot(hb, wconv_ref[...])
    cu = ucb[:, 0:CONV_WIDTH] * ucb[:, CONV_WIDTH:2 * CONV_WIDTH]
    b_gate = ucb[:, 2 * CONV_WIDTH:3 * CONV_WIDTH]

    @pl.when(i % tiles_per_seq == 0)
    def _():
        carry_ref[...] = jnp.zeros_like(carry_ref)

    prev = carry_ref[...]
    row = lax.broadcasted_iota(jnp.int32, cu.shape, 0)
    cu1 = jnp.where(row == 0, prev[7:8], pltpu.roll(cu, 1, 0))
    cu2 = jnp.where(row == 0, prev[6:7], jnp.where(row == 1, prev[7:8], pltpu.roll(cu, 2, 0)))
    cw = cw_ref[...]
    z = cw[2:3] * cu + cw[1:2] * cu1 + cw[0:1] * cu2
    carry_ref[...] = cu[tm - 8:tm]
    c_branch = _dot((b_gate * z).astype(BF16), wuc_ref[...])

    gates = _dot(hb, wgate_ref[...])
    sga_ref[...] = jax.nn.sigmoid(gates[:, 0:D_MODEL]).astype(BF16)
    gc_ref[...] = (jax.nn.sigmoid(gates[:, D_MODEL:]) * c_branch).astype(BF16)


def _pre(x2, mod, norm_g, w_lat, w_conv, w_gate, q_norm_g, wq2, kv_norm_g, wk2, wv, pos,
         freqs, conv_w, w_up_conv, seq):
    t = x2.shape[0]
    tm = ROW_TILE
    tiles_per_seq = seq // tm
    full = lambda a: pl.BlockSpec(a.shape, lambda i: (0,) * a.ndim)
    rows = lambda w: pl.BlockSpec((tm, w), lambda i: (i, 0))
    outs = [jax.ShapeDtypeStruct((t, N_HEADS * HEAD_PAD), BF16),
            jax.ShapeDtypeStruct((t, N_HEADS * HEAD_PAD), BF16),
            jax.ShapeDtypeStruct((t, N_HEADS * HEAD_PAD), BF16),
            jax.ShapeDtypeStruct((t, D_MODEL), BF16),
            jax.ShapeDtypeStruct((t, D_MODEL), BF16)]
    return pl.pallas_call(
        functools.partial(_pre_kernel, tiles_per_seq),
        out_shape=outs,
        grid=(t // tm,),
        in_specs=[
            rows(D_MODEL),
            pl.BlockSpec((None, 8, D_MODEL), lambda i: (i // tiles_per_seq, 0, 0)),
            full(norm_g), full(w_lat), full(w_conv), full(w_gate), full(q_norm_g), full(wq2),
            full(kv_norm_g), full(wk2), full(wv),
            pl.BlockSpec((None, 1, tm), lambda i: (i, 0, 0)), full(freqs), full(conv_w),
            full(w_up_conv),
        ],
        out_specs=[rows(N_HEADS * HEAD_PAD), rows(N_HEADS * HEAD_PAD), rows(N_HEADS * HEAD_PAD),
                   rows(D_MODEL), rows(D_MODEL)],
        scratch_shapes=[pltpu.VMEM((8, CONV_WIDTH), F32)],
        compiler_params=pltpu.CompilerParams(
            dimension_semantics=("arbitrary",), vmem_limit_bytes=VMEM_LIMIT),
        name="pre_mixer",
    )(x2, mod, norm_g, w_lat, w_conv, w_gate, q_norm_g, wq2, kv_norm_g, wk2, wv, pos,
      freqs, conv_w, w_up_conv)


def _attn_kernel(n_blocks, blocks_per_seq, q_ref, k_ref, v_ref, sga_ref, gc_ref, wua_ref, o_ref,
                 m_ref, acc_ref, heads_ref):
    s = pl.program_id(0)
    live = s < n_blocks
    i = s % blocks_per_seq
    tq = q_ref.shape[0]

    @pl.when(s == 0)
    def _():
        heads_ref[...] = jnp.zeros_like(heads_ref)

    def flush_previous():
        a_branch = _dot(heads_ref[...], wua_ref[...])
        o_ref[...] = (sga_ref[...].astype(F32) * a_branch + gc_ref[...].astype(F32)).astype(BF16)

    def step(k0, tk, masked, first=False):
        if masked:
            rq = (lax.broadcasted_iota(jnp.int32, (tq, tk), 0) + (tk - tq)) // CHUNK
            ck = lax.broadcasted_iota(jnp.int32, (tq, tk), 1) // CHUNK
            allowed = ck <= rq
        for hd in range(N_HEADS):
            hs = slice(hd * HEAD_PAD, (hd + 1) * HEAD_PAD)
            s = lax.dot_general(q_ref[:, hs], k_ref[pl.ds(k0, tk), hs],
                                (((1,), (1,)), ((), ())), preferred_element_type=F32)
            if masked:
                s = jnp.where(allowed, s, NEG_BIG)
            s_max = s[:, 0:LANES]
            for c in range(1, tk // LANES):
                s_max = jnp.maximum(s_max, s[:, c * LANES:(c + 1) * LANES])
            m_new = jnp.broadcast_to(jnp.max(s_max, axis=-1, keepdims=True), (tq, LANES))
            if not first:
                m_old = m_ref[hd]
                m_new = jnp.maximum(m_old, m_new)
            p = jnp.concatenate(
                [jnp.exp2(s[:, c * LANES:(c + 1) * LANES] - m_new).astype(BF16)
                 for c in range(tk // LANES)], axis=-1)
            pv = _dot(p, v_ref[pl.ds(k0, tk), hs])
            acc_ref[hd] = pv if first else jnp.exp2(m_old - m_new) * acc_ref[hd] + pv
            m_ref[hd] = m_new

    wide = ATT_WIDE * tq
    n_wide = jnp.where(live, i // ATT_WIDE, 0)

    @pl.when(n_wide > 0)
    def _():
        step(0, wide, False, first=True)

    @pl.when(jnp.logical_and(live, n_wide == 0))
    def _():
        m_ref[...] = jnp.full_like(m_ref, NEG_BIG)
        acc_ref[...] = jnp.zeros_like(acc_ref)

    def body(j, carry):
        step(pl.multiple_of(j * wide, wide), wide, False)
        return carry

    lax.fori_loop(1, n_wide, body, 0)

    for r in range(ATT_WIDE):
        @pl.when(jnp.logical_and(live, i % ATT_WIDE == r))
        def _():
            step(pl.multiple_of((i - r) * tq, tq), (r + 1) * tq, True)
            flush_previous()

    @pl.when(jnp.logical_not(live))
    def _():
        flush_previous()

    @pl.when(live)
    def _():
        for hd in range(N_HEADS):
            acc = acc_ref[hd]
            heads_ref[:, hd * V_HEAD:(hd + 1) * V_HEAD] = (
                acc[:, 0:V_HEAD] / acc[:, V_HEAD:V_HEAD + 1]).astype(BF16)


def _attention(q, k, v, sga, gc, wua, batch, seq):
    tq = ATT_BLOCK
    nq = seq // tq
    n_blocks = batch * nq
    cur = lambda s: jnp.minimum(s, n_blocks - 1)
    prev_rows = pl.BlockSpec((tq, D_MODEL), lambda s: (jnp.maximum(s - 1, 0), 0))
    whole_seq = pl.BlockSpec((seq, N_HEADS * HEAD_PAD), lambda s: (cur(s) // nq, 0))
    return pl.pallas_call(
        functools.partial(_attn_kernel, n_blocks, nq),
        out_shape=jax.ShapeDtypeStruct((batch * seq, D_MODEL), BF16),
        grid=(n_blocks + 1,),
        in_specs=[pl.BlockSpec((tq, N_HEADS * HEAD_PAD), lambda s: (cur(s), 0)), whole_seq,
                  whole_seq, prev_rows, prev_rows, pl.BlockSpec(wua.shape, lambda s: (0, 0))],
        out_specs=prev_rows,
        scratch_shapes=[pltpu.VMEM((N_HEADS, tq, LANES), F32),
                        pltpu.VMEM((N_HEADS, tq, LANES), F32),
                        pltpu.VMEM((tq, N_HEADS * V_HEAD), BF16)],
        compiler_params=pltpu.CompilerParams(
            dimension_semantics=("arbitrary",), vmem_limit_bytes=VMEM_LIMIT),
        name="attention",
    )(q, k, v, sga, gc, wua)


def _post_kernel(merged_ref, x_ref, mod_ref, wo_ref, g_ref, rwh_ref, rwl_ref, rb_ref,
                 x1_ref, h2_ref, idx_ref, gate_ref, rank_ref, cnt_out_ref, cnt_ref, lg_ref):
    i = pl.program_id(0)

    @pl.when(i == 0)
    def _():
        cnt_ref[...] = jnp.zeros_like(cnt_ref)
        lg_ref[...] = jnp.zeros_like(lg_ref)

    prev_logits = lg_ref[(i + 1) % 2]
    counts = cnt_ref[...]
    routed = counts
    for r0 in range(0, x_ref.shape[0], POST_SUB):
        rs = slice(r0, r0 + POST_SUB)
        lg_ref[i % 2, rs, :] = _post_mix(rs, merged_ref, x_ref, mod_ref, wo_ref, g_ref, rwh_ref,
                                         rwl_ref, rb_ref, x1_ref, h2_ref)
        routed = _post_route(rs, prev_logits[rs, :], routed, idx_ref, gate_ref, rank_ref)
    counts = jnp.where(i > 0, routed, counts)
    cnt_ref[...] = counts
    cnt_out_ref[...] = counts.astype(jnp.int32)


def _post_mix(rs, merged_ref, x_ref, mod_ref, wo_ref, g_ref, rwh_ref, rwl_ref, rb_ref, x1_ref,
              h2_ref):
    mod = mod_ref[...]
    mix = _dot(merged_ref[rs, :], wo_ref[...])
    x1 = x_ref[rs, :] + mod[2:3] * mix
    x1_ref[rs, :] = x1
    h2 = _rms(x1, g_ref[...]) * (1.0 + mod[4:5]) + mod[3:4]
    h2_ref[rs, :] = _pack_row(h2)

    h_hi = h2.astype(BF16)
    h_lo = (h2 - h_hi.astype(F32)).astype(BF16)
    return (_dot(h_hi, rwh_ref[...]) + _dot(h_lo, rwh_ref[...]) + _dot(h_hi, rwl_ref[...])
            + rb_ref[...])


def _post_route(rs, logits, counts, idx_ref, gate_ref, rank_ref):
    lane = lax.broadcasted_iota(jnp.int32, logits.shape, 1)
    work = logits
    vals, idxs = [], []
    for _ in range(TOP_K):
        mk = jnp.max(work, axis=-1, keepdims=True)
        ik = jnp.min(jnp.where(work == mk, lane, LANES), axis=-1, keepdims=True)
        vals.append(mk)
        idxs.append(ik)
        work = jnp.where(lane == ik, -jnp.inf, work)
    es = [jnp.exp(vk - vals[0]) for vk in vals]
    denom = es[0] + es[1] + es[2] + es[3]
    tm = logits.shape[0]
    chosen = jnp.zeros(logits.shape, F32)
    for kk in range(TOP_K):
        chosen = chosen + jnp.where(lane == idxs[kk], 1.0, 0.0)
    r_i = lax.broadcasted_iota(jnp.int32, (tm, tm), 0)
    c_i = lax.broadcasted_iota(jnp.int32, (tm, tm), 1)
    earlier = jnp.where(c_i < r_i, 1.0, 0.0).astype(BF16)
    before = _dot(earlier, chosen.astype(BF16)) + counts[0:1]

    idx_out = jnp.zeros(logits.shape, F32)
    gate_out = jnp.zeros(logits.shape, F32)
    rank_out = jnp.zeros(logits.shape, F32)
    for kk in range(TOP_K):
        rank_k = jnp.sum(jnp.where(lane == idxs[kk], before, 0.0), axis=-1, keepdims=True)
        idx_out = jnp.where(lane == kk, idxs[kk].astype(F32), idx_out)
        gate_out = jnp.where(lane == kk, es[kk] / denom, gate_out)
        rank_out = jnp.where(lane == kk, rank_k, rank_out)
    gate_ref[rs, :] = gate_out
    idx_ref[:, rs] = idx_out.T[0:8].astype(jnp.int32)
    rank_ref[:, rs] = rank_out.T[0:8].astype(jnp.int32)
    return counts + jnp.sum(chosen, axis=0, keepdims=True)


def _post(merged, x2, mod, wo, norm_g, rw_hi, rw_lo, rb_pad, seq):
    t = x2.shape[0]
    tm = POST_TILE
    tiles_per_seq = seq // tm
    n_tiles = t // tm
    full = lambda a: pl.BlockSpec(a.shape, lambda i: (0,) * a.ndim)
    mix_tile = lambda i: jnp.minimum(i, n_tiles - 1)
    route_tile = lambda i: jnp.maximum(i - 1, 0)
    rows = lambda w: pl.BlockSpec((tm, w), lambda i: (mix_tile(i), 0))
    outs = [jax.ShapeDtypeStruct((t, D_MODEL), F32),
            jax.ShapeDtypeStruct((t, PACKED), jnp.uint32),
            jax.ShapeDtypeStruct((8, t), jnp.int32),
            jax.ShapeDtypeStruct((t, LANES), F32),
            jax.ShapeDtypeStruct((8, t), jnp.int32),
            jax.ShapeDtypeStruct((8, LANES), jnp.int32)]
    slots = pl.BlockSpec((8, tm), lambda i: (0, route_tile(i)))
    return pl.pallas_call(
        _post_kernel,
        out_shape=outs,
        grid=(n_tiles + 1,),
        in_specs=[
            rows(D_MODEL), rows(D_MODEL),
            pl.BlockSpec((None, 8, D_MODEL), lambda i: (mix_tile(i) // tiles_per_seq, 0, 0)),
            full(wo), full(norm_g), full(rw_hi), full(rw_lo), full(rb_pad),
        ],
        out_specs=[rows(D_MODEL), rows(PACKED), slots,
                   pl.BlockSpec((tm, LANES), lambda i: (route_tile(i), 0)), slots,
                   pl.BlockSpec((8, LANES), lambda i: (0, 0))],
        scratch_shapes=[pltpu.VMEM((8, LANES), F32), pltpu.VMEM((2, tm, LANES), F32)],
        compiler_params=pltpu.CompilerParams(
            dimension_semantics=("arbitrary",), vmem_limit_bytes=VMEM_LIMIT),
        name="post_mixer",
    )(merged, x2, mod, wo, norm_g, rw_hi, rw_lo, rb_pad)


_GM_COUNT, _GM_FIRST, _GM_BLOCKS = range(3)
_ST_EXPERT, _ST_SLOT = range(2)


def _moe_kernel(gm_ref, xs_ref, wgu_hbm, bgu_ref, wd_hbm, bd_ref, o_ref,
                wgu_f, wd_f, wgu_bf, wd_bf, sem, st_ref):
    weights = (wgu_hbm, wd_hbm, wgu_f, wd_f, wgu_bf, wd_bf, sem)
    rows = (xs_ref, bgu_ref, bd_ref, o_ref, wgu_bf, wd_bf)
    b0 = pl.program_id(0) * MOE_STEP_BLOCKS
    e, in_group, used = _moe_enter(b0, gm_ref, st_ref, *weights)
    together = jnp.logical_and(used, in_group + MOE_STEP_BLOCKS <= gm_ref[_GM_BLOCKS, e])

    @pl.when(together)
    def _():
        _moe_rows(slice(0, MOE_STEP_BLOCKS * MOE_BLOCK), e, in_group, True, gm_ref, *rows)

    @pl.when(jnp.logical_not(together))
    def _():
        _moe_rows(slice(0, MOE_BLOCK), e, in_group, used, gm_ref, *rows)
        for r in range(1, MOE_STEP_BLOCKS):
            e_r, in_group_r, used_r = _moe_enter(b0 + r, gm_ref, st_ref, *weights)
            _moe_rows(slice(r * MOE_BLOCK, (r + 1) * MOE_BLOCK), e_r, in_group_r, used_r, gm_ref,
                      *rows)


def _moe_enter(b, gm_ref, st_ref, wgu_hbm, wd_hbm, wgu_f, wd_f, wgu_bf, wd_bf, sem):
    def weight_copies(expert, sl):
        return (pltpu.make_async_copy(wgu_hbm.at[expert], wgu_f.at[sl], sem.at[0, sl]),
                pltpu.make_async_copy(wd_hbm.at[expert], wd_f.at[sl], sem.at[1, sl]))

    def next_group(e):
        return lax.while_loop(
            lambda k: jnp.logical_and(k < N_EXPERTS,
                                      gm_ref[_GM_BLOCKS, jnp.minimum(k, N_EXPERTS - 1)] == 0),
            lambda k: k + 1, e)

    @pl.when(b == 0)
    def _():
        e0 = next_group(0)
        st_ref[_ST_EXPERT] = e0
        st_ref[_ST_SLOT] = 1
        for cp in weight_copies(e0, 0):
            cp.start()

    e_prev = st_ref[_ST_EXPERT]
    past = b >= gm_ref[_GM_FIRST, e_prev] + gm_ref[_GM_BLOCKS, e_prev]
    e = jnp.minimum(jnp.where(past, next_group(e_prev + 1), e_prev), N_EXPERTS - 1)
    st_ref[_ST_EXPERT] = e
    in_group = b - gm_ref[_GM_FIRST, e]
    used = jnp.logical_and(in_group >= 0, in_group < gm_ref[_GM_BLOCKS, e])

    @pl.when(jnp.logical_and(used, in_group == 0))
    def _():
        slot = 1 - st_ref[_ST_SLOT]
        st_ref[_ST_SLOT] = slot
        nxt = next_group(e + 1)

        @pl.when(nxt < N_EXPERTS)
        def _():
            for cp in weight_copies(nxt, 1 - slot):
                cp.start(priority=1)

        for cp in weight_copies(e, slot):
            cp.wait()
        wgu_bf[...] = wgu_f[slot].astype(BF16)
        wd_bf[...] = wd_f[slot].astype(BF16)

    return e, in_group, used


def _moe_rows(rs, e, in_group, used, gm_ref, xs_ref, bgu_ref, bd_ref, o_ref, wgu_bf, wd_bf):
    n_valid = gm_ref[_GM_COUNT, e] - in_group * MOE_BLOCK

    def ffn(r):
        n = r.stop - r.start
        row = lax.broadcasted_iota(jnp.int32, (n, PACKED), 0)
        xs = _unpack_row(jnp.where(row < n_valid, xs_ref[r, :], 0)).astype(BF16)
        gu = _dot(xs, wgu_bf[...]) + bgu_ref[e]
        gate = jnp.minimum(gu[:, :D_EXPERT], SWIGLU_LIMIT)
        up = jnp.clip(gu[:, D_EXPERT:], -SWIGLU_LIMIT, SWIGLU_LIMIT)
        act = (up + 1.0) * (gate * jax.nn.sigmoid(gate * SWIGLU_ALPHA))
        o_ref[r, :] = _pack_row(_dot(act.astype(BF16), wd_bf[...]) + bd_ref[e])

    def zeros(r):
        o_ref[r, :] = jnp.zeros((r.stop - r.start, PACKED), o_ref.dtype)

    if used is True:
        ffn(rs)
        return

    mid = rs.start + (rs.stop - rs.start) // 2
    short = n_valid <= mid - rs.start

    @pl.when(jnp.logical_and(used, jnp.logical_not(short)))
    def _():
        ffn(rs)

    @pl.when(jnp.logical_and(used, short))
    def _():
        ffn(slice(rs.start, mid))
        zeros(slice(mid, rs.stop))

    @pl.when(jnp.logical_not(used))
    def _():
        zeros(rs)


def _moe(group_table, xs, w_gu, b_gu, w_down, b_down):
    n_rows = xs.shape[0]
    step_rows = MOE_STEP_BLOCKS * MOE_BLOCK
    assert n_rows % step_rows == 0
    grid_spec = pltpu.PrefetchScalarGridSpec(
        num_scalar_prefetch=1,
        grid=(n_rows // step_rows,),
        in_specs=[
            pl.BlockSpec((step_rows, PACKED), lambda b, gm: (b, 0)),
            pl.BlockSpec(memory_space=pl.ANY),
            pl.BlockSpec(b_gu.shape, lambda b, gm: (0, 0, 0)),
            pl.BlockSpec(memory_space=pl.ANY),
            pl.BlockSpec(b_down.shape, lambda b, gm: (0, 0, 0)),
        ],
        out_specs=pl.BlockSpec((step_rows, PACKED), lambda b, gm: (b, 0)),
        scratch_shapes=[pltpu.VMEM((2, D_MODEL, 2 * D_EXPERT), F32),
                        pltpu.VMEM((2, D_EXPERT, D_MODEL), F32),
                        pltpu.VMEM((D_MODEL, 2 * D_EXPERT), BF16),
                        pltpu.VMEM((D_EXPERT, D_MODEL), BF16),
                        pltpu.SemaphoreType.DMA((2, 2)),
                        pltpu.SMEM((2,), jnp.int32)],
    )
    return pl.pallas_call(
        _moe_kernel,
        out_shape=jax.ShapeDtypeStruct((n_rows, PACKED), jnp.uint32),
        grid_spec=grid_spec,
        compiler_params=pltpu.CompilerParams(
            dimension_semantics=("arbitrary",), vmem_limit_bytes=VMEM_LIMIT),
        name="moe_experts",
    )(group_table, xs, w_gu, b_gu, w_down, b_down)


def _final_kernel(last_layer, x1_ref, y_ref, gate_ref, mod_ref, g_ref, o_ref):
    mod = mod_ref[...]
    gate = gate_ref[...]
    ffn = gate[:, 0:1] * _unpack_row(y_ref[0])
    for kk in range(1, TOP_K):
        ffn = ffn + gate[:, kk:kk + 1] * _unpack_row(y_ref[kk])
    x = x1_ref[...] + mod[5:6] * ffn
    o_ref[...] = _rms(x, g_ref[...]) if last_layer else x


def _final(x1, y_kt, gate, mod, norm_g, seq, last_layer):
    t = x1.shape[0]
    tm = ROW_TILE
    tiles_per_seq = seq // tm
    rows = lambda w: pl.BlockSpec((tm, w), lambda i: (i, 0))
    return pl.pallas_call(
        functools.partial(_final_kernel, last_layer),
        out_shape=jax.ShapeDtypeStruct((t, D_MODEL), F32),
        grid=(t // tm,),
        in_specs=[
            rows(D_MODEL), pl.BlockSpec((TOP_K, tm, PACKED), lambda i: (0, i, 0)), rows(LANES),
            pl.BlockSpec((None, 8, D_MODEL), lambda i: (i // tiles_per_seq, 0, 0)),
            pl.BlockSpec(norm_g.shape, lambda i: (0, 0)),
        ],
        out_specs=rows(D_MODEL),
        compiler_params=pltpu.CompilerParams(
            dimension_semantics=("arbitrary",), vmem_limit_bytes=VMEM_LIMIT),
        name="combine_final",
    )(x1, y_kt, gate, mod, norm_g)


def _prep_weights(w_in, w_uq, w_ukv):
    d = w_in.shape[0]
    splits = (Q_LORA, KV_LORA, QK_ROPE, CONV_WIDTH, CONV_WIDTH, CONV_WIDTH, D_MODEL, D_MODEL)
    offs = [0]
    for s in splits:
        offs.append(offs[-1] + s)
    part = lambda n: w_in[:, offs[n]:offs[n + 1]]
    z = lambda n: jnp.zeros((d, n), w_in.dtype)
    w_kpe = part(2)
    kpe_slab = jnp.concatenate([z(QK_NOPE), w_kpe, z(HEAD_PAD - QK_HEAD)], axis=1)
    w_lat = jnp.concatenate([part(0), part(1), kpe_slab], axis=1).astype(BF16)
    w_conv = w_in[:, offs[3]:offs[6]].astype(BF16)
    w_gate = w_in[:, offs[6]:offs[8]].astype(BF16)

    wq = w_uq.reshape(Q_LORA, N_HEADS, QK_HEAD)
    zq = lambda n: jnp.zeros((Q_LORA, N_HEADS, n), w_uq.dtype)
    wq2 = jnp.concatenate([wq, zq(HEAD_PAD - QK_HEAD)], axis=-1)
    wq2 = wq2.reshape(Q_LORA, N_HEADS * HEAD_PAD).astype(BF16)

    wkv = w_ukv.reshape(KV_LORA, N_HEADS, QK_NOPE + V_HEAD)
    wk2 = jnp.concatenate([wkv[..., :QK_NOPE],
                           jnp.zeros((KV_LORA, N_HEADS, HEAD_PAD - QK_NOPE), w_ukv.dtype)], axis=-1)
    wk2 = wk2.reshape(KV_LORA, N_HEADS * HEAD_PAD).astype(BF16)
    wv = jnp.concatenate([wkv[..., QK_NOPE:],
                          jnp.zeros((KV_LORA, N_HEADS, HEAD_PAD - V_HEAD), w_ukv.dtype)], axis=-1)
    wv = wv.reshape(KV_LORA, N_HEADS * HEAD_PAD).astype(BF16)
    return w_lat, w_conv, w_gate, wq2, wk2, wv


def _rope_freqs():
    inv_freq = 1.0 / (ROPE_THETA ** (jnp.arange(0, QK_ROPE, 2, dtype=F32) / QK_ROPE))
    return inv_freq.reshape(QK_ROPE // 2, 1)


def _dest_kernel(gm_ref, idx_ref, rank_ref, o_ref):
    idx = idx_ref[...]
    dest = rank_ref[...]
    for e in range(N_EXPERTS):
        dest = dest + jnp.where(idx == e, gm_ref[_GM_FIRST, e] * MOE_BLOCK, 0)
    o_ref[...] = dest


def _route(top_idx, rank, counts, n_tokens):
    blocks = (counts + MOE_BLOCK - 1) // MOE_BLOCK
    first_block = jnp.cumsum(blocks) - blocks
    table = jnp.stack([counts, first_block, blocks]).astype(jnp.int32)
    whole = pl.BlockSpec(top_idx.shape, lambda i, gm: (0, 0))
    dest = pl.pallas_call(
        _dest_kernel,
        out_shape=jax.ShapeDtypeStruct(rank.shape, jnp.int32),
        grid_spec=pltpu.PrefetchScalarGridSpec(
            num_scalar_prefetch=1, grid=(1,), in_specs=[whole, whole], out_specs=whole),
        name="row_destinations",
    )(table, top_idx, rank)[:TOP_K]
    n_rows = n_tokens * TOP_K + N_EXPERTS * MOE_BLOCK
    return dest, table, n_rows


SC_CORES = 2
SC_SUBCORES = 16
SC_WORKERS = SC_CORES * SC_SUBCORES
SC_CHUNK = 64
SC_GATHER_RING = 3

def _sc_mesh():
    return plsc.VectorSubcoreMesh(core_axis_name="c", subcore_axis_name="s")


def _sc_worker():
    return lax.axis_index("s") * SC_CORES + lax.axis_index("c")


def _dispatch(h2, dest, n_rows):
    t, d = h2.shape
    per_w = t // SC_WORKERS
    n_chunks = per_w // SC_CHUNK
    assert per_w % (2 * SC_CHUNK) == 0
    idx = dest.reshape(TOP_K, SC_WORKERS, n_chunks, SC_CHUNK).transpose(1, 0, 2, 3)
    idx = idx.reshape(SC_WORKERS, TOP_K * n_chunks, SC_CHUNK)

    @functools.partial(
        pl.kernel, mesh=_sc_mesh(),
        out_type=jax.ShapeDtypeStruct((n_rows, d), h2.dtype),
        scratch_types=[pltpu.VMEM((TOP_K * n_chunks, SC_CHUNK), jnp.int32),
                       pltpu.VMEM((2, SC_CHUNK, d), h2.dtype),
                       pltpu.SemaphoreType.DMA((2,)),
                       pltpu.SemaphoreType.DMA((2,))],
        name="moe_dispatch")
    def run(h2_hbm, idx_hbm, xs_hbm, idx_v, rows_v, rsem, ssem):
        w = _sc_worker()
        pltpu.sync_copy(idx_hbm.at[w], idx_v)

        def read(g, b):
            src = h2_hbm.at[pl.ds(w * per_w + g * SC_CHUNK, SC_CHUNK)]
            return pltpu.make_async_copy(src, rows_v.at[b], rsem.at[b])

        def scatter(g, kk, b):
            dst = xs_hbm.at[idx_v.at[kk * n_chunks + g]]
            return pltpu.make_async_copy(rows_v.at[b], dst, ssem.at[b])

        read(0, 0).start()

        @pl.loop(0, n_chunks, step=2)
        def _(g0):
            for b in range(2):
                g = g0 + b
                read(g, b).wait()

                @pl.when(g + 1 < n_chunks)
                def _():
                    read(g + 1, 1 - b).start()

                for kk in range(TOP_K):
                    scatter(g, kk, b).start()
                for kk in range(TOP_K):
                    scatter(g, kk, b).wait()

    return run(h2, idx)


def _undispatch(ys, dest):
    t = dest.shape[1]
    d = ys.shape[1]
    n_out = t * TOP_K
    per_w = n_out // SC_WORKERS
    n_chunks = per_w // SC_CHUNK
    idx = dest.reshape(SC_WORKERS, n_chunks, SC_CHUNK)
    ring = SC_GATHER_RING

    @functools.partial(
        pl.kernel, mesh=_sc_mesh(),
        out_type=jax.ShapeDtypeStruct((n_out, d), ys.dtype),
        scratch_types=[pltpu.VMEM((n_chunks, SC_CHUNK), jnp.int32),
                       pltpu.VMEM((ring, SC_CHUNK, d), ys.dtype),
                       pltpu.SemaphoreType.DMA((ring,)),
                       pltpu.SemaphoreType.DMA((ring,))],
        name="moe_undispatch")
    def run(ys_hbm, idx_hbm, out_hbm, idx_v, rows_v, gsem, wsem):
        w = _sc_worker()
        pltpu.sync_copy(idx_hbm.at[w], idx_v)

        def gather(g):
            b = g % ring
            return pltpu.make_async_copy(ys_hbm.at[idx_v.at[g]], rows_v.at[b], gsem.at[b])

        def write(g):
            b = g % ring
            dst = out_hbm.at[pl.ds(w * per_w + g * SC_CHUNK, SC_CHUNK)]
            return pltpu.make_async_copy(rows_v.at[b], dst, wsem.at[b])

        for g in range(min(ring - 1, n_chunks)):
            gather(g).start()
        for g in range(n_chunks):
            gather(g).wait()
            ahead = g + ring - 1
            if ahead < n_chunks:
                if g >= 1:
                    write(g - 1).wait()
                gather(ahead).start()
            write(g).start()
        for g in range(max(n_chunks - ring, 0), n_chunks):
            write(g).wait()

    return run(ys, idx).reshape(TOP_K, t, d)


def kernel(x, c, positions, w_ada, b_ada, norm_mix_g, w_in, q_norm_g, w_uq, kv_norm_g, w_ukv,
           w_up_attn, conv_w, w_up_conv, w_o, norm_ffn_g, router_w, router_b, w_gu, b_gu,
           w_down, b_down, norm_final_g):
    batch, seq, d = x.shape
    t = batch * seq
    depth = w_ada.shape[0]
    assert d == D_MODEL and batch <= 8 and conv_w.shape[1:] == (CONV_K, CONV_WIDTH)
    assert seq % ROW_TILE == 0 and seq % POST_TILE == 0 and seq % ATT_BLOCK == 0
    assert t % (2 * SC_CHUNK * SC_WORKERS) == 0
    x2 = x.reshape(t, d)
    pos = positions.astype(F32).reshape(t // ROW_TILE, 1, ROW_TILE)
    freqs = _rope_freqs()
    c_pad = jnp.zeros((8, d), F32).at[:batch].set(c)

    for l in range(depth):
        ada = _ada(c_pad, w_ada[l], b_ada[l].reshape(1, -1))
        mod = ada[:batch].reshape(batch, 6, d)
        mod = jnp.concatenate([mod, jnp.zeros((batch, 2, d), F32)], axis=1)

        w_lat, w_conv, w_gate, wq2, wk2, wv = _prep_weights(w_in[l], w_uq[l], w_ukv[l])
        q, k, v, sga, gc = _pre(x2, mod, norm_mix_g[l].reshape(1, d), w_lat, w_conv, w_gate,
                                q_norm_g[l].reshape(1, -1), wq2,
                                kv_norm_g[l].reshape(1, -1), wk2, wv, pos, freqs, conv_w[l],
                                w_up_conv[l].astype(BF16), seq)
        merged = _attention(q, k, v, sga, gc, w_up_attn[l].astype(BF16), batch, seq)

        rw_pad = jnp.concatenate([router_w[l], jnp.zeros((d, LANES - N_EXPERTS), F32)], axis=1)
        rb_pad = jnp.concatenate([router_b[l], jnp.full((LANES - N_EXPERTS,), NEG_BIG, F32)])
        rw_hi = rw_pad.astype(BF16)
        rw_lo = (rw_pad - rw_hi.astype(F32)).astype(BF16)
        x1, h2, idx_pad, gate_pad, rank_pad, counts = _post(
            merged, x2, mod, w_o[l].astype(BF16), norm_ffn_g[l].reshape(1, d), rw_hi, rw_lo,
            rb_pad.reshape(1, LANES), seq)

        dest, group_table, n_rows = _route(
            idx_pad, rank_pad, counts[0, :N_EXPERTS], t)
        xs = _dispatch(h2, dest, n_rows)
        ys = _moe(group_table, xs, w_gu[l], b_gu[l].reshape(N_EXPERTS, 1, -1),
                  w_down[l], b_down[l].reshape(N_EXPERTS, 1, -1))
        y_kt = _undispatch(ys, dest)
        x2 = _final(x1, y_kt, gate_pad, mod, norm_final_g.reshape(1, d), seq, l == depth - 1)

    return x2.reshape(batch, seq, d)
```

```python
import functools
import math

import jax
import jax.numpy as jnp
from jax import lax
from jax.experimental import pallas as pl
from jax.experimental.pallas import tpu as pltpu
from jax.experimental.pallas import tpu_sc as plsc

D_MODEL = 1024
CHUNK = 64
N_HEADS = 8
Q_LORA = 256
KV_LORA = 128
QK_NOPE = 64
QK_ROPE = 32
V_HEAD = 64
QK_HEAD = QK_NOPE + QK_ROPE
ROPE_THETA = 10000.0
CONV_WIDTH = 512
CONV_K = 3
N_EXPERTS = 32
TOP_K = 4
D_EXPERT = 1024
SWIGLU_LIMIT = 7.0
SWIGLU_ALPHA = 1.702
MOE_BLOCK = 256
RMS_EPS = 1e-6

LANES = 128
HEAD_PAD = 128
NEG_BIG = -1e30
VMEM_LIMIT = 56 * 1024 * 1024

F32 = jnp.float32
BF16 = jnp.bfloat16

Q_PRESCALE = (QK_HEAD ** -0.5) * math.log2(math.e)

ROW_TILE = 1024
MOE_STEP_BLOCKS = 4
POST_TILE = 1024
POST_SUB = 512
ATT_BLOCK = 512
ATT_WIDE = 2


def _rms(x, g):
    ms = jnp.mean(x * x, axis=-1, keepdims=True)
    return x * lax.rsqrt(ms + RMS_EPS) * g


def _dot(a, b):
    return jnp.dot(a, b, preferred_element_type=F32)


PACKED = D_MODEL // 2


def _pack_row(x):
    return pltpu.pack_elementwise([x[:, :PACKED], x[:, PACKED:]], packed_dtype=BF16)


def _unpack_row(w):
    half = lambda i: pltpu.unpack_elementwise(w, index=i, packed_dtype=BF16, unpacked_dtype=F32)
    return jnp.concatenate([half(0), half(1)], axis=-1)


def _ada_kernel(c_ref, w_ref, b_ref, o_ref):
    c = c_ref[...]
    ca = (c * jax.nn.sigmoid(c)).astype(BF16)
    o_ref[...] = _dot(ca, w_ref[...].astype(BF16)) + b_ref[...]


def _ada(c_pad, w_ada, b_ada):
    n = w_ada.shape[1]
    tn = 1024
    return pl.pallas_call(
        _ada_kernel,
        out_shape=jax.ShapeDtypeStruct((c_pad.shape[0], n), F32),
        grid=(n // tn,),
        in_specs=[
            pl.BlockSpec(c_pad.shape, lambda j: (0, 0)),
            pl.BlockSpec((D_MODEL, tn), lambda j: (0, j)),
            pl.BlockSpec((1, tn), lambda j: (0, j)),
        ],
        out_specs=pl.BlockSpec((c_pad.shape[0], tn), lambda j: (0, j)),
        compiler_params=pltpu.CompilerParams(
            dimension_semantics=("arbitrary",), vmem_limit_bytes=VMEM_LIMIT),
        name="ada",
    )(c_pad, w_ada, b_ada)


_C_QLAT = 0
_C_KVLAT = _C_QLAT + Q_LORA
_C_KPE = _C_KVLAT + KV_LORA
_C_END = _C_KPE + HEAD_PAD


def _pre_kernel(tiles_per_seq, x_ref, mod_ref, g_ref, wlat_ref, wconv_ref, wgate_ref, qg_ref,
                wq_ref, kvg_ref, wk_ref, wv_ref, pos_ref, freq_ref, cw_ref, wuc_ref,
                q_ref, k_ref, v_ref, sga_ref, gc_ref, carry_ref):
    i = pl.program_id(0)
    tm = x_ref.shape[0]
    mod = mod_ref[...]
    h = _rms(x_ref[...], g_ref[...]) * (1.0 + mod[1:2]) + mod[0:1]
    hb = h.astype(BF16)

    ang = freq_ref[...] * pos_ref[...]
    cos_t, sin_t = jnp.cos(ang), jnp.sin(ang)
    ones_t = jnp.ones((QK_NOPE, tm), F32)
    zeros_t = jnp.zeros((QK_NOPE, tm), F32)
    pad_t = jnp.zeros((HEAD_PAD - QK_HEAD, tm), F32)
    cosf = jnp.concatenate([ones_t, cos_t, cos_t, pad_t], axis=0).T
    sinf = jnp.concatenate([zeros_t, -sin_t, sin_t, pad_t], axis=0).T

    first_half = lax.broadcasted_iota(jnp.int32, (tm, HEAD_PAD), 1) < QK_NOPE + QK_ROPE // 2

    def rope(slab):
        swapped = jnp.where(first_half, pltpu.roll(slab, HEAD_PAD - QK_ROPE // 2, 1),
                            pltpu.roll(slab, QK_ROPE // 2, 1))
        return slab * cosf + swapped * sinf

    small = _dot(hb, wlat_ref[...])
    q_lat = small[:, _C_QLAT:_C_KVLAT]
    kv_lat = small[:, _C_KVLAT:_C_KPE]
    kpe = rope(small[:, _C_KPE:_C_END])
    qn = _rms(q_lat, qg_ref[...]).astype(BF16)
    q = _dot(qn, wq_ref[...])
    q = jnp.concatenate([rope(q[:, hd * HEAD_PAD:(hd + 1) * HEAD_PAD]) for hd in range(N_HEADS)],
                        axis=-1)
    q_ref[...] = (q * Q_PRESCALE).astype(BF16)
    kvn = _rms(kv_lat, kvg_ref[...]).astype(BF16)
    k = _dot(kvn, wk_ref[...]) + jnp.concatenate([kpe] * N_HEADS, axis=-1)
    k_ref[...] = k.astype(BF16)
    lane = lax.broadcasted_iota(jnp.int32, (tm, N_HEADS * HEAD_PAD), 1)
    ones_col = jnp.where(lane % HEAD_PAD == V_HEAD, 1.0, 0.0)
    v_ref[...] = (_dot(kvn, wv_ref[...]) + ones_col).astype(BF16)

    ucb = _dot(hb, wconv_ref[...])
    cu = ucb[:, 0:CONV_WIDTH] * ucb[:, CONV_WIDTH:2 * CONV_WIDTH]
    b_gate = ucb[:, 2 * CONV_WIDTH:3 * CONV_WIDTH]

    @pl.when(i % tiles_per_seq == 0)
    def _():
        carry_ref[...] = jnp.zeros_like(carry_ref)

    prev = carry_ref[...]
    row = lax.broadcasted_iota(jnp.int32, cu.shape, 0)
    cu1 = jnp.where(row == 0, prev[7:8], pltpu.roll(cu, 1, 0))
    cu2 = jnp.where(row == 0, prev[6:7], jnp.where(row == 1, prev[7:8], pltpu.roll(cu, 2, 0)))
    cw = cw_ref[...]
    z = cw[2:3] * cu + cw[1:2] * cu1 + cw[0:1] * cu2
    carry_ref[...] = cu[tm - 8:tm]
    c_branch = _dot((b_gate * z).astype(BF16), wuc_ref[...])

    gates = _dot(hb, wgate_ref[...])
    sga_ref[...] = jax.nn.sigmoid(gates[:, 0:D_MODEL]).astype(BF16)
    gc_ref[...] = (jax.nn.sigmoid(gates[:, D_MODEL:]) * c_branch).astype(BF16)


def _pre(x2, mod, norm_g, w_lat, w_conv, w_gate, q_norm_g, wq2, kv_norm_g, wk2, wv, pos,
         freqs, conv_w, w_up_conv, seq):
    t = x2.shape[0]
    tm = ROW_TILE
    tiles_per_seq = seq // tm
    full = lambda a: pl.BlockSpec(a.shape, lambda i: (0,) * a.ndim)
    rows = lambda w: pl.BlockSpec((tm, w), lambda i: (i, 0))
    outs = [jax.ShapeDtypeStruct((t, N_HEADS * HEAD_PAD), BF16),
            jax.ShapeDtypeStruct((t, N_HEADS * HEAD_PAD), BF16),
            jax.ShapeDtypeStruct((t, N_HEADS * HEAD_PAD), BF16),
            jax.ShapeDtypeStruct((t, D_MODEL), BF16),
            jax.ShapeDtypeStruct((t, D_MODEL), BF16)]
    return pl.pallas_call(
        functools.partial(_pre_kernel, tiles_per_seq),
        out_shape=outs,
        grid=(t // tm,),
        in_specs=[
            rows(D_MODEL),
            pl.BlockSpec((None, 8, D_MODEL), lambda i: (i // tiles_per_seq, 0, 0)),
            full(norm_g), full(w_lat), full(w_conv), full(w_gate), full(q_norm_g), full(wq2),
            full(kv_norm_g), full(wk2), full(wv),
            pl.BlockSpec((None, 1, tm), lambda i: (i, 0, 0)), full(freqs), full(conv_w),
            full(w_up_conv),
        ],
        out_specs=[rows(N_HEADS * HEAD_PAD), rows(N_HEADS * HEAD_PAD), rows(N_HEADS * HEAD_PAD),
                   rows(D_MODEL), rows(D_MODEL)],
        scratch_shapes=[pltpu.VMEM((8, CONV_WIDTH), F32)],
        compiler_params=pltpu.CompilerParams(
            dimension_semantics=("arbitrary",), vmem_limit_bytes=VMEM_LIMIT),
        name="pre_mixer",
    )(x2, mod, norm_g, w_lat, w_conv, w_gate, q_norm_g, wq2, kv_norm_g, wk2, wv, pos,
      freqs, conv_w, w_up_conv)


def _attn_kernel(q_ref, k_ref, v_ref, sga_ref, gc_ref, wua_ref, o_ref, m_ref, acc_ref):
    i = pl.program_id(1)
    tq = q_ref.shape[0]

    def step(k0, tk, masked, first=False):
        if masked:
            rq = (lax.broadcasted_iota(jnp.int32, (tq, tk), 0) + (tk - tq)) // CHUNK
            ck = lax.broadcasted_iota(jnp.int32, (tq, tk), 1) // CHUNK
            allowed = ck <= rq
        for hd in range(N_HEADS):
            hs = slice(hd * HEAD_PAD, (hd + 1) * HEAD_PAD)
            s = lax.dot_general(q_ref[:, hs], k_ref[pl.ds(k0, tk), hs],
                                (((1,), (1,)), ((), ())), preferred_element_type=F32)
            if masked:
                s = jnp.where(allowed, s, NEG_BIG)
            s_max = s[:, 0:LANES]
            for c in range(1, tk // LANES):
                s_max = jnp.maximum(s_max, s[:, c * LANES:(c + 1) * LANES])
            m_new = jnp.broadcast_to(jnp.max(s_max, axis=-1, keepdims=True), (tq, LANES))
            if not first:
                m_old = m_ref[hd]
                m_new = jnp.maximum(m_old, m_new)
            p = jnp.concatenate(
                [jnp.exp2(s[:, c * LANES:(c + 1) * LANES] - m_new).astype(BF16)
                 for c in range(tk // LANES)], axis=-1)
            pv = _dot(p, v_ref[pl.ds(k0, tk), hs])
            acc_ref[hd] = pv if first else jnp.exp2(m_old - m_new) * acc_ref[hd] + pv
            m_ref[hd] = m_new

    wide = ATT_WIDE * tq
    n_wide = i // ATT_WIDE

    @pl.when(n_wide > 0)
    def _():
        step(0, wide, False, first=True)

    @pl.when(n_wide == 0)
    def _():
        m_ref[...] = jnp.full_like(m_ref, NEG_BIG)
        acc_ref[...] = jnp.zeros_like(acc_ref)

    def body(j, carry):
        step(pl.multiple_of(j * wide, wide), wide, False)
        return carry

    lax.fori_loop(1, n_wide, body, 0)

    for r in range(ATT_WIDE):
        @pl.when(i % ATT_WIDE == r)
        def _():
            step(pl.multiple_of((i - r) * tq, tq), (r + 1) * tq, True)

    heads = []
    for hd in range(N_HEADS):
        acc = acc_ref[hd]
        heads.append((acc[:, 0:V_HEAD] / acc[:, V_HEAD:V_HEAD + 1]).astype(BF16))
    a_branch = _dot(jnp.concatenate(heads, axis=-1), wua_ref[...])
    o_ref[...] = (sga_ref[...].astype(F32) * a_branch + gc_ref[...].astype(F32)).astype(BF16)


def _attention(q, k, v, sga, gc, wua, batch, seq):
    tq = ATT_BLOCK
    nq = seq // tq
    q_rows = lambda w: pl.BlockSpec((tq, w), lambda b, i: (b * nq + i, 0))
    whole_seq = pl.BlockSpec((seq, N_HEADS * HEAD_PAD), lambda b, i: (b, 0))
    return pl.pallas_call(
        _attn_kernel,
        out_shape=jax.ShapeDtypeStruct((batch * seq, D_MODEL), BF16),
        grid=(batch, nq),
        in_specs=[q_rows(N_HEADS * HEAD_PAD), whole_seq, whole_seq, q_rows(D_MODEL),
                  q_rows(D_MODEL), pl.BlockSpec(wua.shape, lambda b, i: (0, 0))],
        out_specs=q_rows(D_MODEL),
        scratch_shapes=[pltpu.VMEM((N_HEADS, tq, LANES), F32),
                        pltpu.VMEM((N_HEADS, tq, LANES), F32)],
        compiler_params=pltpu.CompilerParams(
            dimension_semantics=("arbitrary", "arbitrary"), vmem_limit_bytes=VMEM_LIMIT),
        name="attention",
    )(q, k, v, sga, gc, wua)


def _post_kernel(merged_ref, x_ref, mod_ref, wo_ref, g_ref, rw_ref, rb_ref,
                 x1_ref, h2_ref, idx_ref, gate_ref, rank_ref, cnt_out_ref, cnt_ref, lg_ref):
    i = pl.program_id(0)

    @pl.when(i == 0)
    def _():
        cnt_ref[...] = jnp.zeros_like(cnt_ref)
        lg_ref[...] = jnp.zeros_like(lg_ref)

    prev_logits = lg_ref[(i + 1) % 2]
    counts = cnt_ref[...]
    routed = counts
    for r0 in range(0, x_ref.shape[0], POST_SUB):
        rs = slice(r0, r0 + POST_SUB)
        lg_ref[i % 2, rs, :] = _post_mix(rs, merged_ref, x_ref, mod_ref, wo_ref, g_ref, rw_ref,
                                         rb_ref, x1_ref, h2_ref)
        routed = _post_route(rs, prev_logits[rs, :], routed, idx_ref, gate_ref, rank_ref)
    counts = jnp.where(i > 0, routed, counts)
    cnt_ref[...] = counts
    cnt_out_ref[...] = counts.astype(jnp.int32)


def _post_mix(rs, merged_ref, x_ref, mod_ref, wo_ref, g_ref, rw_ref, rb_ref, x1_ref, h2_ref):
    mod = mod_ref[...]
    mix = _dot(merged_ref[rs, :], wo_ref[...])
    x1 = x_ref[rs, :] + mod[2:3] * mix
    x1_ref[rs, :] = x1
    h2 = _rms(x1, g_ref[...]) * (1.0 + mod[4:5]) + mod[3:4]
    h2_ref[rs, :] = _pack_row(h2)

    h_hi = h2.astype(BF16)
    h_lo = (h2 - h_hi.astype(F32)).astype(BF16)
    both = _dot(h_hi, rw_ref[...])
    return both[:, :LANES] + both[:, LANES:] + _dot(h_lo, rw_ref[:, :LANES]) + rb_ref[...]


def _post_route(rs, logits, counts, idx_ref, gate_ref, rank_ref):
    lane = lax.broadcasted_iota(jnp.int32, logits.shape, 1)
    work = logits
    vals, idxs = [], []
    for _ in range(TOP_K):
        mk = jnp.max(work, axis=-1, keepdims=True)
        ik = jnp.min(jnp.where(work == mk, lane, LANES), axis=-1, keepdims=True)
        vals.append(mk)
        idxs.append(ik)
        work = jnp.where(lane == ik, -jnp.inf, work)
    es = [jnp.exp(vk - vals[0]) for vk in vals]
    denom = es[0] + es[1] + es[2] + es[3]
    tm = logits.shape[0]
    chosen = jnp.zeros(logits.shape, F32)
    for kk in range(TOP_K):
        chosen = chosen + jnp.where(lane == idxs[kk], 1.0, 0.0)
    r_i = lax.broadcasted_iota(jnp.int32, (tm, tm), 0)
    c_i = lax.broadcasted_iota(jnp.int32, (tm, tm), 1)
    earlier = jnp.where(c_i < r_i, 1.0, 0.0).astype(BF16)
    before = _dot(earlier, chosen.astype(BF16)) + counts[0:1]

    idx_out = jnp.zeros(logits.shape, F32)
    gate_out = jnp.zeros(logits.shape, F32)
    rank_out = jnp.zeros(logits.shape, F32)
    for kk in range(TOP_K):
        rank_k = jnp.sum(jnp.where(lane == idxs[kk], before, 0.0), axis=-1, keepdims=True)
        idx_out = jnp.where(lane == kk, idxs[kk].astype(F32), idx_out)
        gate_out = jnp.where(lane == kk, es[kk] / denom, gate_out)
        rank_out = jnp.where(lane == kk, rank_k, rank_out)
    gate_ref[rs, :] = gate_out
    idx_ref[:, rs] = idx_out.T[0:8].astype(jnp.int32)
    rank_ref[:, rs] = rank_out.T[0:8].astype(jnp.int32)
    return counts + jnp.sum(chosen, axis=0, keepdims=True)


def _post(merged, x2, mod, wo, norm_g, rw_hl, rb_pad, seq):
    t = x2.shape[0]
    tm = POST_TILE
    tiles_per_seq = seq // tm
    n_tiles = t // tm
    full = lambda a: pl.BlockSpec(a.shape, lambda i: (0,) * a.ndim)
    mix_tile = lambda i: jnp.minimum(i, n_tiles - 1)
    route_tile = lambda i: jnp.maximum(i - 1, 0)
    rows = lambda w: pl.BlockSpec((tm, w), lambda i: (mix_tile(i), 0))
    outs = [jax.ShapeDtypeStruct((t, D_MODEL), F32),
            jax.ShapeDtypeStruct((t, PACKED), jnp.uint32),
            jax.ShapeDtypeStruct((8, t), jnp.int32),
            jax.ShapeDtypeStruct((t, LANES), F32),
            jax.ShapeDtypeStruct((8, t), jnp.int32),
            jax.ShapeDtypeStruct((8, LANES), jnp.int32)]
    slots = pl.BlockSpec((8, tm), lambda i: (0, route_tile(i)))
    return pl.pallas_call(
        _post_kernel,
        out_shape=outs,
        grid=(n_tiles + 1,),
        in_specs=[
            rows(D_MODEL), rows(D_MODEL),
            pl.BlockSpec((None, 8, D_MODEL), lambda i: (mix_tile(i) // tiles_per_seq, 0, 0)),
            full(wo), full(norm_g), full(rw_hl), full(rb_pad),
        ],
        out_specs=[rows(D_MODEL), rows(PACKED), slots,
                   pl.BlockSpec((tm, LANES), lambda i: (route_tile(i), 0)), slots,
                   pl.BlockSpec((8, LANES), lambda i: (0, 0))],
        scratch_shapes=[pltpu.VMEM((8, LANES), F32), pltpu.VMEM((2, tm, LANES), F32)],
        compiler_params=pltpu.CompilerParams(
            dimension_semantics=("arbitrary",), vmem_limit_bytes=VMEM_LIMIT),
        name="post_mixer",
    )(merged, x2, mod, wo, norm_g, rw_hl, rb_pad)


_GM_COUNT, _GM_FIRST, _GM_BLOCKS = range(3)
_ST_EXPERT, _ST_SLOT = range(2)


def _moe_kernel(gm_ref, xs_ref, wgu_hbm, bgu_ref, wd_hbm, bd_ref, o_ref,
                wgu_f, wd_f, wgu_bf, wd_bf, sem, st_ref):
    weights = (wgu_hbm, wd_hbm, wgu_f, wd_f, wgu_bf, wd_bf, sem)
    rows = (xs_ref, bgu_ref, bd_ref, o_ref, wgu_bf, wd_bf)
    b0 = pl.program_id(0) * MOE_STEP_BLOCKS
    e, in_group, used = _moe_enter(b0, gm_ref, st_ref, *weights)
    together = jnp.logical_and(used, in_group + MOE_STEP_BLOCKS <= gm_ref[_GM_BLOCKS, e])

    @pl.when(together)
    def _():
        _moe_rows(slice(0, MOE_STEP_BLOCKS * MOE_BLOCK), e, in_group, True, gm_ref, *rows)

    @pl.when(jnp.logical_not(together))
    def _():
        _moe_rows(slice(0, MOE_BLOCK), e, in_group, used, gm_ref, *rows)
        for r in range(1, MOE_STEP_BLOCKS):
            e_r, in_group_r, used_r = _moe_enter(b0 + r, gm_ref, st_ref, *weights)
            _moe_rows(slice(r * MOE_BLOCK, (r + 1) * MOE_BLOCK), e_r, in_group_r, used_r, gm_ref,
                      *rows)


def _moe_enter(b, gm_ref, st_ref, wgu_hbm, wd_hbm, wgu_f, wd_f, wgu_bf, wd_bf, sem):
    def weight_copies(expert, sl):
        return (pltpu.make_async_copy(wgu_hbm.at[expert], wgu_f.at[sl], sem.at[0, sl]),
                pltpu.make_async_copy(wd_hbm.at[expert], wd_f.at[sl], sem.at[1, sl]))

    def next_group(e):
        return lax.while_loop(
            lambda k: jnp.logical_and(k < N_EXPERTS,
                                      gm_ref[_GM_BLOCKS, jnp.minimum(k, N_EXPERTS - 1)] == 0),
            lambda k: k + 1, e)

    @pl.when(b == 0)
    def _():
        e0 = next_group(0)
        st_ref[_ST_EXPERT] = e0
        st_ref[_ST_SLOT] = 1
        for cp in weight_copies(e0, 0):
            cp.start()

    e_prev = st_ref[_ST_EXPERT]
    past = b >= gm_ref[_GM_FIRST, e_prev] + gm_ref[_GM_BLOCKS, e_prev]
    e = jnp.minimum(jnp.where(past, next_group(e_prev + 1), e_prev), N_EXPERTS - 1)
    st_ref[_ST_EXPERT] = e
    in_group = b - gm_ref[_GM_FIRST, e]
    used = jnp.logical_and(in_group >= 0, in_group < gm_ref[_GM_BLOCKS, e])

    @pl.when(jnp.logical_and(used, in_group == 0))
    def _():
        slot = 1 - st_ref[_ST_SLOT]
        st_ref[_ST_SLOT] = slot
        nxt = next_group(e + 1)

        @pl.when(nxt < N_EXPERTS)
        def _():
            for cp in weight_copies(nxt, 1 - slot):
                cp.start(priority=1)

        for cp in weight_copies(e, slot):
            cp.wait()
        wgu_bf[...] = wgu_f[slot].astype(BF16)
        wd_bf[...] = wd_f[slot].astype(BF16)

    return e, in_group, used


def _moe_rows(rs, e, in_group, used, gm_ref, xs_ref, bgu_ref, bd_ref, o_ref, wgu_bf, wd_bf):
    n_valid = gm_ref[_GM_COUNT, e] - in_group * MOE_BLOCK

    def ffn(r):
        n = r.stop - r.start
        row = lax.broadcasted_iota(jnp.int32, (n, PACKED), 0)
        xs = _unpack_row(jnp.where(row < n_valid, xs_ref[r, :], 0)).astype(BF16)
        gu = _dot(xs, wgu_bf[...]) + bgu_ref[e]
        gate = jnp.minimum(gu[:, :D_EXPERT], SWIGLU_LIMIT)
        up = jnp.clip(gu[:, D_EXPERT:], -SWIGLU_LIMIT, SWIGLU_LIMIT)
        act = (up + 1.0) * (gate * jax.nn.sigmoid(gate * SWIGLU_ALPHA))
        o_ref[r, :] = _pack_row(_dot(act.astype(BF16), wd_bf[...]) + bd_ref[e])

    def zeros(r):
        o_ref[r, :] = jnp.zeros((r.stop - r.start, PACKED), o_ref.dtype)

    if used is True:
        ffn(rs)
        return

    mid = rs.start + (rs.stop - rs.start) // 2
    short = n_valid <= mid - rs.start

    @pl.when(jnp.logical_and(used, jnp.logical_not(short)))
    def _():
        ffn(rs)

    @pl.when(jnp.logical_and(used, short))
    def _():
        ffn(slice(rs.start, mid))
        zeros(slice(mid, rs.stop))

    @pl.when(jnp.logical_not(used))
    def _():
        zeros(rs)


def _moe(group_table, xs, w_gu, b_gu, w_down, b_down):
    n_rows = xs.shape[0]
    step_rows = MOE_STEP_BLOCKS * MOE_BLOCK
    assert n_rows % step_rows == 0
    grid_spec = pltpu.PrefetchScalarGridSpec(
        num_scalar_prefetch=1,
        grid=(n_rows // step_rows,),
        in_specs=[
            pl.BlockSpec((step_rows, PACKED), lambda b, gm: (b, 0)),
            pl.BlockSpec(memory_space=pl.ANY),
            pl.BlockSpec(b_gu.shape, lambda b, gm: (0, 0, 0)),
            pl.BlockSpec(memory_space=pl.ANY),
            pl.BlockSpec(b_down.shape, lambda b, gm: (0, 0, 0)),
        ],
        out_specs=pl.BlockSpec((step_rows, PACKED), lambda b, gm: (b, 0)),
        scratch_shapes=[pltpu.VMEM((2, D_MODEL, 2 * D_EXPERT), F32),
                        pltpu.VMEM((2, D_EXPERT, D_MODEL), F32),
                        pltpu.VMEM((D_MODEL, 2 * D_EXPERT), BF16),
                        pltpu.VMEM((D_EXPERT, D_MODEL), BF16),
                        pltpu.SemaphoreType.DMA((2, 2)),
                        pltpu.SMEM((2,), jnp.int32)],
    )
    return pl.pallas_call(
        _moe_kernel,
        out_shape=jax.ShapeDtypeStruct((n_rows, PACKED), jnp.uint32),
        grid_spec=grid_spec,
        compiler_params=pltpu.CompilerParams(
            dimension_semantics=("arbitrary",), vmem_limit_bytes=VMEM_LIMIT),
        name="moe_experts",
    )(group_table, xs, w_gu, b_gu, w_down, b_down)


def _final_kernel(last_layer, x1_ref, y_ref, gate_ref, mod_ref, g_ref, o_ref):
    mod = mod_ref[...]
    gate = gate_ref[...]
    ffn = gate[:, 0:1] * _unpack_row(y_ref[0])
    for kk in range(1, TOP_K):
        ffn = ffn + gate[:, kk:kk + 1] * _unpack_row(y_ref[kk])
    x = x1_ref[...] + mod[5:6] * ffn
    o_ref[...] = _rms(x, g_ref[...]) if last_layer else x


def _final(x1, y_kt, gate, mod, norm_g, seq, last_layer):
    t = x1.shape[0]
    tm = ROW_TILE
    tiles_per_seq = seq // tm
    rows = lambda w: pl.BlockSpec((tm, w), lambda i: (i, 0))
    return pl.pallas_call(
        functools.partial(_final_kernel, last_layer),
        out_shape=jax.ShapeDtypeStruct((t, D_MODEL), F32),
        grid=(t // tm,),
        in_specs=[
            rows(D_MODEL), pl.BlockSpec((TOP_K, tm, PACKED), lambda i: (0, i, 0)), rows(LANES),
            pl.BlockSpec((None, 8, D_MODEL), lambda i: (i // tiles_per_seq, 0, 0)),
            pl.BlockSpec(norm_g.shape, lambda i: (0, 0)),
        ],
        out_specs=rows(D_MODEL),
        compiler_params=pltpu.CompilerParams(
            dimension_semantics=("arbitrary",), vmem_limit_bytes=VMEM_LIMIT),
        name="combine_final",
    )(x1, y_kt, gate, mod, norm_g)


def _prep_weights(w_in, w_uq, w_ukv):
    d = w_in.shape[0]
    splits = (Q_LORA, KV_LORA, QK_ROPE, CONV_WIDTH, CONV_WIDTH, CONV_WIDTH, D_MODEL, D_MODEL)
    offs = [0]
    for s in splits:
        offs.append(offs[-1] + s)
    part = lambda n: w_in[:, offs[n]:offs[n + 1]]
    z = lambda n: jnp.zeros((d, n), w_in.dtype)
    w_kpe = part(2)
    kpe_slab = jnp.concatenate([z(QK_NOPE), w_kpe, z(HEAD_PAD - QK_HEAD)], axis=1)
    w_lat = jnp.concatenate([part(0), part(1), kpe_slab], axis=1).astype(BF16)
    w_conv = w_in[:, offs[3]:offs[6]].astype(BF16)
    w_gate = w_in[:, offs[6]:offs[8]].astype(BF16)

    wq = w_uq.reshape(Q_LORA, N_HEADS, QK_HEAD)
    zq = lambda n: jnp.zeros((Q_LORA, N_HEADS, n), w_uq.dtype)
    wq2 = jnp.concatenate([wq, zq(HEAD_PAD - QK_HEAD)], axis=-1)
    wq2 = wq2.reshape(Q_LORA, N_HEADS * HEAD_PAD).astype(BF16)

    wkv = w_ukv.reshape(KV_LORA, N_HEADS, QK_NOPE + V_HEAD)
    wk2 = jnp.concatenate([wkv[..., :QK_NOPE],
                           jnp.zeros((KV_LORA, N_HEADS, HEAD_PAD - QK_NOPE), w_ukv.dtype)], axis=-1)
    wk2 = wk2.reshape(KV_LORA, N_HEADS * HEAD_PAD).astype(BF16)
    wv = jnp.concatenate([wkv[..., QK_NOPE:],
                          jnp.zeros((KV_LORA, N_HEADS, HEAD_PAD - V_HEAD), w_ukv.dtype)], axis=-1)
    wv = wv.reshape(KV_LORA, N_HEADS * HEAD_PAD).astype(BF16)
    return w_lat, w_conv, w_gate, wq2, wk2, wv


def _rope_freqs():
    inv_freq = 1.0 / (ROPE_THETA ** (jnp.arange(0, QK_ROPE, 2, dtype=F32) / QK_ROPE))
    return inv_freq.reshape(QK_ROPE // 2, 1)


def _dest_kernel(gm_ref, idx_ref, rank_ref, o_ref):
    idx = idx_ref[...]
    dest = rank_ref[...]
    for e in range(N_EXPERTS):
        dest = dest + jnp.where(idx == e, gm_ref[_GM_FIRST, e] * MOE_BLOCK, 0)
    o_ref[...] = dest


def _route(top_idx, rank, counts, n_tokens):
    blocks = (counts + MOE_BLOCK - 1) // MOE_BLOCK
    first_block = jnp.cumsum(blocks) - blocks
    table = jnp.stack([counts, first_block, blocks]).astype(jnp.int32)
    whole = pl.BlockSpec(top_idx.shape, lambda i, gm: (0, 0))
    dest = pl.pallas_call(
        _dest_kernel,
        out_shape=jax.ShapeDtypeStruct(rank.shape, jnp.int32),
        grid_spec=pltpu.PrefetchScalarGridSpec(
            num_scalar_prefetch=1, grid=(1,), in_specs=[whole, whole], out_specs=whole),
        name="row_destinations",
    )(table, top_idx, rank)[:TOP_K]
    n_rows = n_tokens * TOP_K + N_EXPERTS * MOE_BLOCK
    return dest, table, n_rows


SC_CORES = 2
SC_SUBCORES = 16
SC_WORKERS = SC_CORES * SC_SUBCORES
SC_CHUNK = 64
SC_GATHER_RING = 3

def _sc_mesh():
    return plsc.VectorSubcoreMesh(core_axis_name="c", subcore_axis_name="s")


def _sc_worker():
    return lax.axis_index("s") * SC_CORES + lax.axis_index("c")


def _dispatch(h2, dest, n_rows):
    t, d = h2.shape
    per_w = t // SC_WORKERS
    n_chunks = per_w // SC_CHUNK
    assert per_w % (2 * SC_CHUNK) == 0
    idx = dest.reshape(TOP_K, SC_WORKERS, n_chunks, SC_CHUNK).transpose(1, 0, 2, 3)
    idx = idx.reshape(SC_WORKERS, TOP_K * n_chunks, SC_CHUNK)

    @functools.partial(
        pl.kernel, mesh=_sc_mesh(),
        out_type=jax.ShapeDtypeStruct((n_rows, d), h2.dtype),
        scratch_types=[pltpu.VMEM((TOP_K * n_chunks, SC_CHUNK), jnp.int32),
                       pltpu.VMEM((2, SC_CHUNK, d), h2.dtype),
                       pltpu.SemaphoreType.DMA((2,)),
                       pltpu.SemaphoreType.DMA((2,))],
        name="moe_dispatch")
    def run(h2_hbm, idx_hbm, xs_hbm, idx_v, rows_v, rsem, ssem):
        w = _sc_worker()
        pltpu.sync_copy(idx_hbm.at[w], idx_v)

        def read(g, b):
            src = h2_hbm.at[pl.ds(w * per_w + g * SC_CHUNK, SC_CHUNK)]
            return pltpu.make_async_copy(src, rows_v.at[b], rsem.at[b])

        def scatter(g, kk, b):
            dst = xs_hbm.at[idx_v.at[kk * n_chunks + g]]
            return pltpu.make_async_copy(rows_v.at[b], dst, ssem.at[b])

        read(0, 0).start()

        @pl.loop(0, n_chunks, step=2)
        def _(g0):
            for b in range(2):
                g = g0 + b
                read(g, b).wait()

                @pl.when(g + 1 < n_chunks)
                def _():
                    read(g + 1, 1 - b).start()

                for kk in range(TOP_K):
                    scatter(g, kk, b).start()
                for kk in range(TOP_K):
                    scatter(g, kk, b).wait()

    return run(h2, idx)


def _undispatch(ys, dest):
    t = dest.shape[1]
    d = ys.shape[1]
    n_out = t * TOP_K
    per_w = n_out // SC_WORKERS
    n_chunks = per_w // SC_CHUNK
    idx = dest.reshape(SC_WORKERS, n_chunks, SC_CHUNK)
    ring = SC_GATHER_RING

    @functools.partial(
        pl.kernel, mesh=_sc_mesh(),
        out_type=jax.ShapeDtypeStruct((n_out, d), ys.dtype),
        scratch_types=[pltpu.VMEM((n_chunks, SC_CHUNK), jnp.int32),
                       pltpu.VMEM((ring, SC_CHUNK, d), ys.dtype),
                       pltpu.SemaphoreType.DMA((ring,)),
                       pltpu.SemaphoreType.DMA((ring,))],
        name="moe_undispatch")
    def run(ys_hbm, idx_hbm, out_hbm, idx_v, rows_v, gsem, wsem):
        w = _sc_worker()
        pltpu.sync_copy(idx_hbm.at[w], idx_v)

        def gather(g):
            b = g % ring
            return pltpu.make_async_copy(ys_hbm.at[idx_v.at[g]], rows_v.at[b], gsem.at[b])

        def write(g):
            b = g % ring
            dst = out_hbm.at[pl.ds(w * per_w + g * SC_CHUNK, SC_CHUNK)]
            return pltpu.make_async_copy(rows_v.at[b], dst, wsem.at[b])

        for g in range(min(ring - 1, n_chunks)):
            gather(g).start()
        for g in range(n_chunks):
            gather(g).wait()
            ahead = g + ring - 1
            if ahead < n_chunks:
                if g >= 1:
                    write(g - 1).wait()
                gather(ahead).start()
            write(g).start()
        for g in range(max(n_chunks - ring, 0), n_chunks):
            write(g).wait()

    return run(ys, idx).reshape(TOP_K, t, d)


def kernel(x, c, positions, w_ada, b_ada, norm_mix_g, w_in, q_norm_g, w_uq, kv_norm_g, w_ukv,
           w_up_attn, conv_w, w_up_conv, w_o, norm_ffn_g, router_w, router_b, w_gu, b_gu,
           w_down, b_down, norm_final_g):
    batch, seq, d = x.shape
    t = batch * seq
    depth = w_ada.shape[0]
    assert d == D_MODEL and batch <= 8 and conv_w.shape[1:] == (CONV_K, CONV_WIDTH)
    assert seq % ROW_TILE == 0 and seq % POST_TILE == 0 and seq % ATT_BLOCK == 0
    assert t % (2 * SC_CHUNK * SC_WORKERS) == 0
    x2 = x.reshape(t, d)
    pos = positions.astype(F32).reshape(t // ROW_TILE, 1, ROW_TILE)
    freqs = _rope_freqs()
    c_pad = jnp.zeros((8, d), F32).at[:batch].set(c)

    for l in range(depth):
        ada = _ada(c_pad, w_ada[l], b_ada[l].reshape(1, -1))
        mod = ada[:batch].reshape(batch, 6, d)
        mod = jnp.concatenate([mod, jnp.zeros((batch, 2, d), F32)], axis=1)

        w_lat, w_conv, w_gate, wq2, wk2, wv = _prep_weights(w_in[l], w_uq[l], w_ukv[l])
        q, k, v, sga, gc = _pre(x2, mod, norm_mix_g[l].reshape(1, d), w_lat, w_conv, w_gate,
                                q_norm_g[l].reshape(1, -1), wq2,
                                kv_norm_g[l].reshape(1, -1), wk2, wv, pos, freqs, conv_w[l],
                                w_up_conv[l].astype(BF16), seq)
        merged = _attention(q, k, v, sga, gc, w_up_attn[l].astype(BF16), batch, seq)

        rw_pad = jnp.concatenate([router_w[l], jnp.zeros((d, LANES - N_EXPERTS), F32)], axis=1)
        rb_pad = jnp.concatenate([router_b[l], jnp.full((LANES - N_EXPERTS,), NEG_BIG, F32)])
        rw_hi = rw_pad.astype(BF16)
        rw_lo = (rw_pad - rw_hi.astype(F32)).astype(BF16)
        x1, h2, idx_pad, gate_pad, rank_pad, counts = _post(
            merged, x2, mod, w_o[l].astype(BF16), norm_ffn_g[l].reshape(1, d),
            jnp.concatenate([rw_hi, rw_lo], axis=1), rb_pad.reshape(1, LANES), seq)

        dest, group_table, n_rows = _route(
            idx_pad, rank_pad, counts[0, :N_EXPERTS], t)
        xs = _dispatch(h2, dest, n_rows)
        ys = _moe(group_table, xs, w_gu[l], b_gu[l].reshape(N_EXPERTS, 1, -1),
                  w_down[l], b_down[l].reshape(N_EXPERTS, 1, -1))
        y_kt = _undispatch(ys, dest)
        x2 = _final(x1, y_kt, gate_pad, mod, norm_final_g.reshape(1, d), seq, l == depth - 1)

    return x2.reshape(batch, seq, d)
```

```python
import functools
import math

import jax
import jax.numpy as jnp
from jax import lax
from jax.experimental import pallas as pl
from jax.experimental.pallas import tpu as pltpu
from jax.experimental.pallas import tpu_sc as plsc

D_MODEL = 1024
CHUNK = 64
N_HEADS = 8
Q_LORA = 256
KV_LORA = 128
QK_NOPE = 64
QK_ROPE = 32
V_HEAD = 64
QK_HEAD = QK_NOPE + QK_ROPE
ROPE_THETA = 10000.0
CONV_WIDTH = 512
CONV_K = 3
N_EXPERTS = 32
TOP_K = 4
D_EXPERT = 1024
SWIGLU_LIMIT = 7.0
SWIGLU_ALPHA = 1.702
MOE_BLOCK = 256
RMS_EPS = 1e-6

LANES = 128
HEAD_PAD = 128
NEG_BIG = -1e30
VMEM_LIMIT = 56 * 1024 * 1024

F32 = jnp.float32
BF16 = jnp.bfloat16

Q_PRESCALE = (QK_HEAD ** -0.5) * math.log2(math.e)

ADA_TILE = 1024
ROW_TILE = 1024
MOE_STEP_BLOCKS = 4
POST_TILE = 1024
POST_SUB = 512
ATT_BLOCK = 512
ATT_WIDE = 2


def _rms(x, g):
    ms = jnp.mean(x * x, axis=-1, keepdims=True)
    return x * lax.rsqrt(ms + RMS_EPS) * g


def _dot(a, b):
    return jnp.dot(a, b, preferred_element_type=F32)


PACKED = D_MODEL // 2


def _pack_row(x):
    return pltpu.pack_elementwise([x[:, :PACKED], x[:, PACKED:]], packed_dtype=BF16)


def _unpack_row(w):
    half = lambda i: pltpu.unpack_elementwise(w, index=i, packed_dtype=BF16, unpacked_dtype=F32)
    return jnp.concatenate([half(0), half(1)], axis=-1)


def _ada_kernel(c_ref, w_ref, b_ref, o_ref):
    c = c_ref[...]
    ca = (c * jax.nn.sigmoid(c)).astype(BF16)
    o_ref[...] = _dot(ca, w_ref[...].astype(BF16)) + b_ref[...]


def _ada(c_pad, w_ada, b_ada):
    n = w_ada.shape[1]
    tn = ADA_TILE
    return pl.pallas_call(
        _ada_kernel,
        out_shape=jax.ShapeDtypeStruct((c_pad.shape[0], n), F32),
        grid=(n // tn,),
        in_specs=[
            pl.BlockSpec(c_pad.shape, lambda j: (0, 0)),
            pl.BlockSpec((D_MODEL, tn), lambda j: (0, j)),
            pl.BlockSpec((1, tn), lambda j: (0, j)),
        ],
        out_specs=pl.BlockSpec((c_pad.shape[0], tn), lambda j: (0, j)),
        compiler_params=pltpu.CompilerParams(
            dimension_semantics=("arbitrary",), vmem_limit_bytes=VMEM_LIMIT),
        name="ada",
    )(c_pad, w_ada, b_ada)


_C_QLAT = 0
_C_KVLAT = _C_QLAT + Q_LORA
_C_KPE = _C_KVLAT + KV_LORA
_C_END = _C_KPE + HEAD_PAD


def _pre_kernel(tiles_per_seq, x_ref, mod_ref, g_ref, wlat_ref, wconv_ref, wgate_ref, qg_ref,
                wq_ref, kvg_ref, wk_ref, wv_ref, pos_ref, freq_ref, cw_ref, wuc_ref,
                q_ref, k_ref, v_ref, sga_ref, gc_ref, carry_ref):
    i = pl.program_id(0)
    tm = x_ref.shape[0]
    mod = mod_ref[...]
    h = _rms(x_ref[...], g_ref[...]) * (1.0 + mod[1:2]) + mod[0:1]
    hb = h.astype(BF16)

    ang = freq_ref[...] * pos_ref[...]
    cos_t, sin_t = jnp.cos(ang), jnp.sin(ang)
    ones_t = jnp.ones((QK_NOPE, tm), F32)
    zeros_t = jnp.zeros((QK_NOPE, tm), F32)
    pad_t = jnp.zeros((HEAD_PAD - QK_HEAD, tm), F32)
    cosf = jnp.concatenate([ones_t, cos_t, cos_t, pad_t], axis=0).T
    sinf = jnp.concatenate([zeros_t, -sin_t, sin_t, pad_t], axis=0).T

    first_half = lax.broadcasted_iota(jnp.int32, (tm, HEAD_PAD), 1) < QK_NOPE + QK_ROPE // 2

    def rope(slab):
        swapped = jnp.where(first_half, pltpu.roll(slab, HEAD_PAD - QK_ROPE // 2, 1),
                            pltpu.roll(slab, QK_ROPE // 2, 1))
        return slab * cosf + swapped * sinf

    small = _dot(hb, wlat_ref[...])
    q_lat = small[:, _C_QLAT:_C_KVLAT]
    kv_lat = small[:, _C_KVLAT:_C_KPE]
    kpe = rope(small[:, _C_KPE:_C_END])
    qn = _rms(q_lat, qg_ref[...]).astype(BF16)
    q = _dot(qn, wq_ref[...])
    q = jnp.concatenate([rope(q[:, hd * HEAD_PAD:(hd + 1) * HEAD_PAD]) for hd in range(N_HEADS)],
                        axis=-1)
    q_ref[...] = (q * Q_PRESCALE).astype(BF16)
    kvn = _rms(kv_lat, kvg_ref[...]).astype(BF16)
    k = _dot(kvn, wk_ref[...]) + jnp.concatenate([kpe] * N_HEADS, axis=-1)
    k_ref[...] = k.astype(BF16)
    lane = lax.broadcasted_iota(jnp.int32, (tm, N_HEADS * HEAD_PAD), 1)
    ones_col = jnp.where(lane % HEAD_PAD == V_HEAD, 1.0, 0.0)
    v_ref[...] = (_dot(kvn, wv_ref[...]) + ones_col).astype(BF16)

    ucb = _dot(hb, wconv_ref[...])
    cu = ucb[:, 0:CONV_WIDTH] * ucb[:, CONV_WIDTH:2 * CONV_WIDTH]
    b_gate = ucb[:, 2 * CONV_WIDTH:3 * CONV_WIDTH]

    @pl.when(i % tiles_per_seq == 0)
    def _():
        carry_ref[...] = jnp.zeros_like(carry_ref)

    prev = carry_ref[...]
    row = lax.broadcasted_iota(jnp.int32, cu.shape, 0)
    cu1 = jnp.where(row == 0, prev[7:8], pltpu.roll(cu, 1, 0))
    cu2 = jnp.where(row == 0, prev[6:7], jnp.where(row == 1, prev[7:8], pltpu.roll(cu, 2, 0)))
    cw = cw_ref[...]
    z = cw[2:3] * cu + cw[1:2] * cu1 + cw[0:1] * cu2
    carry_ref[...] = cu[tm - 8:tm]
    c_branch = _dot((b_gate * z).astype(BF16), wuc_ref[...])

    gates = _dot(hb, wgate_ref[...])
    sga_ref[...] = jax.nn.sigmoid(gates[:, 0:D_MODEL]).astype(BF16)
    gc_ref[...] = (jax.nn.sigmoid(gates[:, D_MODEL:]) * c_branch).astype(BF16)


def _pre(x2, mod, norm_g, w_lat, w_conv, w_gate, q_norm_g, wq2, kv_norm_g, wk2, wv, pos,
         freqs, conv_w, w_up_conv, seq):
    t = x2.shape[0]
    tm = ROW_TILE
    tiles_per_seq = seq // tm
    full = lambda a: pl.BlockSpec(a.shape, lambda i: (0,) * a.ndim)
    rows = lambda w: pl.BlockSpec((tm, w), lambda i: (i, 0))
    outs = [jax.ShapeDtypeStruct((t, N_HEADS * HEAD_PAD), BF16),
            jax.ShapeDtypeStruct((t, N_HEADS * HEAD_PAD), BF16),
            jax.ShapeDtypeStruct((t, N_HEADS * HEAD_PAD), BF16),
            jax.ShapeDtypeStruct((t, D_MODEL), BF16),
            jax.ShapeDtypeStruct((t, D_MODEL), BF16)]
    return pl.pallas_call(
        functools.partial(_pre_kernel, tiles_per_seq),
        out_shape=outs,
        grid=(t // tm,),
        in_specs=[
            rows(D_MODEL),
            pl.BlockSpec((None, 8, D_MODEL), lambda i: (i // tiles_per_seq, 0, 0)),
            full(norm_g), full(w_lat), full(w_conv), full(w_gate), full(q_norm_g), full(wq2),
            full(kv_norm_g), full(wk2), full(wv),
            pl.BlockSpec((None, 1, tm), lambda i: (i, 0, 0)), full(freqs), full(conv_w),
            full(w_up_conv),
        ],
        out_specs=[rows(N_HEADS * HEAD_PAD), rows(N_HEADS * HEAD_PAD), rows(N_HEADS * HEAD_PAD),
                   rows(D_MODEL), rows(D_MODEL)],
        scratch_shapes=[pltpu.VMEM((8, CONV_WIDTH), F32)],
        compiler_params=pltpu.CompilerParams(
            dimension_semantics=("arbitrary",), vmem_limit_bytes=VMEM_LIMIT),
        name="pre_mixer",
    )(x2, mod, norm_g, w_lat, w_conv, w_gate, q_norm_g, wq2, kv_norm_g, wk2, wv, pos,
      freqs, conv_w, w_up_conv)


def _attn_kernel(q_ref, k_ref, v_ref, sga_ref, gc_ref, wua_ref, o_ref, m_ref, acc_ref, wua_bf):
    i = pl.program_id(1)
    tq = q_ref.shape[0]

    @pl.when(jnp.logical_and(pl.program_id(0) == 0, i == 0))
    def _():
        wua_bf[...] = wua_ref[...].astype(BF16)

    def step(k0, tk, masked, first=False):
        if masked:
            rq = (lax.broadcasted_iota(jnp.int32, (tq, tk), 0) + (tk - tq)) // CHUNK
            ck = lax.broadcasted_iota(jnp.int32, (tq, tk), 1) // CHUNK
            allowed = ck <= rq
        for hd in range(N_HEADS):
            hs = slice(hd * HEAD_PAD, (hd + 1) * HEAD_PAD)
            s = lax.dot_general(q_ref[:, hs], k_ref[pl.ds(k0, tk), hs],
                                (((1,), (1,)), ((), ())), preferred_element_type=F32)
            if masked:
                s = jnp.where(allowed, s, NEG_BIG)
            s_max = s[:, 0:LANES]
            for c in range(1, tk // LANES):
                s_max = jnp.maximum(s_max, s[:, c * LANES:(c + 1) * LANES])
            m_new = jnp.broadcast_to(jnp.max(s_max, axis=-1, keepdims=True), (tq, LANES))
            if not first:
                m_old = m_ref[hd]
                m_new = jnp.maximum(m_old, m_new)
            p = jnp.concatenate(
                [jnp.exp2(s[:, c * LANES:(c + 1) * LANES] - m_new).astype(BF16)
                 for c in range(tk // LANES)], axis=-1)
            pv = _dot(p, v_ref[pl.ds(k0, tk), hs])
            acc_ref[hd] = pv if first else jnp.exp2(m_old - m_new) * acc_ref[hd] + pv
            m_ref[hd] = m_new

    wide = ATT_WIDE * tq
    n_wide = i // ATT_WIDE

    @pl.when(n_wide > 0)
    def _():
        step(0, wide, False, first=True)

    @pl.when(n_wide == 0)
    def _():
        m_ref[...] = jnp.full_like(m_ref, NEG_BIG)
        acc_ref[...] = jnp.zeros_like(acc_ref)

    def body(j, carry):
        step(pl.multiple_of(j * wide, wide), wide, False)
        return carry

    lax.fori_loop(1, n_wide, body, 0)

    for r in range(ATT_WIDE):
        @pl.when(i % ATT_WIDE == r)
        def _():
            step(pl.multiple_of((i - r) * tq, tq), (r + 1) * tq, True)

    heads = []
    for hd in range(N_HEADS):
        acc = acc_ref[hd]
        heads.append((acc[:, 0:V_HEAD] / acc[:, V_HEAD:V_HEAD + 1]).astype(BF16))
    a_branch = _dot(jnp.concatenate(heads, axis=-1), wua_bf[...])
    o_ref[...] = (sga_ref[...].astype(F32) * a_branch + gc_ref[...].astype(F32)).astype(BF16)


def _attention(q, k, v, sga, gc, wua, batch, seq):
    tq = ATT_BLOCK
    nq = seq // tq
    q_rows = lambda w: pl.BlockSpec((tq, w), lambda b, i: (b * nq + i, 0))
    whole_seq = pl.BlockSpec((seq, N_HEADS * HEAD_PAD), lambda b, i: (b, 0))
    return pl.pallas_call(
        _attn_kernel,
        out_shape=jax.ShapeDtypeStruct((batch * seq, D_MODEL), BF16),
        grid=(batch, nq),
        in_specs=[q_rows(N_HEADS * HEAD_PAD), whole_seq, whole_seq, q_rows(D_MODEL),
                  q_rows(D_MODEL), pl.BlockSpec(wua.shape, lambda b, i: (0, 0))],
        out_specs=q_rows(D_MODEL),
        scratch_shapes=[pltpu.VMEM((N_HEADS, tq, LANES), F32),
                        pltpu.VMEM((N_HEADS, tq, LANES), F32),
                        pltpu.VMEM(wua.shape, BF16)],
        compiler_params=pltpu.CompilerParams(
            dimension_semantics=("arbitrary", "arbitrary"), vmem_limit_bytes=VMEM_LIMIT),
        name="attention",
    )(q, k, v, sga, gc, wua)


def _post_kernel(merged_ref, x_ref, mod_ref, wo_ref, g_ref, rw_ref, rb_ref,
                 x1_ref, h2_ref, idx_ref, gate_ref, rank_ref, cnt_out_ref, cnt_ref, lg_ref,
                 wo_bf):
    i = pl.program_id(0)

    @pl.when(i == 0)
    def _():
        cnt_ref[...] = jnp.zeros_like(cnt_ref)
        lg_ref[...] = jnp.zeros_like(lg_ref)
        wo_bf[...] = wo_ref[...].astype(BF16)

    prev_logits = lg_ref[(i + 1) % 2]
    counts = cnt_ref[...]
    routed = counts
    for r0 in range(0, x_ref.shape[0], POST_SUB):
        rs = slice(r0, r0 + POST_SUB)
        lg_ref[i % 2, rs, :] = _post_mix(rs, merged_ref, x_ref, mod_ref, wo_bf, g_ref, rw_ref,
                                         rb_ref, x1_ref, h2_ref)
        routed = _post_route(rs, prev_logits[rs, :], routed, idx_ref, gate_ref, rank_ref)
    counts = jnp.where(i > 0, routed, counts)
    cnt_ref[...] = counts
    cnt_out_ref[...] = counts.astype(jnp.int32)


def _post_mix(rs, merged_ref, x_ref, mod_ref, wo_ref, g_ref, rw_ref, rb_ref, x1_ref, h2_ref):
    mod = mod_ref[...]
    mix = _dot(merged_ref[rs, :], wo_ref[...])
    x1 = x_ref[rs, :] + mod[2:3] * mix
    x1_ref[rs, :] = x1
    h2 = _rms(x1, g_ref[...]) * (1.0 + mod[4:5]) + mod[3:4]
    h2_ref[rs, :] = _pack_row(h2)

    h_hi = h2.astype(BF16)
    h_lo = (h2 - h_hi.astype(F32)).astype(BF16)
    both = _dot(h_hi, rw_ref[...])
    return both[:, :LANES] + both[:, LANES:] + _dot(h_lo, rw_ref[:, :LANES]) + rb_ref[...]


def _post_route(rs, logits, counts, idx_ref, gate_ref, rank_ref):
    lane = lax.broadcasted_iota(jnp.int32, logits.shape, 1)
    work = logits
    vals, idxs = [], []
    for _ in range(TOP_K):
        mk = jnp.max(work, axis=-1, keepdims=True)
        ik = jnp.min(jnp.where(work == mk, lane, LANES), axis=-1, keepdims=True)
        vals.append(mk)
        idxs.append(ik)
        work = jnp.where(lane == ik, -jnp.inf, work)
    es = [jnp.exp(vk - vals[0]) for vk in vals]
    denom = es[0] + es[1] + es[2] + es[3]
    tm = logits.shape[0]
    chosen = jnp.zeros(logits.shape, F32)
    for kk in range(TOP_K):
        chosen = chosen + jnp.where(lane == idxs[kk], 1.0, 0.0)
    r_i = lax.broadcasted_iota(jnp.int32, (tm, tm), 0)
    c_i = lax.broadcasted_iota(jnp.int32, (tm, tm), 1)
    earlier = jnp.where(c_i < r_i, 1.0, 0.0).astype(BF16)
    before = _dot(earlier, chosen.astype(BF16)) + counts[0:1]

    idx_out = jnp.zeros(logits.shape, F32)
    gate_out = jnp.zeros(logits.shape, F32)
    rank_out = jnp.zeros(logits.shape, F32)
    for kk in range(TOP_K):
        rank_k = jnp.sum(jnp.where(lane == idxs[kk], before, 0.0), axis=-1, keepdims=True)
        idx_out = jnp.where(lane == kk, idxs[kk].astype(F32), idx_out)
        gate_out = jnp.where(lane == kk, es[kk] / denom, gate_out)
        rank_out = jnp.where(lane == kk, rank_k, rank_out)
    gate_ref[rs, :] = gate_out
    idx_ref[:, rs] = idx_out.T[0:8].astype(jnp.int32)
    rank_ref[:, rs] = rank_out.T[0:8].astype(jnp.int32)
    return counts + jnp.sum(chosen, axis=0, keepdims=True)


def _post(merged, x2, mod, wo, norm_g, rw_hl, rb_pad, seq):
    t = x2.shape[0]
    tm = POST_TILE
    tiles_per_seq = seq // tm
    n_tiles = t // tm
    full = lambda a: pl.BlockSpec(a.shape, lambda i: (0,) * a.ndim)
    mix_tile = lambda i: jnp.minimum(i, n_tiles - 1)
    route_tile = lambda i: jnp.maximum(i - 1, 0)
    rows = lambda w: pl.BlockSpec((tm, w), lambda i: (mix_tile(i), 0))
    outs = [jax.ShapeDtypeStruct((t, D_MODEL), F32),
            jax.ShapeDtypeStruct((t, PACKED), jnp.uint32),
            jax.ShapeDtypeStruct((8, t), jnp.int32),
            jax.ShapeDtypeStruct((t, LANES), F32),
            jax.ShapeDtypeStruct((8, t), jnp.int32),
            jax.ShapeDtypeStruct((8, LANES), jnp.int32)]
    slots = pl.BlockSpec((8, tm), lambda i: (0, route_tile(i)))
    return pl.pallas_call(
        _post_kernel,
        out_shape=outs,
        grid=(n_tiles + 1,),
        in_specs=[
            rows(D_MODEL), rows(D_MODEL),
            pl.BlockSpec((None, 8, D_MODEL), lambda i: (mix_tile(i) // tiles_per_seq, 0, 0)),
            full(wo), full(norm_g), full(rw_hl), full(rb_pad),
        ],
        out_specs=[rows(D_MODEL), rows(PACKED), slots,
                   pl.BlockSpec((tm, LANES), lambda i: (route_tile(i), 0)), slots,
                   pl.BlockSpec((8, LANES), lambda i: (0, 0))],
        scratch_shapes=[pltpu.VMEM((8, LANES), F32), pltpu.VMEM((2, tm, LANES), F32),
                        pltpu.VMEM(wo.shape, BF16)],
        compiler_params=pltpu.CompilerParams(
            dimension_semantics=("arbitrary",), vmem_limit_bytes=VMEM_LIMIT),
        name="post_mixer",
    )(merged, x2, mod, wo, norm_g, rw_hl, rb_pad)


_GM_COUNT, _GM_FIRST, _GM_BLOCKS = range(3)
_ST_EXPERT, _ST_SLOT = range(2)


def _moe_kernel(gm_ref, xs_ref, wgu_hbm, bgu_ref, wd_hbm, bd_ref, o_ref,
                wgu_f, wd_f, wgu_bf, wd_bf, sem, st_ref):
    weights = (wgu_hbm, wd_hbm, wgu_f, wd_f, wgu_bf, wd_bf, sem)
    rows = (xs_ref, bgu_ref, bd_ref, o_ref, wgu_bf, wd_bf)
    b0 = pl.program_id(0) * MOE_STEP_BLOCKS
    e, in_group, used = _moe_enter(b0, gm_ref, st_ref, *weights)
    together = jnp.logical_and(used, in_group + MOE_STEP_BLOCKS <= gm_ref[_GM_BLOCKS, e])

    @pl.when(together)
    def _():
        _moe_rows(slice(0, MOE_STEP_BLOCKS * MOE_BLOCK), e, in_group, True, gm_ref, *rows)

    @pl.when(jnp.logical_not(together))
    def _():
        _moe_rows(slice(0, MOE_BLOCK), e, in_group, used, gm_ref, *rows)
        for r in range(1, MOE_STEP_BLOCKS):
            e_r, in_group_r, used_r = _moe_enter(b0 + r, gm_ref, st_ref, *weights)
            _moe_rows(slice(r * MOE_BLOCK, (r + 1) * MOE_BLOCK), e_r, in_group_r, used_r, gm_ref,
                      *rows)


def _moe_enter(b, gm_ref, st_ref, wgu_hbm, wd_hbm, wgu_f, wd_f, wgu_bf, wd_bf, sem):
    def weight_copies(expert, sl):
        return (pltpu.make_async_copy(wgu_hbm.at[expert], wgu_f.at[sl], sem.at[0, sl]),
                pltpu.make_async_copy(wd_hbm.at[expert], wd_f.at[sl], sem.at[1, sl]))

    def next_group(e):
        return lax.while_loop(
            lambda k: jnp.logical_and(k < N_EXPERTS,
                                      gm_ref[_GM_BLOCKS, jnp.minimum(k, N_EXPERTS - 1)] == 0),
            lambda k: k + 1, e)

    @pl.when(b == 0)
    def _():
        e0 = next_group(0)
        st_ref[_ST_EXPERT] = e0
        st_ref[_ST_SLOT] = 1
        for cp in weight_copies(e0, 0):
            cp.start()

    e_prev = st_ref[_ST_EXPERT]
    past = b >= gm_ref[_GM_FIRST, e_prev] + gm_ref[_GM_BLOCKS, e_prev]
    e = jnp.minimum(jnp.where(past, next_group(e_prev + 1), e_prev), N_EXPERTS - 1)
    st_ref[_ST_EXPERT] = e
    in_group = b - gm_ref[_GM_FIRST, e]
    used = jnp.logical_and(in_group >= 0, in_group < gm_ref[_GM_BLOCKS, e])

    @pl.when(jnp.logical_and(used, in_group == 0))
    def _():
        slot = 1 - st_ref[_ST_SLOT]
        st_ref[_ST_SLOT] = slot
        nxt = next_group(e + 1)

        @pl.when(nxt < N_EXPERTS)
        def _():
            for cp in weight_copies(nxt, 1 - slot):
                cp.start(priority=1)

        for cp in weight_copies(e, slot):
            cp.wait()
        wgu_bf[...] = wgu_f[slot].astype(BF16)
        wd_bf[...] = wd_f[slot].astype(BF16)

    return e, in_group, used


def _moe_rows(rs, e, in_group, used, gm_ref, xs_ref, bgu_ref, bd_ref, o_ref, wgu_bf, wd_bf):
    n_valid = gm_ref[_GM_COUNT, e] - in_group * MOE_BLOCK

    def ffn(r):
        n = r.stop - r.start
        row = lax.broadcasted_iota(jnp.int32, (n, PACKED), 0)
        xs = _unpack_row(jnp.where(row < n_valid, xs_ref[r, :], 0)).astype(BF16)
        gu = _dot(xs, wgu_bf[...]) + bgu_ref[e]
        gate = jnp.minimum(gu[:, :D_EXPERT], SWIGLU_LIMIT)
        up = jnp.clip(gu[:, D_EXPERT:], -SWIGLU_LIMIT, SWIGLU_LIMIT)
        act = (up + 1.0) * (gate * jax.nn.sigmoid(gate * SWIGLU_ALPHA))
        o_ref[r, :] = _pack_row(_dot(act.astype(BF16), wd_bf[...]) + bd_ref[e])

    def zeros(r):
        o_ref[r, :] = jnp.zeros((r.stop - r.start, PACKED), o_ref.dtype)

    if used is True:
        ffn(rs)
        return

    mid = rs.start + (rs.stop - rs.start) // 2
    short = n_valid <= mid - rs.start

    @pl.when(jnp.logical_and(used, jnp.logical_not(short)))
    def _():
        ffn(rs)

    @pl.when(jnp.logical_and(used, short))
    def _():
        ffn(slice(rs.start, mid))
        zeros(slice(mid, rs.stop))

    @pl.when(jnp.logical_not(used))
    def _():
        zeros(rs)


def _moe(group_table, xs, w_gu, b_gu, w_down, b_down):
    n_rows = xs.shape[0]
    step_rows = MOE_STEP_BLOCKS * MOE_BLOCK
    assert n_rows % step_rows == 0
    grid_spec = pltpu.PrefetchScalarGridSpec(
        num_scalar_prefetch=1,
        grid=(n_rows // step_rows,),
        in_specs=[
            pl.BlockSpec((step_rows, PACKED), lambda b, gm: (b, 0)),
            pl.BlockSpec(memory_space=pl.ANY),
            pl.BlockSpec(b_gu.shape, lambda b, gm: (0, 0, 0)),
            pl.BlockSpec(memory_space=pl.ANY),
            pl.BlockSpec(b_down.shape, lambda b, gm: (0, 0, 0)),
        ],
        out_specs=pl.BlockSpec((step_rows, PACKED), lambda b, gm: (b, 0)),
        scratch_shapes=[pltpu.VMEM((2, D_MODEL, 2 * D_EXPERT), F32),
                        pltpu.VMEM((2, D_EXPERT, D_MODEL), F32),
                        pltpu.VMEM((D_MODEL, 2 * D_EXPERT), BF16),
                        pltpu.VMEM((D_EXPERT, D_MODEL), BF16),
                        pltpu.SemaphoreType.DMA((2, 2)),
                        pltpu.SMEM((2,), jnp.int32)],
    )
    return pl.pallas_call(
        _moe_kernel,
        out_shape=jax.ShapeDtypeStruct((n_rows, PACKED), jnp.uint32),
        grid_spec=grid_spec,
        compiler_params=pltpu.CompilerParams(
            dimension_semantics=("arbitrary",), vmem_limit_bytes=VMEM_LIMIT),
        name="moe_experts",
    )(group_table, xs, w_gu, b_gu, w_down, b_down)


def _final_kernel(last_layer, x1_ref, y_ref, gate_ref, mod_ref, g_ref, o_ref):
    mod = mod_ref[...]
    gate = gate_ref[...]
    ffn = gate[:, 0:1] * _unpack_row(y_ref[0])
    for kk in range(1, TOP_K):
        ffn = ffn + gate[:, kk:kk + 1] * _unpack_row(y_ref[kk])
    x = x1_ref[...] + mod[5:6] * ffn
    o_ref[...] = _rms(x, g_ref[...]) if last_layer else x


def _final(x1, y_kt, gate, mod, norm_g, seq, last_layer):
    t = x1.shape[0]
    tm = ROW_TILE
    tiles_per_seq = seq // tm
    rows = lambda w: pl.BlockSpec((tm, w), lambda i: (i, 0))
    return pl.pallas_call(
        functools.partial(_final_kernel, last_layer),
        out_shape=jax.ShapeDtypeStruct((t, D_MODEL), F32),
        grid=(t // tm,),
        in_specs=[
            rows(D_MODEL), pl.BlockSpec((TOP_K, tm, PACKED), lambda i: (0, i, 0)), rows(LANES),
            pl.BlockSpec((None, 8, D_MODEL), lambda i: (i // tiles_per_seq, 0, 0)),
            pl.BlockSpec(norm_g.shape, lambda i: (0, 0)),
        ],
        out_specs=rows(D_MODEL),
        compiler_params=pltpu.CompilerParams(
            dimension_semantics=("arbitrary",), vmem_limit_bytes=VMEM_LIMIT),
        name="combine_final",
    )(x1, y_kt, gate, mod, norm_g)


def _prep_weights(w_in, w_uq, w_ukv):
    d = w_in.shape[0]
    splits = (Q_LORA, KV_LORA, QK_ROPE, CONV_WIDTH, CONV_WIDTH, CONV_WIDTH, D_MODEL, D_MODEL)
    offs = [0]
    for s in splits:
        offs.append(offs[-1] + s)
    part = lambda n: w_in[:, offs[n]:offs[n + 1]]
    z = lambda n: jnp.zeros((d, n), w_in.dtype)
    w_kpe = part(2)
    kpe_slab = jnp.concatenate([z(QK_NOPE), w_kpe, z(HEAD_PAD - QK_HEAD)], axis=1)
    w_lat = jnp.concatenate([part(0), part(1), kpe_slab], axis=1).astype(BF16)
    w_conv = w_in[:, offs[3]:offs[6]].astype(BF16)
    w_gate = w_in[:, offs[6]:offs[8]].astype(BF16)

    wq = w_uq.reshape(Q_LORA, N_HEADS, QK_HEAD)
    zq = lambda n: jnp.zeros((Q_LORA, N_HEADS, n), w_uq.dtype)
    wq2 = jnp.concatenate([wq, zq(HEAD_PAD - QK_HEAD)], axis=-1)
    wq2 = wq2.reshape(Q_LORA, N_HEADS * HEAD_PAD).astype(BF16)

    wkv = w_ukv.reshape(KV_LORA, N_HEADS, QK_NOPE + V_HEAD)
    wk2 = jnp.concatenate([wkv[..., :QK_NOPE],
                           jnp.zeros((KV_LORA, N_HEADS, HEAD_PAD - QK_NOPE), w_ukv.dtype)], axis=-1)
    wk2 = wk2.reshape(KV_LORA, N_HEADS * HEAD_PAD).astype(BF16)
    wv = jnp.concatenate([wkv[..., QK_NOPE:],
                          jnp.zeros((KV_LORA, N_HEADS, HEAD_PAD - V_HEAD), w_ukv.dtype)], axis=-1)
    wv = wv.reshape(KV_LORA, N_HEADS * HEAD_PAD).astype(BF16)
    return w_lat, w_conv, w_gate, wq2, wk2, wv


def _rope_freqs():
    inv_freq = 1.0 / (ROPE_THETA ** (jnp.arange(0, QK_ROPE, 2, dtype=F32) / QK_ROPE))
    return inv_freq.reshape(QK_ROPE // 2, 1)


def _dest_kernel(gm_ref, idx_ref, rank_ref, o_ref):
    idx = idx_ref[...]
    dest = rank_ref[...]
    for e in range(N_EXPERTS):
        dest = dest + jnp.where(idx == e, gm_ref[_GM_FIRST, e] * MOE_BLOCK, 0)
    o_ref[...] = dest


def _route(top_idx, rank, counts, n_tokens):
    blocks = (counts + MOE_BLOCK - 1) // MOE_BLOCK
    first_block = jnp.cumsum(blocks) - blocks
    table = jnp.stack([counts, first_block, blocks]).astype(jnp.int32)
    whole = pl.BlockSpec(top_idx.shape, lambda i, gm: (0, 0))
    dest = pl.pallas_call(
        _dest_kernel,
        out_shape=jax.ShapeDtypeStruct(rank.shape, jnp.int32),
        grid_spec=pltpu.PrefetchScalarGridSpec(
            num_scalar_prefetch=1, grid=(1,), in_specs=[whole, whole], out_specs=whole),
        name="row_destinations",
    )(table, top_idx, rank)[:TOP_K]
    n_rows = n_tokens * TOP_K + N_EXPERTS * MOE_BLOCK
    return dest, table, n_rows


SC_CORES = 2
SC_SUBCORES = 16
SC_WORKERS = SC_CORES * SC_SUBCORES
SC_CHUNK = 64
SC_GATHER_RING = 3

def _sc_mesh():
    return plsc.VectorSubcoreMesh(core_axis_name="c", subcore_axis_name="s")


def _sc_worker():
    return lax.axis_index("s") * SC_CORES + lax.axis_index("c")


def _dispatch(h2, dest, n_rows):
    t, d = h2.shape
    per_w = t // SC_WORKERS
    n_chunks = per_w // SC_CHUNK
    assert per_w % (2 * SC_CHUNK) == 0
    idx = dest.reshape(TOP_K, SC_WORKERS, n_chunks, SC_CHUNK).transpose(1, 0, 2, 3)
    idx = idx.reshape(SC_WORKERS, TOP_K * n_chunks, SC_CHUNK)

    @functools.partial(
        pl.kernel, mesh=_sc_mesh(),
        out_type=jax.ShapeDtypeStruct((n_rows, d), h2.dtype),
        scratch_types=[pltpu.VMEM((TOP_K * n_chunks, SC_CHUNK), jnp.int32),
                       pltpu.VMEM((2, SC_CHUNK, d), h2.dtype),
                       pltpu.SemaphoreType.DMA((2,)),
                       pltpu.SemaphoreType.DMA((2,))],
        name="moe_dispatch")
    def run(h2_hbm, idx_hbm, xs_hbm, idx_v, rows_v, rsem, ssem):
        w = _sc_worker()
        pltpu.sync_copy(idx_hbm.at[w], idx_v)

        def read(g, b):
            src = h2_hbm.at[pl.ds(w * per_w + g * SC_CHUNK, SC_CHUNK)]
            return pltpu.make_async_copy(src, rows_v.at[b], rsem.at[b])

        def scatter(g, kk, b):
            dst = xs_hbm.at[idx_v.at[kk * n_chunks + g]]
            return pltpu.make_async_copy(rows_v.at[b], dst, ssem.at[b])

        read(0, 0).start()

        @pl.loop(0, n_chunks, step=2)
        def _(g0):
            for b in range(2):
                g = g0 + b
                read(g, b).wait()

                @pl.when(g + 1 < n_chunks)
                def _():
                    read(g + 1, 1 - b).start()

                for kk in range(TOP_K):
                    scatter(g, kk, b).start()
                for kk in range(TOP_K):
                    scatter(g, kk, b).wait()

    return run(h2, idx)


def _undispatch(ys, dest):
    t = dest.shape[1]
    d = ys.shape[1]
    n_out = t * TOP_K
    per_w = n_out // SC_WORKERS
    n_chunks = per_w // SC_CHUNK
    idx = dest.reshape(SC_WORKERS, n_chunks, SC_CHUNK)
    ring = SC_GATHER_RING

    @functools.partial(
        pl.kernel, mesh=_sc_mesh(),
        out_type=jax.ShapeDtypeStruct((n_out, d), ys.dtype),
        scratch_types=[pltpu.VMEM((n_chunks, SC_CHUNK), jnp.int32),
                       pltpu.VMEM((ring, SC_CHUNK, d), ys.dtype),
                       pltpu.SemaphoreType.DMA((ring,)),
                       pltpu.SemaphoreType.DMA((ring,))],
        name="moe_undispatch")
    def run(ys_hbm, idx_hbm, out_hbm, idx_v, rows_v, gsem, wsem):
        w = _sc_worker()
        pltpu.sync_copy(idx_hbm.at[w], idx_v)

        def gather(g):
            b = g % ring
            return pltpu.make_async_copy(ys_hbm.at[idx_v.at[g]], rows_v.at[b], gsem.at[b])

        def write(g):
            b = g % ring
            dst = out_hbm.at[pl.ds(w * per_w + g * SC_CHUNK, SC_CHUNK)]
            return pltpu.make_async_copy(rows_v.at[b], dst, wsem.at[b])

        for g in range(min(ring - 1, n_chunks)):
            gather(g).start()
        for g in range(n_chunks):
            gather(g).wait()
            ahead = g + ring - 1
            if ahead < n_chunks:
                if g >= 1:
                    write(g - 1).wait()
                gather(ahead).start()
            write(g).start()
        for g in range(max(n_chunks - ring, 0), n_chunks):
            write(g).wait()

    return run(ys, idx).reshape(TOP_K, t, d)


def kernel(x, c, positions, w_ada, b_ada, norm_mix_g, w_in, q_norm_g, w_uq, kv_norm_g, w_ukv,
           w_up_attn, conv_w, w_up_conv, w_o, norm_ffn_g, router_w, router_b, w_gu, b_gu,
           w_down, b_down, norm_final_g):
    batch, seq, d = x.shape
    t = batch * seq
    depth = w_ada.shape[0]
    assert d == D_MODEL and batch <= 8 and conv_w.shape[1:] == (CONV_K, CONV_WIDTH)
    assert seq % ROW_TILE == 0 and seq % POST_TILE == 0 and seq % ATT_BLOCK == 0
    assert t % (2 * SC_CHUNK * SC_WORKERS) == 0
    x2 = x.reshape(t, d)
    pos = positions.astype(F32).reshape(t // ROW_TILE, 1, ROW_TILE)
    freqs = _rope_freqs()
    c_pad = jnp.zeros((8, d), F32).at[:batch].set(c)

    for l in range(depth):
        ada = _ada(c_pad, w_ada[l], b_ada[l].reshape(1, -1))
        mod = ada[:batch].reshape(batch, 6, d)
        mod = jnp.concatenate([mod, jnp.zeros((batch, 2, d), F32)], axis=1)

        w_lat, w_conv, w_gate, wq2, wk2, wv = _prep_weights(w_in[l], w_uq[l], w_ukv[l])
        q, k, v, sga, gc = _pre(x2, mod, norm_mix_g[l].reshape(1, d), w_lat, w_conv, w_gate,
                                q_norm_g[l].reshape(1, -1), wq2,
                                kv_norm_g[l].reshape(1, -1), wk2, wv, pos, freqs, conv_w[l],
                                w_up_conv[l].astype(BF16), seq)
        merged = _attention(q, k, v, sga, gc, w_up_attn[l], batch, seq)

        rw_pad = jnp.concatenate([router_w[l], jnp.zeros((d, LANES - N_EXPERTS), F32)], axis=1)
        rb_pad = jnp.concatenate([router_b[l], jnp.full((LANES - N_EXPERTS,), NEG_BIG, F32)])
        rw_hi = rw_pad.astype(BF16)
        rw_lo = (rw_pad - rw_hi.astype(F32)).astype(BF16)
        x1, h2, idx_pad, gate_pad, rank_pad, counts = _post(
            merged, x2, mod, w_o[l], norm_ffn_g[l].reshape(1, d),
            jnp.concatenate([rw_hi, rw_lo], axis=1), rb_pad.reshape(1, LANES), seq)

        dest, group_table, n_rows = _route(
            idx_pad, rank_pad, counts[0, :N_EXPERTS], t)
        xs = _dispatch(h2, dest, n_rows)
        ys = _moe(group_table, xs, w_gu[l], b_gu[l].reshape(N_EXPERTS, 1, -1),
                  w_down[l], b_down[l].reshape(N_EXPERTS, 1, -1))
        y_kt = _undispatch(ys, dest)
        x2 = _final(x1, y_kt, gate_pad, mod, norm_final_g.reshape(1, d), seq, l == depth - 1)

    return x2.reshape(batch, seq, d)
```

```python
import functools
import math

import jax
import jax.numpy as jnp
from jax import lax
from jax.experimental import pallas as pl
from jax.experimental.pallas import tpu as pltpu
from jax.experimental.pallas import tpu_sc as plsc

D_MODEL = 1024
CHUNK = 64
N_HEADS = 8
Q_LORA = 256
KV_LORA = 128
QK_NOPE = 64
QK_ROPE = 32
V_HEAD = 64
QK_HEAD = QK_NOPE + QK_ROPE
ROPE_THETA = 10000.0
CONV_WIDTH = 512
CONV_K = 3
N_EXPERTS = 32
TOP_K = 4
D_EXPERT = 1024
SWIGLU_LIMIT = 7.0
SWIGLU_ALPHA = 1.702
MOE_BLOCK = 256
RMS_EPS = 1e-6

LANES = 128
HEAD_PAD = 128
NEG_BIG = -1e30
VMEM_LIMIT = 56 * 1024 * 1024

F32 = jnp.float32
BF16 = jnp.bfloat16

Q_PRESCALE = (QK_HEAD ** -0.5) * math.log2(math.e)

ADA_TILE = 1024
ROW_TILE = 1024
MOE_STEP_BLOCKS = 4
POST_TILE = 1024
POST_SUB = 512
ATT_BLOCK = 512
ATT_WIDE = 2


def _rms(x, g):
    ms = jnp.mean(x * x, axis=-1, keepdims=True)
    return x * lax.rsqrt(ms + RMS_EPS) * g


def _dot(a, b):
    return jnp.dot(a, b, preferred_element_type=F32)


PACKED = D_MODEL // 2


def _pack_row(x):
    return pltpu.pack_elementwise([x[:, :PACKED], x[:, PACKED:]], packed_dtype=BF16)


def _unpack_row(w):
    half = lambda i: pltpu.unpack_elementwise(w, index=i, packed_dtype=BF16, unpacked_dtype=F32)
    return jnp.concatenate([half(0), half(1)], axis=-1)


def _ada_kernel(c_ref, w_ref, b_ref, o_ref):
    c = c_ref[...]
    ca = (c * jax.nn.sigmoid(c)).astype(BF16)
    o_ref[...] = _dot(ca, w_ref[...].astype(BF16)) + b_ref[...]


def _ada(c_pad, w_ada, b_ada):
    n = w_ada.shape[1]
    tn = ADA_TILE
    return pl.pallas_call(
        _ada_kernel,
        out_shape=jax.ShapeDtypeStruct((c_pad.shape[0], n), F32),
        grid=(n // tn,),
        in_specs=[
            pl.BlockSpec(c_pad.shape, lambda j: (0, 0)),
            pl.BlockSpec((D_MODEL, tn), lambda j: (0, j)),
            pl.BlockSpec((1, tn), lambda j: (0, j)),
        ],
        out_specs=pl.BlockSpec((c_pad.shape[0], tn), lambda j: (0, j)),
        compiler_params=pltpu.CompilerParams(
            dimension_semantics=("arbitrary",), vmem_limit_bytes=VMEM_LIMIT),
        name="ada",
    )(c_pad, w_ada, b_ada)


_C_QLAT = 0
_C_KVLAT = _C_QLAT + Q_LORA
_C_KPE = _C_KVLAT + KV_LORA
_C_END = _C_KPE + HEAD_PAD


def _pre_kernel(tiles_per_seq, x_ref, mod_ref, g_ref, wlat_ref, wconv_ref, wgate_ref, qg_ref,
                wq_ref, kvg_ref, wk_ref, wv_ref, pos_ref, freq_ref, cw_ref, wuc_ref,
                q_ref, k_ref, v_ref, sga_ref, gc_ref, carry_ref):
    i = pl.program_id(0)
    tm = x_ref.shape[0]
    mod = mod_ref[...]
    h = _rms(x_ref[...], g_ref[...]) * (1.0 + mod[1:2]) + mod[0:1]
    hb = h.astype(BF16)

    ang = freq_ref[...] * pos_ref[...]
    cos_t, sin_t = jnp.cos(ang), jnp.sin(ang)
    ones_t = jnp.ones((QK_NOPE, tm), F32)
    zeros_t = jnp.zeros((QK_NOPE, tm), F32)
    pad_t = jnp.zeros((HEAD_PAD - QK_HEAD, tm), F32)
    cosf = jnp.concatenate([ones_t, cos_t, cos_t, pad_t], axis=0).T
    sinf = jnp.concatenate([zeros_t, -sin_t, sin_t, pad_t], axis=0).T

    first_half = lax.broadcasted_iota(jnp.int32, (tm, HEAD_PAD), 1) < QK_NOPE + QK_ROPE // 2

    def rope(slab):
        swapped = jnp.where(first_half, pltpu.roll(slab, HEAD_PAD - QK_ROPE // 2, 1),
                            pltpu.roll(slab, QK_ROPE // 2, 1))
        return slab * cosf + swapped * sinf

    small = _dot(hb, wlat_ref[...])
    q_lat = small[:, _C_QLAT:_C_KVLAT]
    kv_lat = small[:, _C_KVLAT:_C_KPE]
    kpe = rope(small[:, _C_KPE:_C_END])
    qn = _rms(q_lat, qg_ref[...]).astype(BF16)
    q = _dot(qn, wq_ref[...])
    q = jnp.concatenate([rope(q[:, hd * HEAD_PAD:(hd + 1) * HEAD_PAD]) for hd in range(N_HEADS)],
                        axis=-1)
    q_ref[...] = (q * Q_PRESCALE).astype(BF16)
    kvn = _rms(kv_lat, kvg_ref[...]).astype(BF16)
    k = _dot(kvn, wk_ref[...]) + jnp.concatenate([kpe] * N_HEADS, axis=-1)
    k_ref[...] = k.astype(BF16)
    lane = lax.broadcasted_iota(jnp.int32, (tm, N_HEADS * HEAD_PAD), 1)
    ones_col = jnp.where(lane % HEAD_PAD == V_HEAD, 1.0, 0.0)
    v_ref[...] = (_dot(kvn, wv_ref[...]) + ones_col).astype(BF16)

    ucb = _dot(hb, wconv_ref[...])
    cu = ucb[:, 0:CONV_WIDTH] * ucb[:, CONV_WIDTH:2 * CONV_WIDTH]
    b_gate = ucb[:, 2 * CONV_WIDTH:3 * CONV_WIDTH]

    @pl.when(i % tiles_per_seq == 0)
    def _():
        carry_ref[...] = jnp.zeros_like(carry_ref)

    prev = carry_ref[...]
    row = lax.broadcasted_iota(jnp.int32, cu.shape, 0)
    cu1 = jnp.where(row == 0, prev[7:8], pltpu.roll(cu, 1, 0))
    cu2 = jnp.where(row == 0, prev[6:7], jnp.where(row == 1, prev[7:8], pltpu.roll(cu, 2, 0)))
    cw = cw_ref[...]
    z = cw[2:3] * cu + cw[1:2] * cu1 + cw[0:1] * cu2
    carry_ref[...] = cu[tm - 8:tm]
    c_branch = _dot((b_gate * z).astype(BF16), wuc_ref[...])

    gates = _dot(hb, wgate_ref[...])
    sga_ref[...] = jax.nn.sigmoid(gates[:, 0:D_MODEL]).astype(BF16)
    gc_ref[...] = (jax.nn.sigmoid(gates[:, D_MODEL:]) * c_branch).astype(BF16)


def _pre(x2, mod, norm_g, w_lat, w_conv, w_gate, q_norm_g, wq2, kv_norm_g, wk2, wv, pos,
         freqs, conv_w, w_up_conv, seq):
    t = x2.shape[0]
    tm = ROW_TILE
    tiles_per_seq = seq // tm
    full = lambda a: pl.BlockSpec(a.shape, lambda i: (0,) * a.ndim)
    rows = lambda w: pl.BlockSpec((tm, w), lambda i: (i, 0))
    outs = [jax.ShapeDtypeStruct((t, N_HEADS * HEAD_PAD), BF16),
            jax.ShapeDtypeStruct((t, N_HEADS * HEAD_PAD), BF16),
            jax.ShapeDtypeStruct((t, N_HEADS * HEAD_PAD), BF16),
            jax.ShapeDtypeStruct((t, D_MODEL), BF16),
            jax.ShapeDtypeStruct((t, D_MODEL), BF16)]
    return pl.pallas_call(
        functools.partial(_pre_kernel, tiles_per_seq),
        out_shape=outs,
        grid=(t // tm,),
        in_specs=[
            rows(D_MODEL),
            pl.BlockSpec((None, 8, D_MODEL), lambda i: (i // tiles_per_seq, 0, 0)),
            full(norm_g), full(w_lat), full(w_conv), full(w_gate), full(q_norm_g), full(wq2),
            full(kv_norm_g), full(wk2), full(wv),
            pl.BlockSpec((None, 1, tm), lambda i: (i, 0, 0)), full(freqs), full(conv_w),
            full(w_up_conv),
        ],
        out_specs=[rows(N_HEADS * HEAD_PAD), rows(N_HEADS * HEAD_PAD), rows(N_HEADS * HEAD_PAD),
                   rows(D_MODEL), rows(D_MODEL)],
        scratch_shapes=[pltpu.VMEM((8, CONV_WIDTH), F32)],
        compiler_params=pltpu.CompilerParams(
            dimension_semantics=("arbitrary",), vmem_limit_bytes=VMEM_LIMIT),
        name="pre_mixer",
    )(x2, mod, norm_g, w_lat, w_conv, w_gate, q_norm_g, wq2, kv_norm_g, wk2, wv, pos,
      freqs, conv_w, w_up_conv)


def _attn_kernel(q_ref, k_ref, v_ref, sga_ref, gc_ref, wua_ref, o_ref, m_ref, acc_ref, wua_bf):
    i = pl.program_id(1)
    tq = q_ref.shape[0]

    @pl.when(jnp.logical_and(pl.program_id(0) == 0, i == 0))
    def _():
        wua_bf[...] = wua_ref[...].astype(BF16)

    def step(k0, tk, masked, first=False):
        if masked:
            rq = (lax.broadcasted_iota(jnp.int32, (tq, tk), 0) + (tk - tq)) // CHUNK
            ck = lax.broadcasted_iota(jnp.int32, (tq, tk), 1) // CHUNK
            allowed = ck <= rq
        for hd in range(N_HEADS):
            hs = slice(hd * HEAD_PAD, (hd + 1) * HEAD_PAD)
            s = lax.dot_general(q_ref[:, hs], k_ref[pl.ds(k0, tk), hs],
                                (((1,), (1,)), ((), ())), preferred_element_type=F32)
            if masked:
                s = jnp.where(allowed, s, NEG_BIG)
            s_max = s[:, 0:LANES]
            for c in range(1, tk // LANES):
                s_max = jnp.maximum(s_max, s[:, c * LANES:(c + 1) * LANES])
            m_new = jnp.broadcast_to(jnp.max(s_max, axis=-1, keepdims=True), (tq, LANES))
            if not first:
                m_old = m_ref[hd]
                m_new = jnp.maximum(m_old, m_new)
            p = jnp.concatenate(
                [jnp.exp2(s[:, c * LANES:(c + 1) * LANES] - m_new).astype(BF16)
                 for c in range(tk // LANES)], axis=-1)
            pv = _dot(p, v_ref[pl.ds(k0, tk), hs])
            acc_ref[hd] = pv if first else jnp.exp2(m_old - m_new) * acc_ref[hd] + pv
            m_ref[hd] = m_new

    wide = ATT_WIDE * tq
    n_wide = i // ATT_WIDE

    @pl.when(n_wide > 0)
    def _():
        step(0, wide, False, first=True)

    @pl.when(n_wide == 0)
    def _():
        m_ref[...] = jnp.full_like(m_ref, NEG_BIG)
        acc_ref[...] = jnp.zeros_like(acc_ref)

    def body(j, carry):
        step(pl.multiple_of(j * wide, wide), wide, False)
        return carry

    lax.fori_loop(1, n_wide, body, 0)

    for r in range(ATT_WIDE):
        @pl.when(i % ATT_WIDE == r)
        def _():
            step(pl.multiple_of((i - r) * tq, tq), (r + 1) * tq, True)

    heads = []
    for hd in range(N_HEADS):
        acc = acc_ref[hd]
        heads.append((acc[:, 0:V_HEAD] / acc[:, V_HEAD:V_HEAD + 1]).astype(BF16))
    a_branch = _dot(jnp.concatenate(heads, axis=-1), wua_bf[...])
    o_ref[...] = (sga_ref[...].astype(F32) * a_branch + gc_ref[...].astype(F32)).astype(BF16)


def _attention(q, k, v, sga, gc, wua, batch, seq):
    tq = ATT_BLOCK
    nq = seq // tq
    q_rows = lambda w: pl.BlockSpec((tq, w), lambda b, i: (b * nq + i, 0))
    whole_seq = pl.BlockSpec((seq, N_HEADS * HEAD_PAD), lambda b, i: (b, 0))
    return pl.pallas_call(
        _attn_kernel,
        out_shape=jax.ShapeDtypeStruct((batch * seq, D_MODEL), BF16),
        grid=(batch, nq),
        in_specs=[q_rows(N_HEADS * HEAD_PAD), whole_seq, whole_seq, q_rows(D_MODEL),
                  q_rows(D_MODEL), pl.BlockSpec(wua.shape, lambda b, i: (0, 0))],
        out_specs=q_rows(D_MODEL),
        scratch_shapes=[pltpu.VMEM((N_HEADS, tq, LANES), F32),
                        pltpu.VMEM((N_HEADS, tq, LANES), F32),
                        pltpu.VMEM(wua.shape, BF16)],
        compiler_params=pltpu.CompilerParams(
            dimension_semantics=("arbitrary", "arbitrary"), vmem_limit_bytes=VMEM_LIMIT),
        name="attention",
    )(q, k, v, sga, gc, wua)


def _post_kernel(merged_ref, x_ref, mod_ref, wo_ref, g_ref, rw_ref, rb_ref,
                 x1_ref, h2_ref, idx_ref, gate_ref, rank_ref, cnt_out_ref, cnt_ref, lg_ref,
                 wo_bf):
    i = pl.program_id(0)

    @pl.when(i == 0)
    def _():
        cnt_ref[...] = jnp.zeros_like(cnt_ref)
        lg_ref[...] = jnp.zeros_like(lg_ref)
        wo_bf[...] = wo_ref[...].astype(BF16)

    prev_logits = lg_ref[(i + 1) % 2]
    counts = cnt_ref[...]
    routed = counts
    for r0 in range(0, x_ref.shape[0], POST_SUB):
        rs = slice(r0, r0 + POST_SUB)
        lg_ref[i % 2, rs, :] = _post_mix(rs, merged_ref, x_ref, mod_ref, wo_bf, g_ref, rw_ref,
                                         rb_ref, x1_ref, h2_ref)
        routed = _post_route(rs, prev_logits[rs, :], routed, idx_ref, gate_ref, rank_ref)
    counts = jnp.where(i > 0, routed, counts)
    cnt_ref[...] = counts
    cnt_out_ref[...] = counts.astype(jnp.int32)


def _post_mix(rs, merged_ref, x_ref, mod_ref, wo_ref, g_ref, rw_ref, rb_ref, x1_ref, h2_ref):
    mod = mod_ref[...]
    mix = _dot(merged_ref[rs, :], wo_ref[...])
    x1 = x_ref[rs, :] + mod[2:3] * mix
    x1_ref[rs, :] = x1
    h2 = _rms(x1, g_ref[...]) * (1.0 + mod[4:5]) + mod[3:4]
    h2_ref[rs, :] = _pack_row(h2)

    h_hi = h2.astype(BF16)
    h_lo = (h2 - h_hi.astype(F32)).astype(BF16)
    both = _dot(h_hi, rw_ref[...])
    return both[:, :LANES] + both[:, LANES:] + _dot(h_lo, rw_ref[:, :LANES]) + rb_ref[...]


def _post_route(rs, logits, counts, idx_ref, gate_ref, rank_ref):
    lane = lax.broadcasted_iota(jnp.int32, logits.shape, 1)
    work = logits
    vals, idxs = [], []
    for _ in range(TOP_K):
        mk = jnp.max(work, axis=-1, keepdims=True)
        ik = jnp.min(jnp.where(work == mk, lane, LANES), axis=-1, keepdims=True)
        vals.append(mk)
        idxs.append(ik)
        work = jnp.where(lane == ik, -jnp.inf, work)
    es = [jnp.exp(vk - vals[0]) for vk in vals]
    denom = es[0] + es[1] + es[2] + es[3]
    tm = logits.shape[0]
    chosen = jnp.zeros(logits.shape, F32)
    for kk in range(TOP_K):
        chosen = chosen + jnp.where(lane == idxs[kk], 1.0, 0.0)
    r_i = lax.broadcasted_iota(jnp.int32, (tm, tm), 0)
    c_i = lax.broadcasted_iota(jnp.int32, (tm, tm), 1)
    earlier = jnp.where(c_i < r_i, 1.0, 0.0).astype(BF16)
    before = _dot(earlier, chosen.astype(BF16)) + counts[0:1]

    idx_out = jnp.zeros(logits.shape, F32)
    gate_out = jnp.zeros(logits.shape, F32)
    rank_out = jnp.zeros(logits.shape, F32)
    for kk in range(TOP_K):
        rank_k = jnp.sum(jnp.where(lane == idxs[kk], before, 0.0), axis=-1, keepdims=True)
        idx_out = jnp.where(lane == kk, idxs[kk].astype(F32), idx_out)
        gate_out = jnp.where(lane == kk, es[kk] / denom, gate_out)
        rank_out = jnp.where(lane == kk, rank_k, rank_out)
    gate_ref[rs, :] = gate_out
    idx_ref[:, rs] = idx_out.T[0:8].astype(jnp.int32)
    rank_ref[:, rs] = rank_out.T[0:8].astype(jnp.int32)
    return counts + jnp.sum(chosen, axis=0, keepdims=True)


def _post(merged, x2, mod, wo, norm_g, rw_hl, rb_pad, seq):
    t = x2.shape[0]
    tm = POST_TILE
    tiles_per_seq = seq // tm
    n_tiles = t // tm
    full = lambda a: pl.BlockSpec(a.shape, lambda i: (0,) * a.ndim)
    mix_tile = lambda i: jnp.minimum(i, n_tiles - 1)
    route_tile = lambda i: jnp.maximum(i - 1, 0)
    rows = lambda w: pl.BlockSpec((tm, w), lambda i: (mix_tile(i), 0))
    outs = [jax.ShapeDtypeStruct((t, D_MODEL), F32),
            jax.ShapeDtypeStruct((t, PACKED), jnp.uint32),
            jax.ShapeDtypeStruct((8, t), jnp.int32),
            jax.ShapeDtypeStruct((t, LANES), F32),
            jax.ShapeDtypeStruct((8, t), jnp.int32),
            jax.ShapeDtypeStruct((8, LANES), jnp.int32)]
    slots = pl.BlockSpec((8, tm), lambda i: (0, route_tile(i)))
    return pl.pallas_call(
        _post_kernel,
        out_shape=outs,
        grid=(n_tiles + 1,),
        in_specs=[
            rows(D_MODEL), rows(D_MODEL),
            pl.BlockSpec((None, 8, D_MODEL), lambda i: (mix_tile(i) // tiles_per_seq, 0, 0)),
            full(wo), full(norm_g), full(rw_hl), full(rb_pad),
        ],
        out_specs=[rows(D_MODEL), rows(PACKED), slots,
                   pl.BlockSpec((tm, LANES), lambda i: (route_tile(i), 0)), slots,
                   pl.BlockSpec((8, LANES), lambda i: (0, 0))],
        scratch_shapes=[pltpu.VMEM((8, LANES), F32), pltpu.VMEM((2, tm, LANES), F32),
                        pltpu.VMEM(wo.shape, BF16)],
        compiler_params=pltpu.CompilerParams(
            dimension_semantics=("arbitrary",), vmem_limit_bytes=VMEM_LIMIT),
        name="post_mixer",
    )(merged, x2, mod, wo, norm_g, rw_hl, rb_pad)


_GM_COUNT, _GM_FIRST, _GM_BLOCKS = range(3)
_ST_EXPERT, _ST_SLOT = range(2)


def _moe_kernel(gm_ref, xs_ref, wgu_hbm, bgu_ref, wd_hbm, bd_ref, o_ref,
                wgu_f, wd_f, wgu_bf, wd_bf, sem, st_ref):
    weights = (wgu_hbm, wd_hbm, wgu_f, wd_f, wgu_bf, wd_bf, sem)
    rows = (xs_ref, bgu_ref, bd_ref, o_ref, wgu_bf, wd_bf)
    b0 = pl.program_id(0) * MOE_STEP_BLOCKS
    e, in_group, used = _moe_enter(b0, gm_ref, st_ref, *weights)
    together = jnp.logical_and(used, in_group + MOE_STEP_BLOCKS <= gm_ref[_GM_BLOCKS, e])

    @pl.when(together)
    def _():
        _moe_rows(slice(0, MOE_STEP_BLOCKS * MOE_BLOCK), e, in_group, True, gm_ref, *rows)

    @pl.when(jnp.logical_not(together))
    def _():
        for r in range(0, MOE_STEP_BLOCKS, 2):
            if r == 0:
                e_r, in_group_r, used_r = e, in_group, used
            else:
                e_r, in_group_r, used_r = _moe_enter(b0 + r, gm_ref, st_ref, *weights)
            paired = jnp.logical_and(used_r, in_group_r + 2 <= gm_ref[_GM_BLOCKS, e_r])

            @pl.when(paired)
            def _():
                _moe_rows(slice(r * MOE_BLOCK, (r + 2) * MOE_BLOCK), e_r, in_group_r, True,
                          gm_ref, *rows)

            @pl.when(jnp.logical_not(paired))
            def _():
                _moe_rows(slice(r * MOE_BLOCK, (r + 1) * MOE_BLOCK), e_r, in_group_r, used_r,
                          gm_ref, *rows)
                e_n, in_group_n, used_n = _moe_enter(b0 + r + 1, gm_ref, st_ref, *weights)
                _moe_rows(slice((r + 1) * MOE_BLOCK, (r + 2) * MOE_BLOCK), e_n, in_group_n,
                          used_n, gm_ref, *rows)


def _moe_enter(b, gm_ref, st_ref, wgu_hbm, wd_hbm, wgu_f, wd_f, wgu_bf, wd_bf, sem):
    def weight_copies(expert, sl):
        return (pltpu.make_async_copy(wgu_hbm.at[expert], wgu_f.at[sl], sem.at[0, sl]),
                pltpu.make_async_copy(wd_hbm.at[expert], wd_f.at[sl], sem.at[1, sl]))

    def next_group(e):
        return lax.while_loop(
            lambda k: jnp.logical_and(k < N_EXPERTS,
                                      gm_ref[_GM_BLOCKS, jnp.minimum(k, N_EXPERTS - 1)] == 0),
            lambda k: k + 1, e)

    @pl.when(b == 0)
    def _():
        e0 = next_group(0)
        st_ref[_ST_EXPERT] = e0
        st_ref[_ST_SLOT] = 1
        for cp in weight_copies(e0, 0):
            cp.start()

    e_prev = st_ref[_ST_EXPERT]
    past = b >= gm_ref[_GM_FIRST, e_prev] + gm_ref[_GM_BLOCKS, e_prev]
    e = jnp.minimum(jnp.where(past, next_group(e_prev + 1), e_prev), N_EXPERTS - 1)
    st_ref[_ST_EXPERT] = e
    in_group = b - gm_ref[_GM_FIRST, e]
    used = jnp.logical_and(in_group >= 0, in_group < gm_ref[_GM_BLOCKS, e])

    @pl.when(jnp.logical_and(used, in_group == 0))
    def _():
        slot = 1 - st_ref[_ST_SLOT]
        st_ref[_ST_SLOT] = slot
        nxt = next_group(e + 1)

        @pl.when(nxt < N_EXPERTS)
        def _():
            for cp in weight_copies(nxt, 1 - slot):
                cp.start(priority=1)

        for cp in weight_copies(e, slot):
            cp.wait()
        wgu_bf[...] = wgu_f[slot].astype(BF16)
        wd_bf[...] = wd_f[slot].astype(BF16)

    return e, in_group, used


def _moe_rows(rs, e, in_group, used, gm_ref, xs_ref, bgu_ref, bd_ref, o_ref, wgu_bf, wd_bf):
    n_valid = gm_ref[_GM_COUNT, e] - in_group * MOE_BLOCK

    def ffn(r):
        n = r.stop - r.start
        row = lax.broadcasted_iota(jnp.int32, (n, PACKED), 0)
        xs = _unpack_row(jnp.where(row < n_valid, xs_ref[r, :], 0)).astype(BF16)
        gu = _dot(xs, wgu_bf[...]) + bgu_ref[e]
        gate = jnp.minimum(gu[:, :D_EXPERT], SWIGLU_LIMIT)
        up = jnp.clip(gu[:, D_EXPERT:], -SWIGLU_LIMIT, SWIGLU_LIMIT)
        act = (up + 1.0) * (gate * jax.nn.sigmoid(gate * SWIGLU_ALPHA))
        o_ref[r, :] = _pack_row(_dot(act.astype(BF16), wd_bf[...]) + bd_ref[e])

    def zeros(r):
        o_ref[r, :] = jnp.zeros((r.stop - r.start, PACKED), o_ref.dtype)

    if used is True:
        ffn(rs)
        return

    mid = rs.start + (rs.stop - rs.start) // 2
    short = n_valid <= mid - rs.start

    @pl.when(jnp.logical_and(used, jnp.logical_not(short)))
    def _():
        ffn(rs)

    @pl.when(jnp.logical_and(used, short))
    def _():
        ffn(slice(rs.start, mid))
        zeros(slice(mid, rs.stop))

    @pl.when(jnp.logical_not(used))
    def _():
        zeros(rs)


def _moe(group_table, xs, w_gu, b_gu, w_down, b_down):
    n_rows = xs.shape[0]
    step_rows = MOE_STEP_BLOCKS * MOE_BLOCK
    assert n_rows % step_rows == 0
    grid_spec = pltpu.PrefetchScalarGridSpec(
        num_scalar_prefetch=1,
        grid=(n_rows // step_rows,),
        in_specs=[
            pl.BlockSpec((step_rows, PACKED), lambda b, gm: (b, 0)),
            pl.BlockSpec(memory_space=pl.ANY),
            pl.BlockSpec(b_gu.shape, lambda b, gm: (0, 0, 0)),
            pl.BlockSpec(memory_space=pl.ANY),
            pl.BlockSpec(b_down.shape, lambda b, gm: (0, 0, 0)),
        ],
        out_specs=pl.BlockSpec((step_rows, PACKED), lambda b, gm: (b, 0)),
        scratch_shapes=[pltpu.VMEM((2, D_MODEL, 2 * D_EXPERT), F32),
                        pltpu.VMEM((2, D_EXPERT, D_MODEL), F32),
                        pltpu.VMEM((D_MODEL, 2 * D_EXPERT), BF16),
                        pltpu.VMEM((D_EXPERT, D_MODEL), BF16),
                        pltpu.SemaphoreType.DMA((2, 2)),
                        pltpu.SMEM((2,), jnp.int32)],
    )
    return pl.pallas_call(
        _moe_kernel,
        out_shape=jax.ShapeDtypeStruct((n_rows, PACKED), jnp.uint32),
        grid_spec=grid_spec,
        compiler_params=pltpu.CompilerParams(
            dimension_semantics=("arbitrary",), vmem_limit_bytes=VMEM_LIMIT),
        name="moe_experts",
    )(group_table, xs, w_gu, b_gu, w_down, b_down)


def _final_kernel(last_layer, x1_ref, y_ref, gate_ref, mod_ref, g_ref, o_ref):
    mod = mod_ref[...]
    gate = gate_ref[...]
    ffn = gate[:, 0:1] * _unpack_row(y_ref[0])
    for kk in range(1, TOP_K):
        ffn = ffn + gate[:, kk:kk + 1] * _unpack_row(y_ref[kk])
    x = x1_ref[...] + mod[5:6] * ffn
    o_ref[...] = _rms(x, g_ref[...]) if last_layer else x


def _final(x1, y_kt, gate, mod, norm_g, seq, last_layer):
    t = x1.shape[0]
    tm = ROW_TILE
    tiles_per_seq = seq // tm
    rows = lambda w: pl.BlockSpec((tm, w), lambda i: (i, 0))
    return pl.pallas_call(
        functools.partial(_final_kernel, last_layer),
        out_shape=jax.ShapeDtypeStruct((t, D_MODEL), F32),
        grid=(t // tm,),
        in_specs=[
            rows(D_MODEL), pl.BlockSpec((TOP_K, tm, PACKED), lambda i: (0, i, 0)), rows(LANES),
            pl.BlockSpec((None, 8, D_MODEL), lambda i: (i // tiles_per_seq, 0, 0)),
            pl.BlockSpec(norm_g.shape, lambda i: (0, 0)),
        ],
        out_specs=rows(D_MODEL),
        compiler_params=pltpu.CompilerParams(
            dimension_semantics=("arbitrary",), vmem_limit_bytes=VMEM_LIMIT),
        name="combine_final",
    )(x1, y_kt, gate, mod, norm_g)


def _prep_weights(w_in, w_uq, w_ukv):
    d = w_in.shape[0]
    splits = (Q_LORA, KV_LORA, QK_ROPE, CONV_WIDTH, CONV_WIDTH, CONV_WIDTH, D_MODEL, D_MODEL)
    offs = [0]
    for s in splits:
        offs.append(offs[-1] + s)
    part = lambda n: w_in[:, offs[n]:offs[n + 1]]
    z = lambda n: jnp.zeros((d, n), w_in.dtype)
    w_kpe = part(2)
    kpe_slab = jnp.concatenate([z(QK_NOPE), w_kpe, z(HEAD_PAD - QK_HEAD)], axis=1)
    w_lat = jnp.concatenate([part(0), part(1), kpe_slab], axis=1).astype(BF16)
    w_conv = w_in[:, offs[3]:offs[6]].astype(BF16)
    w_gate = w_in[:, offs[6]:offs[8]].astype(BF16)

    wq = w_uq.reshape(Q_LORA, N_HEADS, QK_HEAD)
    zq = lambda n: jnp.zeros((Q_LORA, N_HEADS, n), w_uq.dtype)
    wq2 = jnp.concatenate([wq, zq(HEAD_PAD - QK_HEAD)], axis=-1)
    wq2 = wq2.reshape(Q_LORA, N_HEADS * HEAD_PAD).astype(BF16)

    wkv = w_ukv.reshape(KV_LORA, N_HEADS, QK_NOPE + V_HEAD)
    wk2 = jnp.concatenate([wkv[..., :QK_NOPE],
                           jnp.zeros((KV_LORA, N_HEADS, HEAD_PAD - QK_NOPE), w_ukv.dtype)], axis=-1)
    wk2 = wk2.reshape(KV_LORA, N_HEADS * HEAD_PAD).astype(BF16)
    wv = jnp.concatenate([wkv[..., QK_NOPE:],
                          jnp.zeros((KV_LORA, N_HEADS, HEAD_PAD - V_HEAD), w_ukv.dtype)], axis=-1)
    wv = wv.reshape(KV_LORA, N_HEADS * HEAD_PAD).astype(BF16)
    return w_lat, w_conv, w_gate, wq2, wk2, wv


def _rope_freqs():
    inv_freq = 1.0 / (ROPE_THETA ** (jnp.arange(0, QK_ROPE, 2, dtype=F32) / QK_ROPE))
    return inv_freq.reshape(QK_ROPE // 2, 1)


def _dest_kernel(gm_ref, idx_ref, rank_ref, o_ref):
    idx = idx_ref[...]
    dest = rank_ref[...]
    for e in range(N_EXPERTS):
        dest = dest + jnp.where(idx == e, gm_ref[_GM_FIRST, e] * MOE_BLOCK, 0)
    o_ref[...] = dest


def _route(top_idx, rank, counts, n_tokens):
    blocks = (counts + MOE_BLOCK - 1) // MOE_BLOCK
    first_block = jnp.cumsum(blocks) - blocks
    table = jnp.stack([counts, first_block, blocks]).astype(jnp.int32)
    whole = pl.BlockSpec(top_idx.shape, lambda i, gm: (0, 0))
    dest = pl.pallas_call(
        _dest_kernel,
        out_shape=jax.ShapeDtypeStruct(rank.shape, jnp.int32),
        grid_spec=pltpu.PrefetchScalarGridSpec(
            num_scalar_prefetch=1, grid=(1,), in_specs=[whole, whole], out_specs=whole),
        name="row_destinations",
    )(table, top_idx, rank)[:TOP_K]
    n_rows = n_tokens * TOP_K + N_EXPERTS * MOE_BLOCK
    return dest, table, n_rows


SC_CORES = 2
SC_SUBCORES = 16
SC_WORKERS = SC_CORES * SC_SUBCORES
SC_CHUNK = 64
SC_GATHER_RING = 3

def _sc_mesh():
    return plsc.VectorSubcoreMesh(core_axis_name="c", subcore_axis_name="s")


def _sc_worker():
    return lax.axis_index("s") * SC_CORES + lax.axis_index("c")


def _dispatch(h2, dest, n_rows):
    t, d = h2.shape
    per_w = t // SC_WORKERS
    n_chunks = per_w // SC_CHUNK
    assert per_w % (2 * SC_CHUNK) == 0
    idx = dest.reshape(TOP_K, SC_WORKERS, n_chunks, SC_CHUNK).transpose(1, 0, 2, 3)
    idx = idx.reshape(SC_WORKERS, TOP_K * n_chunks, SC_CHUNK)

    @functools.partial(
        pl.kernel, mesh=_sc_mesh(),
        out_type=jax.ShapeDtypeStruct((n_rows, d), h2.dtype),
        scratch_types=[pltpu.VMEM((TOP_K * n_chunks, SC_CHUNK), jnp.int32),
                       pltpu.VMEM((2, SC_CHUNK, d), h2.dtype),
                       pltpu.SemaphoreType.DMA((2,)),
                       pltpu.SemaphoreType.DMA((2,))],
        name="moe_dispatch")
    def run(h2_hbm, idx_hbm, xs_hbm, idx_v, rows_v, rsem, ssem):
        w = _sc_worker()
        pltpu.sync_copy(idx_hbm.at[w], idx_v)

        def read(g, b):
            src = h2_hbm.at[pl.ds(w * per_w + g * SC_CHUNK, SC_CHUNK)]
            return pltpu.make_async_copy(src, rows_v.at[b], rsem.at[b])

        def scatter(g, kk, b):
            dst = xs_hbm.at[idx_v.at[kk * n_chunks + g]]
            return pltpu.make_async_copy(rows_v.at[b], dst, ssem.at[b])

        read(0, 0).start()

        @pl.loop(0, n_chunks, step=2)
        def _(g0):
            for b in range(2):
                g = g0 + b
                read(g, b).wait()

                @pl.when(g + 1 < n_chunks)
                def _():
                    read(g + 1, 1 - b).start()

                for kk in range(TOP_K):
                    scatter(g, kk, b).start()
                for kk in range(TOP_K):
                    scatter(g, kk, b).wait()

    return run(h2, idx)


def _undispatch(ys, dest):
    t = dest.shape[1]
    d = ys.shape[1]
    n_out = t * TOP_K
    per_w = n_out // SC_WORKERS
    n_chunks = per_w // SC_CHUNK
    idx = dest.reshape(SC_WORKERS, n_chunks, SC_CHUNK)
    ring = SC_GATHER_RING

    @functools.partial(
        pl.kernel, mesh=_sc_mesh(),
        out_type=jax.ShapeDtypeStruct((n_out, d), ys.dtype),
        scratch_types=[pltpu.VMEM((n_chunks, SC_CHUNK), jnp.int32),
                       pltpu.VMEM((ring, SC_CHUNK, d), ys.dtype),
                       pltpu.SemaphoreType.DMA((ring,)),
                       pltpu.SemaphoreType.DMA((ring,))],
        name="moe_undispatch")
    def run(ys_hbm, idx_hbm, out_hbm, idx_v, rows_v, gsem, wsem):
        w = _sc_worker()
        pltpu.sync_copy(idx_hbm.at[w], idx_v)

        def gather(g):
            b = g % ring
            return pltpu.make_async_copy(ys_hbm.at[idx_v.at[g]], rows_v.at[b], gsem.at[b])

        def write(g):
            b = g % ring
            dst = out_hbm.at[pl.ds(w * per_w + g * SC_CHUNK, SC_CHUNK)]
            return pltpu.make_async_copy(rows_v.at[b], dst, wsem.at[b])

        for g in range(min(ring - 1, n_chunks)):
            gather(g).start()
        for g in range(n_chunks):
            gather(g).wait()
            ahead = g + ring - 1
            if ahead < n_chunks:
                if g >= 1:
                    write(g - 1).wait()
                gather(ahead).start()
            write(g).start()
        for g in range(max(n_chunks - ring, 0), n_chunks):
            write(g).wait()

    return run(ys, idx).reshape(TOP_K, t, d)


def kernel(x, c, positions, w_ada, b_ada, norm_mix_g, w_in, q_norm_g, w_uq, kv_norm_g, w_ukv,
           w_up_attn, conv_w, w_up_conv, w_o, norm_ffn_g, router_w, router_b, w_gu, b_gu,
           w_down, b_down, norm_final_g):
    batch, seq, d = x.shape
    t = batch * seq
    depth = w_ada.shape[0]
    assert d == D_MODEL and batch <= 8 and conv_w.shape[1:] == (CONV_K, CONV_WIDTH)
    assert seq % ROW_TILE == 0 and seq % POST_TILE == 0 and seq % ATT_BLOCK == 0
    assert t % (2 * SC_CHUNK * SC_WORKERS) == 0
    x2 = x.reshape(t, d)
    pos = positions.astype(F32).reshape(t // ROW_TILE, 1, ROW_TILE)
    freqs = _rope_freqs()
    c_pad = jnp.zeros((8, d), F32).at[:batch].set(c)

    for l in range(depth):
        ada = _ada(c_pad, w_ada[l], b_ada[l].reshape(1, -1))
        mod = ada[:batch].reshape(batch, 6, d)
        mod = jnp.concatenate([mod, jnp.zeros((batch, 2, d), F32)], axis=1)

        w_lat, w_conv, w_gate, wq2, wk2, wv = _prep_weights(w_in[l], w_uq[l], w_ukv[l])
        q, k, v, sga, gc = _pre(x2, mod, norm_mix_g[l].reshape(1, d), w_lat, w_conv, w_gate,
                                q_norm_g[l].reshape(1, -1), wq2,
                                kv_norm_g[l].reshape(1, -1), wk2, wv, pos, freqs, conv_w[l],
                                w_up_conv[l].astype(BF16), seq)
        merged = _attention(q, k, v, sga, gc, w_up_attn[l], batch, seq)

        rw_pad = jnp.concatenate([router_w[l], jnp.zeros((d, LANES - N_EXPERTS), F32)], axis=1)
        rb_pad = jnp.concatenate([router_b[l], jnp.full((LANES - N_EXPERTS,), NEG_BIG, F32)])
        rw_hi = rw_pad.astype(BF16)
        rw_lo = (rw_pad - rw_hi.astype(F32)).astype(BF16)
        x1, h2, idx_pad, gate_pad, rank_pad, counts = _post(
            merged, x2, mod, w_o[l], norm_ffn_g[l].reshape(1, d),
            jnp.concatenate([rw_hi, rw_lo], axis=1), rb_pad.reshape(1, LANES), seq)

        dest, group_table, n_rows = _route(
            idx_pad, rank_pad, counts[0, :N_EXPERTS], t)
        xs = _dispatch(h2, dest, n_rows)
        ys = _moe(group_table, xs, w_gu[l], b_gu[l].reshape(N_EXPERTS, 1, -1),
                  w_down[l], b_down[l].reshape(N_EXPERTS, 1, -1))
        y_kt = _undispatch(ys, dest)
        x2 = _final(x1, y_kt, gate_pad, mod, norm_final_g.reshape(1, d), seq, l == depth - 1)

    return x2.reshape(batch, seq, d)
```

```python
import functools
import math

import jax
import jax.numpy as jnp
from jax import lax
from jax.experimental import pallas as pl
from jax.experimental.pallas import tpu as pltpu
from jax.experimental.pallas import tpu_sc as plsc

D_MODEL = 1024
CHUNK = 64
N_HEADS = 8
Q_LORA = 256
KV_LORA = 128
QK_NOPE = 64
QK_ROPE = 32
V_HEAD = 64
QK_HEAD = QK_NOPE + QK_ROPE
ROPE_THETA = 10000.0
CONV_WIDTH = 512
CONV_K = 3
N_EXPERTS = 32
TOP_K = 4
D_EXPERT = 1024
SWIGLU_LIMIT = 7.0
SWIGLU_ALPHA = 1.702
MOE_BLOCK = 256
RMS_EPS = 1e-6

LANES = 128
HEAD_PAD = 128
NEG_BIG = -1e30
VMEM_LIMIT = 56 * 1024 * 1024

F32 = jnp.float32
BF16 = jnp.bfloat16

Q_PRESCALE = (QK_HEAD ** -0.5) * math.log2(math.e)

ADA_TILE = 1024
ROW_TILE = 1024
MOE_STEP_BLOCKS = 4
POST_TILE = 1024
POST_SUB = 512
ATT_BLOCK = 512
ATT_WIDE = 2


def _rms(x, g):
    ms = jnp.mean(x * x, axis=-1, keepdims=True)
    return x * lax.rsqrt(ms + RMS_EPS) * g


def _dot(a, b):
    return jnp.dot(a, b, preferred_element_type=F32)


PACKED = D_MODEL // 2


def _pack_row(x):
    return pltpu.pack_elementwise([x[:, :PACKED], x[:, PACKED:]], packed_dtype=BF16)


def _unpack_row(w):
    half = lambda i: pltpu.unpack_elementwise(w, index=i, packed_dtype=BF16, unpacked_dtype=F32)
    return jnp.concatenate([half(0), half(1)], axis=-1)


def _ada_kernel(c_ref, w_ref, b_ref, o_ref):
    c = c_ref[...]
    ca = (c * jax.nn.sigmoid(c)).astype(BF16)
    o_ref[...] = _dot(ca, w_ref[...].astype(BF16)) + b_ref[...]


def _ada(c_pad, w_ada, b_ada):
    n = w_ada.shape[1]
    tn = ADA_TILE
    return pl.pallas_call(
        _ada_kernel,
        out_shape=jax.ShapeDtypeStruct((c_pad.shape[0], n), F32),
        grid=(n // tn,),
        in_specs=[
            pl.BlockSpec(c_pad.shape, lambda j: (0, 0)),
            pl.BlockSpec((D_MODEL, tn), lambda j: (0, j)),
            pl.BlockSpec((1, tn), lambda j: (0, j)),
        ],
        out_specs=pl.BlockSpec((c_pad.shape[0], tn), lambda j: (0, j)),
        compiler_params=pltpu.CompilerParams(
            dimension_semantics=("arbitrary",), vmem_limit_bytes=VMEM_LIMIT),
        name="ada",
    )(c_pad, w_ada, b_ada)


_C_QLAT = 0
_C_KVLAT = _C_QLAT + Q_LORA
_C_KPE = _C_KVLAT + KV_LORA
_C_END = _C_KPE + HEAD_PAD


def _pre_kernel(tiles_per_seq, x_ref, mod_ref, g_ref, wlat_ref, wconv_ref, wgate_ref, qg_ref,
                wq_ref, kvg_ref, wk_ref, wv_ref, pos_ref, freq_ref, cw_ref, wuc_ref,
                q_ref, k_ref, v_ref, sga_ref, gc_ref, carry_ref):
    i = pl.program_id(0)
    tm = x_ref.shape[0]
    mod = mod_ref[...]
    h = _rms(x_ref[...], g_ref[...]) * (1.0 + mod[1:2]) + mod[0:1]
    hb = h.astype(BF16)

    ang = freq_ref[...] * pos_ref[...]
    cos_t, sin_t = jnp.cos(ang), jnp.sin(ang)
    ones_t = jnp.ones((QK_NOPE, tm), F32)
    zeros_t = jnp.zeros((QK_NOPE, tm), F32)
    pad_t = jnp.zeros((HEAD_PAD - QK_HEAD, tm), F32)
    cosf = jnp.concatenate([ones_t, cos_t, cos_t, pad_t], axis=0).T
    sinf = jnp.concatenate([zeros_t, -sin_t, sin_t, pad_t], axis=0).T

    first_half = lax.broadcasted_iota(jnp.int32, (tm, HEAD_PAD), 1) < QK_NOPE + QK_ROPE // 2

    def rope(slab):
        swapped = jnp.where(first_half, pltpu.roll(slab, HEAD_PAD - QK_ROPE // 2, 1),
                            pltpu.roll(slab, QK_ROPE // 2, 1))
        return slab * cosf + swapped * sinf

    small = _dot(hb, wlat_ref[...])
    q_lat = small[:, _C_QLAT:_C_KVLAT]
    kv_lat = small[:, _C_KVLAT:_C_KPE]
    kpe = rope(small[:, _C_KPE:_C_END])
    qn = _rms(q_lat, qg_ref[...]).astype(BF16)
    q = _dot(qn, wq_ref[...])
    q = jnp.concatenate([rope(q[:, hd * HEAD_PAD:(hd + 1) * HEAD_PAD]) for hd in range(N_HEADS)],
                        axis=-1)
    q_ref[...] = (q * Q_PRESCALE).astype(BF16)
    kvn = _rms(kv_lat, kvg_ref[...]).astype(BF16)
    k = _dot(kvn, wk_ref[...]) + jnp.concatenate([kpe] * N_HEADS, axis=-1)
    k_ref[...] = k.astype(BF16)
    lane = lax.broadcasted_iota(jnp.int32, (tm, N_HEADS * HEAD_PAD), 1)
    ones_col = jnp.where(lane % HEAD_PAD == V_HEAD, 1.0, 0.0)
    v_ref[...] = (_dot(kvn, wv_ref[...]) + ones_col).astype(BF16)

    ucb = _dot(hb, wconv_ref[...])
    cu = ucb[:, 0:CONV_WIDTH] * ucb[:, CONV_WIDTH:2 * CONV_WIDTH]
    b_gate = ucb[:, 2 * CONV_WIDTH:3 * CONV_WIDTH]

    @pl.when(i % tiles_per_seq == 0)
    def _():
        carry_ref[...] = jnp.zeros_like(carry_ref)

    prev = carry_ref[...]
    row = lax.broadcasted_iota(jnp.int32, cu.shape, 0)
    cu1 = jnp.where(row == 0, prev[7:8], pltpu.roll(cu, 1, 0))
    cu2 = jnp.where(row == 0, prev[6:7], jnp.where(row == 1, prev[7:8], pltpu.roll(cu, 2, 0)))
    cw = cw_ref[...]
    z = cw[2:3] * cu + cw[1:2] * cu1 + cw[0:1] * cu2
    carry_ref[...] = cu[tm - 8:tm]
    c_branch = _dot((b_gate * z).astype(BF16), wuc_ref[...])

    gates = _dot(hb, wgate_ref[...])
    sga_ref[...] = jax.nn.sigmoid(gates[:, 0:D_MODEL]).astype(BF16)
    gc_ref[...] = (jax.nn.sigmoid(gates[:, D_MODEL:]) * c_branch).astype(BF16)


def _pre(x2, mod, norm_g, w_lat, w_conv, w_gate, q_norm_g, wq2, kv_norm_g, wk2, wv, pos,
         freqs, conv_w, w_up_conv, seq):
    t = x2.shape[0]
    tm = ROW_TILE
    tiles_per_seq = seq // tm
    full = lambda a: pl.BlockSpec(a.shape, lambda i: (0,) * a.ndim)
    rows = lambda w: pl.BlockSpec((tm, w), lambda i: (i, 0))
    outs = [jax.ShapeDtypeStruct((t, N_HEADS * HEAD_PAD), BF16),
            jax.ShapeDtypeStruct((t, N_HEADS * HEAD_PAD), BF16),
            jax.ShapeDtypeStruct((t, N_HEADS * HEAD_PAD), BF16),
            jax.ShapeDtypeStruct((t, D_MODEL), BF16),
            jax.ShapeDtypeStruct((t, D_MODEL), BF16)]
    return pl.pallas_call(
        functools.partial(_pre_kernel, tiles_per_seq),
        out_shape=outs,
        grid=(t // tm,),
        in_specs=[
            rows(D_MODEL),
            pl.BlockSpec((None, 8, D_MODEL), lambda i: (i // tiles_per_seq, 0, 0)),
            full(norm_g), full(w_lat), full(w_conv), full(w_gate), full(q_norm_g), full(wq2),
            full(kv_norm_g), full(wk2), full(wv),
            pl.BlockSpec((None, 1, tm), lambda i: (i, 0, 0)), full(freqs), full(conv_w),
            full(w_up_conv),
        ],
        out_specs=[rows(N_HEADS * HEAD_PAD), rows(N_HEADS * HEAD_PAD), rows(N_HEADS * HEAD_PAD),
                   rows(D_MODEL), rows(D_MODEL)],
        scratch_shapes=[pltpu.VMEM((8, CONV_WIDTH), F32)],
        compiler_params=pltpu.CompilerParams(
            dimension_semantics=("arbitrary",), vmem_limit_bytes=VMEM_LIMIT),
        name="pre_mixer",
    )(x2, mod, norm_g, w_lat, w_conv, w_gate, q_norm_g, wq2, kv_norm_g, wk2, wv, pos,
      freqs, conv_w, w_up_conv)


def _attn_kernel(q_ref, k_ref, v_ref, sga_ref, gc_ref, wua_ref, o_ref, m_ref, acc_ref, wua_bf):
    i = pl.program_id(1)
    tq = q_ref.shape[0]

    @pl.when(jnp.logical_and(pl.program_id(0) == 0, i == 0))
    def _():
        wua_bf[...] = wua_ref[...].astype(BF16)

    def step(k0, tk, masked, first=False):
        if masked:
            rq = (lax.broadcasted_iota(jnp.int32, (tq, tk), 0) + (tk - tq)) // CHUNK
            ck = lax.broadcasted_iota(jnp.int32, (tq, tk), 1) // CHUNK
            allowed = ck <= rq
        for hd in range(N_HEADS):
            hs = slice(hd * HEAD_PAD, (hd + 1) * HEAD_PAD)
            s = lax.dot_general(q_ref[:, hs], k_ref[pl.ds(k0, tk), hs],
                                (((1,), (1,)), ((), ())), preferred_element_type=F32)
            if masked:
                s = jnp.where(allowed, s, NEG_BIG)
            s_max = s[:, 0:LANES]
            for c in range(1, tk // LANES):
                s_max = jnp.maximum(s_max, s[:, c * LANES:(c + 1) * LANES])
            m_new = jnp.broadcast_to(jnp.max(s_max, axis=-1, keepdims=True), (tq, LANES))
            if not first:
                m_old = m_ref[hd]
                m_new = jnp.maximum(m_old, m_new)
            p = jnp.concatenate(
                [jnp.exp2(s[:, c * LANES:(c + 1) * LANES] - m_new).astype(BF16)
                 for c in range(tk // LANES)], axis=-1)
            pv = _dot(p, v_ref[pl.ds(k0, tk), hs])
            acc_ref[hd] = pv if first else jnp.exp2(m_old - m_new) * acc_ref[hd] + pv
            m_ref[hd] = m_new

    wide = ATT_WIDE * tq
    leftover = i % ATT_WIDE
    borrow = jnp.logical_and(leftover < ATT_WIDE - 1, i >= ATT_WIDE)
    n_wide = i // ATT_WIDE - borrow.astype(jnp.int32)
    tail_blocks = leftover + jnp.where(borrow, ATT_WIDE, 0)

    @pl.when(n_wide > 0)
    def _():
        step(0, wide, False, first=True)

    @pl.when(n_wide == 0)
    def _():
        m_ref[...] = jnp.full_like(m_ref, NEG_BIG)
        acc_ref[...] = jnp.zeros_like(acc_ref)

    def body(j, carry):
        step(pl.multiple_of(j * wide, wide), wide, False)
        return carry

    lax.fori_loop(1, n_wide, body, 0)

    for r in range(2 * ATT_WIDE - 1):
        @pl.when(tail_blocks == r)
        def _():
            step(pl.multiple_of((i - r) * tq, tq), (r + 1) * tq, True)

    heads = []
    for hd in range(N_HEADS):
        acc = acc_ref[hd]
        heads.append((acc[:, 0:V_HEAD] / acc[:, V_HEAD:V_HEAD + 1]).astype(BF16))
    a_branch = _dot(jnp.concatenate(heads, axis=-1), wua_bf[...])
    o_ref[...] = (sga_ref[...].astype(F32) * a_branch + gc_ref[...].astype(F32)).astype(BF16)


def _attention(q, k, v, sga, gc, wua, batch, seq):
    tq = ATT_BLOCK
    nq = seq // tq
    q_rows = lambda w: pl.BlockSpec((tq, w), lambda b, i: (b * nq + i, 0))
    whole_seq = pl.BlockSpec((seq, N_HEADS * HEAD_PAD), lambda b, i: (b, 0))
    return pl.pallas_call(
        _attn_kernel,
        out_shape=jax.ShapeDtypeStruct((batch * seq, D_MODEL), BF16),
        grid=(batch, nq),
        in_specs=[q_rows(N_HEADS * HEAD_PAD), whole_seq, whole_seq, q_rows(D_MODEL),
                  q_rows(D_MODEL), pl.BlockSpec(wua.shape, lambda b, i: (0, 0))],
        out_specs=q_rows(D_MODEL),
        scratch_shapes=[pltpu.VMEM((N_HEADS, tq, LANES), F32),
                        pltpu.VMEM((N_HEADS, tq, LANES), F32),
                        pltpu.VMEM(wua.shape, BF16)],
        compiler_params=pltpu.CompilerParams(
            dimension_semantics=("arbitrary", "arbitrary"), vmem_limit_bytes=VMEM_LIMIT),
        name="attention",
    )(q, k, v, sga, gc, wua)


def _post_kernel(merged_ref, x_ref, mod_ref, wo_ref, g_ref, rw_ref, rb_ref,
                 x1_ref, h2_ref, idx_ref, gate_ref, rank_ref, cnt_out_ref, cnt_ref, lg_ref,
                 wo_bf):
    i = pl.program_id(0)

    @pl.when(i == 0)
    def _():
        cnt_ref[...] = jnp.zeros_like(cnt_ref)
        lg_ref[...] = jnp.zeros_like(lg_ref)
        wo_bf[...] = wo_ref[...].astype(BF16)

    prev_logits = lg_ref[(i + 1) % 2]
    counts = cnt_ref[...]
    routed = counts
    for r0 in range(0, x_ref.shape[0], POST_SUB):
        rs = slice(r0, r0 + POST_SUB)
        lg_ref[i % 2, rs, :] = _post_mix(rs, merged_ref, x_ref, mod_ref, wo_bf, g_ref, rw_ref,
                                         rb_ref, x1_ref, h2_ref)
        routed = _post_route(rs, prev_logits[rs, :], routed, idx_ref, gate_ref, rank_ref)
    counts = jnp.where(i > 0, routed, counts)
    cnt_ref[...] = counts
    cnt_out_ref[...] = counts.astype(jnp.int32)


def _post_mix(rs, merged_ref, x_ref, mod_ref, wo_ref, g_ref, rw_ref, rb_ref, x1_ref, h2_ref):
    mod = mod_ref[...]
    mix = _dot(merged_ref[rs, :], wo_ref[...])
    x1 = x_ref[rs, :] + mod[2:3] * mix
    x1_ref[rs, :] = x1
    h2 = _rms(x1, g_ref[...]) * (1.0 + mod[4:5]) + mod[3:4]
    h2_ref[rs, :] = _pack_row(h2)

    h_hi = h2.astype(BF16)
    h_lo = (h2 - h_hi.astype(F32)).astype(BF16)
    both = _dot(h_hi, rw_ref[...])
    return both[:, :LANES] + both[:, LANES:] + _dot(h_lo, rw_ref[:, :LANES]) + rb_ref[...]


def _post_route(rs, logits, counts, idx_ref, gate_ref, rank_ref):
    lane = lax.broadcasted_iota(jnp.int32, logits.shape, 1)
    work = logits
    vals, idxs = [], []
    for _ in range(TOP_K):
        mk = jnp.max(work, axis=-1, keepdims=True)
        ik = jnp.min(jnp.where(work == mk, lane, LANES), axis=-1, keepdims=True)
        vals.append(mk)
        idxs.append(ik)
        work = jnp.where(lane == ik, -jnp.inf, work)
    es = [jnp.exp(vk - vals[0]) for vk in vals]
    denom = es[0] + es[1] + es[2] + es[3]
    tm = logits.shape[0]
    chosen = jnp.zeros(logits.shape, F32)
    for kk in range(TOP_K):
        chosen = chosen + jnp.where(lane == idxs[kk], 1.0, 0.0)
    r_i = lax.broadcasted_iota(jnp.int32, (tm, tm), 0)
    c_i = lax.broadcasted_iota(jnp.int32, (tm, tm), 1)
    earlier = jnp.where(c_i < r_i, 1.0, 0.0).astype(BF16)
    before = _dot(earlier, chosen.astype(BF16)) + counts[0:1]

    idx_out = jnp.zeros(logits.shape, F32)
    gate_out = jnp.zeros(logits.shape, F32)
    rank_out = jnp.zeros(logits.shape, F32)
    for kk in range(TOP_K):
        rank_k = jnp.sum(jnp.where(lane == idxs[kk], before, 0.0), axis=-1, keepdims=True)
        idx_out = jnp.where(lane == kk, idxs[kk].astype(F32), idx_out)
        gate_out = jnp.where(lane == kk, es[kk] / denom, gate_out)
        rank_out = jnp.where(lane == kk, rank_k, rank_out)
    gate_ref[rs, :] = gate_out
    idx_ref[:, rs] = idx_out.T[0:8].astype(jnp.int32)
    rank_ref[:, rs] = rank_out.T[0:8].astype(jnp.int32)
    return counts + jnp.sum(chosen, axis=0, keepdims=True)


def _post(merged, x2, mod, wo, norm_g, rw_hl, rb_pad, seq):
    t = x2.shape[0]
    tm = POST_TILE
    tiles_per_seq = seq // tm
    n_tiles = t // tm
    full = lambda a: pl.BlockSpec(a.shape, lambda i: (0,) * a.ndim)
    mix_tile = lambda i: jnp.minimum(i, n_tiles - 1)
    route_tile = lambda i: jnp.maximum(i - 1, 0)
    rows = lambda w: pl.BlockSpec((tm, w), lambda i: (mix_tile(i), 0))
    outs = [jax.ShapeDtypeStruct((t, D_MODEL), F32),
            jax.ShapeDtypeStruct((t, PACKED), jnp.uint32),
            jax.ShapeDtypeStruct((8, t), jnp.int32),
            jax.ShapeDtypeStruct((t, LANES), F32),
            jax.ShapeDtypeStruct((8, t), jnp.int32),
            jax.ShapeDtypeStruct((8, LANES), jnp.int32)]
    slots = pl.BlockSpec((8, tm), lambda i: (0, route_tile(i)))
    return pl.pallas_call(
        _post_kernel,
        out_shape=outs,
        grid=(n_tiles + 1,),
        in_specs=[
            rows(D_MODEL), rows(D_MODEL),
            pl.BlockSpec((None, 8, D_MODEL), lambda i: (mix_tile(i) // tiles_per_seq, 0, 0)),
            full(wo), full(norm_g), full(rw_hl), full(rb_pad),
        ],
        out_specs=[rows(D_MODEL), rows(PACKED), slots,
                   pl.BlockSpec((tm, LANES), lambda i: (route_tile(i), 0)), slots,
                   pl.BlockSpec((8, LANES), lambda i: (0, 0))],
        scratch_shapes=[pltpu.VMEM((8, LANES), F32), pltpu.VMEM((2, tm, LANES), F32),
                        pltpu.VMEM(wo.shape, BF16)],
        compiler_params=pltpu.CompilerParams(
            dimension_semantics=("arbitrary",), vmem_limit_bytes=VMEM_LIMIT),
        name="post_mixer",
    )(merged, x2, mod, wo, norm_g, rw_hl, rb_pad)


_GM_COUNT, _GM_FIRST, _GM_BLOCKS = range(3)
_ST_EXPERT, _ST_SLOT = range(2)


def _moe_kernel(gm_ref, xs_ref, wgu_hbm, bgu_ref, wd_hbm, bd_ref, o_ref,
                wgu_f, wd_f, wgu_bf, wd_bf, sem, st_ref):
    weights = (wgu_hbm, wd_hbm, wgu_f, wd_f, wgu_bf, wd_bf, sem)
    rows = (xs_ref, bgu_ref, bd_ref, o_ref, wgu_bf, wd_bf)
    b0 = pl.program_id(0) * MOE_STEP_BLOCKS
    e, in_group, used = _moe_enter(b0, gm_ref, st_ref, *weights)
    together = jnp.logical_and(used, in_group + MOE_STEP_BLOCKS <= gm_ref[_GM_BLOCKS, e])

    @pl.when(together)
    def _():
        _moe_rows(slice(0, MOE_STEP_BLOCKS * MOE_BLOCK), e, in_group, True, gm_ref, *rows)

    @pl.when(jnp.logical_not(together))
    def _():
        for r in range(0, MOE_STEP_BLOCKS, 2):
            if r == 0:
                e_r, in_group_r, used_r = e, in_group, used
            else:
                e_r, in_group_r, used_r = _moe_enter(b0 + r, gm_ref, st_ref, *weights)
            paired = jnp.logical_and(used_r, in_group_r + 2 <= gm_ref[_GM_BLOCKS, e_r])

            @pl.when(paired)
            def _():
                _moe_rows(slice(r * MOE_BLOCK, (r + 2) * MOE_BLOCK), e_r, in_group_r, True,
                          gm_ref, *rows)

            @pl.when(jnp.logical_not(paired))
            def _():
                _moe_rows(slice(r * MOE_BLOCK, (r + 1) * MOE_BLOCK), e_r, in_group_r, used_r,
                          gm_ref, *rows)
                e_n, in_group_n, used_n = _moe_enter(b0 + r + 1, gm_ref, st_ref, *weights)
                _moe_rows(slice((r + 1) * MOE_BLOCK, (r + 2) * MOE_BLOCK), e_n, in_group_n,
                          used_n, gm_ref, *rows)


def _moe_enter(b, gm_ref, st_ref, wgu_hbm, wd_hbm, wgu_f, wd_f, wgu_bf, wd_bf, sem):
    def weight_copies(expert, sl):
        return (pltpu.make_async_copy(wgu_hbm.at[expert], wgu_f.at[sl], sem.at[0, sl]),
                pltpu.make_async_copy(wd_hbm.at[expert], wd_f.at[sl], sem.at[1, sl]))

    def next_group(e):
        return lax.while_loop(
            lambda k: jnp.logical_and(k < N_EXPERTS,
                                      gm_ref[_GM_BLOCKS, jnp.minimum(k, N_EXPERTS - 1)] == 0),
            lambda k: k + 1, e)

    @pl.when(b == 0)
    def _():
        e0 = next_group(0)
        st_ref[_ST_EXPERT] = e0
        st_ref[_ST_SLOT] = 1
        for cp in weight_copies(e0, 0):
            cp.start()

    e_prev = st_ref[_ST_EXPERT]
    past = b >= gm_ref[_GM_FIRST, e_prev] + gm_ref[_GM_BLOCKS, e_prev]
    e = jnp.minimum(jnp.where(past, next_group(e_prev + 1), e_prev), N_EXPERTS - 1)
    st_ref[_ST_EXPERT] = e
    in_group = b - gm_ref[_GM_FIRST, e]
    used = jnp.logical_and(in_group >= 0, in_group < gm_ref[_GM_BLOCKS, e])

    @pl.when(jnp.logical_and(used, in_group == 0))
    def _():
        slot = 1 - st_ref[_ST_SLOT]
        st_ref[_ST_SLOT] = slot
        nxt = next_group(e + 1)

        @pl.when(nxt < N_EXPERTS)
        def _():
            for cp in weight_copies(nxt, 1 - slot):
                cp.start(priority=1)

        for cp in weight_copies(e, slot):
            cp.wait()
        wgu_bf[...] = wgu_f[slot].astype(BF16)
        wd_bf[...] = wd_f[slot].astype(BF16)

    return e, in_group, used


def _moe_rows(rs, e, in_group, used, gm_ref, xs_ref, bgu_ref, bd_ref, o_ref, wgu_bf, wd_bf):
    n_valid = gm_ref[_GM_COUNT, e] - in_group * MOE_BLOCK

    def ffn(r):
        n = r.stop - r.start
        row = lax.broadcasted_iota(jnp.int32, (n, PACKED), 0)
        xs = _unpack_row(jnp.where(row < n_valid, xs_ref[r, :], 0)).astype(BF16)
        gu = _dot(xs, wgu_bf[...]) + bgu_ref[e]
        gate = jnp.minimum(gu[:, :D_EXPERT], SWIGLU_LIMIT)
        up = jnp.clip(gu[:, D_EXPERT:], -SWIGLU_LIMIT, SWIGLU_LIMIT)
        act = (up + 1.0) * (gate * jax.nn.sigmoid(gate * SWIGLU_ALPHA))
        o_ref[r, :] = _pack_row(_dot(act.astype(BF16), wd_bf[...]) + bd_ref[e])

    def zeros(r):
        o_ref[r, :] = jnp.zeros((r.stop - r.start, PACKED), o_ref.dtype)

    if used is True:
        ffn(rs)
        return

    mid = rs.start + (rs.stop - rs.start) // 2
    short = n_valid <= mid - rs.start

    @pl.when(jnp.logical_and(used, jnp.logical_not(short)))
    def _():
        ffn(rs)

    @pl.when(jnp.logical_and(used, short))
    def _():
        ffn(slice(rs.start, mid))
        zeros(slice(mid, rs.stop))

    @pl.when(jnp.logical_not(used))
    def _():
        zeros(rs)


def _moe(group_table, xs, w_gu, b_gu, w_down, b_down):
    n_rows = xs.shape[0]
    step_rows = MOE_STEP_BLOCKS * MOE_BLOCK
    assert n_rows % step_rows == 0
    grid_spec = pltpu.PrefetchScalarGridSpec(
        num_scalar_prefetch=1,
        grid=(n_rows // step_rows,),
        in_specs=[
            pl.BlockSpec((step_rows, PACKED), lambda b, gm: (b, 0)),
            pl.BlockSpec(memory_space=pl.ANY),
            pl.BlockSpec(b_gu.shape, lambda b, gm: (0, 0, 0)),
            pl.BlockSpec(memory_space=pl.ANY),
            pl.BlockSpec(b_down.shape, lambda b, gm: (0, 0, 0)),
        ],
        out_specs=pl.BlockSpec((step_rows, PACKED), lambda b, gm: (b, 0)),
        scratch_shapes=[pltpu.VMEM((2, D_MODEL, 2 * D_EXPERT), F32),
                        pltpu.VMEM((2, D_EXPERT, D_MODEL), F32),
                        pltpu.VMEM((D_MODEL, 2 * D_EXPERT), BF16),
                        pltpu.VMEM((D_EXPERT, D_MODEL), BF16),
                        pltpu.SemaphoreType.DMA((2, 2)),
                        pltpu.SMEM((2,), jnp.int32)],
    )
    return pl.pallas_call(
        _moe_kernel,
        out_shape=jax.ShapeDtypeStruct((n_rows, PACKED), jnp.uint32),
        grid_spec=grid_spec,
        compiler_params=pltpu.CompilerParams(
            dimension_semantics=("arbitrary",), vmem_limit_bytes=VMEM_LIMIT),
        name="moe_experts",
    )(group_table, xs, w_gu, b_gu, w_down, b_down)


def _final_kernel(last_layer, x1_ref, y_ref, gate_ref, mod_ref, g_ref, o_ref):
    mod = mod_ref[...]
    gate = gate_ref[...]
    ffn = gate[:, 0:1] * _unpack_row(y_ref[0])
    for kk in range(1, TOP_K):
        ffn = ffn + gate[:, kk:kk + 1] * _unpack_row(y_ref[kk])
    x = x1_ref[...] + mod[5:6] * ffn
    o_ref[...] = _rms(x, g_ref[...]) if last_layer else x


def _final(x1, y_kt, gate, mod, norm_g, seq, last_layer):
    t = x1.shape[0]
    tm = ROW_TILE
    tiles_per_seq = seq // tm
    rows = lambda w: pl.BlockSpec((tm, w), lambda i: (i, 0))
    return pl.pallas_call(
        functools.partial(_final_kernel, last_layer),
        out_shape=jax.ShapeDtypeStruct((t, D_MODEL), F32),
        grid=(t // tm,),
        in_specs=[
            rows(D_MODEL), pl.BlockSpec((TOP_K, tm, PACKED), lambda i: (0, i, 0)), rows(LANES),
            pl.BlockSpec((None, 8, D_MODEL), lambda i: (i // tiles_per_seq, 0, 0)),
            pl.BlockSpec(norm_g.shape, lambda i: (0, 0)),
        ],
        out_specs=rows(D_MODEL),
        compiler_params=pltpu.CompilerParams(
            dimension_semantics=("arbitrary",), vmem_limit_bytes=VMEM_LIMIT),
        name="combine_final",
    )(x1, y_kt, gate, mod, norm_g)


def _prep_weights(w_in, w_uq, w_ukv):
    d = w_in.shape[0]
    splits = (Q_LORA, KV_LORA, QK_ROPE, CONV_WIDTH, CONV_WIDTH, CONV_WIDTH, D_MODEL, D_MODEL)
    offs = [0]
    for s in splits:
        offs.append(offs[-1] + s)
    part = lambda n: w_in[:, offs[n]:offs[n + 1]]
    z = lambda n: jnp.zeros((d, n), w_in.dtype)
    w_kpe = part(2)
    kpe_slab = jnp.concatenate([z(QK_NOPE), w_kpe, z(HEAD_PAD - QK_HEAD)], axis=1)
    w_lat = jnp.concatenate([part(0), part(1), kpe_slab], axis=1).astype(BF16)
    w_conv = w_in[:, offs[3]:offs[6]].astype(BF16)
    w_gate = w_in[:, offs[6]:offs[8]].astype(BF16)

    wq = w_uq.reshape(Q_LORA, N_HEADS, QK_HEAD)
    zq = lambda n: jnp.zeros((Q_LORA, N_HEADS, n), w_uq.dtype)
    wq2 = jnp.concatenate([wq, zq(HEAD_PAD - QK_HEAD)], axis=-1)
    wq2 = wq2.reshape(Q_LORA, N_HEADS * HEAD_PAD).astype(BF16)

    wkv = w_ukv.reshape(KV_LORA, N_HEADS, QK_NOPE + V_HEAD)
    wk2 = jnp.concatenate([wkv[..., :QK_NOPE],
                           jnp.zeros((KV_LORA, N_HEADS, HEAD_PAD - QK_NOPE), w_ukv.dtype)], axis=-1)
    wk2 = wk2.reshape(KV_LORA, N_HEADS * HEAD_PAD).astype(BF16)
    wv = jnp.concatenate([wkv[..., QK_NOPE:],
                          jnp.zeros((KV_LORA, N_HEADS, HEAD_PAD - V_HEAD), w_ukv.dtype)], axis=-1)
    wv = wv.reshape(KV_LORA, N_HEADS * HEAD_PAD).astype(BF16)
    return w_lat, w_conv, w_gate, wq2, wk2, wv


def _rope_freqs():
    inv_freq = 1.0 / (ROPE_THETA ** (jnp.arange(0, QK_ROPE, 2, dtype=F32) / QK_ROPE))
    return inv_freq.reshape(QK_ROPE // 2, 1)


def _dest_kernel(gm_ref, idx_ref, rank_ref, o_ref):
    idx = idx_ref[...]
    dest = rank_ref[...]
    for e in range(N_EXPERTS):
        dest = dest + jnp.where(idx == e, gm_ref[_GM_FIRST, e] * MOE_BLOCK, 0)
    o_ref[...] = dest


def _route(top_idx, rank, counts, n_tokens):
    blocks = (counts + MOE_BLOCK - 1) // MOE_BLOCK
    first_block = jnp.cumsum(blocks) - blocks
    table = jnp.stack([counts, first_block, blocks]).astype(jnp.int32)
    whole = pl.BlockSpec(top_idx.shape, lambda i, gm: (0, 0))
    dest = pl.pallas_call(
        _dest_kernel,
        out_shape=jax.ShapeDtypeStruct(rank.shape, jnp.int32),
        grid_spec=pltpu.PrefetchScalarGridSpec(
            num_scalar_prefetch=1, grid=(1,), in_specs=[whole, whole], out_specs=whole),
        name="row_destinations",
    )(table, top_idx, rank)[:TOP_K]
    n_rows = n_tokens * TOP_K + N_EXPERTS * MOE_BLOCK
    return dest, table, n_rows


SC_CORES = 2
SC_SUBCORES = 16
SC_WORKERS = SC_CORES * SC_SUBCORES
SC_CHUNK = 64
SC_GATHER_RING = 3

def _sc_mesh():
    return plsc.VectorSubcoreMesh(core_axis_name="c", subcore_axis_name="s")


def _sc_worker():
    return lax.axis_index("s") * SC_CORES + lax.axis_index("c")


def _dispatch(h2, dest, n_rows):
    t, d = h2.shape
    per_w = t // SC_WORKERS
    n_chunks = per_w // SC_CHUNK
    assert per_w % (2 * SC_CHUNK) == 0
    idx = dest.reshape(TOP_K, SC_WORKERS, n_chunks, SC_CHUNK).transpose(1, 0, 2, 3)
    idx = idx.reshape(SC_WORKERS, TOP_K * n_chunks, SC_CHUNK)

    @functools.partial(
        pl.kernel, mesh=_sc_mesh(),
        out_type=jax.ShapeDtypeStruct((n_rows, d), h2.dtype),
        scratch_types=[pltpu.VMEM((TOP_K * n_chunks, SC_CHUNK), jnp.int32),
                       pltpu.VMEM((2, SC_CHUNK, d), h2.dtype),
                       pltpu.SemaphoreType.DMA((2,)),
                       pltpu.SemaphoreType.DMA((2,))],
        name="moe_dispatch")
    def run(h2_hbm, idx_hbm, xs_hbm, idx_v, rows_v, rsem, ssem):
        w = _sc_worker()
        pltpu.sync_copy(idx_hbm.at[w], idx_v)

        def read(g, b):
            src = h2_hbm.at[pl.ds(w * per_w + g * SC_CHUNK, SC_CHUNK)]
            return pltpu.make_async_copy(src, rows_v.at[b], rsem.at[b])

        def scatter(g, kk, b):
            dst = xs_hbm.at[idx_v.at[kk * n_chunks + g]]
            return pltpu.make_async_copy(rows_v.at[b], dst, ssem.at[b])

        read(0, 0).start()

        @pl.loop(0, n_chunks, step=2)
        def _(g0):
            for b in range(2):
                g = g0 + b
                read(g, b).wait()

                @pl.when(g + 1 < n_chunks)
                def _():
                    read(g + 1, 1 - b).start()

                for kk in range(TOP_K):
                    scatter(g, kk, b).start()
                for kk in range(TOP_K):
                    scatter(g, kk, b).wait()

    return run(h2, idx)


def _undispatch(ys, dest):
    t = dest.shape[1]
    d = ys.shape[1]
    n_out = t * TOP_K
    per_w = n_out // SC_WORKERS
    n_chunks = per_w // SC_CHUNK
    idx = dest.reshape(SC_WORKERS, n_chunks, SC_CHUNK)
    ring = SC_GATHER_RING

    @functools.partial(
        pl.kernel, mesh=_sc_mesh(),
        out_type=jax.ShapeDtypeStruct((n_out, d), ys.dtype),
        scratch_types=[pltpu.VMEM((n_chunks, SC_CHUNK), jnp.int32),
                       pltpu.VMEM((ring, SC_CHUNK, d), ys.dtype),
                       pltpu.SemaphoreType.DMA((ring,)),
                       pltpu.SemaphoreType.DMA((ring,))],
        name="moe_undispatch")
    def run(ys_hbm, idx_hbm, out_hbm, idx_v, rows_v, gsem, wsem):
        w = _sc_worker()
        pltpu.sync_copy(idx_hbm.at[w], idx_v)

        def gather(g):
            b = g % ring
            return pltpu.make_async_copy(ys_hbm.at[idx_v.at[g]], rows_v.at[b], gsem.at[b])

        def write(g):
            b = g % ring
            dst = out_hbm.at[pl.ds(w * per_w + g * SC_CHUNK, SC_CHUNK)]
            return pltpu.make_async_copy(rows_v.at[b], dst, wsem.at[b])

        for g in range(min(ring - 1, n_chunks)):
            gather(g).start()
        for g in range(n_chunks):
            gather(g).wait()
            ahead = g + ring - 1
            if ahead < n_chunks:
                if g >= 1:
                    write(g - 1).wait()
                gather(ahead).start()
            write(g).start()
        for g in range(max(n_chunks - ring, 0), n_chunks):
            write(g).wait()

    return run(ys, idx).reshape(TOP_K, t, d)


def kernel(x, c, positions, w_ada, b_ada, norm_mix_g, w_in, q_norm_g, w_uq, kv_norm_g, w_ukv,
           w_up_attn, conv_w, w_up_conv, w_o, norm_ffn_g, router_w, router_b, w_gu, b_gu,
           w_down, b_down, norm_final_g):
    batch, seq, d = x.shape
    t = batch * seq
    depth = w_ada.shape[0]
    assert d == D_MODEL and batch <= 8 and conv_w.shape[1:] == (CONV_K, CONV_WIDTH)
    assert seq % ROW_TILE == 0 and seq % POST_TILE == 0 and seq % ATT_BLOCK == 0
    assert t % (2 * SC_CHUNK * SC_WORKERS) == 0
    x2 = x.reshape(t, d)
    pos = positions.astype(F32).reshape(t // ROW_TILE, 1, ROW_TILE)
    freqs = _rope_freqs()
    c_pad = jnp.zeros((8, d), F32).at[:batch].set(c)

    for l in range(depth):
        ada = _ada(c_pad, w_ada[l], b_ada[l].reshape(1, -1))
        mod = ada[:batch].reshape(batch, 6, d)
        mod = jnp.concatenate([mod, jnp.zeros((batch, 2, d), F32)], axis=1)

        w_lat, w_conv, w_gate, wq2, wk2, wv = _prep_weights(w_in[l], w_uq[l], w_ukv[l])
        q, k, v, sga, gc = _pre(x2, mod, norm_mix_g[l].reshape(1, d), w_lat, w_conv, w_gate,
                                q_norm_g[l].reshape(1, -1), wq2,
                                kv_norm_g[l].reshape(1, -1), wk2, wv, pos, freqs, conv_w[l],
                                w_up_conv[l].astype(BF16), seq)
        merged = _attention(q, k, v, sga, gc, w_up_attn[l], batch, seq)

        rw_pad = jnp.concatenate([router_w[l], jnp.zeros((d, LANES - N_EXPERTS), F32)], axis=1)
        rb_pad = jnp.concatenate([router_b[l], jnp.full((LANES - N_EXPERTS,), NEG_BIG, F32)])
        rw_hi = rw_pad.astype(BF16)
        rw_lo = (rw_pad - rw_hi.astype(F32)).astype(BF16)
        x1, h2, idx_pad, gate_pad, rank_pad, counts = _post(
            merged, x2, mod, w_o[l], norm_ffn_g[l].reshape(1, d),
            jnp.concatenate([rw_hi, rw_lo], axis=1), rb_pad.reshape(1, LANES), seq)

        dest, group_table, n_rows = _route(
            idx_pad, rank_pad, counts[0, :N_EXPERTS], t)
        xs = _dispatch(h2, dest, n_rows)
        ys = _moe(group_table, xs, w_gu[l], b_gu[l].reshape(N_EXPERTS, 1, -1),
                  w_down[l], b_down[l].reshape(N_EXPERTS, 1, -1))
        y_kt = _undispatch(ys, dest)
        x2 = _final(x1, y_kt, gate_pad, mod, norm_final_g.reshape(1, d), seq, l == depth - 1)

    return x2.reshape(batch, seq, d)
```

```python
import functools
import math

import jax
import jax.numpy as jnp
from jax import lax
from jax.experimental import pallas as pl
from jax.experimental.pallas import tpu as pltpu
from jax.experimental.pallas import tpu_sc as plsc

D_MODEL = 1024
CHUNK = 64
N_HEADS = 8
Q_LORA = 256
KV_LORA = 128
QK_NOPE = 64
QK_ROPE = 32
V_HEAD = 64
QK_HEAD = QK_NOPE + QK_ROPE
ROPE_THETA = 10000.0
CONV_WIDTH = 512
CONV_K = 3
N_EXPERTS = 32
TOP_K = 4
D_EXPERT = 1024
SWIGLU_LIMIT = 7.0
SWIGLU_ALPHA = 1.702
MOE_BLOCK = 256
RMS_EPS = 1e-6

LANES = 128
HEAD_PAD = 128
NEG_BIG = -1e30
VMEM_LIMIT = 56 * 1024 * 1024

F32 = jnp.float32
BF16 = jnp.bfloat16

Q_PRESCALE = (QK_HEAD ** -0.5) * math.log2(math.e)

ADA_TILE = 1024
ROW_TILE = 1024
MOE_STEP_BLOCKS = 4
POST_TILE = 1024
POST_SUB = 512
ATT_BLOCK = 512
ATT_WIDE = 2


def _rms(x, g):
    ms = jnp.mean(x * x, axis=-1, keepdims=True)
    return x * lax.rsqrt(ms + RMS_EPS) * g


def _dot(a, b):
    return jnp.dot(a, b, preferred_element_type=F32)


PACKED = D_MODEL // 2


def _pack_row(x):
    return pltpu.pack_elementwise([x[:, :PACKED], x[:, PACKED:]], packed_dtype=BF16)


def _unpack_row(w):
    half = lambda i: pltpu.unpack_elementwise(w, index=i, packed_dtype=BF16, unpacked_dtype=F32)
    return jnp.concatenate([half(0), half(1)], axis=-1)


def _ada_kernel(c_ref, w_ref, b_ref, o_ref):
    c = c_ref[...]
    ca = (c * jax.nn.sigmoid(c)).astype(BF16)
    o_ref[...] = _dot(ca, w_ref[...].astype(BF16)) + b_ref[...]


def _ada(c_pad, w_ada, b_ada):
    n = w_ada.shape[1]
    tn = ADA_TILE
    return pl.pallas_call(
        _ada_kernel,
        out_shape=jax.ShapeDtypeStruct((c_pad.shape[0], n), F32),
        grid=(n // tn,),
        in_specs=[
            pl.BlockSpec(c_pad.shape, lambda j: (0, 0)),
            pl.BlockSpec((D_MODEL, tn), lambda j: (0, j)),
            pl.BlockSpec((1, tn), lambda j: (0, j)),
        ],
        out_specs=pl.BlockSpec((c_pad.shape[0], tn), lambda j: (0, j)),
        compiler_params=pltpu.CompilerParams(
            dimension_semantics=("arbitrary",), vmem_limit_bytes=VMEM_LIMIT),
        name="ada",
    )(c_pad, w_ada, b_ada)


_C_QLAT = 0
_C_KVLAT = _C_QLAT + Q_LORA
_C_KPE = _C_KVLAT + KV_LORA
_C_END = _C_KPE + HEAD_PAD


def _pre_kernel(tiles_per_seq, x_ref, mod_ref, g_ref, wlat_ref, wcg_ref, qg_ref,
                wq_ref, kvg_ref, wk_ref, wv_ref, pos_ref, freq_ref, cw_ref, wuc_ref,
                q_ref, k_ref, v_ref, sga_ref, gc_ref, carry_ref):
    i = pl.program_id(0)
    tm = x_ref.shape[0]
    mod = mod_ref[...]
    h = _rms(x_ref[...], g_ref[...]) * (1.0 + mod[1:2]) + mod[0:1]
    hb = h.astype(BF16)

    ang = freq_ref[...] * pos_ref[...]
    cos_t, sin_t = jnp.cos(ang), jnp.sin(ang)
    ones_t = jnp.ones((QK_NOPE, tm), F32)
    zeros_t = jnp.zeros((QK_NOPE, tm), F32)
    pad_t = jnp.zeros((HEAD_PAD - QK_HEAD, tm), F32)
    cosf = jnp.concatenate([ones_t, cos_t, cos_t, pad_t], axis=0).T
    sinf = jnp.concatenate([zeros_t, -sin_t, sin_t, pad_t], axis=0).T

    first_half = lax.broadcasted_iota(jnp.int32, (tm, HEAD_PAD), 1) < QK_NOPE + QK_ROPE // 2

    def rope(slab):
        swapped = jnp.where(first_half, pltpu.roll(slab, HEAD_PAD - QK_ROPE // 2, 1),
                            pltpu.roll(slab, QK_ROPE // 2, 1))
        return slab * cosf + swapped * sinf

    small = _dot(hb, wlat_ref[...])
    q_lat = small[:, _C_QLAT:_C_KVLAT]
    kv_lat = small[:, _C_KVLAT:_C_KPE]
    kpe = rope(small[:, _C_KPE:_C_END])
    qn = _rms(q_lat, qg_ref[...]).astype(BF16)
    q = _dot(qn, wq_ref[...])
    q = jnp.concatenate([rope(q[:, hd * HEAD_PAD:(hd + 1) * HEAD_PAD]) for hd in range(N_HEADS)],
                        axis=-1)
    q_ref[...] = (q * Q_PRESCALE).astype(BF16)
    kvn = _rms(kv_lat, kvg_ref[...]).astype(BF16)
    k = _dot(kvn, wk_ref[...]) + jnp.concatenate([kpe] * N_HEADS, axis=-1)
    k_ref[...] = k.astype(BF16)
    lane = lax.broadcasted_iota(jnp.int32, (tm, N_HEADS * HEAD_PAD), 1)
    ones_col = jnp.where(lane % HEAD_PAD == V_HEAD, 1.0, 0.0)
    v_ref[...] = (_dot(kvn, wv_ref[...]) + ones_col).astype(BF16)

    ucb = _dot(hb, wcg_ref[:, 0:3 * CONV_WIDTH])
    cu = ucb[:, 0:CONV_WIDTH] * ucb[:, CONV_WIDTH:2 * CONV_WIDTH]
    b_gate = ucb[:, 2 * CONV_WIDTH:3 * CONV_WIDTH]

    @pl.when(i % tiles_per_seq == 0)
    def _():
        carry_ref[...] = jnp.zeros_like(carry_ref)

    prev = carry_ref[...]
    row = lax.broadcasted_iota(jnp.int32, cu.shape, 0)
    cu1 = jnp.where(row == 0, prev[7:8], pltpu.roll(cu, 1, 0))
    cu2 = jnp.where(row == 0, prev[6:7], jnp.where(row == 1, prev[7:8], pltpu.roll(cu, 2, 0)))
    cw = cw_ref[...]
    z = cw[2:3] * cu + cw[1:2] * cu1 + cw[0:1] * cu2
    carry_ref[...] = cu[tm - 8:tm]
    c_branch = _dot((b_gate * z).astype(BF16), wuc_ref[...])

    gates = _dot(hb, wcg_ref[:, 3 * CONV_WIDTH:])
    sga_ref[...] = jax.nn.sigmoid(gates[:, 0:D_MODEL]).astype(BF16)
    gc_ref[...] = (jax.nn.sigmoid(gates[:, D_MODEL:]) * c_branch).astype(BF16)


def _pre(x2, mod, norm_g, w_lat, w_cg, q_norm_g, wq2, kv_norm_g, wk2, wv, pos,
         freqs, conv_w, w_up_conv, seq):
    t = x2.shape[0]
    tm = ROW_TILE
    tiles_per_seq = seq // tm
    full = lambda a: pl.BlockSpec(a.shape, lambda i: (0,) * a.ndim)
    rows = lambda w: pl.BlockSpec((tm, w), lambda i: (i, 0))
    outs = [jax.ShapeDtypeStruct((t, N_HEADS * HEAD_PAD), BF16),
            jax.ShapeDtypeStruct((t, N_HEADS * HEAD_PAD), BF16),
            jax.ShapeDtypeStruct((t, N_HEADS * HEAD_PAD), BF16),
            jax.ShapeDtypeStruct((t, D_MODEL), BF16),
            jax.ShapeDtypeStruct((t, D_MODEL), BF16)]
    return pl.pallas_call(
        functools.partial(_pre_kernel, tiles_per_seq),
        out_shape=outs,
        grid=(t // tm,),
        in_specs=[
            rows(D_MODEL),
            pl.BlockSpec((None, 8, D_MODEL), lambda i: (i // tiles_per_seq, 0, 0)),
            full(norm_g), full(w_lat), full(w_cg), full(q_norm_g), full(wq2),
            full(kv_norm_g), full(wk2), full(wv),
            pl.BlockSpec((None, 1, tm), lambda i: (i, 0, 0)), full(freqs), full(conv_w),
            full(w_up_conv),
        ],
        out_specs=[rows(N_HEADS * HEAD_PAD), rows(N_HEADS * HEAD_PAD), rows(N_HEADS * HEAD_PAD),
                   rows(D_MODEL), rows(D_MODEL)],
        scratch_shapes=[pltpu.VMEM((8, CONV_WIDTH), F32)],
        compiler_params=pltpu.CompilerParams(
            dimension_semantics=("arbitrary",), vmem_limit_bytes=VMEM_LIMIT),
        name="pre_mixer",
    )(x2, mod, norm_g, w_lat, w_cg, q_norm_g, wq2, kv_norm_g, wk2, wv, pos,
      freqs, conv_w, w_up_conv)


def _attn_kernel(q_ref, k_ref, v_ref, sga_ref, gc_ref, wua_ref, o_ref, m_ref, acc_ref, wua_bf):
    i = pl.program_id(1)
    tq = q_ref.shape[0]

    @pl.when(jnp.logical_and(pl.program_id(0) == 0, i == 0))
    def _():
        wua_bf[...] = wua_ref[...].astype(BF16)

    def step(k0, tk, masked, first=False):
        if masked:
            rq = (lax.broadcasted_iota(jnp.int32, (tq, tk), 0) + (tk - tq)) // CHUNK
            ck = lax.broadcasted_iota(jnp.int32, (tq, tk), 1) // CHUNK
            allowed = ck <= rq
        for hd in range(N_HEADS):
            hs = slice(hd * HEAD_PAD, (hd + 1) * HEAD_PAD)
            s = lax.dot_general(q_ref[:, hs], k_ref[pl.ds(k0, tk), hs],
                                (((1,), (1,)), ((), ())), preferred_element_type=F32)
            if masked:
                s = jnp.where(allowed, s, NEG_BIG)
            s_max = s[:, 0:LANES]
            for c in range(1, tk // LANES):
                s_max = jnp.maximum(s_max, s[:, c * LANES:(c + 1) * LANES])
            m_new = jnp.broadcast_to(jnp.max(s_max, axis=-1, keepdims=True), (tq, LANES))
            if not first:
                m_old = m_ref[hd]
                m_new = jnp.maximum(m_old, m_new)
            p = jnp.concatenate(
                [jnp.exp2(s[:, c * LANES:(c + 1) * LANES] - m_new).astype(BF16)
                 for c in range(tk // LANES)], axis=-1)
            pv = _dot(p, v_ref[pl.ds(k0, tk), hs])
            acc_ref[hd] = pv if first else jnp.exp2(m_old - m_new) * acc_ref[hd] + pv
            m_ref[hd] = m_new

    wide = ATT_WIDE * tq
    n_wide = i // ATT_WIDE

    @pl.when(n_wide > 0)
    def _():
        step(0, wide, False, first=True)

    @pl.when(n_wide == 0)
    def _():
        m_ref[...] = jnp.full_like(m_ref, NEG_BIG)
        acc_ref[...] = jnp.zeros_like(acc_ref)

    def body(j, carry):
        step(pl.multiple_of(j * wide, wide), wide, False)
        return carry

    lax.fori_loop(1, n_wide, body, 0)

    for r in range(ATT_WIDE):
        @pl.when(i % ATT_WIDE == r)
        def _():
            step(pl.multiple_of((i - r) * tq, tq), (r + 1) * tq, True)

    heads = []
    for hd in range(N_HEADS):
        acc = acc_ref[hd]
        heads.append((acc[:, 0:V_HEAD] / acc[:, V_HEAD:V_HEAD + 1]).astype(BF16))
    a_branch = _dot(jnp.concatenate(heads, axis=-1), wua_bf[...])
    o_ref[...] = (sga_ref[...].astype(F32) * a_branch + gc_ref[...].astype(F32)).astype(BF16)


def _attention(q, k, v, sga, gc, wua, batch, seq):
    tq = ATT_BLOCK
    nq = seq // tq
    q_rows = lambda w: pl.BlockSpec((tq, w), lambda b, i: (b * nq + i, 0))
    whole_seq = pl.BlockSpec((seq, N_HEADS * HEAD_PAD), lambda b, i: (b, 0))
    return pl.pallas_call(
        _attn_kernel,
        out_shape=jax.ShapeDtypeStruct((batch * seq, D_MODEL), BF16),
        grid=(batch, nq),
        in_specs=[q_rows(N_HEADS * HEAD_PAD), whole_seq, whole_seq, q_rows(D_MODEL),
                  q_rows(D_MODEL), pl.BlockSpec(wua.shape, lambda b, i: (0, 0))],
        out_specs=q_rows(D_MODEL),
        scratch_shapes=[pltpu.VMEM((N_HEADS, tq, LANES), F32),
                        pltpu.VMEM((N_HEADS, tq, LANES), F32),
                        pltpu.VMEM(wua.shape, BF16)],
        compiler_params=pltpu.CompilerParams(
            dimension_semantics=("arbitrary", "arbitrary"), vmem_limit_bytes=VMEM_LIMIT),
        name="attention",
    )(q, k, v, sga, gc, wua)


def _post_kernel(merged_ref, x_ref, mod_ref, wo_ref, g_ref, rw_ref, rb_ref,
                 x1_ref, h2_ref, idx_ref, gate_ref, rank_ref, cnt_out_ref, cnt_ref, lg_ref,
                 wo_bf):
    i = pl.program_id(0)

    @pl.when(i == 0)
    def _():
        cnt_ref[...] = jnp.zeros_like(cnt_ref)
        lg_ref[...] = jnp.zeros_like(lg_ref)
        wo_bf[...] = wo_ref[...].astype(BF16)

    prev_logits = lg_ref[(i + 1) % 2]
    counts = cnt_ref[...]
    routed = counts
    for r0 in range(0, x_ref.shape[0], POST_SUB):
        rs = slice(r0, r0 + POST_SUB)
        lg_ref[i % 2, rs, :] = _post_mix(rs, merged_ref, x_ref, mod_ref, wo_bf, g_ref, rw_ref,
                                         rb_ref, x1_ref, h2_ref)
        routed = _post_route(rs, prev_logits[rs, :], routed, idx_ref, gate_ref, rank_ref)
    counts = jnp.where(i > 0, routed, counts)
    cnt_ref[...] = counts
    cnt_out_ref[...] = counts.astype(jnp.int32)


def _post_mix(rs, merged_ref, x_ref, mod_ref, wo_ref, g_ref, rw_ref, rb_ref, x1_ref, h2_ref):
    mod = mod_ref[...]
    mix = _dot(merged_ref[rs, :], wo_ref[...])
    x1 = x_ref[rs, :] + mod[2:3] * mix
    x1_ref[rs, :] = x1
    h2 = _rms(x1, g_ref[...]) * (1.0 + mod[4:5]) + mod[3:4]
    h2_ref[rs, :] = _pack_row(h2)

    h_hi = h2.astype(BF16)
    h_lo = (h2 - h_hi.astype(F32)).astype(BF16)
    both = _dot(h_hi, rw_ref[...])
    return both[:, :LANES] + both[:, LANES:] + _dot(h_lo, rw_ref[:, :LANES]) + rb_ref[...]


def _post_route(rs, logits, counts, idx_ref, gate_ref, rank_ref):
    lane = lax.broadcasted_iota(jnp.int32, logits.shape, 1)
    work = logits
    vals, idxs = [], []
    for _ in range(TOP_K):
        mk = jnp.max(work, axis=-1, keepdims=True)
        ik = jnp.min(jnp.where(work == mk, lane, LANES), axis=-1, keepdims=True)
        vals.append(mk)
        idxs.append(ik)
        work = jnp.where(lane == ik, -jnp.inf, work)
    es = [jnp.exp(vk - vals[0]) for vk in vals]
    denom = es[0] + es[1] + es[2] + es[3]
    tm = logits.shape[0]
    chosen = jnp.zeros(logits.shape, F32)
    for kk in range(TOP_K):
        chosen = chosen + jnp.where(lane == idxs[kk], 1.0, 0.0)
    r_i = lax.broadcasted_iota(jnp.int32, (tm, tm), 0)
    c_i = lax.broadcasted_iota(jnp.int32, (tm, tm), 1)
    earlier = jnp.where(c_i < r_i, 1.0, 0.0).astype(BF16)
    before = _dot(earlier, chosen.astype(BF16)) + counts[0:1]

    idx_out = jnp.zeros(logits.shape, F32)
    gate_out = jnp.zeros(logits.shape, F32)
    rank_out = jnp.zeros(logits.shape, F32)
    for kk in range(TOP_K):
        rank_k = jnp.sum(jnp.where(lane == idxs[kk], before, 0.0), axis=-1, keepdims=True)
        idx_out = jnp.where(lane == kk, idxs[kk].astype(F32), idx_out)
        gate_out = jnp.where(lane == kk, es[kk] / denom, gate_out)
        rank_out = jnp.where(lane == kk, rank_k, rank_out)
    gate_ref[rs, :] = gate_out
    idx_ref[:, rs] = idx_out.T[0:8].astype(jnp.int32)
    rank_ref[:, rs] = rank_out.T[0:8].astype(jnp.int32)
    return counts + jnp.sum(chosen, axis=0, keepdims=True)


def _post(merged, x2, mod, wo, norm_g, rw_hl, rb_pad, seq):
    t = x2.shape[0]
    tm = POST_TILE
    tiles_per_seq = seq // tm
    n_tiles = t // tm
    full = lambda a: pl.BlockSpec(a.shape, lambda i: (0,) * a.ndim)
    mix_tile = lambda i: jnp.minimum(i, n_tiles - 1)
    route_tile = lambda i: jnp.maximum(i - 1, 0)
    rows = lambda w: pl.BlockSpec((tm, w), lambda i: (mix_tile(i), 0))
    outs = [jax.ShapeDtypeStruct((t, D_MODEL), F32),
            jax.ShapeDtypeStruct((t, PACKED), jnp.uint32),
            jax.ShapeDtypeStruct((8, t), jnp.int32),
            jax.ShapeDtypeStruct((t, LANES), F32),
            jax.ShapeDtypeStruct((8, t), jnp.int32),
            jax.ShapeDtypeStruct((8, LANES), jnp.int32)]
    slots = pl.BlockSpec((8, tm), lambda i: (0, route_tile(i)))
    return pl.pallas_call(
        _post_kernel,
        out_shape=outs,
        grid=(n_tiles + 1,),
        in_specs=[
            rows(D_MODEL), rows(D_MODEL),
            pl.BlockSpec((None, 8, D_MODEL), lambda i: (mix_tile(i) // tiles_per_seq, 0, 0)),
            full(wo), full(norm_g), full(rw_hl), full(rb_pad),
        ],
        out_specs=[rows(D_MODEL), rows(PACKED), slots,
                   pl.BlockSpec((tm, LANES), lambda i: (route_tile(i), 0)), slots,
                   pl.BlockSpec((8, LANES), lambda i: (0, 0))],
        scratch_shapes=[pltpu.VMEM((8, LANES), F32), pltpu.VMEM((2, tm, LANES), F32),
                        pltpu.VMEM(wo.shape, BF16)],
        compiler_params=pltpu.CompilerParams(
            dimension_semantics=("arbitrary",), vmem_limit_bytes=VMEM_LIMIT),
        name="post_mixer",
    )(merged, x2, mod, wo, norm_g, rw_hl, rb_pad)


_GM_COUNT, _GM_FIRST, _GM_BLOCKS = range(3)
_ST_EXPERT, _ST_SLOT = range(2)


def _moe_kernel(gm_ref, xs_ref, wgu_hbm, bgu_ref, wd_hbm, bd_ref, o_ref,
                wgu_f, wd_f, wgu_bf, wd_bf, sem, st_ref):
    weights = (wgu_hbm, wd_hbm, wgu_f, wd_f, wgu_bf, wd_bf, sem)
    rows = (xs_ref, bgu_ref, bd_ref, o_ref, wgu_bf, wd_bf)
    b0 = pl.program_id(0) * MOE_STEP_BLOCKS
    e, in_group, used = _moe_enter(b0, gm_ref, st_ref, *weights)
    together = jnp.logical_and(used, in_group + MOE_STEP_BLOCKS <= gm_ref[_GM_BLOCKS, e])

    @pl.when(together)
    def _():
        _moe_rows(slice(0, MOE_STEP_BLOCKS * MOE_BLOCK), e, in_group, True, gm_ref, *rows)

    @pl.when(jnp.logical_not(together))
    def _():
        for r in range(0, MOE_STEP_BLOCKS, 2):
            if r == 0:
                e_r, in_group_r, used_r = e, in_group, used
            else:
                e_r, in_group_r, used_r = _moe_enter(b0 + r, gm_ref, st_ref, *weights)
            paired = jnp.logical_and(used_r, in_group_r + 2 <= gm_ref[_GM_BLOCKS, e_r])

            @pl.when(paired)
            def _():
                _moe_rows(slice(r * MOE_BLOCK, (r + 2) * MOE_BLOCK), e_r, in_group_r, True,
                          gm_ref, *rows)

            @pl.when(jnp.logical_not(paired))
            def _():
                _moe_rows(slice(r * MOE_BLOCK, (r + 1) * MOE_BLOCK), e_r, in_group_r, used_r,
                          gm_ref, *rows)
                e_n, in_group_n, used_n = _moe_enter(b0 + r + 1, gm_ref, st_ref, *weights)
                _moe_rows(slice((r + 1) * MOE_BLOCK, (r + 2) * MOE_BLOCK), e_n, in_group_n,
                          used_n, gm_ref, *rows)


def _moe_enter(b, gm_ref, st_ref, wgu_hbm, wd_hbm, wgu_f, wd_f, wgu_bf, wd_bf, sem):
    def weight_copies(expert, sl):
        return (pltpu.make_async_copy(wgu_hbm.at[expert], wgu_f.at[sl], sem.at[0, sl]),
                pltpu.make_async_copy(wd_hbm.at[expert], wd_f.at[sl], sem.at[1, sl]))

    def next_group(e):
        return lax.while_loop(
            lambda k: jnp.logical_and(k < N_EXPERTS,
                                      gm_ref[_GM_BLOCKS, jnp.minimum(k, N_EXPERTS - 1)] == 0),
            lambda k: k + 1, e)

    @pl.when(b == 0)
    def _():
        e0 = next_group(0)
        st_ref[_ST_EXPERT] = e0
        st_ref[_ST_SLOT] = 1
        for cp in weight_copies(e0, 0):
            cp.start()

    e_prev = st_ref[_ST_EXPERT]
    past = b >= gm_ref[_GM_FIRST, e_prev] + gm_ref[_GM_BLOCKS, e_prev]
    e = jnp.minimum(jnp.where(past, next_group(e_prev + 1), e_prev), N_EXPERTS - 1)
    st_ref[_ST_EXPERT] = e
    in_group = b - gm_ref[_GM_FIRST, e]
    used = jnp.logical_and(in_group >= 0, in_group < gm_ref[_GM_BLOCKS, e])

    @pl.when(jnp.logical_and(used, in_group == 0))
    def _():
        slot = 1 - st_ref[_ST_SLOT]
        st_ref[_ST_SLOT] = slot
        nxt = next_group(e + 1)

        @pl.when(nxt < N_EXPERTS)
        def _():
            for cp in weight_copies(nxt, 1 - slot):
                cp.start(priority=1)

        for cp in weight_copies(e, slot):
            cp.wait()
        wgu_bf[...] = wgu_f[slot].astype(BF16)
        wd_bf[...] = wd_f[slot].astype(BF16)

    return e, in_group, used


def _moe_rows(rs, e, in_group, used, gm_ref, xs_ref, bgu_ref, bd_ref, o_ref, wgu_bf, wd_bf):
    n_valid = gm_ref[_GM_COUNT, e] - in_group * MOE_BLOCK

    def ffn(r):
        n = r.stop - r.start
        row = lax.broadcasted_iota(jnp.int32, (n, PACKED), 0)
        xs = _unpack_row(jnp.where(row < n_valid, xs_ref[r, :], 0)).astype(BF16)
        gu = _dot(xs, wgu_bf[...]) + bgu_ref[e]
        gate = jnp.minimum(gu[:, :D_EXPERT], SWIGLU_LIMIT)
        up = jnp.clip(gu[:, D_EXPERT:], -SWIGLU_LIMIT, SWIGLU_LIMIT)
        act = (up + 1.0) * (gate * jax.nn.sigmoid(gate * SWIGLU_ALPHA))
        o_ref[r, :] = _pack_row(_dot(act.astype(BF16), wd_bf[...]) + bd_ref[e])

    def zeros(r):
        o_ref[r, :] = jnp.zeros((r.stop - r.start, PACKED), o_ref.dtype)

    if used is True:
        ffn(rs)
        return

    mid = rs.start + (rs.stop - rs.start) // 2
    short = n_valid <= mid - rs.start

    @pl.when(jnp.logical_and(used, jnp.logical_not(short)))
    def _():
        ffn(rs)

    @pl.when(jnp.logical_and(used, short))
    def _():
        ffn(slice(rs.start, mid))
        zeros(slice(mid, rs.stop))

    @pl.when(jnp.logical_not(used))
    def _():
        zeros(rs)


def _moe(group_table, xs, w_gu, b_gu, w_down, b_down):
    n_rows = xs.shape[0]
    step_rows = MOE_STEP_BLOCKS * MOE_BLOCK
    assert n_rows % step_rows == 0
    grid_spec = pltpu.PrefetchScalarGridSpec(
        num_scalar_prefetch=1,
        grid=(n_rows // step_rows,),
        in_specs=[
            pl.BlockSpec((step_rows, PACKED), lambda b, gm: (b, 0)),
            pl.BlockSpec(memory_space=pl.ANY),
            pl.BlockSpec(b_gu.shape, lambda b, gm: (0, 0, 0)),
            pl.BlockSpec(memory_space=pl.ANY),
            pl.BlockSpec(b_down.shape, lambda b, gm: (0, 0, 0)),
        ],
        out_specs=pl.BlockSpec((step_rows, PACKED), lambda b, gm: (b, 0)),
        scratch_shapes=[pltpu.VMEM((2, D_MODEL, 2 * D_EXPERT), F32),
                        pltpu.VMEM((2, D_EXPERT, D_MODEL), F32),
                        pltpu.VMEM((D_MODEL, 2 * D_EXPERT), BF16),
                        pltpu.VMEM((D_EXPERT, D_MODEL), BF16),
                        pltpu.SemaphoreType.DMA((2, 2)),
                        pltpu.SMEM((2,), jnp.int32)],
    )
    return pl.pallas_call(
        _moe_kernel,
        out_shape=jax.ShapeDtypeStruct((n_rows, PACKED), jnp.uint32),
        grid_spec=grid_spec,
        compiler_params=pltpu.CompilerParams(
            dimension_semantics=("arbitrary",), vmem_limit_bytes=VMEM_LIMIT),
        name="moe_experts",
    )(group_table, xs, w_gu, b_gu, w_down, b_down)


def _final_kernel(last_layer, x1_ref, y_ref, gate_ref, mod_ref, g_ref, o_ref):
    mod = mod_ref[...]
    gate = gate_ref[...]
    ffn = gate[:, 0:1] * _unpack_row(y_ref[0])
    for kk in range(1, TOP_K):
        ffn = ffn + gate[:, kk:kk + 1] * _unpack_row(y_ref[kk])
    x = x1_ref[...] + mod[5:6] * ffn
    o_ref[...] = _rms(x, g_ref[...]) if last_layer else x


def _final(x1, y_kt, gate, mod, norm_g, seq, last_layer):
    t = x1.shape[0]
    tm = ROW_TILE
    tiles_per_seq = seq // tm
    rows = lambda w: pl.BlockSpec((tm, w), lambda i: (i, 0))
    return pl.pallas_call(
        functools.partial(_final_kernel, last_layer),
        out_shape=jax.ShapeDtypeStruct((t, D_MODEL), F32),
        grid=(t // tm,),
        in_specs=[
            rows(D_MODEL), pl.BlockSpec((TOP_K, tm, PACKED), lambda i: (0, i, 0)), rows(LANES),
            pl.BlockSpec((None, 8, D_MODEL), lambda i: (i // tiles_per_seq, 0, 0)),
            pl.BlockSpec(norm_g.shape, lambda i: (0, 0)),
        ],
        out_specs=rows(D_MODEL),
        compiler_params=pltpu.CompilerParams(
            dimension_semantics=("arbitrary",), vmem_limit_bytes=VMEM_LIMIT),
        name="combine_final",
    )(x1, y_kt, gate, mod, norm_g)


NARROW_TILE = 512


def _narrow_kernel(shift, w_ref, o_ref):
    o_ref[...] = w_ref[:, shift:shift + o_ref.shape[1]].astype(BF16)


def _narrow_columns(w, first_col, n_cols):
    assert n_cols % NARROW_TILE == 0
    shift = first_col % LANES
    return pl.pallas_call(
        functools.partial(_narrow_kernel, shift),
        out_shape=jax.ShapeDtypeStruct((w.shape[0], n_cols), BF16),
        grid=(n_cols // NARROW_TILE,),
        in_specs=[pl.BlockSpec((pl.Element(w.shape[0]), pl.Element(NARROW_TILE + LANES)),
                               lambda j: (0, pl.multiple_of(
                                   first_col - shift + j * NARROW_TILE, LANES)))],
        out_specs=pl.BlockSpec((w.shape[0], NARROW_TILE), lambda j: (0, j)),
        compiler_params=pltpu.CompilerParams(
            dimension_semantics=("arbitrary",), vmem_limit_bytes=VMEM_LIMIT),
        name="narrow_columns",
    )(w)


def _prep_weights(w_in, w_uq, w_ukv):
    d = w_in.shape[0]
    splits = (Q_LORA, KV_LORA, QK_ROPE, CONV_WIDTH, CONV_WIDTH, CONV_WIDTH, D_MODEL, D_MODEL)
    offs = [0]
    for s in splits:
        offs.append(offs[-1] + s)
    part = lambda n: w_in[:, offs[n]:offs[n + 1]]
    z = lambda n: jnp.zeros((d, n), w_in.dtype)
    w_kpe = part(2)
    kpe_slab = jnp.concatenate([z(QK_NOPE), w_kpe, z(HEAD_PAD - QK_HEAD)], axis=1)
    w_lat = jnp.concatenate([part(0), part(1), kpe_slab], axis=1).astype(BF16)
    w_cg = _narrow_columns(w_in, offs[3], offs[8] - offs[3])

    wq = w_uq.reshape(Q_LORA, N_HEADS, QK_HEAD)
    zq = lambda n: jnp.zeros((Q_LORA, N_HEADS, n), w_uq.dtype)
    wq2 = jnp.concatenate([wq, zq(HEAD_PAD - QK_HEAD)], axis=-1)
    wq2 = wq2.reshape(Q_LORA, N_HEADS * HEAD_PAD).astype(BF16)

    wkv = w_ukv.reshape(KV_LORA, N_HEADS, QK_NOPE + V_HEAD)
    wk2 = jnp.concatenate([wkv[..., :QK_NOPE],
                           jnp.zeros((KV_LORA, N_HEADS, HEAD_PAD - QK_NOPE), w_ukv.dtype)], axis=-1)
    wk2 = wk2.reshape(KV_LORA, N_HEADS * HEAD_PAD).astype(BF16)
    wv = jnp.concatenate([wkv[..., QK_NOPE:],
                          jnp.zeros((KV_LORA, N_HEADS, HEAD_PAD - V_HEAD), w_ukv.dtype)], axis=-1)
    wv = wv.reshape(KV_LORA, N_HEADS * HEAD_PAD).astype(BF16)
    return w_lat, w_cg, wq2, wk2, wv


def _rope_freqs():
    inv_freq = 1.0 / (ROPE_THETA ** (jnp.arange(0, QK_ROPE, 2, dtype=F32) / QK_ROPE))
    return inv_freq.reshape(QK_ROPE // 2, 1)


def _dest_kernel(gm_ref, idx_ref, rank_ref, o_ref):
    idx = idx_ref[...]
    dest = rank_ref[...]
    for e in range(N_EXPERTS):
        dest = dest + jnp.where(idx == e, gm_ref[_GM_FIRST, e] * MOE_BLOCK, 0)
    o_ref[...] = dest


def _route(top_idx, rank, counts, n_tokens):
    blocks = (counts + MOE_BLOCK - 1) // MOE_BLOCK
    first_block = jnp.cumsum(blocks) - blocks
    table = jnp.stack([counts, first_block, blocks]).astype(jnp.int32)
    whole = pl.BlockSpec(top_idx.shape, lambda i, gm: (0, 0))
    dest = pl.pallas_call(
        _dest_kernel,
        out_shape=jax.ShapeDtypeStruct(rank.shape, jnp.int32),
        grid_spec=pltpu.PrefetchScalarGridSpec(
            num_scalar_prefetch=1, grid=(1,), in_specs=[whole, whole], out_specs=whole),
        name="row_destinations",
    )(table, top_idx, rank)[:TOP_K]
    n_rows = n_tokens * TOP_K + N_EXPERTS * MOE_BLOCK
    return dest, table, n_rows


SC_CORES = 2
SC_SUBCORES = 16
SC_WORKERS = SC_CORES * SC_SUBCORES
SC_CHUNK = 64
SC_GATHER_RING = 3

def _sc_mesh():
    return plsc.VectorSubcoreMesh(core_axis_name="c", subcore_axis_name="s")


def _sc_worker():
    return lax.axis_index("s") * SC_CORES + lax.axis_index("c")


def _dispatch(h2, dest, n_rows):
    t, d = h2.shape
    per_w = t // SC_WORKERS
    n_chunks = per_w // SC_CHUNK
    assert per_w % (2 * SC_CHUNK) == 0
    idx = dest.reshape(TOP_K, SC_WORKERS, n_chunks, SC_CHUNK).transpose(1, 0, 2, 3)
    idx = idx.reshape(SC_WORKERS, TOP_K * n_chunks, SC_CHUNK)

    @functools.partial(
        pl.kernel, mesh=_sc_mesh(),
        out_type=jax.ShapeDtypeStruct((n_rows, d), h2.dtype),
        scratch_types=[pltpu.VMEM((TOP_K * n_chunks, SC_CHUNK), jnp.int32),
                       pltpu.VMEM((2, SC_CHUNK, d), h2.dtype),
                       pltpu.SemaphoreType.DMA((2,)),
                       pltpu.SemaphoreType.DMA((2,))],
        name="moe_dispatch")
    def run(h2_hbm, idx_hbm, xs_hbm, idx_v, rows_v, rsem, ssem):
        w = _sc_worker()
        pltpu.sync_copy(idx_hbm.at[w], idx_v)

        def read(g, b):
            src = h2_hbm.at[pl.ds(w * per_w + g * SC_CHUNK, SC_CHUNK)]
            return pltpu.make_async_copy(src, rows_v.at[b], rsem.at[b])

        def scatter(g, kk, b):
            dst = xs_hbm.at[idx_v.at[kk * n_chunks + g]]
            return pltpu.make_async_copy(rows_v.at[b], dst, ssem.at[b])

        read(0, 0).start()

        @pl.loop(0, n_chunks, step=2)
        def _(g0):
            for b in range(2):
                g = g0 + b
                read(g, b).wait()

                @pl.when(g + 1 < n_chunks)
                def _():
                    read(g + 1, 1 - b).start()

                for kk in range(TOP_K):
                    scatter(g, kk, b).start()
                for kk in range(TOP_K):
                    scatter(g, kk, b).wait()

    return run(h2, idx)


def _undispatch(ys, dest):
    t = dest.shape[1]
    d = ys.shape[1]
    n_out = t * TOP_K
    per_w = n_out // SC_WORKERS
    n_chunks = per_w // SC_CHUNK
    idx = dest.reshape(SC_WORKERS, n_chunks, SC_CHUNK)
    ring = SC_GATHER_RING

    @functools.partial(
        pl.kernel, mesh=_sc_mesh(),
        out_type=jax.ShapeDtypeStruct((n_out, d), ys.dtype),
        scratch_types=[pltpu.VMEM((n_chunks, SC_CHUNK), jnp.int32),
                       pltpu.VMEM((ring, SC_CHUNK, d), ys.dtype),
                       pltpu.SemaphoreType.DMA((ring,)),
                       pltpu.SemaphoreType.DMA((ring,))],
        name="moe_undispatch")
    def run(ys_hbm, idx_hbm, out_hbm, idx_v, rows_v, gsem, wsem):
        w = _sc_worker()
        pltpu.sync_copy(idx_hbm.at[w], idx_v)

        def gather(g):
            b = g % ring
            return pltpu.make_async_copy(ys_hbm.at[idx_v.at[g]], rows_v.at[b], gsem.at[b])

        def write(g):
            b = g % ring
            dst = out_hbm.at[pl.ds(w * per_w + g * SC_CHUNK, SC_CHUNK)]
            return pltpu.make_async_copy(rows_v.at[b], dst, wsem.at[b])

        for g in range(min(ring - 1, n_chunks)):
            gather(g).start()
        for g in range(n_chunks):
            gather(g).wait()
            ahead = g + ring - 1
            if ahead < n_chunks:
                if g >= 1:
                    write(g - 1).wait()
                gather(ahead).start()
            write(g).start()
        for g in range(max(n_chunks - ring, 0), n_chunks):
            write(g).wait()

    return run(ys, idx).reshape(TOP_K, t, d)


def kernel(x, c, positions, w_ada, b_ada, norm_mix_g, w_in, q_norm_g, w_uq, kv_norm_g, w_ukv,
           w_up_attn, conv_w, w_up_conv, w_o, norm_ffn_g, router_w, router_b, w_gu, b_gu,
           w_down, b_down, norm_final_g):
    batch, seq, d = x.shape
    t = batch * seq
    depth = w_ada.shape[0]
    assert d == D_MODEL and batch <= 8 and conv_w.shape[1:] == (CONV_K, CONV_WIDTH)
    assert seq % ROW_TILE == 0 and seq % POST_TILE == 0 and seq % ATT_BLOCK == 0
    assert t % (2 * SC_CHUNK * SC_WORKERS) == 0
    x2 = x.reshape(t, d)
    pos = positions.astype(F32).reshape(t // ROW_TILE, 1, ROW_TILE)
    freqs = _rope_freqs()
    c_pad = jnp.zeros((8, d), F32).at[:batch].set(c)

    for l in range(depth):
        ada = _ada(c_pad, w_ada[l], b_ada[l].reshape(1, -1))
        mod = ada[:batch].reshape(batch, 6, d)
        mod = jnp.concatenate([mod, jnp.zeros((batch, 2, d), F32)], axis=1)

        w_lat, w_cg, wq2, wk2, wv = _prep_weights(w_in[l], w_uq[l], w_ukv[l])
        q, k, v, sga, gc = _pre(x2, mod, norm_mix_g[l].reshape(1, d), w_lat, w_cg,
                                q_norm_g[l].reshape(1, -1), wq2,
                                kv_norm_g[l].reshape(1, -1), wk2, wv, pos, freqs, conv_w[l],
                                w_up_conv[l].astype(BF16), seq)
        merged = _attention(q, k, v, sga, gc, w_up_attn[l], batch, seq)

        rw_pad = jnp.concatenate([router_w[l], jnp.zeros((d, LANES - N_EXPERTS), F32)], axis=1)
        rb_pad = jnp.concatenate([router_b[l], jnp.full((LANES - N_EXPERTS,), NEG_BIG, F32)])
        rw_hi = rw_pad.astype(BF16)
        rw_lo = (rw_pad - rw_hi.astype(F32)).astype(BF16)
        x1, h2, idx_pad, gate_pad, rank_pad, counts = _post(
            merged, x2, mod, w_o[l], norm_ffn_g[l].reshape(1, d),
            jnp.concatenate([rw_hi, rw_lo], axis=1), rb_pad.reshape(1, LANES), seq)

        dest, group_table, n_rows = _route(
            idx_pad, rank_pad, counts[0, :N_EXPERTS], t)
        xs = _dispatch(h2, dest, n_rows)
        ys = _moe(group_table, xs, w_gu[l], b_gu[l].reshape(N_EXPERTS, 1, -1),
                  w_down[l], b_down[l].reshape(N_EXPERTS, 1, -1))
        y_kt = _undispatch(ys, dest)
        x2 = _final(x1, y_kt, gate_pad, mod, norm_final_g.reshape(1, d), seq, l == depth - 1)

    return x2.reshape(batch, seq, d)
```

```python
import functools
import math

import jax
import jax.numpy as jnp
from jax import lax
from jax.experimental import pallas as pl
from jax.experimental.pallas import tpu as pltpu
from jax.experimental.pallas import tpu_sc as plsc

D_MODEL = 1024
CHUNK = 64
N_HEADS = 8
Q_LORA = 256
KV_LORA = 128
QK_NOPE = 64
QK_ROPE = 32
V_HEAD = 64
QK_HEAD = QK_NOPE + QK_ROPE
ROPE_THETA = 10000.0
CONV_WIDTH = 512
CONV_K = 3
N_EXPERTS = 32
TOP_K = 4
D_EXPERT = 1024
SWIGLU_LIMIT = 7.0
SWIGLU_ALPHA = 1.702
MOE_BLOCK = 256
RMS_EPS = 1e-6

LANES = 128
HEAD_PAD = 128
NEG_BIG = -1e30
VMEM_LIMIT = 56 * 1024 * 1024

F32 = jnp.float32
BF16 = jnp.bfloat16

Q_PRESCALE = (QK_HEAD ** -0.5) * math.log2(math.e)

ADA_TILE = 1024
ROW_TILE = 1024
MOE_STEP_BLOCKS = 4
POST_TILE = 1024
POST_SUB = 512
ATT_BLOCK = 512
ATT_WIDE = 2


def _rms(x, g):
    ms = jnp.mean(x * x, axis=-1, keepdims=True)
    return x * lax.rsqrt(ms + RMS_EPS) * g


def _dot(a, b):
    return jnp.dot(a, b, preferred_element_type=F32)


PACKED = D_MODEL // 2


def _pack_row(x):
    return pltpu.pack_elementwise([x[:, :PACKED], x[:, PACKED:]], packed_dtype=BF16)


def _unpack_row(w):
    half = lambda i: pltpu.unpack_elementwise(w, index=i, packed_dtype=BF16, unpacked_dtype=F32)
    return jnp.concatenate([half(0), half(1)], axis=-1)


def _ada_kernel(c_ref, w_ref, b_ref, o_ref):
    c = c_ref[...]
    ca = (c * jax.nn.sigmoid(c)).astype(BF16)
    o_ref[...] = _dot(ca, w_ref[...].astype(BF16)) + b_ref[...]


def _ada(c_pad, w_ada, b_ada):
    n = w_ada.shape[1]
    tn = ADA_TILE
    return pl.pallas_call(
        _ada_kernel,
        out_shape=jax.ShapeDtypeStruct((c_pad.shape[0], n), F32),
        grid=(n // tn,),
        in_specs=[
            pl.BlockSpec(c_pad.shape, lambda j: (0, 0)),
            pl.BlockSpec((D_MODEL, tn), lambda j: (0, j)),
            pl.BlockSpec((1, tn), lambda j: (0, j)),
        ],
        out_specs=pl.BlockSpec((c_pad.shape[0], tn), lambda j: (0, j)),
        compiler_params=pltpu.CompilerParams(
            dimension_semantics=("arbitrary",), vmem_limit_bytes=VMEM_LIMIT),
        name="ada",
    )(c_pad, w_ada, b_ada)


_C_QLAT = 0
_C_KVLAT = _C_QLAT + Q_LORA
_C_KPE = _C_KVLAT + KV_LORA
_C_END = _C_KPE + HEAD_PAD


def _pre_kernel(tiles_per_seq, x_ref, mod_ref, g_ref, wlat_ref, wconv_ref, wgate_ref, qg_ref,
                wq_ref, kvg_ref, wk_ref, wv_ref, pos_ref, freq_ref, cw_ref, wuc_ref,
                q_ref, k_ref, v_ref, sga_ref, gc_ref, carry_ref):
    i = pl.program_id(0)
    tm = x_ref.shape[0]
    mod = mod_ref[...]
    h = _rms(x_ref[...], g_ref[...]) * (1.0 + mod[1:2]) + mod[0:1]
    hb = h.astype(BF16)

    ang = freq_ref[...] * pos_ref[...]
    cos_t, sin_t = jnp.cos(ang), jnp.sin(ang)
    ones_t = jnp.ones((QK_NOPE, tm), F32)
    zeros_t = jnp.zeros((QK_NOPE, tm), F32)
    pad_t = jnp.zeros((HEAD_PAD - QK_HEAD, tm), F32)
    cosf = jnp.concatenate([ones_t, cos_t, cos_t, pad_t], axis=0).T
    sinf = jnp.concatenate([zeros_t, -sin_t, sin_t, pad_t], axis=0).T

    first_half = lax.broadcasted_iota(jnp.int32, (tm, HEAD_PAD), 1) < QK_NOPE + QK_ROPE // 2

    def rope(slab):
        swapped = jnp.where(first_half, pltpu.roll(slab, HEAD_PAD - QK_ROPE // 2, 1),
                            pltpu.roll(slab, QK_ROPE // 2, 1))
        return slab * cosf + swapped * sinf

    small = _dot(hb, wlat_ref[...])
    q_lat = small[:, _C_QLAT:_C_KVLAT]
    kv_lat = small[:, _C_KVLAT:_C_KPE]
    kpe = rope(small[:, _C_KPE:_C_END])
    qn = _rms(q_lat, qg_ref[...]).astype(BF16)
    q = _dot(qn, wq_ref[...])
    q = jnp.concatenate([rope(q[:, hd * HEAD_PAD:(hd + 1) * HEAD_PAD]) for hd in range(N_HEADS)],
                        axis=-1)
    q_ref[...] = (q * Q_PRESCALE).astype(BF16)
    kvn = _rms(kv_lat, kvg_ref[...]).astype(BF16)
    k = _dot(kvn, wk_ref[...]) + jnp.concatenate([kpe] * N_HEADS, axis=-1)
    k_ref[...] = k.astype(BF16)
    lane = lax.broadcasted_iota(jnp.int32, (tm, N_HEADS * HEAD_PAD), 1)
    ones_col = jnp.where(lane % HEAD_PAD == V_HEAD, 1.0, 0.0)
    v_ref[...] = (_dot(kvn, wv_ref[...]) + ones_col).astype(BF16)

    ucb = _dot(hb, wconv_ref[...])
    cu = ucb[:, 0:CONV_WIDTH] * ucb[:, CONV_WIDTH:2 * CONV_WIDTH]
    b_gate = ucb[:, 2 * CONV_WIDTH:3 * CONV_WIDTH]

    @pl.when(i % tiles_per_seq == 0)
    def _():
        carry_ref[...] = jnp.zeros_like(carry_ref)

    prev = carry_ref[...]
    row = lax.broadcasted_iota(jnp.int32, cu.shape, 0)
    cu1 = jnp.where(row == 0, prev[7:8], pltpu.roll(cu, 1, 0))
    cu2 = jnp.where(row == 0, prev[6:7], jnp.where(row == 1, prev[7:8], pltpu.roll(cu, 2, 0)))
    cw = cw_ref[...]
    z = cw[2:3] * cu + cw[1:2] * cu1 + cw[0:1] * cu2
    carry_ref[...] = cu[tm - 8:tm]
    c_branch = _dot((b_gate * z).astype(BF16), wuc_ref[...])

    gates = _dot(hb, wgate_ref[...])
    sga_ref[...] = jax.nn.sigmoid(gates[:, 0:D_MODEL]).astype(BF16)
    gc_ref[...] = (jax.nn.sigmoid(gates[:, D_MODEL:]) * c_branch).astype(BF16)


def _pre(x2, mod, norm_g, w_lat, w_conv, w_gate, q_norm_g, wq2, kv_norm_g, wk2, wv, pos,
         freqs, conv_w, w_up_conv, seq):
    t = x2.shape[0]
    tm = ROW_TILE
    tiles_per_seq = seq // tm
    full = lambda a: pl.BlockSpec(a.shape, lambda i: (0,) * a.ndim)
    rows = lambda w: pl.BlockSpec((tm, w), lambda i: (i, 0))
    outs = [jax.ShapeDtypeStruct((t, N_HEADS * HEAD_PAD), BF16),
            jax.ShapeDtypeStruct((t, N_HEADS * HEAD_PAD), BF16),
            jax.ShapeDtypeStruct((t, N_HEADS * HEAD_PAD), BF16),
            jax.ShapeDtypeStruct((t, D_MODEL), BF16),
            jax.ShapeDtypeStruct((t, D_MODEL), BF16)]
    return pl.pallas_call(
        functools.partial(_pre_kernel, tiles_per_seq),
        out_shape=outs,
        grid=(t // tm,),
        in_specs=[
            rows(D_MODEL),
            pl.BlockSpec((None, 8, D_MODEL), lambda i: (i // tiles_per_seq, 0, 0)),
            full(norm_g), full(w_lat), full(w_conv), full(w_gate), full(q_norm_g), full(wq2),
            full(kv_norm_g), full(wk2), full(wv),
            pl.BlockSpec((None, 1, tm), lambda i: (i, 0, 0)), full(freqs), full(conv_w),
            full(w_up_conv),
        ],
        out_specs=[rows(N_HEADS * HEAD_PAD), rows(N_HEADS * HEAD_PAD), rows(N_HEADS * HEAD_PAD),
                   rows(D_MODEL), rows(D_MODEL)],
        scratch_shapes=[pltpu.VMEM((8, CONV_WIDTH), F32)],
        compiler_params=pltpu.CompilerParams(
            dimension_semantics=("arbitrary",), vmem_limit_bytes=VMEM_LIMIT),
        name="pre_mixer",
    )(x2, mod, norm_g, w_lat, w_conv, w_gate, q_norm_g, wq2, kv_norm_g, wk2, wv, pos,
      freqs, conv_w, w_up_conv)


def _attn_kernel(q_ref, k_ref, v_ref, sga_ref, gc_ref, wua_ref, o_ref, m_ref, acc_ref, wua_bf):
    i = pl.program_id(1)
    tq = q_ref.shape[0]

    @pl.when(jnp.logical_and(pl.program_id(0) == 0, i == 0))
    def _():
        wua_bf[...] = wua_ref[...].astype(BF16)

    def step(k0, tk, masked, first=False):
        if masked:
            rq = (lax.broadcasted_iota(jnp.int32, (tq, tk), 0) + (tk - tq)) // CHUNK
            ck = lax.broadcasted_iota(jnp.int32, (tq, tk), 1) // CHUNK
            allowed = ck <= rq
        for hd in range(N_HEADS):
            hs = slice(hd * HEAD_PAD, (hd + 1) * HEAD_PAD)
            s = lax.dot_general(q_ref[:, hs], k_ref[pl.ds(k0, tk), hs],
                                (((1,), (1,)), ((), ())), preferred_element_type=F32)
            if masked:
                s = jnp.where(allowed, s, NEG_BIG)
            s_max = s[:, 0:LANES]
            for c in range(1, tk // LANES):
                s_max = jnp.maximum(s_max, s[:, c * LANES:(c + 1) * LANES])
            m_new = jnp.broadcast_to(jnp.max(s_max, axis=-1, keepdims=True), (tq, LANES))
            if not first:
                m_old = m_ref[hd]
                m_new = jnp.maximum(m_old, m_new)
            p = jnp.concatenate(
                [jnp.exp2(s[:, c * LANES:(c + 1) * LANES] - m_new).astype(BF16)
                 for c in range(tk // LANES)], axis=-1)
            pv = _dot(p, v_ref[pl.ds(k0, tk), hs])
            acc_ref[hd] = pv if first else jnp.exp2(m_old - m_new) * acc_ref[hd] + pv
            m_ref[hd] = m_new

    wide = ATT_WIDE * tq
    n_wide = i // ATT_WIDE

    @pl.when(n_wide > 0)
    def _():
        step(0, wide, False, first=True)

    @pl.when(n_wide == 0)
    def _():
        m_ref[...] = jnp.full_like(m_ref, NEG_BIG)
        acc_ref[...] = jnp.zeros_like(acc_ref)

    def body(j, carry):
        step(pl.multiple_of(j * wide, wide), wide, False)
        return carry

    lax.fori_loop(1, n_wide, body, 0)

    for r in range(ATT_WIDE):
        @pl.when(i % ATT_WIDE == r)
        def _():
            step(pl.multiple_of((i - r) * tq, tq), (r + 1) * tq, True)

    heads = []
    for hd in range(N_HEADS):
        acc = acc_ref[hd]
        heads.append((acc[:, 0:V_HEAD] / acc[:, V_HEAD:V_HEAD + 1]).astype(BF16))
    a_branch = _dot(jnp.concatenate(heads, axis=-1), wua_bf[...])
    o_ref[...] = (sga_ref[...].astype(F32) * a_branch + gc_ref[...].astype(F32)).astype(BF16)


def _attention(q, k, v, sga, gc, wua, batch, seq):
    tq = ATT_BLOCK
    nq = seq // tq
    q_rows = lambda w: pl.BlockSpec((tq, w), lambda b, i: (b * nq + i, 0))
    whole_seq = pl.BlockSpec((seq, N_HEADS * HEAD_PAD), lambda b, i: (b, 0))
    return pl.pallas_call(
        _attn_kernel,
        out_shape=jax.ShapeDtypeStruct((batch * seq, D_MODEL), BF16),
        grid=(batch, nq),
        in_specs=[q_rows(N_HEADS * HEAD_PAD), whole_seq, whole_seq, q_rows(D_MODEL),
                  q_rows(D_MODEL), pl.BlockSpec(wua.shape, lambda b, i: (0, 0))],
        out_specs=q_rows(D_MODEL),
        scratch_shapes=[pltpu.VMEM((N_HEADS, tq, LANES), F32),
                        pltpu.VMEM((N_HEADS, tq, LANES), F32),
                        pltpu.VMEM(wua.shape, BF16)],
        compiler_params=pltpu.CompilerParams(
            dimension_semantics=("arbitrary", "arbitrary"), vmem_limit_bytes=VMEM_LIMIT),
        name="attention",
    )(q, k, v, sga, gc, wua)


def _post_kernel(merged_ref, x_ref, mod_ref, wo_ref, g_ref, rw_ref, rb_ref,
                 x1_ref, h2_ref, idx_ref, gate_ref, rank_ref, cnt_out_ref, cnt_ref, lg_ref,
                 wo_bf):
    i = pl.program_id(0)

    @pl.when(i == 0)
    def _():
        cnt_ref[...] = jnp.zeros_like(cnt_ref)
        lg_ref[...] = jnp.zeros_like(lg_ref)
        wo_bf[...] = wo_ref[...].astype(BF16)

    prev_logits = lg_ref[(i + 1) % 2]
    counts = cnt_ref[...]
    routed = counts
    for r0 in range(0, x_ref.shape[0], POST_SUB):
        rs = slice(r0, r0 + POST_SUB)
        lg_ref[i % 2, rs, :] = _post_mix(rs, merged_ref, x_ref, mod_ref, wo_bf, g_ref, rw_ref,
                                         rb_ref, x1_ref, h2_ref)
        routed = _post_route(rs, prev_logits[rs, :], routed, idx_ref, gate_ref, rank_ref)
    counts = jnp.where(i > 0, routed, counts)
    cnt_ref[...] = counts
    cnt_out_ref[...] = counts.astype(jnp.int32)


def _post_mix(rs, merged_ref, x_ref, mod_ref, wo_ref, g_ref, rw_ref, rb_ref, x1_ref, h2_ref):
    mod = mod_ref[...]
    mix = _dot(merged_ref[rs, :], wo_ref[...])
    x1 = x_ref[rs, :] + mod[2:3] * mix
    x1_ref[rs, :] = x1
    h2 = _rms(x1, g_ref[...]) * (1.0 + mod[4:5]) + mod[3:4]
    h2_ref[rs, :] = _pack_row(h2)

    both = _dot(h2.astype(BF16), rw_ref[...])
    return both[:, :LANES] + both[:, LANES:] + rb_ref[...]


def _post_route(rs, logits, counts, idx_ref, gate_ref, rank_ref):
    lane = lax.broadcasted_iota(jnp.int32, logits.shape, 1)
    work = logits
    vals, idxs = [], []
    for _ in range(TOP_K):
        mk = jnp.max(work, axis=-1, keepdims=True)
        ik = jnp.min(jnp.where(work == mk, lane, LANES), axis=-1, keepdims=True)
        vals.append(mk)
        idxs.append(ik)
        work = jnp.where(lane == ik, -jnp.inf, work)
    es = [jnp.exp(vk - vals[0]) for vk in vals]
    denom = es[0] + es[1] + es[2] + es[3]
    tm = logits.shape[0]
    chosen = jnp.zeros(logits.shape, F32)
    for kk in range(TOP_K):
        chosen = chosen + jnp.where(lane == idxs[kk], 1.0, 0.0)
    r_i = lax.broadcasted_iota(jnp.int32, (tm, tm), 0)
    c_i = lax.broadcasted_iota(jnp.int32, (tm, tm), 1)
    earlier = jnp.where(c_i < r_i, 1.0, 0.0).astype(BF16)
    before = _dot(earlier, chosen.astype(BF16)) + counts[0:1]

    idx_out = jnp.zeros(logits.shape, F32)
    gate_out = jnp.zeros(logits.shape, F32)
    rank_out = jnp.zeros(logits.shape, F32)
    for kk in range(TOP_K):
        rank_k = jnp.sum(jnp.where(lane == idxs[kk], before, 0.0), axis=-1, keepdims=True)
        idx_out = jnp.where(lane == kk, idxs[kk].astype(F32), idx_out)
        gate_out = jnp.where(lane == kk, es[kk] / denom, gate_out)
        rank_out = jnp.where(lane == kk, rank_k, rank_out)
    gate_ref[rs, :] = gate_out
    idx_ref[:, rs] = idx_out.T[0:8].astype(jnp.int32)
    rank_ref[:, rs] = rank_out.T[0:8].astype(jnp.int32)
    return counts + jnp.sum(chosen, axis=0, keepdims=True)


def _post(merged, x2, mod, wo, norm_g, rw_hl, rb_pad, seq):
    t = x2.shape[0]
    tm = POST_TILE
    tiles_per_seq = seq // tm
    n_tiles = t // tm
    full = lambda a: pl.BlockSpec(a.shape, lambda i: (0,) * a.ndim)
    mix_tile = lambda i: jnp.minimum(i, n_tiles - 1)
    route_tile = lambda i: jnp.maximum(i - 1, 0)
    rows = lambda w: pl.BlockSpec((tm, w), lambda i: (mix_tile(i), 0))
    outs = [jax.ShapeDtypeStruct((t, D_MODEL), F32),
            jax.ShapeDtypeStruct((t, PACKED), jnp.uint32),
            jax.ShapeDtypeStruct((8, t), jnp.int32),
            jax.ShapeDtypeStruct((t, LANES), F32),
            jax.ShapeDtypeStruct((8, t), jnp.int32),
            jax.ShapeDtypeStruct((8, LANES), jnp.int32)]
    slots = pl.BlockSpec((8, tm), lambda i: (0, route_tile(i)))
    return pl.pallas_call(
        _post_kernel,
        out_shape=outs,
        grid=(n_tiles + 1,),
        in_specs=[
            rows(D_MODEL), rows(D_MODEL),
            pl.BlockSpec((None, 8, D_MODEL), lambda i: (mix_tile(i) // tiles_per_seq, 0, 0)),
            full(wo), full(norm_g), full(rw_hl), full(rb_pad),
        ],
        out_specs=[rows(D_MODEL), rows(PACKED), slots,
                   pl.BlockSpec((tm, LANES), lambda i: (route_tile(i), 0)), slots,
                   pl.BlockSpec((8, LANES), lambda i: (0, 0))],
        scratch_shapes=[pltpu.VMEM((8, LANES), F32), pltpu.VMEM((2, tm, LANES), F32),
                        pltpu.VMEM(wo.shape, BF16)],
        compiler_params=pltpu.CompilerParams(
            dimension_semantics=("arbitrary",), vmem_limit_bytes=VMEM_LIMIT),
        name="post_mixer",
    )(merged, x2, mod, wo, norm_g, rw_hl, rb_pad)


_GM_COUNT, _GM_FIRST, _GM_BLOCKS = range(3)
_ST_EXPERT, _ST_SLOT = range(2)


def _moe_kernel(gm_ref, xs_ref, wgu_hbm, bgu_ref, wd_hbm, bd_ref, o_ref,
                wgu_f, wd_f, wgu_bf, wd_bf, sem, st_ref):
    weights = (wgu_hbm, wd_hbm, wgu_f, wd_f, wgu_bf, wd_bf, sem)
    rows = (xs_ref, bgu_ref, bd_ref, o_ref, wgu_bf, wd_bf)
    b0 = pl.program_id(0) * MOE_STEP_BLOCKS
    e, in_group, used = _moe_enter(b0, gm_ref, st_ref, *weights)
    together = jnp.logical_and(used, in_group + MOE_STEP_BLOCKS <= gm_ref[_GM_BLOCKS, e])

    @pl.when(together)
    def _():
        _moe_rows(slice(0, MOE_STEP_BLOCKS * MOE_BLOCK), e, in_group, True, gm_ref, *rows)

    @pl.when(jnp.logical_not(together))
    def _():
        for r in range(0, MOE_STEP_BLOCKS, 2):
            if r == 0:
                e_r, in_group_r, used_r = e, in_group, used
            else:
                e_r, in_group_r, used_r = _moe_enter(b0 + r, gm_ref, st_ref, *weights)
            paired = jnp.logical_and(used_r, in_group_r + 2 <= gm_ref[_GM_BLOCKS, e_r])

            @pl.when(paired)
            def _():
                _moe_rows(slice(r * MOE_BLOCK, (r + 2) * MOE_BLOCK), e_r, in_group_r, True,
                          gm_ref, *rows)

            @pl.when(jnp.logical_not(paired))
            def _():
                _moe_rows(slice(r * MOE_BLOCK, (r + 1) * MOE_BLOCK), e_r, in_group_r, used_r,
                          gm_ref, *rows)
                e_n, in_group_n, used_n = _moe_enter(b0 + r + 1, gm_ref, st_ref, *weights)
                _moe_rows(slice((r + 1) * MOE_BLOCK, (r + 2) * MOE_BLOCK), e_n, in_group_n,
                          used_n, gm_ref, *rows)


def _moe_enter(b, gm_ref, st_ref, wgu_hbm, wd_hbm, wgu_f, wd_f, wgu_bf, wd_bf, sem):
    def weight_copies(expert, sl):
        return (pltpu.make_async_copy(wgu_hbm.at[expert], wgu_f.at[sl], sem.at[0, sl]),
                pltpu.make_async_copy(wd_hbm.at[expert], wd_f.at[sl], sem.at[1, sl]))

    def next_group(e):
        return lax.while_loop(
            lambda k: jnp.logical_and(k < N_EXPERTS,
                                      gm_ref[_GM_BLOCKS, jnp.minimum(k, N_EXPERTS - 1)] == 0),
            lambda k: k + 1, e)

    @pl.when(b == 0)
    def _():
        e0 = next_group(0)
        st_ref[_ST_EXPERT] = e0
        st_ref[_ST_SLOT] = 1
        for cp in weight_copies(e0, 0):
            cp.start()

    e_prev = st_ref[_ST_EXPERT]
    past = b >= gm_ref[_GM_FIRST, e_prev] + gm_ref[_GM_BLOCKS, e_prev]
    e = jnp.minimum(jnp.where(past, next_group(e_prev + 1), e_prev), N_EXPERTS - 1)
    st_ref[_ST_EXPERT] = e
    in_group = b - gm_ref[_GM_FIRST, e]
    used = jnp.logical_and(in_group >= 0, in_group < gm_ref[_GM_BLOCKS, e])

    @pl.when(jnp.logical_and(used, in_group == 0))
    def _():
        slot = 1 - st_ref[_ST_SLOT]
        st_ref[_ST_SLOT] = slot
        nxt = next_group(e + 1)

        @pl.when(nxt < N_EXPERTS)
        def _():
            for cp in weight_copies(nxt, 1 - slot):
                cp.start(priority=1)

        for cp in weight_copies(e, slot):
            cp.wait()
        wgu_bf[...] = wgu_f[slot].astype(BF16)
        wd_bf[...] = wd_f[slot].astype(BF16)

    return e, in_group, used


def _moe_rows(rs, e, in_group, used, gm_ref, xs_ref, bgu_ref, bd_ref, o_ref, wgu_bf, wd_bf):
    n_valid = gm_ref[_GM_COUNT, e] - in_group * MOE_BLOCK

    def ffn(r):
        n = r.stop - r.start
        row = lax.broadcasted_iota(jnp.int32, (n, PACKED), 0)
        xs = _unpack_row(jnp.where(row < n_valid, xs_ref[r, :], 0)).astype(BF16)
        gu = _dot(xs, wgu_bf[...]) + bgu_ref[e]
        gate = jnp.minimum(gu[:, :D_EXPERT], SWIGLU_LIMIT)
        up = jnp.clip(gu[:, D_EXPERT:], -SWIGLU_LIMIT, SWIGLU_LIMIT)
        act = (up + 1.0) * (gate * jax.nn.sigmoid(gate * SWIGLU_ALPHA))
        o_ref[r, :] = _pack_row(_dot(act.astype(BF16), wd_bf[...]) + bd_ref[e])

    def zeros(r):
        o_ref[r, :] = jnp.zeros((r.stop - r.start, PACKED), o_ref.dtype)

    if used is True:
        ffn(rs)
        return

    mid = rs.start + (rs.stop - rs.start) // 2
    short = n_valid <= mid - rs.start

    @pl.when(jnp.logical_and(used, jnp.logical_not(short)))
    def _():
        ffn(rs)

    @pl.when(jnp.logical_and(used, short))
    def _():
        ffn(slice(rs.start, mid))
        zeros(slice(mid, rs.stop))

    @pl.when(jnp.logical_not(used))
    def _():
        zeros(rs)


def _moe(group_table, xs, w_gu, b_gu, w_down, b_down):
    n_rows = xs.shape[0]
    step_rows = MOE_STEP_BLOCKS * MOE_BLOCK
    assert n_rows % step_rows == 0
    grid_spec = pltpu.PrefetchScalarGridSpec(
        num_scalar_prefetch=1,
        grid=(n_rows // step_rows,),
        in_specs=[
            pl.BlockSpec((step_rows, PACKED), lambda b, gm: (b, 0)),
            pl.BlockSpec(memory_space=pl.ANY),
            pl.BlockSpec(b_gu.shape, lambda b, gm: (0, 0, 0)),
            pl.BlockSpec(memory_space=pl.ANY),
            pl.BlockSpec(b_down.shape, lambda b, gm: (0, 0, 0)),
        ],
        out_specs=pl.BlockSpec((step_rows, PACKED), lambda b, gm: (b, 0)),
        scratch_shapes=[pltpu.VMEM((2, D_MODEL, 2 * D_EXPERT), F32),
                        pltpu.VMEM((2, D_EXPERT, D_MODEL), F32),
                        pltpu.VMEM((D_MODEL, 2 * D_EXPERT), BF16),
                        pltpu.VMEM((D_EXPERT, D_MODEL), BF16),
                        pltpu.SemaphoreType.DMA((2, 2)),
                        pltpu.SMEM((2,), jnp.int32)],
    )
    return pl.pallas_call(
        _moe_kernel,
        out_shape=jax.ShapeDtypeStruct((n_rows, PACKED), jnp.uint32),
        grid_spec=grid_spec,
        compiler_params=pltpu.CompilerParams(
            dimension_semantics=("arbitrary",), vmem_limit_bytes=VMEM_LIMIT),
        name="moe_experts",
    )(group_table, xs, w_gu, b_gu, w_down, b_down)


def _final_kernel(last_layer, x1_ref, y_ref, gate_ref, mod_ref, g_ref, o_ref):
    mod = mod_ref[...]
    gate = gate_ref[...]
    ffn = gate[:, 0:1] * _unpack_row(y_ref[0])
    for kk in range(1, TOP_K):
        ffn = ffn + gate[:, kk:kk + 1] * _unpack_row(y_ref[kk])
    x = x1_ref[...] + mod[5:6] * ffn
    o_ref[...] = _rms(x, g_ref[...]) if last_layer else x


def _final(x1, y_kt, gate, mod, norm_g, seq, last_layer):
    t = x1.shape[0]
    tm = ROW_TILE
    tiles_per_seq = seq // tm
    rows = lambda w: pl.BlockSpec((tm, w), lambda i: (i, 0))
    return pl.pallas_call(
        functools.partial(_final_kernel, last_layer),
        out_shape=jax.ShapeDtypeStruct((t, D_MODEL), F32),
        grid=(t // tm,),
        in_specs=[
            rows(D_MODEL), pl.BlockSpec((TOP_K, tm, PACKED), lambda i: (0, i, 0)), rows(LANES),
            pl.BlockSpec((None, 8, D_MODEL), lambda i: (i // tiles_per_seq, 0, 0)),
            pl.BlockSpec(norm_g.shape, lambda i: (0, 0)),
        ],
        out_specs=rows(D_MODEL),
        compiler_params=pltpu.CompilerParams(
            dimension_semantics=("arbitrary",), vmem_limit_bytes=VMEM_LIMIT),
        name="combine_final",
    )(x1, y_kt, gate, mod, norm_g)


def _prep_weights(w_in, w_uq, w_ukv):
    d = w_in.shape[0]
    splits = (Q_LORA, KV_LORA, QK_ROPE, CONV_WIDTH, CONV_WIDTH, CONV_WIDTH, D_MODEL, D_MODEL)
    offs = [0]
    for s in splits:
        offs.append(offs[-1] + s)
    part = lambda n: w_in[:, offs[n]:offs[n + 1]]
    z = lambda n: jnp.zeros((d, n), w_in.dtype)
    w_kpe = part(2)
    kpe_slab = jnp.concatenate([z(QK_NOPE), w_kpe, z(HEAD_PAD - QK_HEAD)], axis=1)
    w_lat = jnp.concatenate([part(0), part(1), kpe_slab], axis=1).astype(BF16)
    w_conv = w_in[:, offs[3]:offs[6]].astype(BF16)
    w_gate = w_in[:, offs[6]:offs[8]].astype(BF16)

    wq = w_uq.reshape(Q_LORA, N_HEADS, QK_HEAD)
    zq = lambda n: jnp.zeros((Q_LORA, N_HEADS, n), w_uq.dtype)
    wq2 = jnp.concatenate([wq, zq(HEAD_PAD - QK_HEAD)], axis=-1)
    wq2 = wq2.reshape(Q_LORA, N_HEADS * HEAD_PAD).astype(BF16)

    wkv = w_ukv.reshape(KV_LORA, N_HEADS, QK_NOPE + V_HEAD)
    wk2 = jnp.concatenate([wkv[..., :QK_NOPE],
                           jnp.zeros((KV_LORA, N_HEADS, HEAD_PAD - QK_NOPE), w_ukv.dtype)], axis=-1)
    wk2 = wk2.reshape(KV_LORA, N_HEADS * HEAD_PAD).astype(BF16)
    wv = jnp.concatenate([wkv[..., QK_NOPE:],
                          jnp.zeros((KV_LORA, N_HEADS, HEAD_PAD - V_HEAD), w_ukv.dtype)], axis=-1)
    wv = wv.reshape(KV_LORA, N_HEADS * HEAD_PAD).astype(BF16)
    return w_lat, w_conv, w_gate, wq2, wk2, wv


def _rope_freqs():
    inv_freq = 1.0 / (ROPE_THETA ** (jnp.arange(0, QK_ROPE, 2, dtype=F32) / QK_ROPE))
    return inv_freq.reshape(QK_ROPE // 2, 1)


def _dest_kernel(gm_ref, idx_ref, rank_ref, o_ref):
    idx = idx_ref[...]
    dest = rank_ref[...]
    for e in range(N_EXPERTS):
        dest = dest + jnp.where(idx == e, gm_ref[_GM_FIRST, e] * MOE_BLOCK, 0)
    o_ref[...] = dest


def _route(top_idx, rank, counts, n_tokens):
    blocks = (counts + MOE_BLOCK - 1) // MOE_BLOCK
    first_block = jnp.cumsum(blocks) - blocks
    table = jnp.stack([counts, first_block, blocks]).astype(jnp.int32)
    whole = pl.BlockSpec(top_idx.shape, lambda i, gm: (0, 0))
    dest = pl.pallas_call(
        _dest_kernel,
        out_shape=jax.ShapeDtypeStruct(rank.shape, jnp.int32),
        grid_spec=pltpu.PrefetchScalarGridSpec(
            num_scalar_prefetch=1, grid=(1,), in_specs=[whole, whole], out_specs=whole),
        name="row_destinations",
    )(table, top_idx, rank)[:TOP_K]
    n_rows = n_tokens * TOP_K + N_EXPERTS * MOE_BLOCK
    return dest, table, n_rows


SC_CORES = 2
SC_SUBCORES = 16
SC_WORKERS = SC_CORES * SC_SUBCORES
SC_CHUNK = 64
SC_GATHER_RING = 3

def _sc_mesh():
    return plsc.VectorSubcoreMesh(core_axis_name="c", subcore_axis_name="s")


def _sc_worker():
    return lax.axis_index("s") * SC_CORES + lax.axis_index("c")


def _dispatch(h2, dest, n_rows):
    t, d = h2.shape
    per_w = t // SC_WORKERS
    n_chunks = per_w // SC_CHUNK
    assert per_w % (2 * SC_CHUNK) == 0
    idx = dest.reshape(TOP_K, SC_WORKERS, n_chunks, SC_CHUNK).transpose(1, 0, 2, 3)
    idx = idx.reshape(SC_WORKERS, TOP_K * n_chunks, SC_CHUNK)

    @functools.partial(
        pl.kernel, mesh=_sc_mesh(),
        out_type=jax.ShapeDtypeStruct((n_rows, d), h2.dtype),
        scratch_types=[pltpu.VMEM((TOP_K * n_chunks, SC_CHUNK), jnp.int32),
                       pltpu.VMEM((2, SC_CHUNK, d), h2.dtype),
                       pltpu.SemaphoreType.DMA((2,)),
                       pltpu.SemaphoreType.DMA((2,))],
        name="moe_dispatch")
    def run(h2_hbm, idx_hbm, xs_hbm, idx_v, rows_v, rsem, ssem):
        w = _sc_worker()
        pltpu.sync_copy(idx_hbm.at[w], idx_v)

        def read(g, b):
            src = h2_hbm.at[pl.ds(w * per_w + g * SC_CHUNK, SC_CHUNK)]
            return pltpu.make_async_copy(src, rows_v.at[b], rsem.at[b])

        def scatter(g, kk, b):
            dst = xs_hbm.at[idx_v.at[kk * n_chunks + g]]
            return pltpu.make_async_copy(rows_v.at[b], dst, ssem.at[b])

        read(0, 0).start()

        @pl.loop(0, n_chunks, step=2)
        def _(g0):
            for b in range(2):
                g = g0 + b
                read(g, b).wait()

                @pl.when(g + 1 < n_chunks)
                def _():
                    read(g + 1, 1 - b).start()

                for kk in range(TOP_K):
                    scatter(g, kk, b).start()
                for kk in range(TOP_K):
                    scatter(g, kk, b).wait()

    return run(h2, idx)


def _undispatch(ys, dest):
    t = dest.shape[1]
    d = ys.shape[1]
    n_out = t * TOP_K
    per_w = n_out // SC_WORKERS
    n_chunks = per_w // SC_CHUNK
    idx = dest.reshape(SC_WORKERS, n_chunks, SC_CHUNK)
    ring = SC_GATHER_RING

    @functools.partial(
        pl.kernel, mesh=_sc_mesh(),
        out_type=jax.ShapeDtypeStruct((n_out, d), ys.dtype),
        scratch_types=[pltpu.VMEM((n_chunks, SC_CHUNK), jnp.int32),
                       pltpu.VMEM((ring, SC_CHUNK, d), ys.dtype),
                       pltpu.SemaphoreType.DMA((ring,)),
                       pltpu.SemaphoreType.DMA((ring,))],
        name="moe_undispatch")
    def run(ys_hbm, idx_hbm, out_hbm, idx_v, rows_v, gsem, wsem):
        w = _sc_worker()
        pltpu.sync_copy(idx_hbm.at[w], idx_v)

        def gather(g):
            b = g % ring
            return pltpu.make_async_copy(ys_hbm.at[idx_v.at[g]], rows_v.at[b], gsem.at[b])

        def write(g):
            b = g % ring
            dst = out_hbm.at[pl.ds(w * per_w + g * SC_CHUNK, SC_CHUNK)]
            return pltpu.make_async_copy(rows_v.at[b], dst, wsem.at[b])

        for g in range(min(ring - 1, n_chunks)):
            gather(g).start()
        for g in range(n_chunks):
            gather(g).wait()
            ahead = g + ring - 1
            if ahead < n_chunks:
                if g >= 1:
                    write(g - 1).wait()
                gather(ahead).start()
            write(g).start()
        for g in range(max(n_chunks - ring, 0), n_chunks):
            write(g).wait()

    return run(ys, idx).reshape(TOP_K, t, d)


def kernel(x, c, positions, w_ada, b_ada, norm_mix_g, w_in, q_norm_g, w_uq, kv_norm_g, w_ukv,
           w_up_attn, conv_w, w_up_conv, w_o, norm_ffn_g, router_w, router_b, w_gu, b_gu,
           w_down, b_down, norm_final_g):
    batch, seq, d = x.shape
    t = batch * seq
    depth = w_ada.shape[0]
    assert d == D_MODEL and batch <= 8 and conv_w.shape[1:] == (CONV_K, CONV_WIDTH)
    assert seq % ROW_TILE == 0 and seq % POST_TILE == 0 and seq % ATT_BLOCK == 0
    assert t % (2 * SC_CHUNK * SC_WORKERS) == 0
    x2 = x.reshape(t, d)
    pos = positions.astype(F32).reshape(t // ROW_TILE, 1, ROW_TILE)
    freqs = _rope_freqs()
    c_pad = jnp.zeros((8, d), F32).at[:batch].set(c)

    for l in range(depth):
        ada = _ada(c_pad, w_ada[l], b_ada[l].reshape(1, -1))
        mod = ada[:batch].reshape(batch, 6, d)
        mod = jnp.concatenate([mod, jnp.zeros((batch, 2, d), F32)], axis=1)

        w_lat, w_conv, w_gate, wq2, wk2, wv = _prep_weights(w_in[l], w_uq[l], w_ukv[l])
        q, k, v, sga, gc = _pre(x2, mod, norm_mix_g[l].reshape(1, d), w_lat, w_conv, w_gate,
                                q_norm_g[l].reshape(1, -1), wq2,
                                kv_norm_g[l].reshape(1, -1), wk2, wv, pos, freqs, conv_w[l],
                                w_up_conv[l].astype(BF16), seq)
        merged = _attention(q, k, v, sga, gc, w_up_attn[l], batch, seq)

        rw_pad = jnp.concatenate([router_w[l], jnp.zeros((d, LANES - N_EXPERTS), F32)], axis=1)
        rb_pad = jnp.concatenate([router_b[l], jnp.full((LANES - N_EXPERTS,), NEG_BIG, F32)])
        rw_hi = rw_pad.astype(BF16)
        rw_lo = (rw_pad - rw_hi.astype(F32)).astype(BF16)
        x1, h2, idx_pad, gate_pad, rank_pad, counts = _post(
            merged, x2, mod, w_o[l], norm_ffn_g[l].reshape(1, d),
            jnp.concatenate([rw_hi, rw_lo], axis=1), rb_pad.reshape(1, LANES), seq)

        dest, group_table, n_rows = _route(
            idx_pad, rank_pad, counts[0, :N_EXPERTS], t)
        xs = _dispatch(h2, dest, n_rows)
        ys = _moe(group_table, xs, w_gu[l], b_gu[l].reshape(N_EXPERTS, 1, -1),
                  w_down[l], b_down[l].reshape(N_EXPERTS, 1, -1))
        y_kt = _undispatch(ys, dest)
        x2 = _final(x1, y_kt, gate_pad, mod, norm_final_g.reshape(1, d), seq, l == depth - 1)

    return x2.reshape(batch, seq, d)
```

```python
import functools
import math

import jax
import jax.numpy as jnp
from jax import lax
from jax.experimental import pallas as pl
from jax.experimental.pallas import tpu as pltpu
from jax.experimental.pallas import tpu_sc as plsc

D_MODEL = 1024
CHUNK = 64
N_HEADS = 8
Q_LORA = 256
KV_LORA = 128
QK_NOPE = 64
QK_ROPE = 32
V_HEAD = 64
QK_HEAD = QK_NOPE + QK_ROPE
ROPE_THETA = 10000.0
CONV_WIDTH = 512
CONV_K = 3
N_EXPERTS = 32
TOP_K = 4
D_EXPERT = 1024
SWIGLU_LIMIT = 7.0
SWIGLU_ALPHA = 1.702
MOE_BLOCK = 256
RMS_EPS = 1e-6

LANES = 128
HEAD_PAD = 128
NEG_BIG = -1e30
VMEM_LIMIT = 56 * 1024 * 1024

F32 = jnp.float32
BF16 = jnp.bfloat16

Q_PRESCALE = (QK_HEAD ** -0.5) * math.log2(math.e)

ADA_TILE = 1024
ROW_TILE = 1024
MOE_STEP_BLOCKS = 4
POST_TILE = 1024
POST_SUB = 512
ATT_BLOCK = 512
ATT_WIDE = 2


def _rms(x, g):
    ms = jnp.mean(x * x, axis=-1, keepdims=True)
    return x * lax.rsqrt(ms + RMS_EPS) * g


def _dot(a, b):
    return jnp.dot(a, b, preferred_element_type=F32)


PACKED = D_MODEL // 2


def _pack_row(x):
    return pltpu.pack_elementwise([x[:, :PACKED], x[:, PACKED:]], packed_dtype=BF16)


def _unpack_row(w):
    half = lambda i: pltpu.unpack_elementwise(w, index=i, packed_dtype=BF16, unpacked_dtype=F32)
    return jnp.concatenate([half(0), half(1)], axis=-1)


def _ada_kernel(c_ref, w_ref, b_ref, o_ref):
    c = c_ref[...]
    ca = (c * jax.nn.sigmoid(c)).astype(BF16)
    o_ref[...] = _dot(ca, w_ref[...].astype(BF16)) + b_ref[...]


def _ada(c_pad, w_ada, b_ada):
    n = w_ada.shape[1]
    tn = ADA_TILE
    return pl.pallas_call(
        _ada_kernel,
        out_shape=jax.ShapeDtypeStruct((c_pad.shape[0], n), F32),
        grid=(n // tn,),
        in_specs=[
            pl.BlockSpec(c_pad.shape, lambda j: (0, 0)),
            pl.BlockSpec((D_MODEL, tn), lambda j: (0, j)),
            pl.BlockSpec((1, tn), lambda j: (0, j)),
        ],
        out_specs=pl.BlockSpec((c_pad.shape[0], tn), lambda j: (0, j)),
        compiler_params=pltpu.CompilerParams(
            dimension_semantics=("arbitrary",), vmem_limit_bytes=VMEM_LIMIT),
        name="ada",
    )(c_pad, w_ada, b_ada)


_C_QLAT = 0
_C_KVLAT = _C_QLAT + Q_LORA
_C_KPE = _C_KVLAT + KV_LORA
_C_END = _C_KPE + HEAD_PAD


def _pre_kernel(tiles_per_seq, x_ref, mod_ref, g_ref, wlat_ref, wconv_ref, wgate_ref, qg_ref,
                wq_ref, kvg_ref, wk_ref, wv_ref, pos_ref, freq_ref, cw_ref, wuc_ref,
                q_ref, k_ref, v_ref, sga_ref, gc_ref, carry_ref):
    i = pl.program_id(0)
    tm = x_ref.shape[0]
    mod = mod_ref[...]
    h = _rms(x_ref[...], g_ref[...]) * (1.0 + mod[1:2]) + mod[0:1]
    hb = h.astype(BF16)

    ang = freq_ref[...] * pos_ref[...]
    cos_t, sin_t = jnp.cos(ang), jnp.sin(ang)
    ones_t = jnp.ones((QK_NOPE, tm), F32)
    zeros_t = jnp.zeros((QK_NOPE, tm), F32)
    pad_t = jnp.zeros((HEAD_PAD - QK_HEAD, tm), F32)
    cosf = jnp.concatenate([ones_t, cos_t, cos_t, pad_t], axis=0).T
    sinf = jnp.concatenate([zeros_t, -sin_t, sin_t, pad_t], axis=0).T

    first_half = lax.broadcasted_iota(jnp.int32, (tm, HEAD_PAD), 1) < QK_NOPE + QK_ROPE // 2

    def rope(slab):
        swapped = jnp.where(first_half, pltpu.roll(slab, HEAD_PAD - QK_ROPE // 2, 1),
                            pltpu.roll(slab, QK_ROPE // 2, 1))
        return slab * cosf + swapped * sinf

    small = _dot(hb, wlat_ref[...])
    q_lat = small[:, _C_QLAT:_C_KVLAT]
    kv_lat = small[:, _C_KVLAT:_C_KPE]
    kpe = rope(small[:, _C_KPE:_C_END])
    qn = _rms(q_lat, qg_ref[...]).astype(BF16)
    q = _dot(qn, wq_ref[...])
    q = jnp.concatenate([rope(q[:, hd * HEAD_PAD:(hd + 1) * HEAD_PAD]) for hd in range(N_HEADS)],
                        axis=-1)
    q_ref[...] = (q * Q_PRESCALE).astype(BF16)
    kvn = _rms(kv_lat, kvg_ref[...]).astype(BF16)
    k = _dot(kvn, wk_ref[...]) + jnp.concatenate([kpe] * N_HEADS, axis=-1)
    k_ref[...] = k.astype(BF16)
    lane = lax.broadcasted_iota(jnp.int32, (tm, N_HEADS * HEAD_PAD), 1)
    ones_col = jnp.where(lane % HEAD_PAD == V_HEAD, 1.0, 0.0)
    v_ref[...] = (_dot(kvn, wv_ref[...]) + ones_col).astype(BF16)

    ucb = _dot(hb, wconv_ref[...])
    cu = ucb[:, 0:CONV_WIDTH] * ucb[:, CONV_WIDTH:2 * CONV_WIDTH]
    b_gate = ucb[:, 2 * CONV_WIDTH:3 * CONV_WIDTH]

    @pl.when(i % tiles_per_seq == 0)
    def _():
        carry_ref[...] = jnp.zeros_like(carry_ref)

    prev = carry_ref[...]
    row = lax.broadcasted_iota(jnp.int32, cu.shape, 0)
    cu1 = jnp.where(row == 0, prev[7:8], pltpu.roll(cu, 1, 0))
    cu2 = jnp.where(row == 0, prev[6:7], jnp.where(row == 1, prev[7:8], pltpu.roll(cu, 2, 0)))
    cw = cw_ref[...]
    z = cw[2:3] * cu + cw[1:2] * cu1 + cw[0:1] * cu2
    carry_ref[...] = cu[tm - 8:tm]
    c_branch = _dot((b_gate * z).astype(BF16), wuc_ref[...])

    gates = _dot(hb, wgate_ref[...])
    sga_ref[...] = jax.nn.sigmoid(gates[:, 0:D_MODEL]).astype(BF16)
    gc_ref[...] = (jax.nn.sigmoid(gates[:, D_MODEL:]) * c_branch).astype(BF16)


def _pre(x2, mod, norm_g, w_lat, w_conv, w_gate, q_norm_g, wq2, kv_norm_g, wk2, wv, pos,
         freqs, conv_w, w_up_conv, seq):
    t = x2.shape[0]
    tm = ROW_TILE
    tiles_per_seq = seq // tm
    full = lambda a: pl.BlockSpec(a.shape, lambda i: (0,) * a.ndim)
    rows = lambda w: pl.BlockSpec((tm, w), lambda i: (i, 0))
    outs = [jax.ShapeDtypeStruct((t, N_HEADS * HEAD_PAD), BF16),
            jax.ShapeDtypeStruct((t, N_HEADS * HEAD_PAD), BF16),
            jax.ShapeDtypeStruct((t, N_HEADS * HEAD_PAD), BF16),
            jax.ShapeDtypeStruct((t, D_MODEL), BF16),
            jax.ShapeDtypeStruct((t, D_MODEL), BF16)]
    return pl.pallas_call(
        functools.partial(_pre_kernel, tiles_per_seq),
        out_shape=outs,
        grid=(t // tm,),
        in_specs=[
            rows(D_MODEL),
            pl.BlockSpec((None, 8, D_MODEL), lambda i: (i // tiles_per_seq, 0, 0)),
            full(norm_g), full(w_lat), full(w_conv), full(w_gate), full(q_norm_g), full(wq2),
            full(kv_norm_g), full(wk2), full(wv),
            pl.BlockSpec((None, 1, tm), lambda i: (i, 0, 0)), full(freqs), full(conv_w),
            full(w_up_conv),
        ],
        out_specs=[rows(N_HEADS * HEAD_PAD), rows(N_HEADS * HEAD_PAD), rows(N_HEADS * HEAD_PAD),
                   rows(D_MODEL), rows(D_MODEL)],
        scratch_shapes=[pltpu.VMEM((8, CONV_WIDTH), F32)],
        compiler_params=pltpu.CompilerParams(
            dimension_semantics=("arbitrary",), vmem_limit_bytes=VMEM_LIMIT),
        name="pre_mixer",
    )(x2, mod, norm_g, w_lat, w_conv, w_gate, q_norm_g, wq2, kv_norm_g, wk2, wv, pos,
      freqs, conv_w, w_up_conv)


def _attn_kernel(q_ref, k_ref, v_ref, sga_ref, gc_ref, wua_ref, o_ref, m_ref, acc_ref, wua_bf):
    i = pl.program_id(1)
    tq = q_ref.shape[0]

    @pl.when(jnp.logical_and(pl.program_id(0) == 0, i == 0))
    def _():
        wua_bf[...] = wua_ref[...].astype(BF16)

    def step(k0, tk, masked, first=False):
        if masked:
            rq = (lax.broadcasted_iota(jnp.int32, (tq, tk), 0) + (tk - tq)) // CHUNK
            ck = lax.broadcasted_iota(jnp.int32, (tq, tk), 1) // CHUNK
            allowed = ck <= rq
        for hd in range(N_HEADS):
            hs = slice(hd * HEAD_PAD, (hd + 1) * HEAD_PAD)
            s = lax.dot_general(q_ref[:, hs], k_ref[pl.ds(k0, tk), hs],
                                (((1,), (1,)), ((), ())), preferred_element_type=F32)
            if masked:
                s = jnp.where(allowed, s, NEG_BIG)
            s_max = s[:, 0:LANES]
            for c in range(1, tk // LANES):
                s_max = jnp.maximum(s_max, s[:, c * LANES:(c + 1) * LANES])
            m_new = jnp.broadcast_to(jnp.max(s_max, axis=-1, keepdims=True), (tq, LANES))
            if not first:
                m_old = m_ref[hd]
                m_new = jnp.maximum(m_old, m_new)
            p = jnp.concatenate(
                [jnp.exp2(s[:, c * LANES:(c + 1) * LANES] - m_new).astype(BF16)
                 for c in range(tk // LANES)], axis=-1)
            pv = _dot(p, v_ref[pl.ds(k0, tk), hs])
            acc_ref[hd] = pv if first else jnp.exp2(m_old - m_new) * acc_ref[hd] + pv
            m_ref[hd] = m_new

    wide = ATT_WIDE * tq
    n_wide = i // ATT_WIDE

    @pl.when(n_wide > 0)
    def _():
        step(0, wide, False, first=True)

    @pl.when(n_wide == 0)
    def _():
        m_ref[...] = jnp.full_like(m_ref, NEG_BIG)
        acc_ref[...] = jnp.zeros_like(acc_ref)

    def body(j, carry):
        step(pl.multiple_of(j * wide, wide), wide, False)
        return carry

    lax.fori_loop(1, n_wide, body, 0)

    for r in range(ATT_WIDE):
        @pl.when(i % ATT_WIDE == r)
        def _():
            step(pl.multiple_of((i - r) * tq, tq), (r + 1) * tq, True)

    heads = []
    for hd in range(N_HEADS):
        acc = acc_ref[hd]
        heads.append((acc[:, 0:V_HEAD] / acc[:, V_HEAD:V_HEAD + 1]).astype(BF16))
    a_branch = _dot(jnp.concatenate(heads, axis=-1), wua_bf[...])
    o_ref[...] = (sga_ref[...].astype(F32) * a_branch + gc_ref[...].astype(F32)).astype(BF16)


def _attention(q, k, v, sga, gc, wua, batch, seq):
    tq = ATT_BLOCK
    nq = seq // tq
    q_rows = lambda w: pl.BlockSpec((tq, w), lambda b, i: (b * nq + i, 0))
    whole_seq = pl.BlockSpec((seq, N_HEADS * HEAD_PAD), lambda b, i: (b, 0))
    return pl.pallas_call(
        _attn_kernel,
        out_shape=jax.ShapeDtypeStruct((batch * seq, D_MODEL), BF16),
        grid=(batch, nq),
        in_specs=[q_rows(N_HEADS * HEAD_PAD), whole_seq, whole_seq, q_rows(D_MODEL),
                  q_rows(D_MODEL), pl.BlockSpec(wua.shape, lambda b, i: (0, 0))],
        out_specs=q_rows(D_MODEL),
        scratch_shapes=[pltpu.VMEM((N_HEADS, tq, LANES), F32),
                        pltpu.VMEM((N_HEADS, tq, LANES), F32),
                        pltpu.VMEM(wua.shape, BF16)],
        compiler_params=pltpu.CompilerParams(
            dimension_semantics=("arbitrary", "arbitrary"), vmem_limit_bytes=VMEM_LIMIT),
        name="attention",
    )(q, k, v, sga, gc, wua)


def _post_kernel(merged_ref, x_ref, mod_ref, wo_ref, g_ref, rw_ref, rb_ref,
                 x1_ref, h2_ref, idx_ref, gate_ref, rank_ref, cnt_out_ref, cnt_ref, lg_ref,
                 wo_bf):
    i = pl.program_id(0)

    @pl.when(i == 0)
    def _():
        cnt_ref[...] = jnp.zeros_like(cnt_ref)
        lg_ref[...] = jnp.zeros_like(lg_ref)
        wo_bf[...] = wo_ref[...].astype(BF16)

    prev_logits = lg_ref[(i + 1) % 2]
    counts = cnt_ref[...]
    routed = counts
    for r0 in range(0, x_ref.shape[0], POST_SUB):
        rs = slice(r0, r0 + POST_SUB)
        lg_ref[i % 2, rs, :] = _post_mix(rs, merged_ref, x_ref, mod_ref, wo_bf, g_ref, rw_ref,
                                         rb_ref, x1_ref, h2_ref)
        routed = _post_route(rs, prev_logits[rs, :], routed, idx_ref, gate_ref, rank_ref)
    counts = jnp.where(i > 0, routed, counts)
    cnt_ref[...] = counts
    cnt_out_ref[...] = counts.astype(jnp.int32)


def _post_mix(rs, merged_ref, x_ref, mod_ref, wo_ref, g_ref, rw_ref, rb_ref, x1_ref, h2_ref):
    mod = mod_ref[...]
    mix = _dot(merged_ref[rs, :], wo_ref[...])
    x1 = x_ref[rs, :] + mod[2:3] * mix
    x1_ref[rs, :] = x1
    h2 = _rms(x1, g_ref[...]) * (1.0 + mod[4:5]) + mod[3:4]
    h2_ref[rs, :] = _pack_row(h2)

    both = _dot(h2.astype(BF16), rw_ref[...])
    return both[:, :LANES] + both[:, LANES:] + rb_ref[...]


def _post_route(rs, logits, counts, idx_ref, gate_ref, rank_ref):
    lane = lax.broadcasted_iota(jnp.int32, logits.shape, 1)
    work = logits
    vals, idxs = [], []
    for _ in range(TOP_K):
        mk = jnp.max(work, axis=-1, keepdims=True)
        ik = jnp.argmax(work, axis=-1, keepdims=True).astype(jnp.int32)
        vals.append(mk)
        idxs.append(ik)
        work = jnp.where(lane == ik, -jnp.inf, work)
    es = [jnp.exp(vk - vals[0]) for vk in vals]
    denom = es[0] + es[1] + es[2] + es[3]
    tm = logits.shape[0]
    chosen = jnp.zeros(logits.shape, F32)
    for kk in range(TOP_K):
        chosen = chosen + jnp.where(lane == idxs[kk], 1.0, 0.0)
    r_i = lax.broadcasted_iota(jnp.int32, (tm, tm), 0)
    c_i = lax.broadcasted_iota(jnp.int32, (tm, tm), 1)
    earlier = jnp.where(c_i < r_i, 1.0, 0.0).astype(BF16)
    before = _dot(earlier, chosen.astype(BF16)) + counts[0:1]

    idx_out = jnp.zeros(logits.shape, F32)
    gate_out = jnp.zeros(logits.shape, F32)
    rank_out = jnp.zeros(logits.shape, F32)
    for kk in range(TOP_K):
        rank_k = jnp.sum(jnp.where(lane == idxs[kk], before, 0.0), axis=-1, keepdims=True)
        idx_out = jnp.where(lane == kk, idxs[kk].astype(F32), idx_out)
        gate_out = jnp.where(lane == kk, es[kk] / denom, gate_out)
        rank_out = jnp.where(lane == kk, rank_k, rank_out)
    gate_ref[rs, :] = gate_out
    idx_ref[:, rs] = idx_out.T[0:8].astype(jnp.int32)
    rank_ref[:, rs] = rank_out.T[0:8].astype(jnp.int32)
    return counts + jnp.sum(chosen, axis=0, keepdims=True)


def _post(merged, x2, mod, wo, norm_g, rw_hl, rb_pad, seq):
    t = x2.shape[0]
    tm = POST_TILE
    tiles_per_seq = seq // tm
    n_tiles = t // tm
    full = lambda a: pl.BlockSpec(a.shape, lambda i: (0,) * a.ndim)
    mix_tile = lambda i: jnp.minimum(i, n_tiles - 1)
    route_tile = lambda i: jnp.maximum(i - 1, 0)
    rows = lambda w: pl.BlockSpec((tm, w), lambda i: (mix_tile(i), 0))
    outs = [jax.ShapeDtypeStruct((t, D_MODEL), F32),
            jax.ShapeDtypeStruct((t, PACKED), jnp.uint32),
            jax.ShapeDtypeStruct((8, t), jnp.int32),
            jax.ShapeDtypeStruct((t, LANES), F32),
            jax.ShapeDtypeStruct((8, t), jnp.int32),
            jax.ShapeDtypeStruct((8, LANES), jnp.int32)]
    slots = pl.BlockSpec((8, tm), lambda i: (0, route_tile(i)))
    return pl.pallas_call(
        _post_kernel,
        out_shape=outs,
        grid=(n_tiles + 1,),
        in_specs=[
            rows(D_MODEL), rows(D_MODEL),
            pl.BlockSpec((None, 8, D_MODEL), lambda i: (mix_tile(i) // tiles_per_seq, 0, 0)),
            full(wo), full(norm_g), full(rw_hl), full(rb_pad),
        ],
        out_specs=[rows(D_MODEL), rows(PACKED), slots,
                   pl.BlockSpec((tm, LANES), lambda i: (route_tile(i), 0)), slots,
                   pl.BlockSpec((8, LANES), lambda i: (0, 0))],
        scratch_shapes=[pltpu.VMEM((8, LANES), F32), pltpu.VMEM((2, tm, LANES), F32),
                        pltpu.VMEM(wo.shape, BF16)],
        compiler_params=pltpu.CompilerParams(
            dimension_semantics=("arbitrary",), vmem_limit_bytes=VMEM_LIMIT),
        name="post_mixer",
    )(merged, x2, mod, wo, norm_g, rw_hl, rb_pad)


_GM_COUNT, _GM_FIRST, _GM_BLOCKS = range(3)
_ST_EXPERT, _ST_SLOT = range(2)


def _moe_kernel(gm_ref, xs_ref, wgu_hbm, bgu_ref, wd_hbm, bd_ref, o_ref,
                wgu_f, wd_f, wgu_bf, wd_bf, sem, st_ref):
    weights = (wgu_hbm, wd_hbm, wgu_f, wd_f, wgu_bf, wd_bf, sem)
    rows = (xs_ref, bgu_ref, bd_ref, o_ref, wgu_bf, wd_bf)
    b0 = pl.program_id(0) * MOE_STEP_BLOCKS
    e, in_group, used = _moe_enter(b0, gm_ref, st_ref, *weights)
    together = jnp.logical_and(used, in_group + MOE_STEP_BLOCKS <= gm_ref[_GM_BLOCKS, e])

    @pl.when(together)
    def _():
        _moe_rows(slice(0, MOE_STEP_BLOCKS * MOE_BLOCK), e, in_group, True, gm_ref, *rows)

    @pl.when(jnp.logical_not(together))
    def _():
        for r in range(0, MOE_STEP_BLOCKS, 2):
            if r == 0:
                e_r, in_group_r, used_r = e, in_group, used
            else:
                e_r, in_group_r, used_r = _moe_enter(b0 + r, gm_ref, st_ref, *weights)
            paired = jnp.logical_and(used_r, in_group_r + 2 <= gm_ref[_GM_BLOCKS, e_r])

            @pl.when(paired)
            def _():
                _moe_rows(slice(r * MOE_BLOCK, (r + 2) * MOE_BLOCK), e_r, in_group_r, True,
                          gm_ref, *rows)

            @pl.when(jnp.logical_not(paired))
            def _():
                _moe_rows(slice(r * MOE_BLOCK, (r + 1) * MOE_BLOCK), e_r, in_group_r, used_r,
                          gm_ref, *rows)
                e_n, in_group_n, used_n = _moe_enter(b0 + r + 1, gm_ref, st_ref, *weights)
                _moe_rows(slice((r + 1) * MOE_BLOCK, (r + 2) * MOE_BLOCK), e_n, in_group_n,
                          used_n, gm_ref, *rows)


def _moe_enter(b, gm_ref, st_ref, wgu_hbm, wd_hbm, wgu_f, wd_f, wgu_bf, wd_bf, sem):
    def weight_copies(expert, sl):
        return (pltpu.make_async_copy(wgu_hbm.at[expert], wgu_f.at[sl], sem.at[0, sl]),
                pltpu.make_async_copy(wd_hbm.at[expert], wd_f.at[sl], sem.at[1, sl]))

    def next_group(e):
        return lax.while_loop(
            lambda k: jnp.logical_and(k < N_EXPERTS,
                                      gm_ref[_GM_BLOCKS, jnp.minimum(k, N_EXPERTS - 1)] == 0),
            lambda k: k + 1, e)

    @pl.when(b == 0)
    def _():
        e0 = next_group(0)
        st_ref[_ST_EXPERT] = e0
        st_ref[_ST_SLOT] = 1
        for cp in weight_copies(e0, 0):
            cp.start()

    e_prev = st_ref[_ST_EXPERT]
    past = b >= gm_ref[_GM_FIRST, e_prev] + gm_ref[_GM_BLOCKS, e_prev]
    e = jnp.minimum(jnp.where(past, next_group(e_prev + 1), e_prev), N_EXPERTS - 1)
    st_ref[_ST_EXPERT] = e
    in_group = b - gm_ref[_GM_FIRST, e]
    used = jnp.logical_and(in_group >= 0, in_group < gm_ref[_GM_BLOCKS, e])

    @pl.when(jnp.logical_and(used, in_group == 0))
    def _():
        slot = 1 - st_ref[_ST_SLOT]
        st_ref[_ST_SLOT] = slot
        nxt = next_group(e + 1)

        @pl.when(nxt < N_EXPERTS)
        def _():
            for cp in weight_copies(nxt, 1 - slot):
                cp.start(priority=1)

        for cp in weight_copies(e, slot):
            cp.wait()
        wgu_bf[...] = wgu_f[slot].astype(BF16)
        wd_bf[...] = wd_f[slot].astype(BF16)

    return e, in_group, used


def _moe_rows(rs, e, in_group, used, gm_ref, xs_ref, bgu_ref, bd_ref, o_ref, wgu_bf, wd_bf):
    n_valid = gm_ref[_GM_COUNT, e] - in_group * MOE_BLOCK

    def ffn(r):
        n = r.stop - r.start
        row = lax.broadcasted_iota(jnp.int32, (n, PACKED), 0)
        xs = _unpack_row(jnp.where(row < n_valid, xs_ref[r, :], 0)).astype(BF16)
        gu = _dot(xs, wgu_bf[...]) + bgu_ref[e]
        gate = jnp.minimum(gu[:, :D_EXPERT], SWIGLU_LIMIT)
        up = jnp.clip(gu[:, D_EXPERT:], -SWIGLU_LIMIT, SWIGLU_LIMIT)
        act = (up + 1.0) * (gate * jax.nn.sigmoid(gate * SWIGLU_ALPHA))
        o_ref[r, :] = _pack_row(_dot(act.astype(BF16), wd_bf[...]) + bd_ref[e])

    def zeros(r):
        o_ref[r, :] = jnp.zeros((r.stop - r.start, PACKED), o_ref.dtype)

    if used is True:
        ffn(rs)
        return

    mid = rs.start + (rs.stop - rs.start) // 2
    short = n_valid <= mid - rs.start

    @pl.when(jnp.logical_and(used, jnp.logical_not(short)))
    def _():
        ffn(rs)

    @pl.when(jnp.logical_and(used, short))
    def _():
        ffn(slice(rs.start, mid))
        zeros(slice(mid, rs.stop))

    @pl.when(jnp.logical_not(used))
    def _():
        zeros(rs)


def _moe(group_table, xs, w_gu, b_gu, w_down, b_down):
    n_rows = xs.shape[0]
    step_rows = MOE_STEP_BLOCKS * MOE_BLOCK
    assert n_rows % step_rows == 0
    grid_spec = pltpu.PrefetchScalarGridSpec(
        num_scalar_prefetch=1,
        grid=(n_rows // step_rows,),
        in_specs=[
            pl.BlockSpec((step_rows, PACKED), lambda b, gm: (b, 0)),
            pl.BlockSpec(memory_space=pl.ANY),
            pl.BlockSpec(b_gu.shape, lambda b, gm: (0, 0, 0)),
            pl.BlockSpec(memory_space=pl.ANY),
            pl.BlockSpec(b_down.shape, lambda b, gm: (0, 0, 0)),
        ],
        out_specs=pl.BlockSpec((step_rows, PACKED), lambda b, gm: (b, 0)),
        scratch_shapes=[pltpu.VMEM((2, D_MODEL, 2 * D_EXPERT), F32),
                        pltpu.VMEM((2, D_EXPERT, D_MODEL), F32),
                        pltpu.VMEM((D_MODEL, 2 * D_EXPERT), BF16),
                        pltpu.VMEM((D_EXPERT, D_MODEL), BF16),
                        pltpu.SemaphoreType.DMA((2, 2)),
                        pltpu.SMEM((2,), jnp.int32)],
    )
    return pl.pallas_call(
        _moe_kernel,
        out_shape=jax.ShapeDtypeStruct((n_rows, PACKED), jnp.uint32),
        grid_spec=grid_spec,
        compiler_params=pltpu.CompilerParams(
            dimension_semantics=("arbitrary",), vmem_limit_bytes=VMEM_LIMIT),
        name="moe_experts",
    )(group_table, xs, w_gu, b_gu, w_down, b_down)


def _final_kernel(last_layer, x1_ref, y_ref, gate_ref, mod_ref, g_ref, o_ref):
    mod = mod_ref[...]
    gate = gate_ref[...]
    ffn = gate[:, 0:1] * _unpack_row(y_ref[0])
    for kk in range(1, TOP_K):
        ffn = ffn + gate[:, kk:kk + 1] * _unpack_row(y_ref[kk])
    x = x1_ref[...] + mod[5:6] * ffn
    o_ref[...] = _rms(x, g_ref[...]) if last_layer else x


def _final(x1, y_kt, gate, mod, norm_g, seq, last_layer):
    t = x1.shape[0]
    tm = ROW_TILE
    tiles_per_seq = seq // tm
    rows = lambda w: pl.BlockSpec((tm, w), lambda i: (i, 0))
    return pl.pallas_call(
        functools.partial(_final_kernel, last_layer),
        out_shape=jax.ShapeDtypeStruct((t, D_MODEL), F32),
        grid=(t // tm,),
        in_specs=[
            rows(D_MODEL), pl.BlockSpec((TOP_K, tm, PACKED), lambda i: (0, i, 0)), rows(LANES),
            pl.BlockSpec((None, 8, D_MODEL), lambda i: (i // tiles_per_seq, 0, 0)),
            pl.BlockSpec(norm_g.shape, lambda i: (0, 0)),
        ],
        out_specs=rows(D_MODEL),
        compiler_params=pltpu.CompilerParams(
            dimension_semantics=("arbitrary",), vmem_limit_bytes=VMEM_LIMIT),
        name="combine_final",
    )(x1, y_kt, gate, mod, norm_g)


def _prep_weights(w_in, w_uq, w_ukv):
    d = w_in.shape[0]
    splits = (Q_LORA, KV_LORA, QK_ROPE, CONV_WIDTH, CONV_WIDTH, CONV_WIDTH, D_MODEL, D_MODEL)
    offs = [0]
    for s in splits:
        offs.append(offs[-1] + s)
    part = lambda n: w_in[:, offs[n]:offs[n + 1]]
    z = lambda n: jnp.zeros((d, n), w_in.dtype)
    w_kpe = part(2)
    kpe_slab = jnp.concatenate([z(QK_NOPE), w_kpe, z(HEAD_PAD - QK_HEAD)], axis=1)
    w_lat = jnp.concatenate([part(0), part(1), kpe_slab], axis=1).astype(BF16)
    w_conv = w_in[:, offs[3]:offs[6]].astype(BF16)
    w_gate = w_in[:, offs[6]:offs[8]].astype(BF16)

    wq = w_uq.reshape(Q_LORA, N_HEADS, QK_HEAD)
    zq = lambda n: jnp.zeros((Q_LORA, N_HEADS, n), w_uq.dtype)
    wq2 = jnp.concatenate([wq, zq(HEAD_PAD - QK_HEAD)], axis=-1)
    wq2 = wq2.reshape(Q_LORA, N_HEADS * HEAD_PAD).astype(BF16)

    wkv = w_ukv.reshape(KV_LORA, N_HEADS, QK_NOPE + V_HEAD)
    wk2 = jnp.concatenate([wkv[..., :QK_NOPE],
                           jnp.zeros((KV_LORA, N_HEADS, HEAD_PAD - QK_NOPE), w_ukv.dtype)], axis=-1)
    wk2 = wk2.reshape(KV_LORA, N_HEADS * HEAD_PAD).astype(BF16)
    wv = jnp.concatenate([wkv[..., QK_NOPE:],
                          jnp.zeros((KV_LORA, N_HEADS, HEAD_PAD - V_HEAD), w_ukv.dtype)], axis=-1)
    wv = wv.reshape(KV_LORA, N_HEADS * HEAD_PAD).astype(BF16)
    return w_lat, w_conv, w_gate, wq2, wk2, wv


def _rope_freqs():
    inv_freq = 1.0 / (ROPE_THETA ** (jnp.arange(0, QK_ROPE, 2, dtype=F32) / QK_ROPE))
    return inv_freq.reshape(QK_ROPE // 2, 1)


def _dest_kernel(gm_ref, idx_ref, rank_ref, o_ref):
    idx = idx_ref[...]
    dest = rank_ref[...]
    for e in range(N_EXPERTS):
        dest = dest + jnp.where(idx == e, gm_ref[_GM_FIRST, e] * MOE_BLOCK, 0)
    o_ref[...] = dest


def _route(top_idx, rank, counts, n_tokens):
    blocks = (counts + MOE_BLOCK - 1) // MOE_BLOCK
    first_block = jnp.cumsum(blocks) - blocks
    table = jnp.stack([counts, first_block, blocks]).astype(jnp.int32)
    whole = pl.BlockSpec(top_idx.shape, lambda i, gm: (0, 0))
    dest = pl.pallas_call(
        _dest_kernel,
        out_shape=jax.ShapeDtypeStruct(rank.shape, jnp.int32),
        grid_spec=pltpu.PrefetchScalarGridSpec(
            num_scalar_prefetch=1, grid=(1,), in_specs=[whole, whole], out_specs=whole),
        name="row_destinations",
    )(table, top_idx, rank)[:TOP_K]
    n_rows = n_tokens * TOP_K + N_EXPERTS * MOE_BLOCK
    return dest, table, n_rows


SC_CORES = 2
SC_SUBCORES = 16
SC_WORKERS = SC_CORES * SC_SUBCORES
SC_CHUNK = 64
SC_GATHER_RING = 3

def _sc_mesh():
    return plsc.VectorSubcoreMesh(core_axis_name="c", subcore_axis_name="s")


def _sc_worker():
    return lax.axis_index("s") * SC_CORES + lax.axis_index("c")


def _dispatch(h2, dest, n_rows):
    t, d = h2.shape
    per_w = t // SC_WORKERS
    n_chunks = per_w // SC_CHUNK
    assert per_w % (2 * SC_CHUNK) == 0
    idx = dest.reshape(TOP_K, SC_WORKERS, n_chunks, SC_CHUNK).transpose(1, 0, 2, 3)
    idx = idx.reshape(SC_WORKERS, TOP_K * n_chunks, SC_CHUNK)

    @functools.partial(
        pl.kernel, mesh=_sc_mesh(),
        out_type=jax.ShapeDtypeStruct((n_rows, d), h2.dtype),
        scratch_types=[pltpu.VMEM((TOP_K * n_chunks, SC_CHUNK), jnp.int32),
                       pltpu.VMEM((2, SC_CHUNK, d), h2.dtype),
                       pltpu.SemaphoreType.DMA((2,)),
                       pltpu.SemaphoreType.DMA((2,))],
        name="moe_dispatch")
    def run(h2_hbm, idx_hbm, xs_hbm, idx_v, rows_v, rsem, ssem):
        w = _sc_worker()
        pltpu.sync_copy(idx_hbm.at[w], idx_v)

        def read(g, b):
            src = h2_hbm.at[pl.ds(w * per_w + g * SC_CHUNK, SC_CHUNK)]
            return pltpu.make_async_copy(src, rows_v.at[b], rsem.at[b])

        def scatter(g, kk, b):
            dst = xs_hbm.at[idx_v.at[kk * n_chunks + g]]
            return pltpu.make_async_copy(rows_v.at[b], dst, ssem.at[b])

        read(0, 0).start()

        @pl.loop(0, n_chunks, step=2)
        def _(g0):
            for b in range(2):
                g = g0 + b
                read(g, b).wait()

                @pl.when(g + 1 < n_chunks)
                def _():
                    read(g + 1, 1 - b).start()

                for kk in range(TOP_K):
                    scatter(g, kk, b).start()
                for kk in range(TOP_K):
                    scatter(g, kk, b).wait()

    return run(h2, idx)


def _undispatch(ys, dest):
    t = dest.shape[1]
    d = ys.shape[1]
    n_out = t * TOP_K
    per_w = n_out // SC_WORKERS
    n_chunks = per_w // SC_CHUNK
    idx = dest.reshape(SC_WORKERS, n_chunks, SC_CHUNK)
    ring = SC_GATHER_RING

    @functools.partial(
        pl.kernel, mesh=_sc_mesh(),
        out_type=jax.ShapeDtypeStruct((n_out, d), ys.dtype),
        scratch_types=[pltpu.VMEM((n_chunks, SC_CHUNK), jnp.int32),
                       pltpu.VMEM((ring, SC_CHUNK, d), ys.dtype),
                       pltpu.SemaphoreType.DMA((ring,)),
                       pltpu.SemaphoreType.DMA((ring,))],
        name="moe_undispatch")
    def run(ys_hbm, idx_hbm, out_hbm, idx_v, rows_v, gsem, wsem):
        w = _sc_worker()
        pltpu.sync_copy(idx_hbm.at[w], idx_v)

        def gather(g):
            b = g % ring
            return pltpu.make_async_copy(ys_hbm.at[idx_v.at[g]], rows_v.at[b], gsem.at[b])

        def write(g):
            b = g % ring
            dst = out_hbm.at[pl.ds(w * per_w + g * SC_CHUNK, SC_CHUNK)]
            return pltpu.make_async_copy(rows_v.at[b], dst, wsem.at[b])

        for g in range(min(ring - 1, n_chunks)):
            gather(g).start()
        for g in range(n_chunks):
            gather(g).wait()
            ahead = g + ring - 1
            if ahead < n_chunks:
                if g >= 1:
                    write(g - 1).wait()
                gather(ahead).start()
            write(g).start()
        for g in range(max(n_chunks - ring, 0), n_chunks):
            write(g).wait()

    return run(ys, idx).reshape(TOP_K, t, d)


def kernel(x, c, positions, w_ada, b_ada, norm_mix_g, w_in, q_norm_g, w_uq, kv_norm_g, w_ukv,
           w_up_attn, conv_w, w_up_conv, w_o, norm_ffn_g, router_w, router_b, w_gu, b_gu,
           w_down, b_down, norm_final_g):
    batch, seq, d = x.shape
    t = batch * seq
    depth = w_ada.shape[0]
    assert d == D_MODEL and batch <= 8 and conv_w.shape[1:] == (CONV_K, CONV_WIDTH)
    assert seq % ROW_TILE == 0 and seq % POST_TILE == 0 and seq % ATT_BLOCK == 0
    assert t % (2 * SC_CHUNK * SC_WORKERS) == 0
    x2 = x.reshape(t, d)
    pos = positions.astype(F32).reshape(t // ROW_TILE, 1, ROW_TILE)
    freqs = _rope_freqs()
    c_pad = jnp.zeros((8, d), F32).at[:batch].set(c)

    for l in range(depth):
        ada = _ada(c_pad, w_ada[l], b_ada[l].reshape(1, -1))
        mod = ada[:batch].reshape(batch, 6, d)
        mod = jnp.concatenate([mod, jnp.zeros((batch, 2, d), F32)], axis=1)

        w_lat, w_conv, w_gate, wq2, wk2, wv = _prep_weights(w_in[l], w_uq[l], w_ukv[l])
        q, k, v, sga, gc = _pre(x2, mod, norm_mix_g[l].reshape(1, d), w_lat, w_conv, w_gate,
                                q_norm_g[l].reshape(1, -1), wq2,
                                kv_norm_g[l].reshape(1, -1), wk2, wv, pos, freqs, conv_w[l],
                                w_up_conv[l].astype(BF16), seq)
        merged = _attention(q, k, v, sga, gc, w_up_attn[l], batch, seq)

        rw_pad = jnp.concatenate([router_w[l], jnp.zeros((d, LANES - N_EXPERTS), F32)], axis=1)
        rb_pad = jnp.concatenate([router_b[l], jnp.full((LANES - N_EXPERTS,), NEG_BIG, F32)])
        rw_hi = rw_pad.astype(BF16)
        rw_lo = (rw_pad - rw_hi.astype(F32)).astype(BF16)
        x1, h2, idx_pad, gate_pad, rank_pad, counts = _post(
            merged, x2, mod, w_o[l], norm_ffn_g[l].reshape(1, d),
            jnp.concatenate([rw_hi, rw_lo], axis=1), rb_pad.reshape(1, LANES), seq)

        dest, group_table, n_rows = _route(
            idx_pad, rank_pad, counts[0, :N_EXPERTS], t)
        xs = _dispatch(h2, dest, n_rows)
        ys = _moe(group_table, xs, w_gu[l], b_gu[l].reshape(N_EXPERTS, 1, -1),
                  w_down[l], b_down[l].reshape(N_EXPERTS, 1, -1))
        y_kt = _undispatch(ys, dest)
        x2 = _final(x1, y_kt, gate_pad, mod, norm_final_g.reshape(1, d), seq, l == depth - 1)

    return x2.reshape(batch, seq, d)
```

```python
import functools
import math

import jax
import jax.numpy as jnp
from jax import lax
from jax.experimental import pallas as pl
from jax.experimental.pallas import tpu as pltpu
from jax.experimental.pallas import tpu_sc as plsc

D_MODEL = 1024
CHUNK = 64
N_HEADS = 8
Q_LORA = 256
KV_LORA = 128
QK_NOPE = 64
QK_ROPE = 32
V_HEAD = 64
QK_HEAD = QK_NOPE + QK_ROPE
ROPE_THETA = 10000.0
CONV_WIDTH = 512
CONV_K = 3
N_EXPERTS = 32
TOP_K = 4
D_EXPERT = 1024
SWIGLU_LIMIT = 7.0
SWIGLU_ALPHA = 1.702
MOE_BLOCK = 256
RMS_EPS = 1e-6

LANES = 128
HEAD_PAD = 128
NEG_BIG = -1e30
VMEM_LIMIT = 56 * 1024 * 1024

F32 = jnp.float32
BF16 = jnp.bfloat16

Q_PRESCALE = (QK_HEAD ** -0.5) * math.log2(math.e)

ADA_TILE = 1024
ROW_TILE = 1024
MOE_STEP_BLOCKS = 4
POST_TILE = 1024
POST_SUB = 512
ATT_BLOCK = 512
ATT_WIDE = 2


def _rms(x, g):
    ms = jnp.mean(x * x, axis=-1, keepdims=True)
    return x * lax.rsqrt(ms + RMS_EPS) * g


def _dot(a, b):
    return jnp.dot(a, b, preferred_element_type=F32)


PACKED = D_MODEL // 2


def _pack_row(x):
    return pltpu.pack_elementwise([x[:, :PACKED], x[:, PACKED:]], packed_dtype=BF16)


def _unpack_row(w):
    half = lambda i: pltpu.unpack_elementwise(w, index=i, packed_dtype=BF16, unpacked_dtype=F32)
    return jnp.concatenate([half(0), half(1)], axis=-1)


def _ada_kernel(c_ref, w_ref, b_ref, o_ref):
    c = c_ref[...]
    ca = (c * jax.nn.sigmoid(c)).astype(BF16)
    o_ref[...] = _dot(ca, w_ref[...].astype(BF16)) + b_ref[...]


def _ada(c_pad, w_ada, b_ada):
    n = w_ada.shape[1]
    tn = ADA_TILE
    return pl.pallas_call(
        _ada_kernel,
        out_shape=jax.ShapeDtypeStruct((c_pad.shape[0], n), F32),
        grid=(n // tn,),
        in_specs=[
            pl.BlockSpec(c_pad.shape, lambda j: (0, 0)),
            pl.BlockSpec((D_MODEL, tn), lambda j: (0, j)),
            pl.BlockSpec((1, tn), lambda j: (0, j)),
        ],
        out_specs=pl.BlockSpec((c_pad.shape[0], tn), lambda j: (0, j)),
        compiler_params=pltpu.CompilerParams(
            dimension_semantics=("arbitrary",), vmem_limit_bytes=VMEM_LIMIT),
        name="ada",
    )(c_pad, w_ada, b_ada)


_C_QLAT = 0
_C_KVLAT = _C_QLAT + Q_LORA
_C_KPE = _C_KVLAT + KV_LORA
_C_END = _C_KPE + HEAD_PAD


def _pre_kernel(tiles_per_seq, x_ref, mod_ref, g_ref, wlat_ref, wconv_ref, wgate_ref, qg_ref,
                wq_ref, kvg_ref, wk_ref, wv_ref, pos_ref, freq_ref, cw_ref, wuc_ref,
                q_ref, k_ref, v_ref, sga_ref, gc_ref, carry_ref):
    i = pl.program_id(0)
    tm = x_ref.shape[0]
    mod = mod_ref[...]
    h = _rms(x_ref[...], g_ref[...]) * (1.0 + mod[1:2]) + mod[0:1]
    hb = h.astype(BF16)

    ang = freq_ref[...] * pos_ref[...]
    cos_t, sin_t = jnp.cos(ang), jnp.sin(ang)
    ones_t = jnp.ones((QK_NOPE, tm), F32)
    zeros_t = jnp.zeros((QK_NOPE, tm), F32)
    pad_t = jnp.zeros((HEAD_PAD - QK_HEAD, tm), F32)
    cosf = jnp.concatenate([ones_t, cos_t, cos_t, pad_t], axis=0).T
    sinf = jnp.concatenate([zeros_t, -sin_t, sin_t, pad_t], axis=0).T

    first_half = lax.broadcasted_iota(jnp.int32, (tm, HEAD_PAD), 1) < QK_NOPE + QK_ROPE // 2

    def rope(slab):
        swapped = jnp.where(first_half, pltpu.roll(slab, HEAD_PAD - QK_ROPE // 2, 1),
                            pltpu.roll(slab, QK_ROPE // 2, 1))
        return slab * cosf + swapped * sinf

    small = _dot(hb, wlat_ref[...])
    q_lat = small[:, _C_QLAT:_C_KVLAT]
    kv_lat = small[:, _C_KVLAT:_C_KPE]
    kpe = rope(small[:, _C_KPE:_C_END])
    qn = _rms(q_lat, qg_ref[...]).astype(BF16)
    q = _dot(qn, wq_ref[...])
    q = jnp.concatenate([rope(q[:, hd * HEAD_PAD:(hd + 1) * HEAD_PAD]) for hd in range(N_HEADS)],
                        axis=-1)
    q_ref[...] = (q * Q_PRESCALE).astype(BF16)
    kvn = _rms(kv_lat, kvg_ref[...]).astype(BF16)
    k = _dot(kvn, wk_ref[...]) + jnp.concatenate([kpe] * N_HEADS, axis=-1)
    k_ref[...] = k.astype(BF16)
    lane = lax.broadcasted_iota(jnp.int32, (tm, N_HEADS * HEAD_PAD), 1)
    ones_col = jnp.where(lane % HEAD_PAD == V_HEAD, 1.0, 0.0)
    v_ref[...] = (_dot(kvn, wv_ref[...]) + ones_col).astype(BF16)

    ucb = _dot(hb, wconv_ref[...])
    cu = ucb[:, 0:CONV_WIDTH] * ucb[:, CONV_WIDTH:2 * CONV_WIDTH]
    b_gate = ucb[:, 2 * CONV_WIDTH:3 * CONV_WIDTH]

    @pl.when(i % tiles_per_seq == 0)
    def _():
        carry_ref[...] = jnp.zeros_like(carry_ref)

    prev = carry_ref[...]
    row = lax.broadcasted_iota(jnp.int32, cu.shape, 0)
    cu1 = jnp.where(row == 0, prev[7:8], pltpu.roll(cu, 1, 0))
    cu2 = jnp.where(row == 0, prev[6:7], jnp.where(row == 1, prev[7:8], pltpu.roll(cu, 2, 0)))
    cw = cw_ref[...]
    z = cw[2:3] * cu + cw[1:2] * cu1 + cw[0:1] * cu2
    carry_ref[...] = cu[tm - 8:tm]
    c_branch = _dot((b_gate * z).astype(BF16), wuc_ref[...])

    gates = _dot(hb, wgate_ref[...])
    sga_ref[...] = jax.nn.sigmoid(gates[:, 0:D_MODEL]).astype(BF16)
    gc_ref[...] = (jax.nn.sigmoid(gates[:, D_MODEL:]) * c_branch).astype(BF16)


def _pre(x2, mod, norm_g, w_lat, w_conv, w_gate, q_norm_g, wq2, kv_norm_g, wk2, wv, pos,
         freqs, conv_w, w_up_conv, seq):
    t = x2.shape[0]
    tm = ROW_TILE
    tiles_per_seq = seq // tm
    full = lambda a: pl.BlockSpec(a.shape, lambda i: (0,) * a.ndim)
    rows = lambda w: pl.BlockSpec((tm, w), lambda i: (i, 0))
    outs = [jax.ShapeDtypeStruct((t, N_HEADS * HEAD_PAD), BF16),
            jax.ShapeDtypeStruct((t, N_HEADS * HEAD_PAD), BF16),
            jax.ShapeDtypeStruct((t, N_HEADS * HEAD_PAD), BF16),
            jax.ShapeDtypeStruct((t, D_MODEL), BF16),
            jax.ShapeDtypeStruct((t, D_MODEL), BF16)]
    return pl.pallas_call(
        functools.partial(_pre_kernel, tiles_per_seq),
        out_shape=outs,
        grid=(t // tm,),
        in_specs=[
            rows(D_MODEL),
            pl.BlockSpec((None, 8, D_MODEL), lambda i: (i // tiles_per_seq, 0, 0)),
            full(norm_g), full(w_lat), full(w_conv), full(w_gate), full(q_norm_g), full(wq2),
            full(kv_norm_g), full(wk2), full(wv),
            pl.BlockSpec((None, 1, tm), lambda i: (i, 0, 0)), full(freqs), full(conv_w),
            full(w_up_conv),
        ],
        out_specs=[rows(N_HEADS * HEAD_PAD), rows(N_HEADS * HEAD_PAD), rows(N_HEADS * HEAD_PAD),
                   rows(D_MODEL), rows(D_MODEL)],
        scratch_shapes=[pltpu.VMEM((8, CONV_WIDTH), F32)],
        compiler_params=pltpu.CompilerParams(
            dimension_semantics=("arbitrary",), vmem_limit_bytes=VMEM_LIMIT),
        name="pre_mixer",
    )(x2, mod, norm_g, w_lat, w_conv, w_gate, q_norm_g, wq2, kv_norm_g, wk2, wv, pos,
      freqs, conv_w, w_up_conv)


def _attn_kernel(q_ref, k_ref, v_ref, sga_ref, gc_ref, wua_ref, o_ref, m_ref, acc_ref, wua_bf):
    i = pl.program_id(1)
    tq = q_ref.shape[0]

    @pl.when(jnp.logical_and(pl.program_id(0) == 0, i == 0))
    def _():
        wua_bf[...] = wua_ref[...].astype(BF16)

    def step(k0, tk, masked, first=False):
        if masked:
            rq = (lax.broadcasted_iota(jnp.int32, (tq, tk), 0) + (tk - tq)) // CHUNK
            ck = lax.broadcasted_iota(jnp.int32, (tq, tk), 1) // CHUNK
            allowed = ck <= rq
        for hd in range(N_HEADS):
            hs = slice(hd * HEAD_PAD, (hd + 1) * HEAD_PAD)
            s = lax.dot_general(q_ref[:, hs], k_ref[pl.ds(k0, tk), hs],
                                (((1,), (1,)), ((), ())), preferred_element_type=F32)
            if masked:
                s = jnp.where(allowed, s, NEG_BIG)
            s_max = s[:, 0:LANES]
            for c in range(1, tk // LANES):
                s_max = jnp.maximum(s_max, s[:, c * LANES:(c + 1) * LANES])
            m_new = jnp.broadcast_to(jnp.max(s_max, axis=-1, keepdims=True), (tq, LANES))
            if not first:
                m_old = m_ref[hd]
                m_new = jnp.maximum(m_old, m_new)
            p = jnp.concatenate(
                [jnp.exp2((s[:, c * LANES:(c + 1) * LANES] - m_new).astype(BF16))
                 for c in range(tk // LANES)], axis=-1)
            pv = _dot(p, v_ref[pl.ds(k0, tk), hs])
            acc_ref[hd] = pv if first else jnp.exp2(m_old - m_new) * acc_ref[hd] + pv
            m_ref[hd] = m_new

    wide = ATT_WIDE * tq
    n_wide = i // ATT_WIDE

    @pl.when(n_wide > 0)
    def _():
        step(0, wide, False, first=True)

    @pl.when(n_wide == 0)
    def _():
        m_ref[...] = jnp.full_like(m_ref, NEG_BIG)
        acc_ref[...] = jnp.zeros_like(acc_ref)

    def body(j, carry):
        step(pl.multiple_of(j * wide, wide), wide, False)
        return carry

    lax.fori_loop(1, n_wide, body, 0)

    for r in range(ATT_WIDE):
        @pl.when(i % ATT_WIDE == r)
        def _():
            step(pl.multiple_of((i - r) * tq, tq), (r + 1) * tq, True)

    heads = []
    for hd in range(N_HEADS):
        acc = acc_ref[hd]
        heads.append((acc[:, 0:V_HEAD] / acc[:, V_HEAD:V_HEAD + 1]).astype(BF16))
    a_branch = _dot(jnp.concatenate(heads, axis=-1), wua_bf[...])
    o_ref[...] = (sga_ref[...].astype(F32) * a_branch + gc_ref[...].astype(F32)).astype(BF16)


def _attention(q, k, v, sga, gc, wua, batch, seq):
    tq = ATT_BLOCK
    nq = seq // tq
    q_rows = lambda w: pl.BlockSpec((tq, w), lambda b, i: (b * nq + i, 0))
    whole_seq = pl.BlockSpec((seq, N_HEADS * HEAD_PAD), lambda b, i: (b, 0))
    return pl.pallas_call(
        _attn_kernel,
        out_shape=jax.ShapeDtypeStruct((batch * seq, D_MODEL), BF16),
        grid=(batch, nq),
        in_specs=[q_rows(N_HEADS * HEAD_PAD), whole_seq, whole_seq, q_rows(D_MODEL),
                  q_rows(D_MODEL), pl.BlockSpec(wua.shape, lambda b, i: (0, 0))],
        out_specs=q_rows(D_MODEL),
        scratch_shapes=[pltpu.VMEM((N_HEADS, tq, LANES), F32),
                        pltpu.VMEM((N_HEADS, tq, LANES), F32),
                        pltpu.VMEM(wua.shape, BF16)],
        compiler_params=pltpu.CompilerParams(
            dimension_semantics=("arbitrary", "arbitrary"), vmem_limit_bytes=VMEM_LIMIT),
        name="attention",
    )(q, k, v, sga, gc, wua)


def _post_kernel(merged_ref, x_ref, mod_ref, wo_ref, g_ref, rw_ref, rb_ref,
                 x1_ref, h2_ref, idx_ref, gate_ref, rank_ref, cnt_out_ref, cnt_ref, lg_ref,
                 wo_bf):
    i = pl.program_id(0)

    @pl.when(i == 0)
    def _():
        cnt_ref[...] = jnp.zeros_like(cnt_ref)
        lg_ref[...] = jnp.zeros_like(lg_ref)
        wo_bf[...] = wo_ref[...].astype(BF16)

    prev_logits = lg_ref[(i + 1) % 2]
    counts = cnt_ref[...]
    routed = counts
    for r0 in range(0, x_ref.shape[0], POST_SUB):
        rs = slice(r0, r0 + POST_SUB)
        lg_ref[i % 2, rs, :] = _post_mix(rs, merged_ref, x_ref, mod_ref, wo_bf, g_ref, rw_ref,
                                         rb_ref, x1_ref, h2_ref)
        routed = _post_route(rs, prev_logits[rs, :], routed, idx_ref, gate_ref, rank_ref)
    counts = jnp.where(i > 0, routed, counts)
    cnt_ref[...] = counts
    cnt_out_ref[...] = counts.astype(jnp.int32)


def _post_mix(rs, merged_ref, x_ref, mod_ref, wo_ref, g_ref, rw_ref, rb_ref, x1_ref, h2_ref):
    mod = mod_ref[...]
    mix = _dot(merged_ref[rs, :], wo_ref[...])
    x1 = x_ref[rs, :] + mod[2:3] * mix
    x1_ref[rs, :] = x1
    h2 = _rms(x1, g_ref[...]) * (1.0 + mod[4:5]) + mod[3:4]
    h2_ref[rs, :] = _pack_row(h2)

    both = _dot(h2.astype(BF16), rw_ref[...])
    return both[:, :LANES] + both[:, LANES:] + rb_ref[...]


def _post_route(rs, logits, counts, idx_ref, gate_ref, rank_ref):
    lane = lax.broadcasted_iota(jnp.int32, logits.shape, 1)
    work = logits
    vals, idxs = [], []
    for _ in range(TOP_K):
        mk = jnp.max(work, axis=-1, keepdims=True)
        ik = jnp.min(jnp.where(work == mk, lane, LANES), axis=-1, keepdims=True)
        vals.append(mk)
        idxs.append(ik)
        work = jnp.where(lane == ik, -jnp.inf, work)
    es = [jnp.exp(vk - vals[0]) for vk in vals]
    denom = es[0] + es[1] + es[2] + es[3]
    tm = logits.shape[0]
    chosen = jnp.zeros(logits.shape, F32)
    for kk in range(TOP_K):
        chosen = chosen + jnp.where(lane == idxs[kk], 1.0, 0.0)
    r_i = lax.broadcasted_iota(jnp.int32, (tm, tm), 0)
    c_i = lax.broadcasted_iota(jnp.int32, (tm, tm), 1)
    earlier = jnp.where(c_i < r_i, 1.0, 0.0).astype(BF16)
    before = _dot(earlier, chosen.astype(BF16)) + counts[0:1]

    idx_out = jnp.zeros(logits.shape, F32)
    gate_out = jnp.zeros(logits.shape, F32)
    rank_out = jnp.zeros(logits.shape, F32)
    for kk in range(TOP_K):
        rank_k = jnp.sum(jnp.where(lane == idxs[kk], before, 0.0), axis=-1, keepdims=True)
        idx_out = jnp.where(lane == kk, idxs[kk].astype(F32), idx_out)
        gate_out = jnp.where(lane == kk, es[kk] / denom, gate_out)
        rank_out = jnp.where(lane == kk, rank_k, rank_out)
    gate_ref[rs, :] = gate_out
    idx_ref[:, rs] = idx_out.T[0:8].astype(jnp.int32)
    rank_ref[:, rs] = rank_out.T[0:8].astype(jnp.int32)
    return counts + jnp.sum(chosen, axis=0, keepdims=True)


def _post(merged, x2, mod, wo, norm_g, rw_hl, rb_pad, seq):
    t = x2.shape[0]
    tm = POST_TILE
    tiles_per_seq = seq // tm
    n_tiles = t // tm
    full = lambda a: pl.BlockSpec(a.shape, lambda i: (0,) * a.ndim)
    mix_tile = lambda i: jnp.minimum(i, n_tiles - 1)
    route_tile = lambda i: jnp.maximum(i - 1, 0)
    rows = lambda w: pl.BlockSpec((tm, w), lambda i: (mix_tile(i), 0))
    outs = [jax.ShapeDtypeStruct((t, D_MODEL), F32),
            jax.ShapeDtypeStruct((t, PACKED), jnp.uint32),
            jax.ShapeDtypeStruct((8, t), jnp.int32),
            jax.ShapeDtypeStruct((t, LANES), F32),
            jax.ShapeDtypeStruct((8, t), jnp.int32),
            jax.ShapeDtypeStruct((8, LANES), jnp.int32)]
    slots = pl.BlockSpec((8, tm), lambda i: (0, route_tile(i)))
    return pl.pallas_call(
        _post_kernel,
        out_shape=outs,
        grid=(n_tiles + 1,),
        in_specs=[
            rows(D_MODEL), rows(D_MODEL),
            pl.BlockSpec((None, 8, D_MODEL), lambda i: (mix_tile(i) // tiles_per_seq, 0, 0)),
            full(wo), full(norm_g), full(rw_hl), full(rb_pad),
        ],
        out_specs=[rows(D_MODEL), rows(PACKED), slots,
                   pl.BlockSpec((tm, LANES), lambda i: (route_tile(i), 0)), slots,
                   pl.BlockSpec((8, LANES), lambda i: (0, 0))],
        scratch_shapes=[pltpu.VMEM((8, LANES), F32), pltpu.VMEM((2, tm, LANES), F32),
                        pltpu.VMEM(wo.shape, BF16)],
        compiler_params=pltpu.CompilerParams(
            dimension_semantics=("arbitrary",), vmem_limit_bytes=VMEM_LIMIT),
        name="post_mixer",
    )(merged, x2, mod, wo, norm_g, rw_hl, rb_pad)


_GM_COUNT, _GM_FIRST, _GM_BLOCKS = range(3)
_ST_EXPERT, _ST_SLOT = range(2)


def _moe_kernel(gm_ref, xs_ref, wgu_hbm, bgu_ref, wd_hbm, bd_ref, o_ref,
                wgu_f, wd_f, wgu_bf, wd_bf, sem, st_ref):
    weights = (wgu_hbm, wd_hbm, wgu_f, wd_f, wgu_bf, wd_bf, sem)
    rows = (xs_ref, bgu_ref, bd_ref, o_ref, wgu_bf, wd_bf)
    b0 = pl.program_id(0) * MOE_STEP_BLOCKS
    e, in_group, used = _moe_enter(b0, gm_ref, st_ref, *weights)
    together = jnp.logical_and(used, in_group + MOE_STEP_BLOCKS <= gm_ref[_GM_BLOCKS, e])

    @pl.when(together)
    def _():
        _moe_rows(slice(0, MOE_STEP_BLOCKS * MOE_BLOCK), e, in_group, True, gm_ref, *rows)

    @pl.when(jnp.logical_not(together))
    def _():
        for r in range(0, MOE_STEP_BLOCKS, 2):
            if r == 0:
                e_r, in_group_r, used_r = e, in_group, used
            else:
                e_r, in_group_r, used_r = _moe_enter(b0 + r, gm_ref, st_ref, *weights)
            paired = jnp.logical_and(used_r, in_group_r + 2 <= gm_ref[_GM_BLOCKS, e_r])

            @pl.when(paired)
            def _():
                _moe_rows(slice(r * MOE_BLOCK, (r + 2) * MOE_BLOCK), e_r, in_group_r, True,
                          gm_ref, *rows)

            @pl.when(jnp.logical_not(paired))
            def _():
                _moe_rows(slice(r * MOE_BLOCK, (r + 1) * MOE_BLOCK), e_r, in_group_r, used_r,
                          gm_ref, *rows)
                e_n, in_group_n, used_n = _moe_enter(b0 + r + 1, gm_ref, st_ref, *weights)
                _moe_rows(slice((r + 1) * MOE_BLOCK, (r + 2) * MOE_BLOCK), e_n, in_group_n,
                          used_n, gm_ref, *rows)


def _moe_enter(b, gm_ref, st_ref, wgu_hbm, wd_hbm, wgu_f, wd_f, wgu_bf, wd_bf, sem):
    def weight_copies(expert, sl):
        return (pltpu.make_async_copy(wgu_hbm.at[expert], wgu_f.at[sl], sem.at[0, sl]),
                pltpu.make_async_copy(wd_hbm.at[expert], wd_f.at[sl], sem.at[1, sl]))

    def next_group(e):
        return lax.while_loop(
            lambda k: jnp.logical_and(k < N_EXPERTS,
                                      gm_ref[_GM_BLOCKS, jnp.minimum(k, N_EXPERTS - 1)] == 0),
            lambda k: k + 1, e)

    @pl.when(b == 0)
    def _():
        e0 = next_group(0)
        st_ref[_ST_EXPERT] = e0
        st_ref[_ST_SLOT] = 1
        for cp in weight_copies(e0, 0):
            cp.start()

    e_prev = st_ref[_ST_EXPERT]
    past = b >= gm_ref[_GM_FIRST, e_prev] + gm_ref[_GM_BLOCKS, e_prev]
    e = jnp.minimum(jnp.where(past, next_group(e_prev + 1), e_prev), N_EXPERTS - 1)
    st_ref[_ST_EXPERT] = e
    in_group = b - gm_ref[_GM_FIRST, e]
    used = jnp.logical_and(in_group >= 0, in_group < gm_ref[_GM_BLOCKS, e])

    @pl.when(jnp.logical_and(used, in_group == 0))
    def _():
        slot = 1 - st_ref[_ST_SLOT]
        st_ref[_ST_SLOT] = slot
        nxt = next_group(e + 1)

        @pl.when(nxt < N_EXPERTS)
        def _():
            for cp in weight_copies(nxt, 1 - slot):
                cp.start(priority=1)

        for cp in weight_copies(e, slot):
            cp.wait()
        wgu_bf[...] = wgu_f[slot].astype(BF16)
        wd_bf[...] = wd_f[slot].astype(BF16)

    return e, in_group, used


def _moe_rows(rs, e, in_group, used, gm_ref, xs_ref, bgu_ref, bd_ref, o_ref, wgu_bf, wd_bf):
    n_valid = gm_ref[_GM_COUNT, e] - in_group * MOE_BLOCK

    def ffn(r):
        n = r.stop - r.start
        row = lax.broadcasted_iota(jnp.int32, (n, PACKED), 0)
        xs = _unpack_row(jnp.where(row < n_valid, xs_ref[r, :], 0)).astype(BF16)
        gu = _dot(xs, wgu_bf[...]) + bgu_ref[e]
        gate = jnp.minimum(gu[:, :D_EXPERT], SWIGLU_LIMIT)
        up = jnp.clip(gu[:, D_EXPERT:], -SWIGLU_LIMIT, SWIGLU_LIMIT)
        act = (up + 1.0) * (gate * jax.nn.sigmoid(gate * SWIGLU_ALPHA))
        o_ref[r, :] = _pack_row(_dot(act.astype(BF16), wd_bf[...]) + bd_ref[e])

    def zeros(r):
        o_ref[r, :] = jnp.zeros((r.stop - r.start, PACKED), o_ref.dtype)

    if used is True:
        ffn(rs)
        return

    mid = rs.start + (rs.stop - rs.start) // 2
    short = n_valid <= mid - rs.start

    @pl.when(jnp.logical_and(used, jnp.logical_not(short)))
    def _():
        ffn(rs)

    @pl.when(jnp.logical_and(used, short))
    def _():
        ffn(slice(rs.start, mid))
        zeros(slice(mid, rs.stop))

    @pl.when(jnp.logical_not(used))
    def _():
        zeros(rs)


def _moe(group_table, xs, w_gu, b_gu, w_down, b_down):
    n_rows = xs.shape[0]
    step_rows = MOE_STEP_BLOCKS * MOE_BLOCK
    assert n_rows % step_rows == 0
    grid_spec = pltpu.PrefetchScalarGridSpec(
        num_scalar_prefetch=1,
        grid=(n_rows // step_rows,),
        in_specs=[
            pl.BlockSpec((step_rows, PACKED), lambda b, gm: (b, 0)),
            pl.BlockSpec(memory_space=pl.ANY),
            pl.BlockSpec(b_gu.shape, lambda b, gm: (0, 0, 0)),
            pl.BlockSpec(memory_space=pl.ANY),
            pl.BlockSpec(b_down.shape, lambda b, gm: (0, 0, 0)),
        ],
        out_specs=pl.BlockSpec((step_rows, PACKED), lambda b, gm: (b, 0)),
        scratch_shapes=[pltpu.VMEM((2, D_MODEL, 2 * D_EXPERT), F32),
                        pltpu.VMEM((2, D_EXPERT, D_MODEL), F32),
                        pltpu.VMEM((D_MODEL, 2 * D_EXPERT), BF16),
                        pltpu.VMEM((D_EXPERT, D_MODEL), BF16),
                        pltpu.SemaphoreType.DMA((2, 2)),
                        pltpu.SMEM((2,), jnp.int32)],
    )
    return pl.pallas_call(
        _moe_kernel,
        out_shape=jax.ShapeDtypeStruct((n_rows, PACKED), jnp.uint32),
        grid_spec=grid_spec,
        compiler_params=pltpu.CompilerParams(
            dimension_semantics=("arbitrary",), vmem_limit_bytes=VMEM_LIMIT),
        name="moe_experts",
    )(group_table, xs, w_gu, b_gu, w_down, b_down)


def _final_kernel(last_layer, x1_ref, y_ref, gate_ref, mod_ref, g_ref, o_ref):
    mod = mod_ref[...]
    gate = gate_ref[...]
    ffn = gate[:, 0:1] * _unpack_row(y_ref[0])
    for kk in range(1, TOP_K):
        ffn = ffn + gate[:, kk:kk + 1] * _unpack_row(y_ref[kk])
    x = x1_ref[...] + mod[5:6] * ffn
    o_ref[...] = _rms(x, g_ref[...]) if last_layer else x


def _final(x1, y_kt, gate, mod, norm_g, seq, last_layer):
    t = x1.shape[0]
    tm = ROW_TILE
    tiles_per_seq = seq // tm
    rows = lambda w: pl.BlockSpec((tm, w), lambda i: (i, 0))
    return pl.pallas_call(
        functools.partial(_final_kernel, last_layer),
        out_shape=jax.ShapeDtypeStruct((t, D_MODEL), F32),
        grid=(t // tm,),
        in_specs=[
            rows(D_MODEL), pl.BlockSpec((TOP_K, tm, PACKED), lambda i: (0, i, 0)), rows(LANES),
            pl.BlockSpec((None, 8, D_MODEL), lambda i: (i // tiles_per_seq, 0, 0)),
            pl.BlockSpec(norm_g.shape, lambda i: (0, 0)),
        ],
        out_specs=rows(D_MODEL),
        compiler_params=pltpu.CompilerParams(
            dimension_semantics=("arbitrary",), vmem_limit_bytes=VMEM_LIMIT),
        name="combine_final",
    )(x1, y_kt, gate, mod, norm_g)


def _prep_weights(w_in, w_uq, w_ukv):
    d = w_in.shape[0]
    splits = (Q_LORA, KV_LORA, QK_ROPE, CONV_WIDTH, CONV_WIDTH, CONV_WIDTH, D_MODEL, D_MODEL)
    offs = [0]
    for s in splits:
        offs.append(offs[-1] + s)
    part = lambda n: w_in[:, offs[n]:offs[n + 1]]
    z = lambda n: jnp.zeros((d, n), w_in.dtype)
    w_kpe = part(2)
    kpe_slab = jnp.concatenate([z(QK_NOPE), w_kpe, z(HEAD_PAD - QK_HEAD)], axis=1)
    w_lat = jnp.concatenate([part(0), part(1), kpe_slab], axis=1).astype(BF16)
    w_conv = w_in[:, offs[3]:offs[6]].astype(BF16)
    w_gate = w_in[:, offs[6]:offs[8]].astype(BF16)

    wq = w_uq.reshape(Q_LORA, N_HEADS, QK_HEAD)
    zq = lambda n: jnp.zeros((Q_LORA, N_HEADS, n), w_uq.dtype)
    wq2 = jnp.concatenate([wq, zq(HEAD_PAD - QK_HEAD)], axis=-1)
    wq2 = wq2.reshape(Q_LORA, N_HEADS * HEAD_PAD).astype(BF16)

    wkv = w_ukv.reshape(KV_LORA, N_HEADS, QK_NOPE + V_HEAD)
    wk2 = jnp.concatenate([wkv[..., :QK_NOPE],
                           jnp.zeros((KV_LORA, N_HEADS, HEAD_PAD - QK_NOPE), w_ukv.dtype)], axis=-1)
    wk2 = wk2.reshape(KV_LORA, N_HEADS * HEAD_PAD).astype(BF16)
    wv = jnp.concatenate([wkv[..., QK_NOPE:],
                          jnp.zeros((KV_LORA, N_HEADS, HEAD_PAD - V_HEAD), w_ukv.dtype)], axis=-1)
    wv = wv.reshape(KV_LORA, N_HEADS * HEAD_PAD).astype(BF16)
    return w_lat, w_conv, w_gate, wq2, wk2, wv


def _rope_freqs():
    inv_freq = 1.0 / (ROPE_THETA ** (jnp.arange(0, QK_ROPE, 2, dtype=F32) / QK_ROPE))
    return inv_freq.reshape(QK_ROPE // 2, 1)


def _dest_kernel(gm_ref, idx_ref, rank_ref, o_ref):
    idx = idx_ref[...]
    dest = rank_ref[...]
    for e in range(N_EXPERTS):
        dest = dest + jnp.where(idx == e, gm_ref[_GM_FIRST, e] * MOE_BLOCK, 0)
    o_ref[...] = dest


def _route(top_idx, rank, counts, n_tokens):
    blocks = (counts + MOE_BLOCK - 1) // MOE_BLOCK
    first_block = jnp.cumsum(blocks) - blocks
    table = jnp.stack([counts, first_block, blocks]).astype(jnp.int32)
    whole = pl.BlockSpec(top_idx.shape, lambda i, gm: (0, 0))
    dest = pl.pallas_call(
        _dest_kernel,
        out_shape=jax.ShapeDtypeStruct(rank.shape, jnp.int32),
        grid_spec=pltpu.PrefetchScalarGridSpec(
            num_scalar_prefetch=1, grid=(1,), in_specs=[whole, whole], out_specs=whole),
        name="row_destinations",
    )(table, top_idx, rank)[:TOP_K]
    n_rows = n_tokens * TOP_K + N_EXPERTS * MOE_BLOCK
    return dest, table, n_rows


SC_CORES = 2
SC_SUBCORES = 16
SC_WORKERS = SC_CORES * SC_SUBCORES
SC_CHUNK = 64
SC_GATHER_RING = 3

def _sc_mesh():
    return plsc.VectorSubcoreMesh(core_axis_name="c", subcore_axis_name="s")


def _sc_worker():
    return lax.axis_index("s") * SC_CORES + lax.axis_index("c")


def _dispatch(h2, dest, n_rows):
    t, d = h2.shape
    per_w = t // SC_WORKERS
    n_chunks = per_w // SC_CHUNK
    assert per_w % (2 * SC_CHUNK) == 0
    idx = dest.reshape(TOP_K, SC_WORKERS, n_chunks, SC_CHUNK).transpose(1, 0, 2, 3)
    idx = idx.reshape(SC_WORKERS, TOP_K * n_chunks, SC_CHUNK)

    @functools.partial(
        pl.kernel, mesh=_sc_mesh(),
        out_type=jax.ShapeDtypeStruct((n_rows, d), h2.dtype),
        scratch_types=[pltpu.VMEM((TOP_K * n_chunks, SC_CHUNK), jnp.int32),
                       pltpu.VMEM((2, SC_CHUNK, d), h2.dtype),
                       pltpu.SemaphoreType.DMA((2,)),
                       pltpu.SemaphoreType.DMA((2,))],
        name="moe_dispatch")
    def run(h2_hbm, idx_hbm, xs_hbm, idx_v, rows_v, rsem, ssem):
        w = _sc_worker()
        pltpu.sync_copy(idx_hbm.at[w], idx_v)

        def read(g, b):
            src = h2_hbm.at[pl.ds(w * per_w + g * SC_CHUNK, SC_CHUNK)]
            return pltpu.make_async_copy(src, rows_v.at[b], rsem.at[b])

        def scatter(g, kk, b):
            dst = xs_hbm.at[idx_v.at[kk * n_chunks + g]]
            return pltpu.make_async_copy(rows_v.at[b], dst, ssem.at[b])

        read(0, 0).start()

        @pl.loop(0, n_chunks, step=2)
        def _(g0):
            for b in range(2):
                g = g0 + b
                read(g, b).wait()

                @pl.when(g + 1 < n_chunks)
                def _():
                    read(g + 1, 1 - b).start()

                for kk in range(TOP_K):
                    scatter(g, kk, b).start()
                for kk in range(TOP_K):
                    scatter(g, kk, b).wait()

    return run(h2, idx)


def _undispatch(ys, dest):
    t = dest.shape[1]
    d = ys.shape[1]
    n_out = t * TOP_K
    per_w = n_out // SC_WORKERS
    n_chunks = per_w // SC_CHUNK
    idx = dest.reshape(SC_WORKERS, n_chunks, SC_CHUNK)
    ring = SC_GATHER_RING

    @functools.partial(
        pl.kernel, mesh=_sc_mesh(),
        out_type=jax.ShapeDtypeStruct((n_out, d), ys.dtype),
        scratch_types=[pltpu.VMEM((n_chunks, SC_CHUNK), jnp.int32),
                       pltpu.VMEM((ring, SC_CHUNK, d), ys.dtype),
                       pltpu.SemaphoreType.DMA((ring,)),
                       pltpu.SemaphoreType.DMA((ring,))],
        name="moe_undispatch")
    def run(ys_hbm, idx_hbm, out_hbm, idx_v, rows_v, gsem, wsem):
        w = _sc_worker()
        pltpu.sync_copy(idx_hbm.at[w], idx_v)

        def gather(g):
            b = g % ring
            return pltpu.make_async_copy(ys_hbm.at[idx_v.at[g]], rows_v.at[b], gsem.at[b])

        def write(g):
            b = g % ring
            dst = out_hbm.at[pl.ds(w * per_w + g * SC_CHUNK, SC_CHUNK)]
            return pltpu.make_async_copy(rows_v.at[b], dst, wsem.at[b])

        for g in range(min(ring - 1, n_chunks)):
            gather(g).start()
        for g in range(n_chunks):
            gather(g).wait()
            ahead = g + ring - 1
            if ahead < n_chunks:
                if g >= 1:
                    write(g - 1).wait()
                gather(ahead).start()
            write(g).start()
        for g in range(max(n_chunks - ring, 0), n_chunks):
            write(g).wait()

    return run(ys, idx).reshape(TOP_K, t, d)


def kernel(x, c, positions, w_ada, b_ada, norm_mix_g, w_in, q_norm_g, w_uq, kv_norm_g, w_ukv,
           w_up_attn, conv_w, w_up_conv, w_o, norm_ffn_g, router_w, router_b, w_gu, b_gu,
           w_down, b_down, norm_final_g):
    batch, seq, d = x.shape
    t = batch * seq
    depth = w_ada.shape[0]
    assert d == D_MODEL and batch <= 8 and conv_w.shape[1:] == (CONV_K, CONV_WIDTH)
    assert seq % ROW_TILE == 0 and seq % POST_TILE == 0 and seq % ATT_BLOCK == 0
    assert t % (2 * SC_CHUNK * SC_WORKERS) == 0
    x2 = x.reshape(t, d)
    pos = positions.astype(F32).reshape(t // ROW_TILE, 1, ROW_TILE)
    freqs = _rope_freqs()
    c_pad = jnp.zeros((8, d), F32).at[:batch].set(c)

    for l in range(depth):
        ada = _ada(c_pad, w_ada[l], b_ada[l].reshape(1, -1))
        mod = ada[:batch].reshape(batch, 6, d)
        mod = jnp.concatenate([mod, jnp.zeros((batch, 2, d), F32)], axis=1)

        w_lat, w_conv, w_gate, wq2, wk2, wv = _prep_weights(w_in[l], w_uq[l], w_ukv[l])
        q, k, v, sga, gc = _pre(x2, mod, norm_mix_g[l].reshape(1, d), w_lat, w_conv, w_gate,
                                q_norm_g[l].reshape(1, -1), wq2,
                                kv_norm_g[l].reshape(1, -1), wk2, wv, pos, freqs, conv_w[l],
                                w_up_conv[l].astype(BF16), seq)
        merged = _attention(q, k, v, sga, gc, w_up_attn[l], batch, seq)

        rw_pad = jnp.concatenate([router_w[l], jnp.zeros((d, LANES - N_EXPERTS), F32)], axis=1)
        rb_pad = jnp.concatenate([router_b[l], jnp.full((LANES - N_EXPERTS,), NEG_BIG, F32)])
        rw_hi = rw_pad.astype(BF16)
        rw_lo = (rw_pad - rw_hi.astype(F32)).astype(BF16)
        x1, h2, idx_pad, gate_pad, rank_pad, counts = _post(
            merged, x2, mod, w_o[l], norm_ffn_g[l].reshape(1, d),
            jnp.concatenate([rw_hi, rw_lo], axis=1), rb_pad.reshape(1, LANES), seq)

        dest, group_table, n_rows = _route(
            idx_pad, rank_pad, counts[0, :N_EXPERTS], t)
        xs = _dispatch(h2, dest, n_rows)
        ys = _moe(group_table, xs, w_gu[l], b_gu[l].reshape(N_EXPERTS, 1, -1),
                  w_down[l], b_down[l].reshape(N_EXPERTS, 1, -1))
        y_kt = _undispatch(ys, dest)
        x2 = _final(x1, y_kt, gate_pad, mod, norm_final_g.reshape(1, d), seq, l == depth - 1)

    return x2.reshape(batch, seq, d)
```

```python
import functools
import math

import jax
import jax.numpy as jnp
from jax import lax
from jax.experimental import pallas as pl
from jax.experimental.pallas import tpu as pltpu
from jax.experimental.pallas import tpu_sc as plsc

D_MODEL = 1024
CHUNK = 64
N_HEADS = 8
Q_LORA = 256
KV_LORA = 128
QK_NOPE = 64
QK_ROPE = 32
V_HEAD = 64
QK_HEAD = QK_NOPE + QK_ROPE
ROPE_THETA = 10000.0
CONV_WIDTH = 512
CONV_K = 3
N_EXPERTS = 32
TOP_K = 4
D_EXPERT = 1024
SWIGLU_LIMIT = 7.0
SWIGLU_ALPHA = 1.702
MOE_BLOCK = 256
RMS_EPS = 1e-6

LANES = 128
HEAD_PAD = 128
NEG_BIG = -1e30
VMEM_LIMIT = 56 * 1024 * 1024

F32 = jnp.float32
BF16 = jnp.bfloat16

Q_PRESCALE = (QK_HEAD ** -0.5) * math.log2(math.e)

ADA_TILE = 1024
ROW_TILE = 1024
MOE_STEP_BLOCKS = 4
POST_TILE = 1024
POST_SUB = 512
ATT_BLOCK = 512
ATT_WIDE = 2


def _rms(x, g):
    ms = jnp.mean(x * x, axis=-1, keepdims=True)
    return x * lax.rsqrt(ms + RMS_EPS) * g


def _dot(a, b):
    return jnp.dot(a, b, preferred_element_type=F32)


def _dot_t(a, b_t):
    return lax.dot_general(a, b_t, (((1,), (1,)), ((), ())), preferred_element_type=F32)


PACKED = D_MODEL // 2


def _pack_row(x):
    return pltpu.pack_elementwise([x[:, :PACKED], x[:, PACKED:]], packed_dtype=BF16)


def _unpack_row(w):
    half = lambda i: pltpu.unpack_elementwise(w, index=i, packed_dtype=BF16, unpacked_dtype=F32)
    return jnp.concatenate([half(0), half(1)], axis=-1)


def _ada_kernel(c_ref, w_ref, b_ref, o_ref):
    c = c_ref[...]
    ca = (c * jax.nn.sigmoid(c)).astype(BF16)
    o_ref[...] = _dot(ca, w_ref[...].astype(BF16)) + b_ref[...]


def _ada(c_pad, w_ada, b_ada):
    n = w_ada.shape[1]
    tn = ADA_TILE
    return pl.pallas_call(
        _ada_kernel,
        out_shape=jax.ShapeDtypeStruct((c_pad.shape[0], n), F32),
        grid=(n // tn,),
        in_specs=[
            pl.BlockSpec(c_pad.shape, lambda j: (0, 0)),
            pl.BlockSpec((D_MODEL, tn), lambda j: (0, j)),
            pl.BlockSpec((1, tn), lambda j: (0, j)),
        ],
        out_specs=pl.BlockSpec((c_pad.shape[0], tn), lambda j: (0, j)),
        compiler_params=pltpu.CompilerParams(
            dimension_semantics=("arbitrary",), vmem_limit_bytes=VMEM_LIMIT),
        name="ada",
    )(c_pad, w_ada, b_ada)


_C_QLAT = 0
_C_KVLAT = _C_QLAT + Q_LORA
_C_KPE = _C_KVLAT + KV_LORA
_C_END = _C_KPE + HEAD_PAD
_F_KPE = Q_LORA + KV_LORA
_F_CONV = _F_KPE + QK_ROPE
_F_GATE = _F_CONV + 3 * CONV_WIDTH
_F_END = _F_GATE + 2 * D_MODEL


def _pre_kernel(tiles_per_seq, x_ref, mod_ref, g_ref, wlat_ref, wt_ref, qg_ref,
                wq_ref, kvg_ref, wk_ref, wv_ref, pos_ref, freq_ref, cw_ref, wuc_ref,
                q_ref, k_ref, v_ref, sga_ref, gc_ref, carry_ref):
    i = pl.program_id(0)
    tm = x_ref.shape[0]
    mod = mod_ref[...]
    h = _rms(x_ref[...], g_ref[...]) * (1.0 + mod[1:2]) + mod[0:1]
    hb = h.astype(BF16)

    ang = freq_ref[...] * pos_ref[...]
    cos_t, sin_t = jnp.cos(ang), jnp.sin(ang)
    ones_t = jnp.ones((QK_NOPE, tm), F32)
    zeros_t = jnp.zeros((QK_NOPE, tm), F32)
    pad_t = jnp.zeros((HEAD_PAD - QK_HEAD, tm), F32)
    cosf = jnp.concatenate([ones_t, cos_t, cos_t, pad_t], axis=0).T
    sinf = jnp.concatenate([zeros_t, -sin_t, sin_t, pad_t], axis=0).T

    first_half = lax.broadcasted_iota(jnp.int32, (tm, HEAD_PAD), 1) < QK_NOPE + QK_ROPE // 2

    def rope(slab):
        swapped = jnp.where(first_half, pltpu.roll(slab, HEAD_PAD - QK_ROPE // 2, 1),
                            pltpu.roll(slab, QK_ROPE // 2, 1))
        return slab * cosf + swapped * sinf

    small = _dot_t(hb, wlat_ref[...])
    q_lat = small[:, _C_QLAT:_C_KVLAT]
    kv_lat = small[:, _C_KVLAT:_C_KPE]
    kpe = rope(small[:, _C_KPE:_C_END])
    qn = _rms(q_lat, qg_ref[...]).astype(BF16)
    q = _dot(qn, wq_ref[...])
    q = jnp.concatenate([rope(q[:, hd * HEAD_PAD:(hd + 1) * HEAD_PAD]) for hd in range(N_HEADS)],
                        axis=-1)
    q_ref[...] = (q * Q_PRESCALE).astype(BF16)
    kvn = _rms(kv_lat, kvg_ref[...]).astype(BF16)
    k = _dot(kvn, wk_ref[...]) + jnp.concatenate([kpe] * N_HEADS, axis=-1)
    k_ref[...] = k.astype(BF16)
    lane = lax.broadcasted_iota(jnp.int32, (tm, N_HEADS * HEAD_PAD), 1)
    ones_col = jnp.where(lane % HEAD_PAD == V_HEAD, 1.0, 0.0)
    v_ref[...] = (_dot(kvn, wv_ref[...]) + ones_col).astype(BF16)

    ucb = _dot_t(hb, wt_ref[_F_CONV:_F_GATE, :])
    cu = ucb[:, 0:CONV_WIDTH] * ucb[:, CONV_WIDTH:2 * CONV_WIDTH]
    b_gate = ucb[:, 2 * CONV_WIDTH:3 * CONV_WIDTH]

    @pl.when(i % tiles_per_seq == 0)
    def _():
        carry_ref[...] = jnp.zeros_like(carry_ref)

    prev = carry_ref[...]
    row = lax.broadcasted_iota(jnp.int32, cu.shape, 0)
    cu1 = jnp.where(row == 0, prev[7:8], pltpu.roll(cu, 1, 0))
    cu2 = jnp.where(row == 0, prev[6:7], jnp.where(row == 1, prev[7:8], pltpu.roll(cu, 2, 0)))
    cw = cw_ref[...]
    z = cw[2:3] * cu + cw[1:2] * cu1 + cw[0:1] * cu2
    carry_ref[...] = cu[tm - 8:tm]
    c_branch = _dot((b_gate * z).astype(BF16), wuc_ref[...])

    gates = _dot_t(hb, wt_ref[_F_GATE:_F_END, :])
    sga_ref[...] = jax.nn.sigmoid(gates[:, 0:D_MODEL]).astype(BF16)
    gc_ref[...] = (jax.nn.sigmoid(gates[:, D_MODEL:]) * c_branch).astype(BF16)


def _pre(x2, mod, norm_g, w_lat, w_t, q_norm_g, wq2, kv_norm_g, wk2, wv, pos,
         freqs, conv_w, w_up_conv, seq):
    t = x2.shape[0]
    tm = ROW_TILE
    tiles_per_seq = seq // tm
    full = lambda a: pl.BlockSpec(a.shape, lambda i: (0,) * a.ndim)
    rows = lambda w: pl.BlockSpec((tm, w), lambda i: (i, 0))
    outs = [jax.ShapeDtypeStruct((t, N_HEADS * HEAD_PAD), BF16),
            jax.ShapeDtypeStruct((t, N_HEADS * HEAD_PAD), BF16),
            jax.ShapeDtypeStruct((t, N_HEADS * HEAD_PAD), BF16),
            jax.ShapeDtypeStruct((t, D_MODEL), BF16),
            jax.ShapeDtypeStruct((t, D_MODEL), BF16)]
    return pl.pallas_call(
        functools.partial(_pre_kernel, tiles_per_seq),
        out_shape=outs,
        grid=(t // tm,),
        in_specs=[
            rows(D_MODEL),
            pl.BlockSpec((None, 8, D_MODEL), lambda i: (i // tiles_per_seq, 0, 0)),
            full(norm_g), full(w_lat), full(w_t), full(q_norm_g), full(wq2),
            full(kv_norm_g), full(wk2), full(wv),
            pl.BlockSpec((None, 1, tm), lambda i: (i, 0, 0)), full(freqs), full(conv_w),
            full(w_up_conv),
        ],
        out_specs=[rows(N_HEADS * HEAD_PAD), rows(N_HEADS * HEAD_PAD), rows(N_HEADS * HEAD_PAD),
                   rows(D_MODEL), rows(D_MODEL)],
        scratch_shapes=[pltpu.VMEM((8, CONV_WIDTH), F32)],
        compiler_params=pltpu.CompilerParams(
            dimension_semantics=("arbitrary",), vmem_limit_bytes=VMEM_LIMIT),
        name="pre_mixer",
    )(x2, mod, norm_g, w_lat, w_t, q_norm_g, wq2, kv_norm_g, wk2, wv, pos,
      freqs, conv_w, w_up_conv)


def _attn_kernel(q_ref, k_ref, v_ref, sga_ref, gc_ref, wua_ref, o_ref, m_ref, acc_ref, wua_bf):
    i = pl.program_id(1)
    tq = q_ref.shape[0]

    @pl.when(jnp.logical_and(pl.program_id(0) == 0, i == 0))
    def _():
        wua_bf[...] = wua_ref[...].astype(BF16)

    def step(k0, tk, masked, first=False):
        if masked:
            rq = (lax.broadcasted_iota(jnp.int32, (tq, tk), 0) + (tk - tq)) // CHUNK
            ck = lax.broadcasted_iota(jnp.int32, (tq, tk), 1) // CHUNK
            allowed = ck <= rq
        for hd in range(N_HEADS):
            hs = slice(hd * HEAD_PAD, (hd + 1) * HEAD_PAD)
            s = lax.dot_general(q_ref[:, hs], k_ref[pl.ds(k0, tk), hs],
                                (((1,), (1,)), ((), ())), preferred_element_type=F32)
            if masked:
                s = jnp.where(allowed, s, NEG_BIG)
            s_max = s[:, 0:LANES]
            for c in range(1, tk // LANES):
                s_max = jnp.maximum(s_max, s[:, c * LANES:(c + 1) * LANES])
            m_new = jnp.broadcast_to(jnp.max(s_max, axis=-1, keepdims=True), (tq, LANES))
            if not first:
                m_old = m_ref[hd]
                m_new = jnp.maximum(m_old, m_new)
            p = jnp.concatenate(
                [jnp.exp2(s[:, c * LANES:(c + 1) * LANES] - m_new).astype(BF16)
                 for c in range(tk // LANES)], axis=-1)
            pv = _dot(p, v_ref[pl.ds(k0, tk), hs])
            acc_ref[hd] = pv if first else jnp.exp2(m_old - m_new) * acc_ref[hd] + pv
            m_ref[hd] = m_new

    wide = ATT_WIDE * tq
    n_wide = i // ATT_WIDE

    @pl.when(n_wide > 0)
    def _():
        step(0, wide, False, first=True)

    @pl.when(n_wide == 0)
    def _():
        m_ref[...] = jnp.full_like(m_ref, NEG_BIG)
        acc_ref[...] = jnp.zeros_like(acc_ref)

    def body(j, carry):
        step(pl.multiple_of(j * wide, wide), wide, False)
        return carry

    lax.fori_loop(1, n_wide, body, 0)

    for r in range(ATT_WIDE):
        @pl.when(i % ATT_WIDE == r)
        def _():
            step(pl.multiple_of((i - r) * tq, tq), (r + 1) * tq, True)

    heads = []
    for hd in range(N_HEADS):
        acc = acc_ref[hd]
        heads.append((acc[:, 0:V_HEAD] / acc[:, V_HEAD:V_HEAD + 1]).astype(BF16))
    a_branch = _dot(jnp.concatenate(heads, axis=-1), wua_bf[...])
    o_ref[...] = (sga_ref[...].astype(F32) * a_branch + gc_ref[...].astype(F32)).astype(BF16)


def _attention(q, k, v, sga, gc, wua, batch, seq):
    tq = ATT_BLOCK
    nq = seq // tq
    q_rows = lambda w: pl.BlockSpec((tq, w), lambda b, i: (b * nq + i, 0))
    whole_seq = pl.BlockSpec((seq, N_HEADS * HEAD_PAD), lambda b, i: (b, 0))
    return pl.pallas_call(
        _attn_kernel,
        out_shape=jax.ShapeDtypeStruct((batch * seq, D_MODEL), BF16),
        grid=(batch, nq),
        in_specs=[q_rows(N_HEADS * HEAD_PAD), whole_seq, whole_seq, q_rows(D_MODEL),
                  q_rows(D_MODEL), pl.BlockSpec(wua.shape, lambda b, i: (0, 0))],
        out_specs=q_rows(D_MODEL),
        scratch_shapes=[pltpu.VMEM((N_HEADS, tq, LANES), F32),
                        pltpu.VMEM((N_HEADS, tq, LANES), F32),
                        pltpu.VMEM(wua.shape, BF16)],
        compiler_params=pltpu.CompilerParams(
            dimension_semantics=("arbitrary", "arbitrary"), vmem_limit_bytes=VMEM_LIMIT),
        name="attention",
    )(q, k, v, sga, gc, wua)


def _post_kernel(merged_ref, x_ref, mod_ref, wo_ref, g_ref, rw_ref, rb_ref,
                 x1_ref, h2_ref, idx_ref, gate_ref, rank_ref, cnt_out_ref, cnt_ref, lg_ref,
                 wo_bf):
    i = pl.program_id(0)

    @pl.when(i == 0)
    def _():
        cnt_ref[...] = jnp.zeros_like(cnt_ref)
        lg_ref[...] = jnp.zeros_like(lg_ref)
        wo_bf[...] = wo_ref[...].astype(BF16)

    prev_logits = lg_ref[(i + 1) % 2]
    counts = cnt_ref[...]
    routed = counts
    for r0 in range(0, x_ref.shape[0], POST_SUB):
        rs = slice(r0, r0 + POST_SUB)
        lg_ref[i % 2, rs, :] = _post_mix(rs, merged_ref, x_ref, mod_ref, wo_bf, g_ref, rw_ref,
                                         rb_ref, x1_ref, h2_ref)
        routed = _post_route(rs, prev_logits[rs, :], routed, idx_ref, gate_ref, rank_ref)
    counts = jnp.where(i > 0, routed, counts)
    cnt_ref[...] = counts
    cnt_out_ref[...] = counts.astype(jnp.int32)


def _post_mix(rs, merged_ref, x_ref, mod_ref, wo_ref, g_ref, rw_ref, rb_ref, x1_ref, h2_ref):
    mod = mod_ref[...]
    mix = _dot(merged_ref[rs, :], wo_ref[...])
    x1 = x_ref[rs, :] + mod[2:3] * mix
    x1_ref[rs, :] = x1
    h2 = _rms(x1, g_ref[...]) * (1.0 + mod[4:5]) + mod[3:4]
    h2_ref[rs, :] = _pack_row(h2)

    both = _dot(h2.astype(BF16), rw_ref[...])
    return both[:, :LANES] + both[:, LANES:] + rb_ref[...]


def _post_route(rs, logits, counts, idx_ref, gate_ref, rank_ref):
    lane = lax.broadcasted_iota(jnp.int32, logits.shape, 1)
    work = logits
    vals, idxs = [], []
    for _ in range(TOP_K):
        mk = jnp.max(work, axis=-1, keepdims=True)
        ik = jnp.min(jnp.where(work == mk, lane, LANES), axis=-1, keepdims=True)
        vals.append(mk)
        idxs.append(ik)
        work = jnp.where(lane == ik, -jnp.inf, work)
    es = [jnp.exp(vk - vals[0]) for vk in vals]
    denom = es[0] + es[1] + es[2] + es[3]
    tm = logits.shape[0]
    chosen = jnp.zeros(logits.shape, F32)
    for kk in range(TOP_K):
        chosen = chosen + jnp.where(lane == idxs[kk], 1.0, 0.0)
    r_i = lax.broadcasted_iota(jnp.int32, (tm, tm), 0)
    c_i = lax.broadcasted_iota(jnp.int32, (tm, tm), 1)
    earlier = jnp.where(c_i < r_i, 1.0, 0.0).astype(BF16)
    before = _dot(earlier, chosen.astype(BF16)) + counts[0:1]

    idx_out = jnp.zeros(logits.shape, F32)
    gate_out = jnp.zeros(logits.shape, F32)
    rank_out = jnp.zeros(logits.shape, F32)
    for kk in range(TOP_K):
        rank_k = jnp.sum(jnp.where(lane == idxs[kk], before, 0.0), axis=-1, keepdims=True)
        idx_out = jnp.where(lane == kk, idxs[kk].astype(F32), idx_out)
        gate_out = jnp.where(lane == kk, es[kk] / denom, gate_out)
        rank_out = jnp.where(lane == kk, rank_k, rank_out)
    gate_ref[rs, :] = gate_out
    idx_ref[:, rs] = idx_out.T[0:8].astype(jnp.int32)
    rank_ref[:, rs] = rank_out.T[0:8].astype(jnp.int32)
    return counts + jnp.sum(chosen, axis=0, keepdims=True)


def _post(merged, x2, mod, wo, norm_g, rw_hl, rb_pad, seq):
    t = x2.shape[0]
    tm = POST_TILE
    tiles_per_seq = seq // tm
    n_tiles = t // tm
    full = lambda a: pl.BlockSpec(a.shape, lambda i: (0,) * a.ndim)
    mix_tile = lambda i: jnp.minimum(i, n_tiles - 1)
    route_tile = lambda i: jnp.maximum(i - 1, 0)
    rows = lambda w: pl.BlockSpec((tm, w), lambda i: (mix_tile(i), 0))
    outs = [jax.ShapeDtypeStruct((t, D_MODEL), F32),
            jax.ShapeDtypeStruct((t, PACKED), jnp.uint32),
            jax.ShapeDtypeStruct((8, t), jnp.int32),
            jax.ShapeDtypeStruct((t, LANES), F32),
            jax.ShapeDtypeStruct((8, t), jnp.int32),
            jax.ShapeDtypeStruct((8, LANES), jnp.int32)]
    slots = pl.BlockSpec((8, tm), lambda i: (0, route_tile(i)))
    return pl.pallas_call(
        _post_kernel,
        out_shape=outs,
        grid=(n_tiles + 1,),
        in_specs=[
            rows(D_MODEL), rows(D_MODEL),
            pl.BlockSpec((None, 8, D_MODEL), lambda i: (mix_tile(i) // tiles_per_seq, 0, 0)),
            full(wo), full(norm_g), full(rw_hl), full(rb_pad),
        ],
        out_specs=[rows(D_MODEL), rows(PACKED), slots,
                   pl.BlockSpec((tm, LANES), lambda i: (route_tile(i), 0)), slots,
                   pl.BlockSpec((8, LANES), lambda i: (0, 0))],
        scratch_shapes=[pltpu.VMEM((8, LANES), F32), pltpu.VMEM((2, tm, LANES), F32),
                        pltpu.VMEM(wo.shape, BF16)],
        compiler_params=pltpu.CompilerParams(
            dimension_semantics=("arbitrary",), vmem_limit_bytes=VMEM_LIMIT),
        name="post_mixer",
    )(merged, x2, mod, wo, norm_g, rw_hl, rb_pad)


_GM_COUNT, _GM_FIRST, _GM_BLOCKS = range(3)
_ST_EXPERT, _ST_SLOT = range(2)


def _moe_kernel(gm_ref, xs_ref, wgu_hbm, bgu_ref, wd_hbm, bd_ref, o_ref,
                wgu_f, wd_f, wgu_bf, wd_bf, sem, st_ref):
    weights = (wgu_hbm, wd_hbm, wgu_f, wd_f, wgu_bf, wd_bf, sem)
    rows = (xs_ref, bgu_ref, bd_ref, o_ref, wgu_bf, wd_bf)
    b0 = pl.program_id(0) * MOE_STEP_BLOCKS
    e, in_group, used = _moe_enter(b0, gm_ref, st_ref, *weights)
    together = jnp.logical_and(used, in_group + MOE_STEP_BLOCKS <= gm_ref[_GM_BLOCKS, e])

    @pl.when(together)
    def _():
        _moe_rows(slice(0, MOE_STEP_BLOCKS * MOE_BLOCK), e, in_group, True, gm_ref, *rows)

    @pl.when(jnp.logical_not(together))
    def _():
        for r in range(0, MOE_STEP_BLOCKS, 2):
            if r == 0:
                e_r, in_group_r, used_r = e, in_group, used
            else:
                e_r, in_group_r, used_r = _moe_enter(b0 + r, gm_ref, st_ref, *weights)
            paired = jnp.logical_and(used_r, in_group_r + 2 <= gm_ref[_GM_BLOCKS, e_r])

            @pl.when(paired)
            def _():
                _moe_rows(slice(r * MOE_BLOCK, (r + 2) * MOE_BLOCK), e_r, in_group_r, True,
                          gm_ref, *rows)

            @pl.when(jnp.logical_not(paired))
            def _():
                _moe_rows(slice(r * MOE_BLOCK, (r + 1) * MOE_BLOCK), e_r, in_group_r, used_r,
                          gm_ref, *rows)
                e_n, in_group_n, used_n = _moe_enter(b0 + r + 1, gm_ref, st_ref, *weights)
                _moe_rows(slice((r + 1) * MOE_BLOCK, (r + 2) * MOE_BLOCK), e_n, in_group_n,
                          used_n, gm_ref, *rows)


def _moe_enter(b, gm_ref, st_ref, wgu_hbm, wd_hbm, wgu_f, wd_f, wgu_bf, wd_bf, sem):
    def weight_copies(expert, sl):
        return (pltpu.make_async_copy(wgu_hbm.at[expert], wgu_f.at[sl], sem.at[0, sl]),
                pltpu.make_async_copy(wd_hbm.at[expert], wd_f.at[sl], sem.at[1, sl]))

    def next_group(e):
        return lax.while_loop(
            lambda k: jnp.logical_and(k < N_EXPERTS,
                                      gm_ref[_GM_BLOCKS, jnp.minimum(k, N_EXPERTS - 1)] == 0),
            lambda k: k + 1, e)

    @pl.when(b == 0)
    def _():
        e0 = next_group(0)
        st_ref[_ST_EXPERT] = e0
        st_ref[_ST_SLOT] = 1
        for cp in weight_copies(e0, 0):
            cp.start()

    e_prev = st_ref[_ST_EXPERT]
    past = b >= gm_ref[_GM_FIRST, e_prev] + gm_ref[_GM_BLOCKS, e_prev]
    e = jnp.minimum(jnp.where(past, next_group(e_prev + 1), e_prev), N_EXPERTS - 1)
    st_ref[_ST_EXPERT] = e
    in_group = b - gm_ref[_GM_FIRST, e]
    used = jnp.logical_and(in_group >= 0, in_group < gm_ref[_GM_BLOCKS, e])

    @pl.when(jnp.logical_and(used, in_group == 0))
    def _():
        slot = 1 - st_ref[_ST_SLOT]
        st_ref[_ST_SLOT] = slot
        nxt = next_group(e + 1)

        @pl.when(nxt < N_EXPERTS)
        def _():
            for cp in weight_copies(nxt, 1 - slot):
                cp.start(priority=1)

        for cp in weight_copies(e, slot):
            cp.wait()
        wgu_bf[...] = wgu_f[slot].astype(BF16)
        wd_bf[...] = wd_f[slot].astype(BF16)

    return e, in_group, used


def _moe_rows(rs, e, in_group, used, gm_ref, xs_ref, bgu_ref, bd_ref, o_ref, wgu_bf, wd_bf):
    n_valid = gm_ref[_GM_COUNT, e] - in_group * MOE_BLOCK

    def ffn(r):
        n = r.stop - r.start
        row = lax.broadcasted_iota(jnp.int32, (n, PACKED), 0)
        xs = _unpack_row(jnp.where(row < n_valid, xs_ref[r, :], 0)).astype(BF16)
        gu = _dot(xs, wgu_bf[...]) + bgu_ref[e]
        gate = jnp.minimum(gu[:, :D_EXPERT], SWIGLU_LIMIT)
        up = jnp.clip(gu[:, D_EXPERT:], -SWIGLU_LIMIT, SWIGLU_LIMIT)
        act = (up + 1.0) * (gate * jax.nn.sigmoid(gate * SWIGLU_ALPHA))
        o_ref[r, :] = _pack_row(_dot(act.astype(BF16), wd_bf[...]) + bd_ref[e])

    def zeros(r):
        o_ref[r, :] = jnp.zeros((r.stop - r.start, PACKED), o_ref.dtype)

    if used is True:
        ffn(rs)
        return

    mid = rs.start + (rs.stop - rs.start) // 2
    short = n_valid <= mid - rs.start

    @pl.when(jnp.logical_and(used, jnp.logical_not(short)))
    def _():
        ffn(rs)

    @pl.when(jnp.logical_and(used, short))
    def _():
        ffn(slice(rs.start, mid))
        zeros(slice(mid, rs.stop))

    @pl.when(jnp.logical_not(used))
    def _():
        zeros(rs)


def _moe(group_table, xs, w_gu, b_gu, w_down, b_down):
    n_rows = xs.shape[0]
    step_rows = MOE_STEP_BLOCKS * MOE_BLOCK
    assert n_rows % step_rows == 0
    grid_spec = pltpu.PrefetchScalarGridSpec(
        num_scalar_prefetch=1,
        grid=(n_rows // step_rows,),
        in_specs=[
            pl.BlockSpec((step_rows, PACKED), lambda b, gm: (b, 0)),
            pl.BlockSpec(memory_space=pl.ANY),
            pl.BlockSpec(b_gu.shape, lambda b, gm: (0, 0, 0)),
            pl.BlockSpec(memory_space=pl.ANY),
            pl.BlockSpec(b_down.shape, lambda b, gm: (0, 0, 0)),
        ],
        out_specs=pl.BlockSpec((step_rows, PACKED), lambda b, gm: (b, 0)),
        scratch_shapes=[pltpu.VMEM((2, D_MODEL, 2 * D_EXPERT), F32),
                        pltpu.VMEM((2, D_EXPERT, D_MODEL), F32),
                        pltpu.VMEM((D_MODEL, 2 * D_EXPERT), BF16),
                        pltpu.VMEM((D_EXPERT, D_MODEL), BF16),
                        pltpu.SemaphoreType.DMA((2, 2)),
                        pltpu.SMEM((2,), jnp.int32)],
    )
    return pl.pallas_call(
        _moe_kernel,
        out_shape=jax.ShapeDtypeStruct((n_rows, PACKED), jnp.uint32),
        grid_spec=grid_spec,
        compiler_params=pltpu.CompilerParams(
            dimension_semantics=("arbitrary",), vmem_limit_bytes=VMEM_LIMIT),
        name="moe_experts",
    )(group_table, xs, w_gu, b_gu, w_down, b_down)


def _final_kernel(last_layer, x1_ref, y_ref, gate_ref, mod_ref, g_ref, o_ref):
    mod = mod_ref[...]
    gate = gate_ref[...]
    ffn = gate[:, 0:1] * _unpack_row(y_ref[0])
    for kk in range(1, TOP_K):
        ffn = ffn + gate[:, kk:kk + 1] * _unpack_row(y_ref[kk])
    x = x1_ref[...] + mod[5:6] * ffn
    o_ref[...] = _rms(x, g_ref[...]) if last_layer else x


def _final(x1, y_kt, gate, mod, norm_g, seq, last_layer):
    t = x1.shape[0]
    tm = ROW_TILE
    tiles_per_seq = seq // tm
    rows = lambda w: pl.BlockSpec((tm, w), lambda i: (i, 0))
    return pl.pallas_call(
        functools.partial(_final_kernel, last_layer),
        out_shape=jax.ShapeDtypeStruct((t, D_MODEL), F32),
        grid=(t // tm,),
        in_specs=[
            rows(D_MODEL), pl.BlockSpec((TOP_K, tm, PACKED), lambda i: (0, i, 0)), rows(LANES),
            pl.BlockSpec((None, 8, D_MODEL), lambda i: (i // tiles_per_seq, 0, 0)),
            pl.BlockSpec(norm_g.shape, lambda i: (0, 0)),
        ],
        out_specs=rows(D_MODEL),
        compiler_params=pltpu.CompilerParams(
            dimension_semantics=("arbitrary",), vmem_limit_bytes=VMEM_LIMIT),
        name="combine_final",
    )(x1, y_kt, gate, mod, norm_g)


def _prep_weights(w_in, w_uq, w_ukv):
    assert w_in.shape[1] == _F_END
    w_t = jnp.swapaxes(w_in, 0, 1).astype(BF16)
    z = lambda n: jnp.zeros((n, w_t.shape[1]), BF16)
    w_lat = jnp.concatenate([w_t[:_F_KPE], z(QK_NOPE), w_t[_F_KPE:_F_CONV], z(HEAD_PAD - QK_HEAD)],
                            axis=0)

    wq = w_uq.reshape(Q_LORA, N_HEADS, QK_HEAD)
    zq = lambda n: jnp.zeros((Q_LORA, N_HEADS, n), w_uq.dtype)
    wq2 = jnp.concatenate([wq, zq(HEAD_PAD - QK_HEAD)], axis=-1)
    wq2 = wq2.reshape(Q_LORA, N_HEADS * HEAD_PAD).astype(BF16)

    wkv = w_ukv.reshape(KV_LORA, N_HEADS, QK_NOPE + V_HEAD)
    wk2 = jnp.concatenate([wkv[..., :QK_NOPE],
                           jnp.zeros((KV_LORA, N_HEADS, HEAD_PAD - QK_NOPE), w_ukv.dtype)], axis=-1)
    wk2 = wk2.reshape(KV_LORA, N_HEADS * HEAD_PAD).astype(BF16)
    wv = jnp.concatenate([wkv[..., QK_NOPE:],
                          jnp.zeros((KV_LORA, N_HEADS, HEAD_PAD - V_HEAD), w_ukv.dtype)], axis=-1)
    wv = wv.reshape(KV_LORA, N_HEADS * HEAD_PAD).astype(BF16)
    return w_lat, w_t, wq2, wk2, wv


def _rope_freqs():
    inv_freq = 1.0 / (ROPE_THETA ** (jnp.arange(0, QK_ROPE, 2, dtype=F32) / QK_ROPE))
    return inv_freq.reshape(QK_ROPE // 2, 1)


def _dest_kernel(gm_ref, idx_ref, rank_ref, o_ref):
    idx = idx_ref[...]
    dest = rank_ref[...]
    for e in range(N_EXPERTS):
        dest = dest + jnp.where(idx == e, gm_ref[_GM_FIRST, e] * MOE_BLOCK, 0)
    o_ref[...] = dest


def _route(top_idx, rank, counts, n_tokens):
    blocks = (counts + MOE_BLOCK - 1) // MOE_BLOCK
    first_block = jnp.cumsum(blocks) - blocks
    table = jnp.stack([counts, first_block, blocks]).astype(jnp.int32)
    whole = pl.BlockSpec(top_idx.shape, lambda i, gm: (0, 0))
    dest = pl.pallas_call(
        _dest_kernel,
        out_shape=jax.ShapeDtypeStruct(rank.shape, jnp.int32),
        grid_spec=pltpu.PrefetchScalarGridSpec(
            num_scalar_prefetch=1, grid=(1,), in_specs=[whole, whole], out_specs=whole),
        name="row_destinations",
    )(table, top_idx, rank)[:TOP_K]
    n_rows = n_tokens * TOP_K + N_EXPERTS * MOE_BLOCK
    return dest, table, n_rows


SC_CORES = 2
SC_SUBCORES = 16
SC_WORKERS = SC_CORES * SC_SUBCORES
SC_CHUNK = 64
SC_GATHER_RING = 3

def _sc_mesh():
    return plsc.VectorSubcoreMesh(core_axis_name="c", subcore_axis_name="s")


def _sc_worker():
    return lax.axis_index("s") * SC_CORES + lax.axis_index("c")


def _dispatch(h2, dest, n_rows):
    t, d = h2.shape
    per_w = t // SC_WORKERS
    n_chunks = per_w // SC_CHUNK
    assert per_w % (2 * SC_CHUNK) == 0
    idx = dest.reshape(TOP_K, SC_WORKERS, n_chunks, SC_CHUNK).transpose(1, 0, 2, 3)
    idx = idx.reshape(SC_WORKERS, TOP_K * n_chunks, SC_CHUNK)

    @functools.partial(
        pl.kernel, mesh=_sc_mesh(),
        out_type=jax.ShapeDtypeStruct((n_rows, d), h2.dtype),
        scratch_types=[pltpu.VMEM((TOP_K * n_chunks, SC_CHUNK), jnp.int32),
                       pltpu.VMEM((2, SC_CHUNK, d), h2.dtype),
                       pltpu.SemaphoreType.DMA((2,)),
                       pltpu.SemaphoreType.DMA((2,))],
        name="moe_dispatch")
    def run(h2_hbm, idx_hbm, xs_hbm, idx_v, rows_v, rsem, ssem):
        w = _sc_worker()
        pltpu.sync_copy(idx_hbm.at[w], idx_v)

        def read(g, b):
            src = h2_hbm.at[pl.ds(w * per_w + g * SC_CHUNK, SC_CHUNK)]
            return pltpu.make_async_copy(src, rows_v.at[b], rsem.at[b])

        def scatter(g, kk, b):
            dst = xs_hbm.at[idx_v.at[kk * n_chunks + g]]
            return pltpu.make_async_copy(rows_v.at[b], dst, ssem.at[b])

        read(0, 0).start()

        @pl.loop(0, n_chunks, step=2)
        def _(g0):
            for b in range(2):
                g = g0 + b
                read(g, b).wait()

                @pl.when(g + 1 < n_chunks)
                def _():
                    read(g + 1, 1 - b).start()

                for kk in range(TOP_K):
                    scatter(g, kk, b).start()
                for kk in range(TOP_K):
                    scatter(g, kk, b).wait()

    return run(h2, idx)


def _undispatch(ys, dest):
    t = dest.shape[1]
    d = ys.shape[1]
    n_out = t * TOP_K
    per_w = n_out // SC_WORKERS
    n_chunks = per_w // SC_CHUNK
    idx = dest.reshape(SC_WORKERS, n_chunks, SC_CHUNK)
    ring = SC_GATHER_RING

    @functools.partial(
        pl.kernel, mesh=_sc_mesh(),
        out_type=jax.ShapeDtypeStruct((n_out, d), ys.dtype),
        scratch_types=[pltpu.VMEM((n_chunks, SC_CHUNK), jnp.int32),
                       pltpu.VMEM((ring, SC_CHUNK, d), ys.dtype),
                       pltpu.SemaphoreType.DMA((ring,)),
                       pltpu.SemaphoreType.DMA((ring,))],
        name="moe_undispatch")
    def run(ys_hbm, idx_hbm, out_hbm, idx_v, rows_v, gsem, wsem):
        w = _sc_worker()
        pltpu.sync_copy(idx_hbm.at[w], idx_v)

        def gather(g):
            b = g % ring
            return pltpu.make_async_copy(ys_hbm.at[idx_v.at[g]], rows_v.at[b], gsem.at[b])

        def write(g):
            b = g % ring
            dst = out_hbm.at[pl.ds(w * per_w + g * SC_CHUNK, SC_CHUNK)]
            return pltpu.make_async_copy(rows_v.at[b], dst, wsem.at[b])

        for g in range(min(ring - 1, n_chunks)):
            gather(g).start()
        for g in range(n_chunks):
            gather(g).wait()
            ahead = g + ring - 1
            if ahead < n_chunks:
                if g >= 1:
                    write(g - 1).wait()
                gather(ahead).start()
            write(g).start()
        for g in range(max(n_chunks - ring, 0), n_chunks):
            write(g).wait()

    return run(ys, idx).reshape(TOP_K, t, d)


def kernel(x, c, positions, w_ada, b_ada, norm_mix_g, w_in, q_norm_g, w_uq, kv_norm_g, w_ukv,
           w_up_attn, conv_w, w_up_conv, w_o, norm_ffn_g, router_w, router_b, w_gu, b_gu,
           w_down, b_down, norm_final_g):
    batch, seq, d = x.shape
    t = batch * seq
    depth = w_ada.shape[0]
    assert d == D_MODEL and batch <= 8 and conv_w.shape[1:] == (CONV_K, CONV_WIDTH)
    assert seq % ROW_TILE == 0 and seq % POST_TILE == 0 and seq % ATT_BLOCK == 0
    assert t % (2 * SC_CHUNK * SC_WORKERS) == 0
    x2 = x.reshape(t, d)
    pos = positions.astype(F32).reshape(t // ROW_TILE, 1, ROW_TILE)
    freqs = _rope_freqs()
    c_pad = jnp.zeros((8, d), F32).at[:batch].set(c)

    for l in range(depth):
        ada = _ada(c_pad, w_ada[l], b_ada[l].reshape(1, -1))
        mod = ada[:batch].reshape(batch, 6, d)
        mod = jnp.concatenate([mod, jnp.zeros((batch, 2, d), F32)], axis=1)

        w_lat, w_t, wq2, wk2, wv = _prep_weights(w_in[l], w_uq[l], w_ukv[l])
        q, k, v, sga, gc = _pre(x2, mod, norm_mix_g[l].reshape(1, d), w_lat, w_t,
                                q_norm_g[l].reshape(1, -1), wq2,
                                kv_norm_g[l].reshape(1, -1), wk2, wv, pos, freqs, conv_w[l],
                                w_up_conv[l].astype(BF16), seq)
        merged = _attention(q, k, v, sga, gc, w_up_attn[l], batch, seq)

        rw_pad = jnp.concatenate([router_w[l], jnp.zeros((d, LANES - N_EXPERTS), F32)], axis=1)
        rb_pad = jnp.concatenate([router_b[l], jnp.full((LANES - N_EXPERTS,), NEG_BIG, F32)])
        rw_hi = rw_pad.astype(BF16)
        rw_lo = (rw_pad - rw_hi.astype(F32)).astype(BF16)
        x1, h2, idx_pad, gate_pad, rank_pad, counts = _post(
            merged, x2, mod, w_o[l], norm_ffn_g[l].reshape(1, d),
            jnp.concatenate([rw_hi, rw_lo], axis=1), rb_pad.reshape(1, LANES), seq)

        dest, group_table, n_rows = _route(
            idx_pad, rank_pad, counts[0, :N_EXPERTS], t)
        xs = _dispatch(h2, dest, n_rows)
        ys = _moe(group_table, xs, w_gu[l], b_gu[l].reshape(N_EXPERTS, 1, -1),
                  w_down[l], b_down[l].reshape(N_EXPERTS, 1, -1))
        y_kt = _undispatch(ys, dest)
        x2 = _final(x1, y_kt, gate_pad, mod, norm_final_g.reshape(1, d), seq, l == depth - 1)

    return x2.reshape(batch, seq, d)
```

```python
import functools
import math

import jax
import jax.numpy as jnp
from jax import lax
from jax.experimental import pallas as pl
from jax.experimental.pallas import tpu as pltpu
from jax.experimental.pallas import tpu_sc as plsc

D_MODEL = 1024
CHUNK = 64
N_HEADS = 8
Q_LORA = 256
KV_LORA = 128
QK_NOPE = 64
QK_ROPE = 32
V_HEAD = 64
QK_HEAD = QK_NOPE + QK_ROPE
ROPE_THETA = 10000.0
CONV_WIDTH = 512
CONV_K = 3
N_EXPERTS = 32
TOP_K = 4
D_EXPERT = 1024
SWIGLU_LIMIT = 7.0
SWIGLU_ALPHA = 1.702
MOE_BLOCK = 256
RMS_EPS = 1e-6

LANES = 128
HEAD_PAD = 128
NEG_BIG = -1e30
VMEM_LIMIT = 56 * 1024 * 1024

F32 = jnp.float32
BF16 = jnp.bfloat16

Q_PRESCALE = (QK_HEAD ** -0.5) * math.log2(math.e)

ADA_TILE = 1024
ROW_TILE = 1024
MOE_STEP_BLOCKS = 4
POST_TILE = 1024
POST_SUB = 512
ATT_BLOCK = 512
ATT_WIDE = 2


def _rms(x, g):
    ms = jnp.mean(x * x, axis=-1, keepdims=True)
    return x * lax.rsqrt(ms + RMS_EPS) * g


def _dot(a, b):
    return jnp.dot(a, b, preferred_element_type=F32)


def _dot_t(a, b_t):
    return lax.dot_general(a, b_t, (((1,), (1,)), ((), ())), preferred_element_type=F32)


PACKED = D_MODEL // 2


def _pack_row(x):
    return pltpu.pack_elementwise([x[:, :PACKED], x[:, PACKED:]], packed_dtype=BF16)


def _unpack_row(w):
    half = lambda i: pltpu.unpack_elementwise(w, index=i, packed_dtype=BF16, unpacked_dtype=F32)
    return jnp.concatenate([half(0), half(1)], axis=-1)


def _ada_kernel(c_ref, w_ref, b_ref, o_ref):
    c = c_ref[...]
    ca = (c * jax.nn.sigmoid(c)).astype(BF16)
    o_ref[...] = _dot(ca, w_ref[...].astype(BF16)) + b_ref[...]


def _ada(c_pad, w_ada, b_ada):
    n = w_ada.shape[1]
    tn = ADA_TILE
    return pl.pallas_call(
        _ada_kernel,
        out_shape=jax.ShapeDtypeStruct((c_pad.shape[0], n), F32),
        grid=(n // tn,),
        in_specs=[
            pl.BlockSpec(c_pad.shape, lambda j: (0, 0)),
            pl.BlockSpec((D_MODEL, tn), lambda j: (0, j)),
            pl.BlockSpec((1, tn), lambda j: (0, j)),
        ],
        out_specs=pl.BlockSpec((c_pad.shape[0], tn), lambda j: (0, j)),
        compiler_params=pltpu.CompilerParams(
            dimension_semantics=("arbitrary",), vmem_limit_bytes=VMEM_LIMIT),
        name="ada",
    )(c_pad, w_ada, b_ada)


_C_QLAT = 0
_C_KVLAT = _C_QLAT + Q_LORA
_C_KPE = _C_KVLAT + KV_LORA
_C_END = _C_KPE + HEAD_PAD
_F_KPE = Q_LORA + KV_LORA
_F_CONV = _F_KPE + QK_ROPE
_F_GATE = _F_CONV + 3 * CONV_WIDTH
_F_END = _F_GATE + 2 * D_MODEL


def _pre_kernel(tiles_per_seq, x_ref, mod_ref, g_ref, wlat_ref, wt_ref, qg_ref,
                wq_ref, kvg_ref, wk_ref, wv_ref, pos_ref, freq_ref, cw_ref, wuc_ref,
                q_ref, k_ref, v_ref, sga_ref, gc_ref, carry_ref):
    i = pl.program_id(0)
    tm = x_ref.shape[0]
    mod = mod_ref[...]
    h = _rms(x_ref[...], g_ref[...]) * (1.0 + mod[1:2]) + mod[0:1]
    hb = h.astype(BF16)

    ang = freq_ref[...] * pos_ref[...]
    cos_t, sin_t = jnp.cos(ang), jnp.sin(ang)
    ones_t = jnp.ones((QK_NOPE, tm), F32)
    zeros_t = jnp.zeros((QK_NOPE, tm), F32)
    pad_t = jnp.zeros((HEAD_PAD - QK_HEAD, tm), F32)
    cosf = jnp.concatenate([ones_t, cos_t, cos_t, pad_t], axis=0).T
    sinf = jnp.concatenate([zeros_t, -sin_t, sin_t, pad_t], axis=0).T

    first_half = lax.broadcasted_iota(jnp.int32, (tm, HEAD_PAD), 1) < QK_NOPE + QK_ROPE // 2

    def rope(slab):
        swapped = jnp.where(first_half, pltpu.roll(slab, HEAD_PAD - QK_ROPE // 2, 1),
                            pltpu.roll(slab, QK_ROPE // 2, 1))
        return slab * cosf + swapped * sinf

    gates = _dot_t(hb, wt_ref[_F_GATE:_F_END, :])
    sga_ref[...] = jax.nn.sigmoid(gates[:, 0:D_MODEL]).astype(BF16)
    g_conv = jax.nn.sigmoid(gates[:, D_MODEL:])

    small = _dot_t(hb, wlat_ref[...])
    q_lat = small[:, _C_QLAT:_C_KVLAT]
    kv_lat = small[:, _C_KVLAT:_C_KPE]
    kpe = rope(small[:, _C_KPE:_C_END])
    qn = _rms(q_lat, qg_ref[...]).astype(BF16)
    q = _dot(qn, wq_ref[...])
    q = jnp.concatenate([rope(q[:, hd * HEAD_PAD:(hd + 1) * HEAD_PAD]) for hd in range(N_HEADS)],
                        axis=-1)
    q_ref[...] = (q * Q_PRESCALE).astype(BF16)
    kvn = _rms(kv_lat, kvg_ref[...]).astype(BF16)
    k = _dot(kvn, wk_ref[...]) + jnp.concatenate([kpe] * N_HEADS, axis=-1)
    k_ref[...] = k.astype(BF16)
    lane = lax.broadcasted_iota(jnp.int32, (tm, N_HEADS * HEAD_PAD), 1)
    ones_col = jnp.where(lane % HEAD_PAD == V_HEAD, 1.0, 0.0)
    v_ref[...] = (_dot(kvn, wv_ref[...]) + ones_col).astype(BF16)

    ucb = _dot_t(hb, wt_ref[_F_CONV:_F_GATE, :])
    cu = ucb[:, 0:CONV_WIDTH] * ucb[:, CONV_WIDTH:2 * CONV_WIDTH]
    b_gate = ucb[:, 2 * CONV_WIDTH:3 * CONV_WIDTH]

    @pl.when(i % tiles_per_seq == 0)
    def _():
        carry_ref[...] = jnp.zeros_like(carry_ref)

    prev = carry_ref[...]
    row = lax.broadcasted_iota(jnp.int32, cu.shape, 0)
    cu1 = jnp.where(row == 0, prev[7:8], pltpu.roll(cu, 1, 0))
    cu2 = jnp.where(row == 0, prev[6:7], jnp.where(row == 1, prev[7:8], pltpu.roll(cu, 2, 0)))
    cw = cw_ref[...]
    z = cw[2:3] * cu + cw[1:2] * cu1 + cw[0:1] * cu2
    carry_ref[...] = cu[tm - 8:tm]
    c_branch = _dot((b_gate * z).astype(BF16), wuc_ref[...])
    gc_ref[...] = (g_conv * c_branch).astype(BF16)


def _pre(x2, mod, norm_g, w_lat, w_t, q_norm_g, wq2, kv_norm_g, wk2, wv, pos,
         freqs, conv_w, w_up_conv, seq):
    t = x2.shape[0]
    tm = ROW_TILE
    tiles_per_seq = seq // tm
    full = lambda a: pl.BlockSpec(a.shape, lambda i: (0,) * a.ndim)
    rows = lambda w: pl.BlockSpec((tm, w), lambda i: (i, 0))
    outs = [jax.ShapeDtypeStruct((t, N_HEADS * HEAD_PAD), BF16),
            jax.ShapeDtypeStruct((t, N_HEADS * HEAD_PAD), BF16),
            jax.ShapeDtypeStruct((t, N_HEADS * HEAD_PAD), BF16),
            jax.ShapeDtypeStruct((t, D_MODEL), BF16),
            jax.ShapeDtypeStruct((t, D_MODEL), BF16)]
    return pl.pallas_call(
        functools.partial(_pre_kernel, tiles_per_seq),
        out_shape=outs,
        grid=(t // tm,),
        in_specs=[
            rows(D_MODEL),
            pl.BlockSpec((None, 8, D_MODEL), lambda i: (i // tiles_per_seq, 0, 0)),
            full(norm_g), full(w_lat), full(w_t), full(q_norm_g), full(wq2),
            full(kv_norm_g), full(wk2), full(wv),
            pl.BlockSpec((None, 1, tm), lambda i: (i, 0, 0)), full(freqs), full(conv_w),
            full(w_up_conv),
        ],
        out_specs=[rows(N_HEADS * HEAD_PAD), rows(N_HEADS * HEAD_PAD), rows(N_HEADS * HEAD_PAD),
                   rows(D_MODEL), rows(D_MODEL)],
        scratch_shapes=[pltpu.VMEM((8, CONV_WIDTH), F32)],
        compiler_params=pltpu.CompilerParams(
            dimension_semantics=("arbitrary",), vmem_limit_bytes=VMEM_LIMIT),
        name="pre_mixer",
    )(x2, mod, norm_g, w_lat, w_t, q_norm_g, wq2, kv_norm_g, wk2, wv, pos,
      freqs, conv_w, w_up_conv)


def _attn_kernel(q_ref, k_ref, v_ref, sga_ref, gc_ref, wua_ref, o_ref, m_ref, acc_ref, wua_bf):
    i = pl.program_id(1)
    tq = q_ref.shape[0]

    @pl.when(jnp.logical_and(pl.program_id(0) == 0, i == 0))
    def _():
        wua_bf[...] = wua_ref[...].astype(BF16)

    def step(k0, tk, masked, first=False):
        if masked:
            rq = (lax.broadcasted_iota(jnp.int32, (tq, tk), 0) + (tk - tq)) // CHUNK
            ck = lax.broadcasted_iota(jnp.int32, (tq, tk), 1) // CHUNK
            allowed = ck <= rq
        for hd in range(N_HEADS):
            hs = slice(hd * HEAD_PAD, (hd + 1) * HEAD_PAD)
            s = lax.dot_general(q_ref[:, hs], k_ref[pl.ds(k0, tk), hs],
                                (((1,), (1,)), ((), ())), preferred_element_type=F32)
            if masked:
                s = jnp.where(allowed, s, NEG_BIG)
            s_max = s[:, 0:LANES]
            for c in range(1, tk // LANES):
                s_max = jnp.maximum(s_max, s[:, c * LANES:(c + 1) * LANES])
            m_new = jnp.broadcast_to(jnp.max(s_max, axis=-1, keepdims=True), (tq, LANES))
            if not first:
                m_old = m_ref[hd]
                m_new = jnp.maximum(m_old, m_new)
            p = jnp.concatenate(
                [jnp.exp2(s[:, c * LANES:(c + 1) * LANES] - m_new).astype(BF16)
                 for c in range(tk // LANES)], axis=-1)
            pv = _dot(p, v_ref[pl.ds(k0, tk), hs])
            acc_ref[hd] = pv if first else jnp.exp2(m_old - m_new) * acc_ref[hd] + pv
            m_ref[hd] = m_new

    wide = ATT_WIDE * tq
    n_wide = i // ATT_WIDE

    @pl.when(n_wide > 0)
    def _():
        step(0, wide, False, first=True)

    @pl.when(n_wide == 0)
    def _():
        m_ref[...] = jnp.full_like(m_ref, NEG_BIG)
        acc_ref[...] = jnp.zeros_like(acc_ref)

    def body(j, carry):
        step(pl.multiple_of(j * wide, wide), wide, False)
        return carry

    lax.fori_loop(1, n_wide, body, 0)

    def finish():
        heads = []
        for hd in range(N_HEADS):
            acc = acc_ref[hd]
            heads.append((acc[:, 0:V_HEAD] / acc[:, V_HEAD:V_HEAD + 1]).astype(BF16))
        a_branch = _dot(jnp.concatenate(heads, axis=-1), wua_bf[...])
        o_ref[...] = (sga_ref[...].astype(F32) * a_branch + gc_ref[...].astype(F32)).astype(BF16)

    for r in range(ATT_WIDE):
        @pl.when(i % ATT_WIDE == r)
        def _():
            step(pl.multiple_of((i - r) * tq, tq), (r + 1) * tq, True)
            finish()


def _attention(q, k, v, sga, gc, wua, batch, seq):
    tq = ATT_BLOCK
    nq = seq // tq
    q_rows = lambda w: pl.BlockSpec((tq, w), lambda b, i: (b * nq + i, 0))
    whole_seq = pl.BlockSpec((seq, N_HEADS * HEAD_PAD), lambda b, i: (b, 0))
    return pl.pallas_call(
        _attn_kernel,
        out_shape=jax.ShapeDtypeStruct((batch * seq, D_MODEL), BF16),
        grid=(batch, nq),
        in_specs=[q_rows(N_HEADS * HEAD_PAD), whole_seq, whole_seq, q_rows(D_MODEL),
                  q_rows(D_MODEL), pl.BlockSpec(wua.shape, lambda b, i: (0, 0))],
        out_specs=q_rows(D_MODEL),
        scratch_shapes=[pltpu.VMEM((N_HEADS, tq, LANES), F32),
                        pltpu.VMEM((N_HEADS, tq, LANES), F32),
                        pltpu.VMEM(wua.shape, BF16)],
        compiler_params=pltpu.CompilerParams(
            dimension_semantics=("arbitrary", "arbitrary"), vmem_limit_bytes=VMEM_LIMIT),
        name="attention",
    )(q, k, v, sga, gc, wua)


def _post_kernel(merged_ref, x_ref, mod_ref, wo_ref, g_ref, rw_ref, rb_ref,
                 x1_ref, h2_ref, idx_ref, gate_ref, rank_ref, cnt_out_ref, cnt_ref, lg_ref,
                 wo_bf):
    i = pl.program_id(0)

    @pl.when(i == 0)
    def _():
        cnt_ref[...] = jnp.zeros_like(cnt_ref)
        lg_ref[...] = jnp.zeros_like(lg_ref)
        wo_bf[...] = wo_ref[...].astype(BF16)

    prev_logits = lg_ref[(i + 1) % 2]
    counts = cnt_ref[...]
    routed = counts
    for r0 in range(0, x_ref.shape[0], POST_SUB):
        rs = slice(r0, r0 + POST_SUB)
        lg_ref[i % 2, rs, :] = _post_mix(rs, merged_ref, x_ref, mod_ref, wo_bf, g_ref, rw_ref,
                                         rb_ref, x1_ref, h2_ref)
        routed = _post_route(rs, prev_logits[rs, :], routed, idx_ref, gate_ref, rank_ref)
    counts = jnp.where(i > 0, routed, counts)
    cnt_ref[...] = counts
    cnt_out_ref[...] = counts.astype(jnp.int32)


def _post_mix(rs, merged_ref, x_ref, mod_ref, wo_ref, g_ref, rw_ref, rb_ref, x1_ref, h2_ref):
    mod = mod_ref[...]
    mix = _dot(merged_ref[rs, :], wo_ref[...])
    x1 = x_ref[rs, :] + mod[2:3] * mix
    x1_ref[rs, :] = x1
    h2 = _rms(x1, g_ref[...]) * (1.0 + mod[4:5]) + mod[3:4]
    h2_ref[rs, :] = _pack_row(h2)

    both = _dot(h2.astype(BF16), rw_ref[...])
    return both[:, :LANES] + both[:, LANES:] + rb_ref[...]


def _post_route(rs, logits, counts, idx_ref, gate_ref, rank_ref):
    lane = lax.broadcasted_iota(jnp.int32, logits.shape, 1).astype(F32)
    work = logits
    vals, idxs = [], []
    for _ in range(TOP_K):
        mk = jnp.max(work, axis=-1, keepdims=True)
        ik = jnp.min(jnp.where(work == mk, lane, float(LANES)), axis=-1, keepdims=True)
        vals.append(mk)
        idxs.append(ik)
        work = jnp.where(lane == ik, -jnp.inf, work)
    es = [jnp.exp(vk - vals[0]) for vk in vals]
    denom = es[0] + es[1] + es[2] + es[3]
    tm = logits.shape[0]
    chosen = jnp.zeros(logits.shape, F32)
    for kk in range(TOP_K):
        chosen = chosen + jnp.where(lane == idxs[kk], 1.0, 0.0)
    r_i = lax.broadcasted_iota(jnp.int32, (tm, tm), 0)
    c_i = lax.broadcasted_iota(jnp.int32, (tm, tm), 1)
    earlier = jnp.where(c_i < r_i, 1.0, 0.0).astype(BF16)
    before = _dot(earlier, chosen.astype(BF16)) + counts[0:1]

    idx_out = jnp.zeros(logits.shape, F32)
    gate_out = jnp.zeros(logits.shape, F32)
    rank_out = jnp.zeros(logits.shape, F32)
    for kk in range(TOP_K):
        rank_k = jnp.sum(jnp.where(lane == idxs[kk], before, 0.0), axis=-1, keepdims=True)
        idx_out = jnp.where(lane == kk, idxs[kk], idx_out)
        gate_out = jnp.where(lane == kk, es[kk] / denom, gate_out)
        rank_out = jnp.where(lane == kk, rank_k, rank_out)
    gate_ref[rs, :] = gate_out
    idx_ref[:, rs] = idx_out.T[0:8].astype(jnp.int32)
    rank_ref[:, rs] = rank_out.T[0:8].astype(jnp.int32)
    return counts + jnp.sum(chosen, axis=0, keepdims=True)


def _post(merged, x2, mod, wo, norm_g, rw_hl, rb_pad, seq):
    t = x2.shape[0]
    tm = POST_TILE
    tiles_per_seq = seq // tm
    n_tiles = t // tm
    full = lambda a: pl.BlockSpec(a.shape, lambda i: (0,) * a.ndim)
    mix_tile = lambda i: jnp.minimum(i, n_tiles - 1)
    route_tile = lambda i: jnp.maximum(i - 1, 0)
    rows = lambda w: pl.BlockSpec((tm, w), lambda i: (mix_tile(i), 0))
    outs = [jax.ShapeDtypeStruct((t, D_MODEL), F32),
            jax.ShapeDtypeStruct((t, PACKED), jnp.uint32),
            jax.ShapeDtypeStruct((8, t), jnp.int32),
            jax.ShapeDtypeStruct((t, LANES), F32),
            jax.ShapeDtypeStruct((8, t), jnp.int32),
            jax.ShapeDtypeStruct((8, LANES), jnp.int32)]
    slots = pl.BlockSpec((8, tm), lambda i: (0, route_tile(i)))
    return pl.pallas_call(
        _post_kernel,
        out_shape=outs,
        grid=(n_tiles + 1,),
        in_specs=[
            rows(D_MODEL), rows(D_MODEL),
            pl.BlockSpec((None, 8, D_MODEL), lambda i: (mix_tile(i) // tiles_per_seq, 0, 0)),
            full(wo), full(norm_g), full(rw_hl), full(rb_pad),
        ],
        out_specs=[rows(D_MODEL), rows(PACKED), slots,
                   pl.BlockSpec((tm, LANES), lambda i: (route_tile(i), 0)), slots,
                   pl.BlockSpec((8, LANES), lambda i: (0, 0))],
        scratch_shapes=[pltpu.VMEM((8, LANES), F32), pltpu.VMEM((2, tm, LANES), F32),
                        pltpu.VMEM(wo.shape, BF16)],
        compiler_params=pltpu.CompilerParams(
            dimension_semantics=("arbitrary",), vmem_limit_bytes=VMEM_LIMIT),
        name="post_mixer",
    )(merged, x2, mod, wo, norm_g, rw_hl, rb_pad)


_GM_COUNT, _GM_FIRST, _GM_BLOCKS = range(3)
_ST_EXPERT, _ST_SLOT = range(2)


def _moe_kernel(gm_ref, xs_ref, wgu_hbm, bgu_ref, wd_hbm, bd_ref, o_ref,
                wgu_f, wd_f, wgu_bf, wd_bf, sem, st_ref):
    weights = (wgu_hbm, wd_hbm, wgu_f, wd_f, wgu_bf, wd_bf, sem)
    rows = (xs_ref, bgu_ref, bd_ref, o_ref, wgu_bf, wd_bf)
    b0 = pl.program_id(0) * MOE_STEP_BLOCKS
    e, in_group, used = _moe_enter(b0, gm_ref, st_ref, *weights)
    together = jnp.logical_and(used, in_group + MOE_STEP_BLOCKS <= gm_ref[_GM_BLOCKS, e])

    @pl.when(together)
    def _():
        _moe_rows(slice(0, MOE_STEP_BLOCKS * MOE_BLOCK), e, in_group, True, gm_ref, *rows)

    @pl.when(jnp.logical_not(together))
    def _():
        for r in range(0, MOE_STEP_BLOCKS, 2):
            if r == 0:
                e_r, in_group_r, used_r = e, in_group, used
            else:
                e_r, in_group_r, used_r = _moe_enter(b0 + r, gm_ref, st_ref, *weights)
            paired = jnp.logical_and(used_r, in_group_r + 2 <= gm_ref[_GM_BLOCKS, e_r])

            @pl.when(paired)
            def _():
                _moe_rows(slice(r * MOE_BLOCK, (r + 2) * MOE_BLOCK), e_r, in_group_r, True,
                          gm_ref, *rows)

            @pl.when(jnp.logical_not(paired))
            def _():
                _moe_rows(slice(r * MOE_BLOCK, (r + 1) * MOE_BLOCK), e_r, in_group_r, used_r,
                          gm_ref, *rows)
                e_n, in_group_n, used_n = _moe_enter(b0 + r + 1, gm_ref, st_ref, *weights)
                _moe_rows(slice((r + 1) * MOE_BLOCK, (r + 2) * MOE_BLOCK), e_n, in_group_n,
                          used_n, gm_ref, *rows)


def _moe_enter(b, gm_ref, st_ref, wgu_hbm, wd_hbm, wgu_f, wd_f, wgu_bf, wd_bf, sem):
    def weight_copies(expert, sl):
        return (pltpu.make_async_copy(wgu_hbm.at[expert], wgu_f.at[sl], sem.at[0, sl]),
                pltpu.make_async_copy(wd_hbm.at[expert], wd_f.at[sl], sem.at[1, sl]))

    def next_group(e):
        return lax.while_loop(
            lambda k: jnp.logical_and(k < N_EXPERTS,
                                      gm_ref[_GM_BLOCKS, jnp.minimum(k, N_EXPERTS - 1)] == 0),
            lambda k: k + 1, e)

    @pl.when(b == 0)
    def _():
        e0 = next_group(0)
        st_ref[_ST_EXPERT] = e0
        st_ref[_ST_SLOT] = 1
        for cp in weight_copies(e0, 0):
            cp.start()

    e_prev = st_ref[_ST_EXPERT]
    past = b >= gm_ref[_GM_FIRST, e_prev] + gm_ref[_GM_BLOCKS, e_prev]
    e = jnp.minimum(jnp.where(past, next_group(e_prev + 1), e_prev), N_EXPERTS - 1)
    st_ref[_ST_EXPERT] = e
    in_group = b - gm_ref[_GM_FIRST, e]
    used = jnp.logical_and(in_group >= 0, in_group < gm_ref[_GM_BLOCKS, e])

    @pl.when(jnp.logical_and(used, in_group == 0))
    def _():
        slot = 1 - st_ref[_ST_SLOT]
        st_ref[_ST_SLOT] = slot
        nxt = next_group(e + 1)

        @pl.when(nxt < N_EXPERTS)
        def _():
            for cp in weight_copies(nxt, 1 - slot):
                cp.start(priority=1)

        for cp in weight_copies(e, slot):
            cp.wait()
        wgu_bf[...] = wgu_f[slot].astype(BF16)
        wd_bf[...] = wd_f[slot].astype(BF16)

    return e, in_group, used


def _moe_rows(rs, e, in_group, used, gm_ref, xs_ref, bgu_ref, bd_ref, o_ref, wgu_bf, wd_bf):
    n_valid = gm_ref[_GM_COUNT, e] - in_group * MOE_BLOCK

    def ffn(r):
        n = r.stop - r.start
        row = lax.broadcasted_iota(jnp.int32, (n, PACKED), 0)
        xs = _unpack_row(jnp.where(row < n_valid, xs_ref[r, :], 0)).astype(BF16)
        gu = _dot(xs, wgu_bf[...]) + bgu_ref[e]
        gate = jnp.minimum(gu[:, :D_EXPERT], SWIGLU_LIMIT)
        up = jnp.clip(gu[:, D_EXPERT:], -SWIGLU_LIMIT, SWIGLU_LIMIT)
        act = (up + 1.0) * (gate * jax.nn.sigmoid(gate * SWIGLU_ALPHA))
        o_ref[r, :] = _pack_row(_dot(act.astype(BF16), wd_bf[...]) + bd_ref[e])

    def zeros(r):
        o_ref[r, :] = jnp.zeros((r.stop - r.start, PACKED), o_ref.dtype)

    if used is True:
        ffn(rs)
        return

    mid = rs.start + (rs.stop - rs.start) // 2
    short = n_valid <= mid - rs.start

    @pl.when(jnp.logical_and(used, jnp.logical_not(short)))
    def _():
        ffn(rs)

    @pl.when(jnp.logical_and(used, short))
    def _():
        ffn(slice(rs.start, mid))
        zeros(slice(mid, rs.stop))

    @pl.when(jnp.logical_not(used))
    def _():
        zeros(rs)


def _moe(group_table, xs, w_gu, b_gu, w_down, b_down):
    n_rows = xs.shape[0]
    step_rows = MOE_STEP_BLOCKS * MOE_BLOCK
    assert n_rows % step_rows == 0
    grid_spec = pltpu.PrefetchScalarGridSpec(
        num_scalar_prefetch=1,
        grid=(n_rows // step_rows,),
        in_specs=[
            pl.BlockSpec((step_rows, PACKED), lambda b, gm: (b, 0)),
            pl.BlockSpec(memory_space=pl.ANY),
            pl.BlockSpec(b_gu.shape, lambda b, gm: (0, 0, 0)),
            pl.BlockSpec(memory_space=pl.ANY),
            pl.BlockSpec(b_down.shape, lambda b, gm: (0, 0, 0)),
        ],
        out_specs=pl.BlockSpec((step_rows, PACKED), lambda b, gm: (b, 0)),
        scratch_shapes=[pltpu.VMEM((2, D_MODEL, 2 * D_EXPERT), F32),
                        pltpu.VMEM((2, D_EXPERT, D_MODEL), F32),
                        pltpu.VMEM((D_MODEL, 2 * D_EXPERT), BF16),
                        pltpu.VMEM((D_EXPERT, D_MODEL), BF16),
                        pltpu.SemaphoreType.DMA((2, 2)),
                        pltpu.SMEM((2,), jnp.int32)],
    )
    return pl.pallas_call(
        _moe_kernel,
        out_shape=jax.ShapeDtypeStruct((n_rows, PACKED), jnp.uint32),
        grid_spec=grid_spec,
        compiler_params=pltpu.CompilerParams(
            dimension_semantics=("arbitrary",), vmem_limit_bytes=VMEM_LIMIT),
        name="moe_experts",
    )(group_table, xs, w_gu, b_gu, w_down, b_down)


def _final_kernel(last_layer, x1_ref, y_ref, gate_ref, mod_ref, g_ref, o_ref):
    mod = mod_ref[...]
    gate = gate_ref[...]
    ffn = gate[:, 0:1] * _unpack_row(y_ref[0])
    for kk in range(1, TOP_K):
        ffn = ffn + gate[:, kk:kk + 1] * _unpack_row(y_ref[kk])
    x = x1_ref[...] + mod[5:6] * ffn
    o_ref[...] = _rms(x, g_ref[...]) if last_layer else x


def _final(x1, y_kt, gate, mod, norm_g, seq, last_layer):
    t = x1.shape[0]
    tm = ROW_TILE
    tiles_per_seq = seq // tm
    rows = lambda w: pl.BlockSpec((tm, w), lambda i: (i, 0))
    return pl.pallas_call(
        functools.partial(_final_kernel, last_layer),
        out_shape=jax.ShapeDtypeStruct((t, D_MODEL), F32),
        grid=(t // tm,),
        in_specs=[
            rows(D_MODEL), pl.BlockSpec((TOP_K, tm, PACKED), lambda i: (0, i, 0)), rows(LANES),
            pl.BlockSpec((None, 8, D_MODEL), lambda i: (i // tiles_per_seq, 0, 0)),
            pl.BlockSpec(norm_g.shape, lambda i: (0, 0)),
        ],
        out_specs=rows(D_MODEL),
        compiler_params=pltpu.CompilerParams(
            dimension_semantics=("arbitrary",), vmem_limit_bytes=VMEM_LIMIT),
        name="combine_final",
    )(x1, y_kt, gate, mod, norm_g)


def _prep_weights(w_in, w_uq, w_ukv):
    assert w_in.shape[1] == _F_END
    w_t = jnp.swapaxes(w_in, 0, 1).astype(BF16)
    z = lambda n: jnp.zeros((n, w_t.shape[1]), BF16)
    w_lat = jnp.concatenate([w_t[:_F_KPE], z(QK_NOPE), w_t[_F_KPE:_F_CONV], z(HEAD_PAD - QK_HEAD)],
                            axis=0)

    wq = w_uq.reshape(Q_LORA, N_HEADS, QK_HEAD)
    zq = lambda n: jnp.zeros((Q_LORA, N_HEADS, n), w_uq.dtype)
    wq2 = jnp.concatenate([wq, zq(HEAD_PAD - QK_HEAD)], axis=-1)
    wq2 = wq2.reshape(Q_LORA, N_HEADS * HEAD_PAD).astype(BF16)

    wkv = w_ukv.reshape(KV_LORA, N_HEADS, QK_NOPE + V_HEAD)
    wk2 = jnp.concatenate([wkv[..., :QK_NOPE],
                           jnp.zeros((KV_LORA, N_HEADS, HEAD_PAD - QK_NOPE), w_ukv.dtype)], axis=-1)
    wk2 = wk2.reshape(KV_LORA, N_HEADS * HEAD_PAD).astype(BF16)
    wv = jnp.concatenate([wkv[..., QK_NOPE:],
                          jnp.zeros((KV_LORA, N_HEADS, HEAD_PAD - V_HEAD), w_ukv.dtype)], axis=-1)
    wv = wv.reshape(KV_LORA, N_HEADS * HEAD_PAD).astype(BF16)
    return w_lat, w_t, wq2, wk2, wv


def _rope_freqs():
    inv_freq = 1.0 / (ROPE_THETA ** (jnp.arange(0, QK_ROPE, 2, dtype=F32) / QK_ROPE))
    return inv_freq.reshape(QK_ROPE // 2, 1)


def _dest_kernel(gm_ref, idx_ref, rank_ref, o_ref):
    idx = idx_ref[...]
    dest = rank_ref[...]
    for e in range(N_EXPERTS):
        dest = dest + jnp.where(idx == e, gm_ref[_GM_FIRST, e] * MOE_BLOCK, 0)
    o_ref[...] = dest


def _route(top_idx, rank, counts, n_tokens):
    blocks = (counts + MOE_BLOCK - 1) // MOE_BLOCK
    first_block = jnp.cumsum(blocks) - blocks
    table = jnp.stack([counts, first_block, blocks]).astype(jnp.int32)
    whole = pl.BlockSpec(top_idx.shape, lambda i, gm: (0, 0))
    dest = pl.pallas_call(
        _dest_kernel,
        out_shape=jax.ShapeDtypeStruct(rank.shape, jnp.int32),
        grid_spec=pltpu.PrefetchScalarGridSpec(
            num_scalar_prefetch=1, grid=(1,), in_specs=[whole, whole], out_specs=whole),
        name="row_destinations",
    )(table, top_idx, rank)[:TOP_K]
    n_rows = n_tokens * TOP_K + N_EXPERTS * MOE_BLOCK
    return dest, table, n_rows


SC_CORES = 2
SC_SUBCORES = 16
SC_WORKERS = SC_CORES * SC_SUBCORES
SC_CHUNK = 64
SC_GATHER_RING = 3

def _sc_mesh():
    return plsc.VectorSubcoreMesh(core_axis_name="c", subcore_axis_name="s")


def _sc_worker():
    return lax.axis_index("s") * SC_CORES + lax.axis_index("c")


def _dispatch(h2, dest, n_rows):
    t, d = h2.shape
    per_w = t // SC_WORKERS
    n_chunks = per_w // SC_CHUNK
    assert per_w % (2 * SC_CHUNK) == 0
    idx = dest.reshape(TOP_K, SC_WORKERS, n_chunks, SC_CHUNK).transpose(1, 0, 2, 3)
    idx = idx.reshape(SC_WORKERS, TOP_K * n_chunks, SC_CHUNK)

    @functools.partial(
        pl.kernel, mesh=_sc_mesh(),
        out_type=jax.ShapeDtypeStruct((n_rows, d), h2.dtype),
        scratch_types=[pltpu.VMEM((TOP_K * n_chunks, SC_CHUNK), jnp.int32),
                       pltpu.VMEM((2, SC_CHUNK, d), h2.dtype),
                       pltpu.SemaphoreType.DMA((2,)),
                       pltpu.SemaphoreType.DMA((2,))],
        name="moe_dispatch")
    def run(h2_hbm, idx_hbm, xs_hbm, idx_v, rows_v, rsem, ssem):
        w = _sc_worker()
        pltpu.sync_copy(idx_hbm.at[w], idx_v)

        def read(g, b):
            src = h2_hbm.at[pl.ds(w * per_w + g * SC_CHUNK, SC_CHUNK)]
            return pltpu.make_async_copy(src, rows_v.at[b], rsem.at[b])

        def scatter(g, kk, b):
            dst = xs_hbm.at[idx_v.at[kk * n_chunks + g]]
            return pltpu.make_async_copy(rows_v.at[b], dst, ssem.at[b])

        read(0, 0).start()

        @pl.loop(0, n_chunks, step=2)
        def _(g0):
            for b in range(2):
                g = g0 + b
                read(g, b).wait()

                @pl.when(g + 1 < n_chunks)
                def _():
                    read(g + 1, 1 - b).start()

                for kk in range(TOP_K):
                    scatter(g, kk, b).start()
                for kk in range(TOP_K):
                    scatter(g, kk, b).wait()

    return run(h2, idx)


def _undispatch(ys, dest):
    t = dest.shape[1]
    d = ys.shape[1]
    n_out = t * TOP_K
    per_w = n_out // SC_WORKERS
    n_chunks = per_w // SC_CHUNK
    idx = dest.reshape(SC_WORKERS, n_chunks, SC_CHUNK)
    ring = SC_GATHER_RING

    @functools.partial(
        pl.kernel, mesh=_sc_mesh(),
        out_type=jax.ShapeDtypeStruct((n_out, d), ys.dtype),
        scratch_types=[pltpu.VMEM((n_chunks, SC_CHUNK), jnp.int32),
                       pltpu.VMEM((ring, SC_CHUNK, d), ys.dtype),
                       pltpu.SemaphoreType.DMA((ring,)),
                       pltpu.SemaphoreType.DMA((ring,))],
        name="moe_undispatch")
    def run(ys_hbm, idx_hbm, out_hbm, idx_v, rows_v, gsem, wsem):
        w = _sc_worker()
        pltpu.sync_copy(idx_hbm.at[w], idx_v)

        def gather(g):
            b = g % ring
            return pltpu.make_async_copy(ys_hbm.at[idx_v.at[g]], rows_v.at[b], gsem.at[b])

        def write(g):
            b = g % ring
            dst = out_hbm.at[pl.ds(w * per_w + g * SC_CHUNK, SC_CHUNK)]
            return pltpu.make_async_copy(rows_v.at[b], dst, wsem.at[b])

        for g in range(min(ring - 1, n_chunks)):
            gather(g).start()
        for g in range(n_chunks):
            gather(g).wait()
            ahead = g + ring - 1
            if ahead < n_chunks:
                if g >= 1:
                    write(g - 1).wait()
                gather(ahead).start()
            write(g).start()
        for g in range(max(n_chunks - ring, 0), n_chunks):
            write(g).wait()

    return run(ys, idx).reshape(TOP_K, t, d)


def kernel(x, c, positions, w_ada, b_ada, norm_mix_g, w_in, q_norm_g, w_uq, kv_norm_g, w_ukv,
           w_up_attn, conv_w, w_up_conv, w_o, norm_ffn_g, router_w, router_b, w_gu, b_gu,
           w_down, b_down, norm_final_g):
    batch, seq, d = x.shape
    t = batch * seq
    depth = w_ada.shape[0]
    assert d == D_MODEL and batch <= 8 and conv_w.shape[1:] == (CONV_K, CONV_WIDTH)
    assert seq % ROW_TILE == 0 and seq % POST_TILE == 0 and seq % ATT_BLOCK == 0
    assert t % (2 * SC_CHUNK * SC_WORKERS) == 0
    x2 = x.reshape(t, d)
    pos = positions.astype(F32).reshape(t // ROW_TILE, 1, ROW_TILE)
    freqs = _rope_freqs()
    c_pad = jnp.zeros((8, d), F32).at[:batch].set(c)

    for l in range(depth):
        ada = _ada(c_pad, w_ada[l], b_ada[l].reshape(1, -1))
        mod = ada[:batch].reshape(batch, 6, d)
        mod = jnp.concatenate([mod, jnp.zeros((batch, 2, d), F32)], axis=1)

        w_lat, w_t, wq2, wk2, wv = _prep_weights(w_in[l], w_uq[l], w_ukv[l])
        q, k, v, sga, gc = _pre(x2, mod, norm_mix_g[l].reshape(1, d), w_lat, w_t,
                                q_norm_g[l].reshape(1, -1), wq2,
                                kv_norm_g[l].reshape(1, -1), wk2, wv, pos, freqs, conv_w[l],
                                w_up_conv[l].astype(BF16), seq)
        merged = _attention(q, k, v, sga, gc, w_up_attn[l], batch, seq)

        rw_pad = jnp.concatenate([router_w[l], jnp.zeros((d, LANES - N_EXPERTS), F32)], axis=1)
        rb_pad = jnp.concatenate([router_b[l], jnp.full((LANES - N_EXPERTS,), NEG_BIG, F32)])
        rw_hi = rw_pad.astype(BF16)
        rw_lo = (rw_pad - rw_hi.astype(F32)).astype(BF16)
        x1, h2, idx_pad, gate_pad, rank_pad, counts = _post(
            merged, x2, mod, w_o[l], norm_ffn_g[l].reshape(1, d),
            jnp.concatenate([rw_hi, rw_lo], axis=1), rb_pad.reshape(1, LANES), seq)

        dest, group_table, n_rows = _route(
            idx_pad, rank_pad, counts[0, :N_EXPERTS], t)
        xs = _dispatch(h2, dest, n_rows)
        ys = _moe(group_table, xs, w_gu[l], b_gu[l].reshape(N_EXPERTS, 1, -1),
                  w_down[l], b_down[l].reshape(N_EXPERTS, 1, -1))
        y_kt = _undispatch(ys, dest)
        x2 = _final(x1, y_kt, gate_pad, mod, norm_final_g.reshape(1, d), seq, l == depth - 1)

    return x2.reshape(batch, seq, d)
```

```python
import functools
import math

import jax
import jax.numpy as jnp
from jax import lax
from jax.experimental import pallas as pl
from jax.experimental.pallas import tpu as pltpu
from jax.experimental.pallas import tpu_sc as plsc

D_MODEL = 1024
CHUNK = 64
N_HEADS = 8
Q_LORA = 256
KV_LORA = 128
QK_NOPE = 64
QK_ROPE = 32
V_HEAD = 64
QK_HEAD = QK_NOPE + QK_ROPE
ROPE_THETA = 10000.0
CONV_WIDTH = 512
CONV_K = 3
N_EXPERTS = 32
TOP_K = 4
D_EXPERT = 1024
SWIGLU_LIMIT = 7.0
SWIGLU_ALPHA = 1.702
MOE_BLOCK = 256
RMS_EPS = 1e-6

LANES = 128
HEAD_PAD = 128
NEG_BIG = -1e30
VMEM_LIMIT = 56 * 1024 * 1024

F32 = jnp.float32
BF16 = jnp.bfloat16

Q_PRESCALE = (QK_HEAD ** -0.5) * math.log2(math.e)

ADA_TILE = 1024
ROW_TILE = 1024
MOE_STEP_BLOCKS = 4
POST_TILE = 1024
POST_SUB = 512
ATT_BLOCK = 512
ATT_WIDE = 2


def _rms(x, g):
    ms = jnp.mean(x * x, axis=-1, keepdims=True)
    return x * lax.rsqrt(ms + RMS_EPS) * g


def _dot(a, b):
    return jnp.dot(a, b, preferred_element_type=F32)


def _dot_t(a, b_t):
    return lax.dot_general(a, b_t, (((1,), (1,)), ((), ())), preferred_element_type=F32)


PACKED = D_MODEL // 2


def _pack_row(x):
    return pltpu.pack_elementwise([x[:, :PACKED], x[:, PACKED:]], packed_dtype=BF16)


def _unpack_row(w):
    half = lambda i: pltpu.unpack_elementwise(w, index=i, packed_dtype=BF16, unpacked_dtype=F32)
    return jnp.concatenate([half(0), half(1)], axis=-1)


def _ada_kernel(c_ref, w_ref, b_ref, o_ref):
    c = c_ref[...]
    ca = (c * jax.nn.sigmoid(c)).astype(BF16)
    o_ref[...] = _dot(ca, w_ref[...].astype(BF16)) + b_ref[...]


def _ada(c_pad, w_ada, b_ada):
    n = w_ada.shape[1]
    tn = ADA_TILE
    return pl.pallas_call(
        _ada_kernel,
        out_shape=jax.ShapeDtypeStruct((c_pad.shape[0], n), F32),
        grid=(n // tn,),
        in_specs=[
            pl.BlockSpec(c_pad.shape, lambda j: (0, 0)),
            pl.BlockSpec((D_MODEL, tn), lambda j: (0, j)),
            pl.BlockSpec((1, tn), lambda j: (0, j)),
        ],
        out_specs=pl.BlockSpec((c_pad.shape[0], tn), lambda j: (0, j)),
        compiler_params=pltpu.CompilerParams(
            dimension_semantics=("arbitrary",), vmem_limit_bytes=VMEM_LIMIT),
        name="ada",
    )(c_pad, w_ada, b_ada)


_C_QLAT = 0
_C_KVLAT = _C_QLAT + Q_LORA
_C_KPE = _C_KVLAT + KV_LORA
_C_END = _C_KPE + HEAD_PAD
_F_KPE = Q_LORA + KV_LORA
_F_CONV = _F_KPE + QK_ROPE
_F_GATE = _F_CONV + 3 * CONV_WIDTH
_F_END = _F_GATE + 2 * D_MODEL


def _pre_kernel(tiles_per_seq, x_ref, mod_ref, g_ref, wlat_ref, wt_ref, qg_ref,
                wq_ref, kvg_ref, wk_ref, wv_ref, pos_ref, freq_ref, cw_ref, wuc_ref,
                q_ref, k_ref, v_ref, sga_ref, gc_ref, carry_ref):
    i = pl.program_id(0)
    tm = x_ref.shape[0]
    mod = mod_ref[...]
    h = _rms(x_ref[...], g_ref[...]) * (1.0 + mod[1:2]) + mod[0:1]
    hb = h.astype(BF16)

    ang = freq_ref[...] * pos_ref[...]
    cos_t, sin_t = jnp.cos(ang), jnp.sin(ang)
    ones_t = jnp.ones((QK_NOPE, tm), F32)
    zeros_t = jnp.zeros((QK_NOPE, tm), F32)
    pad_t = jnp.zeros((HEAD_PAD - QK_HEAD, tm), F32)
    cosf = jnp.concatenate([ones_t, cos_t, cos_t, pad_t], axis=0).T
    sinf = jnp.concatenate([zeros_t, -sin_t, sin_t, pad_t], axis=0).T

    first_half = lax.broadcasted_iota(jnp.int32, (tm, HEAD_PAD), 1) < QK_NOPE + QK_ROPE // 2

    def rope(slab):
        swapped = jnp.where(first_half, pltpu.roll(slab, HEAD_PAD - QK_ROPE // 2, 1),
                            pltpu.roll(slab, QK_ROPE // 2, 1))
        return slab * cosf + swapped * sinf

    gates = _dot_t(hb, wt_ref[_F_GATE:_F_END, :])
    sga_ref[...] = jax.nn.sigmoid(gates[:, 0:D_MODEL]).astype(BF16)
    g_conv = jax.nn.sigmoid(gates[:, D_MODEL:])

    small = _dot_t(hb, wlat_ref[...])
    q_lat = small[:, _C_QLAT:_C_KVLAT]
    kv_lat = small[:, _C_KVLAT:_C_KPE]
    kpe = rope(small[:, _C_KPE:_C_END])
    qn = _rms(q_lat, qg_ref[...]).astype(BF16)
    q = _dot(qn, wq_ref[...])
    q = jnp.concatenate([rope(q[:, hd * HEAD_PAD:(hd + 1) * HEAD_PAD]) for hd in range(N_HEADS)],
                        axis=-1)
    q_ref[...] = (q * Q_PRESCALE).astype(BF16)
    kvn = _rms(kv_lat, kvg_ref[...]).astype(BF16)
    k = _dot(kvn, wk_ref[...]) + jnp.concatenate([kpe] * N_HEADS, axis=-1)
    k_ref[...] = k.astype(BF16)
    lane = lax.broadcasted_iota(jnp.int32, (tm, N_HEADS * HEAD_PAD), 1)
    ones_col = jnp.where(lane % HEAD_PAD == V_HEAD, 1.0, 0.0)
    v_ref[...] = (_dot(kvn, wv_ref[...]) + ones_col).astype(BF16)

    ucb = _dot_t(hb, wt_ref[_F_CONV:_F_GATE, :])
    cu = ucb[:, 0:CONV_WIDTH] * ucb[:, CONV_WIDTH:2 * CONV_WIDTH]
    b_gate = ucb[:, 2 * CONV_WIDTH:3 * CONV_WIDTH]

    @pl.when(i % tiles_per_seq == 0)
    def _():
        carry_ref[...] = jnp.zeros_like(carry_ref)

    prev = carry_ref[...]
    row = lax.broadcasted_iota(jnp.int32, cu.shape, 0)
    cu1 = jnp.where(row == 0, prev[7:8], pltpu.roll(cu, 1, 0))
    cu2 = jnp.where(row == 0, prev[6:7], jnp.where(row == 1, prev[7:8], pltpu.roll(cu, 2, 0)))
    cw = cw_ref[...]
    z = cw[2:3] * cu + cw[1:2] * cu1 + cw[0:1] * cu2
    carry_ref[...] = cu[tm - 8:tm]
    c_branch = _dot((b_gate * z).astype(BF16), wuc_ref[...])
    gc_ref[...] = (g_conv * c_branch).astype(BF16)


def _pre(x2, mod, norm_g, w_lat, w_t, q_norm_g, wq2, kv_norm_g, wk2, wv, pos,
         freqs, conv_w, w_up_conv, seq):
    t = x2.shape[0]
    tm = ROW_TILE
    tiles_per_seq = seq // tm
    full = lambda a: pl.BlockSpec(a.shape, lambda i: (0,) * a.ndim)
    rows = lambda w: pl.BlockSpec((tm, w), lambda i: (i, 0))
    outs = [jax.ShapeDtypeStruct((t, N_HEADS * HEAD_PAD), BF16),
            jax.ShapeDtypeStruct((t, N_HEADS * HEAD_PAD), BF16),
            jax.ShapeDtypeStruct((t, N_HEADS * HEAD_PAD), BF16),
            jax.ShapeDtypeStruct((t, D_MODEL), BF16),
            jax.ShapeDtypeStruct((t, D_MODEL), BF16)]
    return pl.pallas_call(
        functools.partial(_pre_kernel, tiles_per_seq),
        out_shape=outs,
        grid=(t // tm,),
        in_specs=[
            rows(D_MODEL),
            pl.BlockSpec((None, 8, D_MODEL), lambda i: (i // tiles_per_seq, 0, 0)),
            full(norm_g), full(w_lat), full(w_t), full(q_norm_g), full(wq2),
            full(kv_norm_g), full(wk2), full(wv),
            pl.BlockSpec((None, 1, tm), lambda i: (i, 0, 0)), full(freqs), full(conv_w),
            full(w_up_conv),
        ],
        out_specs=[rows(N_HEADS * HEAD_PAD), rows(N_HEADS * HEAD_PAD), rows(N_HEADS * HEAD_PAD),
                   rows(D_MODEL), rows(D_MODEL)],
        scratch_shapes=[pltpu.VMEM((8, CONV_WIDTH), F32)],
        compiler_params=pltpu.CompilerParams(
            dimension_semantics=("arbitrary",), vmem_limit_bytes=VMEM_LIMIT),
        name="pre_mixer",
    )(x2, mod, norm_g, w_lat, w_t, q_norm_g, wq2, kv_norm_g, wk2, wv, pos,
      freqs, conv_w, w_up_conv)


def _attn_kernel(q_ref, k_ref, v_ref, sga_ref, gc_ref, wua_ref, o_ref, m_ref, acc_ref, wua_bf):
    i = pl.program_id(1)
    tq = q_ref.shape[0]

    @pl.when(jnp.logical_and(pl.program_id(0) == 0, i == 0))
    def _():
        wua_bf[...] = wua_ref[...].astype(BF16)

    def step(k0, tk, masked, first=False):
        if masked:
            rq = (lax.broadcasted_iota(jnp.int32, (tq, tk), 0) + (tk - tq)) // CHUNK
            ck = lax.broadcasted_iota(jnp.int32, (tq, tk), 1) // CHUNK
            allowed = ck <= rq
        def scores(hd):
            hs = slice(hd * HEAD_PAD, (hd + 1) * HEAD_PAD)
            return _dot_t(q_ref[:, hs], k_ref[pl.ds(k0, tk), hs])

        s_next = scores(0)
        for hd in range(N_HEADS):
            hs = slice(hd * HEAD_PAD, (hd + 1) * HEAD_PAD)
            s = s_next
            if hd + 1 < N_HEADS:
                s_next = scores(hd + 1)
            if masked:
                s = jnp.where(allowed, s, NEG_BIG)
            s_max = s[:, 0:LANES]
            for c in range(1, tk // LANES):
                s_max = jnp.maximum(s_max, s[:, c * LANES:(c + 1) * LANES])
            m_new = jnp.broadcast_to(jnp.max(s_max, axis=-1, keepdims=True), (tq, LANES))
            if not first:
                m_old = m_ref[hd]
                m_new = jnp.maximum(m_old, m_new)
            p = jnp.concatenate(
                [jnp.exp2(s[:, c * LANES:(c + 1) * LANES] - m_new).astype(BF16)
                 for c in range(tk // LANES)], axis=-1)
            pv = _dot(p, v_ref[pl.ds(k0, tk), hs])
            acc_ref[hd] = pv if first else jnp.exp2(m_old - m_new) * acc_ref[hd] + pv
            m_ref[hd] = m_new

    wide = ATT_WIDE * tq
    n_wide = i // ATT_WIDE

    @pl.when(n_wide > 0)
    def _():
        step(0, wide, False, first=True)

    @pl.when(n_wide == 0)
    def _():
        m_ref[...] = jnp.full_like(m_ref, NEG_BIG)
        acc_ref[...] = jnp.zeros_like(acc_ref)

    def body(j, carry):
        step(pl.multiple_of(j * wide, wide), wide, False)
        return carry

    lax.fori_loop(1, n_wide, body, 0)

    def finish():
        heads = []
        for hd in range(N_HEADS):
            acc = acc_ref[hd]
            heads.append((acc[:, 0:V_HEAD] / acc[:, V_HEAD:V_HEAD + 1]).astype(BF16))
        a_branch = _dot(jnp.concatenate(heads, axis=-1), wua_bf[...])
        o_ref[...] = (sga_ref[...].astype(F32) * a_branch + gc_ref[...].astype(F32)).astype(BF16)

    for r in range(ATT_WIDE):
        @pl.when(i % ATT_WIDE == r)
        def _():
            step(pl.multiple_of((i - r) * tq, tq), (r + 1) * tq, True)
            finish()


def _attention(q, k, v, sga, gc, wua, batch, seq):
    tq = ATT_BLOCK
    nq = seq // tq
    q_rows = lambda w: pl.BlockSpec((tq, w), lambda b, i: (b * nq + i, 0))
    whole_seq = pl.BlockSpec((seq, N_HEADS * HEAD_PAD), lambda b, i: (b, 0))
    return pl.pallas_call(
        _attn_kernel,
        out_shape=jax.ShapeDtypeStruct((batch * seq, D_MODEL), BF16),
        grid=(batch, nq),
        in_specs=[q_rows(N_HEADS * HEAD_PAD), whole_seq, whole_seq, q_rows(D_MODEL),
                  q_rows(D_MODEL), pl.BlockSpec(wua.shape, lambda b, i: (0, 0))],
        out_specs=q_rows(D_MODEL),
        scratch_shapes=[pltpu.VMEM((N_HEADS, tq, LANES), F32),
                        pltpu.VMEM((N_HEADS, tq, LANES), F32),
                        pltpu.VMEM(wua.shape, BF16)],
        compiler_params=pltpu.CompilerParams(
            dimension_semantics=("arbitrary", "arbitrary"), vmem_limit_bytes=VMEM_LIMIT),
        name="attention",
    )(q, k, v, sga, gc, wua)


def _post_kernel(merged_ref, x_ref, mod_ref, wo_ref, g_ref, rw_ref, rb_ref,
                 x1_ref, h2_ref, idx_ref, gate_ref, rank_ref, cnt_out_ref, cnt_ref, lg_ref,
                 wo_bf):
    i = pl.program_id(0)

    @pl.when(i == 0)
    def _():
        cnt_ref[...] = jnp.zeros_like(cnt_ref)
        lg_ref[...] = jnp.zeros_like(lg_ref)
        wo_bf[...] = wo_ref[...].astype(BF16)

    prev_logits = lg_ref[(i + 1) % 2]
    counts = cnt_ref[...]
    routed = counts
    for r0 in range(0, x_ref.shape[0], POST_SUB):
        rs = slice(r0, r0 + POST_SUB)
        lg_ref[i % 2, rs, :] = _post_mix(rs, merged_ref, x_ref, mod_ref, wo_bf, g_ref, rw_ref,
                                         rb_ref, x1_ref, h2_ref)
        routed = _post_route(rs, prev_logits[rs, :], routed, idx_ref, gate_ref, rank_ref)
    counts = jnp.where(i > 0, routed, counts)
    cnt_ref[...] = counts
    cnt_out_ref[...] = counts.astype(jnp.int32)


def _post_mix(rs, merged_ref, x_ref, mod_ref, wo_ref, g_ref, rw_ref, rb_ref, x1_ref, h2_ref):
    mod = mod_ref[...]
    mix = _dot(merged_ref[rs, :], wo_ref[...])
    x1 = x_ref[rs, :] + mod[2:3] * mix
    x1_ref[rs, :] = x1
    h2 = _rms(x1, g_ref[...]) * (1.0 + mod[4:5]) + mod[3:4]
    h2_ref[rs, :] = _pack_row(h2)

    both = _dot(h2.astype(BF16), rw_ref[...])
    return both[:, :LANES] + both[:, LANES:] + rb_ref[...]


def _post_route(rs, logits, counts, idx_ref, gate_ref, rank_ref):
    lane = lax.broadcasted_iota(jnp.int32, logits.shape, 1).astype(F32)
    work = logits
    vals, idxs = [], []
    for _ in range(TOP_K):
        mk = jnp.max(work, axis=-1, keepdims=True)
        ik = jnp.min(jnp.where(work == mk, lane, float(LANES)), axis=-1, keepdims=True)
        vals.append(mk)
        idxs.append(ik)
        work = jnp.where(lane == ik, -jnp.inf, work)
    es = [jnp.exp(vk - vals[0]) for vk in vals]
    denom = es[0] + es[1] + es[2] + es[3]
    tm = logits.shape[0]
    chosen = jnp.zeros(logits.shape, F32)
    for kk in range(TOP_K):
        chosen = chosen + jnp.where(lane == idxs[kk], 1.0, 0.0)
    r_i = lax.broadcasted_iota(jnp.int32, (tm, tm), 0)
    c_i = lax.broadcasted_iota(jnp.int32, (tm, tm), 1)
    earlier = jnp.where(c_i < r_i, 1.0, 0.0).astype(BF16)
    before = _dot(earlier, chosen.astype(BF16)) + counts[0:1]

    idx_out = jnp.zeros(logits.shape, F32)
    gate_out = jnp.zeros(logits.shape, F32)
    rank_out = jnp.zeros(logits.shape, F32)
    for kk in range(TOP_K):
        rank_k = jnp.sum(jnp.where(lane == idxs[kk], before, 0.0), axis=-1, keepdims=True)
        idx_out = jnp.where(lane == kk, idxs[kk], idx_out)
        gate_out = jnp.where(lane == kk, es[kk] / denom, gate_out)
        rank_out = jnp.where(lane == kk, rank_k, rank_out)
    gate_ref[rs, :] = gate_out
    idx_ref[:, rs] = idx_out.T[0:8].astype(jnp.int32)
    rank_ref[:, rs] = rank_out.T[0:8].astype(jnp.int32)
    return counts + jnp.sum(chosen, axis=0, keepdims=True)


def _post(merged, x2, mod, wo, norm_g, rw_hl, rb_pad, seq):
    t = x2.shape[0]
    tm = POST_TILE
    tiles_per_seq = seq // tm
    n_tiles = t // tm
    full = lambda a: pl.BlockSpec(a.shape, lambda i: (0,) * a.ndim)
    mix_tile = lambda i: jnp.minimum(i, n_tiles - 1)
    route_tile = lambda i: jnp.maximum(i - 1, 0)
    rows = lambda w: pl.BlockSpec((tm, w), lambda i: (mix_tile(i), 0))
    outs = [jax.ShapeDtypeStruct((t, D_MODEL), F32),
            jax.ShapeDtypeStruct((t, PACKED), jnp.uint32),
            jax.ShapeDtypeStruct((8, t), jnp.int32),
            jax.ShapeDtypeStruct((t, LANES), F32),
            jax.ShapeDtypeStruct((8, t), jnp.int32),
            jax.ShapeDtypeStruct((8, LANES), jnp.int32)]
    slots = pl.BlockSpec((8, tm), lambda i: (0, route_tile(i)))
    return pl.pallas_call(
        _post_kernel,
        out_shape=outs,
        grid=(n_tiles + 1,),
        in_specs=[
            rows(D_MODEL), rows(D_MODEL),
            pl.BlockSpec((None, 8, D_MODEL), lambda i: (mix_tile(i) // tiles_per_seq, 0, 0)),
            full(wo), full(norm_g), full(rw_hl), full(rb_pad),
        ],
        out_specs=[rows(D_MODEL), rows(PACKED), slots,
                   pl.BlockSpec((tm, LANES), lambda i: (route_tile(i), 0)), slots,
                   pl.BlockSpec((8, LANES), lambda i: (0, 0))],
        scratch_shapes=[pltpu.VMEM((8, LANES), F32), pltpu.VMEM((2, tm, LANES), F32),
                        pltpu.VMEM(wo.shape, BF16)],
        compiler_params=pltpu.CompilerParams(
            dimension_semantics=("arbitrary",), vmem_limit_bytes=VMEM_LIMIT),
        name="post_mixer",
    )(merged, x2, mod, wo, norm_g, rw_hl, rb_pad)


_GM_COUNT, _GM_FIRST, _GM_BLOCKS = range(3)
_ST_EXPERT, _ST_SLOT = range(2)


def _moe_kernel(gm_ref, xs_ref, wgu_hbm, bgu_ref, wd_hbm, bd_ref, o_ref,
                wgu_f, wd_f, wgu_bf, wd_bf, sem, st_ref):
    weights = (wgu_hbm, wd_hbm, wgu_f, wd_f, wgu_bf, wd_bf, sem)
    rows = (xs_ref, bgu_ref, bd_ref, o_ref, wgu_bf, wd_bf)
    b0 = pl.program_id(0) * MOE_STEP_BLOCKS
    e, in_group, used = _moe_enter(b0, gm_ref, st_ref, *weights)
    together = jnp.logical_and(used, in_group + MOE_STEP_BLOCKS <= gm_ref[_GM_BLOCKS, e])

    @pl.when(together)
    def _():
        _moe_rows(slice(0, MOE_STEP_BLOCKS * MOE_BLOCK), e, in_group, True, gm_ref, *rows)

    @pl.when(jnp.logical_not(together))
    def _():
        for r in range(0, MOE_STEP_BLOCKS, 2):
            if r == 0:
                e_r, in_group_r, used_r = e, in_group, used
            else:
                e_r, in_group_r, used_r = _moe_enter(b0 + r, gm_ref, st_ref, *weights)
            paired = jnp.logical_and(used_r, in_group_r + 2 <= gm_ref[_GM_BLOCKS, e_r])

            @pl.when(paired)
            def _():
                _moe_rows(slice(r * MOE_BLOCK, (r + 2) * MOE_BLOCK), e_r, in_group_r, True,
                          gm_ref, *rows)

            @pl.when(jnp.logical_not(paired))
            def _():
                _moe_rows(slice(r * MOE_BLOCK, (r + 1) * MOE_BLOCK), e_r, in_group_r, used_r,
                          gm_ref, *rows)
                e_n, in_group_n, used_n = _moe_enter(b0 + r + 1, gm_ref, st_ref, *weights)
                _moe_rows(slice((r + 1) * MOE_BLOCK, (r + 2) * MOE_BLOCK), e_n, in_group_n,
                          used_n, gm_ref, *rows)


def _moe_enter(b, gm_ref, st_ref, wgu_hbm, wd_hbm, wgu_f, wd_f, wgu_bf, wd_bf, sem):
    def weight_copies(expert, sl):
        return (pltpu.make_async_copy(wgu_hbm.at[expert], wgu_f.at[sl], sem.at[0, sl]),
                pltpu.make_async_copy(wd_hbm.at[expert], wd_f.at[sl], sem.at[1, sl]))

    def next_group(e):
        return lax.while_loop(
            lambda k: jnp.logical_and(k < N_EXPERTS,
                                      gm_ref[_GM_BLOCKS, jnp.minimum(k, N_EXPERTS - 1)] == 0),
            lambda k: k + 1, e)

    @pl.when(b == 0)
    def _():
        e0 = next_group(0)
        st_ref[_ST_EXPERT] = e0
        st_ref[_ST_SLOT] = 1
        for cp in weight_copies(e0, 0):
            cp.start()

    e_prev = st_ref[_ST_EXPERT]
    past = b >= gm_ref[_GM_FIRST, e_prev] + gm_ref[_GM_BLOCKS, e_prev]
    e = jnp.minimum(jnp.where(past, next_group(e_prev + 1), e_prev), N_EXPERTS - 1)
    st_ref[_ST_EXPERT] = e
    in_group = b - gm_ref[_GM_FIRST, e]
    used = jnp.logical_and(in_group >= 0, in_group < gm_ref[_GM_BLOCKS, e])

    @pl.when(jnp.logical_and(used, in_group == 0))
    def _():
        slot = 1 - st_ref[_ST_SLOT]
        st_ref[_ST_SLOT] = slot
        nxt = next_group(e + 1)

        @pl.when(nxt < N_EXPERTS)
        def _():
            for cp in weight_copies(nxt, 1 - slot):
                cp.start(priority=1)

        for cp in weight_copies(e, slot):
            cp.wait()
        wgu_bf[...] = wgu_f[slot].astype(BF16)
        wd_bf[...] = wd_f[slot].astype(BF16)

    return e, in_group, used


def _moe_rows(rs, e, in_group, used, gm_ref, xs_ref, bgu_ref, bd_ref, o_ref, wgu_bf, wd_bf):
    n_valid = gm_ref[_GM_COUNT, e] - in_group * MOE_BLOCK

    def ffn(r):
        n = r.stop - r.start
        row = lax.broadcasted_iota(jnp.int32, (n, PACKED), 0)
        xs = _unpack_row(jnp.where(row < n_valid, xs_ref[r, :], 0)).astype(BF16)
        gu = _dot(xs, wgu_bf[...]) + bgu_ref[e]
        gate = jnp.minimum(gu[:, :D_EXPERT], SWIGLU_LIMIT)
        up = jnp.clip(gu[:, D_EXPERT:], -SWIGLU_LIMIT, SWIGLU_LIMIT)
        act = (up + 1.0) * (gate * jax.nn.sigmoid(gate * SWIGLU_ALPHA))
        o_ref[r, :] = _pack_row(_dot(act.astype(BF16), wd_bf[...]) + bd_ref[e])

    def zeros(r):
        o_ref[r, :] = jnp.zeros((r.stop - r.start, PACKED), o_ref.dtype)

    if used is True:
        ffn(rs)
        return

    mid = rs.start + (rs.stop - rs.start) // 2
    short = n_valid <= mid - rs.start

    @pl.when(jnp.logical_and(used, jnp.logical_not(short)))
    def _():
        ffn(rs)

    @pl.when(jnp.logical_and(used, short))
    def _():
        ffn(slice(rs.start, mid))
        zeros(slice(mid, rs.stop))

    @pl.when(jnp.logical_not(used))
    def _():
        zeros(rs)


def _moe(group_table, xs, w_gu, b_gu, w_down, b_down):
    n_rows = xs.shape[0]
    step_rows = MOE_STEP_BLOCKS * MOE_BLOCK
    assert n_rows % step_rows == 0
    grid_spec = pltpu.PrefetchScalarGridSpec(
        num_scalar_prefetch=1,
        grid=(n_rows // step_rows,),
        in_specs=[
            pl.BlockSpec((step_rows, PACKED), lambda b, gm: (b, 0)),
            pl.BlockSpec(memory_space=pl.ANY),
            pl.BlockSpec(b_gu.shape, lambda b, gm: (0, 0, 0)),
            pl.BlockSpec(memory_space=pl.ANY),
            pl.BlockSpec(b_down.shape, lambda b, gm: (0, 0, 0)),
        ],
        out_specs=pl.BlockSpec((step_rows, PACKED), lambda b, gm: (b, 0)),
        scratch_shapes=[pltpu.VMEM((2, D_MODEL, 2 * D_EXPERT), F32),
                        pltpu.VMEM((2, D_EXPERT, D_MODEL), F32),
                        pltpu.VMEM((D_MODEL, 2 * D_EXPERT), BF16),
                        pltpu.VMEM((D_EXPERT, D_MODEL), BF16),
                        pltpu.SemaphoreType.DMA((2, 2)),
                        pltpu.SMEM((2,), jnp.int32)],
    )
    return pl.pallas_call(
        _moe_kernel,
        out_shape=jax.ShapeDtypeStruct((n_rows, PACKED), jnp.uint32),
        grid_spec=grid_spec,
        compiler_params=pltpu.CompilerParams(
            dimension_semantics=("arbitrary",), vmem_limit_bytes=VMEM_LIMIT),
        name="moe_experts",
    )(group_table, xs, w_gu, b_gu, w_down, b_down)


def _final_kernel(last_layer, x1_ref, y_ref, gate_ref, mod_ref, g_ref, o_ref):
    mod = mod_ref[...]
    gate = gate_ref[...]
    ffn = gate[:, 0:1] * _unpack_row(y_ref[0])
    for kk in range(1, TOP_K):
        ffn = ffn + gate[:, kk:kk + 1] * _unpack_row(y_ref[kk])
    x = x1_ref[...] + mod[5:6] * ffn
    o_ref[...] = _rms(x, g_ref[...]) if last_layer else x


def _final(x1, y_kt, gate, mod, norm_g, seq, last_layer):
    t = x1.shape[0]
    tm = ROW_TILE
    tiles_per_seq = seq // tm
    rows = lambda w: pl.BlockSpec((tm, w), lambda i: (i, 0))
    return pl.pallas_call(
        functools.partial(_final_kernel, last_layer),
        out_shape=jax.ShapeDtypeStruct((t, D_MODEL), F32),
        grid=(t // tm,),
        in_specs=[
            rows(D_MODEL), pl.BlockSpec((TOP_K, tm, PACKED), lambda i: (0, i, 0)), rows(LANES),
            pl.BlockSpec((None, 8, D_MODEL), lambda i: (i // tiles_per_seq, 0, 0)),
            pl.BlockSpec(norm_g.shape, lambda i: (0, 0)),
        ],
        out_specs=rows(D_MODEL),
        compiler_params=pltpu.CompilerParams(
            dimension_semantics=("arbitrary",), vmem_limit_bytes=VMEM_LIMIT),
        name="combine_final",
    )(x1, y_kt, gate, mod, norm_g)


def _prep_weights(w_in, w_uq, w_ukv):
    assert w_in.shape[1] == _F_END
    w_t = jnp.swapaxes(w_in, 0, 1).astype(BF16)
    z = lambda n: jnp.zeros((n, w_t.shape[1]), BF16)
    w_lat = jnp.concatenate([w_t[:_F_KPE], z(QK_NOPE), w_t[_F_KPE:_F_CONV], z(HEAD_PAD - QK_HEAD)],
                            axis=0)

    wq = w_uq.reshape(Q_LORA, N_HEADS, QK_HEAD)
    zq = lambda n: jnp.zeros((Q_LORA, N_HEADS, n), w_uq.dtype)
    wq2 = jnp.concatenate([wq, zq(HEAD_PAD - QK_HEAD)], axis=-1)
    wq2 = wq2.reshape(Q_LORA, N_HEADS * HEAD_PAD).astype(BF16)

    wkv = w_ukv.reshape(KV_LORA, N_HEADS, QK_NOPE + V_HEAD)
    wk2 = jnp.concatenate([wkv[..., :QK_NOPE],
                           jnp.zeros((KV_LORA, N_HEADS, HEAD_PAD - QK_NOPE), w_ukv.dtype)], axis=-1)
    wk2 = wk2.reshape(KV_LORA, N_HEADS * HEAD_PAD).astype(BF16)
    wv = jnp.concatenate([wkv[..., QK_NOPE:],
                          jnp.zeros((KV_LORA, N_HEADS, HEAD_PAD - V_HEAD), w_ukv.dtype)], axis=-1)
    wv = wv.reshape(KV_LORA, N_HEADS * HEAD_PAD).astype(BF16)
    return w_lat, w_t, wq2, wk2, wv


def _rope_freqs():
    inv_freq = 1.0 / (ROPE_THETA ** (jnp.arange(0, QK_ROPE, 2, dtype=F32) / QK_ROPE))
    return inv_freq.reshape(QK_ROPE // 2, 1)


def _dest_kernel(gm_ref, idx_ref, rank_ref, o_ref):
    idx = idx_ref[...]
    dest = rank_ref[...]
    for e in range(N_EXPERTS):
        dest = dest + jnp.where(idx == e, gm_ref[_GM_FIRST, e] * MOE_BLOCK, 0)
    o_ref[...] = dest


def _route(top_idx, rank, counts, n_tokens):
    blocks = (counts + MOE_BLOCK - 1) // MOE_BLOCK
    first_block = jnp.cumsum(blocks) - blocks
    table = jnp.stack([counts, first_block, blocks]).astype(jnp.int32)
    whole = pl.BlockSpec(top_idx.shape, lambda i, gm: (0, 0))
    dest = pl.pallas_call(
        _dest_kernel,
        out_shape=jax.ShapeDtypeStruct(rank.shape, jnp.int32),
        grid_spec=pltpu.PrefetchScalarGridSpec(
            num_scalar_prefetch=1, grid=(1,), in_specs=[whole, whole], out_specs=whole),
        name="row_destinations",
    )(table, top_idx, rank)[:TOP_K]
    n_rows = n_tokens * TOP_K + N_EXPERTS * MOE_BLOCK
    return dest, table, n_rows


SC_CORES = 2
SC_SUBCORES = 16
SC_WORKERS = SC_CORES * SC_SUBCORES
SC_CHUNK = 64
SC_GATHER_RING = 3

def _sc_mesh():
    return plsc.VectorSubcoreMesh(core_axis_name="c", subcore_axis_name="s")


def _sc_worker():
    return lax.axis_index("s") * SC_CORES + lax.axis_index("c")


def _dispatch(h2, dest, n_rows):
    t, d = h2.shape
    per_w = t // SC_WORKERS
    n_chunks = per_w // SC_CHUNK
    assert per_w % (2 * SC_CHUNK) == 0
    idx = dest.reshape(TOP_K, SC_WORKERS, n_chunks, SC_CHUNK).transpose(1, 0, 2, 3)
    idx = idx.reshape(SC_WORKERS, TOP_K * n_chunks, SC_CHUNK)

    @functools.partial(
        pl.kernel, mesh=_sc_mesh(),
        out_type=jax.ShapeDtypeStruct((n_rows, d), h2.dtype),
        scratch_types=[pltpu.VMEM((TOP_K * n_chunks, SC_CHUNK), jnp.int32),
                       pltpu.VMEM((2, SC_CHUNK, d), h2.dtype),
                       pltpu.SemaphoreType.DMA((2,)),
                       pltpu.SemaphoreType.DMA((2,))],
        name="moe_dispatch")
    def run(h2_hbm, idx_hbm, xs_hbm, idx_v, rows_v, rsem, ssem):
        w = _sc_worker()
        pltpu.sync_copy(idx_hbm.at[w], idx_v)

        def read(g, b):
            src = h2_hbm.at[pl.ds(w * per_w + g * SC_CHUNK, SC_CHUNK)]
            return pltpu.make_async_copy(src, rows_v.at[b], rsem.at[b])

        def scatter(g, kk, b):
            dst = xs_hbm.at[idx_v.at[kk * n_chunks + g]]
            return pltpu.make_async_copy(rows_v.at[b], dst, ssem.at[b])

        read(0, 0).start()

        @pl.loop(0, n_chunks, step=2)
        def _(g0):
            for b in range(2):
                g = g0 + b
                read(g, b).wait()

                @pl.when(g + 1 < n_chunks)
                def _():
                    read(g + 1, 1 - b).start()

                for kk in range(TOP_K):
                    scatter(g, kk, b).start()
                for kk in range(TOP_K):
                    scatter(g, kk, b).wait()

    return run(h2, idx)


def _undispatch(ys, dest):
    t = dest.shape[1]
    d = ys.shape[1]
    n_out = t * TOP_K
    per_w = n_out // SC_WORKERS
    n_chunks = per_w // SC_CHUNK
    idx = dest.reshape(SC_WORKERS, n_chunks, SC_CHUNK)
    ring = SC_GATHER_RING

    @functools.partial(
        pl.kernel, mesh=_sc_mesh(),
        out_type=jax.ShapeDtypeStruct((n_out, d), ys.dtype),
        scratch_types=[pltpu.VMEM((n_chunks, SC_CHUNK), jnp.int32),
                       pltpu.VMEM((ring, SC_CHUNK, d), ys.dtype),
                       pltpu.SemaphoreType.DMA((ring,)),
                       pltpu.SemaphoreType.DMA((ring,))],
        name="moe_undispatch")
    def run(ys_hbm, idx_hbm, out_hbm, idx_v, rows_v, gsem, wsem):
        w = _sc_worker()
        pltpu.sync_copy(idx_hbm.at[w], idx_v)

        def gather(g):
            b = g % ring
            return pltpu.make_async_copy(ys_hbm.at[idx_v.at[g]], rows_v.at[b], gsem.at[b])

        def write(g):
            b = g % ring
            dst = out_hbm.at[pl.ds(w * per_w + g * SC_CHUNK, SC_CHUNK)]
            return pltpu.make_async_copy(rows_v.at[b], dst, wsem.at[b])

        for g in range(min(ring - 1, n_chunks)):
            gather(g).start()
        for g in range(n_chunks):
            gather(g).wait()
            ahead = g + ring - 1
            if ahead < n_chunks:
                if g >= 1:
                    write(g - 1).wait()
                gather(ahead).start()
            write(g).start()
        for g in range(max(n_chunks - ring, 0), n_chunks):
            write(g).wait()

    return run(ys, idx).reshape(TOP_K, t, d)


def kernel(x, c, positions, w_ada, b_ada, norm_mix_g, w_in, q_norm_g, w_uq, kv_norm_g, w_ukv,
           w_up_attn, conv_w, w_up_conv, w_o, norm_ffn_g, router_w, router_b, w_gu, b_gu,
           w_down, b_down, norm_final_g):
    batch, seq, d = x.shape
    t = batch * seq
    depth = w_ada.shape[0]
    assert d == D_MODEL and batch <= 8 and conv_w.shape[1:] == (CONV_K, CONV_WIDTH)
    assert seq % ROW_TILE == 0 and seq % POST_TILE == 0 and seq % ATT_BLOCK == 0
    assert t % (2 * SC_CHUNK * SC_WORKERS) == 0
    x2 = x.reshape(t, d)
    pos = positions.astype(F32).reshape(t // ROW_TILE, 1, ROW_TILE)
    freqs = _rope_freqs()
    c_pad = jnp.zeros((8, d), F32).at[:batch].set(c)

    for l in range(depth):
        ada = _ada(c_pad, w_ada[l], b_ada[l].reshape(1, -1))
        mod = ada[:batch].reshape(batch, 6, d)
        mod = jnp.concatenate([mod, jnp.zeros((batch, 2, d), F32)], axis=1)

        w_lat, w_t, wq2, wk2, wv = _prep_weights(w_in[l], w_uq[l], w_ukv[l])
        q, k, v, sga, gc = _pre(x2, mod, norm_mix_g[l].reshape(1, d), w_lat, w_t,
                                q_norm_g[l].reshape(1, -1), wq2,
                                kv_norm_g[l].reshape(1, -1), wk2, wv, pos, freqs, conv_w[l],
                                w_up_conv[l].astype(BF16), seq)
        merged = _attention(q, k, v, sga, gc, w_up_attn[l], batch, seq)

        rw_pad = jnp.concatenate([router_w[l], jnp.zeros((d, LANES - N_EXPERTS), F32)], axis=1)
        rb_pad = jnp.concatenate([router_b[l], jnp.full((LANES - N_EXPERTS,), NEG_BIG, F32)])
        rw_hi = rw_pad.astype(BF16)
        rw_lo = (rw_pad - rw_hi.astype(F32)).astype(BF16)
        x1, h2, idx_pad, gate_pad, rank_pad, counts = _post(
            merged, x2, mod, w_o[l], norm_ffn_g[l].reshape(1, d),
            jnp.concatenate([rw_hi, rw_lo], axis=1), rb_pad.reshape(1, LANES), seq)

        dest, group_table, n_rows = _route(
            idx_pad, rank_pad, counts[0, :N_EXPERTS], t)
        xs = _dispatch(h2, dest, n_rows)
        ys = _moe(group_table, xs, w_gu[l], b_gu[l].reshape(N_EXPERTS, 1, -1),
                  w_down[l], b_down[l].reshape(N_EXPERTS, 1, -1))
        y_kt = _undispatch(ys, dest)
        x2 = _final(x1, y_kt, gate_pad, mod, norm_final_g.reshape(1, d), seq, l == depth - 1)

    return x2.reshape(batch, seq, d)
```

```python
import functools
import math

import jax
import jax.numpy as jnp
from jax import lax
from jax.experimental import pallas as pl
from jax.experimental.pallas import tpu as pltpu
from jax.experimental.pallas import tpu_sc as plsc

D_MODEL = 1024
CHUNK = 64
N_HEADS = 8
Q_LORA = 256
KV_LORA = 128
QK_NOPE = 64
QK_ROPE = 32
V_HEAD = 64
QK_HEAD = QK_NOPE + QK_ROPE
ROPE_THETA = 10000.0
CONV_WIDTH = 512
CONV_K = 3
N_EXPERTS = 32
TOP_K = 4
D_EXPERT = 1024
SWIGLU_LIMIT = 7.0
SWIGLU_ALPHA = 1.702
MOE_BLOCK = 256
RMS_EPS = 1e-6

LANES = 128
HEAD_PAD = 128
NEG_BIG = -1e30
VMEM_LIMIT = 56 * 1024 * 1024

F32 = jnp.float32
BF16 = jnp.bfloat16

Q_PRESCALE = (QK_HEAD ** -0.5) * math.log2(math.e)

ADA_TILE = 1024
ROW_TILE = 1024
MOE_STEP_BLOCKS = 4
POST_TILE = 1024
POST_SUB = 512
ATT_BLOCK = 512
ATT_WIDE = 2


def _rms(x, g):
    ms = jnp.mean(x * x, axis=-1, keepdims=True)
    return x * lax.rsqrt(ms + RMS_EPS) * g


def _dot(a, b):
    return jnp.dot(a, b, preferred_element_type=F32)


def _dot_t(a, b_t):
    return lax.dot_general(a, b_t, (((1,), (1,)), ((), ())), preferred_element_type=F32)


PACKED = D_MODEL // 2


def _pack_row(x):
    return pltpu.pack_elementwise([x[:, :PACKED], x[:, PACKED:]], packed_dtype=BF16)


def _unpack_row(w):
    half = lambda i: pltpu.unpack_elementwise(w, index=i, packed_dtype=BF16, unpacked_dtype=F32)
    return jnp.concatenate([half(0), half(1)], axis=-1)


def _ada_kernel(c_ref, w_ref, b_ref, o_ref):
    c = c_ref[...]
    ca = (c * jax.nn.sigmoid(c)).astype(BF16)
    o_ref[...] = _dot(ca, w_ref[...].astype(BF16)) + b_ref[...]


def _ada(c_pad, w_ada, b_ada):
    n = w_ada.shape[1]
    tn = ADA_TILE
    return pl.pallas_call(
        _ada_kernel,
        out_shape=jax.ShapeDtypeStruct((c_pad.shape[0], n), F32),
        grid=(n // tn,),
        in_specs=[
            pl.BlockSpec(c_pad.shape, lambda j: (0, 0)),
            pl.BlockSpec((D_MODEL, tn), lambda j: (0, j)),
            pl.BlockSpec((1, tn), lambda j: (0, j)),
        ],
        out_specs=pl.BlockSpec((c_pad.shape[0], tn), lambda j: (0, j)),
        compiler_params=pltpu.CompilerParams(
            dimension_semantics=("arbitrary",), vmem_limit_bytes=VMEM_LIMIT),
        name="ada",
    )(c_pad, w_ada, b_ada)


_C_QLAT = 0
_C_KVLAT = _C_QLAT + Q_LORA
_C_KPE = _C_KVLAT + KV_LORA
_C_END = _C_KPE + HEAD_PAD
_F_KPE = Q_LORA + KV_LORA
_F_CONV = _F_KPE + QK_ROPE
_F_GATE = _F_CONV + 3 * CONV_WIDTH
_F_END = _F_GATE + 2 * D_MODEL


def _pre_kernel(tiles_per_seq, x_ref, mod_ref, g_ref, wlat_ref, wt_ref, qg_ref,
                wq_ref, kvg_ref, wk_ref, wv_ref, pos_ref, freq_ref, cw_ref, wuc_ref,
                q_ref, k_ref, v_ref, sga_ref, gc_ref, carry_ref):
    i = pl.program_id(0)
    tm = x_ref.shape[0]
    mod = mod_ref[...]
    h = _rms(x_ref[...], g_ref[...]) * (1.0 + mod[1:2]) + mod[0:1]
    hb = h.astype(BF16)

    ang = freq_ref[...] * pos_ref[...]
    cos_t, sin_t = jnp.cos(ang), jnp.sin(ang)
    ones_t = jnp.ones((QK_NOPE, tm), F32)
    zeros_t = jnp.zeros((QK_NOPE, tm), F32)
    pad_t = jnp.zeros((HEAD_PAD - QK_HEAD, tm), F32)
    cosf = jnp.concatenate([ones_t, cos_t, cos_t, pad_t], axis=0).T
    sinf = jnp.concatenate([zeros_t, -sin_t, sin_t, pad_t], axis=0).T

    first_half = lax.broadcasted_iota(jnp.int32, (tm, HEAD_PAD), 1) < QK_NOPE + QK_ROPE // 2

    def rope(slab):
        swapped = jnp.where(first_half, pltpu.roll(slab, HEAD_PAD - QK_ROPE // 2, 1),
                            pltpu.roll(slab, QK_ROPE // 2, 1))
        return slab * cosf + swapped * sinf

    gates = _dot_t(hb, wt_ref[_F_GATE:_F_END, :])
    sga_ref[...] = jax.nn.sigmoid(gates[:, 0:D_MODEL]).astype(BF16)
    g_conv = jax.nn.sigmoid(gates[:, D_MODEL:])

    small = _dot_t(hb, wlat_ref[...])
    q_lat = small[:, _C_QLAT:_C_KVLAT]
    kv_lat = small[:, _C_KVLAT:_C_KPE]
    kpe = rope(small[:, _C_KPE:_C_END])
    qn = _rms(q_lat, qg_ref[...]).astype(BF16)
    q = _dot(qn, wq_ref[...])
    q = jnp.concatenate([rope(q[:, hd * HEAD_PAD:(hd + 1) * HEAD_PAD]) for hd in range(N_HEADS)],
                        axis=-1)
    q_ref[...] = (q * Q_PRESCALE).astype(BF16)
    kvn = _rms(kv_lat, kvg_ref[...]).astype(BF16)
    k = _dot(kvn, wk_ref[...]) + jnp.concatenate([kpe] * N_HEADS, axis=-1)
    k_ref[...] = k.astype(BF16)
    lane = lax.broadcasted_iota(jnp.int32, (tm, N_HEADS * HEAD_PAD), 1)
    ones_col = jnp.where(lane % HEAD_PAD == V_HEAD, 1.0, 0.0)
    v_ref[...] = (_dot(kvn, wv_ref[...]) + ones_col).astype(BF16)

    ucb = _dot_t(hb, wt_ref[_F_CONV:_F_GATE, :])
    cu = ucb[:, 0:CONV_WIDTH] * ucb[:, CONV_WIDTH:2 * CONV_WIDTH]
    b_gate = ucb[:, 2 * CONV_WIDTH:3 * CONV_WIDTH]

    @pl.when(i % tiles_per_seq == 0)
    def _():
        carry_ref[...] = jnp.zeros_like(carry_ref)

    prev = carry_ref[...]
    row = lax.broadcasted_iota(jnp.int32, cu.shape, 0)
    cu1 = jnp.where(row == 0, prev[7:8], pltpu.roll(cu, 1, 0))
    cu2 = jnp.where(row == 0, prev[6:7], jnp.where(row == 1, prev[7:8], pltpu.roll(cu, 2, 0)))
    cw = cw_ref[...]
    z = cw[2:3] * cu + cw[1:2] * cu1 + cw[0:1] * cu2
    carry_ref[...] = cu[tm - 8:tm]
    c_branch = _dot((b_gate * z).astype(BF16), wuc_ref[...])
    gc_ref[...] = (g_conv * c_branch).astype(BF16)


def _pre(x2, mod, norm_g, w_lat, w_t, q_norm_g, wq2, kv_norm_g, wk2, wv, pos,
         freqs, conv_w, w_up_conv, seq):
    t = x2.shape[0]
    tm = ROW_TILE
    tiles_per_seq = seq // tm
    full = lambda a: pl.BlockSpec(a.shape, lambda i: (0,) * a.ndim)
    rows = lambda w: pl.BlockSpec((tm, w), lambda i: (i, 0))
    outs = [jax.ShapeDtypeStruct((t, N_HEADS * HEAD_PAD), BF16),
            jax.ShapeDtypeStruct((t, N_HEADS * HEAD_PAD), BF16),
            jax.ShapeDtypeStruct((t, N_HEADS * HEAD_PAD), BF16),
            jax.ShapeDtypeStruct((t, D_MODEL), BF16),
            jax.ShapeDtypeStruct((t, D_MODEL), BF16)]
    return pl.pallas_call(
        functools.partial(_pre_kernel, tiles_per_seq),
        out_shape=outs,
        grid=(t // tm,),
        in_specs=[
            rows(D_MODEL),
            pl.BlockSpec((None, 8, D_MODEL), lambda i: (i // tiles_per_seq, 0, 0)),
            full(norm_g), full(w_lat), full(w_t), full(q_norm_g), full(wq2),
            full(kv_norm_g), full(wk2), full(wv),
            pl.BlockSpec((None, 1, tm), lambda i: (i, 0, 0)), full(freqs), full(conv_w),
            full(w_up_conv),
        ],
        out_specs=[rows(N_HEADS * HEAD_PAD), rows(N_HEADS * HEAD_PAD), rows(N_HEADS * HEAD_PAD),
                   rows(D_MODEL), rows(D_MODEL)],
        scratch_shapes=[pltpu.VMEM((8, CONV_WIDTH), F32)],
        compiler_params=pltpu.CompilerParams(
            dimension_semantics=("arbitrary",), vmem_limit_bytes=VMEM_LIMIT),
        name="pre_mixer",
    )(x2, mod, norm_g, w_lat, w_t, q_norm_g, wq2, kv_norm_g, wk2, wv, pos,
      freqs, conv_w, w_up_conv)


def _attn_kernel(q_ref, k_ref, v_ref, sga_ref, gc_ref, wua_ref, o_ref, m_ref, acc_ref, wua_bf):
    i = pl.program_id(1)
    tq = q_ref.shape[0]

    @pl.when(jnp.logical_and(pl.program_id(0) == 0, i == 0))
    def _():
        wua_bf[...] = wua_ref[...].astype(BF16)

    def step(k0, tk, masked, first=False):
        if masked:
            rq = (lax.broadcasted_iota(jnp.int32, (tq, tk), 0) + (tk - tq)) // CHUNK
            ck = lax.broadcasted_iota(jnp.int32, (tq, tk), 1) // CHUNK
            allowed = ck <= rq
        def scores(hd):
            hs = slice(hd * HEAD_PAD, (hd + 1) * HEAD_PAD)
            return _dot_t(q_ref[:, hs], k_ref[pl.ds(k0, tk), hs])

        s_next = scores(0)
        for hd in range(N_HEADS):
            hs = slice(hd * HEAD_PAD, (hd + 1) * HEAD_PAD)
            s = s_next
            if hd + 1 < N_HEADS:
                s_next = scores(hd + 1)
            if masked:
                s = jnp.where(allowed, s, NEG_BIG)
            s_max = s[:, 0:LANES]
            for c in range(1, tk // LANES):
                s_max = jnp.maximum(s_max, s[:, c * LANES:(c + 1) * LANES])
            m_new = jnp.broadcast_to(jnp.max(s_max, axis=-1, keepdims=True), (tq, LANES))
            if not first:
                m_old = m_ref[hd]
                m_new = jnp.maximum(m_old, m_new)
            p = jnp.concatenate(
                [jnp.exp2((s[:, c * LANES:(c + 1) * LANES] - m_new).astype(BF16))
                 for c in range(tk // LANES)], axis=-1)
            pv = _dot(p, v_ref[pl.ds(k0, tk), hs])
            acc_ref[hd] = pv if first else jnp.exp2(m_old - m_new) * acc_ref[hd] + pv
            m_ref[hd] = m_new

    wide = ATT_WIDE * tq
    n_wide = i // ATT_WIDE

    @pl.when(n_wide > 0)
    def _():
        step(0, wide, False, first=True)

    @pl.when(n_wide == 0)
    def _():
        m_ref[...] = jnp.full_like(m_ref, NEG_BIG)
        acc_ref[...] = jnp.zeros_like(acc_ref)

    def body(j, carry):
        step(pl.multiple_of(j * wide, wide), wide, False)
        return carry

    lax.fori_loop(1, n_wide, body, 0)

    def finish():
        heads = []
        for hd in range(N_HEADS):
            acc = acc_ref[hd]
            heads.append((acc[:, 0:V_HEAD] / acc[:, V_HEAD:V_HEAD + 1]).astype(BF16))
        a_branch = _dot(jnp.concatenate(heads, axis=-1), wua_bf[...])
        o_ref[...] = (sga_ref[...].astype(F32) * a_branch + gc_ref[...].astype(F32)).astype(BF16)

    for r in range(ATT_WIDE):
        @pl.when(i % ATT_WIDE == r)
        def _():
            step(pl.multiple_of((i - r) * tq, tq), (r + 1) * tq, True)
            finish()


def _attention(q, k, v, sga, gc, wua, batch, seq):
    tq = ATT_BLOCK
    nq = seq // tq
    q_rows = lambda w: pl.BlockSpec((tq, w), lambda b, i: (b * nq + i, 0))
    whole_seq = pl.BlockSpec((seq, N_HEADS * HEAD_PAD), lambda b, i: (b, 0))
    return pl.pallas_call(
        _attn_kernel,
        out_shape=jax.ShapeDtypeStruct((batch * seq, D_MODEL), BF16),
        grid=(batch, nq),
        in_specs=[q_rows(N_HEADS * HEAD_PAD), whole_seq, whole_seq, q_rows(D_MODEL),
                  q_rows(D_MODEL), pl.BlockSpec(wua.shape, lambda b, i: (0, 0))],
        out_specs=q_rows(D_MODEL),
        scratch_shapes=[pltpu.VMEM((N_HEADS, tq, LANES), F32),
                        pltpu.VMEM((N_HEADS, tq, LANES), F32),
                        pltpu.VMEM(wua.shape, BF16)],
        compiler_params=pltpu.CompilerParams(
            dimension_semantics=("arbitrary", "arbitrary"), vmem_limit_bytes=VMEM_LIMIT),
        name="attention",
    )(q, k, v, sga, gc, wua)


def _post_kernel(merged_ref, x_ref, mod_ref, wo_ref, g_ref, rw_ref, rb_ref,
                 x1_ref, h2_ref, idx_ref, gate_ref, rank_ref, cnt_out_ref, cnt_ref, lg_ref,
                 wo_bf):
    i = pl.program_id(0)

    @pl.when(i == 0)
    def _():
        cnt_ref[...] = jnp.zeros_like(cnt_ref)
        lg_ref[...] = jnp.zeros_like(lg_ref)
        wo_bf[...] = wo_ref[...].astype(BF16)

    prev_logits = lg_ref[(i + 1) % 2]
    counts = cnt_ref[...]
    routed = counts
    for r0 in range(0, x_ref.shape[0], POST_SUB):
        rs = slice(r0, r0 + POST_SUB)
        lg_ref[i % 2, rs, :] = _post_mix(rs, merged_ref, x_ref, mod_ref, wo_bf, g_ref, rw_ref,
                                         rb_ref, x1_ref, h2_ref)
        routed = _post_route(rs, prev_logits[rs, :], routed, idx_ref, gate_ref, rank_ref)
    counts = jnp.where(i > 0, routed, counts)
    cnt_ref[...] = counts
    cnt_out_ref[...] = counts.astype(jnp.int32)


def _post_mix(rs, merged_ref, x_ref, mod_ref, wo_ref, g_ref, rw_ref, rb_ref, x1_ref, h2_ref):
    mod = mod_ref[...]
    mix = _dot(merged_ref[rs, :], wo_ref[...])
    x1 = x_ref[rs, :] + mod[2:3] * mix
    x1_ref[rs, :] = x1
    h2 = _rms(x1, g_ref[...]) * (1.0 + mod[4:5]) + mod[3:4]
    h2_ref[rs, :] = _pack_row(h2)

    both = _dot(h2.astype(BF16), rw_ref[...])
    return both[:, :LANES] + both[:, LANES:] + rb_ref[...]


def _post_route(rs, logits, counts, idx_ref, gate_ref, rank_ref):
    lane = lax.broadcasted_iota(jnp.int32, logits.shape, 1).astype(F32)
    work = logits
    vals, idxs = [], []
    for _ in range(TOP_K):
        mk = jnp.max(work, axis=-1, keepdims=True)
        ik = jnp.min(jnp.where(work == mk, lane, float(LANES)), axis=-1, keepdims=True)
        vals.append(mk)
        idxs.append(ik)
        work = jnp.where(lane == ik, -jnp.inf, work)
    es = [jnp.exp(vk - vals[0]) for vk in vals]
    denom = es[0] + es[1] + es[2] + es[3]
    tm = logits.shape[0]
    chosen = jnp.zeros(logits.shape, F32)
    for kk in range(TOP_K):
        chosen = chosen + jnp.where(lane == idxs[kk], 1.0, 0.0)
    r_i = lax.broadcasted_iota(jnp.int32, (tm, tm), 0)
    c_i = lax.broadcasted_iota(jnp.int32, (tm, tm), 1)
    earlier = jnp.where(c_i < r_i, 1.0, 0.0).astype(BF16)
    before = _dot(earlier, chosen.astype(BF16)) + counts[0:1]

    idx_out = jnp.zeros(logits.shape, F32)
    gate_out = jnp.zeros(logits.shape, F32)
    rank_out = jnp.zeros(logits.shape, F32)
    for kk in range(TOP_K):
        rank_k = jnp.sum(jnp.where(lane == idxs[kk], before, 0.0), axis=-1, keepdims=True)
        idx_out = jnp.where(lane == kk, idxs[kk], idx_out)
        gate_out = jnp.where(lane == kk, es[kk] / denom, gate_out)
        rank_out = jnp.where(lane == kk, rank_k, rank_out)
    gate_ref[rs, :] = gate_out
    idx_ref[:, rs] = idx_out.T[0:8].astype(jnp.int32)
    rank_ref[:, rs] = rank_out.T[0:8].astype(jnp.int32)
    return counts + jnp.sum(chosen, axis=0, keepdims=True)


def _post(merged, x2, mod, wo, norm_g, rw_hl, rb_pad, seq):
    t = x2.shape[0]
    tm = POST_TILE
    tiles_per_seq = seq // tm
    n_tiles = t // tm
    full = lambda a: pl.BlockSpec(a.shape, lambda i: (0,) * a.ndim)
    mix_tile = lambda i: jnp.minimum(i, n_tiles - 1)
    route_tile = lambda i: jnp.maximum(i - 1, 0)
    rows = lambda w: pl.BlockSpec((tm, w), lambda i: (mix_tile(i), 0))
    outs = [jax.ShapeDtypeStruct((t, D_MODEL), F32),
            jax.ShapeDtypeStruct((t, PACKED), jnp.uint32),
            jax.ShapeDtypeStruct((8, t), jnp.int32),
            jax.ShapeDtypeStruct((t, LANES), F32),
            jax.ShapeDtypeStruct((8, t), jnp.int32),
            jax.ShapeDtypeStruct((8, LANES), jnp.int32)]
    slots = pl.BlockSpec((8, tm), lambda i: (0, route_tile(i)))
    return pl.pallas_call(
        _post_kernel,
        out_shape=outs,
        grid=(n_tiles + 1,),
        in_specs=[
            rows(D_MODEL), rows(D_MODEL),
            pl.BlockSpec((None, 8, D_MODEL), lambda i: (mix_tile(i) // tiles_per_seq, 0, 0)),
            full(wo), full(norm_g), full(rw_hl), full(rb_pad),
        ],
        out_specs=[rows(D_MODEL), rows(PACKED), slots,
                   pl.BlockSpec((tm, LANES), lambda i: (route_tile(i), 0)), slots,
                   pl.BlockSpec((8, LANES), lambda i: (0, 0))],
        scratch_shapes=[pltpu.VMEM((8, LANES), F32), pltpu.VMEM((2, tm, LANES), F32),
                        pltpu.VMEM(wo.shape, BF16)],
        compiler_params=pltpu.CompilerParams(
            dimension_semantics=("arbitrary",), vmem_limit_bytes=VMEM_LIMIT),
        name="post_mixer",
    )(merged, x2, mod, wo, norm_g, rw_hl, rb_pad)


_GM_COUNT, _GM_FIRST, _GM_BLOCKS = range(3)
_ST_EXPERT, _ST_SLOT = range(2)


def _moe_kernel(gm_ref, xs_ref, wgu_hbm, bgu_ref, wd_hbm, bd_ref, o_ref,
                wgu_f, wd_f, wgu_bf, wd_bf, sem, st_ref):
    weights = (wgu_hbm, wd_hbm, wgu_f, wd_f, wgu_bf, wd_bf, sem)
    rows = (xs_ref, bgu_ref, bd_ref, o_ref, wgu_bf, wd_bf)
    b0 = pl.program_id(0) * MOE_STEP_BLOCKS
    e, in_group, used = _moe_enter(b0, gm_ref, st_ref, *weights)
    together = jnp.logical_and(used, in_group + MOE_STEP_BLOCKS <= gm_ref[_GM_BLOCKS, e])

    @pl.when(together)
    def _():
        _moe_rows(slice(0, MOE_STEP_BLOCKS * MOE_BLOCK), e, in_group, True, gm_ref, *rows)

    @pl.when(jnp.logical_not(together))
    def _():
        for r in range(0, MOE_STEP_BLOCKS, 2):
            if r == 0:
                e_r, in_group_r, used_r = e, in_group, used
            else:
                e_r, in_group_r, used_r = _moe_enter(b0 + r, gm_ref, st_ref, *weights)
            paired = jnp.logical_and(used_r, in_group_r + 2 <= gm_ref[_GM_BLOCKS, e_r])

            @pl.when(paired)
            def _():
                _moe_rows(slice(r * MOE_BLOCK, (r + 2) * MOE_BLOCK), e_r, in_group_r, True,
                          gm_ref, *rows)

            @pl.when(jnp.logical_not(paired))
            def _():
                _moe_rows(slice(r * MOE_BLOCK, (r + 1) * MOE_BLOCK), e_r, in_group_r, used_r,
                          gm_ref, *rows)
                e_n, in_group_n, used_n = _moe_enter(b0 + r + 1, gm_ref, st_ref, *weights)
                _moe_rows(slice((r + 1) * MOE_BLOCK, (r + 2) * MOE_BLOCK), e_n, in_group_n,
                          used_n, gm_ref, *rows)


def _moe_enter(b, gm_ref, st_ref, wgu_hbm, wd_hbm, wgu_f, wd_f, wgu_bf, wd_bf, sem):
    def weight_copies(expert, sl):
        return (pltpu.make_async_copy(wgu_hbm.at[expert], wgu_f.at[sl], sem.at[0, sl]),
                pltpu.make_async_copy(wd_hbm.at[expert], wd_f.at[sl], sem.at[1, sl]))

    def next_group(e):
        return lax.while_loop(
            lambda k: jnp.logical_and(k < N_EXPERTS,
                                      gm_ref[_GM_BLOCKS, jnp.minimum(k, N_EXPERTS - 1)] == 0),
            lambda k: k + 1, e)

    @pl.when(b == 0)
    def _():
        e0 = next_group(0)
        st_ref[_ST_EXPERT] = e0
        st_ref[_ST_SLOT] = 1
        for cp in weight_copies(e0, 0):
            cp.start()

    e_prev = st_ref[_ST_EXPERT]
    past = b >= gm_ref[_GM_FIRST, e_prev] + gm_ref[_GM_BLOCKS, e_prev]
    e = jnp.minimum(jnp.where(past, next_group(e_prev + 1), e_prev), N_EXPERTS - 1)
    st_ref[_ST_EXPERT] = e
    in_group = b - gm_ref[_GM_FIRST, e]
    used = jnp.logical_and(in_group >= 0, in_group < gm_ref[_GM_BLOCKS, e])

    @pl.when(jnp.logical_and(used, in_group == 0))
    def _():
        slot = 1 - st_ref[_ST_SLOT]
        st_ref[_ST_SLOT] = slot
        nxt = next_group(e + 1)

        @pl.when(nxt < N_EXPERTS)
        def _():
            for cp in weight_copies(nxt, 1 - slot):
                cp.start(priority=1)

        for cp in weight_copies(e, slot):
            cp.wait()
        wgu_bf[...] = wgu_f[slot].astype(BF16)
        wd_bf[...] = wd_f[slot].astype(BF16)

    return e, in_group, used


def _moe_rows(rs, e, in_group, used, gm_ref, xs_ref, bgu_ref, bd_ref, o_ref, wgu_bf, wd_bf):
    n_valid = gm_ref[_GM_COUNT, e] - in_group * MOE_BLOCK

    def ffn(r):
        n = r.stop - r.start
        row = lax.broadcasted_iota(jnp.int32, (n, PACKED), 0)
        xs = _unpack_row(jnp.where(row < n_valid, xs_ref[r, :], 0)).astype(BF16)
        gu = _dot(xs, wgu_bf[...]) + bgu_ref[e]
        gate = jnp.minimum(gu[:, :D_EXPERT], SWIGLU_LIMIT)
        up = jnp.clip(gu[:, D_EXPERT:], -SWIGLU_LIMIT, SWIGLU_LIMIT)
        act = (up + 1.0) * (gate * jax.nn.sigmoid(gate * SWIGLU_ALPHA))
        o_ref[r, :] = _pack_row(_dot(act.astype(BF16), wd_bf[...]) + bd_ref[e])

    def zeros(r):
        o_ref[r, :] = jnp.zeros((r.stop - r.start, PACKED), o_ref.dtype)

    if used is True:
        ffn(rs)
        return

    mid = rs.start + (rs.stop - rs.start) // 2
    short = n_valid <= mid - rs.start

    @pl.when(jnp.logical_and(used, jnp.logical_not(short)))
    def _():
        ffn(rs)

    @pl.when(jnp.logical_and(used, short))
    def _():
        ffn(slice(rs.start, mid))
        zeros(slice(mid, rs.stop))

    @pl.when(jnp.logical_not(used))
    def _():
        zeros(rs)


def _moe(group_table, xs, w_gu, b_gu, w_down, b_down):
    n_rows = xs.shape[0]
    step_rows = MOE_STEP_BLOCKS * MOE_BLOCK
    assert n_rows % step_rows == 0
    grid_spec = pltpu.PrefetchScalarGridSpec(
        num_scalar_prefetch=1,
        grid=(n_rows // step_rows,),
        in_specs=[
            pl.BlockSpec((step_rows, PACKED), lambda b, gm: (b, 0)),
            pl.BlockSpec(memory_space=pl.ANY),
            pl.BlockSpec(b_gu.shape, lambda b, gm: (0, 0, 0)),
            pl.BlockSpec(memory_space=pl.ANY),
            pl.BlockSpec(b_down.shape, lambda b, gm: (0, 0, 0)),
        ],
        out_specs=pl.BlockSpec((step_rows, PACKED), lambda b, gm: (b, 0)),
        scratch_shapes=[pltpu.VMEM((2, D_MODEL, 2 * D_EXPERT), F32),
                        pltpu.VMEM((2, D_EXPERT, D_MODEL), F32),
                        pltpu.VMEM((D_MODEL, 2 * D_EXPERT), BF16),
                        pltpu.VMEM((D_EXPERT, D_MODEL), BF16),
                        pltpu.SemaphoreType.DMA((2, 2)),
                        pltpu.SMEM((2,), jnp.int32)],
    )
    return pl.pallas_call(
        _moe_kernel,
        out_shape=jax.ShapeDtypeStruct((n_rows, PACKED), jnp.uint32),
        grid_spec=grid_spec,
        compiler_params=pltpu.CompilerParams(
            dimension_semantics=("arbitrary",), vmem_limit_bytes=VMEM_LIMIT),
        name="moe_experts",
    )(group_table, xs, w_gu, b_gu, w_down, b_down)


def _final_kernel(last_layer, x1_ref, y_ref, gate_ref, mod_ref, g_ref, o_ref):
    mod = mod_ref[...]
    gate = gate_ref[...]
    ffn = gate[:, 0:1] * _unpack_row(y_ref[0])
    for kk in range(1, TOP_K):
        ffn = ffn + gate[:, kk:kk + 1] * _unpack_row(y_ref[kk])
    x = x1_ref[...] + mod[5:6] * ffn
    o_ref[...] = _rms(x, g_ref[...]) if last_layer else x


def _final(x1, y_kt, gate, mod, norm_g, seq, last_layer):
    t = x1.shape[0]
    tm = ROW_TILE
    tiles_per_seq = seq // tm
    rows = lambda w: pl.BlockSpec((tm, w), lambda i: (i, 0))
    return pl.pallas_call(
        functools.partial(_final_kernel, last_layer),
        out_shape=jax.ShapeDtypeStruct((t, D_MODEL), F32),
        grid=(t // tm,),
        in_specs=[
            rows(D_MODEL), pl.BlockSpec((TOP_K, tm, PACKED), lambda i: (0, i, 0)), rows(LANES),
            pl.BlockSpec((None, 8, D_MODEL), lambda i: (i // tiles_per_seq, 0, 0)),
            pl.BlockSpec(norm_g.shape, lambda i: (0, 0)),
        ],
        out_specs=rows(D_MODEL),
        compiler_params=pltpu.CompilerParams(
            dimension_semantics=("arbitrary",), vmem_limit_bytes=VMEM_LIMIT),
        name="combine_final",
    )(x1, y_kt, gate, mod, norm_g)


def _prep_weights(w_in, w_uq, w_ukv):
    assert w_in.shape[1] == _F_END
    w_t = jnp.swapaxes(w_in, 0, 1).astype(BF16)
    z = lambda n: jnp.zeros((n, w_t.shape[1]), BF16)
    w_lat = jnp.concatenate([w_t[:_F_KPE], z(QK_NOPE), w_t[_F_KPE:_F_CONV], z(HEAD_PAD - QK_HEAD)],
                            axis=0)

    wq = w_uq.reshape(Q_LORA, N_HEADS, QK_HEAD)
    zq = lambda n: jnp.zeros((Q_LORA, N_HEADS, n), w_uq.dtype)
    wq2 = jnp.concatenate([wq, zq(HEAD_PAD - QK_HEAD)], axis=-1)
    wq2 = wq2.reshape(Q_LORA, N_HEADS * HEAD_PAD).astype(BF16)

    wkv = w_ukv.reshape(KV_LORA, N_HEADS, QK_NOPE + V_HEAD)
    wk2 = jnp.concatenate([wkv[..., :QK_NOPE],
                           jnp.zeros((KV_LORA, N_HEADS, HEAD_PAD - QK_NOPE), w_ukv.dtype)], axis=-1)
    wk2 = wk2.reshape(KV_LORA, N_HEADS * HEAD_PAD).astype(BF16)
    wv = jnp.concatenate([wkv[..., QK_NOPE:],
                          jnp.zeros((KV_LORA, N_HEADS, HEAD_PAD - V_HEAD), w_ukv.dtype)], axis=-1)
    wv = wv.reshape(KV_LORA, N_HEADS * HEAD_PAD).astype(BF16)
    return w_lat, w_t, wq2, wk2, wv


def _rope_freqs():
    inv_freq = 1.0 / (ROPE_THETA ** (jnp.arange(0, QK_ROPE, 2, dtype=F32) / QK_ROPE))
    return inv_freq.reshape(QK_ROPE // 2, 1)


def _dest_kernel(gm_ref, idx_ref, rank_ref, o_ref):
    idx = idx_ref[...]
    dest = rank_ref[...]
    for e in range(N_EXPERTS):
        dest = dest + jnp.where(idx == e, gm_ref[_GM_FIRST, e] * MOE_BLOCK, 0)
    o_ref[...] = dest


def _route(top_idx, rank, counts, n_tokens):
    blocks = (counts + MOE_BLOCK - 1) // MOE_BLOCK
    first_block = jnp.cumsum(blocks) - blocks
    table = jnp.stack([counts, first_block, blocks]).astype(jnp.int32)
    whole = pl.BlockSpec(top_idx.shape, lambda i, gm: (0, 0))
    dest = pl.pallas_call(
        _dest_kernel,
        out_shape=jax.ShapeDtypeStruct(rank.shape, jnp.int32),
        grid_spec=pltpu.PrefetchScalarGridSpec(
            num_scalar_prefetch=1, grid=(1,), in_specs=[whole, whole], out_specs=whole),
        name="row_destinations",
    )(table, top_idx, rank)[:TOP_K]
    n_rows = n_tokens * TOP_K + N_EXPERTS * MOE_BLOCK
    return dest, table, n_rows


SC_CORES = 2
SC_SUBCORES = 16
SC_WORKERS = SC_CORES * SC_SUBCORES
SC_CHUNK = 64
SC_GATHER_RING = 3

def _sc_mesh():
    return plsc.VectorSubcoreMesh(core_axis_name="c", subcore_axis_name="s")


def _sc_worker():
    return lax.axis_index("s") * SC_CORES + lax.axis_index("c")


def _dispatch(h2, dest, n_rows):
    t, d = h2.shape
    per_w = t // SC_WORKERS
    n_chunks = per_w // SC_CHUNK
    assert per_w % (2 * SC_CHUNK) == 0
    idx = dest.reshape(TOP_K, SC_WORKERS, n_chunks, SC_CHUNK).transpose(1, 0, 2, 3)
    idx = idx.reshape(SC_WORKERS, TOP_K * n_chunks, SC_CHUNK)

    @functools.partial(
        pl.kernel, mesh=_sc_mesh(),
        out_type=jax.ShapeDtypeStruct((n_rows, d), h2.dtype),
        scratch_types=[pltpu.VMEM((TOP_K * n_chunks, SC_CHUNK), jnp.int32),
                       pltpu.VMEM((2, SC_CHUNK, d), h2.dtype),
                       pltpu.SemaphoreType.DMA((2,)),
                       pltpu.SemaphoreType.DMA((2,))],
        name="moe_dispatch")
    def run(h2_hbm, idx_hbm, xs_hbm, idx_v, rows_v, rsem, ssem):
        w = _sc_worker()
        pltpu.sync_copy(idx_hbm.at[w], idx_v)

        def read(g, b):
            src = h2_hbm.at[pl.ds(w * per_w + g * SC_CHUNK, SC_CHUNK)]
            return pltpu.make_async_copy(src, rows_v.at[b], rsem.at[b])

        def scatter(g, kk, b):
            dst = xs_hbm.at[idx_v.at[kk * n_chunks + g]]
            return pltpu.make_async_copy(rows_v.at[b], dst, ssem.at[b])

        read(0, 0).start()

        @pl.loop(0, n_chunks, step=2)
        def _(g0):
            for b in range(2):
                g = g0 + b
                read(g, b).wait()

                @pl.when(g + 1 < n_chunks)
                def _():
                    read(g + 1, 1 - b).start()

                for kk in range(TOP_K):
                    scatter(g, kk, b).start()
                for kk in range(TOP_K):
                    scatter(g, kk, b).wait()

    return run(h2, idx)


def _undispatch(ys, dest):
    t = dest.shape[1]
    d = ys.shape[1]
    n_out = t * TOP_K
    per_w = n_out // SC_WORKERS
    n_chunks = per_w // SC_CHUNK
    idx = dest.reshape(SC_WORKERS, n_chunks, SC_CHUNK)
    ring = SC_GATHER_RING

    @functools.partial(
        pl.kernel, mesh=_sc_mesh(),
        out_type=jax.ShapeDtypeStruct((n_out, d), ys.dtype),
        scratch_types=[pltpu.VMEM((n_chunks, SC_CHUNK), jnp.int32),
                       pltpu.VMEM((ring, SC_CHUNK, d), ys.dtype),
                       pltpu.SemaphoreType.DMA((ring,)),
                       pltpu.SemaphoreType.DMA((ring,))],
        name="moe_undispatch")
    def run(ys_hbm, idx_hbm, out_hbm, idx_v, rows_v, gsem, wsem):
        w = _sc_worker()
        pltpu.sync_copy(idx_hbm.at[w], idx_v)

        def gather(g):
            b = g % ring
            return pltpu.make_async_copy(ys_hbm.at[idx_v.at[g]], rows_v.at[b], gsem.at[b])

        def write(g):
            b = g % ring
            dst = out_hbm.at[pl.ds(w * per_w + g * SC_CHUNK, SC_CHUNK)]
            return pltpu.make_async_copy(rows_v.at[b], dst, wsem.at[b])

        for g in range(min(ring - 1, n_chunks)):
            gather(g).start()
        for g in range(n_chunks):
            gather(g).wait()
            ahead = g + ring - 1
            if ahead < n_chunks:
                if g >= 1:
                    write(g - 1).wait()
                gather(ahead).start()
            write(g).start()
        for g in range(max(n_chunks - ring, 0), n_chunks):
            write(g).wait()

    return run(ys, idx).reshape(TOP_K, t, d)


def kernel(x, c, positions, w_ada, b_ada, norm_mix_g, w_in, q_norm_g, w_uq, kv_norm_g, w_ukv,
           w_up_attn, conv_w, w_up_conv, w_o, norm_ffn_g, router_w, router_b, w_gu, b_gu,
           w_down, b_down, norm_final_g):
    batch, seq, d = x.shape
    t = batch * seq
    depth = w_ada.shape[0]
    assert d == D_MODEL and batch <= 8 and conv_w.shape[1:] == (CONV_K, CONV_WIDTH)
    assert seq % ROW_TILE == 0 and seq % POST_TILE == 0 and seq % ATT_BLOCK == 0
    assert t % (2 * SC_CHUNK * SC_WORKERS) == 0
    x2 = x.reshape(t, d)
    pos = positions.astype(F32).reshape(t // ROW_TILE, 1, ROW_TILE)
    freqs = _rope_freqs()
    c_pad = jnp.zeros((8, d), F32).at[:batch].set(c)

    for l in range(depth):
        ada = _ada(c_pad, w_ada[l], b_ada[l].reshape(1, -1))
        mod = ada[:batch].reshape(batch, 6, d)
        mod = jnp.concatenate([mod, jnp.zeros((batch, 2, d), F32)], axis=1)

        w_lat, w_t, wq2, wk2, wv = _prep_weights(w_in[l], w_uq[l], w_ukv[l])
        q, k, v, sga, gc = _pre(x2, mod, norm_mix_g[l].reshape(1, d), w_lat, w_t,
                                q_norm_g[l].reshape(1, -1), wq2,
                                kv_norm_g[l].reshape(1, -1), wk2, wv, pos, freqs, conv_w[l],
                                w_up_conv[l].astype(BF16), seq)
        merged = _attention(q, k, v, sga, gc, w_up_attn[l], batch, seq)

        rw_pad = jnp.concatenate([router_w[l], jnp.zeros((d, LANES - N_EXPERTS), F32)], axis=1)
        rb_pad = jnp.concatenate([router_b[l], jnp.full((LANES - N_EXPERTS,), NEG_BIG, F32)])
        rw_hi = rw_pad.astype(BF16)
        rw_lo = (rw_pad - rw_hi.astype(F32)).astype(BF16)
        x1, h2, idx_pad, gate_pad, rank_pad, counts = _post(
            merged, x2, mod, w_o[l], norm_ffn_g[l].reshape(1, d),
            jnp.concatenate([rw_hi, rw_lo], axis=1), rb_pad.reshape(1, LANES), seq)

        dest, group_table, n_rows = _route(
            idx_pad, rank_pad, counts[0, :N_EXPERTS], t)
        xs = _dispatch(h2, dest, n_rows)
        ys = _moe(group_table, xs, w_gu[l], b_gu[l].reshape(N_EXPERTS, 1, -1),
                  w_down[l], b_down[l].reshape(N_EXPERTS, 1, -1))
        y_kt = _undispatch(ys, dest)
        x2 = _final(x1, y_kt, gate_pad, mod, norm_final_g.reshape(1, d), seq, l == depth - 1)

    return x2.reshape(batch, seq, d)
```
